```python
import jax, jax.numpy as jnp
from jax import lax
import numpy as np

D_MODEL = 1024
BATCH = 8
SEQ = 8192
DEPTH = 4

N_MIXERS = 3
EPS = 1e-6
CHUNK = 128
A_WIDTH = 2 * D_MODEL
A_GROUPS = 8
A_GROUP_DIM = A_WIDTH // A_GROUPS
HEAD_DIM = 128
B_HEADS = D_MODEL // HEAD_DIM
B_PATTERNS = ((128, 1), (512, 4), (2048, 16))
N_B_GROUPS = len(B_PATTERNS)
B_WIDTH = B_HEADS * HEAD_DIM
B_IN_WIDTH = 3 * N_B_GROUPS * B_WIDTH + B_WIDTH
ROPE_DIM = HEAD_DIM // 4
ROPE_THETA = 500000.0
POOL_SIZES = (2, 4, 8, 16)
N_POOL = len(POOL_SIZES)
C_WIDTH = 2 * D_MODEL
C_GROUP = C_WIDTH // N_POOL
N_A = (DEPTH + 2) // 3
N_B = (DEPTH + 1) // 3
N_C = DEPTH // 3

kernel_name = "hybrid_gmlp_dilated_attn_pool_interleaved"


def rms_norm(x, g):
    xf = x.astype(jnp.float32)
    y = xf * lax.rsqrt(jnp.mean(xf * xf, axis=-1, keepdims=True) + EPS)
    return (y * g.astype(jnp.float32)).astype(x.dtype)


def rotary_tables(seq_len):
    half = ROPE_DIM // 2
    inv_freq = jnp.power(jnp.float32(ROPE_THETA), -jnp.arange(half, dtype=jnp.float32) / half)
    ang = jnp.arange(seq_len, dtype=jnp.float32)[:, None] * inv_freq[None, :]
    return jnp.cos(ang)[None, :, None, :], jnp.sin(ang)[None, :, None, :]


def apply_partial_rotary(x, cos, sin):
    half = ROPE_DIM // 2
    x1 = x[..., :half].astype(jnp.float32)
    x2 = x[..., half:ROPE_DIM].astype(jnp.float32)
    rot = jnp.concatenate([x1 * cos - x2 * sin, x2 * cos + x1 * sin], axis=-1)
    return jnp.concatenate([rot.astype(x.dtype), x[..., ROPE_DIM:]], axis=-1)


def dilated_window_attention(q, k, v, span, dilation):
    bsz, S, H, hd = q.shape
    blk = span
    L = S // dilation
    nb = -(-L // blk)
    Lp = nb * blk

    def to_blocks(t):
        t = t.reshape(bsz, L, dilation, H, hd).transpose(0, 2, 1, 3, 4)
        t = jnp.pad(t, ((0, 0), (0, 0), (0, Lp - L), (0, 0), (0, 0)))
        return t.reshape(bsz, dilation, nb, blk, H, hd)

    def with_prev(t):
        prev = jnp.pad(t, ((0, 0), (0, 0), (1, 0), (0, 0), (0, 0), (0, 0)))[:, :, :-1]
        return jnp.concatenate([prev, t], axis=3)

    qb = to_blocks(q).astype(jnp.float32)
    kk = with_prev(to_blocks(k)).astype(jnp.float32)
    vv = with_prev(to_blocks(v)).astype(jnp.float32)
    scores = jnp.einsum('brnqhd,brnkhd->brnhqk', qb, kk) * (1.0 / np.sqrt(hd)).astype(np.float32)
    qi = jnp.arange(blk)[:, None]
    ki = jnp.arange(2 * blk)[None, :]
    dist = blk + qi - ki
    band = (dist >= 0) & (dist <= span)
    has_prev = (jnp.arange(nb) > 0)[:, None, None] | (ki >= blk)[None]
    mask = band[None] & has_prev
    scores = jnp.where(mask[None, None, :, None], scores, -jnp.inf)
    lse = jax.nn.logsumexp(scores, axis=-1)
    p = jnp.exp(scores - lse[..., None])
    o = jnp.einsum('brnhqk,brnkhd->brnqhd', p, vv)
    o = o.reshape(bsz, dilation, Lp, H, hd)[:, :, :L].transpose(0, 2, 1, 3, 4).reshape(bsz, S, H, hd)
    lse = lse.transpose(0, 1, 2, 4, 3).reshape(bsz, dilation, Lp, H)[:, :, :L]
    lse = lse.transpose(0, 2, 1, 3).reshape(bsz, S, H)
    return o, lse


def mixer_a(h, w_in, v_gain, w_s, b_s, w_out):
    bsz, S, _ = h.shape
    proj = h @ w_in
    u = proj[..., :A_WIDTH]
    v = rms_norm(proj[..., A_WIDTH:2 * A_WIDTH], v_gain)
    z = proj[..., 2 * A_WIDTH:]
    nc = S // CHUNK
    v = v.reshape(bsz, nc, CHUNK, A_GROUPS, A_GROUP_DIM)
    causal = jnp.tril(jnp.ones((CHUNK, CHUNK), dtype=bool))
    ws = jnp.where(causal[None], w_s, jnp.zeros_like(w_s))
    mixed = jnp.einsum('gij,bcjgd->bcigd', ws, v) + b_s.T[None, None, :, :, None]
    mixed = mixed.reshape(bsz, S, A_WIDTH)
    y = u * mixed * jax.nn.silu(z)
    return y @ w_out


def mixer_b(h, w_in, q_gain, k_gain, w_out):
    bsz, S, _ = h.shape
    proj = h @ w_in
    n_qkv = 3 * N_B_GROUPS * B_WIDTH
    qkv = proj[..., :n_qkv].reshape(bsz, S, 3, N_B_GROUPS, B_HEADS, HEAD_DIM)
    z = proj[..., n_qkv:]
    cos, sin = rotary_tables(S)
    outs, lses = [], []
    for g, (window, dilation) in enumerate(B_PATTERNS):
        q = apply_partial_rotary(rms_norm(qkv[:, :, 0, g], q_gain[g]), cos, sin)
        k = apply_partial_rotary(rms_norm(qkv[:, :, 1, g], k_gain[g]), cos, sin)
        o, lse = dilated_window_attention(q, k, qkv[:, :, 2, g], window // dilation, dilation)
        outs.append(o)
        lses.append(lse)
    wgt = jax.nn.softmax(jnp.stack(lses), axis=0)
    o = jnp.einsum('gbsh,gbshd->bshd', wgt, jnp.stack(outs))
    y = o.reshape(bsz, S, B_WIDTH).astype(h.dtype) * jax.nn.silu(z)
    return y @ w_out


def causal_mean(x, window):
    S = x.shape[1]
    c = jnp.cumsum(x.astype(jnp.float32), axis=1)
    c_prev = jnp.pad(c, ((0, 0), (window, 0), (0, 0)))[:, :S]
    cnt = jnp.minimum(jnp.arange(S) + 1, window).astype(jnp.float32)
    return ((c - c_prev) / cnt[None, :, None]).astype(x.dtype)


def mixer_c(h, w_in, w_grp, scale, w_out):
    bsz, S, _ = h.shape
    proj = h @ w_in
    xc = proj[..., :C_WIDTH].reshape(bsz, S, N_POOL, C_GROUP)
    z = proj[..., C_WIDTH:]
    pooled = jnp.stack([causal_mean(xc[:, :, g], w) for g, w in enumerate(POOL_SIZES)], axis=2)
    mixed = jnp.einsum('bsgc,gcd->bsgd', pooled - xc, w_grp).reshape(bsz, S, C_WIDTH) * scale
    y = mixed * jax.nn.silu(z)
    return y @ w_out


def _fwd_setup_inputs(seed: int = 0) -> dict:
    key = jax.random.key(seed)
    ks = jax.random.split(key, 16)
    f32 = jnp.float32

    def nrm(k, shape, fan_in):
        return jax.random.normal(k, shape, f32) * (fan_in ** -0.5)

    def gain(k, shape):
        return 1.0 + 0.1 * jax.random.normal(k, shape, f32)

    return {
        "x": jax.random.normal(ks[0], (BATCH, SEQ, D_MODEL), f32),
        "norm_gain": gain(ks[1], (DEPTH, D_MODEL)),
        "a_w_in": nrm(ks[2], (N_A, D_MODEL, 3 * A_WIDTH), D_MODEL),
        "a_v_gain": gain(ks[3], (N_A, A_WIDTH)),
        "a_w_s": nrm(ks[4], (N_A, A_GROUPS, CHUNK, CHUNK), CHUNK),
        "a_b_s": gain(ks[5], (N_A, A_GROUPS, CHUNK)),
        "a_w_out": nrm(ks[6], (N_A, A_WIDTH, D_MODEL), A_WIDTH),
        "b_w_in": nrm(ks[7], (N_B, D_MODEL, B_IN_WIDTH), D_MODEL),
        "b_q_gain": gain(ks[8], (N_B, N_B_GROUPS, HEAD_DIM)),
        "b_k_gain": gain(ks[9], (N_B, N_B_GROUPS, HEAD_DIM)),
        "b_w_out": nrm(ks[10], (N_B, B_WIDTH, D_MODEL), B_WIDTH),
        "c_w_in": nrm(ks[11], (N_C, D_MODEL, 2 * C_WIDTH), D_MODEL),
        "c_w_grp": nrm(ks[12], (N_C, N_POOL, C_GROUP, C_GROUP), C_GROUP),
        "c_scale": gain(ks[13], (N_C, C_WIDTH)),
        "c_w_out": nrm(ks[14], (N_C, C_WIDTH, D_MODEL), C_WIDTH),
    }


def _fwd_reference(x, norm_gain, a_w_in, a_v_gain, a_w_s, a_b_s, a_w_out,
              b_w_in, b_q_gain, b_k_gain, b_w_out,
              c_w_in, c_w_grp, c_scale, c_w_out):
    for i in range(DEPTH):
        kind, j = i % N_MIXERS, i // N_MIXERS
        h = rms_norm(x, norm_gain[i])
        if kind == 0:
            y = mixer_a(h, a_w_in[j], a_v_gain[j], a_w_s[j], a_b_s[j], a_w_out[j])
        elif kind == 1:
            y = mixer_b(h, b_w_in[j], b_q_gain[j], b_k_gain[j], b_w_out[j])
        else:
            y = mixer_c(h, c_w_in[j], c_w_grp[j], c_scale[j], c_w_out[j])
        x = x + y.astype(x.dtype)
    return x


import jax as _jax
import jax.numpy as _jnp

TWIN_FORMAT = 'train_step'
FWD_PARAMS = ['x', 'norm_gain', 'a_w_in', 'a_v_gain', 'a_w_s', 'a_b_s', 'a_w_out', 'b_w_in', 'b_q_gain', 'b_k_gain', 'b_w_out', 'c_w_in', 'c_w_grp', 'c_scale', 'c_w_out']
TWIN_WEIGHTS = ['norm_gain', 'a_w_in', 'a_v_gain', 'a_w_s', 'a_b_s', 'a_w_out', 'b_w_in', 'b_q_gain', 'b_k_gain', 'b_w_out', 'c_w_in', 'c_w_grp', 'c_scale', 'c_w_out']
TWIN_DIFF_INPUT = 'x'
TWIN_INPUTS = ['x', 'norm_gain', 'a_w_in', 'a_v_gain', 'a_w_s', 'a_b_s', 'a_w_out', 'b_w_in', 'b_q_gain', 'b_k_gain', 'b_w_out', 'c_w_in', 'c_w_grp', 'c_scale', 'c_w_out', 'loss_target', 'm_norm_gain', 'm_a_w_in', 'm_a_v_gain', 'm_a_w_s', 'm_a_b_s', 'm_a_w_out', 'm_b_w_in', 'm_b_q_gain', 'm_b_k_gain', 'm_b_w_out', 'm_c_w_in', 'm_c_w_grp', 'm_c_scale', 'm_c_w_out', 'v_norm_gain', 'v_a_w_in', 'v_a_v_gain', 'v_a_w_s', 'v_a_b_s', 'v_a_w_out', 'v_b_w_in', 'v_b_q_gain', 'v_b_k_gain', 'v_b_w_out', 'v_c_w_in', 'v_c_w_grp', 'v_c_scale', 'v_c_w_out']
TWIN_OUTPUTS = ['loss', 'grad_x', 'grad_norm_gain', 'grad_a_w_in', 'grad_a_v_gain', 'grad_a_w_s', 'grad_a_b_s', 'grad_a_w_out', 'grad_b_w_in', 'grad_b_q_gain', 'grad_b_k_gain', 'grad_b_w_out', 'grad_c_w_in', 'grad_c_w_grp', 'grad_c_scale', 'grad_c_w_out', 'delta_norm_gain', 'delta_a_w_in', 'delta_a_v_gain', 'delta_a_w_s', 'delta_a_b_s', 'delta_a_w_out', 'delta_b_w_in', 'delta_b_q_gain', 'delta_b_k_gain', 'delta_b_w_out', 'delta_c_w_in', 'delta_c_w_grp', 'delta_c_scale', 'delta_c_w_out', 'new_m_norm_gain', 'new_m_a_w_in', 'new_m_a_v_gain', 'new_m_a_w_s', 'new_m_a_b_s', 'new_m_a_w_out', 'new_m_b_w_in', 'new_m_b_q_gain', 'new_m_b_k_gain', 'new_m_b_w_out', 'new_m_c_w_in', 'new_m_c_w_grp', 'new_m_c_scale', 'new_m_c_w_out', 'new_v_norm_gain', 'new_v_a_w_in', 'new_v_a_v_gain', 'new_v_a_w_s', 'new_v_a_b_s', 'new_v_a_w_out', 'new_v_b_w_in', 'new_v_b_q_gain', 'new_v_b_k_gain', 'new_v_b_w_out', 'new_v_c_w_in', 'new_v_c_w_grp', 'new_v_c_scale', 'new_v_c_w_out']
TWIN_LEAF_KINDS = {'loss': 'loss', 'grad_x': 'grad_x', 'grad_norm_gain': 'grad_w', 'grad_a_w_in': 'grad_w', 'grad_a_v_gain': 'grad_w', 'grad_a_w_s': 'grad_w', 'grad_a_b_s': 'grad_w', 'grad_a_w_out': 'grad_w', 'grad_b_w_in': 'grad_w', 'grad_b_q_gain': 'grad_w', 'grad_b_k_gain': 'grad_w', 'grad_b_w_out': 'grad_w', 'grad_c_w_in': 'grad_w', 'grad_c_w_grp': 'grad_w', 'grad_c_scale': 'grad_w', 'grad_c_w_out': 'grad_w', 'delta_norm_gain': 'delta_w', 'delta_a_w_in': 'delta_w', 'delta_a_v_gain': 'delta_w', 'delta_a_w_s': 'delta_w', 'delta_a_b_s': 'delta_w', 'delta_a_w_out': 'delta_w', 'delta_b_w_in': 'delta_w', 'delta_b_q_gain': 'delta_w', 'delta_b_k_gain': 'delta_w', 'delta_b_w_out': 'delta_w', 'delta_c_w_in': 'delta_w', 'delta_c_w_grp': 'delta_w', 'delta_c_scale': 'delta_w', 'delta_c_w_out': 'delta_w', 'new_m_norm_gain': 'new_m', 'new_m_a_w_in': 'new_m', 'new_m_a_v_gain': 'new_m', 'new_m_a_w_s': 'new_m', 'new_m_a_b_s': 'new_m', 'new_m_a_w_out': 'new_m', 'new_m_b_w_in': 'new_m', 'new_m_b_q_gain': 'new_m', 'new_m_b_k_gain': 'new_m', 'new_m_b_w_out': 'new_m', 'new_m_c_w_in': 'new_m', 'new_m_c_w_grp': 'new_m', 'new_m_c_scale': 'new_m', 'new_m_c_w_out': 'new_m', 'new_v_norm_gain': 'new_v', 'new_v_a_w_in': 'new_v', 'new_v_a_v_gain': 'new_v', 'new_v_a_w_s': 'new_v', 'new_v_a_b_s': 'new_v', 'new_v_a_w_out': 'new_v', 'new_v_b_w_in': 'new_v', 'new_v_b_q_gain': 'new_v', 'new_v_b_k_gain': 'new_v', 'new_v_b_w_out': 'new_v', 'new_v_c_w_in': 'new_v', 'new_v_c_w_grp': 'new_v', 'new_v_c_scale': 'new_v', 'new_v_c_w_out': 'new_v'}


def _forward(args):
    return _fwd_reference(*[args[k] for k in FWD_PARAMS])


def _output_shape():
    def fwd():
        inp = _fwd_setup_inputs(0)
        return _fwd_reference(*[inp[k] for k in FWD_PARAMS])
    out = _jax.eval_shape(fwd)
    return out.shape, out.dtype

N_MICROBATCH = 1
ADAM_LR = 0.001
ADAM_B1 = 0.9
ADAM_B2 = 0.999
ADAM_EPS = 1e-08
ADAM_WD = 0.01
ADAM_STEP = 10
PER_EXAMPLE_BATCH_AXIS = {'x': 0, 'loss_target': 0}
SHARED_INPUTS = []
_WEIGHT_DTYPES = {'norm_gain': _jnp.float32, 'a_w_in': _jnp.float32, 'a_v_gain': _jnp.float32, 'a_w_s': _jnp.float32, 'a_b_s': _jnp.float32, 'a_w_out': _jnp.float32, 'b_w_in': _jnp.float32, 'b_q_gain': _jnp.float32, 'b_k_gain': _jnp.float32, 'b_w_out': _jnp.float32, 'c_w_in': _jnp.float32, 'c_w_grp': _jnp.float32, 'c_scale': _jnp.float32, 'c_w_out': _jnp.float32}
MOMENT_SCALE = {'norm_gain': 5.688030e+01, 'a_w_in': 5.344194e-01, 'a_v_gain': 5.570439e+00, 'a_w_s': 1.035065e+00, 'a_b_s': 2.290703e+01, 'a_w_out': 7.883023e-01, 'b_w_in': 6.191145e-02, 'b_q_gain': 3.596689e-01, 'b_k_gain': 3.614040e-01, 'b_w_out': 1.008057e-01, 'c_w_in': 4.216058e-01, 'c_w_grp': 4.864654e-01, 'c_scale': 9.377559e+00, 'c_w_out': 5.664173e-01}


def _to_microbatches(a, axis):
    t = _jnp.moveaxis(a, axis, 0)
    t = t.reshape((N_MICROBATCH, t.shape[0] // N_MICROBATCH) + t.shape[1:])
    return _jnp.moveaxis(t, 1, axis + 1)


def setup_inputs(seed: int = 0) -> dict:
    inp = _fwd_setup_inputs(seed)
    key = _jax.random.fold_in(_jax.random.key(seed), 7919)
    shape, _ = _output_shape()
    out = dict(inp)
    out["loss_target"] = _jax.random.normal(_jax.random.fold_in(key, 0), shape, _jnp.float32)
    for i, name in enumerate(TWIN_WEIGHTS):
        w = inp[name].astype(_jnp.float32)
        if MOMENT_SCALE is None:
            s = _jnp.sqrt(_jnp.mean(_jnp.square(w)) + 1e-30)
        else:
            s = MOMENT_SCALE[name]
        km, kv = _jax.random.split(_jax.random.fold_in(key, i + 1))
        out[name] = w
        out["m_" + name] = s * _jax.random.normal(km, w.shape, _jnp.float32)
        out["v_" + name] = (s * s) * _jax.random.uniform(kv, w.shape, _jnp.float32, 0.5, 1.5)
    if N_MICROBATCH > 1:
        for name, axis in PER_EXAMPLE_BATCH_AXIS.items():
            out[name] = _to_microbatches(out[name], axis)
    return {'x': out['x'], 'norm_gain': out['norm_gain'], 'a_w_in': out['a_w_in'], 'a_v_gain': out['a_v_gain'], 'a_w_s': out['a_w_s'], 'a_b_s': out['a_b_s'], 'a_w_out': out['a_w_out'], 'b_w_in': out['b_w_in'], 'b_q_gain': out['b_q_gain'], 'b_k_gain': out['b_k_gain'], 'b_w_out': out['b_w_out'], 'c_w_in': out['c_w_in'], 'c_w_grp': out['c_w_grp'], 'c_scale': out['c_scale'], 'c_w_out': out['c_w_out'], 'loss_target': out['loss_target'], 'm_norm_gain': out['m_norm_gain'], 'm_a_w_in': out['m_a_w_in'], 'm_a_v_gain': out['m_a_v_gain'], 'm_a_w_s': out['m_a_w_s'], 'm_a_b_s': out['m_a_b_s'], 'm_a_w_out': out['m_a_w_out'], 'm_b_w_in': out['m_b_w_in'], 'm_b_q_gain': out['m_b_q_gain'], 'm_b_k_gain': out['m_b_k_gain'], 'm_b_w_out': out['m_b_w_out'], 'm_c_w_in': out['m_c_w_in'], 'm_c_w_grp': out['m_c_w_grp'], 'm_c_scale': out['m_c_scale'], 'm_c_w_out': out['m_c_w_out'], 'v_norm_gain': out['v_norm_gain'], 'v_a_w_in': out['v_a_w_in'], 'v_a_v_gain': out['v_a_v_gain'], 'v_a_w_s': out['v_a_w_s'], 'v_a_b_s': out['v_a_b_s'], 'v_a_w_out': out['v_a_w_out'], 'v_b_w_in': out['v_b_w_in'], 'v_b_q_gain': out['v_b_q_gain'], 'v_b_k_gain': out['v_b_k_gain'], 'v_b_w_out': out['v_b_w_out'], 'v_c_w_in': out['v_c_w_in'], 'v_c_w_grp': out['v_c_w_grp'], 'v_c_scale': out['v_c_scale'], 'v_c_w_out': out['v_c_w_out']}


def _loss(weights, diff, rest, loss_target):
    with _jax.named_scope("forward"):
        args = {**rest, TWIN_DIFF_INPUT: diff, **{k: w.astype(_WEIGHT_DTYPES[k]) for k, w in weights.items()}}
        y = _forward(args)
    with _jax.named_scope("loss_head"):
        err = _jnp.square(y.astype(_jnp.float32) - loss_target)
        return 0.5 * _jnp.sum(_jnp.mean(err, axis=-1)) if err.ndim else 0.5 * err


def _adamw(w, g, m, v):
    m = ADAM_B1 * m + (1.0 - ADAM_B1) * g
    v = ADAM_B2 * v + (1.0 - ADAM_B2) * _jnp.square(g)
    m_hat = m / (1.0 - ADAM_B1 ** ADAM_STEP)
    v_hat = v / (1.0 - ADAM_B2 ** ADAM_STEP)
    delta = -ADAM_LR * (m_hat / (_jnp.sqrt(v_hat) + ADAM_EPS) + ADAM_WD * w)
    return delta, m, v


def reference(x, norm_gain, a_w_in, a_v_gain, a_w_s, a_b_s, a_w_out, b_w_in, b_q_gain, b_k_gain, b_w_out, c_w_in, c_w_grp, c_scale, c_w_out, loss_target, m_norm_gain, m_a_w_in, m_a_v_gain, m_a_w_s, m_a_b_s, m_a_w_out, m_b_w_in, m_b_q_gain, m_b_k_gain, m_b_w_out, m_c_w_in, m_c_w_grp, m_c_scale, m_c_w_out, v_norm_gain, v_a_w_in, v_a_v_gain, v_a_w_s, v_a_b_s, v_a_w_out, v_b_w_in, v_b_q_gain, v_b_k_gain, v_b_w_out, v_c_w_in, v_c_w_grp, v_c_scale, v_c_w_out):
    given = dict(x=x, norm_gain=norm_gain, a_w_in=a_w_in, a_v_gain=a_v_gain, a_w_s=a_w_s, a_b_s=a_b_s, a_w_out=a_w_out, b_w_in=b_w_in, b_q_gain=b_q_gain, b_k_gain=b_k_gain, b_w_out=b_w_out, c_w_in=c_w_in, c_w_grp=c_w_grp, c_scale=c_scale, c_w_out=c_w_out, loss_target=loss_target, m_norm_gain=m_norm_gain, m_a_w_in=m_a_w_in, m_a_v_gain=m_a_v_gain, m_a_w_s=m_a_w_s, m_a_b_s=m_a_b_s, m_a_w_out=m_a_w_out, m_b_w_in=m_b_w_in, m_b_q_gain=m_b_q_gain, m_b_k_gain=m_b_k_gain, m_b_w_out=m_b_w_out, m_c_w_in=m_c_w_in, m_c_w_grp=m_c_w_grp, m_c_scale=m_c_scale, m_c_w_out=m_c_w_out, v_norm_gain=v_norm_gain, v_a_w_in=v_a_w_in, v_a_v_gain=v_a_v_gain, v_a_w_s=v_a_w_s, v_a_b_s=v_a_b_s, v_a_w_out=v_a_w_out, v_b_w_in=v_b_w_in, v_b_q_gain=v_b_q_gain, v_b_k_gain=v_b_k_gain, v_b_w_out=v_b_w_out, v_c_w_in=v_c_w_in, v_c_w_grp=v_c_w_grp, v_c_scale=v_c_scale, v_c_w_out=v_c_w_out)
    weights = {n: given[n] for n in TWIN_WEIGHTS}
    shared = {n: given[n] for n in SHARED_INPUTS}
    per_example = {n: given[n] for n in ['x']}
    grad_fn = _jax.value_and_grad(_loss, argnums=(0, 1))

    def one_microbatch(ex, loss_target):
        ex = dict(ex)
        diff = ex.pop(TWIN_DIFF_INPUT)
        return grad_fn(weights, diff, {**shared, **ex}, loss_target)

    if N_MICROBATCH == 1:
        loss, (grad_w, grad_x) = one_microbatch(per_example, given["loss_target"])
    else:
        def body(carry, xs):
            loss_sum, grad_sum = carry
            l_k, (gw_k, gx_k) = one_microbatch(xs[0], xs[1])
            with _jax.named_scope("update"):
                return (loss_sum + l_k, _jax.tree.map(_jnp.add, grad_sum, gw_k)), gx_k

        init = (_jnp.zeros((), _jnp.float32), _jax.tree.map(_jnp.zeros_like, weights))
        (loss, grad_w), grad_x = _jax.lax.scan(body, init, (per_example, given["loss_target"]))
    with _jax.named_scope("update"):
        delta_w, new_m, new_v = {}, {}, {}
        for n in TWIN_WEIGHTS:
            delta_w[n], new_m[n], new_v[n] = _adamw(weights[n], grad_w[n], given["m_" + n], given["v_" + n])
    return (loss, grad_x, *[grad_w[n] for n in TWIN_WEIGHTS], *[delta_w[n] for n in TWIN_WEIGHTS],
            *[new_m[n] for n in TWIN_WEIGHTS], *[new_v[n] for n in TWIN_WEIGHTS])
```

```python
import functools
import math

import jax
import jax.numpy as jnp
from jax import lax
from jax.experimental import pallas as pl
from jax.experimental.pallas import tpu as pltpu

F32 = jnp.float32
CDT = jnp.bfloat16

D_MODEL = 1024
EPS = 1e-6
CHUNK = 128
A_WIDTH = 2048
A_GROUPS = 8
A_GROUP_DIM = 256
HEAD_DIM = 128
B_HEADS = 8
B_DILATIONS = (1, 4, 16)
B_QK = 6144
B_IN = 10240
ROPE_HALF = 16
ROPE_THETA = 500000.0
POOL_SIZES = (2, 4, 8, 16)
POOL_HALO = 16
C_WIDTH = 2048
C_GROUP = 512
N_CHIPS = 4

ADAM_LR = 0.001
ADAM_B1 = 0.9
ADAM_B2 = 0.999
ADAM_EPS = 1e-08
ADAM_WD = 0.01
ADAM_STEP = 10

VMEM_LIMIT = 48 * 1024 * 1024
ANY = pl.BlockSpec(memory_space=pl.ANY)
MESH = pl.DeviceIdType.MESH

NN = (((1,), (0,)), ((), ()))
NT = (((1,), (1,)), ((), ()))
TN = (((0,), (0,)), ((), ()))


def _cparams(*sem):
    return pltpu.CompilerParams(dimension_semantics=sem, vmem_limit_bytes=VMEM_LIMIT)


def _dot(a, b, dims=NN):
    return lax.dot_general(a, b, dims, preferred_element_type=F32)


def _sigmoid(z):
    return 1.0 / (1.0 + jnp.exp(-z))


def _mm(a, b, mode, out_dtype, name, tm=1024, tn=1024, tk=1024, add=None):
    if mode == "nn":
        (m, k), n = a.shape, b.shape[1]
    elif mode == "nt":
        (m, k), n = a.shape, b.shape[0]
    else:
        (k, m), n = a.shape, b.shape[1]
    tm, tn, tk = min(tm, m), min(tn, n), min(tk, k)
    nk = k // tk
    a_spec = {"nn": pl.BlockSpec((tm, tk), lambda i, j, q: (i, q)),
              "nt": pl.BlockSpec((tm, tk), lambda i, j, q: (i, q)),
              "tn": pl.BlockSpec((tk, tm), lambda i, j, q: (q, i))}[mode]
    b_spec = {"nn": pl.BlockSpec((tk, tn), lambda i, j, q: (q, j)),
              "nt": pl.BlockSpec((tn, tk), lambda i, j, q: (j, q)),
              "tn": pl.BlockSpec((tk, tn), lambda i, j, q: (q, j))}[mode]
    dims = {"nn": NN, "nt": NT, "tn": TN}[mode]
    has_add = add is not None

    def body(*refs):
        a_ref, b_ref = refs[0], refs[1]
        o_ref = refs[3] if has_add else refs[2]
        p = _dot(a_ref[...], b_ref[...], dims)

        def finish(v):
            if has_add:
                v = v + refs[2][...]
            o_ref[...] = v.astype(o_ref.dtype)

        if nk == 1:
            finish(p)
        else:
            acc_ref = refs[-1]
            q = pl.program_id(2)

            @pl.when(q == 0)
            def _():
                acc_ref[...] = p

            @pl.when(q > 0)
            def _():
                acc_ref[...] += p

            @pl.when(q == nk - 1)
            def _():
                finish(acc_ref[...])

    in_specs = [a_spec, b_spec]
    args = [a, b]
    if has_add:
        in_specs.append(pl.BlockSpec((tm, tn), lambda i, j, q: (i, j)))
        args.append(add)
    return pl.pallas_call(
        body, name=name, grid=(m // tm, n // tn, nk), in_specs=in_specs,
        out_specs=pl.BlockSpec((tm, tn), lambda i, j, q: (i, j)),
        out_shape=jax.ShapeDtypeStruct((m, n), out_dtype),
        scratch_shapes=[pltpu.VMEM((tm, tn), F32)] if nk > 1 else [],
        compiler_params=_cparams("parallel", "parallel", "arbitrary"),
    )(*args)


def _rms_fwd(x, g, name, tq=512):
    s, d = x.shape

    def body(x_ref, g_ref, h_ref):
        xv = x_ref[...]
        r = lax.rsqrt(jnp.mean(xv * xv, axis=-1, keepdims=True) + EPS)
        h_ref[...] = (xv * r * g_ref[...]).astype(h_ref.dtype)

    return pl.pallas_call(
        body, name=name, grid=(s // tq,),
        in_specs=[pl.BlockSpec((tq, d), lambda i: (i, 0)), pl.BlockSpec((1, d), lambda i: (0, 0))],
        out_specs=pl.BlockSpec((tq, d), lambda i: (i, 0)),
        out_shape=jax.ShapeDtypeStruct((s, d), CDT), compiler_params=_cparams("parallel"),
    )(x, g)


def _rms_bwd(x, dh, dxo, g, name, tq=512):
    s, d = x.shape

    def body(x_ref, dh_ref, dxo_ref, g_ref, dx_ref, dxc_ref, dg_ref):
        xv = x_ref[...]
        dhv = dh_ref[...].astype(F32)
        r = lax.rsqrt(jnp.mean(xv * xv, axis=-1, keepdims=True) + EPS)
        gd = dhv * g_ref[...]
        dx = dxo_ref[...] + r * gd - xv * (r * r * r) * jnp.mean(gd * xv, axis=-1, keepdims=True)
        dx_ref[...] = dx
        dxc_ref[...] = dx.astype(dxc_ref.dtype)

        @pl.when(pl.program_id(0) == 0)
        def _():
            dg_ref[...] = jnp.zeros_like(dg_ref)

        dg_ref[...] += jnp.sum(dhv * xv * r, axis=0, keepdims=True)

    blk = pl.BlockSpec((tq, d), lambda i: (i, 0))
    vec = pl.BlockSpec((1, d), lambda i: (0, 0))
    return pl.pallas_call(
        body, name=name, grid=(s // tq,), in_specs=[blk, blk, blk, vec], out_specs=[blk, blk, vec],
        out_shape=[jax.ShapeDtypeStruct((s, d), F32), jax.ShapeDtypeStruct((s, d), CDT), jax.ShapeDtypeStruct((1, d), F32)],
        compiler_params=_cparams("arbitrary"),
    )(x, dh, dxo, g)


def _loss_bwd(y, target, name, tq=512):
    s, d = y.shape

    def body(y_ref, t_ref, dx_ref, dxc_ref, sq_ref):
        err = y_ref[...] - t_ref[...]
        dx = err * (1.0 / d)
        dx_ref[...] = dx
        dxc_ref[...] = dx.astype(dxc_ref.dtype)

        @pl.when(pl.program_id(0) == 0)
        def _():
            sq_ref[...] = jnp.zeros_like(sq_ref)

        sq_ref[...] += jnp.sum(err * err, axis=0, keepdims=True)

    blk = pl.BlockSpec((tq, d), lambda i: (i, 0))
    vec = pl.BlockSpec((1, d), lambda i: (0, 0))
    return pl.pallas_call(
        body, name=name, grid=(s // tq,), in_specs=[blk, blk], out_specs=[blk, blk, vec],
        out_shape=[jax.ShapeDtypeStruct((s, d), F32), jax.ShapeDtypeStruct((s, d), CDT), jax.ShapeDtypeStruct((1, d), F32)],
        compiler_params=_cparams("arbitrary"),
    )(y, target)


def _tril_mask():
    row = lax.broadcasted_iota(jnp.int32, (CHUNK, CHUNK), 0)
    col = lax.broadcasted_iota(jnp.int32, (CHUNK, CHUNK), 1)
    return row >= col


def _a_mid_fwd(proj, v_gain, w_s, b_s_t, name, tq=256):
    s = proj.shape[0]

    def body(p_ref, vg_ref, ws_ref, bs_ref, y_ref):
        vraw = p_ref[:, A_WIDTH:2 * A_WIDTH].astype(F32)
        r = lax.rsqrt(jnp.mean(vraw * vraw, axis=-1, keepdims=True) + EPS)
        vn = (vraw * r * vg_ref[...]).astype(CDT)
        tri = _tril_mask()
        for g in range(A_GROUPS):
            w = jnp.where(tri, ws_ref[g], 0.0).astype(CDT)
            bias = bs_ref[:, g:g + 1]
            cols = slice(g * A_GROUP_DIM, (g + 1) * A_GROUP_DIM)
            zcols = slice(2 * A_WIDTH + g * A_GROUP_DIM, 2 * A_WIDTH + (g + 1) * A_GROUP_DIM)
            for c in range(tq // CHUNK):
                rows = slice(c * CHUNK, (c + 1) * CHUNK)
                mixed = _dot(w, vn[rows, cols]) + bias
                u = p_ref[rows, cols].astype(F32)
                z = p_ref[rows, zcols].astype(F32)
                y_ref[rows, cols] = (u * mixed * (z * _sigmoid(z))).astype(y_ref.dtype)

    return pl.pallas_call(
        body, name=name, grid=(s // tq,),
        in_specs=[pl.BlockSpec((tq, 3 * A_WIDTH), lambda i: (i, 0)), pl.BlockSpec((1, A_WIDTH), lambda i: (0, 0)),
                  pl.BlockSpec((A_GROUPS, CHUNK, CHUNK), lambda i: (0, 0, 0)), pl.BlockSpec((CHUNK, A_GROUPS), lambda i: (0, 0))],
        out_specs=pl.BlockSpec((tq, A_WIDTH), lambda i: (i, 0)),
        out_shape=jax.ShapeDtypeStruct((s, A_WIDTH), CDT), compiler_params=_cparams("parallel"),
    )(proj, v_gain, w_s, b_s_t)


def _a_mid_bwd(proj, dy, v_gain, w_s, b_s_t, name, tq=256):
    s = proj.shape[0]

    def body(p_ref, dy_ref, vg_ref, ws_ref, bs_ref, dp_ref, dws_ref, dbs_ref, dvg_ref, dvn_ref):
        @pl.when(pl.program_id(0) == 0)
        def _():
            dws_ref[...] = jnp.zeros_like(dws_ref)
            dbs_ref[...] = jnp.zeros_like(dbs_ref)
            dvg_ref[...] = jnp.zeros_like(dvg_ref)

        vraw = p_ref[:, A_WIDTH:2 * A_WIDTH].astype(F32)
        r = lax.rsqrt(jnp.mean(vraw * vraw, axis=-1, keepdims=True) + EPS)
        vhat = vraw * r
        vg = vg_ref[...]
        vn = (vhat * vg).astype(CDT)
        tri = _tril_mask()
        lane = lax.broadcasted_iota(jnp.int32, (CHUNK, A_GROUPS), 1)
        dbs = jnp.zeros((CHUNK, A_GROUPS), F32)
        for g in range(A_GROUPS):
            w = jnp.where(tri, ws_ref[g], 0.0).astype(CDT)
            bias = bs_ref[:, g:g + 1]
            cols = slice(g * A_GROUP_DIM, (g + 1) * A_GROUP_DIM)
            zcols = slice(2 * A_WIDTH + g * A_GROUP_DIM, 2 * A_WIDTH + (g + 1) * A_GROUP_DIM)
            dws = jnp.zeros((CHUNK, CHUNK), F32)
            for c in range(tq // CHUNK):
                rows = slice(c * CHUNK, (c + 1) * CHUNK)
                vn_g = vn[rows, cols]
                mixed = _dot(w, vn_g) + bias
                u = p_ref[rows, cols].astype(F32)
                z = p_ref[rows, zcols].astype(F32)
                dyv = dy_ref[rows, cols].astype(F32)
                sg = _sigmoid(z)
                sz = z * sg
                dyu = dyv * u
                dmixed = dyu * sz
                dp_ref[rows, cols] = (dyv * mixed * sz).astype(dp_ref.dtype)
                dp_ref[rows, zcols] = (dyu * mixed * (sg * (1.0 + z * (1.0 - sg)))).astype(dp_ref.dtype)
                dmc = dmixed.astype(CDT)
                dws = dws + _dot(dmc, vn_g, NT)
                dbs = dbs + jnp.where(lane == g, jnp.sum(dmixed, axis=-1, keepdims=True), 0.0)
                dvn_ref[rows, cols] = _dot(w, dmc, TN)
            dws_ref[g] += jnp.where(tri, dws, 0.0)
        dbs_ref[...] += dbs
        dvn = dvn_ref[...]
        gd = dvn * vg
        dvraw = r * gd - vraw * (r * r * r) * jnp.mean(gd * vraw, axis=-1, keepdims=True)
        dp_ref[:, A_WIDTH:2 * A_WIDTH] = dvraw.astype(dp_ref.dtype)
        dvg_ref[...] += jnp.sum(dvn * vhat, axis=0, keepdims=True)

    return pl.pallas_call(
        body, name=name, grid=(s // tq,),
        in_specs=[pl.BlockSpec((tq, 3 * A_WIDTH), lambda i: (i, 0)), pl.BlockSpec((tq, A_WIDTH), lambda i: (i, 0)),
                  pl.BlockSpec((1, A_WIDTH), lambda i: (0, 0)), pl.BlockSpec((A_GROUPS, CHUNK, CHUNK), lambda i: (0, 0, 0)),
                  pl.BlockSpec((CHUNK, A_GROUPS), lambda i: (0, 0))],
        out_specs=[pl.BlockSpec((tq, 3 * A_WIDTH), lambda i: (i, 0)), pl.BlockSpec((A_GROUPS, CHUNK, CHUNK), lambda i: (0, 0, 0)),
                   pl.BlockSpec((CHUNK, A_GROUPS), lambda i: (0, 0)), pl.BlockSpec((1, A_WIDTH), lambda i: (0, 0))],
        out_shape=[jax.ShapeDtypeStruct((s, 3 * A_WIDTH), CDT), jax.ShapeDtypeStruct((A_GROUPS, CHUNK, CHUNK), F32),
                   jax.ShapeDtypeStruct((CHUNK, A_GROUPS), F32), jax.ShapeDtypeStruct((1, A_WIDTH), F32)],
        scratch_shapes=[pltpu.VMEM((tq, A_WIDTH), F32)],
        compiler_params=_cparams("arbitrary"),
    )(proj, dy, v_gain, w_s, b_s_t)


def _rope_tables(s):
    inv_freq = jnp.power(jnp.float32(ROPE_THETA), -jnp.arange(ROPE_HALF, dtype=F32) / ROPE_HALF)
    ang = jnp.arange(s, dtype=F32)[:, None] * inv_freq[None, :]
    cos, sin = jnp.cos(ang), jnp.sin(ang)
    rest = HEAD_DIM - 2 * ROPE_HALF
    t_c = jnp.concatenate([cos, cos, jnp.ones((s, rest), F32)], axis=1)
    t_a = jnp.concatenate([-sin, jnp.zeros((s, HEAD_DIM - ROPE_HALF), F32)], axis=1)
    t_b = jnp.concatenate([jnp.zeros((s, ROPE_HALF), F32), sin, jnp.zeros((s, rest), F32)], axis=1)
    return t_c, t_a, t_b


def _b_qk_fwd(proj, gains, tabs, name, tq=256):
    s = proj.shape[0]

    def body(p_ref, g_ref, tc_ref, ta_ref, tb_ref, o_ref):
        tc, ta, tb = tc_ref[...], ta_ref[...], tb_ref[...]
        for tg in range(6):
            gain = g_ref[tg:tg + 1, :]
            for h in range(B_HEADS):
                cols = slice(tg * 1024 + h * HEAD_DIM, tg * 1024 + (h + 1) * HEAD_DIM)
                xv = p_ref[:, cols].astype(F32)
                r = lax.rsqrt(jnp.mean(xv * xv, axis=-1, keepdims=True) + EPS)
                xn = xv * r * gain
                y = xn * tc + pltpu.roll(xn, HEAD_DIM - ROPE_HALF, 1) * ta + pltpu.roll(xn, ROPE_HALF, 1) * tb
                o_ref[:, cols] = y.astype(o_ref.dtype)

    tab = pl.BlockSpec((tq, HEAD_DIM), lambda i: (i, 0))
    return pl.pallas_call(
        body, name=name, grid=(s // tq,),
        in_specs=[pl.BlockSpec((tq, B_QK), lambda i: (i, 0)), pl.BlockSpec((6, HEAD_DIM), lambda i: (0, 0)), tab, tab, tab],
        out_specs=pl.BlockSpec((tq, B_QK), lambda i: (i, 0)),
        out_shape=jax.ShapeDtypeStruct((s, B_QK), CDT), compiler_params=_cparams("parallel"),
    )(proj, gains, *tabs)


def _b_attn_fwd(qk, proj, g, dil, name):
    s = qk.shape[0]
    length = s // dil
    nb = length // CHUNK
    scale = 1.0 / math.sqrt(HEAD_DIM)
    w = B_HEADS * HEAD_DIM

    def body(q_ref, kc_ref, kp_ref, vc_ref, vp_ref, o_ref, lse_ref):
        n = pl.program_id(1)
        qi = lax.broadcasted_iota(jnp.int32, (CHUNK, 2 * CHUNK), 0)
        ki = lax.broadcasted_iota(jnp.int32, (CHUNK, 2 * CHUNK), 1)
        first_key = jnp.where(n > 0, 0, CHUNK)
        mask = (ki >= qi) & (ki <= qi + CHUNK) & (ki >= first_key)
        lane = lax.broadcasted_iota(jnp.int32, (CHUNK, HEAD_DIM), 1)
        lse_all = jnp.zeros((CHUNK, HEAD_DIM), F32)
        for h in range(B_HEADS):
            sl = slice(h * HEAD_DIM, (h + 1) * HEAD_DIM)
            k2 = jnp.concatenate([kp_ref[:, sl], kc_ref[:, sl]], axis=0)
            v2 = jnp.concatenate([vp_ref[:, sl], vc_ref[:, sl]], axis=0)
            sc = jnp.where(mask, _dot(q_ref[:, sl], k2, NT) * scale, -1e30)
            m = jnp.max(sc, axis=-1, keepdims=True)
            p = jnp.exp(sc - m)
            l = jnp.sum(p, axis=-1, keepdims=True)
            o_ref[:, sl] = _dot(p.astype(CDT), v2) / l
            lse_all = jnp.where(lane == h, m + jnp.log(l), lse_all)
        lse_ref[...] = lse_all

    prev = lambda n: jnp.maximum(n - 1, 0)
    blk = lambda f: pl.BlockSpec((CHUNK, w), f)
    o, lse = pl.pallas_call(
        body, name=name, grid=(dil, nb),
        in_specs=[blk(lambda r, n: (n, r * 6 + g)), blk(lambda r, n: (n, r * 6 + 3 + g)), blk(lambda r, n: (prev(n), r * 6 + 3 + g)),
                  blk(lambda r, n: (n, r * 10 + 6 + g)), blk(lambda r, n: (prev(n), r * 10 + 6 + g))],
        out_specs=[blk(lambda r, n: (n, r)), pl.BlockSpec((CHUNK, HEAD_DIM), lambda r, n: (n, r))],
        out_shape=[jax.ShapeDtypeStruct((length, dil * w), F32), jax.ShapeDtypeStruct((length, dil * HEAD_DIM), F32)],
        compiler_params=_cparams("parallel", "parallel"),
    )(qk.reshape(length, dil * B_QK), qk.reshape(length, dil * B_QK), qk.reshape(length, dil * B_QK),
      proj.reshape(length, dil * B_IN), proj.reshape(length, dil * B_IN))
    return o.reshape(s, w), lse.reshape(s, HEAD_DIM)


def _b_combine(os_, lses, proj, name, tq=512):
    s = proj.shape[0]
    w = B_HEADS * HEAD_DIM

    def body(o0_ref, o1_ref, o2_ref, l0_ref, l1_ref, l2_ref, z_ref, y_ref, oj_ref, lj_ref):
        l0, l1, l2 = l0_ref[...], l1_ref[...], l2_ref[...]
        m = jnp.maximum(jnp.maximum(l0, l1), l2)
        lj = m + jnp.log(jnp.exp(l0 - m) + jnp.exp(l1 - m) + jnp.exp(l2 - m))
        lj_ref[...] = lj
        w0, w1, w2 = jnp.exp(l0 - lj), jnp.exp(l1 - lj), jnp.exp(l2 - lj)
        for h in range(B_HEADS):
            sl = slice(h * HEAD_DIM, (h + 1) * HEAD_DIM)
            o = w0[:, h:h + 1] * o0_ref[:, sl] + w1[:, h:h + 1] * o1_ref[:, sl] + w2[:, h:h + 1] * o2_ref[:, sl]
            z = z_ref[:, sl].astype(F32)
            oj_ref[:, sl] = o.astype(oj_ref.dtype)
            y_ref[:, sl] = (o * (z * _sigmoid(z))).astype(y_ref.dtype)

    blk = pl.BlockSpec((tq, w), lambda i: (i, 0))
    st = pl.BlockSpec((tq, HEAD_DIM), lambda i: (i, 0))
    return pl.pallas_call(
        body, name=name, grid=(s // tq,),
        in_specs=[blk, blk, blk, st, st, st, pl.BlockSpec((tq, w), lambda i: (i, 9))],
        out_specs=[blk, blk, st],
        out_shape=[jax.ShapeDtypeStruct((s, w), CDT), jax.ShapeDtypeStruct((s, w), CDT), jax.ShapeDtypeStruct((s, HEAD_DIM), F32)],
        compiler_params=_cparams("parallel"),
    )(*os_, *lses, proj)


def _b_bwd_prep(dy, oj, proj, name, tq=512):
    s = proj.shape[0]
    w = B_HEADS * HEAD_DIM

    def body(dy_ref, oj_ref, z_ref, do_ref, dz_ref, dd_ref):
        lane = lax.broadcasted_iota(jnp.int32, (tq, HEAD_DIM), 1)
        dd = jnp.zeros((tq, HEAD_DIM), F32)
        for h in range(B_HEADS):
            sl = slice(h * HEAD_DIM, (h + 1) * HEAD_DIM)
            z = z_ref[:, sl].astype(F32)
            dyv = dy_ref[:, sl].astype(F32)
            o = oj_ref[:, sl].astype(F32)
            sg = _sigmoid(z)
            do = dyv * (z * sg)
            do_ref[:, sl] = do.astype(do_ref.dtype)
            dz_ref[:, sl] = (dyv * o * (sg * (1.0 + z * (1.0 - sg)))).astype(dz_ref.dtype)
            dd = jnp.where(lane == h, jnp.sum(do * o, axis=-1, keepdims=True), dd)
        dd_ref[...] = dd

    blk = pl.BlockSpec((tq, w), lambda i: (i, 0))
    st = pl.BlockSpec((tq, HEAD_DIM), lambda i: (i, 0))
    return pl.pallas_call(
        body, name=name, grid=(s // tq,),
        in_specs=[blk, blk, pl.BlockSpec((tq, w), lambda i: (i, 9))], out_specs=[blk, blk, st],
        out_shape=[jax.ShapeDtypeStruct((s, w), CDT), jax.ShapeDtypeStruct((s, w), CDT), jax.ShapeDtypeStruct((s, HEAD_DIM), F32)],
        compiler_params=_cparams("parallel"),
    )(dy, oj, proj)


def _b_attn_bwd(qk, proj, do, lj, dd, g, dil, name):
    s = qk.shape[0]
    length = s // dil
    nb = length // CHUNK
    scale = 1.0 / math.sqrt(HEAD_DIM)
    w = B_HEADS * HEAD_DIM

    def body(qj_ref, qn_ref, k_ref, v_ref, doj_ref, don_ref, lj_ref, ln_ref, dj_ref, dn_ref, out_ref, carry_ref):
        j = pl.program_id(1)

        @pl.when(j == 0)
        def _():
            carry_ref[...] = jnp.zeros_like(carry_ref)

        qi = lax.broadcasted_iota(jnp.int32, (2 * CHUNK, CHUNK), 0)
        ki = lax.broadcasted_iota(jnp.int32, (2 * CHUNK, CHUNK), 1)
        no_next = jnp.where(j + 1 < nb, 0, 2 * CHUNK)
        mask = ((qi < CHUNK) & (ki <= qi)) | ((qi >= CHUNK) & (ki >= qi - CHUNK + no_next))
        for h in range(B_HEADS):
            sl = slice(h * HEAD_DIM, (h + 1) * HEAD_DIM)
            q2 = jnp.concatenate([qj_ref[:, sl], qn_ref[:, sl]], axis=0)
            do2 = jnp.concatenate([doj_ref[:, sl], don_ref[:, sl]], axis=0)
            lse2 = jnp.concatenate([lj_ref[:, h:h + 1], ln_ref[:, h:h + 1]], axis=0)
            d2 = jnp.concatenate([dj_ref[:, h:h + 1], dn_ref[:, h:h + 1]], axis=0)
            k = k_ref[:, sl]
            v = v_ref[:, sl]
            sc = _dot(q2, k, NT) * scale
            p = jnp.where(mask, jnp.exp(sc - lse2), 0.0)
            dp = _dot(do2, v, NT)
            ds = (p * (dp - d2) * scale).astype(CDT)
            dq2 = _dot(ds, k)
            out_ref[:, sl] = (carry_ref[:, sl] + dq2[:CHUNK]).astype(out_ref.dtype)
            carry_ref[:, sl] = dq2[CHUNK:]
            out_ref[:, w + h * HEAD_DIM:w + (h + 1) * HEAD_DIM] = _dot(ds, q2, TN).astype(out_ref.dtype)
            out_ref[:, 2 * w + h * HEAD_DIM:2 * w + (h + 1) * HEAD_DIM] = _dot(p.astype(CDT), do2, TN).astype(out_ref.dtype)

    nxt = lambda j: jnp.minimum(j + 1, nb - 1)
    blk = lambda f: pl.BlockSpec((CHUNK, w), f)
    st = lambda f: pl.BlockSpec((CHUNK, HEAD_DIM), f)
    qk_v = qk.reshape(length, dil * B_QK)
    pj_v = proj.reshape(length, dil * B_IN)
    do_v = do.reshape(length, dil * w)
    lj_v = lj.reshape(length, dil * HEAD_DIM)
    dd_v = dd.reshape(length, dil * HEAD_DIM)
    out = pl.pallas_call(
        body, name=name, grid=(dil, nb),
        in_specs=[blk(lambda r, j: (j, r * 6 + g)), blk(lambda r, j: (nxt(j), r * 6 + g)),
                  blk(lambda r, j: (j, r * 6 + 3 + g)), blk(lambda r, j: (j, r * 10 + 6 + g)),
                  blk(lambda r, j: (j, r)), blk(lambda r, j: (nxt(j), r)),
                  st(lambda r, j: (j, r)), st(lambda r, j: (nxt(j), r)), st(lambda r, j: (j, r)), st(lambda r, j: (nxt(j), r))],
        out_specs=pl.BlockSpec((CHUNK, 3 * w), lambda r, j: (j, r)),
        out_shape=jax.ShapeDtypeStruct((length, dil * 3 * w), CDT),
        scratch_shapes=[pltpu.VMEM((CHUNK, w), F32)],
        compiler_params=_cparams("parallel", "arbitrary"),
    )(qk_v, qk_v, qk_v, pj_v, do_v, do_v, lj_v, lj_v, dd_v, dd_v)
    return out.reshape(s, 3 * w)


def _b_qk_bwd(proj, dqkv, dz, gains, tabs, name, tq=256):
    s = proj.shape[0]
    w = B_HEADS * HEAD_DIM

    def body(p_ref, d0_ref, d1_ref, d2_ref, dz_ref, g_ref, tc_ref, ta_ref, tb_ref, dp_ref, dg_ref):
        @pl.when(pl.program_id(0) == 0)
        def _():
            dg_ref[...] = jnp.zeros_like(dg_ref)

        tc, ta, tb = tc_ref[...], ta_ref[...], tb_ref[...]
        d_refs = (d0_ref, d1_ref, d2_ref)
        for g in range(3):
            for t in range(2):
                tg = t * 3 + g
                gain = g_ref[tg:tg + 1, :]
                dgain = jnp.zeros((1, HEAD_DIM), F32)
                for h in range(B_HEADS):
                    cols = slice(tg * w + h * HEAD_DIM, tg * w + (h + 1) * HEAD_DIM)
                    xv = p_ref[:, cols].astype(F32)
                    r = lax.rsqrt(jnp.mean(xv * xv, axis=-1, keepdims=True) + EPS)
                    dyv = d_refs[g][:, t * w + h * HEAD_DIM:t * w + (h + 1) * HEAD_DIM].astype(F32)
                    dxn = dyv * tc + pltpu.roll(dyv * ta, ROPE_HALF, 1) + pltpu.roll(dyv * tb, HEAD_DIM - ROPE_HALF, 1)
                    gd = dxn * gain
                    dx = r * gd - xv * (r * r * r) * jnp.mean(gd * xv, axis=-1, keepdims=True)
                    dp_ref[:, cols] = dx.astype(dp_ref.dtype)
                    dgain = dgain + jnp.sum(dxn * xv * r, axis=0, keepdims=True)
                dg_ref[tg:tg + 1, :] += dgain
            dp_ref[:, (6 + g) * w:(7 + g) * w] = d_refs[g][:, 2 * w:3 * w]
        dp_ref[:, 9 * w:10 * w] = dz_ref[...]

    tab = pl.BlockSpec((tq, HEAD_DIM), lambda i: (i, 0))
    dblk = pl.BlockSpec((tq, 3 * w), lambda i: (i, 0))
    return pl.pallas_call(
        body, name=name, grid=(s // tq,),
        in_specs=[pl.BlockSpec((tq, B_QK), lambda i: (i, 0)), dblk, dblk, dblk, pl.BlockSpec((tq, w), lambda i: (i, 0)),
                  pl.BlockSpec((6, HEAD_DIM), lambda i: (0, 0)), tab, tab, tab],
        out_specs=[pl.BlockSpec((tq, B_IN), lambda i: (i, 0)), pl.BlockSpec((6, HEAD_DIM), lambda i: (0, 0))],
        out_shape=[jax.ShapeDtypeStruct((s, B_IN), CDT), jax.ShapeDtypeStruct((6, HEAD_DIM), F32)],
        compiler_params=_cparams("arbitrary"),
    )(proj, *dqkv, dz, gains, *tabs)


def _inv_count(t, window):
    return 1.0 / jnp.minimum(t + 1, window).astype(F32)


def _c_pool_fwd(proj, name, tq=512):
    s = proj.shape[0]
    per = tq // POOL_HALO

    def body(x_ref, halo_ref, o_ref, ext_ref):
        i = pl.program_id(0)
        xv = x_ref[...].astype(F32)
        ext_ref[POOL_HALO:, :] = xv
        ext_ref[:POOL_HALO, :] = jnp.where(i > 0, halo_ref[...].astype(F32), 0.0)
        t = i * tq + lax.broadcasted_iota(jnp.int32, (tq, 1), 0)
        for g, window in enumerate(POOL_SIZES):
            cols = slice(g * C_GROUP, (g + 1) * C_GROUP)
            acc = xv[:, cols]
            for back in range(1, window):
                acc = acc + ext_ref[POOL_HALO - back:POOL_HALO - back + tq, cols]
            o_ref[:, cols] = (acc * _inv_count(t, window) - xv[:, cols]).astype(o_ref.dtype)

    return pl.pallas_call(
        body, name=name, grid=(s // tq,),
        in_specs=[pl.BlockSpec((tq, C_WIDTH), lambda i: (i, 0)),
                  pl.BlockSpec((POOL_HALO, C_WIDTH), lambda i: (jnp.maximum(i * per - 1, 0), 0))],
        out_specs=pl.BlockSpec((tq, C_WIDTH), lambda i: (i, 0)),
        out_shape=jax.ShapeDtypeStruct((s, C_WIDTH), CDT),
        scratch_shapes=[pltpu.VMEM((tq + POOL_HALO, C_WIDTH), F32)],
        compiler_params=_cparams("parallel"),
    )(proj, proj)


def _c_mid_fwd(diff, proj, w_grp, scale, name, tq=512):
    s = diff.shape[0]

    def body(d_ref, z_ref, w_ref, sc_ref, y_ref):
        for g in range(len(POOL_SIZES)):
            cols = slice(g * C_GROUP, (g + 1) * C_GROUP)
            z = z_ref[:, cols].astype(F32)
            y_ref[:, cols] = (_dot(d_ref[:, cols], w_ref[g]) * sc_ref[:, cols] * (z * _sigmoid(z))).astype(y_ref.dtype)

    return pl.pallas_call(
        body, name=name, grid=(s // tq,),
        in_specs=[pl.BlockSpec((tq, C_WIDTH), lambda i: (i, 0)), pl.BlockSpec((tq, C_WIDTH), lambda i: (i, 1)),
                  pl.BlockSpec((4, C_GROUP, C_GROUP), lambda i: (0, 0, 0)), pl.BlockSpec((1, C_WIDTH), lambda i: (0, 0))],
        out_specs=pl.BlockSpec((tq, C_WIDTH), lambda i: (i, 0)),
        out_shape=jax.ShapeDtypeStruct((s, C_WIDTH), CDT), compiler_params=_cparams("parallel"),
    )(diff, proj, w_grp, scale)


def _c_mid_bwd(diff, proj, w_grp, scale, dy, name, tq=512):
    s = diff.shape[0]

    def body(d_ref, z_ref, w_ref, sc_ref, dy_ref, dd_ref, dz_ref, dw_ref, dsc_ref):
        @pl.when(pl.program_id(0) == 0)
        def _():
            dw_ref[...] = jnp.zeros_like(dw_ref)
            dsc_ref[...] = jnp.zeros_like(dsc_ref)

        for g in range(len(POOL_SIZES)):
            cols = slice(g * C_GROUP, (g + 1) * C_GROUP)
            d = d_ref[:, cols]
            m0 = _dot(d, w_ref[g])
            z = z_ref[:, cols].astype(F32)
            dyv = dy_ref[:, cols].astype(F32)
            sc = sc_ref[:, cols]
            sg = _sigmoid(z)
            dmixed = dyv * (z * sg)
            dz_ref[:, cols] = (dyv * m0 * sc * (sg * (1.0 + z * (1.0 - sg)))).astype(dz_ref.dtype)
            dsc_ref[:, cols] += jnp.sum(dmixed * m0, axis=0, keepdims=True)
            dm0 = (dmixed * sc).astype(CDT)
            dw_ref[g] += _dot(d, dm0, TN)
            dd_ref[:, cols] = _dot(dm0, w_ref[g], NT)

    blk = pl.BlockSpec((tq, C_WIDTH), lambda i: (i, 0))
    wsp = pl.BlockSpec((4, C_GROUP, C_GROUP), lambda i: (0, 0, 0))
    vec = pl.BlockSpec((1, C_WIDTH), lambda i: (0, 0))
    return pl.pallas_call(
        body, name=name, grid=(s // tq,),
        in_specs=[blk, pl.BlockSpec((tq, C_WIDTH), lambda i: (i, 1)), wsp, vec, blk],
        out_specs=[blk, blk, wsp, vec],
        out_shape=[jax.ShapeDtypeStruct((s, C_WIDTH), F32), jax.ShapeDtypeStruct((s, C_WIDTH), CDT),
                   jax.ShapeDtypeStruct((4, C_GROUP, C_GROUP), F32), jax.ShapeDtypeStruct((1, C_WIDTH), F32)],
        compiler_params=_cparams("arbitrary"),
    )(diff, proj, w_grp, scale, dy)


def _c_pool_bwd(ddiff, dz, name, tq=512):
    s = ddiff.shape[0]
    per = tq // POOL_HALO
    last = s // tq - 1

    def body(d_ref, halo_ref, dz_ref, o_ref, ext_ref):
        i = pl.program_id(0)
        dv = d_ref[...]
        t = i * tq + lax.broadcasted_iota(jnp.int32, (tq, 1), 0)
        for g, window in enumerate(POOL_SIZES):
            cols = slice(g * C_GROUP, (g + 1) * C_GROUP)
            ext_ref[:tq, cols] = dv[:, cols] * _inv_count(t, window)
            ext_ref[tq:, cols] = jnp.where(i < last, halo_ref[:, cols] * (1.0 / window), 0.0)
            acc = -dv[:, cols]
            for fwd in range(window):
                acc = acc + ext_ref[fwd:fwd + tq, cols]
            o_ref[:, cols] = acc.astype(o_ref.dtype)
        o_ref[:, C_WIDTH:] = dz_ref[...]

    return pl.pallas_call(
        body, name=name, grid=(s // tq,),
        in_specs=[pl.BlockSpec((tq, C_WIDTH), lambda i: (i, 0)),
                  pl.BlockSpec((POOL_HALO, C_WIDTH), lambda i: (jnp.minimum((i + 1) * per, s // POOL_HALO - 1), 0)),
                  pl.BlockSpec((tq, C_WIDTH), lambda i: (i, 0))],
        out_specs=pl.BlockSpec((tq, 2 * C_WIDTH), lambda i: (i, 0)),
        out_shape=jax.ShapeDtypeStruct((s, 2 * C_WIDTH), CDT),
        scratch_shapes=[pltpu.VMEM((tq + POOL_HALO, C_WIDTH), F32)],
        compiler_params=_cparams("parallel"),
    )(ddiff, ddiff, dz)


LAYERS = (("a", 0), ("b", 0), ("c", 0), ("a", 1))


def _local_step(x, target, w):
    s = x.shape[0]
    tabs = _rope_tables(s)
    qk_gains = jnp.concatenate([w["b_q_gain"][0], w["b_k_gain"][0]], axis=0)
    saved = []
    for li, (kind, j) in enumerate(LAYERS):
        h = _rms_fwd(x, w["norm_gain"][li:li + 1], f"rms_fwd{li}")
        if kind == "a":
            proj = _mm(h, w["a_w_in"][j], "nn", CDT, f"a_in{li}")
            bs_t = w["a_b_s"][j].T
            y = _a_mid_fwd(proj, w["a_v_gain"][j:j + 1], w["a_w_s"][j], bs_t, f"a_mid_fwd{li}")
            x_next = _mm(y, w["a_w_out"][j], "nn", F32, f"a_out{li}", tm=512, tk=2048, add=x)
            saved.append((x, h, proj, y))
        elif kind == "b":
            proj = _mm(h, w["b_w_in"][j], "nn", CDT, f"b_in{li}")
            qk = _b_qk_fwd(proj, qk_gains, tabs, f"b_qk_fwd{li}")
            outs = [_b_attn_fwd(qk, proj, g, dil, f"b_attn_fwd{li}_{g}") for g, dil in enumerate(B_DILATIONS)]
            y, oj, lj = _b_combine([o for o, _ in outs], [l for _, l in outs], proj, f"b_combine{li}")
            x_next = _mm(y, w["b_w_out"][j], "nn", F32, f"b_out{li}", tm=512, add=x)
            saved.append((x, h, proj, y, qk, oj, lj))
        else:
            proj = _mm(h, w["c_w_in"][j], "nn", CDT, f"c_in{li}")
            diff = _c_pool_fwd(proj, f"c_pool_fwd{li}")
            y = _c_mid_fwd(diff, proj, w["c_w_grp"][j], w["c_scale"][j:j + 1], f"c_mid_fwd{li}")
            x_next = _mm(y, w["c_w_out"][j], "nn", F32, f"c_out{li}", tm=512, tk=2048, add=x)
            saved.append((x, h, proj, y, diff))
        x = x_next

    dx, dxc, sq = _loss_bwd(x, target, "loss_bwd")
    grads = {"norm_gain": [None] * len(LAYERS), "a_w_in": [None, None], "a_v_gain": [None, None], "a_w_s": [None, None],
             "a_b_s": [None, None], "a_w_out": [None, None]}
    for li in reversed(range(len(LAYERS))):
        kind, j = LAYERS[li]
        sv = saved[li]
        xin, h, proj, y = sv[:4]
        if kind == "a":
            grads["a_w_out"][j] = _mm(y, dxc, "tn", CDT, f"a_dwout{li}")
            dy = _mm(dxc, w["a_w_out"][j], "nt", CDT, f"a_dy{li}")
            dproj, dws, dbs_t, dvg = _a_mid_bwd(proj, dy, w["a_v_gain"][j:j + 1], w["a_w_s"][j], w["a_b_s"][j].T, f"a_mid_bwd{li}")
            grads["a_w_s"][j], grads["a_b_s"][j], grads["a_v_gain"][j] = dws, dbs_t.T, dvg[0]
            grads["a_w_in"][j] = _mm(h, dproj, "tn", CDT, f"a_dwin{li}")
            dh = _mm(dproj, w["a_w_in"][j], "nt", F32, f"a_dh{li}")
        elif kind == "b":
            qk, oj, lj = sv[4:]
            grads["b_w_out"] = _mm(y, dxc, "tn", CDT, f"b_dwout{li}")[None]
            dy = _mm(dxc, w["b_w_out"][j], "nt", CDT, f"b_dy{li}")
            do, dz, dd = _b_bwd_prep(dy, oj, proj, f"b_bwd_prep{li}")
            dqkv = [_b_attn_bwd(qk, proj, do, lj, dd, g, dil, f"b_attn_bwd{li}_{g}") for g, dil in enumerate(B_DILATIONS)]
            dproj, dgains = _b_qk_bwd(proj, dqkv, dz, qk_gains, tabs, f"b_qk_bwd{li}")
            grads["b_q_gain"], grads["b_k_gain"] = dgains[None, :3], dgains[None, 3:]
            grads["b_w_in"] = _mm(h, dproj, "tn", CDT, f"b_dwin{li}")[None]
            dh = _mm(dproj, w["b_w_in"][j], "nt", F32, f"b_dh{li}")
        else:
            diff = sv[4]
            grads["c_w_out"] = _mm(y, dxc, "tn", CDT, f"c_dwout{li}")[None]
            dy = _mm(dxc, w["c_w_out"][j], "nt", CDT, f"c_dy{li}")
            ddiff, dz, dwg, dsc = _c_mid_bwd(diff, proj, w["c_w_grp"][j], w["c_scale"][j:j + 1], dy, f"c_mid_bwd{li}")
            grads["c_w_grp"], grads["c_scale"] = dwg[None], dsc
            dproj = _c_pool_bwd(ddiff, dz, f"c_pool_bwd{li}")
            grads["c_w_in"] = _mm(h, dproj, "tn", CDT, f"c_dwin{li}")[None]
            dh = _mm(dproj, w["c_w_in"][j], "nt", F32, f"c_dh{li}")
        dx, dxc, dng = _rms_bwd(xin, dh, dx, w["norm_gain"][li:li + 1], f"rms_bwd{li}")
        grads["norm_gain"][li] = dng[0]
    for name in ("norm_gain", "a_w_in", "a_v_gain", "a_w_s", "a_b_s", "a_w_out"):
        grads[name] = jnp.stack(grads[name])
    return sq, dx, grads


ROW = 1024
BIG = (("a_w_in", 3072), ("a_w_out", 1024), ("b_w_in", 2560), ("b_w_out", 256), ("c_w_in", 1024), ("c_w_grp", 256), ("c_w_out", 512))
BIG_ROWS = sum(r for _, r in BIG)
SMALL_PAD = 16
W_ROWS = 8736
REP = (("norm_gain", (4, 1024)), ("a_w_s", (2, 8, 128, 128)), ("a_b_s", (2, 8, 128)), ("b_q_gain", (1, 3, 128)), ("b_k_gain", (1, 3, 128)))
REP_PART = 80
G_AVG, G_CSC, G_REP = BIG_ROWS, BIG_ROWS + SMALL_PAD, BIG_ROWS + 2 * SMALL_PAD
G_ROWS = 8832
G_HALF = G_ROWS // 2
SHARDED = ("a_w_in", "a_v_gain", "a_w_out", "b_w_in", "b_w_out", "c_w_in", "c_w_grp", "c_scale", "c_w_out")
BLOCK_SHAPES = {"a_w_in": (2, 1024, 1536), "a_v_gain": (2, 512), "a_w_out": (2, 512, 1024), "b_w_in": (1, 1024, 2560),
                "b_w_out": (1, 256, 1024), "c_w_in": (1, 1024, 1024), "c_w_grp": (1, 4, 128, 512), "c_scale": (1, 512),
                "c_w_out": (1, 512, 1024)}
SHARD_AXIS = {"a_w_in": 2, "a_v_gain": 1, "a_w_out": 1, "b_w_in": 2, "b_w_out": 1, "c_w_in": 2, "c_w_grp": 2, "c_scale": 1, "c_w_out": 1}
WEIGHTS = ("norm_gain", "a_w_in", "a_v_gain", "a_w_s", "a_b_s", "a_w_out", "b_w_in", "b_q_gain", "b_k_gain", "b_w_out",
           "c_w_in", "c_w_grp", "c_scale", "c_w_out")


def _rows(a, rows):
    a = a.reshape(-1)
    return jnp.pad(a, (0, rows * ROW - a.shape[0])).reshape(rows, ROW)


def _pack_blocks(blocks, rep_part, dtype):
    parts = [blocks[n].astype(dtype).reshape(r, ROW) for n, r in BIG]
    parts += [_rows(blocks["a_v_gain"].astype(dtype), SMALL_PAD), _rows(blocks["c_scale"].astype(dtype), SMALL_PAD),
              rep_part.astype(dtype), jnp.zeros((G_ROWS - G_REP - REP_PART, ROW), dtype)]
    return jnp.concatenate(parts, axis=0)


def _unpack_blocks(slab):
    out, off = {}, 0
    for n, r in BIG:
        out[n] = slab[off:off + r].reshape(BLOCK_SHAPES[n])
        off += r
    out["a_v_gain"] = slab[G_AVG].reshape(2, 512)
    out["c_scale"] = slab[G_CSC, :512].reshape(1, 512)
    return out


def _pack_rep(tree):
    return _rows(jnp.concatenate([tree[n].astype(F32).reshape(-1) for n, _ in REP]), N_CHIPS * REP_PART)


def _unpack_rep(slab):
    flat, out, off = slab.reshape(-1), {}, 0
    for n, shape in REP:
        size = math.prod(shape)
        out[n] = flat[off:off + size].reshape(shape)
        off += size
    return out


def _shard(full, name, k):
    axis, size = SHARD_AXIS[name], BLOCK_SHAPES[name][SHARD_AXIS[name]]
    return lax.slice_in_dim(full, k * size, (k + 1) * size, axis=axis)


def _pack_weights(blocks):
    parts = [blocks[n].astype(CDT).reshape(r, ROW) for n, r in BIG]
    gains = jnp.concatenate([blocks["a_v_gain"].reshape(-1), blocks["c_scale"].reshape(-1)])
    parts.append(_rows(lax.bitcast_convert_type(gains, CDT), W_ROWS - BIG_ROWS))
    return jnp.concatenate(parts, axis=0)


def _unpack_weights(gathered):
    out, off = {}, 0
    for n, r in BIG:
        out[n] = jnp.concatenate([gathered[k, off:off + r].reshape(BLOCK_SHAPES[n]) for k in range(N_CHIPS)], axis=SHARD_AXIS[n])
        off += r
    gains = lax.bitcast_convert_type(gathered[:, off:off + 3].reshape(N_CHIPS, 1536, 2), F32)
    out["a_v_gain"] = jnp.concatenate([gains[k, :1024].reshape(2, 512) for k in range(N_CHIPS)], axis=1)
    out["c_scale"] = jnp.concatenate([gains[k, 1024:].reshape(1, 512) for k in range(N_CHIPS)], axis=1)
    return out


def _place():
    x, y, c = lax.axis_index("x"), lax.axis_index("y"), lax.axis_index("c")
    chips = [(1 - x, y), (x, 1 - y), (1 - x, 1 - y)]
    return x, y, c, 2 * x + y, (x, y, 1 - c), chips


def _remote(src, dst, sems, j, to):
    send_sems, recv_sems = sems
    return pltpu.make_async_remote_copy(src_ref=src, dst_ref=dst, send_sem=send_sems.at[j], recv_sem=recv_sems.at[j],
                                        device_id=to, device_id_type=MESH)


def _comm_call(body, name, out_shape, n_sems, *args):
    return pl.pallas_call(
        body, name=name, in_specs=[ANY] * len(args), out_specs=ANY, out_shape=out_shape,
        scratch_shapes=[pltpu.SemaphoreType.DMA((n_sems,)), pltpu.SemaphoreType.DMA((n_sems,)), pltpu.SemaphoreType.DMA(())],
    )(*args)


def _allgather_weights(wp):
    half = W_ROWS // 2

    def body(w_ref, o_ref, send_sems, recv_sems, local_sem):
        x, y, c, me, sibling, chips = _place()
        sems = (send_sems, recv_sems)
        mine = pl.ds(pl.multiple_of(c * half, 16), half)
        other = pl.ds(pl.multiple_of((1 - c) * half, 16), half)
        own = pltpu.make_async_copy(w_ref, o_ref.at[me], local_sem)
        own.start()
        first = [_remote(w_ref.at[mine], o_ref.at[me, mine], sems, j, (*chip, c)) for j, chip in enumerate(chips)]
        for cp in first:
            cp.start()
        passed = []
        for j, (cx, cy) in enumerate(chips):
            landed = o_ref.at[2 * cx + cy, mine]
            _remote(landed, landed, sems, j, sibling).wait_recv()
            passed.append(_remote(landed, landed, sems, 3 + j, sibling))
            passed[-1].start()
        for j, (cx, cy) in enumerate(chips):
            landed = o_ref.at[2 * cx + cy, other]
            _remote(landed, landed, sems, 3 + j, sibling).wait_recv()
        for cp in first + passed:
            cp.wait_send()
        own.wait()

    return _comm_call(body, "allgather_weights", jax.ShapeDtypeStruct((N_CHIPS, W_ROWS, ROW), wp.dtype), 6, wp)


def _swap_halves(gp):
    def body(g_ref, o_ref, send_sems, recv_sems, local_sem):
        x, y, c, me, sibling, chips = _place()
        theirs = g_ref.at[:, pl.ds(pl.multiple_of((1 - c) * G_HALF, 16), G_HALF), :]
        cp = _remote(theirs, o_ref, (send_sems, recv_sems), 0, sibling)
        cp.start()
        cp.wait()

    return _comm_call(body, "grad_swap_halves", jax.ShapeDtypeStruct((N_CHIPS, G_HALF, ROW), gp.dtype), 1, gp)


def _scatter_shards(s1):
    def body(s_ref, o_ref, send_sems, recv_sems, local_sem):
        x, y, c, me, sibling, chips = _place()
        sems = (send_sems, recv_sems)
        own = pltpu.make_async_copy(s_ref.at[me], o_ref.at[me], local_sem)
        own.start()
        sent = [_remote(s_ref.at[2 * cx + cy], o_ref.at[me], sems, j, (cx, cy, c)) for j, (cx, cy) in enumerate(chips)]
        for cp in sent:
            cp.start()
        for j, (cx, cy) in enumerate(chips):
            slot = o_ref.at[2 * cx + cy]
            _remote(slot, slot, sems, j, sibling).wait_recv()
        for cp in sent:
            cp.wait_send()
        own.wait()

    return _comm_call(body, "grad_scatter_shards", jax.ShapeDtypeStruct((N_CHIPS, G_HALF, ROW), s1.dtype), 3, s1)


def _share_halves(out4):
    def body(h_ref, o_ref, send_sems, recv_sems, local_sem):
        x, y, c, me, sibling, chips = _place()
        mine = o_ref.at[:, pl.ds(pl.multiple_of(c * G_HALF, 16), G_HALF), :]
        theirs = o_ref.at[:, pl.ds(pl.multiple_of((1 - c) * G_HALF, 16), G_HALF), :]
        own = pltpu.make_async_copy(h_ref, mine, local_sem)
        own.start()
        cp = _remote(h_ref, mine, (send_sems, recv_sems), 0, sibling)
        cp.start()
        _remote(theirs, theirs, (send_sems, recv_sems), 0, sibling).wait_recv()
        cp.wait_send()
        own.wait()

    return _comm_call(body, "share_halves", jax.ShapeDtypeStruct((4, G_ROWS, ROW), out4.dtype), 1, out4)


def _allgather_rep(rep4):
    def body(r_ref, o_ref, send_sems, recv_sems, local_sem):
        x, y, c, me, sibling, chips = _place()
        sems = (send_sems, recv_sems)
        own = pltpu.make_async_copy(r_ref, o_ref.at[me], local_sem)
        own.start()
        sent = [_remote(r_ref, o_ref.at[me], sems, j, (*chip, c)) for j, chip in enumerate(chips)]
        for cp in sent:
            cp.start()
        for j, (cx, cy) in enumerate(chips):
            slot = o_ref.at[2 * cx + cy]
            _remote(slot, slot, sems, j, sibling).wait_recv()
        for cp in sent:
            cp.wait_send()
        own.wait()

    return _comm_call(body, "allgather_rep", jax.ShapeDtypeStruct((N_CHIPS,) + rep4.shape, rep4.dtype), 3, rep4)


ADAM_TILE = 192


def _add_halves(a, b):
    def body(a_ref, b_ref, o_ref):
        o_ref[...] = (a_ref[...].astype(F32) + b_ref[...].astype(F32)).astype(o_ref.dtype)

    blk = pl.BlockSpec((1, ADAM_TILE, ROW), lambda k, i: (k, i, 0))
    return pl.pallas_call(
        body, name="grad_add_halves", grid=(N_CHIPS, G_HALF // ADAM_TILE), in_specs=[blk, blk], out_specs=blk,
        out_shape=jax.ShapeDtypeStruct(a.shape, a.dtype), compiler_params=_cparams("parallel", "parallel"),
    )(a, b)


def _adamw(parts, w, m, v):
    def body(p_ref, w_ref, m_ref, v_ref, o_ref):
        g = p_ref[0].astype(F32)
        for k in range(1, N_CHIPS):
            g = g + p_ref[k].astype(F32)
        m2 = ADAM_B1 * m_ref[...] + (1.0 - ADAM_B1) * g
        v2 = ADAM_B2 * v_ref[...] + (1.0 - ADAM_B2) * jnp.square(g)
        m_hat = m2 / (1.0 - ADAM_B1 ** ADAM_STEP)
        v_hat = v2 / (1.0 - ADAM_B2 ** ADAM_STEP)
        o_ref[0] = g
        o_ref[1] = -ADAM_LR * (m_hat / (jnp.sqrt(v_hat) + ADAM_EPS) + ADAM_WD * w_ref[...])
        o_ref[2] = m2
        o_ref[3] = v2

    blk = pl.BlockSpec((ADAM_TILE, ROW), lambda i: (i, 0))
    blk4 = pl.BlockSpec((4, ADAM_TILE, ROW), lambda i: (0, i, 0))
    return pl.pallas_call(
        body, name="adamw", grid=(G_HALF // ADAM_TILE,), in_specs=[blk4, blk, blk, blk], out_specs=blk4,
        out_shape=jax.ShapeDtypeStruct((4, G_HALF, ROW), F32), compiler_params=_cparams("parallel"),
    )(parts, w, m, v)


def kernel(x, norm_gain, a_w_in, a_v_gain, a_w_s, a_b_s, a_w_out, b_w_in, b_q_gain, b_k_gain, b_w_out, c_w_in, c_w_grp, c_scale, c_w_out, loss_target, m_norm_gain, m_a_w_in, m_a_v_gain, m_a_w_s, m_a_b_s, m_a_w_out, m_b_w_in, m_b_q_gain, m_b_k_gain, m_b_w_out, m_c_w_in, m_c_w_grp, m_c_scale, m_c_w_out, v_norm_gain, v_a_w_in, v_a_v_gain, v_a_w_s, v_a_b_s, v_a_w_out, v_b_w_in, v_b_q_gain, v_b_k_gain, v_b_w_out, v_c_w_in, v_c_w_grp, v_c_scale, v_c_w_out):
    wts = dict(norm_gain=norm_gain, a_w_in=a_w_in, a_v_gain=a_v_gain, a_w_s=a_w_s, a_b_s=a_b_s, a_w_out=a_w_out, b_w_in=b_w_in,
               b_q_gain=b_q_gain, b_k_gain=b_k_gain, b_w_out=b_w_out, c_w_in=c_w_in, c_w_grp=c_w_grp, c_scale=c_scale, c_w_out=c_w_out)
    mom1 = dict(norm_gain=m_norm_gain, a_w_in=m_a_w_in, a_v_gain=m_a_v_gain, a_w_s=m_a_w_s, a_b_s=m_a_b_s, a_w_out=m_a_w_out,
                b_w_in=m_b_w_in, b_q_gain=m_b_q_gain, b_k_gain=m_b_k_gain, b_w_out=m_b_w_out, c_w_in=m_c_w_in, c_w_grp=m_c_w_grp,
                c_scale=m_c_scale, c_w_out=m_c_w_out)
    mom2 = dict(norm_gain=v_norm_gain, a_w_in=v_a_w_in, a_v_gain=v_a_v_gain, a_w_s=v_a_w_s, a_b_s=v_a_b_s, a_w_out=v_a_w_out,
                b_w_in=v_b_w_in, b_q_gain=v_b_q_gain, b_k_gain=v_b_k_gain, b_w_out=v_b_w_out, c_w_in=v_c_w_in, c_w_grp=v_c_w_grp,
                c_scale=v_c_scale, c_w_out=v_c_w_out)
    axes = ("x", "y", "c")
    me = 2 * lax.axis_index("x") + lax.axis_index("y")
    core = lax.axis_index("c")

    full = _unpack_weights(_allgather_weights(_pack_weights(wts)))
    for n, _ in REP:
        full[n] = wts[n]
    sq, grad_x, grads = _local_step(x[0], loss_target[0], full)
    loss = lax.psum(0.5 * jnp.sum(sq) / D_MODEL, axes)

    rep_g = _pack_rep(grads)
    gp = jnp.stack([_pack_blocks({n: _shard(grads[n], n, k) for n in SHARDED}, rep_g[k * REP_PART:(k + 1) * REP_PART], CDT)
                    for k in range(N_CHIPS)])
    from_sibling = _swap_halves(gp)
    my_rows = lax.dynamic_slice_in_dim(gp, core * G_HALF, G_HALF, axis=1)
    parts = _scatter_shards(_add_halves(my_rows, from_sibling))

    def my_half(tree):
        rep_part = lax.dynamic_slice_in_dim(_pack_rep(tree), me * REP_PART, REP_PART, axis=0)
        return lax.dynamic_slice_in_dim(_pack_blocks(tree, rep_part, F32), core * G_HALF, G_HALF, axis=0)

    res = _share_halves(_adamw(parts, my_half(wts), my_half(mom1), my_half(mom2)))
    rep_all = _allgather_rep(res[:, G_REP:G_REP + REP_PART])

    outs = []
    for q in range(4):
        tree = _unpack_blocks(res[q])
        tree.update(_unpack_rep(rep_all[:, q].reshape(N_CHIPS * REP_PART, ROW)))
        outs.append(tree)
    return (loss, grad_x[None], *[t[n] for t in outs for n in WEIGHTS])
```

```python
import functools
import math

import jax
import jax.numpy as jnp
from jax import lax
from jax.experimental import pallas as pl
from jax.experimental.pallas import tpu as pltpu

F32 = jnp.float32
CDT = jnp.bfloat16

D_MODEL = 1024
EPS = 1e-6
CHUNK = 128
A_WIDTH = 2048
A_GROUPS = 8
A_GROUP_DIM = 256
HEAD_DIM = 128
B_HEADS = 8
B_DILATIONS = (1, 4, 16)
B_QK = 6144
B_IN = 10240
ROPE_HALF = 16
ROPE_THETA = 500000.0
POOL_SIZES = (2, 4, 8, 16)
POOL_HALO = 16
C_WIDTH = 2048
C_GROUP = 512
N_CHIPS = 4

ADAM_LR = 0.001
ADAM_B1 = 0.9
ADAM_B2 = 0.999
ADAM_EPS = 1e-08
ADAM_WD = 0.01
ADAM_STEP = 10

VMEM_LIMIT = 48 * 1024 * 1024
ANY = pl.BlockSpec(memory_space=pl.ANY)
MESH = pl.DeviceIdType.MESH

NN = (((1,), (0,)), ((), ()))
NT = (((1,), (1,)), ((), ()))
TN = (((0,), (0,)), ((), ()))


def _cparams(*sem):
    return pltpu.CompilerParams(dimension_semantics=sem, vmem_limit_bytes=VMEM_LIMIT)


def _dot(a, b, dims=NN):
    return lax.dot_general(a, b, dims, preferred_element_type=F32)


def _sigmoid(z):
    return 1.0 / (1.0 + jnp.exp(-z))


def _mm(a, b, mode, out_dtype, name, tm=1024, tn=1024, tk=1024, add=None):
    if mode == "nn":
        (m, k), n = a.shape, b.shape[1]
    elif mode == "nt":
        (m, k), n = a.shape, b.shape[0]
    else:
        (k, m), n = a.shape, b.shape[1]
    tm, tn, tk = min(tm, m), min(tn, n), min(tk, k)
    nk = k // tk
    a_spec = {"nn": pl.BlockSpec((tm, tk), lambda i, j, q: (i, q)),
              "nt": pl.BlockSpec((tm, tk), lambda i, j, q: (i, q)),
              "tn": pl.BlockSpec((tk, tm), lambda i, j, q: (q, i))}[mode]
    b_spec = {"nn": pl.BlockSpec((tk, tn), lambda i, j, q: (q, j)),
              "nt": pl.BlockSpec((tn, tk), lambda i, j, q: (j, q)),
              "tn": pl.BlockSpec((tk, tn), lambda i, j, q: (q, j))}[mode]
    dims = {"nn": NN, "nt": NT, "tn": TN}[mode]
    has_add = add is not None

    def body(*refs):
        a_ref, b_ref = refs[0], refs[1]
        o_ref = refs[3] if has_add else refs[2]
        p = _dot(a_ref[...], b_ref[...], dims)

        def finish(v):
            if has_add:
                v = v + refs[2][...]
            o_ref[...] = v.astype(o_ref.dtype)

        if nk == 1:
            finish(p)
        else:
            acc_ref = refs[-1]
            q = pl.program_id(2)

            @pl.when(q == 0)
            def _():
                acc_ref[...] = p

            @pl.when(q > 0)
            def _():
                acc_ref[...] += p

            @pl.when(q == nk - 1)
            def _():
                finish(acc_ref[...])

    in_specs = [a_spec, b_spec]
    args = [a, b]
    if has_add:
        in_specs.append(pl.BlockSpec((tm, tn), lambda i, j, q: (i, j)))
        args.append(add)
    return pl.pallas_call(
        body, name=name, grid=(m // tm, n // tn, nk), in_specs=in_specs,
        out_specs=pl.BlockSpec((tm, tn), lambda i, j, q: (i, j)),
        out_shape=jax.ShapeDtypeStruct((m, n), out_dtype),
        scratch_shapes=[pltpu.VMEM((tm, tn), F32)] if nk > 1 else [],
        compiler_params=_cparams("parallel", "parallel", "arbitrary"),
    )(*args)


def _rms_fwd(x, g, name, tq=512):
    s, d = x.shape

    def body(x_ref, g_ref, h_ref):
        xv = x_ref[...]
        r = lax.rsqrt(jnp.mean(xv * xv, axis=-1, keepdims=True) + EPS)
        h_ref[...] = (xv * r * g_ref[...]).astype(h_ref.dtype)

    return pl.pallas_call(
        body, name=name, grid=(s // tq,),
        in_specs=[pl.BlockSpec((tq, d), lambda i: (i, 0)), pl.BlockSpec((1, d), lambda i: (0, 0))],
        out_specs=pl.BlockSpec((tq, d), lambda i: (i, 0)),
        out_shape=jax.ShapeDtypeStruct((s, d), CDT), compiler_params=_cparams("parallel"),
    )(x, g)


def _rms_bwd(x, dh, dxo, g, name, tq=512):
    s, d = x.shape

    def body(x_ref, dh_ref, dxo_ref, g_ref, dx_ref, dxc_ref, dg_ref):
        xv = x_ref[...]
        dhv = dh_ref[...].astype(F32)
        r = lax.rsqrt(jnp.mean(xv * xv, axis=-1, keepdims=True) + EPS)
        gd = dhv * g_ref[...]
        dx = dxo_ref[...] + r * gd - xv * (r * r * r) * jnp.mean(gd * xv, axis=-1, keepdims=True)
        dx_ref[...] = dx
        dxc_ref[...] = dx.astype(dxc_ref.dtype)

        @pl.when(pl.program_id(0) == 0)
        def _():
            dg_ref[...] = jnp.zeros_like(dg_ref)

        dg_ref[...] += jnp.sum(dhv * xv * r, axis=0, keepdims=True)

    blk = pl.BlockSpec((tq, d), lambda i: (i, 0))
    vec = pl.BlockSpec((1, d), lambda i: (0, 0))
    return pl.pallas_call(
        body, name=name, grid=(s // tq,), in_specs=[blk, blk, blk, vec], out_specs=[blk, blk, vec],
        out_shape=[jax.ShapeDtypeStruct((s, d), F32), jax.ShapeDtypeStruct((s, d), CDT), jax.ShapeDtypeStruct((1, d), F32)],
        compiler_params=_cparams("arbitrary"),
    )(x, dh, dxo, g)


def _loss_bwd(y, target, name, tq=512):
    s, d = y.shape

    def body(y_ref, t_ref, dx_ref, dxc_ref, sq_ref):
        err = y_ref[...] - t_ref[...]
        dx = err * (1.0 / d)
        dx_ref[...] = dx
        dxc_ref[...] = dx.astype(dxc_ref.dtype)

        @pl.when(pl.program_id(0) == 0)
        def _():
            sq_ref[...] = jnp.zeros_like(sq_ref)

        sq_ref[...] += jnp.sum(err * err, axis=0, keepdims=True)

    blk = pl.BlockSpec((tq, d), lambda i: (i, 0))
    vec = pl.BlockSpec((1, d), lambda i: (0, 0))
    return pl.pallas_call(
        body, name=name, grid=(s // tq,), in_specs=[blk, blk], out_specs=[blk, blk, vec],
        out_shape=[jax.ShapeDtypeStruct((s, d), F32), jax.ShapeDtypeStruct((s, d), CDT), jax.ShapeDtypeStruct((1, d), F32)],
        compiler_params=_cparams("arbitrary"),
    )(y, target)


def _tril_mask():
    row = lax.broadcasted_iota(jnp.int32, (CHUNK, CHUNK), 0)
    col = lax.broadcasted_iota(jnp.int32, (CHUNK, CHUNK), 1)
    return row >= col


def _a_mid_fwd(proj, v_gain, w_s, b_s_t, name, tq=256):
    s = proj.shape[0]

    def body(p_ref, vg_ref, ws_ref, bs_ref, y_ref):
        vraw = p_ref[:, A_WIDTH:2 * A_WIDTH].astype(F32)
        r = lax.rsqrt(jnp.mean(vraw * vraw, axis=-1, keepdims=True) + EPS)
        vn = (vraw * r * vg_ref[...]).astype(CDT)
        tri = _tril_mask()
        for g in range(A_GROUPS):
            w = jnp.where(tri, ws_ref[g], 0.0).astype(CDT)
            bias = bs_ref[:, g:g + 1]
            cols = slice(g * A_GROUP_DIM, (g + 1) * A_GROUP_DIM)
            zcols = slice(2 * A_WIDTH + g * A_GROUP_DIM, 2 * A_WIDTH + (g + 1) * A_GROUP_DIM)
            for c in range(tq // CHUNK):
                rows = slice(c * CHUNK, (c + 1) * CHUNK)
                mixed = _dot(w, vn[rows, cols]) + bias
                u = p_ref[rows, cols].astype(F32)
                z = p_ref[rows, zcols].astype(F32)
                y_ref[rows, cols] = (u * mixed * (z * _sigmoid(z))).astype(y_ref.dtype)

    return pl.pallas_call(
        body, name=name, grid=(s // tq,),
        in_specs=[pl.BlockSpec((tq, 3 * A_WIDTH), lambda i: (i, 0)), pl.BlockSpec((1, A_WIDTH), lambda i: (0, 0)),
                  pl.BlockSpec((A_GROUPS, CHUNK, CHUNK), lambda i: (0, 0, 0)), pl.BlockSpec((CHUNK, A_GROUPS), lambda i: (0, 0))],
        out_specs=pl.BlockSpec((tq, A_WIDTH), lambda i: (i, 0)),
        out_shape=jax.ShapeDtypeStruct((s, A_WIDTH), CDT), compiler_params=_cparams("parallel"),
    )(proj, v_gain, w_s, b_s_t)


def _a_mid_bwd(proj, dy, v_gain, w_s, b_s_t, name, tq=256):
    s = proj.shape[0]

    def body(p_ref, dy_ref, vg_ref, ws_ref, bs_ref, dp_ref, dws_ref, dbs_ref, dvg_ref, dvn_ref):
        @pl.when(pl.program_id(0) == 0)
        def _():
            dws_ref[...] = jnp.zeros_like(dws_ref)
            dbs_ref[...] = jnp.zeros_like(dbs_ref)
            dvg_ref[...] = jnp.zeros_like(dvg_ref)

        vraw = p_ref[:, A_WIDTH:2 * A_WIDTH].astype(F32)
        r = lax.rsqrt(jnp.mean(vraw * vraw, axis=-1, keepdims=True) + EPS)
        vhat = vraw * r
        vg = vg_ref[...]
        vn = (vhat * vg).astype(CDT)
        tri = _tril_mask()
        lane = lax.broadcasted_iota(jnp.int32, (CHUNK, A_GROUPS), 1)
        dbs = jnp.zeros((CHUNK, A_GROUPS), F32)
        for g in range(A_GROUPS):
            w = jnp.where(tri, ws_ref[g], 0.0).astype(CDT)
            bias = bs_ref[:, g:g + 1]
            cols = slice(g * A_GROUP_DIM, (g + 1) * A_GROUP_DIM)
            zcols = slice(2 * A_WIDTH + g * A_GROUP_DIM, 2 * A_WIDTH + (g + 1) * A_GROUP_DIM)
            dws = jnp.zeros((CHUNK, CHUNK), F32)
            for c in range(tq // CHUNK):
                rows = slice(c * CHUNK, (c + 1) * CHUNK)
                vn_g = vn[rows, cols]
                mixed = _dot(w, vn_g) + bias
                u = p_ref[rows, cols].astype(F32)
                z = p_ref[rows, zcols].astype(F32)
                dyv = dy_ref[rows, cols].astype(F32)
                sg = _sigmoid(z)
                sz = z * sg
                dyu = dyv * u
                dmixed = dyu * sz
                dp_ref[rows, cols] = (dyv * mixed * sz).astype(dp_ref.dtype)
                dp_ref[rows, zcols] = (dyu * mixed * (sg * (1.0 + z * (1.0 - sg)))).astype(dp_ref.dtype)
                dmc = dmixed.astype(CDT)
                dws = dws + _dot(dmc, vn_g, NT)
                dbs = dbs + jnp.where(lane == g, jnp.sum(dmixed, axis=-1, keepdims=True), 0.0)
                dvn_ref[rows, cols] = _dot(w, dmc, TN)
            dws_ref[g] += jnp.where(tri, dws, 0.0)
        dbs_ref[...] += dbs
        dvn = dvn_ref[...]
        gd = dvn * vg
        dvraw = r * gd - vraw * (r * r * r) * jnp.mean(gd * vraw, axis=-1, keepdims=True)
        dp_ref[:, A_WIDTH:2 * A_WIDTH] = dvraw.astype(dp_ref.dtype)
        dvg_ref[...] += jnp.sum(dvn * vhat, axis=0, keepdims=True)

    return pl.pallas_call(
        body, name=name, grid=(s // tq,),
        in_specs=[pl.BlockSpec((tq, 3 * A_WIDTH), lambda i: (i, 0)), pl.BlockSpec((tq, A_WIDTH), lambda i: (i, 0)),
                  pl.BlockSpec((1, A_WIDTH), lambda i: (0, 0)), pl.BlockSpec((A_GROUPS, CHUNK, CHUNK), lambda i: (0, 0, 0)),
                  pl.BlockSpec((CHUNK, A_GROUPS), lambda i: (0, 0))],
        out_specs=[pl.BlockSpec((tq, 3 * A_WIDTH), lambda i: (i, 0)), pl.BlockSpec((A_GROUPS, CHUNK, CHUNK), lambda i: (0, 0, 0)),
                   pl.BlockSpec((CHUNK, A_GROUPS), lambda i: (0, 0)), pl.BlockSpec((1, A_WIDTH), lambda i: (0, 0))],
        out_shape=[jax.ShapeDtypeStruct((s, 3 * A_WIDTH), CDT), jax.ShapeDtypeStruct((A_GROUPS, CHUNK, CHUNK), F32),
                   jax.ShapeDtypeStruct((CHUNK, A_GROUPS), F32), jax.ShapeDtypeStruct((1, A_WIDTH), F32)],
        scratch_shapes=[pltpu.VMEM((tq, A_WIDTH), F32)],
        compiler_params=_cparams("arbitrary"),
    )(proj, dy, v_gain, w_s, b_s_t)


def _rope_tables(s):
    inv_freq = jnp.power(jnp.float32(ROPE_THETA), -jnp.arange(ROPE_HALF, dtype=F32) / ROPE_HALF)
    ang = jnp.arange(s, dtype=F32)[:, None] * inv_freq[None, :]
    cos, sin = jnp.cos(ang), jnp.sin(ang)
    rest = HEAD_DIM - 2 * ROPE_HALF
    t_c = jnp.concatenate([cos, cos, jnp.ones((s, rest), F32)], axis=1)
    t_a = jnp.concatenate([-sin, jnp.zeros((s, HEAD_DIM - ROPE_HALF), F32)], axis=1)
    t_b = jnp.concatenate([jnp.zeros((s, ROPE_HALF), F32), sin, jnp.zeros((s, rest), F32)], axis=1)
    return t_c, t_a, t_b


def _b_qk_fwd(proj, gains, tabs, name, tq=256):
    s = proj.shape[0]

    def body(p_ref, g_ref, tc_ref, ta_ref, tb_ref, o_ref):
        tc, ta, tb = tc_ref[...], ta_ref[...], tb_ref[...]
        for tg in range(6):
            gain = g_ref[tg:tg + 1, :]
            for h in range(B_HEADS):
                cols = slice(tg * 1024 + h * HEAD_DIM, tg * 1024 + (h + 1) * HEAD_DIM)
                xv = p_ref[:, cols].astype(F32)
                r = lax.rsqrt(jnp.mean(xv * xv, axis=-1, keepdims=True) + EPS)
                xn = xv * r * gain
                y = xn * tc + pltpu.roll(xn, HEAD_DIM - ROPE_HALF, 1) * ta + pltpu.roll(xn, ROPE_HALF, 1) * tb
                o_ref[:, cols] = y.astype(o_ref.dtype)

    tab = pl.BlockSpec((tq, HEAD_DIM), lambda i: (i, 0))
    return pl.pallas_call(
        body, name=name, grid=(s // tq,),
        in_specs=[pl.BlockSpec((tq, B_QK), lambda i: (i, 0)), pl.BlockSpec((6, HEAD_DIM), lambda i: (0, 0)), tab, tab, tab],
        out_specs=pl.BlockSpec((tq, B_QK), lambda i: (i, 0)),
        out_shape=jax.ShapeDtypeStruct((s, B_QK), CDT), compiler_params=_cparams("parallel"),
    )(proj, gains, *tabs)


def _b_attn_fwd(qk, proj, g, dil, name):
    s = qk.shape[0]
    length = s // dil
    nb = length // CHUNK
    scale = 1.0 / math.sqrt(HEAD_DIM)
    w = B_HEADS * HEAD_DIM

    def body(q_ref, kc_ref, kp_ref, vc_ref, vp_ref, o_ref, lse_ref):
        n = pl.program_id(1)
        qi = lax.broadcasted_iota(jnp.int32, (CHUNK, 2 * CHUNK), 0)
        ki = lax.broadcasted_iota(jnp.int32, (CHUNK, 2 * CHUNK), 1)
        first_key = jnp.where(n > 0, 0, CHUNK)
        mask = (ki >= qi) & (ki <= qi + CHUNK) & (ki >= first_key)
        lane = lax.broadcasted_iota(jnp.int32, (CHUNK, HEAD_DIM), 1)
        lse_all = jnp.zeros((CHUNK, HEAD_DIM), F32)
        for h in range(B_HEADS):
            sl = slice(h * HEAD_DIM, (h + 1) * HEAD_DIM)
            k2 = jnp.concatenate([kp_ref[:, sl], kc_ref[:, sl]], axis=0)
            v2 = jnp.concatenate([vp_ref[:, sl], vc_ref[:, sl]], axis=0)
            sc = jnp.where(mask, _dot(q_ref[:, sl], k2, NT) * scale, -1e30)
            m = jnp.max(sc, axis=-1, keepdims=True)
            p = jnp.exp(sc - m)
            l = jnp.sum(p, axis=-1, keepdims=True)
            o_ref[:, sl] = _dot(p.astype(CDT), v2) / l
            lse_all = jnp.where(lane == h, m + jnp.log(l), lse_all)
        lse_ref[...] = lse_all

    prev = lambda n: jnp.maximum(n - 1, 0)
    blk = lambda f: pl.BlockSpec((CHUNK, w), f)
    o, lse = pl.pallas_call(
        body, name=name, grid=(dil, nb),
        in_specs=[blk(lambda r, n: (n, r * 6 + g)), blk(lambda r, n: (n, r * 6 + 3 + g)), blk(lambda r, n: (prev(n), r * 6 + 3 + g)),
                  blk(lambda r, n: (n, r * 10 + 6 + g)), blk(lambda r, n: (prev(n), r * 10 + 6 + g))],
        out_specs=[blk(lambda r, n: (n, r)), pl.BlockSpec((CHUNK, HEAD_DIM), lambda r, n: (n, r))],
        out_shape=[jax.ShapeDtypeStruct((length, dil * w), F32), jax.ShapeDtypeStruct((length, dil * HEAD_DIM), F32)],
        compiler_params=_cparams("parallel", "parallel"),
    )(qk.reshape(length, dil * B_QK), qk.reshape(length, dil * B_QK), qk.reshape(length, dil * B_QK),
      proj.reshape(length, dil * B_IN), proj.reshape(length, dil * B_IN))
    return o.reshape(s, w), lse.reshape(s, HEAD_DIM)


def _b_combine(os_, lses, proj, name, tq=512):
    s = proj.shape[0]
    w = B_HEADS * HEAD_DIM

    def body(o0_ref, o1_ref, o2_ref, l0_ref, l1_ref, l2_ref, z_ref, y_ref, oj_ref, lj_ref):
        l0, l1, l2 = l0_ref[...], l1_ref[...], l2_ref[...]
        m = jnp.maximum(jnp.maximum(l0, l1), l2)
        lj = m + jnp.log(jnp.exp(l0 - m) + jnp.exp(l1 - m) + jnp.exp(l2 - m))
        lj_ref[...] = lj
        w0, w1, w2 = jnp.exp(l0 - lj), jnp.exp(l1 - lj), jnp.exp(l2 - lj)
        for h in range(B_HEADS):
            sl = slice(h * HEAD_DIM, (h + 1) * HEAD_DIM)
            o = w0[:, h:h + 1] * o0_ref[:, sl] + w1[:, h:h + 1] * o1_ref[:, sl] + w2[:, h:h + 1] * o2_ref[:, sl]
            z = z_ref[:, sl].astype(F32)
            oj_ref[:, sl] = o.astype(oj_ref.dtype)
            y_ref[:, sl] = (o * (z * _sigmoid(z))).astype(y_ref.dtype)

    blk = pl.BlockSpec((tq, w), lambda i: (i, 0))
    st = pl.BlockSpec((tq, HEAD_DIM), lambda i: (i, 0))
    return pl.pallas_call(
        body, name=name, grid=(s // tq,),
        in_specs=[blk, blk, blk, st, st, st, pl.BlockSpec((tq, w), lambda i: (i, 9))],
        out_specs=[blk, blk, st],
        out_shape=[jax.ShapeDtypeStruct((s, w), CDT), jax.ShapeDtypeStruct((s, w), CDT), jax.ShapeDtypeStruct((s, HEAD_DIM), F32)],
        compiler_params=_cparams("parallel"),
    )(*os_, *lses, proj)


def _b_bwd_prep(dy, oj, proj, name, tq=512):
    s = proj.shape[0]
    w = B_HEADS * HEAD_DIM

    def body(dy_ref, oj_ref, z_ref, do_ref, dz_ref, dd_ref):
        lane = lax.broadcasted_iota(jnp.int32, (tq, HEAD_DIM), 1)
        dd = jnp.zeros((tq, HEAD_DIM), F32)
        for h in range(B_HEADS):
            sl = slice(h * HEAD_DIM, (h + 1) * HEAD_DIM)
            z = z_ref[:, sl].astype(F32)
            dyv = dy_ref[:, sl].astype(F32)
            o = oj_ref[:, sl].astype(F32)
            sg = _sigmoid(z)
            do = dyv * (z * sg)
            do_ref[:, sl] = do.astype(do_ref.dtype)
            dz_ref[:, sl] = (dyv * o * (sg * (1.0 + z * (1.0 - sg)))).astype(dz_ref.dtype)
            dd = jnp.where(lane == h, jnp.sum(do * o, axis=-1, keepdims=True), dd)
        dd_ref[...] = dd

    blk = pl.BlockSpec((tq, w), lambda i: (i, 0))
    st = pl.BlockSpec((tq, HEAD_DIM), lambda i: (i, 0))
    return pl.pallas_call(
        body, name=name, grid=(s // tq,),
        in_specs=[blk, blk, pl.BlockSpec((tq, w), lambda i: (i, 9))], out_specs=[blk, blk, st],
        out_shape=[jax.ShapeDtypeStruct((s, w), CDT), jax.ShapeDtypeStruct((s, w), CDT), jax.ShapeDtypeStruct((s, HEAD_DIM), F32)],
        compiler_params=_cparams("parallel"),
    )(dy, oj, proj)


def _b_attn_bwd(qk, proj, do, lj, dd, g, dil, name):
    s = qk.shape[0]
    length = s // dil
    nb = length // CHUNK
    scale = 1.0 / math.sqrt(HEAD_DIM)
    w = B_HEADS * HEAD_DIM

    def body(qj_ref, qn_ref, k_ref, v_ref, doj_ref, don_ref, lj_ref, ln_ref, dj_ref, dn_ref, out_ref, carry_ref):
        j = pl.program_id(1)

        @pl.when(j == 0)
        def _():
            carry_ref[...] = jnp.zeros_like(carry_ref)

        qi = lax.broadcasted_iota(jnp.int32, (2 * CHUNK, CHUNK), 0)
        ki = lax.broadcasted_iota(jnp.int32, (2 * CHUNK, CHUNK), 1)
        no_next = jnp.where(j + 1 < nb, 0, 2 * CHUNK)
        mask = ((qi < CHUNK) & (ki <= qi)) | ((qi >= CHUNK) & (ki >= qi - CHUNK + no_next))
        for h in range(B_HEADS):
            sl = slice(h * HEAD_DIM, (h + 1) * HEAD_DIM)
            q2 = jnp.concatenate([qj_ref[:, sl], qn_ref[:, sl]], axis=0)
            do2 = jnp.concatenate([doj_ref[:, sl], don_ref[:, sl]], axis=0)
            lse2 = jnp.concatenate([lj_ref[:, h:h + 1], ln_ref[:, h:h + 1]], axis=0)
            d2 = jnp.concatenate([dj_ref[:, h:h + 1], dn_ref[:, h:h + 1]], axis=0)
            k = k_ref[:, sl]
            v = v_ref[:, sl]
            sc = _dot(q2, k, NT) * scale
            p = jnp.where(mask, jnp.exp(sc - lse2), 0.0)
            dp = _dot(do2, v, NT)
            ds = (p * (dp - d2) * scale).astype(CDT)
            dq2 = _dot(ds, k)
            out_ref[:, sl] = (carry_ref[:, sl] + dq2[:CHUNK]).astype(out_ref.dtype)
            carry_ref[:, sl] = dq2[CHUNK:]
            out_ref[:, w + h * HEAD_DIM:w + (h + 1) * HEAD_DIM] = _dot(ds, q2, TN).astype(out_ref.dtype)
            out_ref[:, 2 * w + h * HEAD_DIM:2 * w + (h + 1) * HEAD_DIM] = _dot(p.astype(CDT), do2, TN).astype(out_ref.dtype)

    nxt = lambda j: jnp.minimum(j + 1, nb - 1)
    blk = lambda f: pl.BlockSpec((CHUNK, w), f)
    st = lambda f: pl.BlockSpec((CHUNK, HEAD_DIM), f)
    qk_v = qk.reshape(length, dil * B_QK)
    pj_v = proj.reshape(length, dil * B_IN)
    do_v = do.reshape(length, dil * w)
    lj_v = lj.reshape(length, dil * HEAD_DIM)
    dd_v = dd.reshape(length, dil * HEAD_DIM)
    out = pl.pallas_call(
        body, name=name, grid=(dil, nb),
        in_specs=[blk(lambda r, j: (j, r * 6 + g)), blk(lambda r, j: (nxt(j), r * 6 + g)),
                  blk(lambda r, j: (j, r * 6 + 3 + g)), blk(lambda r, j: (j, r * 10 + 6 + g)),
                  blk(lambda r, j: (j, r)), blk(lambda r, j: (nxt(j), r)),
                  st(lambda r, j: (j, r)), st(lambda r, j: (nxt(j), r)), st(lambda r, j: (j, r)), st(lambda r, j: (nxt(j), r))],
        out_specs=pl.BlockSpec((CHUNK, 3 * w), lambda r, j: (j, r)),
        out_shape=jax.ShapeDtypeStruct((length, dil * 3 * w), CDT),
        scratch_shapes=[pltpu.VMEM((CHUNK, w), F32)],
        compiler_params=_cparams("parallel", "arbitrary"),
    )(qk_v, qk_v, qk_v, pj_v, do_v, do_v, lj_v, lj_v, dd_v, dd_v)
    return out.reshape(s, 3 * w)


def _b_qk_bwd(proj, dqkv, dz, gains, tabs, name, tq=256):
    s = proj.shape[0]
    w = B_HEADS * HEAD_DIM

    def body(p_ref, d0_ref, d1_ref, d2_ref, dz_ref, g_ref, tc_ref, ta_ref, tb_ref, dp_ref, dg_ref):
        @pl.when(pl.program_id(0) == 0)
        def _():
            dg_ref[...] = jnp.zeros_like(dg_ref)

        tc, ta, tb = tc_ref[...], ta_ref[...], tb_ref[...]
        d_refs = (d0_ref, d1_ref, d2_ref)
        for g in range(3):
            for t in range(2):
                tg = t * 3 + g
                gain = g_ref[tg:tg + 1, :]
                dgain = jnp.zeros((1, HEAD_DIM), F32)
                for h in range(B_HEADS):
                    cols = slice(tg * w + h * HEAD_DIM, tg * w + (h + 1) * HEAD_DIM)
                    xv = p_ref[:, cols].astype(F32)
                    r = lax.rsqrt(jnp.mean(xv * xv, axis=-1, keepdims=True) + EPS)
                    dyv = d_refs[g][:, t * w + h * HEAD_DIM:t * w + (h + 1) * HEAD_DIM].astype(F32)
                    dxn = dyv * tc + pltpu.roll(dyv * ta, ROPE_HALF, 1) + pltpu.roll(dyv * tb, HEAD_DIM - ROPE_HALF, 1)
                    gd = dxn * gain
                    dx = r * gd - xv * (r * r * r) * jnp.mean(gd * xv, axis=-1, keepdims=True)
                    dp_ref[:, cols] = dx.astype(dp_ref.dtype)
                    dgain = dgain + jnp.sum(dxn * xv * r, axis=0, keepdims=True)
                dg_ref[tg:tg + 1, :] += dgain
            dp_ref[:, (6 + g) * w:(7 + g) * w] = d_refs[g][:, 2 * w:3 * w]
        dp_ref[:, 9 * w:10 * w] = dz_ref[...]

    tab = pl.BlockSpec((tq, HEAD_DIM), lambda i: (i, 0))
    dblk = pl.BlockSpec((tq, 3 * w), lambda i: (i, 0))
    return pl.pallas_call(
        body, name=name, grid=(s // tq,),
        in_specs=[pl.BlockSpec((tq, B_QK), lambda i: (i, 0)), dblk, dblk, dblk, pl.BlockSpec((tq, w), lambda i: (i, 0)),
                  pl.BlockSpec((6, HEAD_DIM), lambda i: (0, 0)), tab, tab, tab],
        out_specs=[pl.BlockSpec((tq, B_IN), lambda i: (i, 0)), pl.BlockSpec((6, HEAD_DIM), lambda i: (0, 0))],
        out_shape=[jax.ShapeDtypeStruct((s, B_IN), CDT), jax.ShapeDtypeStruct((6, HEAD_DIM), F32)],
        compiler_params=_cparams("arbitrary"),
    )(proj, *dqkv, dz, gains, *tabs)


def _inv_count(t, window):
    return 1.0 / jnp.minimum(t + 1, window).astype(F32)


def _c_pool_fwd(proj, name, tq=512):
    s = proj.shape[0]
    per = tq // POOL_HALO

    def body(x_ref, halo_ref, o_ref, ext_ref):
        i = pl.program_id(0)
        xv = x_ref[...].astype(F32)
        ext_ref[POOL_HALO:, :] = xv
        ext_ref[:POOL_HALO, :] = jnp.where(i > 0, halo_ref[...].astype(F32), 0.0)
        t = i * tq + lax.broadcasted_iota(jnp.int32, (tq, 1), 0)
        for g, window in enumerate(POOL_SIZES):
            cols = slice(g * C_GROUP, (g + 1) * C_GROUP)
            acc = xv[:, cols]
            for back in range(1, window):
                acc = acc + ext_ref[POOL_HALO - back:POOL_HALO - back + tq, cols]
            o_ref[:, cols] = (acc * _inv_count(t, window) - xv[:, cols]).astype(o_ref.dtype)

    return pl.pallas_call(
        body, name=name, grid=(s // tq,),
        in_specs=[pl.BlockSpec((tq, C_WIDTH), lambda i: (i, 0)),
                  pl.BlockSpec((POOL_HALO, C_WIDTH), lambda i: (jnp.maximum(i * per - 1, 0), 0))],
        out_specs=pl.BlockSpec((tq, C_WIDTH), lambda i: (i, 0)),
        out_shape=jax.ShapeDtypeStruct((s, C_WIDTH), CDT),
        scratch_shapes=[pltpu.VMEM((tq + POOL_HALO, C_WIDTH), F32)],
        compiler_params=_cparams("parallel"),
    )(proj, proj)


def _c_mid_fwd(diff, proj, w_grp, scale, name, tq=512):
    s = diff.shape[0]

    def body(d_ref, z_ref, w_ref, sc_ref, y_ref):
        for g in range(len(POOL_SIZES)):
            cols = slice(g * C_GROUP, (g + 1) * C_GROUP)
            z = z_ref[:, cols].astype(F32)
            y_ref[:, cols] = (_dot(d_ref[:, cols], w_ref[g]) * sc_ref[:, cols] * (z * _sigmoid(z))).astype(y_ref.dtype)

    return pl.pallas_call(
        body, name=name, grid=(s // tq,),
        in_specs=[pl.BlockSpec((tq, C_WIDTH), lambda i: (i, 0)), pl.BlockSpec((tq, C_WIDTH), lambda i: (i, 1)),
                  pl.BlockSpec((4, C_GROUP, C_GROUP), lambda i: (0, 0, 0)), pl.BlockSpec((1, C_WIDTH), lambda i: (0, 0))],
        out_specs=pl.BlockSpec((tq, C_WIDTH), lambda i: (i, 0)),
        out_shape=jax.ShapeDtypeStruct((s, C_WIDTH), CDT), compiler_params=_cparams("parallel"),
    )(diff, proj, w_grp, scale)


def _c_mid_bwd(diff, proj, w_grp, scale, dy, name, tq=512):
    s = diff.shape[0]

    def body(d_ref, z_ref, w_ref, sc_ref, dy_ref, dd_ref, dz_ref, dw_ref, dsc_ref):
        @pl.when(pl.program_id(0) == 0)
        def _():
            dw_ref[...] = jnp.zeros_like(dw_ref)
            dsc_ref[...] = jnp.zeros_like(dsc_ref)

        for g in range(len(POOL_SIZES)):
            cols = slice(g * C_GROUP, (g + 1) * C_GROUP)
            d = d_ref[:, cols]
            m0 = _dot(d, w_ref[g])
            z = z_ref[:, cols].astype(F32)
            dyv = dy_ref[:, cols].astype(F32)
            sc = sc_ref[:, cols]
            sg = _sigmoid(z)
            dmixed = dyv * (z * sg)
            dz_ref[:, cols] = (dyv * m0 * sc * (sg * (1.0 + z * (1.0 - sg)))).astype(dz_ref.dtype)
            dsc_ref[:, cols] += jnp.sum(dmixed * m0, axis=0, keepdims=True)
            dm0 = (dmixed * sc).astype(CDT)
            dw_ref[g] += _dot(d, dm0, TN)
            dd_ref[:, cols] = _dot(dm0, w_ref[g], NT)

    blk = pl.BlockSpec((tq, C_WIDTH), lambda i: (i, 0))
    wsp = pl.BlockSpec((4, C_GROUP, C_GROUP), lambda i: (0, 0, 0))
    vec = pl.BlockSpec((1, C_WIDTH), lambda i: (0, 0))
    return pl.pallas_call(
        body, name=name, grid=(s // tq,),
        in_specs=[blk, pl.BlockSpec((tq, C_WIDTH), lambda i: (i, 1)), wsp, vec, blk],
        out_specs=[blk, blk, wsp, vec],
        out_shape=[jax.ShapeDtypeStruct((s, C_WIDTH), F32), jax.ShapeDtypeStruct((s, C_WIDTH), CDT),
                   jax.ShapeDtypeStruct((4, C_GROUP, C_GROUP), F32), jax.ShapeDtypeStruct((1, C_WIDTH), F32)],
        compiler_params=_cparams("arbitrary"),
    )(diff, proj, w_grp, scale, dy)


def _c_pool_bwd(ddiff, dz, name, tq=512):
    s = ddiff.shape[0]
    per = tq // POOL_HALO
    last = s // tq - 1

    def body(d_ref, halo_ref, dz_ref, o_ref, ext_ref):
        i = pl.program_id(0)
        dv = d_ref[...]
        t = i * tq + lax.broadcasted_iota(jnp.int32, (tq, 1), 0)
        for g, window in enumerate(POOL_SIZES):
            cols = slice(g * C_GROUP, (g + 1) * C_GROUP)
            ext_ref[:tq, cols] = dv[:, cols] * _inv_count(t, window)
            ext_ref[tq:, cols] = jnp.where(i < last, halo_ref[:, cols] * (1.0 / window), 0.0)
            acc = -dv[:, cols]
            for fwd in range(window):
                acc = acc + ext_ref[fwd:fwd + tq, cols]
            o_ref[:, cols] = acc.astype(o_ref.dtype)
        o_ref[:, C_WIDTH:] = dz_ref[...]

    return pl.pallas_call(
        body, name=name, grid=(s // tq,),
        in_specs=[pl.BlockSpec((tq, C_WIDTH), lambda i: (i, 0)),
                  pl.BlockSpec((POOL_HALO, C_WIDTH), lambda i: (jnp.minimum((i + 1) * per, s // POOL_HALO - 1), 0)),
                  pl.BlockSpec((tq, C_WIDTH), lambda i: (i, 0))],
        out_specs=pl.BlockSpec((tq, 2 * C_WIDTH), lambda i: (i, 0)),
        out_shape=jax.ShapeDtypeStruct((s, 2 * C_WIDTH), CDT),
        scratch_shapes=[pltpu.VMEM((tq + POOL_HALO, C_WIDTH), F32)],
        compiler_params=_cparams("parallel"),
    )(ddiff, ddiff, dz)


LAYERS = (("a", 0), ("b", 0), ("c", 0), ("a", 1))


def _local_step(x, target, w):
    s = x.shape[0]
    tabs = _rope_tables(s)
    qk_gains = jnp.concatenate([w["b_q_gain"][0], w["b_k_gain"][0]], axis=0)
    saved = []
    for li, (kind, j) in enumerate(LAYERS):
        h = _rms_fwd(x, w["norm_gain"][li:li + 1], f"rms_fwd{li}")
        if kind == "a":
            proj = _mm(h, w["a_w_in"][j], "nn", CDT, f"a_in{li}")
            bs_t = w["a_b_s"][j].T
            y = _a_mid_fwd(proj, w["a_v_gain"][j:j + 1], w["a_w_s"][j], bs_t, f"a_mid_fwd{li}")
            x_next = _mm(y, w["a_w_out"][j], "nn", F32, f"a_out{li}", tm=512, tk=2048, add=x)
            saved.append((x, h, proj, y))
        elif kind == "b":
            proj = _mm(h, w["b_w_in"][j], "nn", CDT, f"b_in{li}")
            qk = _b_qk_fwd(proj, qk_gains, tabs, f"b_qk_fwd{li}")
            outs = [_b_attn_fwd(qk, proj, g, dil, f"b_attn_fwd{li}_{g}") for g, dil in enumerate(B_DILATIONS)]
            y, oj, lj = _b_combine([o for o, _ in outs], [l for _, l in outs], proj, f"b_combine{li}")
            x_next = _mm(y, w["b_w_out"][j], "nn", F32, f"b_out{li}", tm=512, add=x)
            saved.append((x, h, proj, y, qk, oj, lj))
        else:
            proj = _mm(h, w["c_w_in"][j], "nn", CDT, f"c_in{li}")
            diff = _c_pool_fwd(proj, f"c_pool_fwd{li}")
            y = _c_mid_fwd(diff, proj, w["c_w_grp"][j], w["c_scale"][j:j + 1], f"c_mid_fwd{li}")
            x_next = _mm(y, w["c_w_out"][j], "nn", F32, f"c_out{li}", tm=512, tk=2048, add=x)
            saved.append((x, h, proj, y, diff))
        x = x_next

    dx, dxc, sq = _loss_bwd(x, target, "loss_bwd")
    grads = {"norm_gain": [None] * len(LAYERS), "a_w_in": [None, None], "a_v_gain": [None, None], "a_w_s": [None, None],
             "a_b_s": [None, None], "a_w_out": [None, None]}
    for li in reversed(range(len(LAYERS))):
        kind, j = LAYERS[li]
        sv = saved[li]
        xin, h, proj, y = sv[:4]
        if kind == "a":
            grads["a_w_out"][j] = _mm(y, dxc, "tn", CDT, f"a_dwout{li}")
            dy = _mm(dxc, w["a_w_out"][j], "nt", CDT, f"a_dy{li}")
            dproj, dws, dbs_t, dvg = _a_mid_bwd(proj, dy, w["a_v_gain"][j:j + 1], w["a_w_s"][j], w["a_b_s"][j].T, f"a_mid_bwd{li}")
            grads["a_w_s"][j], grads["a_b_s"][j], grads["a_v_gain"][j] = dws, dbs_t.T, dvg[0]
            grads["a_w_in"][j] = _mm(h, dproj, "tn", CDT, f"a_dwin{li}")
            dh = _mm(dproj, w["a_w_in"][j], "nt", F32, f"a_dh{li}")
        elif kind == "b":
            qk, oj, lj = sv[4:]
            grads["b_w_out"] = _mm(y, dxc, "tn", CDT, f"b_dwout{li}")[None]
            dy = _mm(dxc, w["b_w_out"][j], "nt", CDT, f"b_dy{li}")
            do, dz, dd = _b_bwd_prep(dy, oj, proj, f"b_bwd_prep{li}")
            dqkv = [_b_attn_bwd(qk, proj, do, lj, dd, g, dil, f"b_attn_bwd{li}_{g}") for g, dil in enumerate(B_DILATIONS)]
            dproj, dgains = _b_qk_bwd(proj, dqkv, dz, qk_gains, tabs, f"b_qk_bwd{li}")
            grads["b_q_gain"], grads["b_k_gain"] = dgains[None, :3], dgains[None, 3:]
            grads["b_w_in"] = _mm(h, dproj, "tn", CDT, f"b_dwin{li}")[None]
            dh = _mm(dproj, w["b_w_in"][j], "nt", F32, f"b_dh{li}")
        else:
            diff = sv[4]
            grads["c_w_out"] = _mm(y, dxc, "tn", CDT, f"c_dwout{li}")[None]
            dy = _mm(dxc, w["c_w_out"][j], "nt", CDT, f"c_dy{li}")
            ddiff, dz, dwg, dsc = _c_mid_bwd(diff, proj, w["c_w_grp"][j], w["c_scale"][j:j + 1], dy, f"c_mid_bwd{li}")
            grads["c_w_grp"], grads["c_scale"] = dwg[None], dsc
            dproj = _c_pool_bwd(ddiff, dz, f"c_pool_bwd{li}")
            grads["c_w_in"] = _mm(h, dproj, "tn", CDT, f"c_dwin{li}")[None]
            dh = _mm(dproj, w["c_w_in"][j], "nt", F32, f"c_dh{li}")
        dx, dxc, dng = _rms_bwd(xin, dh, dx, w["norm_gain"][li:li + 1], f"rms_bwd{li}")
        grads["norm_gain"][li] = dng[0]
    for name in ("norm_gain", "a_w_in", "a_v_gain", "a_w_s", "a_b_s", "a_w_out"):
        grads[name] = jnp.stack(grads[name])
    return sq, dx, grads


ROW = 1024
HALVES = ((("a_w_in", 3072), ("a_w_out", 1024), ("b_w_out", 256)), (("b_w_in", 2560), ("c_w_in", 1024), ("c_w_grp", 256), ("c_w_out", 512)))
SMALL_OF_HALF = ("a_v_gain", "c_scale")
BIG_HALF = 4352
SMALL_PAD = 16
W_HALF = BIG_HALF + SMALL_PAD
W_ROWS = 2 * W_HALF
REP = (("norm_gain", (4, 1024)), ("a_w_s", (2, 8, 128, 128)), ("a_b_s", (2, 8, 128)), ("b_q_gain", (1, 3, 128)), ("b_k_gain", (1, 3, 128)))
REP_CORE = 48
REP_PART = 2 * REP_CORE
G_SMALL, G_REP = BIG_HALF, BIG_HALF + SMALL_PAD
G_HALF = G_REP + REP_CORE
G_ROWS = 2 * G_HALF
SHARDED = ("a_w_in", "a_v_gain", "a_w_out", "b_w_in", "b_w_out", "c_w_in", "c_w_grp", "c_scale", "c_w_out")
BLOCK_SHAPES = {"a_w_in": (2, 1024, 1536), "a_v_gain": (2, 512), "a_w_out": (2, 512, 1024), "b_w_in": (1, 1024, 2560),
                "b_w_out": (1, 256, 1024), "c_w_in": (1, 1024, 1024), "c_w_grp": (1, 4, 128, 512), "c_scale": (1, 512),
                "c_w_out": (1, 512, 1024)}
SHARD_AXIS = {"a_w_in": 2, "a_v_gain": 1, "a_w_out": 1, "b_w_in": 2, "b_w_out": 1, "c_w_in": 2, "c_w_grp": 2, "c_scale": 1, "c_w_out": 1}
WEIGHTS = ("norm_gain", "a_w_in", "a_v_gain", "a_w_s", "a_b_s", "a_w_out", "b_w_in", "b_q_gain", "b_k_gain", "b_w_out",
           "c_w_in", "c_w_grp", "c_scale", "c_w_out")


def _rows(a, rows):
    a = a.reshape(-1)
    return jnp.pad(a, (0, rows * ROW - a.shape[0])).reshape(rows, ROW)


def _pack_blocks(blocks, rep_part, dtype):
    parts = []
    for h in range(2):
        parts += [blocks[n].astype(dtype).reshape(r, ROW) for n, r in HALVES[h]]
        parts += [_rows(blocks[SMALL_OF_HALF[h]].astype(dtype), SMALL_PAD), rep_part[h * REP_CORE:(h + 1) * REP_CORE].astype(dtype)]
    return jnp.concatenate(parts, axis=0)


def _unpack_blocks(lo, hi):
    out = {}
    for h, slab in enumerate((lo, hi)):
        off = 0
        for n, r in HALVES[h]:
            out[n] = slab[off:off + r].reshape(BLOCK_SHAPES[n])
            off += r
    out["a_v_gain"] = lo[G_SMALL].reshape(2, 512)
    out["c_scale"] = hi[G_SMALL, :512].reshape(1, 512)
    return out


def _pack_rep(tree):
    return _rows(jnp.concatenate([tree[n].astype(F32).reshape(-1) for n, _ in REP]), N_CHIPS * REP_PART)


def _unpack_rep(slab):
    flat, out, off = slab.reshape(-1), {}, 0
    for n, shape in REP:
        size = math.prod(shape)
        out[n] = flat[off:off + size].reshape(shape)
        off += size
    return out


def _shard(full, name, k):
    axis, size = SHARD_AXIS[name], BLOCK_SHAPES[name][SHARD_AXIS[name]]
    return lax.slice_in_dim(full, k * size, (k + 1) * size, axis=axis)


def _pack_weights(blocks):
    gains = jnp.concatenate([blocks["a_v_gain"].reshape(-1), blocks["c_scale"].reshape(-1)])
    parts = [blocks[n].astype(CDT).reshape(r, ROW) for n, r in HALVES[0]]
    parts.append(_rows(lax.bitcast_convert_type(gains, CDT), SMALL_PAD))
    parts += [blocks[n].astype(CDT).reshape(r, ROW) for n, r in HALVES[1]]
    parts.append(jnp.zeros((SMALL_PAD, ROW), CDT))
    return jnp.concatenate(parts, axis=0)


def _unpack_weights(gathered):
    out = {}
    for h in range(2):
        off = h * W_HALF
        for n, r in HALVES[h]:
            out[n] = jnp.concatenate([gathered[k, off:off + r].reshape(BLOCK_SHAPES[n]) for k in range(N_CHIPS)], axis=SHARD_AXIS[n])
            off += r
    gains = lax.bitcast_convert_type(gathered[:, BIG_HALF:BIG_HALF + 3].reshape(N_CHIPS, 1536, 2), F32)
    out["a_v_gain"] = jnp.concatenate([gains[k, :1024].reshape(2, 512) for k in range(N_CHIPS)], axis=1)
    out["c_scale"] = jnp.concatenate([gains[k, 1024:].reshape(1, 512) for k in range(N_CHIPS)], axis=1)
    return out


def _place():
    x, y, c = lax.axis_index("x"), lax.axis_index("y"), lax.axis_index("c")
    chips = [(1 - x, y), (x, 1 - y), (1 - x, 1 - y)]
    return x, y, c, 2 * x + y, (x, y, 1 - c), chips


def _remote(src, dst, sems, j, to):
    send_sems, recv_sems = sems
    return pltpu.make_async_remote_copy(src_ref=src, dst_ref=dst, send_sem=send_sems.at[j], recv_sem=recv_sems.at[j],
                                        device_id=to, device_id_type=MESH)


def _comm_call(body, name, out_shape, n_sems, *args):
    return pl.pallas_call(
        body, name=name, in_specs=[ANY] * len(args), out_specs=ANY, out_shape=out_shape,
        scratch_shapes=[pltpu.SemaphoreType.DMA((n_sems,)), pltpu.SemaphoreType.DMA((n_sems,))],
    )(*args)


def _allgather_weights(wp):
    half = W_HALF

    def body(w_ref, o_ref, send_sems, recv_sems):
        x, y, c, me, sibling, chips = _place()
        sems = (send_sems, recv_sems)
        mine = pl.ds(pl.multiple_of(c * half, 16), half)
        other = pl.ds(pl.multiple_of((1 - c) * half, 16), half)
        first = [_remote(w_ref.at[mine], o_ref.at[me, mine], sems, j, (*chip, c)) for j, chip in enumerate(chips)]
        for cp in first:
            cp.start()
        passed = []
        for j, (cx, cy) in enumerate(chips):
            landed = o_ref.at[2 * cx + cy, mine]
            _remote(landed, landed, sems, j, sibling).wait_recv()
            passed.append(_remote(landed, landed, sems, 3 + j, sibling))
            passed[-1].start()
        for j, (cx, cy) in enumerate(chips):
            landed = o_ref.at[2 * cx + cy, other]
            _remote(landed, landed, sems, 3 + j, sibling).wait_recv()
        for cp in first + passed:
            cp.wait_send()

    return _comm_call(body, "allgather_weights", jax.ShapeDtypeStruct((N_CHIPS, W_ROWS, ROW), wp.dtype), 6, wp)


def _swap_halves(gp):
    def body(g_ref, o_ref, send_sems, recv_sems):
        x, y, c, me, sibling, chips = _place()
        theirs = g_ref.at[:, pl.ds(pl.multiple_of((1 - c) * G_HALF, 16), G_HALF), :]
        cp = _remote(theirs, o_ref, (send_sems, recv_sems), 0, sibling)
        cp.start()
        cp.wait()

    return _comm_call(body, "grad_swap_halves", jax.ShapeDtypeStruct((N_CHIPS, G_HALF, ROW), gp.dtype), 1, gp)


def _scatter_shards(s1):
    def body(s_ref, o_ref, send_sems, recv_sems):
        x, y, c, me, sibling, chips = _place()
        sems = (send_sems, recv_sems)
        sent = [_remote(s_ref.at[2 * cx + cy], o_ref.at[me], sems, j, (cx, cy, c)) for j, (cx, cy) in enumerate(chips)]
        for cp in sent:
            cp.start()
        for j, (cx, cy) in enumerate(chips):
            slot = o_ref.at[2 * cx + cy]
            _remote(slot, slot, sems, j, sibling).wait_recv()
        for cp in sent:
            cp.wait_send()

    return _comm_call(body, "grad_scatter_shards", jax.ShapeDtypeStruct((N_CHIPS, G_HALF, ROW), s1.dtype), 3, s1)


def _share_halves(out4):
    def body(h_ref, o_ref, send_sems, recv_sems):
        x, y, c, me, sibling, chips = _place()
        cp = _remote(h_ref, o_ref, (send_sems, recv_sems), 0, sibling)
        cp.start()
        cp.wait()

    return _comm_call(body, "share_halves", jax.ShapeDtypeStruct(out4.shape, out4.dtype), 1, out4)


def _allgather_rep(rep4):
    def body(r_ref, o_ref, send_sems, recv_sems):
        x, y, c, me, sibling, chips = _place()
        sems = (send_sems, recv_sems)
        sent = [_remote(r_ref, o_ref.at[me], sems, j, (*chip, c)) for j, chip in enumerate(chips)]
        for cp in sent:
            cp.start()
        for j, (cx, cy) in enumerate(chips):
            slot = o_ref.at[2 * cx + cy]
            _remote(slot, slot, sems, j, sibling).wait_recv()
        for cp in sent:
            cp.wait_send()

    return _comm_call(body, "allgather_rep", jax.ShapeDtypeStruct((N_CHIPS,) + rep4.shape, rep4.dtype), 3, rep4)


ADAM_TILE = 192


def _add_halves(a, b):
    def body(a_ref, b_ref, o_ref):
        o_ref[...] = (a_ref[...].astype(F32) + b_ref[...].astype(F32)).astype(o_ref.dtype)

    blk = pl.BlockSpec((1, ADAM_TILE, ROW), lambda k, i: (k, i, 0))
    return pl.pallas_call(
        body, name="grad_add_halves", grid=(N_CHIPS, G_HALF // ADAM_TILE), in_specs=[blk, blk], out_specs=blk,
        out_shape=jax.ShapeDtypeStruct(a.shape, a.dtype), compiler_params=_cparams("parallel", "parallel"),
    )(a, b)


def _adamw(me, parts, own, w, m, v):
    def body(me_ref, p_ref, own_ref, w_ref, m_ref, v_ref, o_ref):
        g = jnp.zeros((ADAM_TILE, ROW), F32)
        for k in range(N_CHIPS):
            g = g + jnp.where(me_ref[0] == k, own_ref[0], p_ref[k]).astype(F32)
        m2 = ADAM_B1 * m_ref[...] + (1.0 - ADAM_B1) * g
        v2 = ADAM_B2 * v_ref[...] + (1.0 - ADAM_B2) * jnp.square(g)
        m_hat = m2 / (1.0 - ADAM_B1 ** ADAM_STEP)
        v_hat = v2 / (1.0 - ADAM_B2 ** ADAM_STEP)
        o_ref[0] = g
        o_ref[1] = -ADAM_LR * (m_hat / (jnp.sqrt(v_hat) + ADAM_EPS) + ADAM_WD * w_ref[...])
        o_ref[2] = m2
        o_ref[3] = v2

    blk = pl.BlockSpec((ADAM_TILE, ROW), lambda i, me_ref: (i, 0))
    blk4 = pl.BlockSpec((4, ADAM_TILE, ROW), lambda i, me_ref: (0, i, 0))
    mine = pl.BlockSpec((1, ADAM_TILE, ROW), lambda i, me_ref: (me_ref[0], i, 0))
    return pl.pallas_call(
        body, name="adamw",
        grid_spec=pltpu.PrefetchScalarGridSpec(num_scalar_prefetch=1, grid=(G_HALF // ADAM_TILE,),
                                               in_specs=[blk4, mine, blk, blk, blk], out_specs=blk4),
        out_shape=jax.ShapeDtypeStruct((4, G_HALF, ROW), F32), compiler_params=_cparams("parallel"),
    )(me.reshape(1).astype(jnp.int32), parts, own, w, m, v)


def kernel(x, norm_gain, a_w_in, a_v_gain, a_w_s, a_b_s, a_w_out, b_w_in, b_q_gain, b_k_gain, b_w_out, c_w_in, c_w_grp, c_scale, c_w_out, loss_target, m_norm_gain, m_a_w_in, m_a_v_gain, m_a_w_s, m_a_b_s, m_a_w_out, m_b_w_in, m_b_q_gain, m_b_k_gain, m_b_w_out, m_c_w_in, m_c_w_grp, m_c_scale, m_c_w_out, v_norm_gain, v_a_w_in, v_a_v_gain, v_a_w_s, v_a_b_s, v_a_w_out, v_b_w_in, v_b_q_gain, v_b_k_gain, v_b_w_out, v_c_w_in, v_c_w_grp, v_c_scale, v_c_w_out):
    wts = dict(norm_gain=norm_gain, a_w_in=a_w_in, a_v_gain=a_v_gain, a_w_s=a_w_s, a_b_s=a_b_s, a_w_out=a_w_out, b_w_in=b_w_in,
               b_q_gain=b_q_gain, b_k_gain=b_k_gain, b_w_out=b_w_out, c_w_in=c_w_in, c_w_grp=c_w_grp, c_scale=c_scale, c_w_out=c_w_out)
    mom1 = dict(norm_gain=m_norm_gain, a_w_in=m_a_w_in, a_v_gain=m_a_v_gain, a_w_s=m_a_w_s, a_b_s=m_a_b_s, a_w_out=m_a_w_out,
                b_w_in=m_b_w_in, b_q_gain=m_b_q_gain, b_k_gain=m_b_k_gain, b_w_out=m_b_w_out, c_w_in=m_c_w_in, c_w_grp=m_c_w_grp,
                c_scale=m_c_scale, c_w_out=m_c_w_out)
    mom2 = dict(norm_gain=v_norm_gain, a_w_in=v_a_w_in, a_v_gain=v_a_v_gain, a_w_s=v_a_w_s, a_b_s=v_a_b_s, a_w_out=v_a_w_out,
                b_w_in=v_b_w_in, b_q_gain=v_b_q_gain, b_k_gain=v_b_k_gain, b_w_out=v_b_w_out, c_w_in=v_c_w_in, c_w_grp=v_c_w_grp,
                c_scale=v_c_scale, c_w_out=v_c_w_out)
    axes = ("x", "y", "c")
    me = 2 * lax.axis_index("x") + lax.axis_index("y")
    core = lax.axis_index("c")

    wp = _pack_weights(wts)
    gathered = lax.dynamic_update_slice_in_dim(_allgather_weights(wp), wp[None], me, axis=0)
    full = _unpack_weights(gathered)
    for n, _ in REP:
        full[n] = wts[n]
    sq, grad_x, grads = _local_step(x[0], loss_target[0], full)
    loss = lax.psum(0.5 * jnp.sum(sq) / D_MODEL, axes)

    rep_g = _pack_rep(grads)
    gp = jnp.stack([_pack_blocks({n: _shard(grads[n], n, k) for n in SHARDED}, rep_g[k * REP_PART:(k + 1) * REP_PART], CDT)
                    for k in range(N_CHIPS)])
    from_sibling = _swap_halves(gp)
    my_rows = lax.dynamic_slice_in_dim(gp, core * G_HALF, G_HALF, axis=1)
    chip_sums = _add_halves(my_rows, from_sibling)
    parts = _scatter_shards(chip_sums)

    def my_half(tree):
        rep_part = lax.dynamic_slice_in_dim(_pack_rep(tree), me * REP_PART, REP_PART, axis=0)
        return lax.dynamic_slice_in_dim(_pack_blocks(tree, rep_part, F32), core * G_HALF, G_HALF, axis=0)

    mine = _adamw(me, parts, chip_sums, my_half(wts), my_half(mom1), my_half(mom2))
    theirs = _share_halves(mine)
    lo = jnp.where(core == 0, mine, theirs)
    hi = jnp.where(core == 0, theirs, mine)
    rep_mine = jnp.concatenate([lo[:, G_REP:], hi[:, G_REP:]], axis=1)
    rep_all = lax.dynamic_update_slice_in_dim(_allgather_rep(rep_mine), rep_mine[None], me, axis=0)

    outs = []
    for q in range(4):
        tree = _unpack_blocks(lo[q], hi[q])
        tree.update(_unpack_rep(rep_all[:, q].reshape(N_CHIPS * REP_PART, ROW)))
        outs.append(tree)
    return (loss, grad_x[None], *[t[n] for t in outs for n in WEIGHTS])
```

```python
import functools
import math

import jax
import jax.numpy as jnp
from jax import lax
from jax.experimental import pallas as pl
from jax.experimental.pallas import tpu as pltpu

F32 = jnp.float32
CDT = jnp.bfloat16

D_MODEL = 1024
EPS = 1e-6
CHUNK = 128
A_WIDTH = 2048
A_GROUPS = 8
A_GROUP_DIM = 256
HEAD_DIM = 128
B_HEADS = 8
B_DILATIONS = (1, 4, 16)
B_QK = 6144
B_IN = 10240
ROPE_HALF = 16
ROPE_THETA = 500000.0
POOL_SIZES = (2, 4, 8, 16)
POOL_HALO = 16
C_WIDTH = 2048
C_GROUP = 512
N_CHIPS = 4

ADAM_LR = 0.001
ADAM_B1 = 0.9
ADAM_B2 = 0.999
ADAM_EPS = 1e-08
ADAM_WD = 0.01
ADAM_STEP = 10

VMEM_LIMIT = 48 * 1024 * 1024
ANY = pl.BlockSpec(memory_space=pl.ANY)
MESH = pl.DeviceIdType.MESH

NN = (((1,), (0,)), ((), ()))
NT = (((1,), (1,)), ((), ()))
TN = (((0,), (0,)), ((), ()))


def _cparams(*sem):
    return pltpu.CompilerParams(dimension_semantics=sem, vmem_limit_bytes=VMEM_LIMIT)


def _dot(a, b, dims=NN):
    return lax.dot_general(a, b, dims, preferred_element_type=F32)


def _sigmoid(z):
    return 1.0 / (1.0 + jnp.exp(-z))


def _lane_sums(v):
    ones = jnp.ones((HEAD_DIM, HEAD_DIM), jnp.bfloat16)
    hi = v.astype(jnp.bfloat16)
    lo = (v - hi.astype(F32)).astype(jnp.bfloat16)
    return _dot(hi, ones) + _dot(lo, ones)


def _mm(a, b, mode, out_dtype, name, tm=1024, tn=1024, tk=1024, add=None):
    if mode == "nn":
        (m, k), n = a.shape, b.shape[1]
    elif mode == "nt":
        (m, k), n = a.shape, b.shape[0]
    else:
        (k, m), n = a.shape, b.shape[1]
    tm, tn, tk = min(tm, m), min(tn, n), min(tk, k)
    nk = k // tk
    a_spec = {"nn": pl.BlockSpec((tm, tk), lambda i, j, q: (i, q)),
              "nt": pl.BlockSpec((tm, tk), lambda i, j, q: (i, q)),
              "tn": pl.BlockSpec((tk, tm), lambda i, j, q: (q, i))}[mode]
    b_spec = {"nn": pl.BlockSpec((tk, tn), lambda i, j, q: (q, j)),
              "nt": pl.BlockSpec((tn, tk), lambda i, j, q: (j, q)),
              "tn": pl.BlockSpec((tk, tn), lambda i, j, q: (q, j))}[mode]
    dims = {"nn": NN, "nt": NT, "tn": TN}[mode]
    has_add = add is not None

    def body(*refs):
        a_ref, b_ref = refs[0], refs[1]
        o_ref = refs[3] if has_add else refs[2]
        p = _dot(a_ref[...], b_ref[...], dims)

        def finish(v):
            if has_add:
                v = v + refs[2][...]
            o_ref[...] = v.astype(o_ref.dtype)

        if nk == 1:
            finish(p)
        else:
            acc_ref = refs[-1]
            q = pl.program_id(2)

            @pl.when(q == 0)
            def _():
                acc_ref[...] = p

            @pl.when(q > 0)
            def _():
                acc_ref[...] += p

            @pl.when(q == nk - 1)
            def _():
                finish(acc_ref[...])

    in_specs = [a_spec, b_spec]
    args = [a, b]
    if has_add:
        in_specs.append(pl.BlockSpec((tm, tn), lambda i, j, q: (i, j)))
        args.append(add)
    return pl.pallas_call(
        body, name=name, grid=(m // tm, n // tn, nk), in_specs=in_specs,
        out_specs=pl.BlockSpec((tm, tn), lambda i, j, q: (i, j)),
        out_shape=jax.ShapeDtypeStruct((m, n), out_dtype),
        scratch_shapes=[pltpu.VMEM((tm, tn), F32)] if nk > 1 else [],
        compiler_params=_cparams("parallel", "parallel", "arbitrary"),
    )(*args)


def _rms_fwd(x, g, name, tq=512):
    s, d = x.shape

    def body(x_ref, g_ref, h_ref):
        xv = x_ref[...]
        r = lax.rsqrt(jnp.mean(xv * xv, axis=-1, keepdims=True) + EPS)
        h_ref[...] = (xv * r * g_ref[...]).astype(h_ref.dtype)

    return pl.pallas_call(
        body, name=name, grid=(s // tq,),
        in_specs=[pl.BlockSpec((tq, d), lambda i: (i, 0)), pl.BlockSpec((1, d), lambda i: (0, 0))],
        out_specs=pl.BlockSpec((tq, d), lambda i: (i, 0)),
        out_shape=jax.ShapeDtypeStruct((s, d), CDT), compiler_params=_cparams("parallel"),
    )(x, g)


def _rms_bwd(x, dh, dxo, g, name, tq=512):
    s, d = x.shape

    def body(x_ref, dh_ref, dxo_ref, g_ref, dx_ref, dxc_ref, dg_ref):
        xv = x_ref[...]
        dhv = dh_ref[...].astype(F32)
        r = lax.rsqrt(jnp.mean(xv * xv, axis=-1, keepdims=True) + EPS)
        gd = dhv * g_ref[...]
        dx = dxo_ref[...] + r * gd - xv * (r * r * r) * jnp.mean(gd * xv, axis=-1, keepdims=True)
        dx_ref[...] = dx
        dxc_ref[...] = dx.astype(dxc_ref.dtype)

        @pl.when(pl.program_id(0) == 0)
        def _():
            dg_ref[...] = jnp.zeros_like(dg_ref)

        dg_ref[...] += jnp.sum(dhv * xv * r, axis=0, keepdims=True)

    blk = pl.BlockSpec((tq, d), lambda i: (i, 0))
    vec = pl.BlockSpec((1, d), lambda i: (0, 0))
    return pl.pallas_call(
        body, name=name, grid=(s // tq,), in_specs=[blk, blk, blk, vec], out_specs=[blk, blk, vec],
        out_shape=[jax.ShapeDtypeStruct((s, d), F32), jax.ShapeDtypeStruct((s, d), CDT), jax.ShapeDtypeStruct((1, d), F32)],
        compiler_params=_cparams("arbitrary"),
    )(x, dh, dxo, g)


def _loss_bwd(y, target, name, tq=512):
    s, d = y.shape

    def body(y_ref, t_ref, dx_ref, dxc_ref, sq_ref):
        err = y_ref[...] - t_ref[...]
        dx = err * (1.0 / d)
        dx_ref[...] = dx
        dxc_ref[...] = dx.astype(dxc_ref.dtype)

        @pl.when(pl.program_id(0) == 0)
        def _():
            sq_ref[...] = jnp.zeros_like(sq_ref)

        sq_ref[...] += jnp.sum(err * err, axis=0, keepdims=True)

    blk = pl.BlockSpec((tq, d), lambda i: (i, 0))
    vec = pl.BlockSpec((1, d), lambda i: (0, 0))
    return pl.pallas_call(
        body, name=name, grid=(s // tq,), in_specs=[blk, blk], out_specs=[blk, blk, vec],
        out_shape=[jax.ShapeDtypeStruct((s, d), F32), jax.ShapeDtypeStruct((s, d), CDT), jax.ShapeDtypeStruct((1, d), F32)],
        compiler_params=_cparams("arbitrary"),
    )(y, target)


def _tril_mask():
    row = lax.broadcasted_iota(jnp.int32, (CHUNK, CHUNK), 0)
    col = lax.broadcasted_iota(jnp.int32, (CHUNK, CHUNK), 1)
    return row >= col


def _a_mid_fwd(proj, v_gain, w_s, b_s_t, name, tq=256):
    s = proj.shape[0]

    def body(p_ref, vg_ref, ws_ref, bs_ref, y_ref):
        vraw = p_ref[:, A_WIDTH:2 * A_WIDTH].astype(F32)
        r = lax.rsqrt(jnp.mean(vraw * vraw, axis=-1, keepdims=True) + EPS)
        vn = (vraw * r * vg_ref[...]).astype(CDT)
        tri = _tril_mask()
        for g in range(A_GROUPS):
            w = jnp.where(tri, ws_ref[g], 0.0).astype(CDT)
            bias = bs_ref[:, g:g + 1]
            cols = slice(g * A_GROUP_DIM, (g + 1) * A_GROUP_DIM)
            zcols = slice(2 * A_WIDTH + g * A_GROUP_DIM, 2 * A_WIDTH + (g + 1) * A_GROUP_DIM)
            for c in range(tq // CHUNK):
                rows = slice(c * CHUNK, (c + 1) * CHUNK)
                mixed = _dot(w, vn[rows, cols]) + bias
                u = p_ref[rows, cols].astype(F32)
                z = p_ref[rows, zcols].astype(F32)
                y_ref[rows, cols] = (u * mixed * (z * _sigmoid(z))).astype(y_ref.dtype)

    return pl.pallas_call(
        body, name=name, grid=(s // tq,),
        in_specs=[pl.BlockSpec((tq, 3 * A_WIDTH), lambda i: (i, 0)), pl.BlockSpec((1, A_WIDTH), lambda i: (0, 0)),
                  pl.BlockSpec((A_GROUPS, CHUNK, CHUNK), lambda i: (0, 0, 0)), pl.BlockSpec((CHUNK, A_GROUPS), lambda i: (0, 0))],
        out_specs=pl.BlockSpec((tq, A_WIDTH), lambda i: (i, 0)),
        out_shape=jax.ShapeDtypeStruct((s, A_WIDTH), CDT), compiler_params=_cparams("parallel"),
    )(proj, v_gain, w_s, b_s_t)


def _a_mid_bwd(proj, dy, v_gain, w_s, b_s_t, name, tq=256):
    s = proj.shape[0]

    def body(p_ref, dy_ref, vg_ref, ws_ref, bs_ref, dp_ref, dws_ref, dbs_ref, dvg_ref, dvn_ref):
        @pl.when(pl.program_id(0) == 0)
        def _():
            dws_ref[...] = jnp.zeros_like(dws_ref)
            dbs_ref[...] = jnp.zeros_like(dbs_ref)
            dvg_ref[...] = jnp.zeros_like(dvg_ref)

        vraw = p_ref[:, A_WIDTH:2 * A_WIDTH].astype(F32)
        r = lax.rsqrt(jnp.mean(vraw * vraw, axis=-1, keepdims=True) + EPS)
        vhat = vraw * r
        vg = vg_ref[...]
        vn = (vhat * vg).astype(CDT)
        tri = _tril_mask()
        lane = lax.broadcasted_iota(jnp.int32, (CHUNK, A_GROUPS), 1)
        dbs = jnp.zeros((CHUNK, A_GROUPS), F32)
        for g in range(A_GROUPS):
            w = jnp.where(tri, ws_ref[g], 0.0).astype(CDT)
            bias = bs_ref[:, g:g + 1]
            cols = slice(g * A_GROUP_DIM, (g + 1) * A_GROUP_DIM)
            zcols = slice(2 * A_WIDTH + g * A_GROUP_DIM, 2 * A_WIDTH + (g + 1) * A_GROUP_DIM)
            dws = jnp.zeros((CHUNK, CHUNK), F32)
            for c in range(tq // CHUNK):
                rows = slice(c * CHUNK, (c + 1) * CHUNK)
                vn_g = vn[rows, cols]
                mixed = _dot(w, vn_g) + bias
                u = p_ref[rows, cols].astype(F32)
                z = p_ref[rows, zcols].astype(F32)
                dyv = dy_ref[rows, cols].astype(F32)
                sg = _sigmoid(z)
                sz = z * sg
                dyu = dyv * u
                dmixed = dyu * sz
                dp_ref[rows, cols] = (dyv * mixed * sz).astype(dp_ref.dtype)
                dp_ref[rows, zcols] = (dyu * mixed * (sg * (1.0 + z * (1.0 - sg)))).astype(dp_ref.dtype)
                dmc = dmixed.astype(CDT)
                dws = dws + _dot(dmc, vn_g, NT)
                dbs = dbs + jnp.where(lane == g, jnp.sum(dmixed, axis=-1, keepdims=True), 0.0)
                dvn_ref[rows, cols] = _dot(w, dmc, TN)
            dws_ref[g] += jnp.where(tri, dws, 0.0)
        dbs_ref[...] += dbs
        dvn = dvn_ref[...]
        gd = dvn * vg
        dvraw = r * gd - vraw * (r * r * r) * jnp.mean(gd * vraw, axis=-1, keepdims=True)
        dp_ref[:, A_WIDTH:2 * A_WIDTH] = dvraw.astype(dp_ref.dtype)
        dvg_ref[...] += jnp.sum(dvn * vhat, axis=0, keepdims=True)

    return pl.pallas_call(
        body, name=name, grid=(s // tq,),
        in_specs=[pl.BlockSpec((tq, 3 * A_WIDTH), lambda i: (i, 0)), pl.BlockSpec((tq, A_WIDTH), lambda i: (i, 0)),
                  pl.BlockSpec((1, A_WIDTH), lambda i: (0, 0)), pl.BlockSpec((A_GROUPS, CHUNK, CHUNK), lambda i: (0, 0, 0)),
                  pl.BlockSpec((CHUNK, A_GROUPS), lambda i: (0, 0))],
        out_specs=[pl.BlockSpec((tq, 3 * A_WIDTH), lambda i: (i, 0)), pl.BlockSpec((A_GROUPS, CHUNK, CHUNK), lambda i: (0, 0, 0)),
                   pl.BlockSpec((CHUNK, A_GROUPS), lambda i: (0, 0)), pl.BlockSpec((1, A_WIDTH), lambda i: (0, 0))],
        out_shape=[jax.ShapeDtypeStruct((s, 3 * A_WIDTH), CDT), jax.ShapeDtypeStruct((A_GROUPS, CHUNK, CHUNK), F32),
                   jax.ShapeDtypeStruct((CHUNK, A_GROUPS), F32), jax.ShapeDtypeStruct((1, A_WIDTH), F32)],
        scratch_shapes=[pltpu.VMEM((tq, A_WIDTH), F32)],
        compiler_params=_cparams("arbitrary"),
    )(proj, dy, v_gain, w_s, b_s_t)


def _rope_tables(s):
    inv_freq = jnp.power(jnp.float32(ROPE_THETA), -jnp.arange(ROPE_HALF, dtype=F32) / ROPE_HALF)
    ang = jnp.arange(s, dtype=F32)[:, None] * inv_freq[None, :]
    cos, sin = jnp.cos(ang), jnp.sin(ang)
    rest = HEAD_DIM - 2 * ROPE_HALF
    t_c = jnp.concatenate([cos, cos, jnp.ones((s, rest), F32)], axis=1)
    t_a = jnp.concatenate([-sin, jnp.zeros((s, HEAD_DIM - ROPE_HALF), F32)], axis=1)
    t_b = jnp.concatenate([jnp.zeros((s, ROPE_HALF), F32), sin, jnp.zeros((s, rest), F32)], axis=1)
    return t_c, t_a, t_b


def _b_qk_fwd(proj, gains, tabs, name, tq=256):
    s = proj.shape[0]

    def body(p_ref, g_ref, tc_ref, ta_ref, tb_ref, o_ref):
        tc, ta, tb = tc_ref[...], ta_ref[...], tb_ref[...]
        for tg in range(6):
            gain = g_ref[tg:tg + 1, :]
            for h in range(B_HEADS):
                cols = slice(tg * 1024 + h * HEAD_DIM, tg * 1024 + (h + 1) * HEAD_DIM)
                xv = p_ref[:, cols].astype(F32)
                r = lax.rsqrt(_lane_sums(xv * xv) * (1.0 / HEAD_DIM) + EPS)
                xn = xv * r * gain
                y = xn * tc + pltpu.roll(xn, HEAD_DIM - ROPE_HALF, 1) * ta + pltpu.roll(xn, ROPE_HALF, 1) * tb
                o_ref[:, cols] = y.astype(o_ref.dtype)

    tab = pl.BlockSpec((tq, HEAD_DIM), lambda i: (i, 0))
    return pl.pallas_call(
        body, name=name, grid=(s // tq,),
        in_specs=[pl.BlockSpec((tq, B_QK), lambda i: (i, 0)), pl.BlockSpec((6, HEAD_DIM), lambda i: (0, 0)), tab, tab, tab],
        out_specs=pl.BlockSpec((tq, B_QK), lambda i: (i, 0)),
        out_shape=jax.ShapeDtypeStruct((s, B_QK), CDT), compiler_params=_cparams("parallel"),
    )(proj, gains, *tabs)


PERMUTE_BLOCK_BYTES = 4 * 1024 * 1024


def _view_rows(length, dil, width, itemsize):
    rows = 16
    while 2 * rows * dil * width * itemsize <= PERMUTE_BLOCK_BYTES and 2 * rows <= length:
        rows *= 2
    return rows


def _to_view(x, col, width, dil, name):
    s = x.shape[0]
    length = s // dil
    tl = _view_rows(length, dil, width, 4)
    lanes = HEAD_DIM
    nblk = width // lanes

    def body(x_ref, o_ref, slab_ref):
        for b in range(nblk):
            slab_ref[b] = x_ref[:, b * lanes:(b + 1) * lanes].astype(F32)
        for r in range(dil):
            for b in range(nblk):
                o_ref[:, r * width + b * lanes:r * width + (b + 1) * lanes] = (
                    slab_ref.at[b][pl.ds(r, tl, stride=dil), :].astype(o_ref.dtype))

    return pl.pallas_call(
        body, name=name, grid=(length // tl,),
        in_specs=[pl.BlockSpec((tl * dil, width), lambda i: (i, col))],
        out_specs=pl.BlockSpec((tl, dil * width), lambda i: (i, 0)),
        out_shape=jax.ShapeDtypeStruct((length, dil * width), x.dtype),
        scratch_shapes=[pltpu.VMEM((nblk, tl * dil, lanes), F32)],
        compiler_params=_cparams("parallel"),
    )(x)


def _from_view(v, dil, name):
    length, width = v.shape[0], v.shape[1] // dil
    tl = _view_rows(length, dil, width, 4)
    lanes = HEAD_DIM
    nblk = width // lanes

    def body(v_ref, o_ref, slab_ref):
        for r in range(dil):
            for b in range(nblk):
                slab_ref.at[b][pl.ds(r, tl, stride=dil), :] = v_ref[:, r * width + b * lanes:r * width + (b + 1) * lanes].astype(F32)
        for b in range(nblk):
            o_ref[:, b * lanes:(b + 1) * lanes] = slab_ref[b].astype(o_ref.dtype)

    return pl.pallas_call(
        body, name=name, grid=(length // tl,),
        in_specs=[pl.BlockSpec((tl, dil * width), lambda i: (i, 0))],
        out_specs=pl.BlockSpec((tl * dil, width), lambda i: (i, 0)),
        out_shape=jax.ShapeDtypeStruct((length * dil, width), v.dtype),
        scratch_shapes=[pltpu.VMEM((nblk, tl * dil, lanes), F32)],
        compiler_params=_cparams("parallel"),
    )(v)


def _b_attn_fwd(q, k, v, bases, dil, name):
    length = q.shape[0]
    nb = length // CHUNK
    scale = 1.0 / math.sqrt(HEAD_DIM)
    w = B_HEADS * HEAD_DIM
    qb, kb, vb = bases

    def body(q_ref, kc_ref, kp_ref, vc_ref, vp_ref, o_ref, lse_ref):
        n = pl.program_id(1)
        qi = lax.broadcasted_iota(jnp.int32, (CHUNK, 2 * CHUNK), 0)
        ki = lax.broadcasted_iota(jnp.int32, (CHUNK, 2 * CHUNK), 1)
        first_key = jnp.where(n > 0, 0, CHUNK)
        mask = (ki >= qi) & (ki <= qi + CHUNK) & (ki >= first_key)
        lane = lax.broadcasted_iota(jnp.int32, (CHUNK, HEAD_DIM), 1)
        lse_all = jnp.zeros((CHUNK, HEAD_DIM), F32)
        for h in range(B_HEADS):
            sl = slice(h * HEAD_DIM, (h + 1) * HEAD_DIM)
            k2 = jnp.concatenate([kp_ref[:, sl], kc_ref[:, sl]], axis=0)
            v2 = jnp.concatenate([vp_ref[:, sl], vc_ref[:, sl]], axis=0)
            sc = jnp.where(mask, _dot(q_ref[:, sl], k2, NT) * scale, -1e30)
            m = jnp.max(sc, axis=-1, keepdims=True)
            p = jnp.exp(sc - m)
            l = jnp.sum(p, axis=-1, keepdims=True)
            o_ref[:, sl] = _dot(p.astype(CDT), v2) / l
            lse_all = jnp.where(lane == h, m + jnp.log(l), lse_all)
        lse_ref[...] = lse_all

    prev = lambda n: jnp.maximum(n - 1, 0)
    blk = lambda f: pl.BlockSpec((CHUNK, w), f)
    return pl.pallas_call(
        body, name=name, grid=(dil, nb),
        in_specs=[blk(lambda r, n: (n, qb + r)), blk(lambda r, n: (n, kb + r)), blk(lambda r, n: (prev(n), kb + r)),
                  blk(lambda r, n: (n, vb + r)), blk(lambda r, n: (prev(n), vb + r))],
        out_specs=[blk(lambda r, n: (n, r)), pl.BlockSpec((CHUNK, HEAD_DIM), lambda r, n: (n, r))],
        out_shape=[jax.ShapeDtypeStruct((length, dil * w), F32), jax.ShapeDtypeStruct((length, dil * HEAD_DIM), F32)],
        compiler_params=_cparams("parallel", "parallel"),
    )(q, k, k, v, v)


def _b_combine(os_, lses, proj, name, tq=512):
    s = proj.shape[0]
    w = B_HEADS * HEAD_DIM

    def body(o0_ref, o1_ref, o2_ref, l0_ref, l1_ref, l2_ref, z_ref, y_ref, oj_ref, lj_ref):
        l0, l1, l2 = l0_ref[...], l1_ref[...], l2_ref[...]
        m = jnp.maximum(jnp.maximum(l0, l1), l2)
        lj = m + jnp.log(jnp.exp(l0 - m) + jnp.exp(l1 - m) + jnp.exp(l2 - m))
        lj_ref[...] = lj
        w0, w1, w2 = jnp.exp(l0 - lj), jnp.exp(l1 - lj), jnp.exp(l2 - lj)
        for h in range(B_HEADS):
            sl = slice(h * HEAD_DIM, (h + 1) * HEAD_DIM)
            o = w0[:, h:h + 1] * o0_ref[:, sl] + w1[:, h:h + 1] * o1_ref[:, sl] + w2[:, h:h + 1] * o2_ref[:, sl]
            z = z_ref[:, sl].astype(F32)
            oj_ref[:, sl] = o.astype(oj_ref.dtype)
            y_ref[:, sl] = (o * (z * _sigmoid(z))).astype(y_ref.dtype)

    blk = pl.BlockSpec((tq, w), lambda i: (i, 0))
    st = pl.BlockSpec((tq, HEAD_DIM), lambda i: (i, 0))
    return pl.pallas_call(
        body, name=name, grid=(s // tq,),
        in_specs=[blk, blk, blk, st, st, st, pl.BlockSpec((tq, w), lambda i: (i, 9))],
        out_specs=[blk, blk, st],
        out_shape=[jax.ShapeDtypeStruct((s, w), CDT), jax.ShapeDtypeStruct((s, w), CDT), jax.ShapeDtypeStruct((s, HEAD_DIM), F32)],
        compiler_params=_cparams("parallel"),
    )(*os_, *lses, proj)


def _b_bwd_prep(dy, oj, proj, name, tq=512):
    s = proj.shape[0]
    w = B_HEADS * HEAD_DIM

    def body(dy_ref, oj_ref, z_ref, do_ref, dz_ref, dd_ref):
        lane = lax.broadcasted_iota(jnp.int32, (tq, HEAD_DIM), 1)
        dd = jnp.zeros((tq, HEAD_DIM), F32)
        for h in range(B_HEADS):
            sl = slice(h * HEAD_DIM, (h + 1) * HEAD_DIM)
            z = z_ref[:, sl].astype(F32)
            dyv = dy_ref[:, sl].astype(F32)
            o = oj_ref[:, sl].astype(F32)
            sg = _sigmoid(z)
            do = dyv * (z * sg)
            do_ref[:, sl] = do.astype(do_ref.dtype)
            dz_ref[:, sl] = (dyv * o * (sg * (1.0 + z * (1.0 - sg)))).astype(dz_ref.dtype)
            dd = jnp.where(lane == h, jnp.sum(do * o, axis=-1, keepdims=True), dd)
        dd_ref[...] = dd

    blk = pl.BlockSpec((tq, w), lambda i: (i, 0))
    st = pl.BlockSpec((tq, HEAD_DIM), lambda i: (i, 0))
    return pl.pallas_call(
        body, name=name, grid=(s // tq,),
        in_specs=[blk, blk, pl.BlockSpec((tq, w), lambda i: (i, 9))], out_specs=[blk, blk, st],
        out_shape=[jax.ShapeDtypeStruct((s, w), CDT), jax.ShapeDtypeStruct((s, w), CDT), jax.ShapeDtypeStruct((s, HEAD_DIM), F32)],
        compiler_params=_cparams("parallel"),
    )(dy, oj, proj)


def _b_attn_bwd(q, k, v, bases, do, lj, dd, dil, name):
    length = q.shape[0]
    nb = length // CHUNK
    scale = 1.0 / math.sqrt(HEAD_DIM)
    w = B_HEADS * HEAD_DIM
    qb, kb, vb = bases

    def body(qj_ref, qn_ref, k_ref, v_ref, doj_ref, don_ref, lj_ref, ln_ref, dj_ref, dn_ref, out_ref, carry_ref):
        j = pl.program_id(1)

        @pl.when(j == 0)
        def _():
            carry_ref[...] = jnp.zeros_like(carry_ref)

        qi = lax.broadcasted_iota(jnp.int32, (2 * CHUNK, CHUNK), 0)
        ki = lax.broadcasted_iota(jnp.int32, (2 * CHUNK, CHUNK), 1)
        no_next = jnp.where(j + 1 < nb, 0, 2 * CHUNK)
        mask = ((qi < CHUNK) & (ki <= qi)) | ((qi >= CHUNK) & (ki >= qi - CHUNK + no_next))
        for h in range(B_HEADS):
            sl = slice(h * HEAD_DIM, (h + 1) * HEAD_DIM)
            q2 = jnp.concatenate([qj_ref[:, sl], qn_ref[:, sl]], axis=0)
            do2 = jnp.concatenate([doj_ref[:, sl], don_ref[:, sl]], axis=0)
            lse2 = jnp.concatenate([lj_ref[:, h:h + 1], ln_ref[:, h:h + 1]], axis=0)
            d2 = jnp.concatenate([dj_ref[:, h:h + 1], dn_ref[:, h:h + 1]], axis=0)
            k = k_ref[:, sl]
            v = v_ref[:, sl]
            sc = _dot(q2, k, NT) * scale
            p = jnp.where(mask, jnp.exp(sc - lse2), 0.0)
            dp = _dot(do2, v, NT)
            ds = (p * (dp - d2) * scale).astype(CDT)
            dq2 = _dot(ds, k)
            out_ref[:, sl] = (carry_ref[:, sl] + dq2[:CHUNK]).astype(out_ref.dtype)
            carry_ref[:, sl] = dq2[CHUNK:]
            out_ref[:, w + h * HEAD_DIM:w + (h + 1) * HEAD_DIM] = _dot(ds, q2, TN).astype(out_ref.dtype)
            out_ref[:, 2 * w + h * HEAD_DIM:2 * w + (h + 1) * HEAD_DIM] = _dot(p.astype(CDT), do2, TN).astype(out_ref.dtype)

    nxt = lambda j: jnp.minimum(j + 1, nb - 1)
    blk = lambda f: pl.BlockSpec((CHUNK, w), f)
    st = lambda f: pl.BlockSpec((CHUNK, HEAD_DIM), f)
    return pl.pallas_call(
        body, name=name, grid=(dil, nb),
        in_specs=[blk(lambda r, j: (j, qb + r)), blk(lambda r, j: (nxt(j), qb + r)),
                  blk(lambda r, j: (j, kb + r)), blk(lambda r, j: (j, vb + r)),
                  blk(lambda r, j: (j, r)), blk(lambda r, j: (nxt(j), r)),
                  st(lambda r, j: (j, r)), st(lambda r, j: (nxt(j), r)), st(lambda r, j: (j, r)), st(lambda r, j: (nxt(j), r))],
        out_specs=pl.BlockSpec((CHUNK, 3 * w), lambda r, j: (j, r)),
        out_shape=jax.ShapeDtypeStruct((length, dil * 3 * w), CDT),
        scratch_shapes=[pltpu.VMEM((CHUNK, w), F32)],
        compiler_params=_cparams("parallel", "arbitrary"),
    )(q, q, k, v, do, do, lj, lj, dd, dd)


def _b_qk_bwd(proj, dqkv, dz, gains, tabs, name, tq=256):
    s = proj.shape[0]
    w = B_HEADS * HEAD_DIM

    def body(p_ref, d0_ref, d1_ref, d2_ref, dz_ref, g_ref, tc_ref, ta_ref, tb_ref, dp_ref, dg_ref):
        @pl.when(pl.program_id(0) == 0)
        def _():
            dg_ref[...] = jnp.zeros_like(dg_ref)

        tc, ta, tb = tc_ref[...], ta_ref[...], tb_ref[...]
        d_refs = (d0_ref, d1_ref, d2_ref)
        for g in range(3):
            for t in range(2):
                tg = t * 3 + g
                gain = g_ref[tg:tg + 1, :]
                dgain = jnp.zeros((1, HEAD_DIM), F32)
                for h in range(B_HEADS):
                    cols = slice(tg * w + h * HEAD_DIM, tg * w + (h + 1) * HEAD_DIM)
                    xv = p_ref[:, cols].astype(F32)
                    r = lax.rsqrt(_lane_sums(xv * xv) * (1.0 / HEAD_DIM) + EPS)
                    dyv = d_refs[g][:, t * w + h * HEAD_DIM:t * w + (h + 1) * HEAD_DIM].astype(F32)
                    dxn = dyv * tc + pltpu.roll(dyv * ta, ROPE_HALF, 1) + pltpu.roll(dyv * tb, HEAD_DIM - ROPE_HALF, 1)
                    gd = dxn * gain
                    dx = r * gd - xv * (r * r * r) * (_lane_sums(gd * xv) * (1.0 / HEAD_DIM))
                    dp_ref[:, cols] = dx.astype(dp_ref.dtype)
                    dgain = dgain + jnp.sum(dxn * xv * r, axis=0, keepdims=True)
                dg_ref[tg:tg + 1, :] += dgain
            dp_ref[:, (6 + g) * w:(7 + g) * w] = d_refs[g][:, 2 * w:3 * w]
        dp_ref[:, 9 * w:10 * w] = dz_ref[...]

    tab = pl.BlockSpec((tq, HEAD_DIM), lambda i: (i, 0))
    dblk = pl.BlockSpec((tq, 3 * w), lambda i: (i, 0))
    return pl.pallas_call(
        body, name=name, grid=(s // tq,),
        in_specs=[pl.BlockSpec((tq, B_QK), lambda i: (i, 0)), dblk, dblk, dblk, pl.BlockSpec((tq, w), lambda i: (i, 0)),
                  pl.BlockSpec((6, HEAD_DIM), lambda i: (0, 0)), tab, tab, tab],
        out_specs=[pl.BlockSpec((tq, B_IN), lambda i: (i, 0)), pl.BlockSpec((6, HEAD_DIM), lambda i: (0, 0))],
        out_shape=[jax.ShapeDtypeStruct((s, B_IN), CDT), jax.ShapeDtypeStruct((6, HEAD_DIM), F32)],
        compiler_params=_cparams("arbitrary"),
    )(proj, *dqkv, dz, gains, *tabs)


def _inv_count(t, window):
    return 1.0 / jnp.minimum(t + 1, window).astype(F32)


def _c_pool_fwd(proj, name, tq=512):
    s = proj.shape[0]
    per = tq // POOL_HALO

    def body(x_ref, halo_ref, o_ref, ext_ref):
        i = pl.program_id(0)
        xv = x_ref[...].astype(F32)
        ext_ref[POOL_HALO:, :] = xv
        ext_ref[:POOL_HALO, :] = jnp.where(i > 0, halo_ref[...].astype(F32), 0.0)
        t = i * tq + lax.broadcasted_iota(jnp.int32, (tq, 1), 0)
        for g, window in enumerate(POOL_SIZES):
            cols = slice(g * C_GROUP, (g + 1) * C_GROUP)
            acc = xv[:, cols]
            for back in range(1, window):
                acc = acc + ext_ref[POOL_HALO - back:POOL_HALO - back + tq, cols]
            o_ref[:, cols] = (acc * _inv_count(t, window) - xv[:, cols]).astype(o_ref.dtype)

    return pl.pallas_call(
        body, name=name, grid=(s // tq,),
        in_specs=[pl.BlockSpec((tq, C_WIDTH), lambda i: (i, 0)),
                  pl.BlockSpec((POOL_HALO, C_WIDTH), lambda i: (jnp.maximum(i * per - 1, 0), 0))],
        out_specs=pl.BlockSpec((tq, C_WIDTH), lambda i: (i, 0)),
        out_shape=jax.ShapeDtypeStruct((s, C_WIDTH), CDT),
        scratch_shapes=[pltpu.VMEM((tq + POOL_HALO, C_WIDTH), F32)],
        compiler_params=_cparams("parallel"),
    )(proj, proj)


def _c_mid_fwd(diff, proj, w_grp, scale, name, tq=512):
    s = diff.shape[0]

    def body(d_ref, z_ref, w_ref, sc_ref, y_ref):
        for g in range(len(POOL_SIZES)):
            cols = slice(g * C_GROUP, (g + 1) * C_GROUP)
            z = z_ref[:, cols].astype(F32)
            y_ref[:, cols] = (_dot(d_ref[:, cols], w_ref[g]) * sc_ref[:, cols] * (z * _sigmoid(z))).astype(y_ref.dtype)

    return pl.pallas_call(
        body, name=name, grid=(s // tq,),
        in_specs=[pl.BlockSpec((tq, C_WIDTH), lambda i: (i, 0)), pl.BlockSpec((tq, C_WIDTH), lambda i: (i, 1)),
                  pl.BlockSpec((4, C_GROUP, C_GROUP), lambda i: (0, 0, 0)), pl.BlockSpec((1, C_WIDTH), lambda i: (0, 0))],
        out_specs=pl.BlockSpec((tq, C_WIDTH), lambda i: (i, 0)),
        out_shape=jax.ShapeDtypeStruct((s, C_WIDTH), CDT), compiler_params=_cparams("parallel"),
    )(diff, proj, w_grp, scale)


def _c_mid_bwd(diff, proj, w_grp, scale, dy, name, tq=512):
    s = diff.shape[0]

    def body(d_ref, z_ref, w_ref, sc_ref, dy_ref, dd_ref, dz_ref, dw_ref, dsc_ref):
        @pl.when(pl.program_id(0) == 0)
        def _():
            dw_ref[...] = jnp.zeros_like(dw_ref)
            dsc_ref[...] = jnp.zeros_like(dsc_ref)

        for g in range(len(POOL_SIZES)):
            cols = slice(g * C_GROUP, (g + 1) * C_GROUP)
            d = d_ref[:, cols]
            m0 = _dot(d, w_ref[g])
            z = z_ref[:, cols].astype(F32)
            dyv = dy_ref[:, cols].astype(F32)
            sc = sc_ref[:, cols]
            sg = _sigmoid(z)
            dmixed = dyv * (z * sg)
            dz_ref[:, cols] = (dyv * m0 * sc * (sg * (1.0 + z * (1.0 - sg)))).astype(dz_ref.dtype)
            dsc_ref[:, cols] += jnp.sum(dmixed * m0, axis=0, keepdims=True)
            dm0 = (dmixed * sc).astype(CDT)
            dw_ref[g] += _dot(d, dm0, TN)
            dd_ref[:, cols] = _dot(dm0, w_ref[g], NT)

    blk = pl.BlockSpec((tq, C_WIDTH), lambda i: (i, 0))
    wsp = pl.BlockSpec((4, C_GROUP, C_GROUP), lambda i: (0, 0, 0))
    vec = pl.BlockSpec((1, C_WIDTH), lambda i: (0, 0))
    return pl.pallas_call(
        body, name=name, grid=(s // tq,),
        in_specs=[blk, pl.BlockSpec((tq, C_WIDTH), lambda i: (i, 1)), wsp, vec, blk],
        out_specs=[blk, blk, wsp, vec],
        out_shape=[jax.ShapeDtypeStruct((s, C_WIDTH), F32), jax.ShapeDtypeStruct((s, C_WIDTH), CDT),
                   jax.ShapeDtypeStruct((4, C_GROUP, C_GROUP), F32), jax.ShapeDtypeStruct((1, C_WIDTH), F32)],
        compiler_params=_cparams("arbitrary"),
    )(diff, proj, w_grp, scale, dy)


def _c_pool_bwd(ddiff, dz, name, tq=512):
    s = ddiff.shape[0]
    per = tq // POOL_HALO
    last = s // tq - 1

    def body(d_ref, halo_ref, dz_ref, o_ref, ext_ref):
        i = pl.program_id(0)
        dv = d_ref[...]
        t = i * tq + lax.broadcasted_iota(jnp.int32, (tq, 1), 0)
        for g, window in enumerate(POOL_SIZES):
            cols = slice(g * C_GROUP, (g + 1) * C_GROUP)
            ext_ref[:tq, cols] = dv[:, cols] * _inv_count(t, window)
            ext_ref[tq:, cols] = jnp.where(i < last, halo_ref[:, cols] * (1.0 / window), 0.0)
            acc = -dv[:, cols]
            for fwd in range(window):
                acc = acc + ext_ref[fwd:fwd + tq, cols]
            o_ref[:, cols] = acc.astype(o_ref.dtype)
        o_ref[:, C_WIDTH:] = dz_ref[...]

    return pl.pallas_call(
        body, name=name, grid=(s // tq,),
        in_specs=[pl.BlockSpec((tq, C_WIDTH), lambda i: (i, 0)),
                  pl.BlockSpec((POOL_HALO, C_WIDTH), lambda i: (jnp.minimum((i + 1) * per, s // POOL_HALO - 1), 0)),
                  pl.BlockSpec((tq, C_WIDTH), lambda i: (i, 0))],
        out_specs=pl.BlockSpec((tq, 2 * C_WIDTH), lambda i: (i, 0)),
        out_shape=jax.ShapeDtypeStruct((s, 2 * C_WIDTH), CDT),
        scratch_shapes=[pltpu.VMEM((tq + POOL_HALO, C_WIDTH), F32)],
        compiler_params=_cparams("parallel"),
    )(ddiff, ddiff, dz)


LAYERS = (("a", 0), ("b", 0), ("c", 0), ("a", 1))


def _local_step(x, target, w):
    s = x.shape[0]
    tabs = _rope_tables(s)
    qk_gains = jnp.concatenate([w["b_q_gain"][0], w["b_k_gain"][0]], axis=0)
    saved = []
    for li, (kind, j) in enumerate(LAYERS):
        h = _rms_fwd(x, w["norm_gain"][li:li + 1], f"rms_fwd{li}")
        if kind == "a":
            proj = _mm(h, w["a_w_in"][j], "nn", CDT, f"a_in{li}")
            bs_t = w["a_b_s"][j].T
            y = _a_mid_fwd(proj, w["a_v_gain"][j:j + 1], w["a_w_s"][j], bs_t, f"a_mid_fwd{li}")
            x_next = _mm(y, w["a_w_out"][j], "nn", F32, f"a_out{li}", tm=512, tk=2048, add=x)
            saved.append((x, h, proj, y))
        elif kind == "b":
            proj = _mm(h, w["b_w_in"][j], "nn", CDT, f"b_in{li}")
            qk = _b_qk_fwd(proj, qk_gains, tabs, f"b_qk_fwd{li}")
            qkv, os_, lses = [], [], []
            for g, dil in enumerate(B_DILATIONS):
                if dil == 1:
                    ops = (qk, qk, proj, (g, 3 + g, 6 + g))
                else:
                    ops = (_to_view(qk, g, 1024, dil, f"b_q_view{li}_{g}"), _to_view(qk, 3 + g, 1024, dil, f"b_k_view{li}_{g}"),
                           _to_view(proj, 6 + g, 1024, dil, f"b_v_view{li}_{g}"), (0, 0, 0))
                o, lse = _b_attn_fwd(*ops, dil, f"b_attn_fwd{li}_{g}")
                if dil > 1:
                    o, lse = _from_view(o, dil, f"b_o_nat{li}_{g}"), _from_view(lse, dil, f"b_lse_nat{li}_{g}")
                qkv.append(ops)
                os_.append(o)
                lses.append(lse)
            y, oj, lj = _b_combine(os_, lses, proj, f"b_combine{li}")
            x_next = _mm(y, w["b_w_out"][j], "nn", F32, f"b_out{li}", tm=512, add=x)
            saved.append((x, h, proj, y, qkv, oj, lj))
        else:
            proj = _mm(h, w["c_w_in"][j], "nn", CDT, f"c_in{li}")
            diff = _c_pool_fwd(proj, f"c_pool_fwd{li}")
            y = _c_mid_fwd(diff, proj, w["c_w_grp"][j], w["c_scale"][j:j + 1], f"c_mid_fwd{li}")
            x_next = _mm(y, w["c_w_out"][j], "nn", F32, f"c_out{li}", tm=512, tk=2048, add=x)
            saved.append((x, h, proj, y, diff))
        x = x_next

    dx, dxc, sq = _loss_bwd(x, target, "loss_bwd")
    grads = {"norm_gain": [None] * len(LAYERS), "a_w_in": [None, None], "a_v_gain": [None, None], "a_w_s": [None, None],
             "a_b_s": [None, None], "a_w_out": [None, None]}
    for li in reversed(range(len(LAYERS))):
        kind, j = LAYERS[li]
        sv = saved[li]
        xin, h, proj, y = sv[:4]
        if kind == "a":
            grads["a_w_out"][j] = _mm(y, dxc, "tn", CDT, f"a_dwout{li}")
            dy = _mm(dxc, w["a_w_out"][j], "nt", CDT, f"a_dy{li}")
            dproj, dws, dbs_t, dvg = _a_mid_bwd(proj, dy, w["a_v_gain"][j:j + 1], w["a_w_s"][j], w["a_b_s"][j].T, f"a_mid_bwd{li}")
            grads["a_w_s"][j], grads["a_b_s"][j], grads["a_v_gain"][j] = dws, dbs_t.T, dvg[0]
            grads["a_w_in"][j] = _mm(h, dproj, "tn", CDT, f"a_dwin{li}")
            dh = _mm(dproj, w["a_w_in"][j], "nt", F32, f"a_dh{li}")
        elif kind == "b":
            qkv, oj, lj = sv[4:]
            grads["b_w_out"] = _mm(y, dxc, "tn", CDT, f"b_dwout{li}")[None]
            dy = _mm(dxc, w["b_w_out"][j], "nt", CDT, f"b_dy{li}")
            do, dz, dd = _b_bwd_prep(dy, oj, proj, f"b_bwd_prep{li}")
            dqkv = []
            for g, dil in enumerate(B_DILATIONS):
                stats = (do, lj, dd)
                if dil > 1:
                    stats = (_to_view(do, 0, 1024, dil, f"b_do_view{li}_{g}"), _to_view(lj, 0, HEAD_DIM, dil, f"b_lj_view{li}_{g}"),
                             _to_view(dd, 0, HEAD_DIM, dil, f"b_dd_view{li}_{g}"))
                d = _b_attn_bwd(*qkv[g], *stats, dil, f"b_attn_bwd{li}_{g}")
                dqkv.append(_from_view(d, dil, f"b_dqkv_nat{li}_{g}") if dil > 1 else d)
            dproj, dgains = _b_qk_bwd(proj, dqkv, dz, qk_gains, tabs, f"b_qk_bwd{li}")
            grads["b_q_gain"], grads["b_k_gain"] = dgains[None, :3], dgains[None, 3:]
            grads["b_w_in"] = _mm(h, dproj, "tn", CDT, f"b_dwin{li}")[None]
            dh = _mm(dproj, w["b_w_in"][j], "nt", F32, f"b_dh{li}")
        else:
            diff = sv[4]
            grads["c_w_out"] = _mm(y, dxc, "tn", CDT, f"c_dwout{li}")[None]
            dy = _mm(dxc, w["c_w_out"][j], "nt", CDT, f"c_dy{li}")
            ddiff, dz, dwg, dsc = _c_mid_bwd(diff, proj, w["c_w_grp"][j], w["c_scale"][j:j + 1], dy, f"c_mid_bwd{li}")
            grads["c_w_grp"], grads["c_scale"] = dwg[None], dsc
            dproj = _c_pool_bwd(ddiff, dz, f"c_pool_bwd{li}")
            grads["c_w_in"] = _mm(h, dproj, "tn", CDT, f"c_dwin{li}")[None]
            dh = _mm(dproj, w["c_w_in"][j], "nt", F32, f"c_dh{li}")
        dx, dxc, dng = _rms_bwd(xin, dh, dx, w["norm_gain"][li:li + 1], f"rms_bwd{li}")
        grads["norm_gain"][li] = dng[0]
    for name in ("norm_gain", "a_w_in", "a_v_gain", "a_w_s", "a_b_s", "a_w_out"):
        grads[name] = jnp.stack(grads[name])
    return sq, dx, grads


ROW = 1024
HALVES = ((("a_w_in", 3072), ("a_w_out", 1024), ("b_w_out", 256)), (("b_w_in", 2560), ("c_w_in", 1024), ("c_w_grp", 256), ("c_w_out", 512)))
SMALL_OF_HALF = ("a_v_gain", "c_scale")
BIG_HALF = 4352
SMALL_PAD = 16
W_HALF = BIG_HALF + SMALL_PAD
W_ROWS = 2 * W_HALF
REP = (("norm_gain", (4, 1024)), ("a_w_s", (2, 8, 128, 128)), ("a_b_s", (2, 8, 128)), ("b_q_gain", (1, 3, 128)), ("b_k_gain", (1, 3, 128)))
REP_CORE = 48
REP_PART = 2 * REP_CORE
G_SMALL, G_REP = BIG_HALF, BIG_HALF + SMALL_PAD
G_HALF = G_REP + REP_CORE
G_ROWS = 2 * G_HALF
SHARDED = ("a_w_in", "a_v_gain", "a_w_out", "b_w_in", "b_w_out", "c_w_in", "c_w_grp", "c_scale", "c_w_out")
BLOCK_SHAPES = {"a_w_in": (2, 1024, 1536), "a_v_gain": (2, 512), "a_w_out": (2, 512, 1024), "b_w_in": (1, 1024, 2560),
                "b_w_out": (1, 256, 1024), "c_w_in": (1, 1024, 1024), "c_w_grp": (1, 4, 128, 512), "c_scale": (1, 512),
                "c_w_out": (1, 512, 1024)}
SHARD_AXIS = {"a_w_in": 2, "a_v_gain": 1, "a_w_out": 1, "b_w_in": 2, "b_w_out": 1, "c_w_in": 2, "c_w_grp": 2, "c_scale": 1, "c_w_out": 1}
WEIGHTS = ("norm_gain", "a_w_in", "a_v_gain", "a_w_s", "a_b_s", "a_w_out", "b_w_in", "b_q_gain", "b_k_gain", "b_w_out",
           "c_w_in", "c_w_grp", "c_scale", "c_w_out")


def _rows(a, rows):
    a = a.reshape(-1)
    return jnp.pad(a, (0, rows * ROW - a.shape[0])).reshape(rows, ROW)


def _pack_blocks(blocks, rep_part, dtype):
    parts = []
    for h in range(2):
        parts += [blocks[n].astype(dtype).reshape(r, ROW) for n, r in HALVES[h]]
        parts += [_rows(blocks[SMALL_OF_HALF[h]].astype(dtype), SMALL_PAD), rep_part[h * REP_CORE:(h + 1) * REP_CORE].astype(dtype)]
    return jnp.concatenate(parts, axis=0)


def _unpack_blocks(lo, hi):
    out = {}
    for h, slab in enumerate((lo, hi)):
        off = 0
        for n, r in HALVES[h]:
            out[n] = slab[off:off + r].reshape(BLOCK_SHAPES[n])
            off += r
    out["a_v_gain"] = lo[G_SMALL].reshape(2, 512)
    out["c_scale"] = hi[G_SMALL, :512].reshape(1, 512)
    return out


def _pack_rep(tree):
    return _rows(jnp.concatenate([tree[n].astype(F32).reshape(-1) for n, _ in REP]), N_CHIPS * REP_PART)


def _unpack_rep(slab):
    flat, out, off = slab.reshape(-1), {}, 0
    for n, shape in REP:
        size = math.prod(shape)
        out[n] = flat[off:off + size].reshape(shape)
        off += size
    return out


def _shard(full, name, k):
    axis, size = SHARD_AXIS[name], BLOCK_SHAPES[name][SHARD_AXIS[name]]
    return lax.slice_in_dim(full, k * size, (k + 1) * size, axis=axis)


def _pack_weights(blocks):
    gains = jnp.concatenate([blocks["a_v_gain"].reshape(-1), blocks["c_scale"].reshape(-1)])
    parts = [blocks[n].astype(CDT).reshape(r, ROW) for n, r in HALVES[0]]
    parts.append(_rows(lax.bitcast_convert_type(gains, CDT), SMALL_PAD))
    parts += [blocks[n].astype(CDT).reshape(r, ROW) for n, r in HALVES[1]]
    parts.append(jnp.zeros((SMALL_PAD, ROW), CDT))
    return jnp.concatenate(parts, axis=0)


def _unpack_weights(gathered):
    out = {}
    for h in range(2):
        off = h * W_HALF
        for n, r in HALVES[h]:
            out[n] = jnp.concatenate([gathered[k, off:off + r].reshape(BLOCK_SHAPES[n]) for k in range(N_CHIPS)], axis=SHARD_AXIS[n])
            off += r
    gains = lax.bitcast_convert_type(gathered[:, BIG_HALF:BIG_HALF + 3].reshape(N_CHIPS, 1536, 2), F32)
    out["a_v_gain"] = jnp.concatenate([gains[k, :1024].reshape(2, 512) for k in range(N_CHIPS)], axis=1)
    out["c_scale"] = jnp.concatenate([gains[k, 1024:].reshape(1, 512) for k in range(N_CHIPS)], axis=1)
    return out


def _place():
    x, y, c = lax.axis_index("x"), lax.axis_index("y"), lax.axis_index("c")
    chips = [(1 - x, y), (x, 1 - y), (1 - x, 1 - y)]
    return x, y, c, 2 * x + y, (x, y, 1 - c), chips


def _remote(src, dst, sems, j, to):
    send_sems, recv_sems = sems
    return pltpu.make_async_remote_copy(src_ref=src, dst_ref=dst, send_sem=send_sems.at[j], recv_sem=recv_sems.at[j],
                                        device_id=to, device_id_type=MESH)


def _comm_call(body, name, out_shape, n_sems, *args):
    return pl.pallas_call(
        body, name=name, in_specs=[ANY] * len(args), out_specs=ANY, out_shape=out_shape,
        scratch_shapes=[pltpu.SemaphoreType.DMA((n_sems,)), pltpu.SemaphoreType.DMA((n_sems,))],
    )(*args)


def _allgather_weights(wp):
    half = W_HALF

    def body(w_ref, o_ref, send_sems, recv_sems):
        x, y, c, me, sibling, chips = _place()
        sems = (send_sems, recv_sems)
        mine = pl.ds(pl.multiple_of(c * half, 16), half)
        other = pl.ds(pl.multiple_of((1 - c) * half, 16), half)
        first = [_remote(w_ref.at[mine], o_ref.at[me, mine], sems, j, (*chip, c)) for j, chip in enumerate(chips)]
        for cp in first:
            cp.start()
        passed = []
        for j, (cx, cy) in enumerate(chips):
            landed = o_ref.at[2 * cx + cy, mine]
            _remote(landed, landed, sems, j, sibling).wait_recv()
            passed.append(_remote(landed, landed, sems, 3 + j, sibling))
            passed[-1].start()
        for j, (cx, cy) in enumerate(chips):
            landed = o_ref.at[2 * cx + cy, other]
            _remote(landed, landed, sems, 3 + j, sibling).wait_recv()
        for cp in first + passed:
            cp.wait_send()

    return _comm_call(body, "allgather_weights", jax.ShapeDtypeStruct((N_CHIPS, W_ROWS, ROW), wp.dtype), 6, wp)


def _swap_halves(gp):
    def body(g_ref, o_ref, send_sems, recv_sems):
        x, y, c, me, sibling, chips = _place()
        theirs = g_ref.at[:, pl.ds(pl.multiple_of((1 - c) * G_HALF, 16), G_HALF), :]
        cp = _remote(theirs, o_ref, (send_sems, recv_sems), 0, sibling)
        cp.start()
        cp.wait()

    return _comm_call(body, "grad_swap_halves", jax.ShapeDtypeStruct((N_CHIPS, G_HALF, ROW), gp.dtype), 1, gp)


def _scatter_shards(s1):
    def body(s_ref, o_ref, send_sems, recv_sems):
        x, y, c, me, sibling, chips = _place()
        sems = (send_sems, recv_sems)
        sent = [_remote(s_ref.at[2 * cx + cy], o_ref.at[me], sems, j, (cx, cy, c)) for j, (cx, cy) in enumerate(chips)]
        for cp in sent:
            cp.start()
        for j, (cx, cy) in enumerate(chips):
            slot = o_ref.at[2 * cx + cy]
            _remote(slot, slot, sems, j, sibling).wait_recv()
        for cp in sent:
            cp.wait_send()

    return _comm_call(body, "grad_scatter_shards", jax.ShapeDtypeStruct((N_CHIPS, G_HALF, ROW), s1.dtype), 3, s1)


def _share_halves(out4):
    def body(h_ref, o_ref, send_sems, recv_sems):
        x, y, c, me, sibling, chips = _place()
        cp = _remote(h_ref, o_ref, (send_sems, recv_sems), 0, sibling)
        cp.start()
        cp.wait()

    return _comm_call(body, "share_halves", jax.ShapeDtypeStruct(out4.shape, out4.dtype), 1, out4)


def _allgather_rep(rep4):
    def body(r_ref, o_ref, send_sems, recv_sems):
        x, y, c, me, sibling, chips = _place()
        sems = (send_sems, recv_sems)
        sent = [_remote(r_ref, o_ref.at[me], sems, j, (*chip, c)) for j, chip in enumerate(chips)]
        for cp in sent:
            cp.start()
        for j, (cx, cy) in enumerate(chips):
            slot = o_ref.at[2 * cx + cy]
            _remote(slot, slot, sems, j, sibling).wait_recv()
        for cp in sent:
            cp.wait_send()

    return _comm_call(body, "allgather_rep", jax.ShapeDtypeStruct((N_CHIPS,) + rep4.shape, rep4.dtype), 3, rep4)


ADAM_TILE = 192


def _add_halves(a, b):
    def body(a_ref, b_ref, o_ref):
        o_ref[...] = (a_ref[...].astype(F32) + b_ref[...].astype(F32)).astype(o_ref.dtype)

    blk = pl.BlockSpec((1, ADAM_TILE, ROW), lambda k, i: (k, i, 0))
    return pl.pallas_call(
        body, name="grad_add_halves", grid=(N_CHIPS, G_HALF // ADAM_TILE), in_specs=[blk, blk], out_specs=blk,
        out_shape=jax.ShapeDtypeStruct(a.shape, a.dtype), compiler_params=_cparams("parallel", "parallel"),
    )(a, b)


def _adamw(me, parts, own, w, m, v):
    def body(me_ref, p_ref, own_ref, w_ref, m_ref, v_ref, o_ref):
        g = jnp.zeros((ADAM_TILE, ROW), F32)
        for k in range(N_CHIPS):
            g = g + jnp.where(me_ref[0] == k, own_ref[0], p_ref[k]).astype(F32)
        m2 = ADAM_B1 * m_ref[...] + (1.0 - ADAM_B1) * g
        v2 = ADAM_B2 * v_ref[...] + (1.0 - ADAM_B2) * jnp.square(g)
        m_hat = m2 / (1.0 - ADAM_B1 ** ADAM_STEP)
        v_hat = v2 / (1.0 - ADAM_B2 ** ADAM_STEP)
        o_ref[0] = g
        o_ref[1] = -ADAM_LR * (m_hat / (jnp.sqrt(v_hat) + ADAM_EPS) + ADAM_WD * w_ref[...])
        o_ref[2] = m2
        o_ref[3] = v2

    blk = pl.BlockSpec((ADAM_TILE, ROW), lambda i, me_ref: (i, 0))
    blk4 = pl.BlockSpec((4, ADAM_TILE, ROW), lambda i, me_ref: (0, i, 0))
    mine = pl.BlockSpec((1, ADAM_TILE, ROW), lambda i, me_ref: (me_ref[0], i, 0))
    return pl.pallas_call(
        body, name="adamw",
        grid_spec=pltpu.PrefetchScalarGridSpec(num_scalar_prefetch=1, grid=(G_HALF // ADAM_TILE,),
                                               in_specs=[blk4, mine, blk, blk, blk], out_specs=blk4),
        out_shape=jax.ShapeDtypeStruct((4, G_HALF, ROW), F32), compiler_params=_cparams("parallel"),
    )(me.reshape(1).astype(jnp.int32), parts, own, w, m, v)


def kernel(x, norm_gain, a_w_in, a_v_gain, a_w_s, a_b_s, a_w_out, b_w_in, b_q_gain, b_k_gain, b_w_out, c_w_in, c_w_grp, c_scale, c_w_out, loss_target, m_norm_gain, m_a_w_in, m_a_v_gain, m_a_w_s, m_a_b_s, m_a_w_out, m_b_w_in, m_b_q_gain, m_b_k_gain, m_b_w_out, m_c_w_in, m_c_w_grp, m_c_scale, m_c_w_out, v_norm_gain, v_a_w_in, v_a_v_gain, v_a_w_s, v_a_b_s, v_a_w_out, v_b_w_in, v_b_q_gain, v_b_k_gain, v_b_w_out, v_c_w_in, v_c_w_grp, v_c_scale, v_c_w_out):
    wts = dict(norm_gain=norm_gain, a_w_in=a_w_in, a_v_gain=a_v_gain, a_w_s=a_w_s, a_b_s=a_b_s, a_w_out=a_w_out, b_w_in=b_w_in,
               b_q_gain=b_q_gain, b_k_gain=b_k_gain, b_w_out=b_w_out, c_w_in=c_w_in, c_w_grp=c_w_grp, c_scale=c_scale, c_w_out=c_w_out)
    mom1 = dict(norm_gain=m_norm_gain, a_w_in=m_a_w_in, a_v_gain=m_a_v_gain, a_w_s=m_a_w_s, a_b_s=m_a_b_s, a_w_out=m_a_w_out,
                b_w_in=m_b_w_in, b_q_gain=m_b_q_gain, b_k_gain=m_b_k_gain, b_w_out=m_b_w_out, c_w_in=m_c_w_in, c_w_grp=m_c_w_grp,
                c_scale=m_c_scale, c_w_out=m_c_w_out)
    mom2 = dict(norm_gain=v_norm_gain, a_w_in=v_a_w_in, a_v_gain=v_a_v_gain, a_w_s=v_a_w_s, a_b_s=v_a_b_s, a_w_out=v_a_w_out,
                b_w_in=v_b_w_in, b_q_gain=v_b_q_gain, b_k_gain=v_b_k_gain, b_w_out=v_b_w_out, c_w_in=v_c_w_in, c_w_grp=v_c_w_grp,
                c_scale=v_c_scale, c_w_out=v_c_w_out)
    axes = ("x", "y", "c")
    me = 2 * lax.axis_index("x") + lax.axis_index("y")
    core = lax.axis_index("c")

    wp = _pack_weights(wts)
    gathered = lax.dynamic_update_slice_in_dim(_allgather_weights(wp), wp[None], me, axis=0)
    full = _unpack_weights(gathered)
    for n, _ in REP:
        full[n] = wts[n]
    sq, grad_x, grads = _local_step(x[0], loss_target[0], full)
    loss = lax.psum(0.5 * jnp.sum(sq) / D_MODEL, axes)

    rep_g = _pack_rep(grads)
    gp = jnp.stack([_pack_blocks({n: _shard(grads[n], n, k) for n in SHARDED}, rep_g[k * REP_PART:(k + 1) * REP_PART], CDT)
                    for k in range(N_CHIPS)])
    from_sibling = _swap_halves(gp)
    my_rows = lax.dynamic_slice_in_dim(gp, core * G_HALF, G_HALF, axis=1)
    chip_sums = _add_halves(my_rows, from_sibling)
    parts = _scatter_shards(chip_sums)

    def my_half(tree):
        rep_part = lax.dynamic_slice_in_dim(_pack_rep(tree), me * REP_PART, REP_PART, axis=0)
        return lax.dynamic_slice_in_dim(_pack_blocks(tree, rep_part, F32), core * G_HALF, G_HALF, axis=0)

    mine = _adamw(me, parts, chip_sums, my_half(wts), my_half(mom1), my_half(mom2))
    theirs = _share_halves(mine)
    lo = jnp.where(core == 0, mine, theirs)
    hi = jnp.where(core == 0, theirs, mine)
    rep_mine = jnp.concatenate([lo[:, G_REP:], hi[:, G_REP:]], axis=1)
    rep_all = lax.dynamic_update_slice_in_dim(_allgather_rep(rep_mine), rep_mine[None], me, axis=0)

    outs = []
    for q in range(4):
        tree = _unpack_blocks(lo[q], hi[q])
        tree.update(_unpack_rep(rep_all[:, q].reshape(N_CHIPS * REP_PART, ROW)))
        outs.append(tree)
    return (loss, grad_x[None], *[t[n] for t in outs for n in WEIGHTS])
```

```python
import functools
import math

import jax
import jax.numpy as jnp
from jax import lax
from jax.experimental import pallas as pl
from jax.experimental.pallas import tpu as pltpu

F32 = jnp.float32
CDT = jnp.bfloat16

D_MODEL = 1024
EPS = 1e-6
CHUNK = 128
A_WIDTH = 2048
A_GROUPS = 8
A_GROUP_DIM = 256
HEAD_DIM = 128
B_HEADS = 8
B_DILATIONS = (1, 4, 16)
B_QK = 6144
B_IN = 10240
ROPE_HALF = 16
ROPE_THETA = 500000.0
POOL_SIZES = (2, 4, 8, 16)
POOL_HALO = 16
C_WIDTH = 2048
C_GROUP = 512
N_CHIPS = 4

ADAM_LR = 0.001
ADAM_B1 = 0.9
ADAM_B2 = 0.999
ADAM_EPS = 1e-08
ADAM_WD = 0.01
ADAM_STEP = 10

VMEM_LIMIT = 48 * 1024 * 1024
ANY = pl.BlockSpec(memory_space=pl.ANY)
MESH = pl.DeviceIdType.MESH

NN = (((1,), (0,)), ((), ()))
NT = (((1,), (1,)), ((), ()))
TN = (((0,), (0,)), ((), ()))


def _cparams(*sem):
    return pltpu.CompilerParams(dimension_semantics=sem, vmem_limit_bytes=VMEM_LIMIT)


def _dot(a, b, dims=NN):
    return lax.dot_general(a, b, dims, preferred_element_type=F32)


def _sigmoid(z):
    return 1.0 / (1.0 + jnp.exp(-z))


def _lane_sums(v):
    ones = jnp.ones((HEAD_DIM, HEAD_DIM), jnp.bfloat16)
    hi = v.astype(jnp.bfloat16)
    lo = (v - hi.astype(F32)).astype(jnp.bfloat16)
    return _dot(hi, ones) + _dot(lo, ones)


def _mm(a, b, mode, out_dtype, name, mnk, tiles, b_spec=None, o_spec=None, o_shape=None, add=None):
    m, n, k = mnk
    tm, tn, tk = min(tiles[0], m), min(tiles[1], n), min(tiles[2], k)
    nk = k // tk
    a_spec = {"nn": pl.BlockSpec((tm, tk), lambda i, j, q: (i, q)),
              "nt": pl.BlockSpec((tm, tk), lambda i, j, q: (i, q)),
              "tn": pl.BlockSpec((tk, tm), lambda i, j, q: (q, i))}[mode]
    if b_spec is None:
        b_spec = {"nn": pl.BlockSpec((tk, tn), lambda i, j, q: (q, j)),
                  "nt": pl.BlockSpec((tn, tk), lambda i, j, q: (j, q)),
                  "tn": pl.BlockSpec((tk, tn), lambda i, j, q: (q, j))}[mode]
    if o_spec is None:
        o_spec, o_shape = pl.BlockSpec((tm, tn), lambda i, j, q: (i, j)), (m, n)
    dims = {"nn": NN, "nt": NT, "tn": TN}[mode]
    has_add = add is not None

    def body(*refs):
        a_ref, b_ref = refs[0], refs[1]
        o_ref = refs[3] if has_add else refs[2]
        p = _dot(a_ref[...], b_ref[...], dims)

        def finish(v):
            if has_add:
                v = v + refs[2][...]
            o_ref[...] = v.astype(o_ref.dtype)

        if nk == 1:
            finish(p)
        else:
            acc_ref = refs[-1]
            q = pl.program_id(2)

            @pl.when(q == 0)
            def _():
                acc_ref[...] = p

            @pl.when(q > 0)
            def _():
                acc_ref[...] += p

            @pl.when(q == nk - 1)
            def _():
                finish(acc_ref[...])

    in_specs = [a_spec, b_spec]
    args = [a, b]
    if has_add:
        in_specs.append(pl.BlockSpec((tm, tn), lambda i, j, q: (i, j)))
        args.append(add)
    return pl.pallas_call(
        body, name=name, grid=(m // tm, n // tn, nk), in_specs=in_specs, out_specs=o_spec,
        out_shape=jax.ShapeDtypeStruct(o_shape, out_dtype),
        scratch_shapes=[pltpu.VMEM((tm, tn), F32)] if nk > 1 else [],
        compiler_params=_cparams("parallel", "parallel", "arbitrary"),
    )(*args)


def _rms_fwd(x, g, name, tq=512):
    s, d = x.shape

    def body(x_ref, g_ref, h_ref):
        xv = x_ref[...]
        r = lax.rsqrt(jnp.mean(xv * xv, axis=-1, keepdims=True) + EPS)
        h_ref[...] = (xv * r * g_ref[...]).astype(h_ref.dtype)

    return pl.pallas_call(
        body, name=name, grid=(s // tq,),
        in_specs=[pl.BlockSpec((tq, d), lambda i: (i, 0)), pl.BlockSpec((1, d), lambda i: (0, 0))],
        out_specs=pl.BlockSpec((tq, d), lambda i: (i, 0)),
        out_shape=jax.ShapeDtypeStruct((s, d), CDT), compiler_params=_cparams("parallel"),
    )(x, g)


def _rms_bwd(x, dh, dxo, g, name, tq=512):
    s, d = x.shape

    def body(x_ref, dh_ref, dxo_ref, g_ref, dx_ref, dxc_ref, dg_ref):
        xv = x_ref[...]
        dhv = dh_ref[...].astype(F32)
        r = lax.rsqrt(jnp.mean(xv * xv, axis=-1, keepdims=True) + EPS)
        gd = dhv * g_ref[...]
        dx = dxo_ref[...] + r * gd - xv * (r * r * r) * jnp.mean(gd * xv, axis=-1, keepdims=True)
        dx_ref[...] = dx
        dxc_ref[...] = dx.astype(dxc_ref.dtype)

        @pl.when(pl.program_id(0) == 0)
        def _():
            dg_ref[...] = jnp.zeros_like(dg_ref)

        dg_ref[...] += jnp.sum(dhv * xv * r, axis=0, keepdims=True)

    blk = pl.BlockSpec((tq, d), lambda i: (i, 0))
    vec = pl.BlockSpec((1, d), lambda i: (0, 0))
    return pl.pallas_call(
        body, name=name, grid=(s // tq,), in_specs=[blk, blk, blk, vec], out_specs=[blk, blk, vec],
        out_shape=[jax.ShapeDtypeStruct((s, d), F32), jax.ShapeDtypeStruct((s, d), CDT), jax.ShapeDtypeStruct((1, d), F32)],
        compiler_params=_cparams("arbitrary"),
    )(x, dh, dxo, g)


def _loss_bwd(y, target, name, tq=512):
    s, d = y.shape

    def body(y_ref, t_ref, dx_ref, dxc_ref, sq_ref):
        err = y_ref[...] - t_ref[...]
        dx = err * (1.0 / d)
        dx_ref[...] = dx
        dxc_ref[...] = dx.astype(dxc_ref.dtype)

        @pl.when(pl.program_id(0) == 0)
        def _():
            sq_ref[...] = jnp.zeros_like(sq_ref)

        sq_ref[...] += jnp.sum(err * err, axis=0, keepdims=True)

    blk = pl.BlockSpec((tq, d), lambda i: (i, 0))
    vec = pl.BlockSpec((1, d), lambda i: (0, 0))
    return pl.pallas_call(
        body, name=name, grid=(s // tq,), in_specs=[blk, blk], out_specs=[blk, blk, vec],
        out_shape=[jax.ShapeDtypeStruct((s, d), F32), jax.ShapeDtypeStruct((s, d), CDT), jax.ShapeDtypeStruct((1, d), F32)],
        compiler_params=_cparams("arbitrary"),
    )(y, target)


def _tril_mask():
    row = lax.broadcasted_iota(jnp.int32, (CHUNK, CHUNK), 0)
    col = lax.broadcasted_iota(jnp.int32, (CHUNK, CHUNK), 1)
    return row >= col


def _a_mid_fwd(proj, v_gain, w_s, b_s_t, name, tq=256):
    s = proj.shape[0]

    def body(p_ref, vg_ref, ws_ref, bs_ref, y_ref):
        vraw = p_ref[:, A_WIDTH:2 * A_WIDTH].astype(F32)
        r = lax.rsqrt(jnp.mean(vraw * vraw, axis=-1, keepdims=True) + EPS)
        vn = (vraw * r * vg_ref[...]).astype(CDT)
        tri = _tril_mask()
        for g in range(A_GROUPS):
            w = jnp.where(tri, ws_ref[g], 0.0).astype(CDT)
            bias = bs_ref[:, g:g + 1]
            cols = slice(g * A_GROUP_DIM, (g + 1) * A_GROUP_DIM)
            zcols = slice(2 * A_WIDTH + g * A_GROUP_DIM, 2 * A_WIDTH + (g + 1) * A_GROUP_DIM)
            for c in range(tq // CHUNK):
                rows = slice(c * CHUNK, (c + 1) * CHUNK)
                mixed = _dot(w, vn[rows, cols]) + bias
                u = p_ref[rows, cols].astype(F32)
                z = p_ref[rows, zcols].astype(F32)
                y_ref[rows, cols] = (u * mixed * (z * _sigmoid(z))).astype(y_ref.dtype)

    return pl.pallas_call(
        body, name=name, grid=(s // tq,),
        in_specs=[pl.BlockSpec((tq, 3 * A_WIDTH), lambda i: (i, 0)), pl.BlockSpec((1, A_WIDTH), lambda i: (0, 0)),
                  pl.BlockSpec((A_GROUPS, CHUNK, CHUNK), lambda i: (0, 0, 0)), pl.BlockSpec((CHUNK, A_GROUPS), lambda i: (0, 0))],
        out_specs=pl.BlockSpec((tq, A_WIDTH), lambda i: (i, 0)),
        out_shape=jax.ShapeDtypeStruct((s, A_WIDTH), CDT), compiler_params=_cparams("parallel"),
    )(proj, v_gain, w_s, b_s_t)


def _a_mid_bwd(proj, dy, v_gain, w_s, b_s_t, name, tq=256):
    s = proj.shape[0]

    def body(p_ref, dy_ref, vg_ref, ws_ref, bs_ref, dp_ref, dws_ref, dbs_ref, dvg_ref, dvn_ref):
        @pl.when(pl.program_id(0) == 0)
        def _():
            dws_ref[...] = jnp.zeros_like(dws_ref)
            dbs_ref[...] = jnp.zeros_like(dbs_ref)
            dvg_ref[...] = jnp.zeros_like(dvg_ref)

        vraw = p_ref[:, A_WIDTH:2 * A_WIDTH].astype(F32)
        r = lax.rsqrt(jnp.mean(vraw * vraw, axis=-1, keepdims=True) + EPS)
        vhat = vraw * r
        vg = vg_ref[...]
        vn = (vhat * vg).astype(CDT)
        tri = _tril_mask()
        lane = lax.broadcasted_iota(jnp.int32, (CHUNK, A_GROUPS), 1)
        dbs = jnp.zeros((CHUNK, A_GROUPS), F32)
        for g in range(A_GROUPS):
            w = jnp.where(tri, ws_ref[g], 0.0).astype(CDT)
            bias = bs_ref[:, g:g + 1]
            cols = slice(g * A_GROUP_DIM, (g + 1) * A_GROUP_DIM)
            zcols = slice(2 * A_WIDTH + g * A_GROUP_DIM, 2 * A_WIDTH + (g + 1) * A_GROUP_DIM)
            dws = jnp.zeros((CHUNK, CHUNK), F32)
            for c in range(tq // CHUNK):
                rows = slice(c * CHUNK, (c + 1) * CHUNK)
                vn_g = vn[rows, cols]
                mixed = _dot(w, vn_g) + bias
                u = p_ref[rows, cols].astype(F32)
                z = p_ref[rows, zcols].astype(F32)
                dyv = dy_ref[rows, cols].astype(F32)
                sg = _sigmoid(z)
                sz = z * sg
                dyu = dyv * u
                dmixed = dyu * sz
                dp_ref[rows, cols] = (dyv * mixed * sz).astype(dp_ref.dtype)
                dp_ref[rows, zcols] = (dyu * mixed * (sg * (1.0 + z * (1.0 - sg)))).astype(dp_ref.dtype)
                dmc = dmixed.astype(CDT)
                dws = dws + _dot(dmc, vn_g, NT)
                dbs = dbs + jnp.where(lane == g, jnp.sum(dmixed, axis=-1, keepdims=True), 0.0)
                dvn_ref[rows, cols] = _dot(w, dmc, TN)
            dws_ref[g] += jnp.where(tri, dws, 0.0)
        dbs_ref[...] += dbs
        dvn = dvn_ref[...]
        gd = dvn * vg
        dvraw = r * gd - vraw * (r * r * r) * jnp.mean(gd * vraw, axis=-1, keepdims=True)
        dp_ref[:, A_WIDTH:2 * A_WIDTH] = dvraw.astype(dp_ref.dtype)
        dvg_ref[...] += jnp.sum(dvn * vhat, axis=0, keepdims=True)

    return pl.pallas_call(
        body, name=name, grid=(s // tq,),
        in_specs=[pl.BlockSpec((tq, 3 * A_WIDTH), lambda i: (i, 0)), pl.BlockSpec((tq, A_WIDTH), lambda i: (i, 0)),
                  pl.BlockSpec((1, A_WIDTH), lambda i: (0, 0)), pl.BlockSpec((A_GROUPS, CHUNK, CHUNK), lambda i: (0, 0, 0)),
                  pl.BlockSpec((CHUNK, A_GROUPS), lambda i: (0, 0))],
        out_specs=[pl.BlockSpec((tq, 3 * A_WIDTH), lambda i: (i, 0)), pl.BlockSpec((A_GROUPS, CHUNK, CHUNK), lambda i: (0, 0, 0)),
                   pl.BlockSpec((CHUNK, A_GROUPS), lambda i: (0, 0)), pl.BlockSpec((1, A_WIDTH), lambda i: (0, 0))],
        out_shape=[jax.ShapeDtypeStruct((s, 3 * A_WIDTH), CDT), jax.ShapeDtypeStruct((A_GROUPS, CHUNK, CHUNK), F32),
                   jax.ShapeDtypeStruct((CHUNK, A_GROUPS), F32), jax.ShapeDtypeStruct((1, A_WIDTH), F32)],
        scratch_shapes=[pltpu.VMEM((tq, A_WIDTH), F32)],
        compiler_params=_cparams("arbitrary"),
    )(proj, dy, v_gain, w_s, b_s_t)


def _rope_tables(s):
    inv_freq = jnp.power(jnp.float32(ROPE_THETA), -jnp.arange(ROPE_HALF, dtype=F32) / ROPE_HALF)
    ang = jnp.arange(s, dtype=F32)[:, None] * inv_freq[None, :]
    cos, sin = jnp.cos(ang), jnp.sin(ang)
    rest = HEAD_DIM - 2 * ROPE_HALF
    t_c = jnp.concatenate([cos, cos, jnp.ones((s, rest), F32)], axis=1)
    t_a = jnp.concatenate([-sin, jnp.zeros((s, HEAD_DIM - ROPE_HALF), F32)], axis=1)
    t_b = jnp.concatenate([jnp.zeros((s, ROPE_HALF), F32), sin, jnp.zeros((s, rest), F32)], axis=1)
    return t_c, t_a, t_b


def _b_qk_fwd(proj, gains, tabs, name, tq=256):
    s = proj.shape[0]

    def body(p_ref, g_ref, tc_ref, ta_ref, tb_ref, o_ref):
        tc, ta, tb = tc_ref[...], ta_ref[...], tb_ref[...]
        for tg in range(6):
            gain = g_ref[tg:tg + 1, :]
            for h in range(B_HEADS):
                cols = slice(tg * 1024 + h * HEAD_DIM, tg * 1024 + (h + 1) * HEAD_DIM)
                xv = p_ref[:, cols].astype(F32)
                r = lax.rsqrt(_lane_sums(xv * xv) * (1.0 / HEAD_DIM) + EPS)
                xn = xv * r * gain
                y = xn * tc + pltpu.roll(xn, HEAD_DIM - ROPE_HALF, 1) * ta + pltpu.roll(xn, ROPE_HALF, 1) * tb
                o_ref[:, cols] = y.astype(o_ref.dtype)

    tab = pl.BlockSpec((tq, HEAD_DIM), lambda i: (i, 0))
    return pl.pallas_call(
        body, name=name, grid=(s // tq,),
        in_specs=[pl.BlockSpec((tq, B_QK), lambda i: (i, 0)), pl.BlockSpec((6, HEAD_DIM), lambda i: (0, 0)), tab, tab, tab],
        out_specs=pl.BlockSpec((tq, B_QK), lambda i: (i, 0)),
        out_shape=jax.ShapeDtypeStruct((s, B_QK), CDT), compiler_params=_cparams("parallel"),
    )(proj, gains, *tabs)


PERMUTE_BLOCK_BYTES = 4 * 1024 * 1024


def _view_rows(length, dil, width, itemsize):
    rows = 16
    while 2 * rows * dil * width * itemsize <= PERMUTE_BLOCK_BYTES and 2 * rows <= length:
        rows *= 2
    return rows


def _to_view(x, col, width, dil, name):
    s = x.shape[0]
    length = s // dil
    tl = _view_rows(length, dil, width, 4)
    lanes = HEAD_DIM
    nblk = width // lanes

    def body(x_ref, o_ref, slab_ref):
        for b in range(nblk):
            slab_ref[b] = x_ref[:, b * lanes:(b + 1) * lanes].astype(F32)
        for r in range(dil):
            for b in range(nblk):
                o_ref[:, r * width + b * lanes:r * width + (b + 1) * lanes] = (
                    slab_ref.at[b][pl.ds(r, tl, stride=dil), :].astype(o_ref.dtype))

    return pl.pallas_call(
        body, name=name, grid=(length // tl,),
        in_specs=[pl.BlockSpec((tl * dil, width), lambda i: (i, col))],
        out_specs=pl.BlockSpec((tl, dil * width), lambda i: (i, 0)),
        out_shape=jax.ShapeDtypeStruct((length, dil * width), x.dtype),
        scratch_shapes=[pltpu.VMEM((nblk, tl * dil, lanes), F32)],
        compiler_params=_cparams("parallel"),
    )(x)


def _from_view(v, dil, name):
    length, width = v.shape[0], v.shape[1] // dil
    tl = _view_rows(length, dil, width, 4)
    lanes = HEAD_DIM
    nblk = width // lanes

    def body(v_ref, o_ref, slab_ref):
        for r in range(dil):
            for b in range(nblk):
                slab_ref.at[b][pl.ds(r, tl, stride=dil), :] = v_ref[:, r * width + b * lanes:r * width + (b + 1) * lanes].astype(F32)
        for b in range(nblk):
            o_ref[:, b * lanes:(b + 1) * lanes] = slab_ref[b].astype(o_ref.dtype)

    return pl.pallas_call(
        body, name=name, grid=(length // tl,),
        in_specs=[pl.BlockSpec((tl, dil * width), lambda i: (i, 0))],
        out_specs=pl.BlockSpec((tl * dil, width), lambda i: (i, 0)),
        out_shape=jax.ShapeDtypeStruct((length * dil, width), v.dtype),
        scratch_shapes=[pltpu.VMEM((nblk, tl * dil, lanes), F32)],
        compiler_params=_cparams("parallel"),
    )(v)


def _b_attn_fwd(q, k, v, bases, dil, name):
    length = q.shape[0]
    nb = length // CHUNK
    scale = 1.0 / math.sqrt(HEAD_DIM)
    w = B_HEADS * HEAD_DIM
    qb, kb, vb = bases

    def body(q_ref, kc_ref, kp_ref, vc_ref, vp_ref, o_ref, lse_ref):
        n = pl.program_id(1)
        qi = lax.broadcasted_iota(jnp.int32, (CHUNK, 2 * CHUNK), 0)
        ki = lax.broadcasted_iota(jnp.int32, (CHUNK, 2 * CHUNK), 1)
        first_key = jnp.where(n > 0, 0, CHUNK)
        mask = (ki >= qi) & (ki <= qi + CHUNK) & (ki >= first_key)
        lane = lax.broadcasted_iota(jnp.int32, (CHUNK, HEAD_DIM), 1)
        lse_all = jnp.zeros((CHUNK, HEAD_DIM), F32)
        for h in range(B_HEADS):
            sl = slice(h * HEAD_DIM, (h + 1) * HEAD_DIM)
            k2 = jnp.concatenate([kp_ref[:, sl], kc_ref[:, sl]], axis=0)
            v2 = jnp.concatenate([vp_ref[:, sl], vc_ref[:, sl]], axis=0)
            sc = jnp.where(mask, _dot(q_ref[:, sl], k2, NT) * scale, -1e30)
            m = jnp.max(sc, axis=-1, keepdims=True)
            p = jnp.exp(sc - m)
            l = jnp.sum(p, axis=-1, keepdims=True)
            o_ref[:, sl] = _dot(p.astype(CDT), v2) / l
            lse_all = jnp.where(lane == h, m + jnp.log(l), lse_all)
        lse_ref[...] = lse_all

    prev = lambda n: jnp.maximum(n - 1, 0)
    blk = lambda f: pl.BlockSpec((CHUNK, w), f)
    return pl.pallas_call(
        body, name=name, grid=(dil, nb),
        in_specs=[blk(lambda r, n: (n, qb + r)), blk(lambda r, n: (n, kb + r)), blk(lambda r, n: (prev(n), kb + r)),
                  blk(lambda r, n: (n, vb + r)), blk(lambda r, n: (prev(n), vb + r))],
        out_specs=[blk(lambda r, n: (n, r)), pl.BlockSpec((CHUNK, HEAD_DIM), lambda r, n: (n, r))],
        out_shape=[jax.ShapeDtypeStruct((length, dil * w), F32), jax.ShapeDtypeStruct((length, dil * HEAD_DIM), F32)],
        compiler_params=_cparams("parallel", "parallel"),
    )(q, k, k, v, v)


def _b_combine(os_, lses, proj, name, tq=512):
    s = proj.shape[0]
    w = B_HEADS * HEAD_DIM

    def body(o0_ref, o1_ref, o2_ref, l0_ref, l1_ref, l2_ref, z_ref, y_ref, oj_ref, lj_ref):
        l0, l1, l2 = l0_ref[...], l1_ref[...], l2_ref[...]
        m = jnp.maximum(jnp.maximum(l0, l1), l2)
        lj = m + jnp.log(jnp.exp(l0 - m) + jnp.exp(l1 - m) + jnp.exp(l2 - m))
        lj_ref[...] = lj
        w0, w1, w2 = jnp.exp(l0 - lj), jnp.exp(l1 - lj), jnp.exp(l2 - lj)
        for h in range(B_HEADS):
            sl = slice(h * HEAD_DIM, (h + 1) * HEAD_DIM)
            o = w0[:, h:h + 1] * o0_ref[:, sl] + w1[:, h:h + 1] * o1_ref[:, sl] + w2[:, h:h + 1] * o2_ref[:, sl]
            z = z_ref[:, sl].astype(F32)
            oj_ref[:, sl] = o.astype(oj_ref.dtype)
            y_ref[:, sl] = (o * (z * _sigmoid(z))).astype(y_ref.dtype)

    blk = pl.BlockSpec((tq, w), lambda i: (i, 0))
    st = pl.BlockSpec((tq, HEAD_DIM), lambda i: (i, 0))
    return pl.pallas_call(
        body, name=name, grid=(s // tq,),
        in_specs=[blk, blk, blk, st, st, st, pl.BlockSpec((tq, w), lambda i: (i, 9))],
        out_specs=[blk, blk, st],
        out_shape=[jax.ShapeDtypeStruct((s, w), CDT), jax.ShapeDtypeStruct((s, w), CDT), jax.ShapeDtypeStruct((s, HEAD_DIM), F32)],
        compiler_params=_cparams("parallel"),
    )(*os_, *lses, proj)


def _b_bwd_prep(dy, oj, proj, name, tq=512):
    s = proj.shape[0]
    w = B_HEADS * HEAD_DIM

    def body(dy_ref, oj_ref, z_ref, do_ref, dz_ref, dd_ref):
        lane = lax.broadcasted_iota(jnp.int32, (tq, HEAD_DIM), 1)
        dd = jnp.zeros((tq, HEAD_DIM), F32)
        for h in range(B_HEADS):
            sl = slice(h * HEAD_DIM, (h + 1) * HEAD_DIM)
            z = z_ref[:, sl].astype(F32)
            dyv = dy_ref[:, sl].astype(F32)
            o = oj_ref[:, sl].astype(F32)
            sg = _sigmoid(z)
            do = dyv * (z * sg)
            do_ref[:, sl] = do.astype(do_ref.dtype)
            dz_ref[:, sl] = (dyv * o * (sg * (1.0 + z * (1.0 - sg)))).astype(dz_ref.dtype)
            dd = jnp.where(lane == h, jnp.sum(do * o, axis=-1, keepdims=True), dd)
        dd_ref[...] = dd

    blk = pl.BlockSpec((tq, w), lambda i: (i, 0))
    st = pl.BlockSpec((tq, HEAD_DIM), lambda i: (i, 0))
    return pl.pallas_call(
        body, name=name, grid=(s // tq,),
        in_specs=[blk, blk, pl.BlockSpec((tq, w), lambda i: (i, 9))], out_specs=[blk, blk, st],
        out_shape=[jax.ShapeDtypeStruct((s, w), CDT), jax.ShapeDtypeStruct((s, w), CDT), jax.ShapeDtypeStruct((s, HEAD_DIM), F32)],
        compiler_params=_cparams("parallel"),
    )(dy, oj, proj)


def _b_attn_bwd(q, k, v, bases, do, lj, dd, dil, name):
    length = q.shape[0]
    nb = length // CHUNK
    scale = 1.0 / math.sqrt(HEAD_DIM)
    w = B_HEADS * HEAD_DIM
    qb, kb, vb = bases

    def body(qj_ref, qn_ref, k_ref, v_ref, doj_ref, don_ref, lj_ref, ln_ref, dj_ref, dn_ref, out_ref, carry_ref):
        j = pl.program_id(1)

        @pl.when(j == 0)
        def _():
            carry_ref[...] = jnp.zeros_like(carry_ref)

        qi = lax.broadcasted_iota(jnp.int32, (2 * CHUNK, CHUNK), 0)
        ki = lax.broadcasted_iota(jnp.int32, (2 * CHUNK, CHUNK), 1)
        no_next = jnp.where(j + 1 < nb, 0, 2 * CHUNK)
        mask = ((qi < CHUNK) & (ki <= qi)) | ((qi >= CHUNK) & (ki >= qi - CHUNK + no_next))
        for h in range(B_HEADS):
            sl = slice(h * HEAD_DIM, (h + 1) * HEAD_DIM)
            q2 = jnp.concatenate([qj_ref[:, sl], qn_ref[:, sl]], axis=0)
            do2 = jnp.concatenate([doj_ref[:, sl], don_ref[:, sl]], axis=0)
            lse2 = jnp.concatenate([lj_ref[:, h:h + 1], ln_ref[:, h:h + 1]], axis=0)
            d2 = jnp.concatenate([dj_ref[:, h:h + 1], dn_ref[:, h:h + 1]], axis=0)
            k = k_ref[:, sl]
            v = v_ref[:, sl]
            sc = _dot(q2, k, NT) * scale
            p = jnp.where(mask, jnp.exp(sc - lse2), 0.0)
            dp = _dot(do2, v, NT)
            ds = (p * (dp - d2) * scale).astype(CDT)
            dq2 = _dot(ds, k)
            out_ref[:, sl] = (carry_ref[:, sl] + dq2[:CHUNK]).astype(out_ref.dtype)
            carry_ref[:, sl] = dq2[CHUNK:]
            out_ref[:, w + h * HEAD_DIM:w + (h + 1) * HEAD_DIM] = _dot(ds, q2, TN).astype(out_ref.dtype)
            out_ref[:, 2 * w + h * HEAD_DIM:2 * w + (h + 1) * HEAD_DIM] = _dot(p.astype(CDT), do2, TN).astype(out_ref.dtype)

    nxt = lambda j: jnp.minimum(j + 1, nb - 1)
    blk = lambda f: pl.BlockSpec((CHUNK, w), f)
    st = lambda f: pl.BlockSpec((CHUNK, HEAD_DIM), f)
    return pl.pallas_call(
        body, name=name, grid=(dil, nb),
        in_specs=[blk(lambda r, j: (j, qb + r)), blk(lambda r, j: (nxt(j), qb + r)),
                  blk(lambda r, j: (j, kb + r)), blk(lambda r, j: (j, vb + r)),
                  blk(lambda r, j: (j, r)), blk(lambda r, j: (nxt(j), r)),
                  st(lambda r, j: (j, r)), st(lambda r, j: (nxt(j), r)), st(lambda r, j: (j, r)), st(lambda r, j: (nxt(j), r))],
        out_specs=pl.BlockSpec((CHUNK, 3 * w), lambda r, j: (j, r)),
        out_shape=jax.ShapeDtypeStruct((length, dil * 3 * w), CDT),
        scratch_shapes=[pltpu.VMEM((CHUNK, w), F32)],
        compiler_params=_cparams("parallel", "arbitrary"),
    )(q, q, k, v, do, do, lj, lj, dd, dd)


def _b_qk_bwd(proj, dqkv, dz, gains, tabs, name, tq=256):
    s = proj.shape[0]
    w = B_HEADS * HEAD_DIM

    def body(p_ref, d0_ref, d1_ref, d2_ref, dz_ref, g_ref, tc_ref, ta_ref, tb_ref, dp_ref, dg_ref):
        @pl.when(pl.program_id(0) == 0)
        def _():
            dg_ref[...] = jnp.zeros_like(dg_ref)

        tc, ta, tb = tc_ref[...], ta_ref[...], tb_ref[...]
        d_refs = (d0_ref, d1_ref, d2_ref)
        for g in range(3):
            for t in range(2):
                tg = t * 3 + g
                gain = g_ref[tg:tg + 1, :]
                dgain = jnp.zeros((1, HEAD_DIM), F32)
                for h in range(B_HEADS):
                    cols = slice(tg * w + h * HEAD_DIM, tg * w + (h + 1) * HEAD_DIM)
                    xv = p_ref[:, cols].astype(F32)
                    r = lax.rsqrt(_lane_sums(xv * xv) * (1.0 / HEAD_DIM) + EPS)
                    dyv = d_refs[g][:, t * w + h * HEAD_DIM:t * w + (h + 1) * HEAD_DIM].astype(F32)
                    dxn = dyv * tc + pltpu.roll(dyv * ta, ROPE_HALF, 1) + pltpu.roll(dyv * tb, HEAD_DIM - ROPE_HALF, 1)
                    gd = dxn * gain
                    dx = r * gd - xv * (r * r * r) * (_lane_sums(gd * xv) * (1.0 / HEAD_DIM))
                    dp_ref[:, cols] = dx.astype(dp_ref.dtype)
                    dgain = dgain + jnp.sum(dxn * xv * r, axis=0, keepdims=True)
                dg_ref[tg:tg + 1, :] += dgain
            dp_ref[:, (6 + g) * w:(7 + g) * w] = d_refs[g][:, 2 * w:3 * w]
        dp_ref[:, 9 * w:10 * w] = dz_ref[...]

    tab = pl.BlockSpec((tq, HEAD_DIM), lambda i: (i, 0))
    dblk = pl.BlockSpec((tq, 3 * w), lambda i: (i, 0))
    return pl.pallas_call(
        body, name=name, grid=(s // tq,),
        in_specs=[pl.BlockSpec((tq, B_QK), lambda i: (i, 0)), dblk, dblk, dblk, pl.BlockSpec((tq, w), lambda i: (i, 0)),
                  pl.BlockSpec((6, HEAD_DIM), lambda i: (0, 0)), tab, tab, tab],
        out_specs=[pl.BlockSpec((tq, B_IN), lambda i: (i, 0)), pl.BlockSpec((6, HEAD_DIM), lambda i: (0, 0))],
        out_shape=[jax.ShapeDtypeStruct((s, B_IN), CDT), jax.ShapeDtypeStruct((6, HEAD_DIM), F32)],
        compiler_params=_cparams("arbitrary"),
    )(proj, *dqkv, dz, gains, *tabs)


def _inv_count(t, window):
    return 1.0 / jnp.minimum(t + 1, window).astype(F32)


def _c_pool_fwd(proj, name, tq=512):
    s = proj.shape[0]
    per = tq // POOL_HALO

    def body(x_ref, halo_ref, o_ref, ext_ref):
        i = pl.program_id(0)
        xv = x_ref[...].astype(F32)
        ext_ref[POOL_HALO:, :] = xv
        ext_ref[:POOL_HALO, :] = jnp.where(i > 0, halo_ref[...].astype(F32), 0.0)
        t = i * tq + lax.broadcasted_iota(jnp.int32, (tq, 1), 0)
        for g, window in enumerate(POOL_SIZES):
            cols = slice(g * C_GROUP, (g + 1) * C_GROUP)
            acc = xv[:, cols]
            for back in range(1, window):
                acc = acc + ext_ref[POOL_HALO - back:POOL_HALO - back + tq, cols]
            o_ref[:, cols] = (acc * _inv_count(t, window) - xv[:, cols]).astype(o_ref.dtype)

    return pl.pallas_call(
        body, name=name, grid=(s // tq,),
        in_specs=[pl.BlockSpec((tq, C_WIDTH), lambda i: (i, 0)),
                  pl.BlockSpec((POOL_HALO, C_WIDTH), lambda i: (jnp.maximum(i * per - 1, 0), 0))],
        out_specs=pl.BlockSpec((tq, C_WIDTH), lambda i: (i, 0)),
        out_shape=jax.ShapeDtypeStruct((s, C_WIDTH), CDT),
        scratch_shapes=[pltpu.VMEM((tq + POOL_HALO, C_WIDTH), F32)],
        compiler_params=_cparams("parallel"),
    )(proj, proj)


GRP_SHARD = (N_CHIPS, 2, 256, C_GROUP)
GRP_ROWS = C_GROUP // N_CHIPS


def _grp_rows(g):
    return g // 2, slice((g % 2) * GRP_ROWS, (g % 2 + 1) * GRP_ROWS)


def _grp_weight(w_ref, g):
    half, rows = _grp_rows(g)
    return jnp.concatenate([w_ref[k, half, rows, :] for k in range(N_CHIPS)], axis=0)


def _c_mid_fwd(diff, proj, w_grp, scale, name, tq=512):
    s = diff.shape[0]

    def body(d_ref, z_ref, w_ref, sc_ref, y_ref):
        for g in range(len(POOL_SIZES)):
            cols = slice(g * C_GROUP, (g + 1) * C_GROUP)
            z = z_ref[:, cols].astype(F32)
            y_ref[:, cols] = (_dot(d_ref[:, cols], _grp_weight(w_ref, g)) * sc_ref[:, cols] * (z * _sigmoid(z))).astype(y_ref.dtype)

    return pl.pallas_call(
        body, name=name, grid=(s // tq,),
        in_specs=[pl.BlockSpec((tq, C_WIDTH), lambda i: (i, 0)), pl.BlockSpec((tq, C_WIDTH), lambda i: (i, 1)),
                  pl.BlockSpec(GRP_SHARD, lambda i: (0, 0, 0, 0)), pl.BlockSpec((1, C_WIDTH), lambda i: (0, 0))],
        out_specs=pl.BlockSpec((tq, C_WIDTH), lambda i: (i, 0)),
        out_shape=jax.ShapeDtypeStruct((s, C_WIDTH), CDT), compiler_params=_cparams("parallel"),
    )(diff, proj, w_grp, scale)


def _c_mid_bwd(diff, proj, w_grp, scale, dy, name, tq=512):
    s = diff.shape[0]

    def body(d_ref, z_ref, w_ref, sc_ref, dy_ref, dd_ref, dz_ref, dw_ref, dsc_ref):
        @pl.when(pl.program_id(0) == 0)
        def _():
            dw_ref[...] = jnp.zeros_like(dw_ref)
            dsc_ref[...] = jnp.zeros_like(dsc_ref)

        for g in range(len(POOL_SIZES)):
            cols = slice(g * C_GROUP, (g + 1) * C_GROUP)
            d = d_ref[:, cols]
            wg = _grp_weight(w_ref, g)
            half, rows = _grp_rows(g)
            m0 = _dot(d, wg)
            z = z_ref[:, cols].astype(F32)
            dyv = dy_ref[:, cols].astype(F32)
            sc = sc_ref[:, cols]
            sg = _sigmoid(z)
            dmixed = dyv * (z * sg)
            dz_ref[:, cols] = (dyv * m0 * sc * (sg * (1.0 + z * (1.0 - sg)))).astype(dz_ref.dtype)
            dsc_ref[:, cols] += jnp.sum(dmixed * m0, axis=0, keepdims=True)
            dm0 = (dmixed * sc).astype(CDT)
            dwg = _dot(d, dm0, TN)
            for k in range(N_CHIPS):
                dw_ref[k, half, rows, :] += dwg[k * GRP_ROWS:(k + 1) * GRP_ROWS]
            dd_ref[:, cols] = _dot(dm0, wg, NT)

    blk = pl.BlockSpec((tq, C_WIDTH), lambda i: (i, 0))
    wsp = pl.BlockSpec(GRP_SHARD, lambda i: (0, 0, 0, 0))
    vec = pl.BlockSpec((1, C_WIDTH), lambda i: (0, 0))
    return pl.pallas_call(
        body, name=name, grid=(s // tq,),
        in_specs=[blk, pl.BlockSpec((tq, C_WIDTH), lambda i: (i, 1)), wsp, vec, blk],
        out_specs=[blk, blk, wsp, vec],
        out_shape=[jax.ShapeDtypeStruct((s, C_WIDTH), F32), jax.ShapeDtypeStruct((s, C_WIDTH), CDT),
                   jax.ShapeDtypeStruct(GRP_SHARD, F32), jax.ShapeDtypeStruct((1, C_WIDTH), F32)],
        compiler_params=_cparams("arbitrary"),
    )(diff, proj, w_grp, scale, dy)


def _c_pool_bwd(ddiff, dz, name, tq=512):
    s = ddiff.shape[0]
    per = tq // POOL_HALO
    last = s // tq - 1

    def body(d_ref, halo_ref, dz_ref, o_ref, ext_ref):
        i = pl.program_id(0)
        dv = d_ref[...]
        t = i * tq + lax.broadcasted_iota(jnp.int32, (tq, 1), 0)
        for g, window in enumerate(POOL_SIZES):
            cols = slice(g * C_GROUP, (g + 1) * C_GROUP)
            ext_ref[:tq, cols] = dv[:, cols] * _inv_count(t, window)
            ext_ref[tq:, cols] = jnp.where(i < last, halo_ref[:, cols] * (1.0 / window), 0.0)
            acc = -dv[:, cols]
            for fwd in range(window):
                acc = acc + ext_ref[fwd:fwd + tq, cols]
            o_ref[:, cols] = acc.astype(o_ref.dtype)
        o_ref[:, C_WIDTH:] = dz_ref[...]

    return pl.pallas_call(
        body, name=name, grid=(s // tq,),
        in_specs=[pl.BlockSpec((tq, C_WIDTH), lambda i: (i, 0)),
                  pl.BlockSpec((POOL_HALO, C_WIDTH), lambda i: (jnp.minimum((i + 1) * per, s // POOL_HALO - 1), 0)),
                  pl.BlockSpec((tq, C_WIDTH), lambda i: (i, 0))],
        out_specs=pl.BlockSpec((tq, 2 * C_WIDTH), lambda i: (i, 0)),
        out_shape=jax.ShapeDtypeStruct((s, 2 * C_WIDTH), CDT),
        scratch_shapes=[pltpu.VMEM((tq + POOL_HALO, C_WIDTH), F32)],
        compiler_params=_cparams("parallel"),
    )(ddiff, ddiff, dz)


LAYERS = (("a", 0), ("b", 0), ("c", 0), ("a", 1))
SHARD_TILE = {1536: 768, 2560: 1280, 1024: 1024}


def _shard_spec(block, layer, where):
    if layer is None:
        return pl.BlockSpec((None,) + block, where)
    return pl.BlockSpec((None, None) + block, lambda i, j, q: (where(i, j, q)[0], layer) + where(i, j, q)[1:])


def _w_in_fwd(h, gw, layer, name):
    s, width = h.shape[0], gw.shape[-1]
    tn = SHARD_TILE[width]
    per = width // tn
    return _mm(h, gw, "nn", CDT, name, (s, N_CHIPS * width, D_MODEL), (1024, tn, D_MODEL),
               b_spec=_shard_spec((D_MODEL, tn), layer, lambda i, j, q: (j // per, 0, j % per)))


def _w_in_dh(dproj, gw, layer, name):
    s, width = dproj.shape[0], gw.shape[-1]
    tk = SHARD_TILE[width]
    per = width // tk
    return _mm(dproj, gw, "nt", F32, name, (s, D_MODEL, N_CHIPS * width), (1024, D_MODEL, tk),
               b_spec=_shard_spec((D_MODEL, tk), layer, lambda i, j, q: (q // per, 0, q % per)))


def _w_in_grad(h, dproj, width, name):
    s = h.shape[0]
    tn = SHARD_TILE[width]
    per = width // tn
    return _mm(h, dproj, "tn", CDT, name, (D_MODEL, N_CHIPS * width, s), (D_MODEL, tn, 1024),
               o_spec=pl.BlockSpec((None, D_MODEL, tn), lambda i, j, q: (j // per, 0, j % per)), o_shape=(N_CHIPS, D_MODEL, width))


def _w_out_fwd(y, gw, layer, x, name):
    s, rows = y.shape[0], gw.shape[-2]
    return _mm(y, gw, "nn", F32, name, (s, D_MODEL, N_CHIPS * rows), (512, D_MODEL, rows), add=x,
               b_spec=_shard_spec((rows, D_MODEL), layer, lambda i, j, q: (q, 0, 0)))


def _w_out_dy(dxc, gw, layer, name):
    s, rows = dxc.shape[0], gw.shape[-2]
    return _mm(dxc, gw, "nt", CDT, name, (s, N_CHIPS * rows, D_MODEL), (1024, rows, D_MODEL),
               b_spec=_shard_spec((rows, D_MODEL), layer, lambda i, j, q: (j, 0, 0)))


def _w_out_grad(y, dxc, rows, name):
    s = y.shape[0]
    return _mm(y, dxc, "tn", CDT, name, (N_CHIPS * rows, D_MODEL, s), (rows, D_MODEL, 1024),
               o_spec=pl.BlockSpec((None, rows, D_MODEL), lambda i, j, q: (i, 0, 0)), o_shape=(N_CHIPS, rows, D_MODEL))


def _local_step(x, target, w):
    s = x.shape[0]
    tabs = _rope_tables(s)
    qk_gains = jnp.concatenate([w["b_q_gain"][0], w["b_k_gain"][0]], axis=0)
    saved = []
    for li, (kind, j) in enumerate(LAYERS):
        h = _rms_fwd(x, w["norm_gain"][li:li + 1], f"rms_fwd{li}")
        if kind == "a":
            proj = _w_in_fwd(h, w["a_w_in"], j, f"a_in{li}")
            bs_t = w["a_b_s"][j].T
            y = _a_mid_fwd(proj, w["a_v_gain"][j:j + 1], w["a_w_s"][j], bs_t, f"a_mid_fwd{li}")
            x_next = _w_out_fwd(y, w["a_w_out"], j, x, f"a_out{li}")
            saved.append((x, h, proj, y))
        elif kind == "b":
            proj = _w_in_fwd(h, w["b_w_in"], None, f"b_in{li}")
            qk = _b_qk_fwd(proj, qk_gains, tabs, f"b_qk_fwd{li}")
            qkv, os_, lses = [], [], []
            for g, dil in enumerate(B_DILATIONS):
                if dil == 1:
                    ops = (qk, qk, proj, (g, 3 + g, 6 + g))
                else:
                    ops = (_to_view(qk, g, 1024, dil, f"b_q_view{li}_{g}"), _to_view(qk, 3 + g, 1024, dil, f"b_k_view{li}_{g}"),
                           _to_view(proj, 6 + g, 1024, dil, f"b_v_view{li}_{g}"), (0, 0, 0))
                o, lse = _b_attn_fwd(*ops, dil, f"b_attn_fwd{li}_{g}")
                if dil > 1:
                    o, lse = _from_view(o, dil, f"b_o_nat{li}_{g}"), _from_view(lse, dil, f"b_lse_nat{li}_{g}")
                qkv.append(ops)
                os_.append(o)
                lses.append(lse)
            y, oj, lj = _b_combine(os_, lses, proj, f"b_combine{li}")
            x_next = _w_out_fwd(y, w["b_w_out"], None, x, f"b_out{li}")
            saved.append((x, h, proj, y, qkv, oj, lj))
        else:
            proj = _w_in_fwd(h, w["c_w_in"], None, f"c_in{li}")
            diff = _c_pool_fwd(proj, f"c_pool_fwd{li}")
            y = _c_mid_fwd(diff, proj, w["c_w_grp"], w["c_scale"][j:j + 1], f"c_mid_fwd{li}")
            x_next = _w_out_fwd(y, w["c_w_out"], None, x, f"c_out{li}")
            saved.append((x, h, proj, y, diff))
        x = x_next

    dx, dxc, sq = _loss_bwd(x, target, "loss_bwd")
    grads = {"norm_gain": [None] * len(LAYERS), "a_w_in": [None, None], "a_v_gain": [None, None], "a_w_s": [None, None],
             "a_b_s": [None, None], "a_w_out": [None, None]}
    for li in reversed(range(len(LAYERS))):
        kind, j = LAYERS[li]
        sv = saved[li]
        xin, h, proj, y = sv[:4]
        if kind == "a":
            grads["a_w_out"][j] = _w_out_grad(y, dxc, 512, f"a_dwout{li}")
            dy = _w_out_dy(dxc, w["a_w_out"], j, f"a_dy{li}")
            dproj, dws, dbs_t, dvg = _a_mid_bwd(proj, dy, w["a_v_gain"][j:j + 1], w["a_w_s"][j], w["a_b_s"][j].T, f"a_mid_bwd{li}")
            grads["a_w_s"][j], grads["a_b_s"][j], grads["a_v_gain"][j] = dws, dbs_t.T, dvg[0]
            grads["a_w_in"][j] = _w_in_grad(h, dproj, 1536, f"a_dwin{li}")
            dh = _w_in_dh(dproj, w["a_w_in"], j, f"a_dh{li}")
        elif kind == "b":
            qkv, oj, lj = sv[4:]
            grads["b_w_out"] = _w_out_grad(y, dxc, 256, f"b_dwout{li}")
            dy = _w_out_dy(dxc, w["b_w_out"], None, f"b_dy{li}")
            do, dz, dd = _b_bwd_prep(dy, oj, proj, f"b_bwd_prep{li}")
            dqkv = []
            for g, dil in enumerate(B_DILATIONS):
                stats = (do, lj, dd)
                if dil > 1:
                    stats = (_to_view(do, 0, 1024, dil, f"b_do_view{li}_{g}"), _to_view(lj, 0, HEAD_DIM, dil, f"b_lj_view{li}_{g}"),
                             _to_view(dd, 0, HEAD_DIM, dil, f"b_dd_view{li}_{g}"))
                d = _b_attn_bwd(*qkv[g], *stats, dil, f"b_attn_bwd{li}_{g}")
                dqkv.append(_from_view(d, dil, f"b_dqkv_nat{li}_{g}") if dil > 1 else d)
            dproj, dgains = _b_qk_bwd(proj, dqkv, dz, qk_gains, tabs, f"b_qk_bwd{li}")
            grads["b_q_gain"], grads["b_k_gain"] = dgains[None, :3], dgains[None, 3:]
            grads["b_w_in"] = _w_in_grad(h, dproj, 2560, f"b_dwin{li}")
            dh = _w_in_dh(dproj, w["b_w_in"], None, f"b_dh{li}")
        else:
            diff = sv[4]
            grads["c_w_out"] = _w_out_grad(y, dxc, 512, f"c_dwout{li}")
            dy = _w_out_dy(dxc, w["c_w_out"], None, f"c_dy{li}")
            ddiff, dz, dwg, dsc = _c_mid_bwd(diff, proj, w["c_w_grp"], w["c_scale"][j:j + 1], dy, f"c_mid_bwd{li}")
            grads["c_w_grp"], grads["c_scale"] = dwg, dsc
            dproj = _c_pool_bwd(ddiff, dz, f"c_pool_bwd{li}")
            grads["c_w_in"] = _w_in_grad(h, dproj, 1024, f"c_dwin{li}")
            dh = _w_in_dh(dproj, w["c_w_in"], None, f"c_dh{li}")
        dx, dxc, dng = _rms_bwd(xin, dh, dx, w["norm_gain"][li:li + 1], f"rms_bwd{li}")
        grads["norm_gain"][li] = dng[0]
    for name in ("norm_gain", "a_v_gain", "a_w_s", "a_b_s"):
        grads[name] = jnp.stack(grads[name])
    return sq, dx, grads


ROW = 1024
GATHER_UNITS = (("a_w_in", 1024, 1536), ("a_w_out", 512, 1024), ("b_w_in", 512, 2560), ("b_w_out", 128, 1024),
                ("c_w_in", 512, 1024), ("c_w_grp", 256, 512), ("c_w_out", 256, 1024))
GAIN_ROWS = 16
REDUCE_UNITS = (("a_w_in", 0, 512, 1536), ("a_w_in", 1, 512, 1536), ("a_w_out", 0, 256, 1024), ("a_w_out", 1, 256, 1024),
                ("b_w_in", 0, 512, 2560), ("b_w_out", 0, 128, 1024), ("c_w_in", 0, 512, 1024), ("c_w_grp", 0, 256, 512),
                ("c_w_out", 0, 256, 1024))
SMALL_OF_HALF = ("a_v_gain", "c_scale")
SMALL_PAD = 16
REP = (("norm_gain", (4, 1024)), ("a_w_s", (2, 8, 128, 128)), ("a_b_s", (2, 8, 128)), ("b_q_gain", (1, 3, 128)), ("b_k_gain", (1, 3, 128)))
REP_CORE = 48
REP_PART = 2 * REP_CORE
SMALL_ROWS = SMALL_PAD + REP_CORE
BLOCK_SHAPES = {"a_w_in": (2, 1024, 1536), "a_v_gain": (2, 512), "a_w_out": (2, 512, 1024), "b_w_in": (1, 1024, 2560),
                "b_w_out": (1, 256, 1024), "c_w_in": (1, 1024, 1024), "c_w_grp": (1, 4, 128, 512), "c_scale": (1, 512),
                "c_w_out": (1, 512, 1024)}
WEIGHTS = ("norm_gain", "a_w_in", "a_v_gain", "a_w_s", "a_b_s", "a_w_out", "b_w_in", "b_q_gain", "b_k_gain", "b_w_out",
           "c_w_in", "c_w_grp", "c_scale", "c_w_out")


def _rows(a, rows):
    a = a.reshape(-1)
    return jnp.pad(a, (0, rows * ROW - a.shape[0])).reshape(rows, ROW)


def _small_unit(vecs, rep_part, dtype):
    halves = [jnp.concatenate([_rows(vecs[h].astype(dtype), SMALL_PAD), rep_part[h * REP_CORE:(h + 1) * REP_CORE].astype(dtype)])
              for h in range(2)]
    return jnp.stack(halves)


def _pack_rep(tree):
    return _rows(jnp.concatenate([tree[n].astype(F32).reshape(-1) for n, _ in REP]), N_CHIPS * REP_PART)


def _unpack_rep(slab):
    flat, out, off = slab.reshape(-1), {}, 0
    for n, shape in REP:
        size = math.prod(shape)
        out[n] = flat[off:off + size].reshape(shape)
        off += size
    return out


def _into_slot(x, me, name):
    rows, width = x.shape
    tr = min(rows, 256)

    def body(me_ref, x_ref, o_ref):
        o_ref[...] = x_ref[...].astype(o_ref.dtype)

    return pl.pallas_call(
        body, name=name,
        grid_spec=pltpu.PrefetchScalarGridSpec(
            num_scalar_prefetch=1, grid=(rows // tr,), in_specs=[pl.BlockSpec((tr, width), lambda i, me_ref: (i, 0))],
            out_specs=pl.BlockSpec((None, tr, width), lambda i, me_ref: (me_ref[0], i, 0))),
        out_shape=jax.ShapeDtypeStruct((N_CHIPS, rows, width), CDT), compiler_params=_cparams("parallel"),
    )(me.reshape(1), x)


def _gather_inputs(wts, me):
    units = [_into_slot(wts[n].reshape(2 * r, w), me, f"slot_{n}").reshape(N_CHIPS, 2, r, w) for n, r, w in GATHER_UNITS]
    gains = jnp.concatenate([wts["a_v_gain"].reshape(-1), wts["c_scale"].reshape(-1)])
    gains = _rows(lax.bitcast_convert_type(gains, CDT), 2 * GAIN_ROWS)
    return units + [_into_slot(gains, me, "slot_gains").reshape(N_CHIPS, 2, GAIN_ROWS, ROW)]


def _gathered_weights(units):
    out = {n: u for (n, _, _), u in zip(GATHER_UNITS, units)}
    for n in ("b_w_in", "b_w_out", "c_w_in", "c_w_out"):
        u = out[n]
        out[n] = u.reshape(N_CHIPS, 2 * u.shape[2], u.shape[3])
    gains = lax.bitcast_convert_type(units[-1][:, 0, :3].reshape(N_CHIPS, 1536, 2), F32)
    out["a_v_gain"] = jnp.concatenate([gains[k, :1024].reshape(2, 512) for k in range(N_CHIPS)], axis=1)
    out["c_scale"] = jnp.concatenate([gains[k, 1024:].reshape(1, 512) for k in range(N_CHIPS)], axis=1)
    return out


def _place():
    x, y, c = lax.axis_index("x"), lax.axis_index("y"), lax.axis_index("c")
    chips = [(1 - x, y), (x, 1 - y), (1 - x, 1 - y)]
    return x, y, c, 2 * x + y, (x, y, 1 - c), chips


def _remote(src, dst, sems, j, to):
    send_sems, recv_sems = sems
    return pltpu.make_async_remote_copy(src_ref=src, dst_ref=dst, send_sem=send_sems.at[j], recv_sem=recv_sems.at[j],
                                        device_id=to, device_id_type=MESH)


def _comm_call(body, name, out_shape, n_sems, *args):
    return pl.pallas_call(
        body, name=name, in_specs=[ANY] * len(args), out_specs=ANY, out_shape=out_shape,
        scratch_shapes=[pltpu.SemaphoreType.DMA((n_sems,)), pltpu.SemaphoreType.DMA((n_sems,))],
    )(*args)


def _comm_call_multi(body, name, out_shapes, n_sems, args, aliases=None):
    return pl.pallas_call(
        body, name=name, in_specs=[ANY] * len(args), out_specs=[ANY] * len(out_shapes), out_shape=out_shapes,
        scratch_shapes=[pltpu.SemaphoreType.DMA((n_sems,)), pltpu.SemaphoreType.DMA((n_sems,))],
        input_output_aliases=aliases or {},
    )(*args)


def _allgather_units(units):
    n = len(units)

    def body(*refs):
        outs, sems = refs[n:2 * n], (refs[2 * n], refs[2 * n + 1])
        x, y, c, me, sibling, chips = _place()
        first, passed = [], []
        for u, o_ref in enumerate(outs):
            for j, chip in enumerate(chips):
                first.append(_remote(o_ref.at[me, c], o_ref.at[me, c], sems, 6 * u + j, (*chip, c)))
                first[-1].start()
        for u, o_ref in enumerate(outs):
            for j, (cx, cy) in enumerate(chips):
                landed = o_ref.at[2 * cx + cy, c]
                _remote(landed, landed, sems, 6 * u + j, sibling).wait_recv()
                passed.append(_remote(landed, landed, sems, 6 * u + 3 + j, sibling))
                passed[-1].start()
        for u, o_ref in enumerate(outs):
            for j, (cx, cy) in enumerate(chips):
                landed = o_ref.at[2 * cx + cy, 1 - c]
                _remote(landed, landed, sems, 6 * u + 3 + j, sibling).wait_recv()
        for cp in first + passed:
            cp.wait_send()

    return _comm_call_multi(body, "allgather_weights", [jax.ShapeDtypeStruct(u.shape, u.dtype) for u in units], 6 * n, units,
                            aliases={u: u for u in range(n)})


def _swap_halves(units):
    n = len(units)

    def body(*refs):
        ins, outs, sems = refs[:n], refs[n:2 * n], (refs[2 * n], refs[2 * n + 1])
        x, y, c, me, sibling, chips = _place()
        sent = [_remote(g_ref.at[:, 1 - c], o_ref, sems, u, sibling) for u, (g_ref, o_ref) in enumerate(zip(ins, outs))]
        for cp in sent:
            cp.start()
        for cp in sent:
            cp.wait()

    shapes = [jax.ShapeDtypeStruct((N_CHIPS,) + u.shape[2:], u.dtype) for u in units]
    return _comm_call_multi(body, "grad_swap_halves", shapes, n, units)


def _scatter_shards(units):
    n = len(units)

    def body(*refs):
        ins, outs, sems = refs[:n], refs[n:2 * n], (refs[2 * n], refs[2 * n + 1])
        x, y, c, me, sibling, chips = _place()
        sent = []
        for u, (s_ref, o_ref) in enumerate(zip(ins, outs)):
            for j, (cx, cy) in enumerate(chips):
                sent.append(_remote(s_ref.at[2 * cx + cy], o_ref.at[me], sems, 3 * u + j, (cx, cy, c)))
                sent[-1].start()
        for u, o_ref in enumerate(outs):
            for j, (cx, cy) in enumerate(chips):
                slot = o_ref.at[2 * cx + cy]
                _remote(slot, slot, sems, 3 * u + j, sibling).wait_recv()
        for cp in sent:
            cp.wait_send()

    return _comm_call_multi(body, "grad_scatter_shards", [jax.ShapeDtypeStruct(u.shape, u.dtype) for u in units], 3 * n, units)


def _share_halves(units):
    n = len(units)

    def body(*refs):
        outs, sems = refs[n:2 * n], (refs[2 * n], refs[2 * n + 1])
        x, y, c, me, sibling, chips = _place()
        sent = [_remote(o_ref.at[:, c], o_ref.at[:, c], sems, u, sibling) for u, o_ref in enumerate(outs)]
        for cp in sent:
            cp.start()
        for u, o_ref in enumerate(outs):
            theirs = o_ref.at[:, 1 - c]
            _remote(theirs, theirs, sems, u, sibling).wait_recv()
        for cp in sent:
            cp.wait_send()

    return _comm_call_multi(body, "share_halves", [jax.ShapeDtypeStruct(u.shape, u.dtype) for u in units], n, units,
                            aliases={u: u for u in range(n)})


def _allgather_rep(rep4):
    def body(r_ref, o_ref, send_sems, recv_sems):
        x, y, c, me, sibling, chips = _place()
        sems = (send_sems, recv_sems)
        sent = [_remote(r_ref, o_ref.at[me], sems, j, (*chip, c)) for j, chip in enumerate(chips)]
        for cp in sent:
            cp.start()
        for j, (cx, cy) in enumerate(chips):
            slot = o_ref.at[2 * cx + cy]
            _remote(slot, slot, sems, j, sibling).wait_recv()
        for cp in sent:
            cp.wait_send()

    return _comm_call(body, "allgather_rep", jax.ShapeDtypeStruct((N_CHIPS,) + rep4.shape, rep4.dtype), 3, rep4)


ADAM_TILE = 64


def _add_halves(mine, theirs, place, name):
    rows, width = theirs.shape[1:]
    tr = min(rows, 256)

    def body(place_ref, a_ref, b_ref, o_ref):
        o_ref[...] = (a_ref[...].astype(F32) + b_ref[...].astype(F32)).astype(o_ref.dtype)

    blk = pl.BlockSpec((None, tr, width), lambda k, i, place_ref: (k, i, 0))
    return pl.pallas_call(
        body, name=name,
        grid_spec=pltpu.PrefetchScalarGridSpec(
            num_scalar_prefetch=1, grid=(N_CHIPS, rows // tr),
            in_specs=[pl.BlockSpec((None, None, tr, width), lambda k, i, place_ref: (k, place_ref[1], i, 0)), blk], out_specs=blk),
        out_shape=jax.ShapeDtypeStruct(theirs.shape, theirs.dtype), compiler_params=_cparams("parallel", "parallel"),
    )(place, mine, theirs)


def _adamw(place, parts, own, w, m, v, layer, name):
    rows, width = parts.shape[1:]

    def body(place_ref, p_ref, own_ref, w_ref, m_ref, v_ref, o_ref):
        g = jnp.zeros((ADAM_TILE, width), F32)
        for k in range(N_CHIPS):
            g = g + jnp.where(place_ref[0] == k, own_ref[...], p_ref[k]).astype(F32)
        m2 = ADAM_B1 * m_ref[...] + (1.0 - ADAM_B1) * g
        v2 = ADAM_B2 * v_ref[...] + (1.0 - ADAM_B2) * jnp.square(g)
        m_hat = m2 / (1.0 - ADAM_B1 ** ADAM_STEP)
        v_hat = v2 / (1.0 - ADAM_B2 ** ADAM_STEP)
        o_ref[0] = g
        o_ref[1] = -ADAM_LR * (m_hat / (jnp.sqrt(v_hat) + ADAM_EPS) + ADAM_WD * w_ref[...])
        o_ref[2] = m2
        o_ref[3] = v2

    blk = pl.BlockSpec((None, None, ADAM_TILE, width), lambda i, place_ref: (layer, place_ref[1], i, 0))
    blk4 = pl.BlockSpec((4, ADAM_TILE, width), lambda i, place_ref: (0, i, 0))
    mine = pl.BlockSpec((None, ADAM_TILE, width), lambda i, place_ref: (place_ref[0], i, 0))
    out = pl.BlockSpec((4, None, ADAM_TILE, width), lambda i, place_ref: (0, place_ref[1], i, 0))
    return pl.pallas_call(
        body, name=name,
        grid_spec=pltpu.PrefetchScalarGridSpec(num_scalar_prefetch=1, grid=(rows // ADAM_TILE,),
                                               in_specs=[blk4, mine, blk, blk, blk], out_specs=out),
        out_shape=jax.ShapeDtypeStruct((4, 2, rows, width), F32), compiler_params=_cparams("parallel"),
    )(place, parts, own, w, m, v)


def kernel(x, norm_gain, a_w_in, a_v_gain, a_w_s, a_b_s, a_w_out, b_w_in, b_q_gain, b_k_gain, b_w_out, c_w_in, c_w_grp, c_scale, c_w_out, loss_target, m_norm_gain, m_a_w_in, m_a_v_gain, m_a_w_s, m_a_b_s, m_a_w_out, m_b_w_in, m_b_q_gain, m_b_k_gain, m_b_w_out, m_c_w_in, m_c_w_grp, m_c_scale, m_c_w_out, v_norm_gain, v_a_w_in, v_a_v_gain, v_a_w_s, v_a_b_s, v_a_w_out, v_b_w_in, v_b_q_gain, v_b_k_gain, v_b_w_out, v_c_w_in, v_c_w_grp, v_c_scale, v_c_w_out):
    wts = dict(norm_gain=norm_gain, a_w_in=a_w_in, a_v_gain=a_v_gain, a_w_s=a_w_s, a_b_s=a_b_s, a_w_out=a_w_out, b_w_in=b_w_in,
               b_q_gain=b_q_gain, b_k_gain=b_k_gain, b_w_out=b_w_out, c_w_in=c_w_in, c_w_grp=c_w_grp, c_scale=c_scale, c_w_out=c_w_out)
    mom1 = dict(norm_gain=m_norm_gain, a_w_in=m_a_w_in, a_v_gain=m_a_v_gain, a_w_s=m_a_w_s, a_b_s=m_a_b_s, a_w_out=m_a_w_out,
                b_w_in=m_b_w_in, b_q_gain=m_b_q_gain, b_k_gain=m_b_k_gain, b_w_out=m_b_w_out, c_w_in=m_c_w_in, c_w_grp=m_c_w_grp,
                c_scale=m_c_scale, c_w_out=m_c_w_out)
    mom2 = dict(norm_gain=v_norm_gain, a_w_in=v_a_w_in, a_v_gain=v_a_v_gain, a_w_s=v_a_w_s, a_b_s=v_a_b_s, a_w_out=v_a_w_out,
                b_w_in=v_b_w_in, b_q_gain=v_b_q_gain, b_k_gain=v_b_k_gain, b_w_out=v_b_w_out, c_w_in=v_c_w_in, c_w_grp=v_c_w_grp,
                c_scale=v_c_scale, c_w_out=v_c_w_out)
    axes = ("x", "y", "c")
    me = 2 * lax.axis_index("x") + lax.axis_index("y")
    core = lax.axis_index("c")

    place = jnp.stack([me, core]).astype(jnp.int32)

    full = _gathered_weights(_allgather_units(_gather_inputs(wts, me)))
    for n, _ in REP:
        full[n] = wts[n]
    sq, grad_x, grads = _local_step(x[0], loss_target[0], full)
    loss = lax.psum(0.5 * jnp.sum(sq) / D_MODEL, axes)

    rep_g = _pack_rep(grads)
    g_units = []
    for n, layer, r, w in REDUCE_UNITS:
        g = grads[n][layer] if n in ("a_w_in", "a_w_out") else grads[n]
        g_units.append(g.astype(CDT).reshape(N_CHIPS, 2, r, w))
    g_units.append(jnp.stack([
        _small_unit([lax.slice_in_dim(grads[v], 512 * k, 512 * (k + 1), axis=1) for v in SMALL_OF_HALF],
                    rep_g[k * REP_PART:(k + 1) * REP_PART], CDT) for k in range(N_CHIPS)]))
    from_sibling = _swap_halves(g_units)
    chip_sums = [_add_halves(g, t, place, f"grad_add_halves{u}") for u, (g, t) in enumerate(zip(g_units, from_sibling))]
    parts = _scatter_shards(chip_sums)

    def state_units(tree):
        units = [tree[n].reshape(BLOCK_SHAPES[n][0], 2, r, w) for n, _, r, w in REDUCE_UNITS]
        rep_part = lax.dynamic_slice_in_dim(_pack_rep(tree), me * REP_PART, REP_PART, axis=0)
        return units + [_small_unit([tree[v] for v in SMALL_OF_HALF], rep_part, F32)[None]]

    layers = [layer for _, layer, _, _ in REDUCE_UNITS] + [0]
    res = _share_halves([_adamw(place, p, s, w_, m_, v_, layer, f"adamw{u}") for u, (p, s, w_, m_, v_, layer) in enumerate(
        zip(parts, chip_sums, state_units(wts), state_units(mom1), state_units(mom2), layers))])
    small = res[-1]
    rep_mine = jnp.concatenate([small[:, 0, SMALL_PAD:], small[:, 1, SMALL_PAD:]], axis=1)
    rep_all = lax.dynamic_update_slice_in_dim(_allgather_rep(rep_mine), rep_mine[None], me, axis=0)

    outs = []
    for q in range(4):
        tree = {}
        for (n, layer, r, w), u in zip(REDUCE_UNITS, res):
            tree.setdefault(n, []).append(u[q].reshape(BLOCK_SHAPES[n][1:]))
        tree = {n: jnp.stack(v) for n, v in tree.items()}
        tree["a_v_gain"] = small[q, 0, 0].reshape(2, 512)
        tree["c_scale"] = small[q, 1, 0, :512].reshape(1, 512)
        tree.update(_unpack_rep(rep_all[:, q].reshape(N_CHIPS * REP_PART, ROW)))
        outs.append(tree)
    return (loss, grad_x[None], *[t[n] for t in outs for n in WEIGHTS])
```

```python
import functools
import math

import jax
import jax.numpy as jnp
from jax import lax
from jax.experimental import pallas as pl
from jax.experimental.pallas import tpu as pltpu

F32 = jnp.float32
CDT = jnp.bfloat16

D_MODEL = 1024
EPS = 1e-6
CHUNK = 128
A_WIDTH = 2048
A_GROUPS = 8
A_GROUP_DIM = 256
HEAD_DIM = 128
B_HEADS = 8
B_DILATIONS = (1, 4, 16)
B_QK = 6144
B_IN = 10240
ROPE_HALF = 16
ROPE_THETA = 500000.0
POOL_SIZES = (2, 4, 8, 16)
POOL_HALO = 16
C_WIDTH = 2048
C_GROUP = 512
N_CHIPS = 4

ADAM_LR = 0.001
ADAM_B1 = 0.9
ADAM_B2 = 0.999
ADAM_EPS = 1e-08
ADAM_WD = 0.01
ADAM_STEP = 10

VMEM_LIMIT = 48 * 1024 * 1024
ANY = pl.BlockSpec(memory_space=pl.ANY)
MESH = pl.DeviceIdType.MESH

NN = (((1,), (0,)), ((), ()))
NT = (((1,), (1,)), ((), ()))
TN = (((0,), (0,)), ((), ()))


def _cparams(*sem):
    return pltpu.CompilerParams(dimension_semantics=sem, vmem_limit_bytes=VMEM_LIMIT)


def _dot(a, b, dims=NN):
    return lax.dot_general(a, b, dims, preferred_element_type=F32)


def _sigmoid(z):
    return 1.0 / (1.0 + jnp.exp(-z))


def _lane_sums(v):
    ones = jnp.ones((HEAD_DIM, HEAD_DIM), jnp.bfloat16)
    hi = v.astype(jnp.bfloat16)
    lo = (v - hi.astype(F32)).astype(jnp.bfloat16)
    return _dot(hi, ones) + _dot(lo, ones)


def _mm(a, b, mode, out_dtype, name, mnk, tiles, b_spec=None, o_spec=None, o_shape=None, add=None):
    m, n, k = mnk
    tm, tn, tk = min(tiles[0], m), min(tiles[1], n), min(tiles[2], k)
    nk = k // tk
    a_spec = {"nn": pl.BlockSpec((tm, tk), lambda i, j, q: (i, q)),
              "nt": pl.BlockSpec((tm, tk), lambda i, j, q: (i, q)),
              "tn": pl.BlockSpec((tk, tm), lambda i, j, q: (q, i))}[mode]
    if b_spec is None:
        b_spec = {"nn": pl.BlockSpec((tk, tn), lambda i, j, q: (q, j)),
                  "nt": pl.BlockSpec((tn, tk), lambda i, j, q: (j, q)),
                  "tn": pl.BlockSpec((tk, tn), lambda i, j, q: (q, j))}[mode]
    if o_spec is None:
        o_spec, o_shape = pl.BlockSpec((tm, tn), lambda i, j, q: (i, j)), (m, n)
    dims = {"nn": NN, "nt": NT, "tn": TN}[mode]
    has_add = add is not None

    def body(*refs):
        a_ref, b_ref = refs[0], refs[1]
        o_ref = refs[3] if has_add else refs[2]
        p = _dot(a_ref[...], b_ref[...], dims)

        def finish(v):
            if has_add:
                v = v + refs[2][...]
            o_ref[...] = v.astype(o_ref.dtype)

        if nk == 1:
            finish(p)
        else:
            acc_ref = refs[-1]
            q = pl.program_id(2)

            @pl.when(q == 0)
            def _():
                acc_ref[...] = p

            @pl.when(q > 0)
            def _():
                acc_ref[...] += p

            @pl.when(q == nk - 1)
            def _():
                finish(acc_ref[...])

    in_specs = [a_spec, b_spec]
    args = [a, b]
    if has_add:
        in_specs.append(pl.BlockSpec((tm, tn), lambda i, j, q: (i, j)))
        args.append(add)
    return pl.pallas_call(
        body, name=name, grid=(m // tm, n // tn, nk), in_specs=in_specs, out_specs=o_spec,
        out_shape=jax.ShapeDtypeStruct(o_shape, out_dtype),
        scratch_shapes=[pltpu.VMEM((tm, tn), F32)] if nk > 1 else [],
        compiler_params=_cparams("parallel", "parallel", "arbitrary"),
    )(*args)


def _rms_fwd(x, g, name, tq=512):
    s, d = x.shape

    def body(x_ref, g_ref, h_ref):
        xv = x_ref[...]
        r = lax.rsqrt(jnp.mean(xv * xv, axis=-1, keepdims=True) + EPS)
        h_ref[...] = (xv * r * g_ref[...]).astype(h_ref.dtype)

    return pl.pallas_call(
        body, name=name, grid=(s // tq,),
        in_specs=[pl.BlockSpec((tq, d), lambda i: (i, 0)), pl.BlockSpec((1, d), lambda i: (0, 0))],
        out_specs=pl.BlockSpec((tq, d), lambda i: (i, 0)),
        out_shape=jax.ShapeDtypeStruct((s, d), CDT), compiler_params=_cparams("parallel"),
    )(x, g)


def _rms_bwd(x, dh, dxo, g, name, tq=512):
    s, d = x.shape

    def body(x_ref, dh_ref, dxo_ref, g_ref, dx_ref, dxc_ref, dg_ref):
        xv = x_ref[...]
        dhv = dh_ref[...].astype(F32)
        r = lax.rsqrt(jnp.mean(xv * xv, axis=-1, keepdims=True) + EPS)
        gd = dhv * g_ref[...]
        dx = dxo_ref[...] + r * gd - xv * (r * r * r) * jnp.mean(gd * xv, axis=-1, keepdims=True)
        dx_ref[...] = dx
        dxc_ref[...] = dx.astype(dxc_ref.dtype)

        @pl.when(pl.program_id(0) == 0)
        def _():
            dg_ref[...] = jnp.zeros_like(dg_ref)

        dg_ref[...] += jnp.sum(dhv * xv * r, axis=0, keepdims=True)

    blk = pl.BlockSpec((tq, d), lambda i: (i, 0))
    vec = pl.BlockSpec((1, d), lambda i: (0, 0))
    return pl.pallas_call(
        body, name=name, grid=(s // tq,), in_specs=[blk, blk, blk, vec], out_specs=[blk, blk, vec],
        out_shape=[jax.ShapeDtypeStruct((s, d), F32), jax.ShapeDtypeStruct((s, d), CDT), jax.ShapeDtypeStruct((1, d), F32)],
        compiler_params=_cparams("arbitrary"),
    )(x, dh, dxo, g)


def _loss_bwd(y, target, name, tq=512):
    s, d = y.shape

    def body(y_ref, t_ref, dx_ref, dxc_ref, sq_ref):
        err = y_ref[...] - t_ref[...]
        dx = err * (1.0 / d)
        dx_ref[...] = dx
        dxc_ref[...] = dx.astype(dxc_ref.dtype)

        @pl.when(pl.program_id(0) == 0)
        def _():
            sq_ref[...] = jnp.zeros_like(sq_ref)

        sq_ref[...] += jnp.sum(err * err, axis=0, keepdims=True)

    blk = pl.BlockSpec((tq, d), lambda i: (i, 0))
    vec = pl.BlockSpec((1, d), lambda i: (0, 0))
    return pl.pallas_call(
        body, name=name, grid=(s // tq,), in_specs=[blk, blk], out_specs=[blk, blk, vec],
        out_shape=[jax.ShapeDtypeStruct((s, d), F32), jax.ShapeDtypeStruct((s, d), CDT), jax.ShapeDtypeStruct((1, d), F32)],
        compiler_params=_cparams("arbitrary"),
    )(y, target)


def _tril_mask():
    row = lax.broadcasted_iota(jnp.int32, (CHUNK, CHUNK), 0)
    col = lax.broadcasted_iota(jnp.int32, (CHUNK, CHUNK), 1)
    return row >= col


def _a_mid_fwd(proj, v_gain, w_s, b_s_t, name, tq=256):
    s = proj.shape[0]

    def body(p_ref, vg_ref, ws_ref, bs_ref, y_ref):
        vraw = p_ref[:, A_WIDTH:2 * A_WIDTH].astype(F32)
        r = lax.rsqrt(jnp.mean(vraw * vraw, axis=-1, keepdims=True) + EPS)
        vn = (vraw * r * vg_ref[...]).astype(CDT)
        tri = _tril_mask()
        for g in range(A_GROUPS):
            w = jnp.where(tri, ws_ref[g], 0.0).astype(CDT)
            bias = bs_ref[:, g:g + 1]
            cols = slice(g * A_GROUP_DIM, (g + 1) * A_GROUP_DIM)
            zcols = slice(2 * A_WIDTH + g * A_GROUP_DIM, 2 * A_WIDTH + (g + 1) * A_GROUP_DIM)
            for c in range(tq // CHUNK):
                rows = slice(c * CHUNK, (c + 1) * CHUNK)
                mixed = _dot(w, vn[rows, cols]) + bias
                u = p_ref[rows, cols].astype(F32)
                z = p_ref[rows, zcols].astype(F32)
                y_ref[rows, cols] = (u * mixed * (z * _sigmoid(z))).astype(y_ref.dtype)

    return pl.pallas_call(
        body, name=name, grid=(s // tq,),
        in_specs=[pl.BlockSpec((tq, 3 * A_WIDTH), lambda i: (i, 0)), pl.BlockSpec((1, A_WIDTH), lambda i: (0, 0)),
                  pl.BlockSpec((A_GROUPS, CHUNK, CHUNK), lambda i: (0, 0, 0)), pl.BlockSpec((CHUNK, A_GROUPS), lambda i: (0, 0))],
        out_specs=pl.BlockSpec((tq, A_WIDTH), lambda i: (i, 0)),
        out_shape=jax.ShapeDtypeStruct((s, A_WIDTH), CDT), compiler_params=_cparams("parallel"),
    )(proj, v_gain, w_s, b_s_t)


def _a_mid_bwd(proj, dy, v_gain, w_s, b_s_t, name, tq=256):
    s = proj.shape[0]

    def body(p_ref, dy_ref, vg_ref, ws_ref, bs_ref, dp_ref, dws_ref, dbs_ref, dvg_ref, dvn_ref):
        @pl.when(pl.program_id(0) == 0)
        def _():
            dws_ref[...] = jnp.zeros_like(dws_ref)
            dbs_ref[...] = jnp.zeros_like(dbs_ref)
            dvg_ref[...] = jnp.zeros_like(dvg_ref)

        vraw = p_ref[:, A_WIDTH:2 * A_WIDTH].astype(F32)
        r = lax.rsqrt(jnp.mean(vraw * vraw, axis=-1, keepdims=True) + EPS)
        vhat = vraw * r
        vg = vg_ref[...]
        vn = (vhat * vg).astype(CDT)
        tri = _tril_mask()
        lane = lax.broadcasted_iota(jnp.int32, (CHUNK, A_GROUPS), 1)
        dbs = jnp.zeros((CHUNK, A_GROUPS), F32)
        for g in range(A_GROUPS):
            w = jnp.where(tri, ws_ref[g], 0.0).astype(CDT)
            bias = bs_ref[:, g:g + 1]
            cols = slice(g * A_GROUP_DIM, (g + 1) * A_GROUP_DIM)
            zcols = slice(2 * A_WIDTH + g * A_GROUP_DIM, 2 * A_WIDTH + (g + 1) * A_GROUP_DIM)
            dws = jnp.zeros((CHUNK, CHUNK), F32)
            for c in range(tq // CHUNK):
                rows = slice(c * CHUNK, (c + 1) * CHUNK)
                vn_g = vn[rows, cols]
                mixed = _dot(w, vn_g) + bias
                u = p_ref[rows, cols].astype(F32)
                z = p_ref[rows, zcols].astype(F32)
                dyv = dy_ref[rows, cols].astype(F32)
                sg = _sigmoid(z)
                sz = z * sg
                dyu = dyv * u
                dmixed = dyu * sz
                dp_ref[rows, cols] = (dyv * mixed * sz).astype(dp_ref.dtype)
                dp_ref[rows, zcols] = (dyu * mixed * (sg * (1.0 + z * (1.0 - sg)))).astype(dp_ref.dtype)
                dmc = dmixed.astype(CDT)
                dws = dws + _dot(dmc, vn_g, NT)
                dbs = dbs + jnp.where(lane == g, jnp.sum(dmixed, axis=-1, keepdims=True), 0.0)
                dvn_ref[rows, cols] = _dot(w, dmc, TN)
            dws_ref[g] += jnp.where(tri, dws, 0.0)
        dbs_ref[...] += dbs
        dvn = dvn_ref[...]
        gd = dvn * vg
        dvraw = r * gd - vraw * (r * r * r) * jnp.mean(gd * vraw, axis=-1, keepdims=True)
        dp_ref[:, A_WIDTH:2 * A_WIDTH] = dvraw.astype(dp_ref.dtype)
        dvg_ref[...] += jnp.sum(dvn * vhat, axis=0, keepdims=True)

    return pl.pallas_call(
        body, name=name, grid=(s // tq,),
        in_specs=[pl.BlockSpec((tq, 3 * A_WIDTH), lambda i: (i, 0)), pl.BlockSpec((tq, A_WIDTH), lambda i: (i, 0)),
                  pl.BlockSpec((1, A_WIDTH), lambda i: (0, 0)), pl.BlockSpec((A_GROUPS, CHUNK, CHUNK), lambda i: (0, 0, 0)),
                  pl.BlockSpec((CHUNK, A_GROUPS), lambda i: (0, 0))],
        out_specs=[pl.BlockSpec((tq, 3 * A_WIDTH), lambda i: (i, 0)), pl.BlockSpec((A_GROUPS, CHUNK, CHUNK), lambda i: (0, 0, 0)),
                   pl.BlockSpec((CHUNK, A_GROUPS), lambda i: (0, 0)), pl.BlockSpec((1, A_WIDTH), lambda i: (0, 0))],
        out_shape=[jax.ShapeDtypeStruct((s, 3 * A_WIDTH), CDT), jax.ShapeDtypeStruct((A_GROUPS, CHUNK, CHUNK), F32),
                   jax.ShapeDtypeStruct((CHUNK, A_GROUPS), F32), jax.ShapeDtypeStruct((1, A_WIDTH), F32)],
        scratch_shapes=[pltpu.VMEM((tq, A_WIDTH), F32)],
        compiler_params=_cparams("arbitrary"),
    )(proj, dy, v_gain, w_s, b_s_t)


def _rope_tables(s):
    inv_freq = jnp.power(jnp.float32(ROPE_THETA), -jnp.arange(ROPE_HALF, dtype=F32) / ROPE_HALF)
    ang = jnp.arange(s, dtype=F32)[:, None] * inv_freq[None, :]
    cos, sin = jnp.cos(ang), jnp.sin(ang)
    rest = HEAD_DIM - 2 * ROPE_HALF
    t_c = jnp.concatenate([cos, cos, jnp.ones((s, rest), F32)], axis=1)
    t_a = jnp.concatenate([-sin, jnp.zeros((s, HEAD_DIM - ROPE_HALF), F32)], axis=1)
    t_b = jnp.concatenate([jnp.zeros((s, ROPE_HALF), F32), sin, jnp.zeros((s, rest), F32)], axis=1)
    return t_c, t_a, t_b


def _b_qk_fwd(proj, gains, tabs, name, tq=256):
    s = proj.shape[0]

    def body(p_ref, g_ref, tc_ref, ta_ref, tb_ref, o_ref):
        tc, ta, tb = tc_ref[...], ta_ref[...], tb_ref[...]
        for tg in range(6):
            gain = g_ref[tg:tg + 1, :]
            for h in range(B_HEADS):
                cols = slice(tg * 1024 + h * HEAD_DIM, tg * 1024 + (h + 1) * HEAD_DIM)
                xv = p_ref[:, cols].astype(F32)
                r = lax.rsqrt(_lane_sums(xv * xv) * (1.0 / HEAD_DIM) + EPS)
                xn = xv * r * gain
                y = xn * tc + pltpu.roll(xn, HEAD_DIM - ROPE_HALF, 1) * ta + pltpu.roll(xn, ROPE_HALF, 1) * tb
                o_ref[:, cols] = y.astype(o_ref.dtype)

    tab = pl.BlockSpec((tq, HEAD_DIM), lambda i: (i, 0))
    return pl.pallas_call(
        body, name=name, grid=(s // tq,),
        in_specs=[pl.BlockSpec((tq, B_QK), lambda i: (i, 0)), pl.BlockSpec((6, HEAD_DIM), lambda i: (0, 0)), tab, tab, tab],
        out_specs=pl.BlockSpec((tq, B_QK), lambda i: (i, 0)),
        out_shape=jax.ShapeDtypeStruct((s, B_QK), CDT), compiler_params=_cparams("parallel"),
    )(proj, gains, *tabs)


PERMUTE_BLOCK_BYTES = 4 * 1024 * 1024


def _view_rows(length, dil, width, itemsize):
    rows = 16
    while 2 * rows * dil * width * itemsize <= PERMUTE_BLOCK_BYTES and 2 * rows <= length:
        rows *= 2
    return rows


def _to_view(x, col, width, dil, name):
    s = x.shape[0]
    length = s // dil
    tl = _view_rows(length, dil, width, 4)
    lanes = HEAD_DIM
    nblk = width // lanes

    def body(x_ref, o_ref, slab_ref):
        for b in range(nblk):
            slab_ref[b] = x_ref[:, b * lanes:(b + 1) * lanes].astype(F32)
        for r in range(dil):
            for b in range(nblk):
                o_ref[:, r * width + b * lanes:r * width + (b + 1) * lanes] = (
                    slab_ref.at[b][pl.ds(r, tl, stride=dil), :].astype(o_ref.dtype))

    return pl.pallas_call(
        body, name=name, grid=(length // tl,),
        in_specs=[pl.BlockSpec((tl * dil, width), lambda i: (i, col))],
        out_specs=pl.BlockSpec((tl, dil * width), lambda i: (i, 0)),
        out_shape=jax.ShapeDtypeStruct((length, dil * width), x.dtype),
        scratch_shapes=[pltpu.VMEM((nblk, tl * dil, lanes), F32)],
        compiler_params=_cparams("parallel"),
    )(x)


def _from_view(v, dil, name):
    length, width = v.shape[0], v.shape[1] // dil
    tl = _view_rows(length, dil, width, 4)
    lanes = HEAD_DIM
    nblk = width // lanes

    def body(v_ref, o_ref, slab_ref):
        for r in range(dil):
            for b in range(nblk):
                slab_ref.at[b][pl.ds(r, tl, stride=dil), :] = v_ref[:, r * width + b * lanes:r * width + (b + 1) * lanes].astype(F32)
        for b in range(nblk):
            o_ref[:, b * lanes:(b + 1) * lanes] = slab_ref[b].astype(o_ref.dtype)

    return pl.pallas_call(
        body, name=name, grid=(length // tl,),
        in_specs=[pl.BlockSpec((tl, dil * width), lambda i: (i, 0))],
        out_specs=pl.BlockSpec((tl * dil, width), lambda i: (i, 0)),
        out_shape=jax.ShapeDtypeStruct((length * dil, width), v.dtype),
        scratch_shapes=[pltpu.VMEM((nblk, tl * dil, lanes), F32)],
        compiler_params=_cparams("parallel"),
    )(v)


def _b_attn_fwd(q, k, v, bases, dil, name):
    length = q.shape[0]
    nb = length // CHUNK
    scale = 1.0 / math.sqrt(HEAD_DIM)
    w = B_HEADS * HEAD_DIM
    qb, kb, vb = bases

    def body(q_ref, kc_ref, kp_ref, vc_ref, vp_ref, o_ref, lse_ref):
        n = pl.program_id(1)
        qi = lax.broadcasted_iota(jnp.int32, (CHUNK, 2 * CHUNK), 0)
        ki = lax.broadcasted_iota(jnp.int32, (CHUNK, 2 * CHUNK), 1)
        first_key = jnp.where(n > 0, 0, CHUNK)
        mask = (ki >= qi) & (ki <= qi + CHUNK) & (ki >= first_key)
        lane = lax.broadcasted_iota(jnp.int32, (CHUNK, HEAD_DIM), 1)
        lse_all = jnp.zeros((CHUNK, HEAD_DIM), F32)
        for h in range(B_HEADS):
            sl = slice(h * HEAD_DIM, (h + 1) * HEAD_DIM)
            k2 = jnp.concatenate([kp_ref[:, sl], kc_ref[:, sl]], axis=0)
            v2 = jnp.concatenate([vp_ref[:, sl], vc_ref[:, sl]], axis=0)
            sc = jnp.where(mask, _dot(q_ref[:, sl], k2, NT) * scale, -1e30)
            m = jnp.max(sc, axis=-1, keepdims=True)
            p = jnp.exp(sc - m)
            l = jnp.sum(p, axis=-1, keepdims=True)
            o_ref[:, sl] = _dot(p.astype(CDT), v2) / l
            lse_all = jnp.where(lane == h, m + jnp.log(l), lse_all)
        lse_ref[...] = lse_all

    prev = lambda n: jnp.maximum(n - 1, 0)
    blk = lambda f: pl.BlockSpec((CHUNK, w), f)
    return pl.pallas_call(
        body, name=name, grid=(dil, nb),
        in_specs=[blk(lambda r, n: (n, qb + r)), blk(lambda r, n: (n, kb + r)), blk(lambda r, n: (prev(n), kb + r)),
                  blk(lambda r, n: (n, vb + r)), blk(lambda r, n: (prev(n), vb + r))],
        out_specs=[blk(lambda r, n: (n, r)), pl.BlockSpec((CHUNK, HEAD_DIM), lambda r, n: (n, r))],
        out_shape=[jax.ShapeDtypeStruct((length, dil * w), F32), jax.ShapeDtypeStruct((length, dil * HEAD_DIM), F32)],
        compiler_params=_cparams("parallel", "parallel"),
    )(q, k, k, v, v)


def _b_combine(os_, lses, proj, name, tq=512):
    s = proj.shape[0]
    w = B_HEADS * HEAD_DIM

    def body(o0_ref, o1_ref, o2_ref, l0_ref, l1_ref, l2_ref, z_ref, y_ref, oj_ref, lj_ref):
        l0, l1, l2 = l0_ref[...], l1_ref[...], l2_ref[...]
        m = jnp.maximum(jnp.maximum(l0, l1), l2)
        lj = m + jnp.log(jnp.exp(l0 - m) + jnp.exp(l1 - m) + jnp.exp(l2 - m))
        lj_ref[...] = lj
        w0, w1, w2 = jnp.exp(l0 - lj), jnp.exp(l1 - lj), jnp.exp(l2 - lj)
        for h in range(B_HEADS):
            sl = slice(h * HEAD_DIM, (h + 1) * HEAD_DIM)
            o = w0[:, h:h + 1] * o0_ref[:, sl] + w1[:, h:h + 1] * o1_ref[:, sl] + w2[:, h:h + 1] * o2_ref[:, sl]
            z = z_ref[:, sl].astype(F32)
            oj_ref[:, sl] = o.astype(oj_ref.dtype)
            y_ref[:, sl] = (o * (z * _sigmoid(z))).astype(y_ref.dtype)

    blk = pl.BlockSpec((tq, w), lambda i: (i, 0))
    st = pl.BlockSpec((tq, HEAD_DIM), lambda i: (i, 0))
    return pl.pallas_call(
        body, name=name, grid=(s // tq,),
        in_specs=[blk, blk, blk, st, st, st, pl.BlockSpec((tq, w), lambda i: (i, 9))],
        out_specs=[blk, blk, st],
        out_shape=[jax.ShapeDtypeStruct((s, w), CDT), jax.ShapeDtypeStruct((s, w), CDT), jax.ShapeDtypeStruct((s, HEAD_DIM), F32)],
        compiler_params=_cparams("parallel"),
    )(*os_, *lses, proj)


def _b_bwd_prep(dy, oj, proj, name, tq=512):
    s = proj.shape[0]
    w = B_HEADS * HEAD_DIM

    def body(dy_ref, oj_ref, z_ref, do_ref, dz_ref, dd_ref):
        lane = lax.broadcasted_iota(jnp.int32, (tq, HEAD_DIM), 1)
        dd = jnp.zeros((tq, HEAD_DIM), F32)
        for h in range(B_HEADS):
            sl = slice(h * HEAD_DIM, (h + 1) * HEAD_DIM)
            z = z_ref[:, sl].astype(F32)
            dyv = dy_ref[:, sl].astype(F32)
            o = oj_ref[:, sl].astype(F32)
            sg = _sigmoid(z)
            do = dyv * (z * sg)
            do_ref[:, sl] = do.astype(do_ref.dtype)
            dz_ref[:, sl] = (dyv * o * (sg * (1.0 + z * (1.0 - sg)))).astype(dz_ref.dtype)
            dd = jnp.where(lane == h, jnp.sum(do * o, axis=-1, keepdims=True), dd)
        dd_ref[...] = dd

    blk = pl.BlockSpec((tq, w), lambda i: (i, 0))
    st = pl.BlockSpec((tq, HEAD_DIM), lambda i: (i, 0))
    return pl.pallas_call(
        body, name=name, grid=(s // tq,),
        in_specs=[blk, blk, pl.BlockSpec((tq, w), lambda i: (i, 9))], out_specs=[blk, blk, st],
        out_shape=[jax.ShapeDtypeStruct((s, w), CDT), jax.ShapeDtypeStruct((s, w), CDT), jax.ShapeDtypeStruct((s, HEAD_DIM), F32)],
        compiler_params=_cparams("parallel"),
    )(dy, oj, proj)


def _b_attn_bwd(q, k, v, bases, do, lj, dd, dil, name):
    length = q.shape[0]
    nb = length // CHUNK
    scale = 1.0 / math.sqrt(HEAD_DIM)
    w = B_HEADS * HEAD_DIM
    qb, kb, vb = bases

    def body(qj_ref, qn_ref, k_ref, v_ref, doj_ref, don_ref, lj_ref, ln_ref, dj_ref, dn_ref, out_ref, carry_ref):
        j = pl.program_id(1)

        @pl.when(j == 0)
        def _():
            carry_ref[...] = jnp.zeros_like(carry_ref)

        qi = lax.broadcasted_iota(jnp.int32, (2 * CHUNK, CHUNK), 0)
        ki = lax.broadcasted_iota(jnp.int32, (2 * CHUNK, CHUNK), 1)
        no_next = jnp.where(j + 1 < nb, 0, 2 * CHUNK)
        mask = ((qi < CHUNK) & (ki <= qi)) | ((qi >= CHUNK) & (ki >= qi - CHUNK + no_next))
        for h in range(B_HEADS):
            sl = slice(h * HEAD_DIM, (h + 1) * HEAD_DIM)
            q2 = jnp.concatenate([qj_ref[:, sl], qn_ref[:, sl]], axis=0)
            do2 = jnp.concatenate([doj_ref[:, sl], don_ref[:, sl]], axis=0)
            lse2 = jnp.concatenate([lj_ref[:, h:h + 1], ln_ref[:, h:h + 1]], axis=0)
            d2 = jnp.concatenate([dj_ref[:, h:h + 1], dn_ref[:, h:h + 1]], axis=0)
            k = k_ref[:, sl]
            v = v_ref[:, sl]
            sc = _dot(q2, k, NT) * scale
            p = jnp.where(mask, jnp.exp(sc - lse2), 0.0)
            dp = _dot(do2, v, NT)
            ds = (p * (dp - d2) * scale).astype(CDT)
            dq2 = _dot(ds, k)
            out_ref[:, sl] = (carry_ref[:, sl] + dq2[:CHUNK]).astype(out_ref.dtype)
            carry_ref[:, sl] = dq2[CHUNK:]
            out_ref[:, w + h * HEAD_DIM:w + (h + 1) * HEAD_DIM] = _dot(ds, q2, TN).astype(out_ref.dtype)
            out_ref[:, 2 * w + h * HEAD_DIM:2 * w + (h + 1) * HEAD_DIM] = _dot(p.astype(CDT), do2, TN).astype(out_ref.dtype)

    nxt = lambda j: jnp.minimum(j + 1, nb - 1)
    blk = lambda f: pl.BlockSpec((CHUNK, w), f)
    st = lambda f: pl.BlockSpec((CHUNK, HEAD_DIM), f)
    return pl.pallas_call(
        body, name=name, grid=(dil, nb),
        in_specs=[blk(lambda r, j: (j, qb + r)), blk(lambda r, j: (nxt(j), qb + r)),
                  blk(lambda r, j: (j, kb + r)), blk(lambda r, j: (j, vb + r)),
                  blk(lambda r, j: (j, r)), blk(lambda r, j: (nxt(j), r)),
                  st(lambda r, j: (j, r)), st(lambda r, j: (nxt(j), r)), st(lambda r, j: (j, r)), st(lambda r, j: (nxt(j), r))],
        out_specs=pl.BlockSpec((CHUNK, 3 * w), lambda r, j: (j, r)),
        out_shape=jax.ShapeDtypeStruct((length, dil * 3 * w), CDT),
        scratch_shapes=[pltpu.VMEM((CHUNK, w), F32)],
        compiler_params=_cparams("parallel", "arbitrary"),
    )(q, q, k, v, do, do, lj, lj, dd, dd)


def _b_qk_bwd(proj, dqkv, dz, gains, tabs, name, tq=256):
    s = proj.shape[0]
    w = B_HEADS * HEAD_DIM

    def body(p_ref, d0_ref, d1_ref, d2_ref, dz_ref, g_ref, tc_ref, ta_ref, tb_ref, dp_ref, dg_ref):
        @pl.when(pl.program_id(0) == 0)
        def _():
            dg_ref[...] = jnp.zeros_like(dg_ref)

        tc, ta, tb = tc_ref[...], ta_ref[...], tb_ref[...]
        d_refs = (d0_ref, d1_ref, d2_ref)
        for g in range(3):
            for t in range(2):
                tg = t * 3 + g
                gain = g_ref[tg:tg + 1, :]
                dgain = jnp.zeros((1, HEAD_DIM), F32)
                for h in range(B_HEADS):
                    cols = slice(tg * w + h * HEAD_DIM, tg * w + (h + 1) * HEAD_DIM)
                    xv = p_ref[:, cols].astype(F32)
                    r = lax.rsqrt(_lane_sums(xv * xv) * (1.0 / HEAD_DIM) + EPS)
                    dyv = d_refs[g][:, t * w + h * HEAD_DIM:t * w + (h + 1) * HEAD_DIM].astype(F32)
                    dxn = dyv * tc + pltpu.roll(dyv * ta, ROPE_HALF, 1) + pltpu.roll(dyv * tb, HEAD_DIM - ROPE_HALF, 1)
                    gd = dxn * gain
                    dx = r * gd - xv * (r * r * r) * (_lane_sums(gd * xv) * (1.0 / HEAD_DIM))
                    dp_ref[:, cols] = dx.astype(dp_ref.dtype)
                    dgain = dgain + jnp.sum(dxn * xv * r, axis=0, keepdims=True)
                dg_ref[tg:tg + 1, :] += dgain
            dp_ref[:, (6 + g) * w:(7 + g) * w] = d_refs[g][:, 2 * w:3 * w]
        dp_ref[:, 9 * w:10 * w] = dz_ref[...]

    tab = pl.BlockSpec((tq, HEAD_DIM), lambda i: (i, 0))
    dblk = pl.BlockSpec((tq, 3 * w), lambda i: (i, 0))
    return pl.pallas_call(
        body, name=name, grid=(s // tq,),
        in_specs=[pl.BlockSpec((tq, B_QK), lambda i: (i, 0)), dblk, dblk, dblk, pl.BlockSpec((tq, w), lambda i: (i, 0)),
                  pl.BlockSpec((6, HEAD_DIM), lambda i: (0, 0)), tab, tab, tab],
        out_specs=[pl.BlockSpec((tq, B_IN), lambda i: (i, 0)), pl.BlockSpec((6, HEAD_DIM), lambda i: (0, 0))],
        out_shape=[jax.ShapeDtypeStruct((s, B_IN), CDT), jax.ShapeDtypeStruct((6, HEAD_DIM), F32)],
        compiler_params=_cparams("arbitrary"),
    )(proj, *dqkv, dz, gains, *tabs)


def _inv_count(t, window):
    return 1.0 / jnp.minimum(t + 1, window).astype(F32)


def _c_pool_fwd(proj, name, tq=512):
    s = proj.shape[0]
    per = tq // POOL_HALO

    def body(x_ref, halo_ref, o_ref, ext_ref):
        i = pl.program_id(0)
        xv = x_ref[...].astype(F32)
        ext_ref[POOL_HALO:, :] = xv
        ext_ref[:POOL_HALO, :] = jnp.where(i > 0, halo_ref[...].astype(F32), 0.0)
        t = i * tq + lax.broadcasted_iota(jnp.int32, (tq, 1), 0)
        for g, window in enumerate(POOL_SIZES):
            cols = slice(g * C_GROUP, (g + 1) * C_GROUP)
            acc = xv[:, cols]
            for back in range(1, window):
                acc = acc + ext_ref[POOL_HALO - back:POOL_HALO - back + tq, cols]
            o_ref[:, cols] = (acc * _inv_count(t, window) - xv[:, cols]).astype(o_ref.dtype)

    return pl.pallas_call(
        body, name=name, grid=(s // tq,),
        in_specs=[pl.BlockSpec((tq, C_WIDTH), lambda i: (i, 0)),
                  pl.BlockSpec((POOL_HALO, C_WIDTH), lambda i: (jnp.maximum(i * per - 1, 0), 0))],
        out_specs=pl.BlockSpec((tq, C_WIDTH), lambda i: (i, 0)),
        out_shape=jax.ShapeDtypeStruct((s, C_WIDTH), CDT),
        scratch_shapes=[pltpu.VMEM((tq + POOL_HALO, C_WIDTH), F32)],
        compiler_params=_cparams("parallel"),
    )(proj, proj)


GRP_SHARD = (N_CHIPS, 2, 256, C_GROUP)
GRP_ROWS = C_GROUP // N_CHIPS


def _grp_rows(g):
    return g // 2, slice((g % 2) * GRP_ROWS, (g % 2 + 1) * GRP_ROWS)


def _grp_weight(w_ref, g):
    half, rows = _grp_rows(g)
    return jnp.concatenate([w_ref[k, half, rows, :] for k in range(N_CHIPS)], axis=0)


def _c_mid_fwd(diff, proj, w_grp, scale, name, tq=512):
    s = diff.shape[0]

    def body(d_ref, z_ref, w_ref, sc_ref, y_ref):
        for g in range(len(POOL_SIZES)):
            cols = slice(g * C_GROUP, (g + 1) * C_GROUP)
            z = z_ref[:, cols].astype(F32)
            y_ref[:, cols] = (_dot(d_ref[:, cols], _grp_weight(w_ref, g)) * sc_ref[:, cols] * (z * _sigmoid(z))).astype(y_ref.dtype)

    return pl.pallas_call(
        body, name=name, grid=(s // tq,),
        in_specs=[pl.BlockSpec((tq, C_WIDTH), lambda i: (i, 0)), pl.BlockSpec((tq, C_WIDTH), lambda i: (i, 1)),
                  pl.BlockSpec(GRP_SHARD, lambda i: (0, 0, 0, 0)), pl.BlockSpec((1, C_WIDTH), lambda i: (0, 0))],
        out_specs=pl.BlockSpec((tq, C_WIDTH), lambda i: (i, 0)),
        out_shape=jax.ShapeDtypeStruct((s, C_WIDTH), CDT), compiler_params=_cparams("parallel"),
    )(diff, proj, w_grp, scale)


def _c_mid_bwd(diff, proj, w_grp, scale, dy, name, tq=512):
    s = diff.shape[0]

    def body(d_ref, z_ref, w_ref, sc_ref, dy_ref, dd_ref, dz_ref, dw_ref, dsc_ref):
        @pl.when(pl.program_id(0) == 0)
        def _():
            dw_ref[...] = jnp.zeros_like(dw_ref)
            dsc_ref[...] = jnp.zeros_like(dsc_ref)

        for g in range(len(POOL_SIZES)):
            cols = slice(g * C_GROUP, (g + 1) * C_GROUP)
            d = d_ref[:, cols]
            wg = _grp_weight(w_ref, g)
            half, rows = _grp_rows(g)
            m0 = _dot(d, wg)
            z = z_ref[:, cols].astype(F32)
            dyv = dy_ref[:, cols].astype(F32)
            sc = sc_ref[:, cols]
            sg = _sigmoid(z)
            dmixed = dyv * (z * sg)
            dz_ref[:, cols] = (dyv * m0 * sc * (sg * (1.0 + z * (1.0 - sg)))).astype(dz_ref.dtype)
            dsc_ref[:, cols] += jnp.sum(dmixed * m0, axis=0, keepdims=True)
            dm0 = (dmixed * sc).astype(CDT)
            dwg = _dot(d, dm0, TN)
            for k in range(N_CHIPS):
                dw_ref[k, half, rows, :] += dwg[k * GRP_ROWS:(k + 1) * GRP_ROWS]
            dd_ref[:, cols] = _dot(dm0, wg, NT)

    blk = pl.BlockSpec((tq, C_WIDTH), lambda i: (i, 0))
    wsp = pl.BlockSpec(GRP_SHARD, lambda i: (0, 0, 0, 0))
    vec = pl.BlockSpec((1, C_WIDTH), lambda i: (0, 0))
    return pl.pallas_call(
        body, name=name, grid=(s // tq,),
        in_specs=[blk, pl.BlockSpec((tq, C_WIDTH), lambda i: (i, 1)), wsp, vec, blk],
        out_specs=[blk, blk, wsp, vec],
        out_shape=[jax.ShapeDtypeStruct((s, C_WIDTH), F32), jax.ShapeDtypeStruct((s, C_WIDTH), CDT),
                   jax.ShapeDtypeStruct(GRP_SHARD, F32), jax.ShapeDtypeStruct((1, C_WIDTH), F32)],
        compiler_params=_cparams("arbitrary"),
    )(diff, proj, w_grp, scale, dy)


def _c_pool_bwd(ddiff, dz, name, tq=512):
    s = ddiff.shape[0]
    per = tq // POOL_HALO
    last = s // tq - 1

    def body(d_ref, halo_ref, dz_ref, o_ref, ext_ref):
        i = pl.program_id(0)
        dv = d_ref[...]
        t = i * tq + lax.broadcasted_iota(jnp.int32, (tq, 1), 0)
        for g, window in enumerate(POOL_SIZES):
            cols = slice(g * C_GROUP, (g + 1) * C_GROUP)
            ext_ref[:tq, cols] = dv[:, cols] * _inv_count(t, window)
            ext_ref[tq:, cols] = jnp.where(i < last, halo_ref[:, cols] * (1.0 / window), 0.0)
            acc = -dv[:, cols]
            for fwd in range(window):
                acc = acc + ext_ref[fwd:fwd + tq, cols]
            o_ref[:, cols] = acc.astype(o_ref.dtype)
        o_ref[:, C_WIDTH:] = dz_ref[...]

    return pl.pallas_call(
        body, name=name, grid=(s // tq,),
        in_specs=[pl.BlockSpec((tq, C_WIDTH), lambda i: (i, 0)),
                  pl.BlockSpec((POOL_HALO, C_WIDTH), lambda i: (jnp.minimum((i + 1) * per, s // POOL_HALO - 1), 0)),
                  pl.BlockSpec((tq, C_WIDTH), lambda i: (i, 0))],
        out_specs=pl.BlockSpec((tq, 2 * C_WIDTH), lambda i: (i, 0)),
        out_shape=jax.ShapeDtypeStruct((s, 2 * C_WIDTH), CDT),
        scratch_shapes=[pltpu.VMEM((tq + POOL_HALO, C_WIDTH), F32)],
        compiler_params=_cparams("parallel"),
    )(ddiff, ddiff, dz)


LAYERS = (("a", 0), ("b", 0), ("c", 0), ("a", 1))
def _shard_spec(block, layer, where):
    if layer is None:
        return pl.BlockSpec((None,) + block, where)
    return pl.BlockSpec((None, None) + block, lambda i, j, q: (where(i, j, q)[0], layer) + where(i, j, q)[1:])


def _w_in_fwd(h, gw, layer, name):
    s, width = h.shape[0], gw.shape[-1]
    return _mm(h, gw, "nn", CDT, name, (s, N_CHIPS * width, D_MODEL), (1024, width, D_MODEL),
               b_spec=_shard_spec((D_MODEL, width), layer, lambda i, j, q: (j, 0, 0)))


def _w_in_dh(dproj, gw, layer, name):
    s, width = dproj.shape[0], gw.shape[-1]
    return _mm(dproj, gw, "nt", F32, name, (s, D_MODEL, N_CHIPS * width), (1024, D_MODEL, width),
               b_spec=_shard_spec((D_MODEL, width), layer, lambda i, j, q: (q, 0, 0)))


def _w_in_grad(h, dproj, width, name):
    s = h.shape[0]
    tokens = 2048 if width <= 1536 else 1024
    return _mm(h, dproj, "tn", CDT, name, (D_MODEL, N_CHIPS * width, s), (D_MODEL, width, tokens),
               o_spec=pl.BlockSpec((None, D_MODEL, width), lambda i, j, q: (j, 0, 0)), o_shape=(N_CHIPS, D_MODEL, width))


def _w_out_spec(gw, layer):
    rows = gw.shape[-2]
    if layer is None:
        return pl.BlockSpec((N_CHIPS, rows, D_MODEL), lambda i: (0, 0, 0))
    return pl.BlockSpec((N_CHIPS, None, rows, D_MODEL), lambda i: (0, layer, 0, 0))


def _w_out_fwd(y, gw, layer, x, name, tm=512):
    s, k = y.shape

    def body(y_ref, w_ref, x_ref, o_ref):
        o_ref[...] = x_ref[...] + _dot(y_ref[...], w_ref[...].reshape(k, D_MODEL))

    blk = pl.BlockSpec((tm, D_MODEL), lambda i: (i, 0))
    return pl.pallas_call(
        body, name=name, grid=(s // tm,), in_specs=[pl.BlockSpec((tm, k), lambda i: (i, 0)), _w_out_spec(gw, layer), blk],
        out_specs=blk, out_shape=jax.ShapeDtypeStruct((s, D_MODEL), F32), compiler_params=_cparams("parallel"),
    )(y, gw, x)


def _w_out_dy(dxc, gw, layer, name, tm=1024):
    s, k = dxc.shape[0], N_CHIPS * gw.shape[-2]
    tm = min(tm, s)

    def body(dx_ref, w_ref, o_ref):
        o_ref[...] = _dot(dx_ref[...], w_ref[...].reshape(k, D_MODEL), NT).astype(o_ref.dtype)

    return pl.pallas_call(
        body, name=name, grid=(s // tm,), in_specs=[pl.BlockSpec((tm, D_MODEL), lambda i: (i, 0)), _w_out_spec(gw, layer)],
        out_specs=pl.BlockSpec((tm, k), lambda i: (i, 0)), out_shape=jax.ShapeDtypeStruct((s, k), CDT),
        compiler_params=_cparams("parallel"),
    )(dxc, gw)


def _w_out_grad(y, dxc, rows, name, tokens=1024):
    s, k = y.shape
    tokens = min(tokens, s)
    steps = s // tokens

    def body(y_ref, dx_ref, o_ref, acc_ref):
        i = pl.program_id(0)
        p = _dot(y_ref[...], dx_ref[...], TN)

        @pl.when(i == 0)
        def _():
            acc_ref[...] = p

        @pl.when(i > 0)
        def _():
            acc_ref[...] += p

        @pl.when(i == steps - 1)
        def _():
            o_ref[...] = acc_ref[...].reshape(N_CHIPS, rows, D_MODEL).astype(o_ref.dtype)

    return pl.pallas_call(
        body, name=name, grid=(steps,),
        in_specs=[pl.BlockSpec((tokens, k), lambda i: (i, 0)), pl.BlockSpec((tokens, D_MODEL), lambda i: (i, 0))],
        out_specs=pl.BlockSpec((N_CHIPS, rows, D_MODEL), lambda i: (0, 0, 0)),
        out_shape=jax.ShapeDtypeStruct((N_CHIPS, rows, D_MODEL), CDT),
        scratch_shapes=[pltpu.VMEM((k, D_MODEL), F32)], compiler_params=_cparams("arbitrary"),
    )(y, dxc)


def _local_step(x, target, w):
    s = x.shape[0]
    tabs = _rope_tables(s)
    qk_gains = jnp.concatenate([w["b_q_gain"][0], w["b_k_gain"][0]], axis=0)
    saved = []
    for li, (kind, j) in enumerate(LAYERS):
        h = _rms_fwd(x, w["norm_gain"][li:li + 1], f"rms_fwd{li}")
        if kind == "a":
            proj = _w_in_fwd(h, w["a_w_in"], j, f"a_in{li}")
            bs_t = w["a_b_s"][j].T
            y = _a_mid_fwd(proj, w["a_v_gain"][j:j + 1], w["a_w_s"][j], bs_t, f"a_mid_fwd{li}")
            x_next = _w_out_fwd(y, w["a_w_out"], j, x, f"a_out{li}")
            saved.append((x, h, proj, y))
        elif kind == "b":
            proj = _w_in_fwd(h, w["b_w_in"], None, f"b_in{li}")
            qk = _b_qk_fwd(proj, qk_gains, tabs, f"b_qk_fwd{li}")
            qkv, os_, lses = [], [], []
            for g, dil in enumerate(B_DILATIONS):
                if dil == 1:
                    ops = (qk, qk, proj, (g, 3 + g, 6 + g))
                else:
                    ops = (_to_view(qk, g, 1024, dil, f"b_q_view{li}_{g}"), _to_view(qk, 3 + g, 1024, dil, f"b_k_view{li}_{g}"),
                           _to_view(proj, 6 + g, 1024, dil, f"b_v_view{li}_{g}"), (0, 0, 0))
                o, lse = _b_attn_fwd(*ops, dil, f"b_attn_fwd{li}_{g}")
                if dil > 1:
                    o, lse = _from_view(o, dil, f"b_o_nat{li}_{g}"), _from_view(lse, dil, f"b_lse_nat{li}_{g}")
                qkv.append(ops)
                os_.append(o)
                lses.append(lse)
            y, oj, lj = _b_combine(os_, lses, proj, f"b_combine{li}")
            x_next = _w_out_fwd(y, w["b_w_out"], None, x, f"b_out{li}")
            saved.append((x, h, proj, y, qkv, oj, lj))
        else:
            proj = _w_in_fwd(h, w["c_w_in"], None, f"c_in{li}")
            diff = _c_pool_fwd(proj, f"c_pool_fwd{li}")
            y = _c_mid_fwd(diff, proj, w["c_w_grp"], w["c_scale"][j:j + 1], f"c_mid_fwd{li}")
            x_next = _w_out_fwd(y, w["c_w_out"], None, x, f"c_out{li}")
            saved.append((x, h, proj, y, diff))
        x = x_next

    dx, dxc, sq = _loss_bwd(x, target, "loss_bwd")
    grads = {"norm_gain": [None] * len(LAYERS), "a_w_in": [None, None], "a_v_gain": [None, None], "a_w_s": [None, None],
             "a_b_s": [None, None], "a_w_out": [None, None]}
    for li in reversed(range(len(LAYERS))):
        kind, j = LAYERS[li]
        sv = saved[li]
        xin, h, proj, y = sv[:4]
        if kind == "a":
            grads["a_w_out"][j] = _w_out_grad(y, dxc, 512, f"a_dwout{li}")
            dy = _w_out_dy(dxc, w["a_w_out"], j, f"a_dy{li}")
            dproj, dws, dbs_t, dvg = _a_mid_bwd(proj, dy, w["a_v_gain"][j:j + 1], w["a_w_s"][j], w["a_b_s"][j].T, f"a_mid_bwd{li}")
            grads["a_w_s"][j], grads["a_b_s"][j], grads["a_v_gain"][j] = dws, dbs_t.T, dvg[0]
            grads["a_w_in"][j] = _w_in_grad(h, dproj, 1536, f"a_dwin{li}")
            dh = _w_in_dh(dproj, w["a_w_in"], j, f"a_dh{li}")
        elif kind == "b":
            qkv, oj, lj = sv[4:]
            grads["b_w_out"] = _w_out_grad(y, dxc, 256, f"b_dwout{li}")
            dy = _w_out_dy(dxc, w["b_w_out"], None, f"b_dy{li}")
            do, dz, dd = _b_bwd_prep(dy, oj, proj, f"b_bwd_prep{li}")
            dqkv = []
            for g, dil in enumerate(B_DILATIONS):
                stats = (do, lj, dd)
                if dil > 1:
                    stats = (_to_view(do, 0, 1024, dil, f"b_do_view{li}_{g}"), _to_view(lj, 0, HEAD_DIM, dil, f"b_lj_view{li}_{g}"),
                             _to_view(dd, 0, HEAD_DIM, dil, f"b_dd_view{li}_{g}"))
                d = _b_attn_bwd(*qkv[g], *stats, dil, f"b_attn_bwd{li}_{g}")
                dqkv.append(_from_view(d, dil, f"b_dqkv_nat{li}_{g}") if dil > 1 else d)
            dproj, dgains = _b_qk_bwd(proj, dqkv, dz, qk_gains, tabs, f"b_qk_bwd{li}")
            grads["b_q_gain"], grads["b_k_gain"] = dgains[None, :3], dgains[None, 3:]
            grads["b_w_in"] = _w_in_grad(h, dproj, 2560, f"b_dwin{li}")
            dh = _w_in_dh(dproj, w["b_w_in"], None, f"b_dh{li}")
        else:
            diff = sv[4]
            grads["c_w_out"] = _w_out_grad(y, dxc, 512, f"c_dwout{li}")
            dy = _w_out_dy(dxc, w["c_w_out"], None, f"c_dy{li}")
            ddiff, dz, dwg, dsc = _c_mid_bwd(diff, proj, w["c_w_grp"], w["c_scale"][j:j + 1], dy, f"c_mid_bwd{li}")
            grads["c_w_grp"], grads["c_scale"] = dwg, dsc
            dproj = _c_pool_bwd(ddiff, dz, f"c_pool_bwd{li}")
            grads["c_w_in"] = _w_in_grad(h, dproj, 1024, f"c_dwin{li}")
            dh = _w_in_dh(dproj, w["c_w_in"], None, f"c_dh{li}")
        dx, dxc, dng = _rms_bwd(xin, dh, dx, w["norm_gain"][li:li + 1], f"rms_bwd{li}")
        grads["norm_gain"][li] = dng[0]
    for name in ("norm_gain", "a_v_gain", "a_w_s", "a_b_s"):
        grads[name] = jnp.stack(grads[name])
    return sq, dx, grads


ROW = 1024
GATHER_UNITS = (("a_w_in", 1024, 1536), ("a_w_out", 512, 1024), ("b_w_in", 512, 2560), ("b_w_out", 128, 1024),
                ("c_w_in", 512, 1024), ("c_w_grp", 256, 512), ("c_w_out", 256, 1024))
GAIN_ROWS = 16
REDUCE_UNITS = (("a_w_in", 0, 512, 1536), ("a_w_in", 1, 512, 1536), ("a_w_out", 0, 256, 1024), ("a_w_out", 1, 256, 1024),
                ("b_w_in", 0, 512, 2560), ("b_w_out", 0, 128, 1024), ("c_w_in", 0, 512, 1024), ("c_w_grp", 0, 256, 512),
                ("c_w_out", 0, 256, 1024))
SMALL_OF_HALF = ("a_v_gain", "c_scale")
SMALL_PAD = 16
REP = (("norm_gain", (4, 1024)), ("a_w_s", (2, 8, 128, 128)), ("a_b_s", (2, 8, 128)), ("b_q_gain", (1, 3, 128)), ("b_k_gain", (1, 3, 128)))
REP_CORE = 48
REP_PART = 2 * REP_CORE
SMALL_ROWS = SMALL_PAD + REP_CORE
BLOCK_SHAPES = {"a_w_in": (2, 1024, 1536), "a_v_gain": (2, 512), "a_w_out": (2, 512, 1024), "b_w_in": (1, 1024, 2560),
                "b_w_out": (1, 256, 1024), "c_w_in": (1, 1024, 1024), "c_w_grp": (1, 4, 128, 512), "c_scale": (1, 512),
                "c_w_out": (1, 512, 1024)}
WEIGHTS = ("norm_gain", "a_w_in", "a_v_gain", "a_w_s", "a_b_s", "a_w_out", "b_w_in", "b_q_gain", "b_k_gain", "b_w_out",
           "c_w_in", "c_w_grp", "c_scale", "c_w_out")


def _rows(a, rows):
    a = a.reshape(-1)
    return jnp.pad(a, (0, rows * ROW - a.shape[0])).reshape(rows, ROW)


def _small_unit(vecs, rep_part, dtype):
    halves = [jnp.concatenate([_rows(vecs[h].astype(dtype), SMALL_PAD), rep_part[h * REP_CORE:(h + 1) * REP_CORE].astype(dtype)])
              for h in range(2)]
    return jnp.stack(halves)


def _pack_rep(tree):
    return _rows(jnp.concatenate([tree[n].astype(F32).reshape(-1) for n, _ in REP]), N_CHIPS * REP_PART)


def _unpack_rep(slab):
    flat, out, off = slab.reshape(-1), {}, 0
    for n, shape in REP:
        size = math.prod(shape)
        out[n] = flat[off:off + size].reshape(shape)
        off += size
    return out


def _into_slot(x, me, name):
    rows, width = x.shape
    tr = min(rows, 256)

    def body(me_ref, x_ref, o_ref):
        o_ref[...] = x_ref[...].astype(o_ref.dtype)

    return pl.pallas_call(
        body, name=name,
        grid_spec=pltpu.PrefetchScalarGridSpec(
            num_scalar_prefetch=1, grid=(rows // tr,), in_specs=[pl.BlockSpec((tr, width), lambda i, me_ref: (i, 0))],
            out_specs=pl.BlockSpec((None, tr, width), lambda i, me_ref: (me_ref[0], i, 0))),
        out_shape=jax.ShapeDtypeStruct((N_CHIPS, rows, width), CDT), compiler_params=_cparams("parallel"),
    )(me.reshape(1), x)


def _gather_inputs(wts, me):
    units = [_into_slot(wts[n].reshape(2 * r, w), me, f"slot_{n}").reshape(N_CHIPS, 2, r, w) for n, r, w in GATHER_UNITS]
    gains = jnp.concatenate([wts["a_v_gain"].reshape(-1), wts["c_scale"].reshape(-1)])
    gains = _rows(lax.bitcast_convert_type(gains, CDT), 2 * GAIN_ROWS)
    return units + [_into_slot(gains, me, "slot_gains").reshape(N_CHIPS, 2, GAIN_ROWS, ROW)]


def _gathered_weights(units):
    out = {n: u for (n, _, _), u in zip(GATHER_UNITS, units)}
    for n in ("b_w_in", "b_w_out", "c_w_in", "c_w_out"):
        u = out[n]
        out[n] = u.reshape(N_CHIPS, 2 * u.shape[2], u.shape[3])
    gains = lax.bitcast_convert_type(units[-1][:, 0, :3].reshape(N_CHIPS, 1536, 2), F32)
    out["a_v_gain"] = jnp.concatenate([gains[k, :1024].reshape(2, 512) for k in range(N_CHIPS)], axis=1)
    out["c_scale"] = jnp.concatenate([gains[k, 1024:].reshape(1, 512) for k in range(N_CHIPS)], axis=1)
    return out


def _place():
    x, y, c = lax.axis_index("x"), lax.axis_index("y"), lax.axis_index("c")
    chips = [(1 - x, y), (x, 1 - y), (1 - x, 1 - y)]
    return x, y, c, 2 * x + y, (x, y, 1 - c), chips


def _remote(src, dst, sems, j, to):
    send_sems, recv_sems = sems
    return pltpu.make_async_remote_copy(src_ref=src, dst_ref=dst, send_sem=send_sems.at[j], recv_sem=recv_sems.at[j],
                                        device_id=to, device_id_type=MESH)


def _comm_call(body, name, out_shape, n_sems, *args):
    return pl.pallas_call(
        body, name=name, in_specs=[ANY] * len(args), out_specs=ANY, out_shape=out_shape,
        scratch_shapes=[pltpu.SemaphoreType.DMA((n_sems,)), pltpu.SemaphoreType.DMA((n_sems,))],
    )(*args)


def _comm_call_multi(body, name, out_shapes, n_sems, args, aliases=None):
    return pl.pallas_call(
        body, name=name, in_specs=[ANY] * len(args), out_specs=[ANY] * len(out_shapes), out_shape=out_shapes,
        scratch_shapes=[pltpu.SemaphoreType.DMA((n_sems,)), pltpu.SemaphoreType.DMA((n_sems,))],
        input_output_aliases=aliases or {},
    )(*args)


def _allgather_units(units):
    n = len(units)

    def body(*refs):
        outs, sems = refs[n:2 * n], (refs[2 * n], refs[2 * n + 1])
        x, y, c, me, sibling, chips = _place()
        first, passed = [], []
        for u, o_ref in enumerate(outs):
            for j, chip in enumerate(chips):
                first.append(_remote(o_ref.at[me, c], o_ref.at[me, c], sems, 6 * u + j, (*chip, c)))
                first[-1].start()
        for u, o_ref in enumerate(outs):
            for j, (cx, cy) in enumerate(chips):
                landed = o_ref.at[2 * cx + cy, c]
                _remote(landed, landed, sems, 6 * u + j, sibling).wait_recv()
                passed.append(_remote(landed, landed, sems, 6 * u + 3 + j, sibling))
                passed[-1].start()
        for u, o_ref in enumerate(outs):
            for j, (cx, cy) in enumerate(chips):
                landed = o_ref.at[2 * cx + cy, 1 - c]
                _remote(landed, landed, sems, 6 * u + 3 + j, sibling).wait_recv()
        for cp in first + passed:
            cp.wait_send()

    return _comm_call_multi(body, "allgather_weights", [jax.ShapeDtypeStruct(u.shape, u.dtype) for u in units], 6 * n, units,
                            aliases={u: u for u in range(n)})


def _swap_halves(units):
    n = len(units)

    def body(*refs):
        ins, outs, sems = refs[:n], refs[n:2 * n], (refs[2 * n], refs[2 * n + 1])
        x, y, c, me, sibling, chips = _place()
        sent = [_remote(g_ref.at[:, 1 - c], o_ref, sems, u, sibling) for u, (g_ref, o_ref) in enumerate(zip(ins, outs))]
        for cp in sent:
            cp.start()
        for cp in sent:
            cp.wait()

    shapes = [jax.ShapeDtypeStruct((N_CHIPS,) + u.shape[2:], u.dtype) for u in units]
    return _comm_call_multi(body, "grad_swap_halves", shapes, n, units)


def _scatter_shards(units):
    n = len(units)

    def body(*refs):
        ins, outs, sems = refs[:n], refs[n:2 * n], (refs[2 * n], refs[2 * n + 1])
        x, y, c, me, sibling, chips = _place()
        sent = []
        for u, (s_ref, o_ref) in enumerate(zip(ins, outs)):
            for j, (cx, cy) in enumerate(chips):
                sent.append(_remote(s_ref.at[2 * cx + cy], o_ref.at[me], sems, 3 * u + j, (cx, cy, c)))
                sent[-1].start()
        for u, o_ref in enumerate(outs):
            for j, (cx, cy) in enumerate(chips):
                slot = o_ref.at[2 * cx + cy]
                _remote(slot, slot, sems, 3 * u + j, sibling).wait_recv()
        for cp in sent:
            cp.wait_send()

    return _comm_call_multi(body, "grad_scatter_shards", [jax.ShapeDtypeStruct(u.shape, u.dtype) for u in units], 3 * n, units)


def _share_halves(units):
    n = len(units)

    def body(*refs):
        outs, sems = refs[n:2 * n], (refs[2 * n], refs[2 * n + 1])
        x, y, c, me, sibling, chips = _place()
        sent = [_remote(o_ref.at[:, c], o_ref.at[:, c], sems, u, sibling) for u, o_ref in enumerate(outs)]
        for cp in sent:
            cp.start()
        for u, o_ref in enumerate(outs):
            theirs = o_ref.at[:, 1 - c]
            _remote(theirs, theirs, sems, u, sibling).wait_recv()
        for cp in sent:
            cp.wait_send()

    return _comm_call_multi(body, "share_halves", [jax.ShapeDtypeStruct(u.shape, u.dtype) for u in units], n, units,
                            aliases={u: u for u in range(n)})


def _allgather_rep(rep4):
    def body(r_ref, o_ref, send_sems, recv_sems):
        x, y, c, me, sibling, chips = _place()
        sems = (send_sems, recv_sems)
        sent = [_remote(r_ref, o_ref.at[me], sems, j, (*chip, c)) for j, chip in enumerate(chips)]
        for cp in sent:
            cp.start()
        for j, (cx, cy) in enumerate(chips):
            slot = o_ref.at[2 * cx + cy]
            _remote(slot, slot, sems, j, sibling).wait_recv()
        for cp in sent:
            cp.wait_send()

    return _comm_call(body, "allgather_rep", jax.ShapeDtypeStruct((N_CHIPS,) + rep4.shape, rep4.dtype), 3, rep4)


ADAM_TILE = 64


def _add_halves(mine, theirs, place, name):
    rows, width = theirs.shape[1:]
    tr = min(rows, 256)

    def body(place_ref, a_ref, b_ref, o_ref):
        o_ref[...] = (a_ref[...].astype(F32) + b_ref[...].astype(F32)).astype(o_ref.dtype)

    blk = pl.BlockSpec((None, tr, width), lambda k, i, place_ref: (k, i, 0))
    return pl.pallas_call(
        body, name=name,
        grid_spec=pltpu.PrefetchScalarGridSpec(
            num_scalar_prefetch=1, grid=(N_CHIPS, rows // tr),
            in_specs=[pl.BlockSpec((None, None, tr, width), lambda k, i, place_ref: (k, place_ref[1], i, 0)), blk], out_specs=blk),
        out_shape=jax.ShapeDtypeStruct(theirs.shape, theirs.dtype), compiler_params=_cparams("parallel", "parallel"),
    )(place, mine, theirs)


def _adamw(place, parts, own, w, m, v, layer, name):
    rows, width = parts.shape[1:]

    def body(place_ref, p_ref, own_ref, w_ref, m_ref, v_ref, o_ref):
        g = jnp.zeros((ADAM_TILE, width), F32)
        for k in range(N_CHIPS):
            g = g + jnp.where(place_ref[0] == k, own_ref[...], p_ref[k]).astype(F32)
        m2 = ADAM_B1 * m_ref[...] + (1.0 - ADAM_B1) * g
        v2 = ADAM_B2 * v_ref[...] + (1.0 - ADAM_B2) * jnp.square(g)
        m_hat = m2 / (1.0 - ADAM_B1 ** ADAM_STEP)
        v_hat = v2 / (1.0 - ADAM_B2 ** ADAM_STEP)
        o_ref[0] = g
        o_ref[1] = -ADAM_LR * (m_hat / (jnp.sqrt(v_hat) + ADAM_EPS) + ADAM_WD * w_ref[...])
        o_ref[2] = m2
        o_ref[3] = v2

    blk = pl.BlockSpec((None, None, ADAM_TILE, width), lambda i, place_ref: (layer, place_ref[1], i, 0))
    blk4 = pl.BlockSpec((4, ADAM_TILE, width), lambda i, place_ref: (0, i, 0))
    mine = pl.BlockSpec((None, ADAM_TILE, width), lambda i, place_ref: (place_ref[0], i, 0))
    out = pl.BlockSpec((4, None, ADAM_TILE, width), lambda i, place_ref: (0, place_ref[1], i, 0))
    return pl.pallas_call(
        body, name=name,
        grid_spec=pltpu.PrefetchScalarGridSpec(num_scalar_prefetch=1, grid=(rows // ADAM_TILE,),
                                               in_specs=[blk4, mine, blk, blk, blk], out_specs=out),
        out_shape=jax.ShapeDtypeStruct((4, 2, rows, width), F32), compiler_params=_cparams("parallel"),
    )(place, parts, own, w, m, v)


def kernel(x, norm_gain, a_w_in, a_v_gain, a_w_s, a_b_s, a_w_out, b_w_in, b_q_gain, b_k_gain, b_w_out, c_w_in, c_w_grp, c_scale, c_w_out, loss_target, m_norm_gain, m_a_w_in, m_a_v_gain, m_a_w_s, m_a_b_s, m_a_w_out, m_b_w_in, m_b_q_gain, m_b_k_gain, m_b_w_out, m_c_w_in, m_c_w_grp, m_c_scale, m_c_w_out, v_norm_gain, v_a_w_in, v_a_v_gain, v_a_w_s, v_a_b_s, v_a_w_out, v_b_w_in, v_b_q_gain, v_b_k_gain, v_b_w_out, v_c_w_in, v_c_w_grp, v_c_scale, v_c_w_out):
    wts = dict(norm_gain=norm_gain, a_w_in=a_w_in, a_v_gain=a_v_gain, a_w_s=a_w_s, a_b_s=a_b_s, a_w_out=a_w_out, b_w_in=b_w_in,
               b_q_gain=b_q_gain, b_k_gain=b_k_gain, b_w_out=b_w_out, c_w_in=c_w_in, c_w_grp=c_w_grp, c_scale=c_scale, c_w_out=c_w_out)
    mom1 = dict(norm_gain=m_norm_gain, a_w_in=m_a_w_in, a_v_gain=m_a_v_gain, a_w_s=m_a_w_s, a_b_s=m_a_b_s, a_w_out=m_a_w_out,
                b_w_in=m_b_w_in, b_q_gain=m_b_q_gain, b_k_gain=m_b_k_gain, b_w_out=m_b_w_out, c_w_in=m_c_w_in, c_w_grp=m_c_w_grp,
                c_scale=m_c_scale, c_w_out=m_c_w_out)
    mom2 = dict(norm_gain=v_norm_gain, a_w_in=v_a_w_in, a_v_gain=v_a_v_gain, a_w_s=v_a_w_s, a_b_s=v_a_b_s, a_w_out=v_a_w_out,
                b_w_in=v_b_w_in, b_q_gain=v_b_q_gain, b_k_gain=v_b_k_gain, b_w_out=v_b_w_out, c_w_in=v_c_w_in, c_w_grp=v_c_w_grp,
                c_scale=v_c_scale, c_w_out=v_c_w_out)
    axes = ("x", "y", "c")
    me = 2 * lax.axis_index("x") + lax.axis_index("y")
    core = lax.axis_index("c")

    place = jnp.stack([me, core]).astype(jnp.int32)

    full = _gathered_weights(_allgather_units(_gather_inputs(wts, me)))
    for n, _ in REP:
        full[n] = wts[n]
    sq, grad_x, grads = _local_step(x[0], loss_target[0], full)
    loss = lax.psum(0.5 * jnp.sum(sq) / D_MODEL, axes)

    rep_g = _pack_rep(grads)
    g_units = []
    for n, layer, r, w in REDUCE_UNITS:
        g = grads[n][layer] if n in ("a_w_in", "a_w_out") else grads[n]
        g_units.append(g.astype(CDT).reshape(N_CHIPS, 2, r, w))
    g_units.append(jnp.stack([
        _small_unit([lax.slice_in_dim(grads[v], 512 * k, 512 * (k + 1), axis=1) for v in SMALL_OF_HALF],
                    rep_g[k * REP_PART:(k + 1) * REP_PART], CDT) for k in range(N_CHIPS)]))
    from_sibling = _swap_halves(g_units)
    chip_sums = [_add_halves(g, t, place, f"grad_add_halves{u}") for u, (g, t) in enumerate(zip(g_units, from_sibling))]
    parts = _scatter_shards(chip_sums)

    def state_units(tree):
        units = [tree[n].reshape(BLOCK_SHAPES[n][0], 2, r, w) for n, _, r, w in REDUCE_UNITS]
        rep_part = lax.dynamic_slice_in_dim(_pack_rep(tree), me * REP_PART, REP_PART, axis=0)
        return units + [_small_unit([tree[v] for v in SMALL_OF_HALF], rep_part, F32)[None]]

    layers = [layer for _, layer, _, _ in REDUCE_UNITS] + [0]
    res = _share_halves([_adamw(place, p, s, w_, m_, v_, layer, f"adamw{u}") for u, (p, s, w_, m_, v_, layer) in enumerate(
        zip(parts, chip_sums, state_units(wts), state_units(mom1), state_units(mom2), layers))])
    small = res[-1]
    rep_mine = jnp.concatenate([small[:, 0, SMALL_PAD:], small[:, 1, SMALL_PAD:]], axis=1)
    rep_all = lax.dynamic_update_slice_in_dim(_allgather_rep(rep_mine), rep_mine[None], me, axis=0)

    outs = []
    for q in range(4):
        tree = {}
        for (n, layer, r, w), u in zip(REDUCE_UNITS, res):
            tree.setdefault(n, []).append(u[q].reshape(BLOCK_SHAPES[n][1:]))
        tree = {n: jnp.stack(v) for n, v in tree.items()}
        tree["a_v_gain"] = small[q, 0, 0].reshape(2, 512)
        tree["c_scale"] = small[q, 1, 0, :512].reshape(1, 512)
        tree.update(_unpack_rep(rep_all[:, q].reshape(N_CHIPS * REP_PART, ROW)))
        outs.append(tree)
    return (loss, grad_x[None], *[t[n] for t in outs for n in WEIGHTS])
```

```python
import functools
import math

import jax
import jax.numpy as jnp
from jax import lax
from jax.experimental import pallas as pl
from jax.experimental.pallas import tpu as pltpu

F32 = jnp.float32
CDT = jnp.bfloat16

D_MODEL = 1024
EPS = 1e-6
CHUNK = 128
A_WIDTH = 2048
A_GROUPS = 8
A_GROUP_DIM = 256
HEAD_DIM = 128
B_HEADS = 8
B_DILATIONS = (1, 4, 16)
B_QK = 6144
B_IN = 10240
ROPE_HALF = 16
ROPE_THETA = 500000.0
POOL_SIZES = (2, 4, 8, 16)
POOL_HALO = 16
C_WIDTH = 2048
C_GROUP = 512
N_CHIPS = 4

ADAM_LR = 0.001
ADAM_B1 = 0.9
ADAM_B2 = 0.999
ADAM_EPS = 1e-08
ADAM_WD = 0.01
ADAM_STEP = 10

VMEM_LIMIT = 48 * 1024 * 1024
ANY = pl.BlockSpec(memory_space=pl.ANY)
MESH = pl.DeviceIdType.MESH

NN = (((1,), (0,)), ((), ()))
NT = (((1,), (1,)), ((), ()))
TN = (((0,), (0,)), ((), ()))


def _cparams(*sem):
    return pltpu.CompilerParams(dimension_semantics=sem, vmem_limit_bytes=VMEM_LIMIT)


def _dot(a, b, dims=NN):
    return lax.dot_general(a, b, dims, preferred_element_type=F32)


def _sigmoid(z):
    return 1.0 / (1.0 + jnp.exp(-z))


def _lane_sums(v):
    ones = jnp.ones((HEAD_DIM, HEAD_DIM), jnp.bfloat16)
    hi = v.astype(jnp.bfloat16)
    lo = (v - hi.astype(F32)).astype(jnp.bfloat16)
    return _dot(hi, ones) + _dot(lo, ones)


def _mm(a, b, mode, out_dtype, name, mnk, tiles, b_spec=None, o_spec=None, o_shape=None, add=None):
    m, n, k = mnk
    tm, tn, tk = min(tiles[0], m), min(tiles[1], n), min(tiles[2], k)
    nk = k // tk
    a_spec = {"nn": pl.BlockSpec((tm, tk), lambda i, j, q: (i, q)),
              "nt": pl.BlockSpec((tm, tk), lambda i, j, q: (i, q)),
              "tn": pl.BlockSpec((tk, tm), lambda i, j, q: (q, i))}[mode]
    if b_spec is None:
        b_spec = {"nn": pl.BlockSpec((tk, tn), lambda i, j, q: (q, j)),
                  "nt": pl.BlockSpec((tn, tk), lambda i, j, q: (j, q)),
                  "tn": pl.BlockSpec((tk, tn), lambda i, j, q: (q, j))}[mode]
    if o_spec is None:
        o_spec, o_shape = pl.BlockSpec((tm, tn), lambda i, j, q: (i, j)), (m, n)
    dims = {"nn": NN, "nt": NT, "tn": TN}[mode]
    has_add = add is not None

    def body(*refs):
        a_ref, b_ref = refs[0], refs[1]
        o_ref = refs[3] if has_add else refs[2]
        p = _dot(a_ref[...], b_ref[...], dims)

        def finish(v):
            if has_add:
                v = v + refs[2][...]
            o_ref[...] = v.astype(o_ref.dtype)

        if nk == 1:
            finish(p)
        else:
            acc_ref = refs[-1]
            q = pl.program_id(2)

            @pl.when(q == 0)
            def _():
                acc_ref[...] = p

            @pl.when(q > 0)
            def _():
                acc_ref[...] += p

            @pl.when(q == nk - 1)
            def _():
                finish(acc_ref[...])

    in_specs = [a_spec, b_spec]
    args = [a, b]
    if has_add:
        in_specs.append(pl.BlockSpec((tm, tn), lambda i, j, q: (i, j)))
        args.append(add)
    return pl.pallas_call(
        body, name=name, grid=(m // tm, n // tn, nk), in_specs=in_specs, out_specs=o_spec,
        out_shape=jax.ShapeDtypeStruct(o_shape, out_dtype),
        scratch_shapes=[pltpu.VMEM((tm, tn), F32)] if nk > 1 else [],
        compiler_params=_cparams("parallel", "parallel", "arbitrary"),
    )(*args)


def _rms_fwd(x, g, name, tq=512):
    s, d = x.shape

    def body(x_ref, g_ref, h_ref):
        xv = x_ref[...]
        r = lax.rsqrt(jnp.mean(xv * xv, axis=-1, keepdims=True) + EPS)
        h_ref[...] = (xv * r * g_ref[...]).astype(h_ref.dtype)

    return pl.pallas_call(
        body, name=name, grid=(s // tq,),
        in_specs=[pl.BlockSpec((tq, d), lambda i: (i, 0)), pl.BlockSpec((1, d), lambda i: (0, 0))],
        out_specs=pl.BlockSpec((tq, d), lambda i: (i, 0)),
        out_shape=jax.ShapeDtypeStruct((s, d), CDT), compiler_params=_cparams("parallel"),
    )(x, g)


def _rms_bwd(x, dh, dxo, g, name, tq=512):
    s, d = x.shape

    def body(x_ref, dh_ref, dxo_ref, g_ref, dx_ref, dxc_ref, dg_ref):
        xv = x_ref[...]
        dhv = dh_ref[...].astype(F32)
        r = lax.rsqrt(jnp.mean(xv * xv, axis=-1, keepdims=True) + EPS)
        gd = dhv * g_ref[...]
        dx = dxo_ref[...] + r * gd - xv * (r * r * r) * jnp.mean(gd * xv, axis=-1, keepdims=True)
        dx_ref[...] = dx
        dxc_ref[...] = dx.astype(dxc_ref.dtype)

        @pl.when(pl.program_id(0) == 0)
        def _():
            dg_ref[...] = jnp.zeros_like(dg_ref)

        dg_ref[...] += jnp.sum(dhv * xv * r, axis=0, keepdims=True)

    blk = pl.BlockSpec((tq, d), lambda i: (i, 0))
    vec = pl.BlockSpec((1, d), lambda i: (0, 0))
    return pl.pallas_call(
        body, name=name, grid=(s // tq,), in_specs=[blk, blk, blk, vec], out_specs=[blk, blk, vec],
        out_shape=[jax.ShapeDtypeStruct((s, d), F32), jax.ShapeDtypeStruct((s, d), CDT), jax.ShapeDtypeStruct((1, d), F32)],
        compiler_params=_cparams("arbitrary"),
    )(x, dh, dxo, g)


def _loss_bwd(y, target, name, tq=512):
    s, d = y.shape

    def body(y_ref, t_ref, dx_ref, dxc_ref, sq_ref):
        err = y_ref[...] - t_ref[...]
        dx = err * (1.0 / d)
        dx_ref[...] = dx
        dxc_ref[...] = dx.astype(dxc_ref.dtype)

        @pl.when(pl.program_id(0) == 0)
        def _():
            sq_ref[...] = jnp.zeros_like(sq_ref)

        sq_ref[...] += jnp.sum(err * err, axis=0, keepdims=True)

    blk = pl.BlockSpec((tq, d), lambda i: (i, 0))
    vec = pl.BlockSpec((1, d), lambda i: (0, 0))
    return pl.pallas_call(
        body, name=name, grid=(s // tq,), in_specs=[blk, blk], out_specs=[blk, blk, vec],
        out_shape=[jax.ShapeDtypeStruct((s, d), F32), jax.ShapeDtypeStruct((s, d), CDT), jax.ShapeDtypeStruct((1, d), F32)],
        compiler_params=_cparams("arbitrary"),
    )(y, target)


def _tril_mask():
    row = lax.broadcasted_iota(jnp.int32, (CHUNK, CHUNK), 0)
    col = lax.broadcasted_iota(jnp.int32, (CHUNK, CHUNK), 1)
    return row >= col


def _a_mid_fwd(proj, v_gain, w_s, b_s_t, name, tq=256):
    s = proj.shape[0]

    def body(p_ref, vg_ref, ws_ref, bs_ref, y_ref):
        vraw = p_ref[:, A_WIDTH:2 * A_WIDTH].astype(F32)
        r = lax.rsqrt(jnp.mean(vraw * vraw, axis=-1, keepdims=True) + EPS)
        vn = (vraw * r * vg_ref[...]).astype(CDT)
        tri = _tril_mask()
        for g in range(A_GROUPS):
            w = jnp.where(tri, ws_ref[g], 0.0).astype(CDT)
            bias = bs_ref[:, g:g + 1]
            cols = slice(g * A_GROUP_DIM, (g + 1) * A_GROUP_DIM)
            zcols = slice(2 * A_WIDTH + g * A_GROUP_DIM, 2 * A_WIDTH + (g + 1) * A_GROUP_DIM)
            for c in range(tq // CHUNK):
                rows = slice(c * CHUNK, (c + 1) * CHUNK)
                mixed = _dot(w, vn[rows, cols]) + bias
                u = p_ref[rows, cols].astype(F32)
                z = p_ref[rows, zcols].astype(F32)
                y_ref[rows, cols] = (u * mixed * (z * _sigmoid(z))).astype(y_ref.dtype)

    return pl.pallas_call(
        body, name=name, grid=(s // tq,),
        in_specs=[pl.BlockSpec((tq, 3 * A_WIDTH), lambda i: (i, 0)), pl.BlockSpec((1, A_WIDTH), lambda i: (0, 0)),
                  pl.BlockSpec((A_GROUPS, CHUNK, CHUNK), lambda i: (0, 0, 0)), pl.BlockSpec((CHUNK, A_GROUPS), lambda i: (0, 0))],
        out_specs=pl.BlockSpec((tq, A_WIDTH), lambda i: (i, 0)),
        out_shape=jax.ShapeDtypeStruct((s, A_WIDTH), CDT), compiler_params=_cparams("parallel"),
    )(proj, v_gain, w_s, b_s_t)


def _a_mid_bwd(proj, dy, v_gain, w_s, b_s_t, name, tq=256):
    s = proj.shape[0]

    def body(p_ref, dy_ref, vg_ref, ws_ref, bs_ref, dp_ref, dws_ref, dbs_ref, dvg_ref, dvn_ref):
        @pl.when(pl.program_id(0) == 0)
        def _():
            dws_ref[...] = jnp.zeros_like(dws_ref)
            dbs_ref[...] = jnp.zeros_like(dbs_ref)
            dvg_ref[...] = jnp.zeros_like(dvg_ref)

        vraw = p_ref[:, A_WIDTH:2 * A_WIDTH].astype(F32)
        r = lax.rsqrt(jnp.mean(vraw * vraw, axis=-1, keepdims=True) + EPS)
        vhat = vraw * r
        vg = vg_ref[...]
        vn = (vhat * vg).astype(CDT)
        tri = _tril_mask()
        lane = lax.broadcasted_iota(jnp.int32, (CHUNK, A_GROUPS), 1)
        dbs = jnp.zeros((CHUNK, A_GROUPS), F32)
        for g in range(A_GROUPS):
            w = jnp.where(tri, ws_ref[g], 0.0).astype(CDT)
            bias = bs_ref[:, g:g + 1]
            cols = slice(g * A_GROUP_DIM, (g + 1) * A_GROUP_DIM)
            zcols = slice(2 * A_WIDTH + g * A_GROUP_DIM, 2 * A_WIDTH + (g + 1) * A_GROUP_DIM)
            dws = jnp.zeros((CHUNK, CHUNK), F32)
            for c in range(tq // CHUNK):
                rows = slice(c * CHUNK, (c + 1) * CHUNK)
                vn_g = vn[rows, cols]
                mixed = _dot(w, vn_g) + bias
                u = p_ref[rows, cols].astype(F32)
                z = p_ref[rows, zcols].astype(F32)
                dyv = dy_ref[rows, cols].astype(F32)
                sg = _sigmoid(z)
                sz = z * sg
                dyu = dyv * u
                dmixed = dyu * sz
                dp_ref[rows, cols] = (dyv * mixed * sz).astype(dp_ref.dtype)
                dp_ref[rows, zcols] = (dyu * mixed * (sg * (1.0 + z * (1.0 - sg)))).astype(dp_ref.dtype)
                dmc = dmixed.astype(CDT)
                dws = dws + _dot(dmc, vn_g, NT)
                dbs = dbs + jnp.where(lane == g, jnp.sum(dmixed, axis=-1, keepdims=True), 0.0)
                dvn_ref[rows, cols] = _dot(w, dmc, TN)
            dws_ref[g] += jnp.where(tri, dws, 0.0)
        dbs_ref[...] += dbs
        dvn = dvn_ref[...]
        gd = dvn * vg
        dvraw = r * gd - vraw * (r * r * r) * jnp.mean(gd * vraw, axis=-1, keepdims=True)
        dp_ref[:, A_WIDTH:2 * A_WIDTH] = dvraw.astype(dp_ref.dtype)
        dvg_ref[...] += jnp.sum(dvn * vhat, axis=0, keepdims=True)

    return pl.pallas_call(
        body, name=name, grid=(s // tq,),
        in_specs=[pl.BlockSpec((tq, 3 * A_WIDTH), lambda i: (i, 0)), pl.BlockSpec((tq, A_WIDTH), lambda i: (i, 0)),
                  pl.BlockSpec((1, A_WIDTH), lambda i: (0, 0)), pl.BlockSpec((A_GROUPS, CHUNK, CHUNK), lambda i: (0, 0, 0)),
                  pl.BlockSpec((CHUNK, A_GROUPS), lambda i: (0, 0))],
        out_specs=[pl.BlockSpec((tq, 3 * A_WIDTH), lambda i: (i, 0)), pl.BlockSpec((A_GROUPS, CHUNK, CHUNK), lambda i: (0, 0, 0)),
                   pl.BlockSpec((CHUNK, A_GROUPS), lambda i: (0, 0)), pl.BlockSpec((1, A_WIDTH), lambda i: (0, 0))],
        out_shape=[jax.ShapeDtypeStruct((s, 3 * A_WIDTH), CDT), jax.ShapeDtypeStruct((A_GROUPS, CHUNK, CHUNK), F32),
                   jax.ShapeDtypeStruct((CHUNK, A_GROUPS), F32), jax.ShapeDtypeStruct((1, A_WIDTH), F32)],
        scratch_shapes=[pltpu.VMEM((tq, A_WIDTH), F32)],
        compiler_params=_cparams("arbitrary"),
    )(proj, dy, v_gain, w_s, b_s_t)


def _rope_tables(s):
    inv_freq = jnp.power(jnp.float32(ROPE_THETA), -jnp.arange(ROPE_HALF, dtype=F32) / ROPE_HALF)
    ang = jnp.arange(s, dtype=F32)[:, None] * inv_freq[None, :]
    cos, sin = jnp.cos(ang), jnp.sin(ang)
    rest = HEAD_DIM - 2 * ROPE_HALF
    t_c = jnp.concatenate([cos, cos, jnp.ones((s, rest), F32)], axis=1)
    t_a = jnp.concatenate([-sin, jnp.zeros((s, HEAD_DIM - ROPE_HALF), F32)], axis=1)
    t_b = jnp.concatenate([jnp.zeros((s, ROPE_HALF), F32), sin, jnp.zeros((s, rest), F32)], axis=1)
    return t_c, t_a, t_b


def _b_qk_fwd(proj, gains, tabs, name, tq=256):
    s = proj.shape[0]

    def body(p_ref, g_ref, tc_ref, ta_ref, tb_ref, o_ref):
        tc, ta, tb = tc_ref[...], ta_ref[...], tb_ref[...]
        for tg in range(6):
            gain = g_ref[tg:tg + 1, :]
            for h in range(B_HEADS):
                cols = slice(tg * 1024 + h * HEAD_DIM, tg * 1024 + (h + 1) * HEAD_DIM)
                xv = p_ref[:, cols].astype(F32)
                r = lax.rsqrt(_lane_sums(xv * xv) * (1.0 / HEAD_DIM) + EPS)
                xn = xv * r * gain
                y = xn * tc + pltpu.roll(xn, HEAD_DIM - ROPE_HALF, 1) * ta + pltpu.roll(xn, ROPE_HALF, 1) * tb
                o_ref[:, cols] = y.astype(o_ref.dtype)

    tab = pl.BlockSpec((tq, HEAD_DIM), lambda i: (i, 0))
    return pl.pallas_call(
        body, name=name, grid=(s // tq,),
        in_specs=[pl.BlockSpec((tq, B_QK), lambda i: (i, 0)), pl.BlockSpec((6, HEAD_DIM), lambda i: (0, 0)), tab, tab, tab],
        out_specs=pl.BlockSpec((tq, B_QK), lambda i: (i, 0)),
        out_shape=jax.ShapeDtypeStruct((s, B_QK), CDT), compiler_params=_cparams("parallel"),
    )(proj, gains, *tabs)


PERMUTE_BLOCK_BYTES = 4 * 1024 * 1024


def _view_rows(length, dil, width, itemsize):
    rows = 16
    while 2 * rows * dil * width * itemsize <= PERMUTE_BLOCK_BYTES and 2 * rows <= length:
        rows *= 2
    return rows


def _to_view(x, col, width, dil, name):
    s = x.shape[0]
    length = s // dil
    tl = _view_rows(length, dil, width, 4)
    lanes = HEAD_DIM
    nblk = width // lanes

    def body(x_ref, o_ref, slab_ref):
        for b in range(nblk):
            slab_ref[b] = x_ref[:, b * lanes:(b + 1) * lanes].astype(F32)
        for r in range(dil):
            for b in range(nblk):
                o_ref[:, r * width + b * lanes:r * width + (b + 1) * lanes] = (
                    slab_ref.at[b][pl.ds(r, tl, stride=dil), :].astype(o_ref.dtype))

    return pl.pallas_call(
        body, name=name, grid=(length // tl,),
        in_specs=[pl.BlockSpec((tl * dil, width), lambda i: (i, col))],
        out_specs=pl.BlockSpec((tl, dil * width), lambda i: (i, 0)),
        out_shape=jax.ShapeDtypeStruct((length, dil * width), x.dtype),
        scratch_shapes=[pltpu.VMEM((nblk, tl * dil, lanes), F32)],
        compiler_params=_cparams("parallel"),
    )(x)


def _from_view(v, dil, name):
    length, width = v.shape[0], v.shape[1] // dil
    tl = _view_rows(length, dil, width, 4)
    lanes = HEAD_DIM
    nblk = width // lanes

    def body(v_ref, o_ref, slab_ref):
        for r in range(dil):
            for b in range(nblk):
                slab_ref.at[b][pl.ds(r, tl, stride=dil), :] = v_ref[:, r * width + b * lanes:r * width + (b + 1) * lanes].astype(F32)
        for b in range(nblk):
            o_ref[:, b * lanes:(b + 1) * lanes] = slab_ref[b].astype(o_ref.dtype)

    return pl.pallas_call(
        body, name=name, grid=(length // tl,),
        in_specs=[pl.BlockSpec((tl, dil * width), lambda i: (i, 0))],
        out_specs=pl.BlockSpec((tl * dil, width), lambda i: (i, 0)),
        out_shape=jax.ShapeDtypeStruct((length * dil, width), v.dtype),
        scratch_shapes=[pltpu.VMEM((nblk, tl * dil, lanes), F32)],
        compiler_params=_cparams("parallel"),
    )(v)


def _b_attn_fwd(q, k, v, bases, dil, name):
    length = q.shape[0]
    nb = length // CHUNK
    scale = 1.0 / math.sqrt(HEAD_DIM)
    w = B_HEADS * HEAD_DIM
    qb, kb, vb = bases

    def body(q_ref, kc_ref, kp_ref, vc_ref, vp_ref, o_ref, lse_ref):
        n = pl.program_id(1)
        qi = lax.broadcasted_iota(jnp.int32, (CHUNK, 2 * CHUNK), 0)
        ki = lax.broadcasted_iota(jnp.int32, (CHUNK, 2 * CHUNK), 1)
        first_key = jnp.where(n > 0, 0, CHUNK)
        mask = (ki >= qi) & (ki <= qi + CHUNK) & (ki >= first_key)
        lane = lax.broadcasted_iota(jnp.int32, (CHUNK, HEAD_DIM), 1)
        lse_all = jnp.zeros((CHUNK, HEAD_DIM), F32)
        for h in range(B_HEADS):
            sl = slice(h * HEAD_DIM, (h + 1) * HEAD_DIM)
            k2 = jnp.concatenate([kp_ref[:, sl], kc_ref[:, sl]], axis=0)
            v2 = jnp.concatenate([vp_ref[:, sl], vc_ref[:, sl]], axis=0)
            sc = jnp.where(mask, _dot(q_ref[:, sl], k2, NT) * scale, -1e30)
            m = jnp.max(sc, axis=-1, keepdims=True)
            p = jnp.exp(sc - m)
            l = jnp.sum(p, axis=-1, keepdims=True)
            o_ref[:, sl] = _dot(p.astype(CDT), v2) / l
            lse_all = jnp.where(lane == h, m + jnp.log(l), lse_all)
        lse_ref[...] = lse_all

    prev = lambda n: jnp.maximum(n - 1, 0)
    blk = lambda f: pl.BlockSpec((CHUNK, w), f)
    return pl.pallas_call(
        body, name=name, grid=(dil, nb),
        in_specs=[blk(lambda r, n: (n, qb + r)), blk(lambda r, n: (n, kb + r)), blk(lambda r, n: (prev(n), kb + r)),
                  blk(lambda r, n: (n, vb + r)), blk(lambda r, n: (prev(n), vb + r))],
        out_specs=[blk(lambda r, n: (n, r)), pl.BlockSpec((CHUNK, HEAD_DIM), lambda r, n: (n, r))],
        out_shape=[jax.ShapeDtypeStruct((length, dil * w), F32), jax.ShapeDtypeStruct((length, dil * HEAD_DIM), F32)],
        compiler_params=_cparams("parallel", "parallel"),
    )(q, k, k, v, v)


def _b_combine(os_, lses, proj, name, tq=512):
    s = proj.shape[0]
    w = B_HEADS * HEAD_DIM

    def body(o0_ref, o1_ref, o2_ref, l0_ref, l1_ref, l2_ref, z_ref, y_ref, oj_ref, lj_ref):
        l0, l1, l2 = l0_ref[...], l1_ref[...], l2_ref[...]
        m = jnp.maximum(jnp.maximum(l0, l1), l2)
        lj = m + jnp.log(jnp.exp(l0 - m) + jnp.exp(l1 - m) + jnp.exp(l2 - m))
        lj_ref[...] = lj
        w0, w1, w2 = jnp.exp(l0 - lj), jnp.exp(l1 - lj), jnp.exp(l2 - lj)
        for h in range(B_HEADS):
            sl = slice(h * HEAD_DIM, (h + 1) * HEAD_DIM)
            o = w0[:, h:h + 1] * o0_ref[:, sl] + w1[:, h:h + 1] * o1_ref[:, sl] + w2[:, h:h + 1] * o2_ref[:, sl]
            z = z_ref[:, sl].astype(F32)
            oj_ref[:, sl] = o.astype(oj_ref.dtype)
            y_ref[:, sl] = (o * (z * _sigmoid(z))).astype(y_ref.dtype)

    blk = pl.BlockSpec((tq, w), lambda i: (i, 0))
    st = pl.BlockSpec((tq, HEAD_DIM), lambda i: (i, 0))
    return pl.pallas_call(
        body, name=name, grid=(s // tq,),
        in_specs=[blk, blk, blk, st, st, st, pl.BlockSpec((tq, w), lambda i: (i, 9))],
        out_specs=[blk, blk, st],
        out_shape=[jax.ShapeDtypeStruct((s, w), CDT), jax.ShapeDtypeStruct((s, w), CDT), jax.ShapeDtypeStruct((s, HEAD_DIM), F32)],
        compiler_params=_cparams("parallel"),
    )(*os_, *lses, proj)


def _b_bwd_prep(dy, oj, proj, name, tq=512):
    s = proj.shape[0]
    w = B_HEADS * HEAD_DIM

    def body(dy_ref, oj_ref, z_ref, do_ref, dz_ref, dd_ref):
        lane = lax.broadcasted_iota(jnp.int32, (tq, HEAD_DIM), 1)
        dd = jnp.zeros((tq, HEAD_DIM), F32)
        for h in range(B_HEADS):
            sl = slice(h * HEAD_DIM, (h + 1) * HEAD_DIM)
            z = z_ref[:, sl].astype(F32)
            dyv = dy_ref[:, sl].astype(F32)
            o = oj_ref[:, sl].astype(F32)
            sg = _sigmoid(z)
            do = dyv * (z * sg)
            do_ref[:, sl] = do.astype(do_ref.dtype)
            dz_ref[:, sl] = (dyv * o * (sg * (1.0 + z * (1.0 - sg)))).astype(dz_ref.dtype)
            dd = jnp.where(lane == h, jnp.sum(do * o, axis=-1, keepdims=True), dd)
        dd_ref[...] = dd

    blk = pl.BlockSpec((tq, w), lambda i: (i, 0))
    st = pl.BlockSpec((tq, HEAD_DIM), lambda i: (i, 0))
    return pl.pallas_call(
        body, name=name, grid=(s // tq,),
        in_specs=[blk, blk, pl.BlockSpec((tq, w), lambda i: (i, 9))], out_specs=[blk, blk, st],
        out_shape=[jax.ShapeDtypeStruct((s, w), CDT), jax.ShapeDtypeStruct((s, w), CDT), jax.ShapeDtypeStruct((s, HEAD_DIM), F32)],
        compiler_params=_cparams("parallel"),
    )(dy, oj, proj)


def _b_attn_bwd(q, k, v, bases, do, lj, dd, dil, name):
    length = q.shape[0]
    nb = length // CHUNK
    scale = 1.0 / math.sqrt(HEAD_DIM)
    w = B_HEADS * HEAD_DIM
    qb, kb, vb = bases

    def body(qj_ref, qn_ref, k_ref, v_ref, doj_ref, don_ref, lj_ref, ln_ref, dj_ref, dn_ref, out_ref, carry_ref):
        j = pl.program_id(1)

        @pl.when(j == 0)
        def _():
            carry_ref[...] = jnp.zeros_like(carry_ref)

        qi = lax.broadcasted_iota(jnp.int32, (2 * CHUNK, CHUNK), 0)
        ki = lax.broadcasted_iota(jnp.int32, (2 * CHUNK, CHUNK), 1)
        no_next = jnp.where(j + 1 < nb, 0, 2 * CHUNK)
        mask = ((qi < CHUNK) & (ki <= qi)) | ((qi >= CHUNK) & (ki >= qi - CHUNK + no_next))
        for h in range(B_HEADS):
            sl = slice(h * HEAD_DIM, (h + 1) * HEAD_DIM)
            q2 = jnp.concatenate([qj_ref[:, sl], qn_ref[:, sl]], axis=0)
            do2 = jnp.concatenate([doj_ref[:, sl], don_ref[:, sl]], axis=0)
            lse2 = jnp.concatenate([lj_ref[:, h:h + 1], ln_ref[:, h:h + 1]], axis=0)
            d2 = jnp.concatenate([dj_ref[:, h:h + 1], dn_ref[:, h:h + 1]], axis=0)
            k = k_ref[:, sl]
            v = v_ref[:, sl]
            sc = _dot(q2, k, NT) * scale
            p = jnp.where(mask, jnp.exp(sc - lse2), 0.0)
            dp = _dot(do2, v, NT)
            ds = (p * (dp - d2) * scale).astype(CDT)
            dq2 = _dot(ds, k)
            out_ref[:, sl] = (carry_ref[:, sl] + dq2[:CHUNK]).astype(out_ref.dtype)
            carry_ref[:, sl] = dq2[CHUNK:]
            out_ref[:, w + h * HEAD_DIM:w + (h + 1) * HEAD_DIM] = _dot(ds, q2, TN).astype(out_ref.dtype)
            out_ref[:, 2 * w + h * HEAD_DIM:2 * w + (h + 1) * HEAD_DIM] = _dot(p.astype(CDT), do2, TN).astype(out_ref.dtype)

    nxt = lambda j: jnp.minimum(j + 1, nb - 1)
    blk = lambda f: pl.BlockSpec((CHUNK, w), f)
    st = lambda f: pl.BlockSpec((CHUNK, HEAD_DIM), f)
    return pl.pallas_call(
        body, name=name, grid=(dil, nb),
        in_specs=[blk(lambda r, j: (j, qb + r)), blk(lambda r, j: (nxt(j), qb + r)),
                  blk(lambda r, j: (j, kb + r)), blk(lambda r, j: (j, vb + r)),
                  blk(lambda r, j: (j, r)), blk(lambda r, j: (nxt(j), r)),
                  st(lambda r, j: (j, r)), st(lambda r, j: (nxt(j), r)), st(lambda r, j: (j, r)), st(lambda r, j: (nxt(j), r))],
        out_specs=pl.BlockSpec((CHUNK, 3 * w), lambda r, j: (j, r)),
        out_shape=jax.ShapeDtypeStruct((length, dil * 3 * w), CDT),
        scratch_shapes=[pltpu.VMEM((CHUNK, w), F32)],
        compiler_params=_cparams("parallel", "arbitrary"),
    )(q, q, k, v, do, do, lj, lj, dd, dd)


def _b_qk_bwd(proj, dqkv, dz, gains, tabs, name, tq=256):
    s = proj.shape[0]
    w = B_HEADS * HEAD_DIM

    def body(p_ref, d0_ref, d1_ref, d2_ref, dz_ref, g_ref, tc_ref, ta_ref, tb_ref, dp_ref, dg_ref):
        @pl.when(pl.program_id(0) == 0)
        def _():
            dg_ref[...] = jnp.zeros_like(dg_ref)

        tc, ta, tb = tc_ref[...], ta_ref[...], tb_ref[...]
        d_refs = (d0_ref, d1_ref, d2_ref)
        for g in range(3):
            for t in range(2):
                tg = t * 3 + g
                gain = g_ref[tg:tg + 1, :]
                dgain = jnp.zeros((1, HEAD_DIM), F32)
                for h in range(B_HEADS):
                    cols = slice(tg * w + h * HEAD_DIM, tg * w + (h + 1) * HEAD_DIM)
                    xv = p_ref[:, cols].astype(F32)
                    r = lax.rsqrt(_lane_sums(xv * xv) * (1.0 / HEAD_DIM) + EPS)
                    dyv = d_refs[g][:, t * w + h * HEAD_DIM:t * w + (h + 1) * HEAD_DIM].astype(F32)
                    dxn = dyv * tc + pltpu.roll(dyv * ta, ROPE_HALF, 1) + pltpu.roll(dyv * tb, HEAD_DIM - ROPE_HALF, 1)
                    gd = dxn * gain
                    dx = r * gd - xv * (r * r * r) * (_lane_sums(gd * xv) * (1.0 / HEAD_DIM))
                    dp_ref[:, cols] = dx.astype(dp_ref.dtype)
                    dgain = dgain + jnp.sum(dxn * xv * r, axis=0, keepdims=True)
                dg_ref[tg:tg + 1, :] += dgain
            dp_ref[:, (6 + g) * w:(7 + g) * w] = d_refs[g][:, 2 * w:3 * w]
        dp_ref[:, 9 * w:10 * w] = dz_ref[...]

    tab = pl.BlockSpec((tq, HEAD_DIM), lambda i: (i, 0))
    dblk = pl.BlockSpec((tq, 3 * w), lambda i: (i, 0))
    return pl.pallas_call(
        body, name=name, grid=(s // tq,),
        in_specs=[pl.BlockSpec((tq, B_QK), lambda i: (i, 0)), dblk, dblk, dblk, pl.BlockSpec((tq, w), lambda i: (i, 0)),
                  pl.BlockSpec((6, HEAD_DIM), lambda i: (0, 0)), tab, tab, tab],
        out_specs=[pl.BlockSpec((tq, B_IN), lambda i: (i, 0)), pl.BlockSpec((6, HEAD_DIM), lambda i: (0, 0))],
        out_shape=[jax.ShapeDtypeStruct((s, B_IN), CDT), jax.ShapeDtypeStruct((6, HEAD_DIM), F32)],
        compiler_params=_cparams("arbitrary"),
    )(proj, *dqkv, dz, gains, *tabs)


def _inv_count(t, window):
    return 1.0 / jnp.minimum(t + 1, window).astype(F32)


POOL_BLOCK = 128


def _band(window, forward):
    r = lax.broadcasted_iota(jnp.int32, (POOL_BLOCK, 2 * POOL_BLOCK), 0)
    j = lax.broadcasted_iota(jnp.int32, (POOL_BLOCK, 2 * POOL_BLOCK), 1)
    if forward:
        return jnp.where((j >= r) & (j < r + window), 1.0, 0.0).astype(CDT)
    return jnp.where((j <= r + POOL_BLOCK) & (j > r + POOL_BLOCK - window), 1.0, 0.0).astype(CDT)


def _split_dot(band, v):
    hi = v.astype(jnp.bfloat16)
    lo = (v - hi.astype(F32)).astype(jnp.bfloat16)
    band = band.astype(jnp.bfloat16)
    return _dot(band, hi) + _dot(band, lo)


def _pool_diff(x_ref, halo_ref, diff_ref, i, tq):
    t = i * tq + lax.broadcasted_iota(jnp.int32, (tq, 1), 0)
    for g, window in enumerate(POOL_SIZES):
        cols = slice(g * C_GROUP, (g + 1) * C_GROUP)
        band = _band(window, forward=False)
        inv = _inv_count(t, window)
        for b in range(tq // POOL_BLOCK):
            rows = slice(b * POOL_BLOCK, (b + 1) * POOL_BLOCK)
            cur = x_ref[rows, cols]
            if b == 0:
                above = jnp.where(i > 0, halo_ref[:, cols], jnp.zeros_like(cur))
            else:
                above = x_ref[(b - 1) * POOL_BLOCK:b * POOL_BLOCK, cols]
            pooled = _dot(band, jnp.concatenate([above, cur], axis=0)) * inv[rows]
            diff_ref[rows, cols] = (pooled - cur.astype(F32)).astype(diff_ref.dtype)


GRP_SHARD = (N_CHIPS, 2, 256, C_GROUP)
GRP_ROWS = C_GROUP // N_CHIPS


def _grp_rows(g):
    return g // 2, slice((g % 2) * GRP_ROWS, (g % 2 + 1) * GRP_ROWS)


def _grp_weight(w_ref, g):
    half, rows = _grp_rows(g)
    return jnp.concatenate([w_ref[k, half, rows, :] for k in range(N_CHIPS)], axis=0)


def _c_mid_fwd(proj, w_grp, scale, name, tq=512):
    s = proj.shape[0]
    per = tq // POOL_BLOCK

    def body(x_ref, halo_ref, z_ref, w_ref, sc_ref, y_ref, diff_ref):
        _pool_diff(x_ref, halo_ref, diff_ref, pl.program_id(0), tq)
        for g in range(len(POOL_SIZES)):
            cols = slice(g * C_GROUP, (g + 1) * C_GROUP)
            z = z_ref[:, cols].astype(F32)
            y_ref[:, cols] = (_dot(diff_ref[:, cols], _grp_weight(w_ref, g)) * sc_ref[:, cols] * (z * _sigmoid(z))).astype(y_ref.dtype)

    return pl.pallas_call(
        body, name=name, grid=(s // tq,),
        in_specs=[pl.BlockSpec((tq, C_WIDTH), lambda i: (i, 0)),
                  pl.BlockSpec((POOL_BLOCK, C_WIDTH), lambda i: (jnp.maximum(i * per - 1, 0), 0)),
                  pl.BlockSpec((tq, C_WIDTH), lambda i: (i, 1)),
                  pl.BlockSpec(GRP_SHARD, lambda i: (0, 0, 0, 0)), pl.BlockSpec((1, C_WIDTH), lambda i: (0, 0))],
        out_specs=pl.BlockSpec((tq, C_WIDTH), lambda i: (i, 0)),
        out_shape=jax.ShapeDtypeStruct((s, C_WIDTH), CDT),
        scratch_shapes=[pltpu.VMEM((tq, C_WIDTH), CDT)], compiler_params=_cparams("parallel"),
    )(proj, proj, proj, w_grp, scale)


def _c_mid_bwd(proj, w_grp, scale, dy, name, tq=512):
    s = proj.shape[0]
    per = tq // POOL_BLOCK

    def body(x_ref, halo_ref, z_ref, w_ref, sc_ref, dy_ref, dd_ref, dz_ref, dw_ref, dsc_ref, diff_ref):
        @pl.when(pl.program_id(0) == 0)
        def _():
            dw_ref[...] = jnp.zeros_like(dw_ref)
            dsc_ref[...] = jnp.zeros_like(dsc_ref)

        _pool_diff(x_ref, halo_ref, diff_ref, pl.program_id(0), tq)
        for g in range(len(POOL_SIZES)):
            cols = slice(g * C_GROUP, (g + 1) * C_GROUP)
            d = diff_ref[:, cols]
            wg = _grp_weight(w_ref, g)
            half, rows = _grp_rows(g)
            m0 = _dot(d, wg)
            z = z_ref[:, cols].astype(F32)
            dyv = dy_ref[:, cols].astype(F32)
            sc = sc_ref[:, cols]
            sg = _sigmoid(z)
            dmixed = dyv * (z * sg)
            dz_ref[:, cols] = (dyv * m0 * sc * (sg * (1.0 + z * (1.0 - sg)))).astype(dz_ref.dtype)
            dsc_ref[:, cols] += jnp.sum(dmixed * m0, axis=0, keepdims=True)
            dm0 = (dmixed * sc).astype(CDT)
            dwg = _dot(d, dm0, TN)
            for k in range(N_CHIPS):
                dw_ref[k, half, rows, :] += dwg[k * GRP_ROWS:(k + 1) * GRP_ROWS]
            dd_ref[:, cols] = _dot(dm0, wg, NT)

    blk = pl.BlockSpec((tq, C_WIDTH), lambda i: (i, 0))
    wsp = pl.BlockSpec(GRP_SHARD, lambda i: (0, 0, 0, 0))
    vec = pl.BlockSpec((1, C_WIDTH), lambda i: (0, 0))
    return pl.pallas_call(
        body, name=name, grid=(s // tq,),
        in_specs=[blk, pl.BlockSpec((POOL_BLOCK, C_WIDTH), lambda i: (jnp.maximum(i * per - 1, 0), 0)),
                  pl.BlockSpec((tq, C_WIDTH), lambda i: (i, 1)), wsp, vec, blk],
        out_specs=[blk, blk, wsp, vec],
        out_shape=[jax.ShapeDtypeStruct((s, C_WIDTH), F32), jax.ShapeDtypeStruct((s, C_WIDTH), CDT),
                   jax.ShapeDtypeStruct(GRP_SHARD, F32), jax.ShapeDtypeStruct((1, C_WIDTH), F32)],
        scratch_shapes=[pltpu.VMEM((tq, C_WIDTH), CDT)], compiler_params=_cparams("arbitrary"),
    )(proj, proj, proj, w_grp, scale, dy)


def _c_pool_bwd(ddiff, dz, name, tq=512):
    s = ddiff.shape[0]
    per = tq // POOL_BLOCK
    last = s // tq - 1

    def body(d_ref, halo_ref, dz_ref, o_ref):
        i = pl.program_id(0)
        t = i * tq + lax.broadcasted_iota(jnp.int32, (tq, 1), 0)
        for g, window in enumerate(POOL_SIZES):
            cols = slice(g * C_GROUP, (g + 1) * C_GROUP)
            band = _band(window, forward=True)
            inv = _inv_count(t, window)
            for b in range(tq // POOL_BLOCK):
                rows = slice(b * POOL_BLOCK, (b + 1) * POOL_BLOCK)
                cur = d_ref[rows, cols]
                if b == tq // POOL_BLOCK - 1:
                    below = jnp.where(i < last, halo_ref[:, cols] * (1.0 / window), 0.0)
                else:
                    nxt = slice((b + 1) * POOL_BLOCK, (b + 2) * POOL_BLOCK)
                    below = d_ref[nxt, cols] * inv[nxt]
                summed = _split_dot(band, jnp.concatenate([cur * inv[rows], below], axis=0))
                o_ref[rows, cols] = (summed - cur).astype(o_ref.dtype)
        o_ref[:, C_WIDTH:] = dz_ref[...]

    return pl.pallas_call(
        body, name=name, grid=(s // tq,),
        in_specs=[pl.BlockSpec((tq, C_WIDTH), lambda i: (i, 0)),
                  pl.BlockSpec((POOL_BLOCK, C_WIDTH), lambda i: (jnp.minimum((i + 1) * per, s // POOL_BLOCK - 1), 0)),
                  pl.BlockSpec((tq, C_WIDTH), lambda i: (i, 0))],
        out_specs=pl.BlockSpec((tq, 2 * C_WIDTH), lambda i: (i, 0)),
        out_shape=jax.ShapeDtypeStruct((s, 2 * C_WIDTH), CDT),
        compiler_params=_cparams("parallel"),
    )(ddiff, ddiff, dz)


LAYERS = (("a", 0), ("b", 0), ("c", 0), ("a", 1))


def _shard_spec(block, layer, where):
    if layer is None:
        return pl.BlockSpec((None,) + block, where)
    return pl.BlockSpec((None, None) + block, lambda i, j, q: (where(i, j, q)[0], layer) + where(i, j, q)[1:])


def _w_in_fwd(h, gw, layer, name):
    s, width = h.shape[0], gw.shape[-1]
    return _mm(h, gw, "nn", CDT, name, (s, N_CHIPS * width, D_MODEL), (1024, width, D_MODEL),
               b_spec=_shard_spec((D_MODEL, width), layer, lambda i, j, q: (j, 0, 0)))


def _w_in_dh(dproj, gw, layer, name):
    s, width = dproj.shape[0], gw.shape[-1]
    return _mm(dproj, gw, "nt", F32, name, (s, D_MODEL, N_CHIPS * width), (1024, D_MODEL, width),
               b_spec=_shard_spec((D_MODEL, width), layer, lambda i, j, q: (q, 0, 0)))


def _w_in_grad(h, dproj, width, name):
    s = h.shape[0]
    tokens = 2048 if width <= 1536 else 1024
    return _mm(h, dproj, "tn", CDT, name, (D_MODEL, N_CHIPS * width, s), (D_MODEL, width, tokens),
               o_spec=pl.BlockSpec((None, D_MODEL, width), lambda i, j, q: (j, 0, 0)), o_shape=(N_CHIPS, D_MODEL, width))


def _w_out_spec(gw, layer):
    rows = gw.shape[-2]
    if layer is None:
        return pl.BlockSpec((N_CHIPS, rows, D_MODEL), lambda i: (0, 0, 0))
    return pl.BlockSpec((N_CHIPS, None, rows, D_MODEL), lambda i: (0, layer, 0, 0))


def _w_out_fwd(y, gw, layer, x, name, tm=512):
    s, k = y.shape

    def body(y_ref, w_ref, x_ref, o_ref):
        o_ref[...] = x_ref[...] + _dot(y_ref[...], w_ref[...].reshape(k, D_MODEL))

    blk = pl.BlockSpec((tm, D_MODEL), lambda i: (i, 0))
    return pl.pallas_call(
        body, name=name, grid=(s // tm,), in_specs=[pl.BlockSpec((tm, k), lambda i: (i, 0)), _w_out_spec(gw, layer), blk],
        out_specs=blk, out_shape=jax.ShapeDtypeStruct((s, D_MODEL), F32), compiler_params=_cparams("parallel"),
    )(y, gw, x)


def _w_out_dy(dxc, gw, layer, name, tm=1024):
    s, k = dxc.shape[0], N_CHIPS * gw.shape[-2]
    tm = min(tm, s)

    def body(dx_ref, w_ref, o_ref):
        o_ref[...] = _dot(dx_ref[...], w_ref[...].reshape(k, D_MODEL), NT).astype(o_ref.dtype)

    return pl.pallas_call(
        body, name=name, grid=(s // tm,), in_specs=[pl.BlockSpec((tm, D_MODEL), lambda i: (i, 0)), _w_out_spec(gw, layer)],
        out_specs=pl.BlockSpec((tm, k), lambda i: (i, 0)), out_shape=jax.ShapeDtypeStruct((s, k), CDT),
        compiler_params=_cparams("parallel"),
    )(dxc, gw)


def _w_out_grad(y, dxc, rows, name, tokens=1024):
    s, k = y.shape
    tokens = min(tokens, s)
    steps = s // tokens

    def body(y_ref, dx_ref, o_ref, acc_ref):
        i = pl.program_id(0)
        p = _dot(y_ref[...], dx_ref[...], TN)

        @pl.when(i == 0)
        def _():
            acc_ref[...] = p

        @pl.when(i > 0)
        def _():
            acc_ref[...] += p

        @pl.when(i == steps - 1)
        def _():
            o_ref[...] = acc_ref[...].reshape(N_CHIPS, rows, D_MODEL).astype(o_ref.dtype)

    return pl.pallas_call(
        body, name=name, grid=(steps,),
        in_specs=[pl.BlockSpec((tokens, k), lambda i: (i, 0)), pl.BlockSpec((tokens, D_MODEL), lambda i: (i, 0))],
        out_specs=pl.BlockSpec((N_CHIPS, rows, D_MODEL), lambda i: (0, 0, 0)),
        out_shape=jax.ShapeDtypeStruct((N_CHIPS, rows, D_MODEL), CDT),
        scratch_shapes=[pltpu.VMEM((k, D_MODEL), F32)], compiler_params=_cparams("arbitrary"),
    )(y, dxc)


def _local_step(x, target, w, on_grads=None):
    s = x.shape[0]
    tabs = _rope_tables(s)
    qk_gains = jnp.concatenate([w["b_q_gain"][0], w["b_k_gain"][0]], axis=0)
    saved = []
    for li, (kind, j) in enumerate(LAYERS):
        h = _rms_fwd(x, w["norm_gain"][li:li + 1], f"rms_fwd{li}")
        if kind == "a":
            proj = _w_in_fwd(h, w["a_w_in"], j, f"a_in{li}")
            bs_t = w["a_b_s"][j].T
            y = _a_mid_fwd(proj, w["a_v_gain"][j:j + 1], w["a_w_s"][j], bs_t, f"a_mid_fwd{li}")
            x_next = _w_out_fwd(y, w["a_w_out"], j, x, f"a_out{li}")
            saved.append((x, h, proj, y))
        elif kind == "b":
            proj = _w_in_fwd(h, w["b_w_in"], None, f"b_in{li}")
            qk = _b_qk_fwd(proj, qk_gains, tabs, f"b_qk_fwd{li}")
            qkv, os_, lses = [], [], []
            for g, dil in enumerate(B_DILATIONS):
                if dil == 1:
                    ops = (qk, qk, proj, (g, 3 + g, 6 + g))
                else:
                    ops = (_to_view(qk, g, 1024, dil, f"b_q_view{li}_{g}"), _to_view(qk, 3 + g, 1024, dil, f"b_k_view{li}_{g}"),
                           _to_view(proj, 6 + g, 1024, dil, f"b_v_view{li}_{g}"), (0, 0, 0))
                o, lse = _b_attn_fwd(*ops, dil, f"b_attn_fwd{li}_{g}")
                if dil > 1:
                    o, lse = _from_view(o, dil, f"b_o_nat{li}_{g}"), _from_view(lse, dil, f"b_lse_nat{li}_{g}")
                qkv.append(ops)
                os_.append(o)
                lses.append(lse)
            y, oj, lj = _b_combine(os_, lses, proj, f"b_combine{li}")
            x_next = _w_out_fwd(y, w["b_w_out"], None, x, f"b_out{li}")
            saved.append((x, h, proj, y, qkv, oj, lj))
        else:
            proj = _w_in_fwd(h, w["c_w_in"], None, f"c_in{li}")
            y = _c_mid_fwd(proj, w["c_w_grp"], w["c_scale"][j:j + 1], f"c_mid_fwd{li}")
            x_next = _w_out_fwd(y, w["c_w_out"], None, x, f"c_out{li}")
            saved.append((x, h, proj, y))
        x = x_next

    dx, dxc, sq = _loss_bwd(x, target, "loss_bwd")
    grads = {"norm_gain": [None] * len(LAYERS), "a_w_in": [None, None], "a_v_gain": [None, None], "a_w_s": [None, None],
             "a_b_s": [None, None], "a_w_out": [None, None]}
    for li in reversed(range(len(LAYERS))):
        kind, j = LAYERS[li]
        sv = saved[li]
        xin, h, proj, y = sv[:4]
        if kind == "a":
            grads["a_w_out"][j] = _w_out_grad(y, dxc, 512, f"a_dwout{li}")
            dy = _w_out_dy(dxc, w["a_w_out"], j, f"a_dy{li}")
            dproj, dws, dbs_t, dvg = _a_mid_bwd(proj, dy, w["a_v_gain"][j:j + 1], w["a_w_s"][j], w["a_b_s"][j].T, f"a_mid_bwd{li}")
            grads["a_w_s"][j], grads["a_b_s"][j], grads["a_v_gain"][j] = dws, dbs_t.T, dvg[0]
            grads["a_w_in"][j] = _w_in_grad(h, dproj, 1536, f"a_dwin{li}")
            dh = _w_in_dh(dproj, w["a_w_in"], j, f"a_dh{li}")
        elif kind == "b":
            qkv, oj, lj = sv[4:]
            grads["b_w_out"] = _w_out_grad(y, dxc, 256, f"b_dwout{li}")
            dy = _w_out_dy(dxc, w["b_w_out"], None, f"b_dy{li}")
            do, dz, dd = _b_bwd_prep(dy, oj, proj, f"b_bwd_prep{li}")
            dqkv = []
            for g, dil in enumerate(B_DILATIONS):
                stats = (do, lj, dd)
                if dil > 1:
                    stats = (_to_view(do, 0, 1024, dil, f"b_do_view{li}_{g}"), _to_view(lj, 0, HEAD_DIM, dil, f"b_lj_view{li}_{g}"),
                             _to_view(dd, 0, HEAD_DIM, dil, f"b_dd_view{li}_{g}"))
                d = _b_attn_bwd(*qkv[g], *stats, dil, f"b_attn_bwd{li}_{g}")
                dqkv.append(_from_view(d, dil, f"b_dqkv_nat{li}_{g}") if dil > 1 else d)
            dproj, dgains = _b_qk_bwd(proj, dqkv, dz, qk_gains, tabs, f"b_qk_bwd{li}")
            grads["b_q_gain"], grads["b_k_gain"] = dgains[None, :3], dgains[None, 3:]
            grads["b_w_in"] = _w_in_grad(h, dproj, 2560, f"b_dwin{li}")
            dh = _w_in_dh(dproj, w["b_w_in"], None, f"b_dh{li}")
        else:
            grads["c_w_out"] = _w_out_grad(y, dxc, 512, f"c_dwout{li}")
            dy = _w_out_dy(dxc, w["c_w_out"], None, f"c_dy{li}")
            ddiff, dz, dwg, dsc = _c_mid_bwd(proj, w["c_w_grp"], w["c_scale"][j:j + 1], dy, f"c_mid_bwd{li}")
            grads["c_w_grp"], grads["c_scale"] = dwg, dsc
            dproj = _c_pool_bwd(ddiff, dz, f"c_pool_bwd{li}")
            grads["c_w_in"] = _w_in_grad(h, dproj, 1024, f"c_dwin{li}")
            dh = _w_in_dh(dproj, w["c_w_in"], None, f"c_dh{li}")
        gain = w["norm_gain"][li:li + 1]
        if on_grads is not None:
            gain = gain + on_grads(li, grads)
        dx, dxc, dng = _rms_bwd(xin, dh, dx, gain, f"rms_bwd{li}")
        grads["norm_gain"][li] = dng[0]
    for name in ("norm_gain", "a_v_gain", "a_w_s", "a_b_s"):
        grads[name] = jnp.stack(grads[name])
    return sq, dx, grads


ROW = 1024
GATHER_UNITS = (("a_w_in", 1024, 1536), ("a_w_out", 512, 1024), ("b_w_in", 512, 2560), ("b_w_out", 128, 1024),
                ("c_w_in", 512, 1024), ("c_w_grp", 256, 512), ("c_w_out", 256, 1024))
GAIN_ROWS = 16
REDUCE_UNITS = (("a_w_in", 0, 512, 1536), ("a_w_in", 1, 512, 1536), ("a_w_out", 0, 256, 1024), ("a_w_out", 1, 256, 1024),
                ("b_w_in", 0, 512, 2560), ("b_w_out", 0, 128, 1024), ("c_w_in", 0, 512, 1024), ("c_w_grp", 0, 256, 512),
                ("c_w_out", 0, 256, 1024))
EARLY_UNITS = {3: (1, 3), 2: (6, 7, 8), 1: (4, 5)}
SMALL_OF_HALF = ("a_v_gain", "c_scale")
SMALL_PAD = 16
REP = (("norm_gain", (4, 1024)), ("a_w_s", (2, 8, 128, 128)), ("a_b_s", (2, 8, 128)), ("b_q_gain", (1, 3, 128)), ("b_k_gain", (1, 3, 128)))
REP_CORE = 48
REP_PART = 2 * REP_CORE
SMALL_ROWS = SMALL_PAD + REP_CORE
BLOCK_SHAPES = {"a_w_in": (2, 1024, 1536), "a_v_gain": (2, 512), "a_w_out": (2, 512, 1024), "b_w_in": (1, 1024, 2560),
                "b_w_out": (1, 256, 1024), "c_w_in": (1, 1024, 1024), "c_w_grp": (1, 4, 128, 512), "c_scale": (1, 512),
                "c_w_out": (1, 512, 1024)}
WEIGHTS = ("norm_gain", "a_w_in", "a_v_gain", "a_w_s", "a_b_s", "a_w_out", "b_w_in", "b_q_gain", "b_k_gain", "b_w_out",
           "c_w_in", "c_w_grp", "c_scale", "c_w_out")


def _rows(a, rows):
    a = a.reshape(-1)
    return jnp.pad(a, (0, rows * ROW - a.shape[0])).reshape(rows, ROW)


def _small_unit(vecs, rep_part, dtype):
    halves = [jnp.concatenate([_rows(vecs[h].astype(dtype), SMALL_PAD), rep_part[h * REP_CORE:(h + 1) * REP_CORE].astype(dtype)])
              for h in range(2)]
    return jnp.stack(halves)


def _pack_rep(tree):
    return _rows(jnp.concatenate([tree[n].astype(F32).reshape(-1) for n, _ in REP]), N_CHIPS * REP_PART)


def _unpack_rep(slab):
    flat, out, off = slab.reshape(-1), {}, 0
    for n, shape in REP:
        size = math.prod(shape)
        out[n] = flat[off:off + size].reshape(shape)
        off += size
    return out


def _into_slot(x, me, name):
    rows, width = x.shape
    tr = min(rows, 256)

    def body(me_ref, x_ref, o_ref):
        o_ref[...] = x_ref[...].astype(o_ref.dtype)

    return pl.pallas_call(
        body, name=name,
        grid_spec=pltpu.PrefetchScalarGridSpec(
            num_scalar_prefetch=1, grid=(rows // tr,), in_specs=[pl.BlockSpec((tr, width), lambda i, me_ref: (i, 0))],
            out_specs=pl.BlockSpec((None, tr, width), lambda i, me_ref: (me_ref[0], i, 0))),
        out_shape=jax.ShapeDtypeStruct((N_CHIPS, rows, width), CDT), compiler_params=_cparams("parallel"),
    )(me.reshape(1), x)


def _gather_inputs(wts, me):
    units = [_into_slot(wts[n].reshape(2 * r, w), me, f"slot_{n}").reshape(N_CHIPS, 2, r, w) for n, r, w in GATHER_UNITS]
    gains = jnp.concatenate([wts["a_v_gain"].reshape(-1), wts["c_scale"].reshape(-1)])
    gains = _rows(lax.bitcast_convert_type(gains, CDT), 2 * GAIN_ROWS)
    return units + [_into_slot(gains, me, "slot_gains").reshape(N_CHIPS, 2, GAIN_ROWS, ROW)]


def _gathered_weights(units):
    out = {n: u for (n, _, _), u in zip(GATHER_UNITS, units)}
    for n in ("b_w_in", "b_w_out", "c_w_in", "c_w_out"):
        u = out[n]
        out[n] = u.reshape(N_CHIPS, 2 * u.shape[2], u.shape[3])
    gains = lax.bitcast_convert_type(units[-1][:, 0, :3].reshape(N_CHIPS, 1536, 2), F32)
    out["a_v_gain"] = jnp.concatenate([gains[k, :1024].reshape(2, 512) for k in range(N_CHIPS)], axis=1)
    out["c_scale"] = jnp.concatenate([gains[k, 1024:].reshape(1, 512) for k in range(N_CHIPS)], axis=1)
    return out


def _place():
    x, y, c = lax.axis_index("x"), lax.axis_index("y"), lax.axis_index("c")
    chips = [(1 - x, y), (x, 1 - y), (1 - x, 1 - y)]
    return x, y, c, 2 * x + y, (x, y, 1 - c), chips


def _remote(src, dst, sems, j, to):
    send_sems, recv_sems = sems
    return pltpu.make_async_remote_copy(src_ref=src, dst_ref=dst, send_sem=send_sems.at[j], recv_sem=recv_sems.at[j],
                                        device_id=to, device_id_type=MESH)


def _comm_call(body, name, out_shape, n_sems, *args):
    return pl.pallas_call(
        body, name=name, in_specs=[ANY] * len(args), out_specs=ANY, out_shape=out_shape,
        scratch_shapes=[pltpu.SemaphoreType.DMA((n_sems,)), pltpu.SemaphoreType.DMA((n_sems,))],
    )(*args)


def _comm_call_multi(body, name, out_shapes, n_sems, args, aliases=None):
    return pl.pallas_call(
        body, name=name, in_specs=[ANY] * len(args), out_specs=[ANY] * len(out_shapes), out_shape=out_shapes,
        scratch_shapes=[pltpu.SemaphoreType.DMA((n_sems,)), pltpu.SemaphoreType.DMA((n_sems,))],
        input_output_aliases=aliases or {},
    )(*args)


def _allgather_units(units):
    n = len(units)

    def body(*refs):
        outs, sems = refs[n:2 * n], (refs[2 * n], refs[2 * n + 1])
        x, y, c, me, sibling, chips = _place()
        first, passed = [], []
        for u, o_ref in enumerate(outs):
            for j, chip in enumerate(chips):
                first.append(_remote(o_ref.at[me, c], o_ref.at[me, c], sems, 6 * u + j, (*chip, c)))
                first[-1].start()
        for u, o_ref in enumerate(outs):
            for j, (cx, cy) in enumerate(chips):
                landed = o_ref.at[2 * cx + cy, c]
                _remote(landed, landed, sems, 6 * u + j, sibling).wait_recv()
                passed.append(_remote(landed, landed, sems, 6 * u + 3 + j, sibling))
                passed[-1].start()
        for u, o_ref in enumerate(outs):
            for j, (cx, cy) in enumerate(chips):
                landed = o_ref.at[2 * cx + cy, 1 - c]
                _remote(landed, landed, sems, 6 * u + 3 + j, sibling).wait_recv()
        for cp in first + passed:
            cp.wait_send()

    return _comm_call_multi(body, "allgather_weights", [jax.ShapeDtypeStruct(u.shape, u.dtype) for u in units], 6 * n, units,
                            aliases={u: u for u in range(n)})


def _swap_halves(units, name):
    n = len(units)

    def body(*refs):
        ins, outs, sems = refs[:n], refs[n:2 * n], (refs[2 * n], refs[2 * n + 1])
        x, y, c, me, sibling, chips = _place()
        sent = [_remote(g_ref.at[:, 1 - c], o_ref, sems, u, sibling) for u, (g_ref, o_ref) in enumerate(zip(ins, outs))]
        for cp in sent:
            cp.start()
        for cp in sent:
            cp.wait()

    shapes = [jax.ShapeDtypeStruct((N_CHIPS,) + u.shape[2:], u.dtype) for u in units]
    return _comm_call_multi(body, name, shapes, n, units)


def _scatter_shards(units):
    n = len(units)

    def body(*refs):
        ins, outs, sems = refs[:n], refs[n:2 * n], (refs[2 * n], refs[2 * n + 1])
        x, y, c, me, sibling, chips = _place()
        sent = []
        for u, (s_ref, o_ref) in enumerate(zip(ins, outs)):
            for j, (cx, cy) in enumerate(chips):
                sent.append(_remote(s_ref.at[2 * cx + cy], o_ref.at[me], sems, 3 * u + j, (cx, cy, c)))
                sent[-1].start()
        for u, o_ref in enumerate(outs):
            for j, (cx, cy) in enumerate(chips):
                slot = o_ref.at[2 * cx + cy]
                _remote(slot, slot, sems, 3 * u + j, sibling).wait_recv()
        for cp in sent:
            cp.wait_send()

    return _comm_call_multi(body, "grad_scatter_shards", [jax.ShapeDtypeStruct(u.shape, u.dtype) for u in units], 3 * n, units)


HBM = pl.BlockSpec(memory_space=pltpu.HBM)
SEM = pl.BlockSpec(memory_space=pltpu.SEMAPHORE)
SIDE_EFFECT = pltpu.SideEffectType.DATAFLOW_SIDE_EFFECTING


def _scatter_start(units, name):
    n = len(units)

    def body(*refs):
        srcs, lands = refs[:n], refs[n:2 * n]
        send_sems, recv_sems, token = refs[2 * n], refs[2 * n + 1], refs[-1]
        x, y, c, me, sibling, chips = _place()
        for u in range(n):
            for j, (cx, cy) in enumerate(chips):
                _remote(srcs[u].at[2 * cx + cy], lands[u].at[me], (send_sems, recv_sems), 3 * u + j, (cx, cy, c)).start()
        token[...] = jnp.zeros_like(token)

    hbm = [pltpu.HBM(u.shape, u.dtype) for u in units]
    outs = pl.pallas_call(
        body, name=name,
        out_shape=(pltpu.SemaphoreType.DMA((3 * n,)), pltpu.SemaphoreType.DMA((3 * n,)), *hbm, *hbm, jax.ShapeDtypeStruct((8, 128), F32)),
        in_specs=[HBM] * (2 * n), out_specs=(SEM, SEM, *[HBM] * (2 * n), pl.BlockSpec(memory_space=pltpu.VMEM)),
        input_output_aliases={i: 2 + i for i in range(2 * n)},
        compiler_params=pltpu.CompilerParams(has_side_effects=SIDE_EFFECT),
    )(*[pltpu.with_memory_space_constraint(u, pltpu.HBM) for u in units],
      *[pltpu.with_memory_space_constraint(lax.empty(u.shape, u.dtype), pltpu.HBM) for u in units])
    return outs[0], outs[1], outs[2:2 + n], outs[2 + n:2 + 2 * n], outs[-1]


def _scatter_wait(send_sems, recv_sems, srcs, lands, after, name):
    n = len(srcs)

    def body(*refs):
        srcs_, lands_, send_sems_, recv_sems_ = refs[:n], refs[n:2 * n], refs[2 * n], refs[2 * n + 1]
        x, y, c, me, sibling, chips = _place()
        for u in range(n):
            for j, (cx, cy) in enumerate(chips):
                slot = lands_[u].at[2 * cx + cy]
                cp = _remote(srcs_[u].at[2 * cx + cy], slot, (send_sems_, recv_sems_), 3 * u + j, (cx, cy, c))
                cp.wait_send()
                cp.wait_recv()

    hbm = [pltpu.HBM(u.shape, u.dtype) for u in srcs]
    outs = pl.pallas_call(
        body, name=name, out_shape=(*hbm, *hbm),
        in_specs=[HBM] * (2 * n) + [SEM, SEM, ANY], out_specs=[HBM] * (2 * n),
        input_output_aliases={i: i for i in range(2 * n)},
        compiler_params=pltpu.CompilerParams(has_side_effects=SIDE_EFFECT),
    )(*srcs, *lands, send_sems, recv_sems, after)
    return list(outs[n:])


def _share_halves(units):
    n = len(units)

    def body(*refs):
        outs, sems = refs[n:2 * n], (refs[2 * n], refs[2 * n + 1])
        x, y, c, me, sibling, chips = _place()
        sent = [_remote(o_ref.at[:, c], o_ref.at[:, c], sems, u, sibling) for u, o_ref in enumerate(outs)]
        for cp in sent:
            cp.start()
        for u, o_ref in enumerate(outs):
            theirs = o_ref.at[:, 1 - c]
            _remote(theirs, theirs, sems, u, sibling).wait_recv()
        for cp in sent:
            cp.wait_send()

    return _comm_call_multi(body, "share_halves", [jax.ShapeDtypeStruct(u.shape, u.dtype) for u in units], n, units,
                            aliases={u: u for u in range(n)})


def _allgather_rep(rep4):
    def body(r_ref, o_ref, send_sems, recv_sems):
        x, y, c, me, sibling, chips = _place()
        sems = (send_sems, recv_sems)
        sent = [_remote(r_ref, o_ref.at[me], sems, j, (*chip, c)) for j, chip in enumerate(chips)]
        for cp in sent:
            cp.start()
        for j, (cx, cy) in enumerate(chips):
            slot = o_ref.at[2 * cx + cy]
            _remote(slot, slot, sems, j, sibling).wait_recv()
        for cp in sent:
            cp.wait_send()

    return _comm_call(body, "allgather_rep", jax.ShapeDtypeStruct((N_CHIPS,) + rep4.shape, rep4.dtype), 3, rep4)


ADAM_TILE = 64


def _add_halves(mine, theirs, place, name):
    rows, width = theirs.shape[1:]
    tr = min(rows, 256)

    def body(place_ref, a_ref, b_ref, o_ref):
        o_ref[...] = (a_ref[...].astype(F32) + b_ref[...].astype(F32)).astype(o_ref.dtype)

    blk = pl.BlockSpec((None, tr, width), lambda k, i, place_ref: (k, i, 0))
    return pl.pallas_call(
        body, name=name,
        grid_spec=pltpu.PrefetchScalarGridSpec(
            num_scalar_prefetch=1, grid=(N_CHIPS, rows // tr),
            in_specs=[pl.BlockSpec((None, None, tr, width), lambda k, i, place_ref: (k, place_ref[1], i, 0)), blk], out_specs=blk),
        out_shape=jax.ShapeDtypeStruct(theirs.shape, theirs.dtype), compiler_params=_cparams("parallel", "parallel"),
    )(place, mine, theirs)


def _adamw(place, parts, own, w, m, v, layer, name):
    rows, width = parts.shape[1:]

    def body(place_ref, p_ref, own_ref, w_ref, m_ref, v_ref, o_ref):
        g = jnp.zeros((ADAM_TILE, width), F32)
        for k in range(N_CHIPS):
            g = g + jnp.where(place_ref[0] == k, own_ref[...], p_ref[k]).astype(F32)
        m2 = ADAM_B1 * m_ref[...] + (1.0 - ADAM_B1) * g
        v2 = ADAM_B2 * v_ref[...] + (1.0 - ADAM_B2) * jnp.square(g)
        m_hat = m2 / (1.0 - ADAM_B1 ** ADAM_STEP)
        v_hat = v2 / (1.0 - ADAM_B2 ** ADAM_STEP)
        o_ref[0] = g
        o_ref[1] = -ADAM_LR * (m_hat / (jnp.sqrt(v_hat) + ADAM_EPS) + ADAM_WD * w_ref[...])
        o_ref[2] = m2
        o_ref[3] = v2

    blk = pl.BlockSpec((None, None, ADAM_TILE, width), lambda i, place_ref: (layer, place_ref[1], i, 0))
    blk4 = pl.BlockSpec((4, ADAM_TILE, width), lambda i, place_ref: (0, i, 0))
    mine = pl.BlockSpec((None, ADAM_TILE, width), lambda i, place_ref: (place_ref[0], i, 0))
    out = pl.BlockSpec((4, None, ADAM_TILE, width), lambda i, place_ref: (0, place_ref[1], i, 0))
    return pl.pallas_call(
        body, name=name,
        grid_spec=pltpu.PrefetchScalarGridSpec(num_scalar_prefetch=1, grid=(rows // ADAM_TILE,),
                                               in_specs=[blk4, mine, blk, blk, blk], out_specs=out),
        out_shape=jax.ShapeDtypeStruct((4, 2, rows, width), F32), compiler_params=_cparams("parallel"),
    )(place, parts, own, w, m, v)


def kernel(x, norm_gain, a_w_in, a_v_gain, a_w_s, a_b_s, a_w_out, b_w_in, b_q_gain, b_k_gain, b_w_out, c_w_in, c_w_grp, c_scale, c_w_out, loss_target, m_norm_gain, m_a_w_in, m_a_v_gain, m_a_w_s, m_a_b_s, m_a_w_out, m_b_w_in, m_b_q_gain, m_b_k_gain, m_b_w_out, m_c_w_in, m_c_w_grp, m_c_scale, m_c_w_out, v_norm_gain, v_a_w_in, v_a_v_gain, v_a_w_s, v_a_b_s, v_a_w_out, v_b_w_in, v_b_q_gain, v_b_k_gain, v_b_w_out, v_c_w_in, v_c_w_grp, v_c_scale, v_c_w_out):
    wts = dict(norm_gain=norm_gain, a_w_in=a_w_in, a_v_gain=a_v_gain, a_w_s=a_w_s, a_b_s=a_b_s, a_w_out=a_w_out, b_w_in=b_w_in,
               b_q_gain=b_q_gain, b_k_gain=b_k_gain, b_w_out=b_w_out, c_w_in=c_w_in, c_w_grp=c_w_grp, c_scale=c_scale, c_w_out=c_w_out)
    mom1 = dict(norm_gain=m_norm_gain, a_w_in=m_a_w_in, a_v_gain=m_a_v_gain, a_w_s=m_a_w_s, a_b_s=m_a_b_s, a_w_out=m_a_w_out,
                b_w_in=m_b_w_in, b_q_gain=m_b_q_gain, b_k_gain=m_b_k_gain, b_w_out=m_b_w_out, c_w_in=m_c_w_in, c_w_grp=m_c_w_grp,
                c_scale=m_c_scale, c_w_out=m_c_w_out)
    mom2 = dict(norm_gain=v_norm_gain, a_w_in=v_a_w_in, a_v_gain=v_a_v_gain, a_w_s=v_a_w_s, a_b_s=v_a_b_s, a_w_out=v_a_w_out,
                b_w_in=v_b_w_in, b_q_gain=v_b_q_gain, b_k_gain=v_b_k_gain, b_w_out=v_b_w_out, c_w_in=v_c_w_in, c_w_grp=v_c_w_grp,
                c_scale=v_c_scale, c_w_out=v_c_w_out)
    axes = ("x", "y", "c")
    me = 2 * lax.axis_index("x") + lax.axis_index("y")
    core = lax.axis_index("c")

    place = jnp.stack([me, core]).astype(jnp.int32)

    full = _gathered_weights(_allgather_units(_gather_inputs(wts, me)))
    for n, _ in REP:
        full[n] = wts[n]

    def unit_grad(u, grads):
        n, layer, r, w = REDUCE_UNITS[u]
        g = grads[n][layer] if n in ("a_w_in", "a_w_out") else grads[n]
        return g.astype(CDT).reshape(N_CHIPS, 2, r, w)

    chip_sums, in_flight = {}, []

    def start_exchange(li, grads):
        if li not in EARLY_UNITS:
            return jnp.zeros((), F32)
        units = EARLY_UNITS[li]
        mine = [unit_grad(u, grads) for u in units]
        theirs = _swap_halves(mine, f"grad_swap_halves_l{li}")
        sums = [_add_halves(g, t, place, f"grad_add_halves{u}") for u, g, t in zip(units, mine, theirs)]
        chip_sums.update(zip(units, sums))
        send_sems, recv_sems, srcs, lands, token = _scatter_start(sums, f"grad_scatter_start_l{li}")
        in_flight.append((li, units, send_sems, recv_sems, srcs, lands))
        return token[0, 0]

    sq, grad_x, grads = _local_step(x[0], loss_target[0], full, start_exchange)
    loss = lax.psum(0.5 * jnp.sum(sq) / D_MODEL, axes)

    rep_g = _pack_rep(grads)
    late = [u for u in range(len(REDUCE_UNITS)) if not any(u in us for us in EARLY_UNITS.values())]
    mine = [unit_grad(u, grads) for u in late]
    mine.append(jnp.stack([
        _small_unit([lax.slice_in_dim(grads[v], 512 * k, 512 * (k + 1), axis=1) for v in SMALL_OF_HALF],
                    rep_g[k * REP_PART:(k + 1) * REP_PART], CDT) for k in range(N_CHIPS)]))
    late.append(len(REDUCE_UNITS))
    theirs = _swap_halves(mine, "grad_swap_halves_late")
    sums = [_add_halves(g, t, place, f"grad_add_halves{u}") for u, g, t in zip(late, mine, theirs)]
    chip_sums.update(zip(late, sums))
    parts = dict(zip(late, _scatter_shards(sums)))
    for li, units, send_sems, recv_sems, srcs, lands in in_flight:
        parts.update(zip(units, _scatter_wait(send_sems, recv_sems, srcs, lands, grad_x, f"grad_scatter_wait_l{li}")))
    order = range(len(REDUCE_UNITS) + 1)
    parts, chip_sums = [parts[u] for u in order], [chip_sums[u] for u in order]

    def state_units(tree):
        units = [tree[n].reshape(BLOCK_SHAPES[n][0], 2, r, w) for n, _, r, w in REDUCE_UNITS]
        rep_part = lax.dynamic_slice_in_dim(_pack_rep(tree), me * REP_PART, REP_PART, axis=0)
        return units + [_small_unit([tree[v] for v in SMALL_OF_HALF], rep_part, F32)[None]]

    layers = [layer for _, layer, _, _ in REDUCE_UNITS] + [0]
    res = _share_halves([_adamw(place, p, s, w_, m_, v_, layer, f"adamw{u}") for u, (p, s, w_, m_, v_, layer) in enumerate(
        zip(parts, chip_sums, state_units(wts), state_units(mom1), state_units(mom2), layers))])
    small = res[-1]
    rep_mine = jnp.concatenate([small[:, 0, SMALL_PAD:], small[:, 1, SMALL_PAD:]], axis=1)
    rep_all = lax.dynamic_update_slice_in_dim(_allgather_rep(rep_mine), rep_mine[None], me, axis=0)

    outs = []
    for q in range(4):
        tree = {}
        for (n, layer, r, w), u in zip(REDUCE_UNITS, res):
            tree.setdefault(n, []).append(u[q].reshape(BLOCK_SHAPES[n][1:]))
        tree = {n: jnp.stack(v) for n, v in tree.items()}
        tree["a_v_gain"] = small[q, 0, 0].reshape(2, 512)
        tree["c_scale"] = small[q, 1, 0, :512].reshape(1, 512)
        tree.update(_unpack_rep(rep_all[:, q].reshape(N_CHIPS * REP_PART, ROW)))
        outs.append(tree)
    return (loss, grad_x[None], *[t[n] for t in outs for n in WEIGHTS])
```

```python
import functools
import math

import jax
import jax.numpy as jnp
from jax import lax
from jax.experimental import pallas as pl
from jax.experimental.pallas import tpu as pltpu

F32 = jnp.float32
CDT = jnp.bfloat16

D_MODEL = 1024
EPS = 1e-6
CHUNK = 128
A_WIDTH = 2048
A_GROUPS = 8
A_GROUP_DIM = 256
HEAD_DIM = 128
B_HEADS = 8
B_DILATIONS = (1, 4, 16)
B_QK = 6144
B_IN = 10240
ROPE_HALF = 16
ROPE_THETA = 500000.0
POOL_SIZES = (2, 4, 8, 16)
POOL_HALO = 16
C_WIDTH = 2048
C_GROUP = 512
N_CHIPS = 4

ADAM_LR = 0.001
ADAM_B1 = 0.9
ADAM_B2 = 0.999
ADAM_EPS = 1e-08
ADAM_WD = 0.01
ADAM_STEP = 10

VMEM_LIMIT = 48 * 1024 * 1024
ANY = pl.BlockSpec(memory_space=pl.ANY)
MESH = pl.DeviceIdType.MESH

NN = (((1,), (0,)), ((), ()))
NT = (((1,), (1,)), ((), ()))
TN = (((0,), (0,)), ((), ()))


def _cparams(*sem):
    return pltpu.CompilerParams(dimension_semantics=sem, vmem_limit_bytes=VMEM_LIMIT)


def _dot(a, b, dims=NN):
    return lax.dot_general(a, b, dims, preferred_element_type=F32)


def _sigmoid(z):
    return 1.0 / (1.0 + jnp.exp(-z))


def _lane_sums(v):
    ones = jnp.ones((HEAD_DIM, HEAD_DIM), jnp.bfloat16)
    hi = v.astype(jnp.bfloat16)
    lo = (v - hi.astype(F32)).astype(jnp.bfloat16)
    return _dot(hi, ones) + _dot(lo, ones)


def _mm(a, b, mode, out_dtype, name, mnk, tiles, b_spec=None, o_spec=None, o_shape=None, add=None):
    m, n, k = mnk
    tm, tn, tk = min(tiles[0], m), min(tiles[1], n), min(tiles[2], k)
    nk = k // tk
    a_spec = {"nn": pl.BlockSpec((tm, tk), lambda i, j, q: (i, q)),
              "nt": pl.BlockSpec((tm, tk), lambda i, j, q: (i, q)),
              "tn": pl.BlockSpec((tk, tm), lambda i, j, q: (q, i))}[mode]
    if b_spec is None:
        b_spec = {"nn": pl.BlockSpec((tk, tn), lambda i, j, q: (q, j)),
                  "nt": pl.BlockSpec((tn, tk), lambda i, j, q: (j, q)),
                  "tn": pl.BlockSpec((tk, tn), lambda i, j, q: (q, j))}[mode]
    if o_spec is None:
        o_spec, o_shape = pl.BlockSpec((tm, tn), lambda i, j, q: (i, j)), (m, n)
    dims = {"nn": NN, "nt": NT, "tn": TN}[mode]
    has_add = add is not None

    def body(*refs):
        a_ref, b_ref = refs[0], refs[1]
        o_ref = refs[3] if has_add else refs[2]
        p = _dot(a_ref[...], b_ref[...], dims)

        def finish(v):
            if has_add:
                v = v + refs[2][...]
            o_ref[...] = v.astype(o_ref.dtype)

        if nk == 1:
            finish(p)
        else:
            acc_ref = refs[-1]
            q = pl.program_id(2)

            @pl.when(q == 0)
            def _():
                acc_ref[...] = p

            @pl.when(q > 0)
            def _():
                acc_ref[...] += p

            @pl.when(q == nk - 1)
            def _():
                finish(acc_ref[...])

    in_specs = [a_spec, b_spec]
    args = [a, b]
    if has_add:
        in_specs.append(pl.BlockSpec((tm, tn), lambda i, j, q: (i, j)))
        args.append(add)
    return pl.pallas_call(
        body, name=name, grid=(m // tm, n // tn, nk), in_specs=in_specs, out_specs=o_spec,
        out_shape=jax.ShapeDtypeStruct(o_shape, out_dtype),
        scratch_shapes=[pltpu.VMEM((tm, tn), F32)] if nk > 1 else [],
        compiler_params=_cparams("parallel", "parallel", "arbitrary"),
    )(*args)


def _rms_fwd(x, g, name, tq=512):
    s, d = x.shape

    def body(x_ref, g_ref, h_ref):
        xv = x_ref[...]
        r = lax.rsqrt(jnp.mean(xv * xv, axis=-1, keepdims=True) + EPS)
        h_ref[...] = (xv * r * g_ref[...]).astype(h_ref.dtype)

    return pl.pallas_call(
        body, name=name, grid=(s // tq,),
        in_specs=[pl.BlockSpec((tq, d), lambda i: (i, 0)), pl.BlockSpec((1, d), lambda i: (0, 0))],
        out_specs=pl.BlockSpec((tq, d), lambda i: (i, 0)),
        out_shape=jax.ShapeDtypeStruct((s, d), CDT), compiler_params=_cparams("parallel"),
    )(x, g)


def _rms_bwd(x, dh, dxo, g, name, tq=512):
    s, d = x.shape

    def body(x_ref, dh_ref, dxo_ref, g_ref, dx_ref, dxc_ref, dg_ref):
        xv = x_ref[...]
        dhv = dh_ref[...].astype(F32)
        r = lax.rsqrt(jnp.mean(xv * xv, axis=-1, keepdims=True) + EPS)
        gd = dhv * g_ref[...]
        dx = dxo_ref[...] + r * gd - xv * (r * r * r) * jnp.mean(gd * xv, axis=-1, keepdims=True)
        dx_ref[...] = dx
        dxc_ref[...] = dx.astype(dxc_ref.dtype)

        @pl.when(pl.program_id(0) == 0)
        def _():
            dg_ref[...] = jnp.zeros_like(dg_ref)

        dg_ref[...] += jnp.sum(dhv * xv * r, axis=0, keepdims=True)

    blk = pl.BlockSpec((tq, d), lambda i: (i, 0))
    vec = pl.BlockSpec((1, d), lambda i: (0, 0))
    return pl.pallas_call(
        body, name=name, grid=(s // tq,), in_specs=[blk, blk, blk, vec], out_specs=[blk, blk, vec],
        out_shape=[jax.ShapeDtypeStruct((s, d), F32), jax.ShapeDtypeStruct((s, d), CDT), jax.ShapeDtypeStruct((1, d), F32)],
        compiler_params=_cparams("arbitrary"),
    )(x, dh, dxo, g)


def _loss_bwd(y, target, name, tq=512):
    s, d = y.shape

    def body(y_ref, t_ref, dx_ref, dxc_ref, sq_ref):
        err = y_ref[...] - t_ref[...]
        dx = err * (1.0 / d)
        dx_ref[...] = dx
        dxc_ref[...] = dx.astype(dxc_ref.dtype)

        @pl.when(pl.program_id(0) == 0)
        def _():
            sq_ref[...] = jnp.zeros_like(sq_ref)

        sq_ref[...] += jnp.sum(err * err, axis=0, keepdims=True)

    blk = pl.BlockSpec((tq, d), lambda i: (i, 0))
    vec = pl.BlockSpec((1, d), lambda i: (0, 0))
    return pl.pallas_call(
        body, name=name, grid=(s // tq,), in_specs=[blk, blk], out_specs=[blk, blk, vec],
        out_shape=[jax.ShapeDtypeStruct((s, d), F32), jax.ShapeDtypeStruct((s, d), CDT), jax.ShapeDtypeStruct((1, d), F32)],
        compiler_params=_cparams("arbitrary"),
    )(y, target)


def _tril_mask():
    row = lax.broadcasted_iota(jnp.int32, (CHUNK, CHUNK), 0)
    col = lax.broadcasted_iota(jnp.int32, (CHUNK, CHUNK), 1)
    return row >= col


def _a_mid_fwd(proj, v_gain, w_s, b_s_t, name, tq=256):
    s = proj.shape[0]

    def body(p_ref, vg_ref, ws_ref, bs_ref, y_ref):
        vraw = p_ref[:, A_WIDTH:2 * A_WIDTH].astype(F32)
        r = lax.rsqrt(jnp.mean(vraw * vraw, axis=-1, keepdims=True) + EPS)
        vn = (vraw * r * vg_ref[...]).astype(CDT)
        tri = _tril_mask()
        for g in range(A_GROUPS):
            w = jnp.where(tri, ws_ref[g], 0.0).astype(CDT)
            bias = bs_ref[:, g:g + 1]
            cols = slice(g * A_GROUP_DIM, (g + 1) * A_GROUP_DIM)
            zcols = slice(2 * A_WIDTH + g * A_GROUP_DIM, 2 * A_WIDTH + (g + 1) * A_GROUP_DIM)
            for c in range(tq // CHUNK):
                rows = slice(c * CHUNK, (c + 1) * CHUNK)
                mixed = _dot(w, vn[rows, cols]) + bias
                u = p_ref[rows, cols].astype(F32)
                z = p_ref[rows, zcols].astype(F32)
                y_ref[rows, cols] = (u * mixed * (z * _sigmoid(z))).astype(y_ref.dtype)

    return pl.pallas_call(
        body, name=name, grid=(s // tq,),
        in_specs=[pl.BlockSpec((tq, 3 * A_WIDTH), lambda i: (i, 0)), pl.BlockSpec((1, A_WIDTH), lambda i: (0, 0)),
                  pl.BlockSpec((A_GROUPS, CHUNK, CHUNK), lambda i: (0, 0, 0)), pl.BlockSpec((CHUNK, A_GROUPS), lambda i: (0, 0))],
        out_specs=pl.BlockSpec((tq, A_WIDTH), lambda i: (i, 0)),
        out_shape=jax.ShapeDtypeStruct((s, A_WIDTH), CDT), compiler_params=_cparams("parallel"),
    )(proj, v_gain, w_s, b_s_t)


def _a_mid_bwd(proj, dy, v_gain, w_s, b_s_t, name, tq=256):
    s = proj.shape[0]

    def body(p_ref, dy_ref, vg_ref, ws_ref, bs_ref, dp_ref, dws_ref, dbs_ref, dvg_ref, dvn_ref):
        @pl.when(pl.program_id(0) == 0)
        def _():
            dws_ref[...] = jnp.zeros_like(dws_ref)
            dbs_ref[...] = jnp.zeros_like(dbs_ref)
            dvg_ref[...] = jnp.zeros_like(dvg_ref)

        vraw = p_ref[:, A_WIDTH:2 * A_WIDTH].astype(F32)
        r = lax.rsqrt(jnp.mean(vraw * vraw, axis=-1, keepdims=True) + EPS)
        vhat = vraw * r
        vg = vg_ref[...]
        vn = (vhat * vg).astype(CDT)
        tri = _tril_mask()
        lane = lax.broadcasted_iota(jnp.int32, (CHUNK, A_GROUPS), 1)
        dbs = jnp.zeros((CHUNK, A_GROUPS), F32)
        for g in range(A_GROUPS):
            w = jnp.where(tri, ws_ref[g], 0.0).astype(CDT)
            bias = bs_ref[:, g:g + 1]
            cols = slice(g * A_GROUP_DIM, (g + 1) * A_GROUP_DIM)
            zcols = slice(2 * A_WIDTH + g * A_GROUP_DIM, 2 * A_WIDTH + (g + 1) * A_GROUP_DIM)
            dws = jnp.zeros((CHUNK, CHUNK), F32)
            for c in range(tq // CHUNK):
                rows = slice(c * CHUNK, (c + 1) * CHUNK)
                vn_g = vn[rows, cols]
                mixed = _dot(w, vn_g) + bias
                u = p_ref[rows, cols].astype(F32)
                z = p_ref[rows, zcols].astype(F32)
                dyv = dy_ref[rows, cols].astype(F32)
                sg = _sigmoid(z)
                sz = z * sg
                dyu = dyv * u
                dmixed = dyu * sz
                dp_ref[rows, cols] = (dyv * mixed * sz).astype(dp_ref.dtype)
                dp_ref[rows, zcols] = (dyu * mixed * (sg * (1.0 + z * (1.0 - sg)))).astype(dp_ref.dtype)
                dmc = dmixed.astype(CDT)
                dws = dws + _dot(dmc, vn_g, NT)
                dbs = dbs + jnp.where(lane == g, jnp.sum(dmixed, axis=-1, keepdims=True), 0.0)
                dvn_ref[rows, cols] = _dot(w, dmc, TN)
            dws_ref[g] += jnp.where(tri, dws, 0.0)
        dbs_ref[...] += dbs
        dvn = dvn_ref[...]
        gd = dvn * vg
        dvraw = r * gd - vraw * (r * r * r) * jnp.mean(gd * vraw, axis=-1, keepdims=True)
        dp_ref[:, A_WIDTH:2 * A_WIDTH] = dvraw.astype(dp_ref.dtype)
        dvg_ref[...] += jnp.sum(dvn * vhat, axis=0, keepdims=True)

    return pl.pallas_call(
        body, name=name, grid=(s // tq,),
        in_specs=[pl.BlockSpec((tq, 3 * A_WIDTH), lambda i: (i, 0)), pl.BlockSpec((tq, A_WIDTH), lambda i: (i, 0)),
                  pl.BlockSpec((1, A_WIDTH), lambda i: (0, 0)), pl.BlockSpec((A_GROUPS, CHUNK, CHUNK), lambda i: (0, 0, 0)),
                  pl.BlockSpec((CHUNK, A_GROUPS), lambda i: (0, 0))],
        out_specs=[pl.BlockSpec((tq, 3 * A_WIDTH), lambda i: (i, 0)), pl.BlockSpec((A_GROUPS, CHUNK, CHUNK), lambda i: (0, 0, 0)),
                   pl.BlockSpec((CHUNK, A_GROUPS), lambda i: (0, 0)), pl.BlockSpec((1, A_WIDTH), lambda i: (0, 0))],
        out_shape=[jax.ShapeDtypeStruct((s, 3 * A_WIDTH), CDT), jax.ShapeDtypeStruct((A_GROUPS, CHUNK, CHUNK), F32),
                   jax.ShapeDtypeStruct((CHUNK, A_GROUPS), F32), jax.ShapeDtypeStruct((1, A_WIDTH), F32)],
        scratch_shapes=[pltpu.VMEM((tq, A_WIDTH), F32)],
        compiler_params=_cparams("arbitrary"),
    )(proj, dy, v_gain, w_s, b_s_t)


def _rope_tables(s):
    inv_freq = jnp.power(jnp.float32(ROPE_THETA), -jnp.arange(ROPE_HALF, dtype=F32) / ROPE_HALF)
    ang = jnp.arange(s, dtype=F32)[:, None] * inv_freq[None, :]
    cos, sin = jnp.cos(ang), jnp.sin(ang)
    rest = HEAD_DIM - 2 * ROPE_HALF
    t_c = jnp.concatenate([cos, cos, jnp.ones((s, rest), F32)], axis=1)
    t_a = jnp.concatenate([-sin, jnp.zeros((s, HEAD_DIM - ROPE_HALF), F32)], axis=1)
    t_b = jnp.concatenate([jnp.zeros((s, ROPE_HALF), F32), sin, jnp.zeros((s, rest), F32)], axis=1)
    return t_c, t_a, t_b


def _b_qk_fwd(proj, gains, tabs, name, tq=256):
    s = proj.shape[0]

    def body(p_ref, g_ref, tc_ref, ta_ref, tb_ref, o_ref):
        tc, ta, tb = tc_ref[...], ta_ref[...], tb_ref[...]
        for tg in range(6):
            gain = g_ref[tg:tg + 1, :]
            for h in range(B_HEADS):
                cols = slice(tg * 1024 + h * HEAD_DIM, tg * 1024 + (h + 1) * HEAD_DIM)
                xv = p_ref[:, cols].astype(F32)
                r = lax.rsqrt(_lane_sums(xv * xv) * (1.0 / HEAD_DIM) + EPS)
                xn = xv * r * gain
                y = xn * tc + pltpu.roll(xn, HEAD_DIM - ROPE_HALF, 1) * ta + pltpu.roll(xn, ROPE_HALF, 1) * tb
                o_ref[:, cols] = y.astype(o_ref.dtype)

    tab = pl.BlockSpec((tq, HEAD_DIM), lambda i: (i, 0))
    return pl.pallas_call(
        body, name=name, grid=(s // tq,),
        in_specs=[pl.BlockSpec((tq, B_QK), lambda i: (i, 0)), pl.BlockSpec((6, HEAD_DIM), lambda i: (0, 0)), tab, tab, tab],
        out_specs=pl.BlockSpec((tq, B_QK), lambda i: (i, 0)),
        out_shape=jax.ShapeDtypeStruct((s, B_QK), CDT), compiler_params=_cparams("parallel"),
    )(proj, gains, *tabs)


PERMUTE_BLOCK_BYTES = 4 * 1024 * 1024


def _view_rows(length, dil, width, itemsize):
    rows = 16
    while 2 * rows * dil * width * itemsize <= PERMUTE_BLOCK_BYTES and 2 * rows <= length:
        rows *= 2
    return rows


def _to_view(x, col, width, dil, name):
    s = x.shape[0]
    length = s // dil
    tl = _view_rows(length, dil, width, 4)
    lanes = HEAD_DIM
    nblk = width // lanes

    def body(x_ref, o_ref, slab_ref):
        for b in range(nblk):
            slab_ref[b] = x_ref[:, b * lanes:(b + 1) * lanes].astype(F32)
        for r in range(dil):
            for b in range(nblk):
                o_ref[:, r * width + b * lanes:r * width + (b + 1) * lanes] = (
                    slab_ref.at[b][pl.ds(r, tl, stride=dil), :].astype(o_ref.dtype))

    return pl.pallas_call(
        body, name=name, grid=(length // tl,),
        in_specs=[pl.BlockSpec((tl * dil, width), lambda i: (i, col))],
        out_specs=pl.BlockSpec((tl, dil * width), lambda i: (i, 0)),
        out_shape=jax.ShapeDtypeStruct((length, dil * width), x.dtype),
        scratch_shapes=[pltpu.VMEM((nblk, tl * dil, lanes), F32)],
        compiler_params=_cparams("parallel"),
    )(x)


def _from_view(v, dil, name):
    length, width = v.shape[0], v.shape[1] // dil
    tl = _view_rows(length, dil, width, 4)
    lanes = HEAD_DIM
    nblk = width // lanes

    def body(v_ref, o_ref, slab_ref):
        for r in range(dil):
            for b in range(nblk):
                slab_ref.at[b][pl.ds(r, tl, stride=dil), :] = v_ref[:, r * width + b * lanes:r * width + (b + 1) * lanes].astype(F32)
        for b in range(nblk):
            o_ref[:, b * lanes:(b + 1) * lanes] = slab_ref[b].astype(o_ref.dtype)

    return pl.pallas_call(
        body, name=name, grid=(length // tl,),
        in_specs=[pl.BlockSpec((tl, dil * width), lambda i: (i, 0))],
        out_specs=pl.BlockSpec((tl * dil, width), lambda i: (i, 0)),
        out_shape=jax.ShapeDtypeStruct((length * dil, width), v.dtype),
        scratch_shapes=[pltpu.VMEM((nblk, tl * dil, lanes), F32)],
        compiler_params=_cparams("parallel"),
    )(v)


def _b_attn_fwd(q, k, v, bases, dil, name):
    length = q.shape[0]
    nb = length // CHUNK
    scale = 1.0 / math.sqrt(HEAD_DIM)
    w = B_HEADS * HEAD_DIM
    qb, kb, vb = bases

    def body(q_ref, kc_ref, kp_ref, vc_ref, vp_ref, o_ref, lse_ref):
        n = pl.program_id(1)
        qi = lax.broadcasted_iota(jnp.int32, (CHUNK, 2 * CHUNK), 0)
        ki = lax.broadcasted_iota(jnp.int32, (CHUNK, 2 * CHUNK), 1)
        first_key = jnp.where(n > 0, 0, CHUNK)
        mask = (ki >= qi) & (ki <= qi + CHUNK) & (ki >= first_key)
        lane = lax.broadcasted_iota(jnp.int32, (CHUNK, HEAD_DIM), 1)
        lse_all = jnp.zeros((CHUNK, HEAD_DIM), F32)
        for h in range(B_HEADS):
            sl = slice(h * HEAD_DIM, (h + 1) * HEAD_DIM)
            k2 = jnp.concatenate([kp_ref[:, sl], kc_ref[:, sl]], axis=0)
            v2 = jnp.concatenate([vp_ref[:, sl], vc_ref[:, sl]], axis=0)
            sc = jnp.where(mask, _dot(q_ref[:, sl], k2, NT) * scale, -1e30)
            m = jnp.max(sc, axis=-1, keepdims=True)
            p = jnp.exp(sc - m)
            l = jnp.sum(p, axis=-1, keepdims=True)
            o_ref[:, sl] = _dot(p.astype(CDT), v2) / l
            lse_all = jnp.where(lane == h, m + jnp.log(l), lse_all)
        lse_ref[...] = lse_all

    prev = lambda n: jnp.maximum(n - 1, 0)
    blk = lambda f: pl.BlockSpec((CHUNK, w), f)
    return pl.pallas_call(
        body, name=name, grid=(dil, nb),
        in_specs=[blk(lambda r, n: (n, qb + r)), blk(lambda r, n: (n, kb + r)), blk(lambda r, n: (prev(n), kb + r)),
                  blk(lambda r, n: (n, vb + r)), blk(lambda r, n: (prev(n), vb + r))],
        out_specs=[blk(lambda r, n: (n, r)), pl.BlockSpec((CHUNK, HEAD_DIM), lambda r, n: (n, r))],
        out_shape=[jax.ShapeDtypeStruct((length, dil * w), F32), jax.ShapeDtypeStruct((length, dil * HEAD_DIM), F32)],
        compiler_params=_cparams("parallel", "parallel"),
    )(q, k, k, v, v)


def _b_combine(os_, lses, proj, name, tq=512):
    s = proj.shape[0]
    w = B_HEADS * HEAD_DIM

    def body(o0_ref, o1_ref, o2_ref, l0_ref, l1_ref, l2_ref, z_ref, y_ref, oj_ref, lj_ref):
        l0, l1, l2 = l0_ref[...], l1_ref[...], l2_ref[...]
        m = jnp.maximum(jnp.maximum(l0, l1), l2)
        lj = m + jnp.log(jnp.exp(l0 - m) + jnp.exp(l1 - m) + jnp.exp(l2 - m))
        lj_ref[...] = lj
        w0, w1, w2 = jnp.exp(l0 - lj), jnp.exp(l1 - lj), jnp.exp(l2 - lj)
        for h in range(B_HEADS):
            sl = slice(h * HEAD_DIM, (h + 1) * HEAD_DIM)
            o = w0[:, h:h + 1] * o0_ref[:, sl] + w1[:, h:h + 1] * o1_ref[:, sl] + w2[:, h:h + 1] * o2_ref[:, sl]
            z = z_ref[:, sl].astype(F32)
            oj_ref[:, sl] = o.astype(oj_ref.dtype)
            y_ref[:, sl] = (o * (z * _sigmoid(z))).astype(y_ref.dtype)

    blk = pl.BlockSpec((tq, w), lambda i: (i, 0))
    st = pl.BlockSpec((tq, HEAD_DIM), lambda i: (i, 0))
    return pl.pallas_call(
        body, name=name, grid=(s // tq,),
        in_specs=[blk, blk, blk, st, st, st, pl.BlockSpec((tq, w), lambda i: (i, 9))],
        out_specs=[blk, blk, st],
        out_shape=[jax.ShapeDtypeStruct((s, w), CDT), jax.ShapeDtypeStruct((s, w), CDT), jax.ShapeDtypeStruct((s, HEAD_DIM), F32)],
        compiler_params=_cparams("parallel"),
    )(*os_, *lses, proj)


def _b_bwd_prep(dy, oj, proj, name, tq=512):
    s = proj.shape[0]
    w = B_HEADS * HEAD_DIM

    def body(dy_ref, oj_ref, z_ref, do_ref, dz_ref, dd_ref):
        lane = lax.broadcasted_iota(jnp.int32, (tq, HEAD_DIM), 1)
        dd = jnp.zeros((tq, HEAD_DIM), F32)
        for h in range(B_HEADS):
            sl = slice(h * HEAD_DIM, (h + 1) * HEAD_DIM)
            z = z_ref[:, sl].astype(F32)
            dyv = dy_ref[:, sl].astype(F32)
            o = oj_ref[:, sl].astype(F32)
            sg = _sigmoid(z)
            do = dyv * (z * sg)
            do_ref[:, sl] = do.astype(do_ref.dtype)
            dz_ref[:, sl] = (dyv * o * (sg * (1.0 + z * (1.0 - sg)))).astype(dz_ref.dtype)
            dd = jnp.where(lane == h, jnp.sum(do * o, axis=-1, keepdims=True), dd)
        dd_ref[...] = dd

    blk = pl.BlockSpec((tq, w), lambda i: (i, 0))
    st = pl.BlockSpec((tq, HEAD_DIM), lambda i: (i, 0))
    return pl.pallas_call(
        body, name=name, grid=(s // tq,),
        in_specs=[blk, blk, pl.BlockSpec((tq, w), lambda i: (i, 9))], out_specs=[blk, blk, st],
        out_shape=[jax.ShapeDtypeStruct((s, w), CDT), jax.ShapeDtypeStruct((s, w), CDT), jax.ShapeDtypeStruct((s, HEAD_DIM), F32)],
        compiler_params=_cparams("parallel"),
    )(dy, oj, proj)


def _b_attn_bwd(q, k, v, bases, do, lj, dd, dil, name):
    length = q.shape[0]
    nb = length // CHUNK
    scale = 1.0 / math.sqrt(HEAD_DIM)
    w = B_HEADS * HEAD_DIM
    qb, kb, vb = bases

    def body(qj_ref, qn_ref, k_ref, v_ref, doj_ref, don_ref, lj_ref, ln_ref, dj_ref, dn_ref, out_ref, carry_ref):
        j = pl.program_id(1)

        @pl.when(j == 0)
        def _():
            carry_ref[...] = jnp.zeros_like(carry_ref)

        qi = lax.broadcasted_iota(jnp.int32, (2 * CHUNK, CHUNK), 0)
        ki = lax.broadcasted_iota(jnp.int32, (2 * CHUNK, CHUNK), 1)
        no_next = jnp.where(j + 1 < nb, 0, 2 * CHUNK)
        mask = ((qi < CHUNK) & (ki <= qi)) | ((qi >= CHUNK) & (ki >= qi - CHUNK + no_next))
        for h in range(B_HEADS):
            sl = slice(h * HEAD_DIM, (h + 1) * HEAD_DIM)
            q2 = jnp.concatenate([qj_ref[:, sl], qn_ref[:, sl]], axis=0)
            do2 = jnp.concatenate([doj_ref[:, sl], don_ref[:, sl]], axis=0)
            lse2 = jnp.concatenate([lj_ref[:, h:h + 1], ln_ref[:, h:h + 1]], axis=0)
            d2 = jnp.concatenate([dj_ref[:, h:h + 1], dn_ref[:, h:h + 1]], axis=0)
            k = k_ref[:, sl]
            v = v_ref[:, sl]
            sc = _dot(q2, k, NT) * scale
            p = jnp.where(mask, jnp.exp(sc - lse2), 0.0)
            dp = _dot(do2, v, NT)
            ds = (p * (dp - d2) * scale).astype(CDT)
            dq2 = _dot(ds, k)
            out_ref[:, sl] = (carry_ref[:, sl] + dq2[:CHUNK]).astype(out_ref.dtype)
            carry_ref[:, sl] = dq2[CHUNK:]
            out_ref[:, w + h * HEAD_DIM:w + (h + 1) * HEAD_DIM] = _dot(ds, q2, TN).astype(out_ref.dtype)
            out_ref[:, 2 * w + h * HEAD_DIM:2 * w + (h + 1) * HEAD_DIM] = _dot(p.astype(CDT), do2, TN).astype(out_ref.dtype)

    nxt = lambda j: jnp.minimum(j + 1, nb - 1)
    blk = lambda f: pl.BlockSpec((CHUNK, w), f)
    st = lambda f: pl.BlockSpec((CHUNK, HEAD_DIM), f)
    return pl.pallas_call(
        body, name=name, grid=(dil, nb),
        in_specs=[blk(lambda r, j: (j, qb + r)), blk(lambda r, j: (nxt(j), qb + r)),
                  blk(lambda r, j: (j, kb + r)), blk(lambda r, j: (j, vb + r)),
                  blk(lambda r, j: (j, r)), blk(lambda r, j: (nxt(j), r)),
                  st(lambda r, j: (j, r)), st(lambda r, j: (nxt(j), r)), st(lambda r, j: (j, r)), st(lambda r, j: (nxt(j), r))],
        out_specs=pl.BlockSpec((CHUNK, 3 * w), lambda r, j: (j, r)),
        out_shape=jax.ShapeDtypeStruct((length, dil * 3 * w), CDT),
        scratch_shapes=[pltpu.VMEM((CHUNK, w), F32)],
        compiler_params=_cparams("parallel", "arbitrary"),
    )(q, q, k, v, do, do, lj, lj, dd, dd)


def _b_qk_bwd(proj, dqkv, dz, gains, tabs, name, tq=256):
    s = proj.shape[0]
    w = B_HEADS * HEAD_DIM

    def body(p_ref, d0_ref, d1_ref, d2_ref, dz_ref, g_ref, tc_ref, ta_ref, tb_ref, dp_ref, dg_ref):
        @pl.when(pl.program_id(0) == 0)
        def _():
            dg_ref[...] = jnp.zeros_like(dg_ref)

        tc, ta, tb = tc_ref[...], ta_ref[...], tb_ref[...]
        d_refs = (d0_ref, d1_ref, d2_ref)
        for g in range(3):
            for t in range(2):
                tg = t * 3 + g
                gain = g_ref[tg:tg + 1, :]
                dgain = jnp.zeros((1, HEAD_DIM), F32)
                for h in range(B_HEADS):
                    cols = slice(tg * w + h * HEAD_DIM, tg * w + (h + 1) * HEAD_DIM)
                    xv = p_ref[:, cols].astype(F32)
                    r = lax.rsqrt(_lane_sums(xv * xv) * (1.0 / HEAD_DIM) + EPS)
                    dyv = d_refs[g][:, t * w + h * HEAD_DIM:t * w + (h + 1) * HEAD_DIM].astype(F32)
                    dxn = dyv * tc + pltpu.roll(dyv * ta, ROPE_HALF, 1) + pltpu.roll(dyv * tb, HEAD_DIM - ROPE_HALF, 1)
                    gd = dxn * gain
                    dx = r * gd - xv * (r * r * r) * (_lane_sums(gd * xv) * (1.0 / HEAD_DIM))
                    dp_ref[:, cols] = dx.astype(dp_ref.dtype)
                    dgain = dgain + jnp.sum(dxn * xv * r, axis=0, keepdims=True)
                dg_ref[tg:tg + 1, :] += dgain
            dp_ref[:, (6 + g) * w:(7 + g) * w] = d_refs[g][:, 2 * w:3 * w]
        dp_ref[:, 9 * w:10 * w] = dz_ref[...]

    tab = pl.BlockSpec((tq, HEAD_DIM), lambda i: (i, 0))
    dblk = pl.BlockSpec((tq, 3 * w), lambda i: (i, 0))
    return pl.pallas_call(
        body, name=name, grid=(s // tq,),
        in_specs=[pl.BlockSpec((tq, B_QK), lambda i: (i, 0)), dblk, dblk, dblk, pl.BlockSpec((tq, w), lambda i: (i, 0)),
                  pl.BlockSpec((6, HEAD_DIM), lambda i: (0, 0)), tab, tab, tab],
        out_specs=[pl.BlockSpec((tq, B_IN), lambda i: (i, 0)), pl.BlockSpec((6, HEAD_DIM), lambda i: (0, 0))],
        out_shape=[jax.ShapeDtypeStruct((s, B_IN), CDT), jax.ShapeDtypeStruct((6, HEAD_DIM), F32)],
        compiler_params=_cparams("arbitrary"),
    )(proj, *dqkv, dz, gains, *tabs)


def _inv_count(t, window):
    return 1.0 / jnp.minimum(t + 1, window).astype(F32)


POOL_BLOCK = 128


def _band(window, forward):
    r = lax.broadcasted_iota(jnp.int32, (POOL_BLOCK, 2 * POOL_BLOCK), 0)
    j = lax.broadcasted_iota(jnp.int32, (POOL_BLOCK, 2 * POOL_BLOCK), 1)
    if forward:
        return jnp.where((j >= r) & (j < r + window), 1.0, 0.0).astype(CDT)
    return jnp.where((j <= r + POOL_BLOCK) & (j > r + POOL_BLOCK - window), 1.0, 0.0).astype(CDT)


def _split_dot(band, v):
    hi = v.astype(jnp.bfloat16)
    lo = (v - hi.astype(F32)).astype(jnp.bfloat16)
    band = band.astype(jnp.bfloat16)
    return _dot(band, hi) + _dot(band, lo)


def _pool_diff(x_ref, halo_ref, diff_ref, i, tq):
    t = i * tq + lax.broadcasted_iota(jnp.int32, (tq, 1), 0)
    for g, window in enumerate(POOL_SIZES):
        cols = slice(g * C_GROUP, (g + 1) * C_GROUP)
        band = _band(window, forward=False)
        inv = _inv_count(t, window)
        for b in range(tq // POOL_BLOCK):
            rows = slice(b * POOL_BLOCK, (b + 1) * POOL_BLOCK)
            cur = x_ref[rows, cols]
            if b == 0:
                above = jnp.where(i > 0, halo_ref[:, cols], jnp.zeros_like(cur))
            else:
                above = x_ref[(b - 1) * POOL_BLOCK:b * POOL_BLOCK, cols]
            pooled = _dot(band, jnp.concatenate([above, cur], axis=0)) * inv[rows]
            diff_ref[rows, cols] = (pooled - cur.astype(F32)).astype(diff_ref.dtype)


GRP_SHARD = (N_CHIPS, 2, 256, C_GROUP)
GRP_ROWS = C_GROUP // N_CHIPS


def _grp_rows(g):
    return g // 2, slice((g % 2) * GRP_ROWS, (g % 2 + 1) * GRP_ROWS)


def _grp_weight(w_ref, g):
    half, rows = _grp_rows(g)
    return jnp.concatenate([w_ref[k, half, rows, :] for k in range(N_CHIPS)], axis=0)


def _c_mid_fwd(proj, w_grp, scale, name, tq=512):
    s = proj.shape[0]
    per = tq // POOL_BLOCK

    def body(x_ref, halo_ref, z_ref, w_ref, sc_ref, y_ref, diff_ref):
        _pool_diff(x_ref, halo_ref, diff_ref, pl.program_id(0), tq)
        for g in range(len(POOL_SIZES)):
            cols = slice(g * C_GROUP, (g + 1) * C_GROUP)
            z = z_ref[:, cols].astype(F32)
            y_ref[:, cols] = (_dot(diff_ref[:, cols], _grp_weight(w_ref, g)) * sc_ref[:, cols] * (z * _sigmoid(z))).astype(y_ref.dtype)

    return pl.pallas_call(
        body, name=name, grid=(s // tq,),
        in_specs=[pl.BlockSpec((tq, C_WIDTH), lambda i: (i, 0)),
                  pl.BlockSpec((POOL_BLOCK, C_WIDTH), lambda i: (jnp.maximum(i * per - 1, 0), 0)),
                  pl.BlockSpec((tq, C_WIDTH), lambda i: (i, 1)),
                  pl.BlockSpec(GRP_SHARD, lambda i: (0, 0, 0, 0)), pl.BlockSpec((1, C_WIDTH), lambda i: (0, 0))],
        out_specs=pl.BlockSpec((tq, C_WIDTH), lambda i: (i, 0)),
        out_shape=jax.ShapeDtypeStruct((s, C_WIDTH), CDT),
        scratch_shapes=[pltpu.VMEM((tq, C_WIDTH), CDT)], compiler_params=_cparams("parallel"),
    )(proj, proj, proj, w_grp, scale)


def _c_mid_bwd(proj, w_grp, scale, dy, name, tq=512):
    s = proj.shape[0]
    per = tq // POOL_BLOCK

    def body(x_ref, halo_ref, z_ref, w_ref, sc_ref, dy_ref, dd_ref, dz_ref, dw_ref, dsc_ref, diff_ref):
        @pl.when(pl.program_id(0) == 0)
        def _():
            dw_ref[...] = jnp.zeros_like(dw_ref)
            dsc_ref[...] = jnp.zeros_like(dsc_ref)

        _pool_diff(x_ref, halo_ref, diff_ref, pl.program_id(0), tq)
        for g in range(len(POOL_SIZES)):
            cols = slice(g * C_GROUP, (g + 1) * C_GROUP)
            d = diff_ref[:, cols]
            wg = _grp_weight(w_ref, g)
            half, rows = _grp_rows(g)
            m0 = _dot(d, wg)
            z = z_ref[:, cols].astype(F32)
            dyv = dy_ref[:, cols].astype(F32)
            sc = sc_ref[:, cols]
            sg = _sigmoid(z)
            dmixed = dyv * (z * sg)
            dz_ref[:, cols] = (dyv * m0 * sc * (sg * (1.0 + z * (1.0 - sg)))).astype(dz_ref.dtype)
            dsc_ref[:, cols] += jnp.sum(dmixed * m0, axis=0, keepdims=True)
            dm0 = (dmixed * sc).astype(CDT)
            dwg = _dot(d, dm0, TN)
            for k in range(N_CHIPS):
                dw_ref[k, half, rows, :] += dwg[k * GRP_ROWS:(k + 1) * GRP_ROWS]
            dd_ref[:, cols] = _dot(dm0, wg, NT)

    blk = pl.BlockSpec((tq, C_WIDTH), lambda i: (i, 0))
    wsp = pl.BlockSpec(GRP_SHARD, lambda i: (0, 0, 0, 0))
    vec = pl.BlockSpec((1, C_WIDTH), lambda i: (0, 0))
    return pl.pallas_call(
        body, name=name, grid=(s // tq,),
        in_specs=[blk, pl.BlockSpec((POOL_BLOCK, C_WIDTH), lambda i: (jnp.maximum(i * per - 1, 0), 0)),
                  pl.BlockSpec((tq, C_WIDTH), lambda i: (i, 1)), wsp, vec, blk],
        out_specs=[blk, blk, wsp, vec],
        out_shape=[jax.ShapeDtypeStruct((s, C_WIDTH), F32), jax.ShapeDtypeStruct((s, C_WIDTH), CDT),
                   jax.ShapeDtypeStruct(GRP_SHARD, F32), jax.ShapeDtypeStruct((1, C_WIDTH), F32)],
        scratch_shapes=[pltpu.VMEM((tq, C_WIDTH), CDT)], compiler_params=_cparams("arbitrary"),
    )(proj, proj, proj, w_grp, scale, dy)


def _c_pool_bwd(ddiff, dz, name, tq=512):
    s = ddiff.shape[0]
    per = tq // POOL_BLOCK
    last = s // tq - 1

    def body(d_ref, halo_ref, dz_ref, o_ref):
        i = pl.program_id(0)
        t = i * tq + lax.broadcasted_iota(jnp.int32, (tq, 1), 0)
        for g, window in enumerate(POOL_SIZES):
            cols = slice(g * C_GROUP, (g + 1) * C_GROUP)
            band = _band(window, forward=True)
            inv = _inv_count(t, window)
            for b in range(tq // POOL_BLOCK):
                rows = slice(b * POOL_BLOCK, (b + 1) * POOL_BLOCK)
                cur = d_ref[rows, cols]
                if b == tq // POOL_BLOCK - 1:
                    below = jnp.where(i < last, halo_ref[:, cols] * (1.0 / window), 0.0)
                else:
                    nxt = slice((b + 1) * POOL_BLOCK, (b + 2) * POOL_BLOCK)
                    below = d_ref[nxt, cols] * inv[nxt]
                summed = _split_dot(band, jnp.concatenate([cur * inv[rows], below], axis=0))
                o_ref[rows, cols] = (summed - cur).astype(o_ref.dtype)
        o_ref[:, C_WIDTH:] = dz_ref[...]

    return pl.pallas_call(
        body, name=name, grid=(s // tq,),
        in_specs=[pl.BlockSpec((tq, C_WIDTH), lambda i: (i, 0)),
                  pl.BlockSpec((POOL_BLOCK, C_WIDTH), lambda i: (jnp.minimum((i + 1) * per, s // POOL_BLOCK - 1), 0)),
                  pl.BlockSpec((tq, C_WIDTH), lambda i: (i, 0))],
        out_specs=pl.BlockSpec((tq, 2 * C_WIDTH), lambda i: (i, 0)),
        out_shape=jax.ShapeDtypeStruct((s, 2 * C_WIDTH), CDT),
        compiler_params=_cparams("parallel"),
    )(ddiff, ddiff, dz)


LAYERS = (("a", 0), ("b", 0), ("c", 0), ("a", 1))


def _shard_spec(block, layer, where):
    if layer is None:
        return pl.BlockSpec((None,) + block, where)
    return pl.BlockSpec((None, None) + block, lambda i, j, q: (where(i, j, q)[0], layer) + where(i, j, q)[1:])


def _w_in_fwd(h, gw, layer, name):
    s, width = h.shape[0], gw.shape[-1]
    return _mm(h, gw, "nn", CDT, name, (s, N_CHIPS * width, D_MODEL), (1024, width, D_MODEL),
               b_spec=_shard_spec((D_MODEL, width), layer, lambda i, j, q: (j, 0, 0)))


def _w_in_dh(dproj, gw, layer, name):
    s, width = dproj.shape[0], gw.shape[-1]
    return _mm(dproj, gw, "nt", F32, name, (s, D_MODEL, N_CHIPS * width), (1024, D_MODEL, width),
               b_spec=_shard_spec((D_MODEL, width), layer, lambda i, j, q: (q, 0, 0)))


def _w_in_grad(h, dproj, width, name):
    s = h.shape[0]
    tokens = 2048 if width <= 1536 else 1024
    return _mm(h, dproj, "tn", CDT, name, (D_MODEL, N_CHIPS * width, s), (D_MODEL, width, tokens),
               o_spec=pl.BlockSpec((None, D_MODEL, width), lambda i, j, q: (j, 0, 0)), o_shape=(N_CHIPS, D_MODEL, width))


def _w_out_spec(gw, layer):
    rows = gw.shape[-2]
    if layer is None:
        return pl.BlockSpec((N_CHIPS, rows, D_MODEL), lambda i: (0, 0, 0))
    return pl.BlockSpec((N_CHIPS, None, rows, D_MODEL), lambda i: (0, layer, 0, 0))


def _w_out_fwd(y, gw, layer, x, name, tm=512):
    s, k = y.shape

    def body(y_ref, w_ref, x_ref, o_ref):
        o_ref[...] = x_ref[...] + _dot(y_ref[...], w_ref[...].reshape(k, D_MODEL))

    blk = pl.BlockSpec((tm, D_MODEL), lambda i: (i, 0))
    return pl.pallas_call(
        body, name=name, grid=(s // tm,), in_specs=[pl.BlockSpec((tm, k), lambda i: (i, 0)), _w_out_spec(gw, layer), blk],
        out_specs=blk, out_shape=jax.ShapeDtypeStruct((s, D_MODEL), F32), compiler_params=_cparams("parallel"),
    )(y, gw, x)


def _w_out_dy(dxc, gw, layer, name, tm=1024):
    s, k = dxc.shape[0], N_CHIPS * gw.shape[-2]
    tm = min(tm, s)

    def body(dx_ref, w_ref, o_ref):
        o_ref[...] = _dot(dx_ref[...], w_ref[...].reshape(k, D_MODEL), NT).astype(o_ref.dtype)

    return pl.pallas_call(
        body, name=name, grid=(s // tm,), in_specs=[pl.BlockSpec((tm, D_MODEL), lambda i: (i, 0)), _w_out_spec(gw, layer)],
        out_specs=pl.BlockSpec((tm, k), lambda i: (i, 0)), out_shape=jax.ShapeDtypeStruct((s, k), CDT),
        compiler_params=_cparams("parallel"),
    )(dxc, gw)


def _w_out_grad(y, dxc, rows, name, tokens=1024):
    s, k = y.shape
    tokens = min(tokens, s)
    steps = s // tokens

    def body(y_ref, dx_ref, o_ref, acc_ref):
        i = pl.program_id(0)
        p = _dot(y_ref[...], dx_ref[...], TN)

        @pl.when(i == 0)
        def _():
            acc_ref[...] = p

        @pl.when(i > 0)
        def _():
            acc_ref[...] += p

        @pl.when(i == steps - 1)
        def _():
            o_ref[...] = acc_ref[...].reshape(N_CHIPS, rows, D_MODEL).astype(o_ref.dtype)

    return pl.pallas_call(
        body, name=name, grid=(steps,),
        in_specs=[pl.BlockSpec((tokens, k), lambda i: (i, 0)), pl.BlockSpec((tokens, D_MODEL), lambda i: (i, 0))],
        out_specs=pl.BlockSpec((N_CHIPS, rows, D_MODEL), lambda i: (0, 0, 0)),
        out_shape=jax.ShapeDtypeStruct((N_CHIPS, rows, D_MODEL), CDT),
        scratch_shapes=[pltpu.VMEM((k, D_MODEL), F32)], compiler_params=_cparams("arbitrary"),
    )(y, dxc)


def _local_step(x, target, w, on_grads=None, layer_weights=None):
    s = x.shape[0]
    tabs = _rope_tables(s)
    qk_gains = jnp.concatenate([w["b_q_gain"][0], w["b_k_gain"][0]], axis=0)
    saved = []
    for li, (kind, j) in enumerate(LAYERS):
        if layer_weights is not None:
            w = {**w, **layer_weights(li, x)}
        h = _rms_fwd(x, w["norm_gain"][li:li + 1], f"rms_fwd{li}")
        if kind == "a":
            proj = _w_in_fwd(h, w[f"a_w_in{j}"], None, f"a_in{li}")
            bs_t = w["a_b_s"][j].T
            y = _a_mid_fwd(proj, w["a_v_gain"][j:j + 1], w["a_w_s"][j], bs_t, f"a_mid_fwd{li}")
            x_next = _w_out_fwd(y, w[f"a_w_out{j}"], None, x, f"a_out{li}")
            saved.append((x, h, proj, y))
        elif kind == "b":
            proj = _w_in_fwd(h, w["b_w_in"], None, f"b_in{li}")
            qk = _b_qk_fwd(proj, qk_gains, tabs, f"b_qk_fwd{li}")
            qkv, os_, lses = [], [], []
            for g, dil in enumerate(B_DILATIONS):
                if dil == 1:
                    ops = (qk, qk, proj, (g, 3 + g, 6 + g))
                else:
                    ops = (_to_view(qk, g, 1024, dil, f"b_q_view{li}_{g}"), _to_view(qk, 3 + g, 1024, dil, f"b_k_view{li}_{g}"),
                           _to_view(proj, 6 + g, 1024, dil, f"b_v_view{li}_{g}"), (0, 0, 0))
                o, lse = _b_attn_fwd(*ops, dil, f"b_attn_fwd{li}_{g}")
                if dil > 1:
                    o, lse = _from_view(o, dil, f"b_o_nat{li}_{g}"), _from_view(lse, dil, f"b_lse_nat{li}_{g}")
                qkv.append(ops)
                os_.append(o)
                lses.append(lse)
            y, oj, lj = _b_combine(os_, lses, proj, f"b_combine{li}")
            x_next = _w_out_fwd(y, w["b_w_out"], None, x, f"b_out{li}")
            saved.append((x, h, proj, y, qkv, oj, lj))
        else:
            proj = _w_in_fwd(h, w["c_w_in"], None, f"c_in{li}")
            y = _c_mid_fwd(proj, w["c_w_grp"], w["c_scale"][j:j + 1], f"c_mid_fwd{li}")
            x_next = _w_out_fwd(y, w["c_w_out"], None, x, f"c_out{li}")
            saved.append((x, h, proj, y))
        x = x_next

    dx, dxc, sq = _loss_bwd(x, target, "loss_bwd")
    grads = {"norm_gain": [None] * len(LAYERS), "a_w_in": [None, None], "a_v_gain": [None, None], "a_w_s": [None, None],
             "a_b_s": [None, None], "a_w_out": [None, None]}
    for li in reversed(range(len(LAYERS))):
        kind, j = LAYERS[li]
        sv = saved[li]
        xin, h, proj, y = sv[:4]
        if kind == "a":
            grads["a_w_out"][j] = _w_out_grad(y, dxc, 512, f"a_dwout{li}")
            dy = _w_out_dy(dxc, w[f"a_w_out{j}"], None, f"a_dy{li}")
            dproj, dws, dbs_t, dvg = _a_mid_bwd(proj, dy, w["a_v_gain"][j:j + 1], w["a_w_s"][j], w["a_b_s"][j].T, f"a_mid_bwd{li}")
            grads["a_w_s"][j], grads["a_b_s"][j], grads["a_v_gain"][j] = dws, dbs_t.T, dvg[0]
            grads["a_w_in"][j] = _w_in_grad(h, dproj, 1536, f"a_dwin{li}")
            dh = _w_in_dh(dproj, w[f"a_w_in{j}"], None, f"a_dh{li}")
        elif kind == "b":
            qkv, oj, lj = sv[4:]
            grads["b_w_out"] = _w_out_grad(y, dxc, 256, f"b_dwout{li}")
            dy = _w_out_dy(dxc, w["b_w_out"], None, f"b_dy{li}")
            do, dz, dd = _b_bwd_prep(dy, oj, proj, f"b_bwd_prep{li}")
            dqkv = []
            for g, dil in enumerate(B_DILATIONS):
                stats = (do, lj, dd)
                if dil > 1:
                    stats = (_to_view(do, 0, 1024, dil, f"b_do_view{li}_{g}"), _to_view(lj, 0, HEAD_DIM, dil, f"b_lj_view{li}_{g}"),
                             _to_view(dd, 0, HEAD_DIM, dil, f"b_dd_view{li}_{g}"))
                d = _b_attn_bwd(*qkv[g], *stats, dil, f"b_attn_bwd{li}_{g}")
                dqkv.append(_from_view(d, dil, f"b_dqkv_nat{li}_{g}") if dil > 1 else d)
            dproj, dgains = _b_qk_bwd(proj, dqkv, dz, qk_gains, tabs, f"b_qk_bwd{li}")
            grads["b_q_gain"], grads["b_k_gain"] = dgains[None, :3], dgains[None, 3:]
            grads["b_w_in"] = _w_in_grad(h, dproj, 2560, f"b_dwin{li}")
            dh = _w_in_dh(dproj, w["b_w_in"], None, f"b_dh{li}")
        else:
            grads["c_w_out"] = _w_out_grad(y, dxc, 512, f"c_dwout{li}")
            dy = _w_out_dy(dxc, w["c_w_out"], None, f"c_dy{li}")
            ddiff, dz, dwg, dsc = _c_mid_bwd(proj, w["c_w_grp"], w["c_scale"][j:j + 1], dy, f"c_mid_bwd{li}")
            grads["c_w_grp"], grads["c_scale"] = dwg, dsc
            dproj = _c_pool_bwd(ddiff, dz, f"c_pool_bwd{li}")
            grads["c_w_in"] = _w_in_grad(h, dproj, 1024, f"c_dwin{li}")
            dh = _w_in_dh(dproj, w["c_w_in"], None, f"c_dh{li}")
        gain = w["norm_gain"][li:li + 1]
        if on_grads is not None:
            gain = gain + on_grads(li, grads)
        dx, dxc, dng = _rms_bwd(xin, dh, dx, gain, f"rms_bwd{li}")
        grads["norm_gain"][li] = dng[0]
    for name in ("norm_gain", "a_v_gain", "a_w_s", "a_b_s"):
        grads[name] = jnp.stack(grads[name])
    return sq, dx, grads


ROW = 1024
REDUCE_UNITS = (("a_w_in", 0, 512, 1536), ("a_w_in", 1, 512, 1536), ("a_w_out", 0, 256, 1024), ("a_w_out", 1, 256, 1024),
                ("b_w_in", 0, 512, 2560), ("b_w_out", 0, 128, 1024), ("c_w_in", 0, 512, 1024), ("c_w_grp", 0, 256, 512),
                ("c_w_out", 0, 256, 1024))
GAIN_ROWS = 16
GAINS = len(REDUCE_UNITS)
EARLY_UNITS = {3: (1, 3), 2: (6, 7, 8), 1: (4, 5)}
FIRST_UNITS = (0, 2, GAINS)
SMALL_OF_HALF = ("a_v_gain", "c_scale")
SMALL_PAD = 16
REP = (("norm_gain", (4, 1024)), ("a_w_s", (2, 8, 128, 128)), ("a_b_s", (2, 8, 128)), ("b_q_gain", (1, 3, 128)), ("b_k_gain", (1, 3, 128)))
REP_CORE = 48
REP_PART = 2 * REP_CORE
SMALL_ROWS = SMALL_PAD + REP_CORE
BLOCK_SHAPES = {"a_w_in": (2, 1024, 1536), "a_v_gain": (2, 512), "a_w_out": (2, 512, 1024), "b_w_in": (1, 1024, 2560),
                "b_w_out": (1, 256, 1024), "c_w_in": (1, 1024, 1024), "c_w_grp": (1, 4, 128, 512), "c_scale": (1, 512),
                "c_w_out": (1, 512, 1024)}
WEIGHTS = ("norm_gain", "a_w_in", "a_v_gain", "a_w_s", "a_b_s", "a_w_out", "b_w_in", "b_q_gain", "b_k_gain", "b_w_out",
           "c_w_in", "c_w_grp", "c_scale", "c_w_out")


def _rows(a, rows):
    a = a.reshape(-1)
    return jnp.pad(a, (0, rows * ROW - a.shape[0])).reshape(rows, ROW)


def _small_unit(vecs, rep_part, dtype):
    halves = [jnp.concatenate([_rows(vecs[h].astype(dtype), SMALL_PAD), rep_part[h * REP_CORE:(h + 1) * REP_CORE].astype(dtype)])
              for h in range(2)]
    return jnp.stack(halves)


def _pack_rep(tree):
    return _rows(jnp.concatenate([tree[n].astype(F32).reshape(-1) for n, _ in REP]), N_CHIPS * REP_PART)


def _unpack_rep(slab):
    flat, out, off = slab.reshape(-1), {}, 0
    for n, shape in REP:
        size = math.prod(shape)
        out[n] = flat[off:off + size].reshape(shape)
        off += size
    return out


def _into_slot(x, first, rows, me, name):
    width = x.shape[1]
    tr = min(rows, 256)

    def body(me_ref, x_ref, o_ref):
        o_ref[...] = x_ref[...].astype(o_ref.dtype)

    return pl.pallas_call(
        body, name=name,
        grid_spec=pltpu.PrefetchScalarGridSpec(
            num_scalar_prefetch=1, grid=(rows // tr,), in_specs=[pl.BlockSpec((tr, width), lambda i, me_ref: (first // tr + i, 0))],
            out_specs=pl.BlockSpec((None, tr, width), lambda i, me_ref: (me_ref[0], i, 0))),
        out_shape=jax.ShapeDtypeStruct((N_CHIPS, rows, width), CDT), compiler_params=_cparams("parallel"),
    )(me.reshape(1), x)


def _gather_inputs(wts, me):
    units = []
    for u, (n, layer, r, w) in enumerate(REDUCE_UNITS):
        slot = _into_slot(wts[n].reshape(-1, w), layer * 2 * r, 2 * r, me, f"slot{u}")
        units.append(slot.reshape(N_CHIPS, 2, r, w))
    gains = jnp.concatenate([wts["a_v_gain"].reshape(-1), wts["c_scale"].reshape(-1)])
    gains = _rows(lax.bitcast_convert_type(gains, CDT), 2 * GAIN_ROWS)
    return units + [_into_slot(gains, 0, 2 * GAIN_ROWS, me, "slot_gains").reshape(N_CHIPS, 2, GAIN_ROWS, ROW)]


def _gathered_weights(indices, units):
    out = {}
    for u, arr in zip(indices, units):
        if u == GAINS:
            gains = lax.bitcast_convert_type(arr[:, 0, :3].reshape(N_CHIPS, 1536, 2), F32)
            out["a_v_gain"] = jnp.concatenate([gains[k, :1024].reshape(2, 512) for k in range(N_CHIPS)], axis=1)
            out["c_scale"] = jnp.concatenate([gains[k, 1024:].reshape(1, 512) for k in range(N_CHIPS)], axis=1)
            continue
        n, layer, r, w = REDUCE_UNITS[u]
        if n == "c_w_grp":
            out[n] = arr
        else:
            out[n + str(layer) if n in ("a_w_in", "a_w_out") else n] = arr.reshape(N_CHIPS, 2 * r, w)
    return out


def _place():
    x, y, c = lax.axis_index("x"), lax.axis_index("y"), lax.axis_index("c")
    chips = [(1 - x, y), (x, 1 - y), (1 - x, 1 - y)]
    return x, y, c, 2 * x + y, (x, y, 1 - c), chips


def _remote(src, dst, sems, j, to):
    send_sems, recv_sems = sems
    return pltpu.make_async_remote_copy(src_ref=src, dst_ref=dst, send_sem=send_sems.at[j], recv_sem=recv_sems.at[j],
                                        device_id=to, device_id_type=MESH)


def _comm_call(body, name, out_shape, n_sems, *args):
    return pl.pallas_call(
        body, name=name, in_specs=[ANY] * len(args), out_specs=ANY, out_shape=out_shape,
        scratch_shapes=[pltpu.SemaphoreType.DMA((n_sems,)), pltpu.SemaphoreType.DMA((n_sems,))],
    )(*args)


def _comm_call_multi(body, name, out_shapes, n_sems, args, aliases=None):
    return pl.pallas_call(
        body, name=name, in_specs=[ANY] * len(args), out_specs=[ANY] * len(out_shapes), out_shape=out_shapes,
        scratch_shapes=[pltpu.SemaphoreType.DMA((n_sems,)), pltpu.SemaphoreType.DMA((n_sems,))],
        input_output_aliases=aliases or {},
    )(*args)


def _allgather_units(units):
    n = len(units)

    def body(*refs):
        outs, sems = refs[n:2 * n], (refs[2 * n], refs[2 * n + 1])
        x, y, c, me, sibling, chips = _place()
        first, passed = [], []
        for u, o_ref in enumerate(outs):
            for j, chip in enumerate(chips):
                first.append(_remote(o_ref.at[me, c], o_ref.at[me, c], sems, 6 * u + j, (*chip, c)))
                first[-1].start()
        for u, o_ref in enumerate(outs):
            for j, (cx, cy) in enumerate(chips):
                landed = o_ref.at[2 * cx + cy, c]
                _remote(landed, landed, sems, 6 * u + j, sibling).wait_recv()
                passed.append(_remote(landed, landed, sems, 6 * u + 3 + j, sibling))
                passed[-1].start()
        for u, o_ref in enumerate(outs):
            for j, (cx, cy) in enumerate(chips):
                landed = o_ref.at[2 * cx + cy, 1 - c]
                _remote(landed, landed, sems, 6 * u + 3 + j, sibling).wait_recv()
        for cp in first + passed:
            cp.wait_send()

    return _comm_call_multi(body, "allgather_first", [jax.ShapeDtypeStruct(u.shape, u.dtype) for u in units], 6 * n, units,
                            aliases={u: u for u in range(n)})


HBM = pl.BlockSpec(memory_space=pltpu.HBM)
SEM = pl.BlockSpec(memory_space=pltpu.SEMAPHORE)
SIDE_EFFECT = pltpu.SideEffectType.DATAFLOW_SIDE_EFFECTING


def _gather_start(groups, after):
    sizes = [len(g) for g in groups]
    units = [u for g in groups for u in g]
    n = len(units)

    def body(*refs):
        arrs, sems, token = refs[:n], refs[n + 1:n + 1 + 2 * len(groups)], refs[-1]
        x, y, c, me, sibling, chips = _place()
        at = 0
        for gi, size in enumerate(sizes):
            for u in range(size):
                mine = arrs[at + u].at[me, c]
                for j, chip in enumerate(chips):
                    _remote(mine, mine, (sems[2 * gi], sems[2 * gi + 1]), 3 * u + j, (*chip, c)).start()
            at += size
        token[...] = jnp.zeros_like(token)

    sem_shapes = [pltpu.SemaphoreType.DMA((3 * size,)) for size in sizes for _ in range(2)]
    outs = pl.pallas_call(
        body, name="gather_start",
        out_shape=(*sem_shapes, *[pltpu.HBM(u.shape, u.dtype) for u in units], jax.ShapeDtypeStruct((8, 128), F32)),
        in_specs=[HBM] * n + [ANY], out_specs=(*[SEM] * len(sem_shapes), *[HBM] * n, pl.BlockSpec(memory_space=pltpu.VMEM)),
        input_output_aliases={i: len(sem_shapes) + i for i in range(n)},
        compiler_params=pltpu.CompilerParams(has_side_effects=SIDE_EFFECT),
    )(*[pltpu.with_memory_space_constraint(u, pltpu.HBM) for u in units], after)
    sems, arrs = outs[:len(sem_shapes)], outs[len(sem_shapes):-1]
    bounds = [sum(sizes[:gi]) for gi in range(len(sizes) + 1)]
    return ([(sems[2 * gi], sems[2 * gi + 1]) for gi in range(len(sizes))],
            [list(arrs[bounds[gi]:bounds[gi + 1]]) for gi in range(len(sizes))], outs[-1])


def _gather_wait(units, sems, after, name):
    n = len(units)

    def body(*refs):
        arrs, send_sems, recv_sems = refs[:n], refs[n], refs[n + 1]
        x, y, c, me, sibling, chips = _place()
        for u in range(n):
            for j, (cx, cy) in enumerate(chips):
                cp = _remote(arrs[u].at[me, c], arrs[u].at[2 * cx + cy, c], (send_sems, recv_sems), 3 * u + j, (cx, cy, c))
                cp.wait_send()
                cp.wait_recv()

    outs = pl.pallas_call(
        body, name=name, out_shape=tuple(pltpu.HBM(u.shape, u.dtype) for u in units),
        in_specs=[HBM] * n + [SEM, SEM, ANY], out_specs=[HBM] * n, input_output_aliases={i: i for i in range(n)},
        compiler_params=pltpu.CompilerParams(has_side_effects=SIDE_EFFECT),
    )(*units, *sems, after)
    return list(outs)


def _forward_units(units, name):
    n = len(units)

    def body(*refs):
        outs, sems = refs[n:2 * n], (refs[2 * n], refs[2 * n + 1])
        x, y, c, me, sibling, chips = _place()
        passed = []
        for u, o_ref in enumerate(outs):
            for j, (cx, cy) in enumerate(chips):
                landed = o_ref.at[2 * cx + cy, c]
                passed.append(_remote(landed, landed, sems, 3 * u + j, sibling))
                passed[-1].start()
        for u, o_ref in enumerate(outs):
            for j, (cx, cy) in enumerate(chips):
                landed = o_ref.at[2 * cx + cy, 1 - c]
                _remote(landed, landed, sems, 3 * u + j, sibling).wait_recv()
        for cp in passed:
            cp.wait_send()

    return _comm_call_multi(body, name, [jax.ShapeDtypeStruct(u.shape, u.dtype) for u in units], 3 * n, units,
                            aliases={u: u for u in range(n)})


def _swap_halves(units, name):
    n = len(units)

    def body(*refs):
        ins, outs, sems = refs[:n], refs[n:2 * n], (refs[2 * n], refs[2 * n + 1])
        x, y, c, me, sibling, chips = _place()
        sent = [_remote(g_ref.at[:, 1 - c], o_ref, sems, u, sibling) for u, (g_ref, o_ref) in enumerate(zip(ins, outs))]
        for cp in sent:
            cp.start()
        for cp in sent:
            cp.wait()

    shapes = [jax.ShapeDtypeStruct((N_CHIPS,) + u.shape[2:], u.dtype) for u in units]
    return _comm_call_multi(body, name, shapes, n, units)


def _scatter_shards(units):
    n = len(units)

    def body(*refs):
        ins, outs, sems = refs[:n], refs[n:2 * n], (refs[2 * n], refs[2 * n + 1])
        x, y, c, me, sibling, chips = _place()
        sent = []
        for u, (s_ref, o_ref) in enumerate(zip(ins, outs)):
            for j, (cx, cy) in enumerate(chips):
                sent.append(_remote(s_ref.at[2 * cx + cy], o_ref.at[me], sems, 3 * u + j, (cx, cy, c)))
                sent[-1].start()
        for u, o_ref in enumerate(outs):
            for j, (cx, cy) in enumerate(chips):
                slot = o_ref.at[2 * cx + cy]
                _remote(slot, slot, sems, 3 * u + j, sibling).wait_recv()
        for cp in sent:
            cp.wait_send()

    return _comm_call_multi(body, "grad_scatter_shards", [jax.ShapeDtypeStruct(u.shape, u.dtype) for u in units], 3 * n, units)


def _scatter_start(units, name):
    n = len(units)

    def body(*refs):
        srcs, lands = refs[:n], refs[n:2 * n]
        send_sems, recv_sems, token = refs[2 * n], refs[2 * n + 1], refs[-1]
        x, y, c, me, sibling, chips = _place()
        for u in range(n):
            for j, (cx, cy) in enumerate(chips):
                _remote(srcs[u].at[2 * cx + cy], lands[u].at[me], (send_sems, recv_sems), 3 * u + j, (cx, cy, c)).start()
        token[...] = jnp.zeros_like(token)

    hbm = [pltpu.HBM(u.shape, u.dtype) for u in units]
    outs = pl.pallas_call(
        body, name=name,
        out_shape=(pltpu.SemaphoreType.DMA((3 * n,)), pltpu.SemaphoreType.DMA((3 * n,)), *hbm, *hbm, jax.ShapeDtypeStruct((8, 128), F32)),
        in_specs=[HBM] * (2 * n), out_specs=(SEM, SEM, *[HBM] * (2 * n), pl.BlockSpec(memory_space=pltpu.VMEM)),
        input_output_aliases={i: 2 + i for i in range(2 * n)},
        compiler_params=pltpu.CompilerParams(has_side_effects=SIDE_EFFECT),
    )(*[pltpu.with_memory_space_constraint(u, pltpu.HBM) for u in units],
      *[pltpu.with_memory_space_constraint(lax.empty(u.shape, u.dtype), pltpu.HBM) for u in units])
    return outs[0], outs[1], outs[2:2 + n], outs[2 + n:2 + 2 * n], outs[-1]


def _scatter_wait(send_sems, recv_sems, srcs, lands, after, name):
    n = len(srcs)

    def body(*refs):
        srcs_, lands_, send_sems_, recv_sems_ = refs[:n], refs[n:2 * n], refs[2 * n], refs[2 * n + 1]
        x, y, c, me, sibling, chips = _place()
        for u in range(n):
            for j, (cx, cy) in enumerate(chips):
                slot = lands_[u].at[2 * cx + cy]
                cp = _remote(srcs_[u].at[2 * cx + cy], slot, (send_sems_, recv_sems_), 3 * u + j, (cx, cy, c))
                cp.wait_send()
                cp.wait_recv()

    hbm = [pltpu.HBM(u.shape, u.dtype) for u in srcs]
    outs = pl.pallas_call(
        body, name=name, out_shape=(*hbm, *hbm),
        in_specs=[HBM] * (2 * n) + [SEM, SEM, ANY], out_specs=[HBM] * (2 * n),
        input_output_aliases={i: i for i in range(2 * n)},
        compiler_params=pltpu.CompilerParams(has_side_effects=SIDE_EFFECT),
    )(*srcs, *lands, send_sems, recv_sems, after)
    return list(outs[n:])


def _share_halves(units):
    n = len(units)

    def body(*refs):
        outs, sems = refs[n:2 * n], (refs[2 * n], refs[2 * n + 1])
        x, y, c, me, sibling, chips = _place()
        sent = [_remote(o_ref.at[:, c], o_ref.at[:, c], sems, u, sibling) for u, o_ref in enumerate(outs)]
        for cp in sent:
            cp.start()
        for u, o_ref in enumerate(outs):
            theirs = o_ref.at[:, 1 - c]
            _remote(theirs, theirs, sems, u, sibling).wait_recv()
        for cp in sent:
            cp.wait_send()

    return _comm_call_multi(body, "share_halves", [jax.ShapeDtypeStruct(u.shape, u.dtype) for u in units], n, units,
                            aliases={u: u for u in range(n)})


def _allgather_rep(rep4):
    def body(r_ref, o_ref, send_sems, recv_sems):
        x, y, c, me, sibling, chips = _place()
        sems = (send_sems, recv_sems)
        sent = [_remote(r_ref, o_ref.at[me], sems, j, (*chip, c)) for j, chip in enumerate(chips)]
        for cp in sent:
            cp.start()
        for j, (cx, cy) in enumerate(chips):
            slot = o_ref.at[2 * cx + cy]
            _remote(slot, slot, sems, j, sibling).wait_recv()
        for cp in sent:
            cp.wait_send()

    return _comm_call(body, "allgather_rep", jax.ShapeDtypeStruct((N_CHIPS,) + rep4.shape, rep4.dtype), 3, rep4)


ADAM_TILE = 64


def _add_halves(mine, theirs, place, name):
    rows, width = theirs.shape[1:]
    tr = min(rows, 256)

    def body(place_ref, a_ref, b_ref, o_ref):
        o_ref[...] = (a_ref[...].astype(F32) + b_ref[...].astype(F32)).astype(o_ref.dtype)

    blk = pl.BlockSpec((None, tr, width), lambda k, i, place_ref: (k, i, 0))
    return pl.pallas_call(
        body, name=name,
        grid_spec=pltpu.PrefetchScalarGridSpec(
            num_scalar_prefetch=1, grid=(N_CHIPS, rows // tr),
            in_specs=[pl.BlockSpec((None, None, tr, width), lambda k, i, place_ref: (k, place_ref[1], i, 0)), blk], out_specs=blk),
        out_shape=jax.ShapeDtypeStruct(theirs.shape, theirs.dtype), compiler_params=_cparams("parallel", "parallel"),
    )(place, mine, theirs)


def _adamw(place, parts, own, w, m, v, layer, name):
    rows, width = parts.shape[1:]

    def body(place_ref, p_ref, own_ref, w_ref, m_ref, v_ref, o_ref):
        g = jnp.zeros((ADAM_TILE, width), F32)
        for k in range(N_CHIPS):
            g = g + jnp.where(place_ref[0] == k, own_ref[...], p_ref[k]).astype(F32)
        m2 = ADAM_B1 * m_ref[...] + (1.0 - ADAM_B1) * g
        v2 = ADAM_B2 * v_ref[...] + (1.0 - ADAM_B2) * jnp.square(g)
        m_hat = m2 / (1.0 - ADAM_B1 ** ADAM_STEP)
        v_hat = v2 / (1.0 - ADAM_B2 ** ADAM_STEP)
        o_ref[0] = g
        o_ref[1] = -ADAM_LR * (m_hat / (jnp.sqrt(v_hat) + ADAM_EPS) + ADAM_WD * w_ref[...])
        o_ref[2] = m2
        o_ref[3] = v2

    blk = pl.BlockSpec((None, None, ADAM_TILE, width), lambda i, place_ref: (layer, place_ref[1], i, 0))
    blk4 = pl.BlockSpec((4, ADAM_TILE, width), lambda i, place_ref: (0, i, 0))
    mine = pl.BlockSpec((None, ADAM_TILE, width), lambda i, place_ref: (place_ref[0], i, 0))
    out = pl.BlockSpec((4, None, ADAM_TILE, width), lambda i, place_ref: (0, place_ref[1], i, 0))
    return pl.pallas_call(
        body, name=name,
        grid_spec=pltpu.PrefetchScalarGridSpec(num_scalar_prefetch=1, grid=(rows // ADAM_TILE,),
                                               in_specs=[blk4, mine, blk, blk, blk], out_specs=out),
        out_shape=jax.ShapeDtypeStruct((4, 2, rows, width), F32), compiler_params=_cparams("parallel"),
    )(place, parts, own, w, m, v)


def kernel(x, norm_gain, a_w_in, a_v_gain, a_w_s, a_b_s, a_w_out, b_w_in, b_q_gain, b_k_gain, b_w_out, c_w_in, c_w_grp, c_scale, c_w_out, loss_target, m_norm_gain, m_a_w_in, m_a_v_gain, m_a_w_s, m_a_b_s, m_a_w_out, m_b_w_in, m_b_q_gain, m_b_k_gain, m_b_w_out, m_c_w_in, m_c_w_grp, m_c_scale, m_c_w_out, v_norm_gain, v_a_w_in, v_a_v_gain, v_a_w_s, v_a_b_s, v_a_w_out, v_b_w_in, v_b_q_gain, v_b_k_gain, v_b_w_out, v_c_w_in, v_c_w_grp, v_c_scale, v_c_w_out):
    wts = dict(norm_gain=norm_gain, a_w_in=a_w_in, a_v_gain=a_v_gain, a_w_s=a_w_s, a_b_s=a_b_s, a_w_out=a_w_out, b_w_in=b_w_in,
               b_q_gain=b_q_gain, b_k_gain=b_k_gain, b_w_out=b_w_out, c_w_in=c_w_in, c_w_grp=c_w_grp, c_scale=c_scale, c_w_out=c_w_out)
    mom1 = dict(norm_gain=m_norm_gain, a_w_in=m_a_w_in, a_v_gain=m_a_v_gain, a_w_s=m_a_w_s, a_b_s=m_a_b_s, a_w_out=m_a_w_out,
                b_w_in=m_b_w_in, b_q_gain=m_b_q_gain, b_k_gain=m_b_k_gain, b_w_out=m_b_w_out, c_w_in=m_c_w_in, c_w_grp=m_c_w_grp,
                c_scale=m_c_scale, c_w_out=m_c_w_out)
    mom2 = dict(norm_gain=v_norm_gain, a_w_in=v_a_w_in, a_v_gain=v_a_v_gain, a_w_s=v_a_w_s, a_b_s=v_a_b_s, a_w_out=v_a_w_out,
                b_w_in=v_b_w_in, b_q_gain=v_b_q_gain, b_k_gain=v_b_k_gain, b_w_out=v_b_w_out, c_w_in=v_c_w_in, c_w_grp=v_c_w_grp,
                c_scale=v_c_scale, c_w_out=v_c_w_out)
    axes = ("x", "y", "c")
    me = 2 * lax.axis_index("x") + lax.axis_index("y")
    core = lax.axis_index("c")

    place = jnp.stack([me, core]).astype(jnp.int32)

    slots = _gather_inputs(wts, me)
    first = _allgather_units([slots[u] for u in FIRST_UNITS])
    full = _gathered_weights(FIRST_UNITS, first)
    later = sorted(EARLY_UNITS)
    gather_sems, gather_units, token = _gather_start([[slots[u] for u in EARLY_UNITS[li]] for li in later], first[-1])
    for n, _ in REP:
        full[n] = wts[n]
    full["norm_gain"] = wts["norm_gain"] + token[0, 0]

    def layer_weights(li, x_in):
        if li not in EARLY_UNITS:
            return {}
        g = later.index(li)
        landed = _gather_wait(gather_units[g], gather_sems[g], x_in, f"gather_wait_l{li}")
        return _gathered_weights(EARLY_UNITS[li], _forward_units(landed, f"gather_forward_l{li}"))

    def unit_grad(u, grads):
        n, layer, r, w = REDUCE_UNITS[u]
        g = grads[n][layer] if n in ("a_w_in", "a_w_out") else grads[n]
        return g.astype(CDT).reshape(N_CHIPS, 2, r, w)

    chip_sums, in_flight = {}, []

    def start_exchange(li, grads):
        if li not in EARLY_UNITS:
            return jnp.zeros((), F32)
        units = EARLY_UNITS[li]
        mine = [unit_grad(u, grads) for u in units]
        theirs = _swap_halves(mine, f"grad_swap_halves_l{li}")
        sums = [_add_halves(g, t, place, f"grad_add_halves{u}") for u, g, t in zip(units, mine, theirs)]
        chip_sums.update(zip(units, sums))
        send_sems, recv_sems, srcs, lands, token = _scatter_start(sums, f"grad_scatter_start_l{li}")
        in_flight.append((li, units, send_sems, recv_sems, srcs, lands))
        return token[0, 0]

    sq, grad_x, grads = _local_step(x[0], loss_target[0], full, start_exchange, layer_weights)
    loss = lax.psum(0.5 * jnp.sum(sq) / D_MODEL, axes)

    rep_g = _pack_rep(grads)
    late = [u for u in range(len(REDUCE_UNITS)) if not any(u in us for us in EARLY_UNITS.values())]
    mine = [unit_grad(u, grads) for u in late]
    mine.append(jnp.stack([
        _small_unit([lax.slice_in_dim(grads[v], 512 * k, 512 * (k + 1), axis=1) for v in SMALL_OF_HALF],
                    rep_g[k * REP_PART:(k + 1) * REP_PART], CDT) for k in range(N_CHIPS)]))
    late.append(len(REDUCE_UNITS))
    theirs = _swap_halves(mine, "grad_swap_halves_late")
    sums = [_add_halves(g, t, place, f"grad_add_halves{u}") for u, g, t in zip(late, mine, theirs)]
    chip_sums.update(zip(late, sums))
    parts = dict(zip(late, _scatter_shards(sums)))
    for li, units, send_sems, recv_sems, srcs, lands in in_flight:
        parts.update(zip(units, _scatter_wait(send_sems, recv_sems, srcs, lands, grad_x, f"grad_scatter_wait_l{li}")))
    order = range(len(REDUCE_UNITS) + 1)
    parts, chip_sums = [parts[u] for u in order], [chip_sums[u] for u in order]

    def state_units(tree):
        units = [tree[n].reshape(BLOCK_SHAPES[n][0], 2, r, w) for n, _, r, w in REDUCE_UNITS]
        rep_part = lax.dynamic_slice_in_dim(_pack_rep(tree), me * REP_PART, REP_PART, axis=0)
        return units + [_small_unit([tree[v] for v in SMALL_OF_HALF], rep_part, F32)[None]]

    layers = [layer for _, layer, _, _ in REDUCE_UNITS] + [0]
    res = _share_halves([_adamw(place, p, s, w_, m_, v_, layer, f"adamw{u}") for u, (p, s, w_, m_, v_, layer) in enumerate(
        zip(parts, chip_sums, state_units(wts), state_units(mom1), state_units(mom2), layers))])
    small = res[-1]
    rep_mine = jnp.concatenate([small[:, 0, SMALL_PAD:], small[:, 1, SMALL_PAD:]], axis=1)
    rep_all = lax.dynamic_update_slice_in_dim(_allgather_rep(rep_mine), rep_mine[None], me, axis=0)

    outs = []
    for q in range(4):
        tree = {}
        for (n, layer, r, w), u in zip(REDUCE_UNITS, res):
            tree.setdefault(n, []).append(u[q].reshape(BLOCK_SHAPES[n][1:]))
        tree = {n: jnp.stack(v) for n, v in tree.items()}
        tree["a_v_gain"] = small[q, 0, 0].reshape(2, 512)
        tree["c_scale"] = small[q, 1, 0, :512].reshape(1, 512)
        tree.update(_unpack_rep(rep_all[:, q].reshape(N_CHIPS * REP_PART, ROW)))
        outs.append(tree)
    return (loss, grad_x[None], *[t[n] for t in outs for n in WEIGHTS])
```

```python
import functools
import math

import jax
import jax.numpy as jnp
from jax import lax
from jax.experimental import pallas as pl
from jax.experimental.pallas import tpu as pltpu

F32 = jnp.float32
CDT = jnp.bfloat16

D_MODEL = 1024
EPS = 1e-6
CHUNK = 128
A_WIDTH = 2048
A_GROUPS = 8
A_GROUP_DIM = 256
HEAD_DIM = 128
B_HEADS = 8
B_DILATIONS = (1, 4, 16)
B_QK = 6144
B_IN = 10240
ROPE_HALF = 16
ROPE_THETA = 500000.0
POOL_SIZES = (2, 4, 8, 16)
POOL_HALO = 16
C_WIDTH = 2048
C_GROUP = 512
N_CHIPS = 4

ADAM_LR = 0.001
ADAM_B1 = 0.9
ADAM_B2 = 0.999
ADAM_EPS = 1e-08
ADAM_WD = 0.01
ADAM_STEP = 10

VMEM_LIMIT = 48 * 1024 * 1024
ANY = pl.BlockSpec(memory_space=pl.ANY)
MESH = pl.DeviceIdType.MESH

NN = (((1,), (0,)), ((), ()))
NT = (((1,), (1,)), ((), ()))
TN = (((0,), (0,)), ((), ()))


def _cparams(*sem):
    return pltpu.CompilerParams(dimension_semantics=sem, vmem_limit_bytes=VMEM_LIMIT)


def _dot(a, b, dims=NN):
    return lax.dot_general(a, b, dims, preferred_element_type=F32)


def _sigmoid(z):
    return 1.0 / (1.0 + jnp.exp(-z))


def _lane_sums(v):
    ones = jnp.ones((HEAD_DIM, HEAD_DIM), jnp.bfloat16)
    hi = v.astype(jnp.bfloat16)
    lo = (v - hi.astype(F32)).astype(jnp.bfloat16)
    return _dot(hi, ones) + _dot(lo, ones)


def _mm(a, b, mode, out_dtype, name, mnk, tiles, b_spec=None, o_spec=None, o_shape=None, add=None):
    m, n, k = mnk
    tm, tn, tk = min(tiles[0], m), min(tiles[1], n), min(tiles[2], k)
    nk = k // tk
    a_spec = {"nn": pl.BlockSpec((tm, tk), lambda i, j, q: (i, q)),
              "nt": pl.BlockSpec((tm, tk), lambda i, j, q: (i, q)),
              "tn": pl.BlockSpec((tk, tm), lambda i, j, q: (q, i))}[mode]
    if b_spec is None:
        b_spec = {"nn": pl.BlockSpec((tk, tn), lambda i, j, q: (q, j)),
                  "nt": pl.BlockSpec((tn, tk), lambda i, j, q: (j, q)),
                  "tn": pl.BlockSpec((tk, tn), lambda i, j, q: (q, j))}[mode]
    if o_spec is None:
        o_spec, o_shape = pl.BlockSpec((tm, tn), lambda i, j, q: (i, j)), (m, n)
    dims = {"nn": NN, "nt": NT, "tn": TN}[mode]
    has_add = add is not None

    def body(*refs):
        a_ref, b_ref = refs[0], refs[1]
        o_ref = refs[3] if has_add else refs[2]
        p = _dot(a_ref[...], b_ref[...], dims)

        def finish(v):
            if has_add:
                v = v + refs[2][...]
            o_ref[...] = v.astype(o_ref.dtype)

        if nk == 1:
            finish(p)
        else:
            acc_ref = refs[-1]
            q = pl.program_id(2)

            @pl.when(q == 0)
            def _():
                acc_ref[...] = p

            @pl.when(q > 0)
            def _():
                acc_ref[...] += p

            @pl.when(q == nk - 1)
            def _():
                finish(acc_ref[...])

    in_specs = [a_spec, b_spec]
    args = [a, b]
    if has_add:
        in_specs.append(pl.BlockSpec((tm, tn), lambda i, j, q: (i, j)))
        args.append(add)
    return pl.pallas_call(
        body, name=name, grid=(m // tm, n // tn, nk), in_specs=in_specs, out_specs=o_spec,
        out_shape=jax.ShapeDtypeStruct(o_shape, out_dtype),
        scratch_shapes=[pltpu.VMEM((tm, tn), F32)] if nk > 1 else [],
        compiler_params=_cparams("parallel", "parallel", "arbitrary"),
    )(*args)


def _rms_fwd(x, g, name, tq=512):
    s, d = x.shape

    def body(x_ref, g_ref, h_ref):
        xv = x_ref[...]
        r = lax.rsqrt(jnp.mean(xv * xv, axis=-1, keepdims=True) + EPS)
        h_ref[...] = (xv * r * g_ref[...]).astype(h_ref.dtype)

    return pl.pallas_call(
        body, name=name, grid=(s // tq,),
        in_specs=[pl.BlockSpec((tq, d), lambda i: (i, 0)), pl.BlockSpec((1, d), lambda i: (0, 0))],
        out_specs=pl.BlockSpec((tq, d), lambda i: (i, 0)),
        out_shape=jax.ShapeDtypeStruct((s, d), CDT), compiler_params=_cparams("parallel"),
    )(x, g)


def _loss_bwd(y, target, name, tq=512):
    s, d = y.shape

    def body(y_ref, t_ref, dx_ref, dxc_ref, sq_ref):
        err = y_ref[...] - t_ref[...]
        dx = err * (1.0 / d)
        dx_ref[...] = dx
        dxc_ref[...] = dx.astype(dxc_ref.dtype)

        @pl.when(pl.program_id(0) == 0)
        def _():
            sq_ref[...] = jnp.zeros_like(sq_ref)

        sq_ref[...] += jnp.sum(err * err, axis=0, keepdims=True)

    blk = pl.BlockSpec((tq, d), lambda i: (i, 0))
    vec = pl.BlockSpec((1, d), lambda i: (0, 0))
    return pl.pallas_call(
        body, name=name, grid=(s // tq,), in_specs=[blk, blk], out_specs=[blk, blk, vec],
        out_shape=[jax.ShapeDtypeStruct((s, d), F32), jax.ShapeDtypeStruct((s, d), CDT), jax.ShapeDtypeStruct((1, d), F32)],
        compiler_params=_cparams("arbitrary"),
    )(y, target)


def _tril_mask():
    row = lax.broadcasted_iota(jnp.int32, (CHUNK, CHUNK), 0)
    col = lax.broadcasted_iota(jnp.int32, (CHUNK, CHUNK), 1)
    return row >= col


def _a_mid_fwd(proj, v_gain, w_s, b_s_t, name, tq=256):
    s = proj.shape[0]

    def body(p_ref, vg_ref, ws_ref, bs_ref, y_ref):
        vraw = p_ref[:, A_WIDTH:2 * A_WIDTH].astype(F32)
        r = lax.rsqrt(jnp.mean(vraw * vraw, axis=-1, keepdims=True) + EPS)
        vn = (vraw * r * vg_ref[...]).astype(CDT)
        tri = _tril_mask()
        for g in range(A_GROUPS):
            w = jnp.where(tri, ws_ref[g], 0.0).astype(CDT)
            bias = bs_ref[:, g:g + 1]
            cols = slice(g * A_GROUP_DIM, (g + 1) * A_GROUP_DIM)
            zcols = slice(2 * A_WIDTH + g * A_GROUP_DIM, 2 * A_WIDTH + (g + 1) * A_GROUP_DIM)
            for c in range(tq // CHUNK):
                rows = slice(c * CHUNK, (c + 1) * CHUNK)
                mixed = _dot(w, vn[rows, cols]) + bias
                u = p_ref[rows, cols].astype(F32)
                z = p_ref[rows, zcols].astype(F32)
                y_ref[rows, cols] = (u * mixed * (z * _sigmoid(z))).astype(y_ref.dtype)

    return pl.pallas_call(
        body, name=name, grid=(s // tq,),
        in_specs=[pl.BlockSpec((tq, 3 * A_WIDTH), lambda i: (i, 0)), pl.BlockSpec((1, A_WIDTH), lambda i: (0, 0)),
                  pl.BlockSpec((A_GROUPS, CHUNK, CHUNK), lambda i: (0, 0, 0)), pl.BlockSpec((CHUNK, A_GROUPS), lambda i: (0, 0))],
        out_specs=pl.BlockSpec((tq, A_WIDTH), lambda i: (i, 0)),
        out_shape=jax.ShapeDtypeStruct((s, A_WIDTH), CDT), compiler_params=_cparams("parallel"),
    )(proj, v_gain, w_s, b_s_t)


def _a_mid_bwd(proj, dy, v_gain, w_s, b_s_t, name, tq=256):
    s = proj.shape[0]

    def body(p_ref, dy_ref, vg_ref, ws_ref, bs_ref, dp_ref, dws_ref, dbs_ref, dvg_ref, dvn_ref):
        @pl.when(pl.program_id(0) == 0)
        def _():
            dws_ref[...] = jnp.zeros_like(dws_ref)
            dbs_ref[...] = jnp.zeros_like(dbs_ref)
            dvg_ref[...] = jnp.zeros_like(dvg_ref)

        vraw = p_ref[:, A_WIDTH:2 * A_WIDTH].astype(F32)
        r = lax.rsqrt(jnp.mean(vraw * vraw, axis=-1, keepdims=True) + EPS)
        vhat = vraw * r
        vg = vg_ref[...]
        vn = (vhat * vg).astype(CDT)
        tri = _tril_mask()
        lane = lax.broadcasted_iota(jnp.int32, (CHUNK, A_GROUPS), 1)
        dbs = jnp.zeros((CHUNK, A_GROUPS), F32)
        for g in range(A_GROUPS):
            w = jnp.where(tri, ws_ref[g], 0.0).astype(CDT)
            bias = bs_ref[:, g:g + 1]
            cols = slice(g * A_GROUP_DIM, (g + 1) * A_GROUP_DIM)
            zcols = slice(2 * A_WIDTH + g * A_GROUP_DIM, 2 * A_WIDTH + (g + 1) * A_GROUP_DIM)
            dws = jnp.zeros((CHUNK, CHUNK), F32)
            for c in range(tq // CHUNK):
                rows = slice(c * CHUNK, (c + 1) * CHUNK)
                vn_g = vn[rows, cols]
                mixed = _dot(w, vn_g) + bias
                u = p_ref[rows, cols].astype(F32)
                z = p_ref[rows, zcols].astype(F32)
                dyv = dy_ref[rows, cols].astype(F32)
                sg = _sigmoid(z)
                sz = z * sg
                dyu = dyv * u
                dmixed = dyu * sz
                dp_ref[rows, cols] = (dyv * mixed * sz).astype(dp_ref.dtype)
                dp_ref[rows, zcols] = (dyu * mixed * (sg * (1.0 + z * (1.0 - sg)))).astype(dp_ref.dtype)
                dmc = dmixed.astype(CDT)
                dws = dws + _dot(dmc, vn_g, NT)
                dbs = dbs + jnp.where(lane == g, jnp.sum(dmixed, axis=-1, keepdims=True), 0.0)
                dvn_ref[rows, cols] = _dot(w, dmc, TN)
            dws_ref[g] += jnp.where(tri, dws, 0.0)
        dbs_ref[...] += dbs
        dvn = dvn_ref[...]
        gd = dvn * vg
        dvraw = r * gd - vraw * (r * r * r) * jnp.mean(gd * vraw, axis=-1, keepdims=True)
        dp_ref[:, A_WIDTH:2 * A_WIDTH] = dvraw.astype(dp_ref.dtype)
        dvg_ref[...] += jnp.sum(dvn * vhat, axis=0, keepdims=True)

    return pl.pallas_call(
        body, name=name, grid=(s // tq,),
        in_specs=[pl.BlockSpec((tq, 3 * A_WIDTH), lambda i: (i, 0)), pl.BlockSpec((tq, A_WIDTH), lambda i: (i, 0)),
                  pl.BlockSpec((1, A_WIDTH), lambda i: (0, 0)), pl.BlockSpec((A_GROUPS, CHUNK, CHUNK), lambda i: (0, 0, 0)),
                  pl.BlockSpec((CHUNK, A_GROUPS), lambda i: (0, 0))],
        out_specs=[pl.BlockSpec((tq, 3 * A_WIDTH), lambda i: (i, 0)), pl.BlockSpec((A_GROUPS, CHUNK, CHUNK), lambda i: (0, 0, 0)),
                   pl.BlockSpec((CHUNK, A_GROUPS), lambda i: (0, 0)), pl.BlockSpec((1, A_WIDTH), lambda i: (0, 0))],
        out_shape=[jax.ShapeDtypeStruct((s, 3 * A_WIDTH), CDT), jax.ShapeDtypeStruct((A_GROUPS, CHUNK, CHUNK), F32),
                   jax.ShapeDtypeStruct((CHUNK, A_GROUPS), F32), jax.ShapeDtypeStruct((1, A_WIDTH), F32)],
        scratch_shapes=[pltpu.VMEM((tq, A_WIDTH), F32)],
        compiler_params=_cparams("arbitrary"),
    )(proj, dy, v_gain, w_s, b_s_t)


def _rope_tables(s):
    inv_freq = jnp.power(jnp.float32(ROPE_THETA), -jnp.arange(ROPE_HALF, dtype=F32) / ROPE_HALF)
    ang = jnp.arange(s, dtype=F32)[:, None] * inv_freq[None, :]
    cos, sin = jnp.cos(ang), jnp.sin(ang)
    rest = HEAD_DIM - 2 * ROPE_HALF
    t_c = jnp.concatenate([cos, cos, jnp.ones((s, rest), F32)], axis=1)
    t_a = jnp.concatenate([-sin, jnp.zeros((s, HEAD_DIM - ROPE_HALF), F32)], axis=1)
    t_b = jnp.concatenate([jnp.zeros((s, ROPE_HALF), F32), sin, jnp.zeros((s, rest), F32)], axis=1)
    return t_c, t_a, t_b


def _b_qk_fwd(proj, gains, tabs, name, tq=256):
    s = proj.shape[0]

    def body(p_ref, g_ref, tc_ref, ta_ref, tb_ref, o_ref):
        tc, ta, tb = tc_ref[...], ta_ref[...], tb_ref[...]
        for tg in range(6):
            gain = g_ref[tg:tg + 1, :]
            for h in range(B_HEADS):
                cols = slice(tg * 1024 + h * HEAD_DIM, tg * 1024 + (h + 1) * HEAD_DIM)
                xv = p_ref[:, cols].astype(F32)
                r = lax.rsqrt(_lane_sums(xv * xv) * (1.0 / HEAD_DIM) + EPS)
                xn = xv * r * gain
                y = xn * tc + pltpu.roll(xn, HEAD_DIM - ROPE_HALF, 1) * ta + pltpu.roll(xn, ROPE_HALF, 1) * tb
                o_ref[:, cols] = y.astype(o_ref.dtype)

    tab = pl.BlockSpec((tq, HEAD_DIM), lambda i: (i, 0))
    return pl.pallas_call(
        body, name=name, grid=(s // tq,),
        in_specs=[pl.BlockSpec((tq, B_QK), lambda i: (i, 0)), pl.BlockSpec((6, HEAD_DIM), lambda i: (0, 0)), tab, tab, tab],
        out_specs=pl.BlockSpec((tq, B_QK), lambda i: (i, 0)),
        out_shape=jax.ShapeDtypeStruct((s, B_QK), CDT), compiler_params=_cparams("parallel"),
    )(proj, gains, *tabs)


PERMUTE_BLOCK_BYTES = 4 * 1024 * 1024


def _view_rows(length, dil, width, itemsize):
    rows = 16
    while 2 * rows * dil * width * itemsize <= PERMUTE_BLOCK_BYTES and 2 * rows <= length:
        rows *= 2
    return rows


def _to_view(x, col, width, dil, name):
    s = x.shape[0]
    length = s // dil
    tl = _view_rows(length, dil, width, 4)
    lanes = HEAD_DIM
    nblk = width // lanes

    def body(x_ref, o_ref, slab_ref):
        for b in range(nblk):
            slab_ref[b] = x_ref[:, b * lanes:(b + 1) * lanes].astype(F32)
        for r in range(dil):
            for b in range(nblk):
                o_ref[:, r * width + b * lanes:r * width + (b + 1) * lanes] = (
                    slab_ref.at[b][pl.ds(r, tl, stride=dil), :].astype(o_ref.dtype))

    return pl.pallas_call(
        body, name=name, grid=(length // tl,),
        in_specs=[pl.BlockSpec((tl * dil, width), lambda i: (i, col))],
        out_specs=pl.BlockSpec((tl, dil * width), lambda i: (i, 0)),
        out_shape=jax.ShapeDtypeStruct((length, dil * width), x.dtype),
        scratch_shapes=[pltpu.VMEM((nblk, tl * dil, lanes), F32)],
        compiler_params=_cparams("parallel"),
    )(x)


def _from_view(v, dil, name):
    length, width = v.shape[0], v.shape[1] // dil
    tl = _view_rows(length, dil, width, 4)
    lanes = HEAD_DIM
    nblk = width // lanes

    def body(v_ref, o_ref, slab_ref):
        for r in range(dil):
            for b in range(nblk):
                slab_ref.at[b][pl.ds(r, tl, stride=dil), :] = v_ref[:, r * width + b * lanes:r * width + (b + 1) * lanes].astype(F32)
        for b in range(nblk):
            o_ref[:, b * lanes:(b + 1) * lanes] = slab_ref[b].astype(o_ref.dtype)

    return pl.pallas_call(
        body, name=name, grid=(length // tl,),
        in_specs=[pl.BlockSpec((tl, dil * width), lambda i: (i, 0))],
        out_specs=pl.BlockSpec((tl * dil, width), lambda i: (i, 0)),
        out_shape=jax.ShapeDtypeStruct((length * dil, width), v.dtype),
        scratch_shapes=[pltpu.VMEM((nblk, tl * dil, lanes), F32)],
        compiler_params=_cparams("parallel"),
    )(v)


def _b_attn_fwd(q, k, v, bases, dil, name):
    length = q.shape[0]
    nb = length // CHUNK
    scale = 1.0 / math.sqrt(HEAD_DIM)
    w = B_HEADS * HEAD_DIM
    qb, kb, vb = bases

    def body(q_ref, kc_ref, kp_ref, vc_ref, vp_ref, o_ref, lse_ref):
        n = pl.program_id(1)
        qi = lax.broadcasted_iota(jnp.int32, (CHUNK, 2 * CHUNK), 0)
        ki = lax.broadcasted_iota(jnp.int32, (CHUNK, 2 * CHUNK), 1)
        first_key = jnp.where(n > 0, 0, CHUNK)
        mask = (ki >= qi) & (ki <= qi + CHUNK) & (ki >= first_key)
        lane = lax.broadcasted_iota(jnp.int32, (CHUNK, HEAD_DIM), 1)
        lse_all = jnp.zeros((CHUNK, HEAD_DIM), F32)
        for h in range(B_HEADS):
            sl = slice(h * HEAD_DIM, (h + 1) * HEAD_DIM)
            k2 = jnp.concatenate([kp_ref[:, sl], kc_ref[:, sl]], axis=0)
            v2 = jnp.concatenate([vp_ref[:, sl], vc_ref[:, sl]], axis=0)
            sc = jnp.where(mask, _dot(q_ref[:, sl], k2, NT) * scale, -1e30)
            m = jnp.max(sc, axis=-1, keepdims=True)
            p = jnp.exp(sc - m)
            l = jnp.sum(p, axis=-1, keepdims=True)
            o_ref[:, sl] = _dot(p.astype(CDT), v2) / l
            lse_all = jnp.where(lane == h, m + jnp.log(l), lse_all)
        lse_ref[...] = lse_all

    prev = lambda n: jnp.maximum(n - 1, 0)
    blk = lambda f: pl.BlockSpec((CHUNK, w), f)
    return pl.pallas_call(
        body, name=name, grid=(dil, nb),
        in_specs=[blk(lambda r, n: (n, qb + r)), blk(lambda r, n: (n, kb + r)), blk(lambda r, n: (prev(n), kb + r)),
                  blk(lambda r, n: (n, vb + r)), blk(lambda r, n: (prev(n), vb + r))],
        out_specs=[blk(lambda r, n: (n, r)), pl.BlockSpec((CHUNK, HEAD_DIM), lambda r, n: (n, r))],
        out_shape=[jax.ShapeDtypeStruct((length, dil * w), F32), jax.ShapeDtypeStruct((length, dil * HEAD_DIM), F32)],
        compiler_params=_cparams("parallel", "parallel"),
    )(q, k, k, v, v)


def _b_combine(os_, lses, proj, name, tq=512):
    s = proj.shape[0]
    w = B_HEADS * HEAD_DIM

    def body(o0_ref, o1_ref, o2_ref, l0_ref, l1_ref, l2_ref, z_ref, y_ref, oj_ref, lj_ref):
        l0, l1, l2 = l0_ref[...], l1_ref[...], l2_ref[...]
        m = jnp.maximum(jnp.maximum(l0, l1), l2)
        lj = m + jnp.log(jnp.exp(l0 - m) + jnp.exp(l1 - m) + jnp.exp(l2 - m))
        lj_ref[...] = lj
        w0, w1, w2 = jnp.exp(l0 - lj), jnp.exp(l1 - lj), jnp.exp(l2 - lj)
        for h in range(B_HEADS):
            sl = slice(h * HEAD_DIM, (h + 1) * HEAD_DIM)
            o = w0[:, h:h + 1] * o0_ref[:, sl] + w1[:, h:h + 1] * o1_ref[:, sl] + w2[:, h:h + 1] * o2_ref[:, sl]
            z = z_ref[:, sl].astype(F32)
            oj_ref[:, sl] = o.astype(oj_ref.dtype)
            y_ref[:, sl] = (o * (z * _sigmoid(z))).astype(y_ref.dtype)

    blk = pl.BlockSpec((tq, w), lambda i: (i, 0))
    st = pl.BlockSpec((tq, HEAD_DIM), lambda i: (i, 0))
    return pl.pallas_call(
        body, name=name, grid=(s // tq,),
        in_specs=[blk, blk, blk, st, st, st, pl.BlockSpec((tq, w), lambda i: (i, 9))],
        out_specs=[blk, blk, st],
        out_shape=[jax.ShapeDtypeStruct((s, w), CDT), jax.ShapeDtypeStruct((s, w), CDT), jax.ShapeDtypeStruct((s, HEAD_DIM), F32)],
        compiler_params=_cparams("parallel"),
    )(*os_, *lses, proj)


def _b_bwd_prep(dy, oj, proj, name, tq=512):
    s = proj.shape[0]
    w = B_HEADS * HEAD_DIM

    def body(dy_ref, oj_ref, z_ref, do_ref, dz_ref, dd_ref):
        lane = lax.broadcasted_iota(jnp.int32, (tq, HEAD_DIM), 1)
        dd = jnp.zeros((tq, HEAD_DIM), F32)
        for h in range(B_HEADS):
            sl = slice(h * HEAD_DIM, (h + 1) * HEAD_DIM)
            z = z_ref[:, sl].astype(F32)
            dyv = dy_ref[:, sl].astype(F32)
            o = oj_ref[:, sl].astype(F32)
            sg = _sigmoid(z)
            do = dyv * (z * sg)
            do_ref[:, sl] = do.astype(do_ref.dtype)
            dz_ref[:, sl] = (dyv * o * (sg * (1.0 + z * (1.0 - sg)))).astype(dz_ref.dtype)
            dd = jnp.where(lane == h, jnp.sum(do * o, axis=-1, keepdims=True), dd)
        dd_ref[...] = dd

    blk = pl.BlockSpec((tq, w), lambda i: (i, 0))
    st = pl.BlockSpec((tq, HEAD_DIM), lambda i: (i, 0))
    return pl.pallas_call(
        body, name=name, grid=(s // tq,),
        in_specs=[blk, blk, pl.BlockSpec((tq, w), lambda i: (i, 9))], out_specs=[blk, blk, st],
        out_shape=[jax.ShapeDtypeStruct((s, w), CDT), jax.ShapeDtypeStruct((s, w), CDT), jax.ShapeDtypeStruct((s, HEAD_DIM), F32)],
        compiler_params=_cparams("parallel"),
    )(dy, oj, proj)


def _b_attn_bwd(q, k, v, bases, do, lj, dd, dil, name):
    length = q.shape[0]
    nb = length // CHUNK
    scale = 1.0 / math.sqrt(HEAD_DIM)
    w = B_HEADS * HEAD_DIM
    qb, kb, vb = bases

    def body(qj_ref, qn_ref, k_ref, v_ref, doj_ref, don_ref, lj_ref, ln_ref, dj_ref, dn_ref, out_ref, carry_ref):
        j = pl.program_id(1)

        @pl.when(j == 0)
        def _():
            carry_ref[...] = jnp.zeros_like(carry_ref)

        qi = lax.broadcasted_iota(jnp.int32, (2 * CHUNK, CHUNK), 0)
        ki = lax.broadcasted_iota(jnp.int32, (2 * CHUNK, CHUNK), 1)
        no_next = jnp.where(j + 1 < nb, 0, 2 * CHUNK)
        mask = ((qi < CHUNK) & (ki <= qi)) | ((qi >= CHUNK) & (ki >= qi - CHUNK + no_next))
        for h in range(B_HEADS):
            sl = slice(h * HEAD_DIM, (h + 1) * HEAD_DIM)
            q2 = jnp.concatenate([qj_ref[:, sl], qn_ref[:, sl]], axis=0)
            do2 = jnp.concatenate([doj_ref[:, sl], don_ref[:, sl]], axis=0)
            lse2 = jnp.concatenate([lj_ref[:, h:h + 1], ln_ref[:, h:h + 1]], axis=0)
            d2 = jnp.concatenate([dj_ref[:, h:h + 1], dn_ref[:, h:h + 1]], axis=0)
            k = k_ref[:, sl]
            v = v_ref[:, sl]
            sc = _dot(q2, k, NT) * scale
            p = jnp.where(mask, jnp.exp(sc - lse2), 0.0)
            dp = _dot(do2, v, NT)
            ds = (p * (dp - d2) * scale).astype(CDT)
            dq2 = _dot(ds, k)
            out_ref[:, sl] = (carry_ref[:, sl] + dq2[:CHUNK]).astype(out_ref.dtype)
            carry_ref[:, sl] = dq2[CHUNK:]
            out_ref[:, w + h * HEAD_DIM:w + (h + 1) * HEAD_DIM] = _dot(ds, q2, TN).astype(out_ref.dtype)
            out_ref[:, 2 * w + h * HEAD_DIM:2 * w + (h + 1) * HEAD_DIM] = _dot(p.astype(CDT), do2, TN).astype(out_ref.dtype)

    nxt = lambda j: jnp.minimum(j + 1, nb - 1)
    blk = lambda f: pl.BlockSpec((CHUNK, w), f)
    st = lambda f: pl.BlockSpec((CHUNK, HEAD_DIM), f)
    return pl.pallas_call(
        body, name=name, grid=(dil, nb),
        in_specs=[blk(lambda r, j: (j, qb + r)), blk(lambda r, j: (nxt(j), qb + r)),
                  blk(lambda r, j: (j, kb + r)), blk(lambda r, j: (j, vb + r)),
                  blk(lambda r, j: (j, r)), blk(lambda r, j: (nxt(j), r)),
                  st(lambda r, j: (j, r)), st(lambda r, j: (nxt(j), r)), st(lambda r, j: (j, r)), st(lambda r, j: (nxt(j), r))],
        out_specs=pl.BlockSpec((CHUNK, 3 * w), lambda r, j: (j, r)),
        out_shape=jax.ShapeDtypeStruct((length, dil * 3 * w), CDT),
        scratch_shapes=[pltpu.VMEM((CHUNK, w), F32)],
        compiler_params=_cparams("parallel", "arbitrary"),
    )(q, q, k, v, do, do, lj, lj, dd, dd)


def _b_qk_bwd(proj, dqkv, dz, gains, tabs, name, tq=256):
    s = proj.shape[0]
    w = B_HEADS * HEAD_DIM

    def body(p_ref, d0_ref, d1_ref, d2_ref, dz_ref, g_ref, tc_ref, ta_ref, tb_ref, dp_ref, dg_ref):
        @pl.when(pl.program_id(0) == 0)
        def _():
            dg_ref[...] = jnp.zeros_like(dg_ref)

        tc, ta, tb = tc_ref[...], ta_ref[...], tb_ref[...]
        d_refs = (d0_ref, d1_ref, d2_ref)
        for g in range(3):
            for t in range(2):
                tg = t * 3 + g
                gain = g_ref[tg:tg + 1, :]
                dgain = jnp.zeros((1, HEAD_DIM), F32)
                for h in range(B_HEADS):
                    cols = slice(tg * w + h * HEAD_DIM, tg * w + (h + 1) * HEAD_DIM)
                    xv = p_ref[:, cols].astype(F32)
                    r = lax.rsqrt(_lane_sums(xv * xv) * (1.0 / HEAD_DIM) + EPS)
                    dyv = d_refs[g][:, t * w + h * HEAD_DIM:t * w + (h + 1) * HEAD_DIM].astype(F32)
                    dxn = dyv * tc + pltpu.roll(dyv * ta, ROPE_HALF, 1) + pltpu.roll(dyv * tb, HEAD_DIM - ROPE_HALF, 1)
                    gd = dxn * gain
                    dx = r * gd - xv * (r * r * r) * (_lane_sums(gd * xv) * (1.0 / HEAD_DIM))
                    dp_ref[:, cols] = dx.astype(dp_ref.dtype)
                    dgain = dgain + jnp.sum(dxn * xv * r, axis=0, keepdims=True)
                dg_ref[tg:tg + 1, :] += dgain
            dp_ref[:, (6 + g) * w:(7 + g) * w] = d_refs[g][:, 2 * w:3 * w]
        dp_ref[:, 9 * w:10 * w] = dz_ref[...]

    tab = pl.BlockSpec((tq, HEAD_DIM), lambda i: (i, 0))
    dblk = pl.BlockSpec((tq, 3 * w), lambda i: (i, 0))
    return pl.pallas_call(
        body, name=name, grid=(s // tq,),
        in_specs=[pl.BlockSpec((tq, B_QK), lambda i: (i, 0)), dblk, dblk, dblk, pl.BlockSpec((tq, w), lambda i: (i, 0)),
                  pl.BlockSpec((6, HEAD_DIM), lambda i: (0, 0)), tab, tab, tab],
        out_specs=[pl.BlockSpec((tq, B_IN), lambda i: (i, 0)), pl.BlockSpec((6, HEAD_DIM), lambda i: (0, 0))],
        out_shape=[jax.ShapeDtypeStruct((s, B_IN), CDT), jax.ShapeDtypeStruct((6, HEAD_DIM), F32)],
        compiler_params=_cparams("arbitrary"),
    )(proj, *dqkv, dz, gains, *tabs)


def _inv_count(t, window):
    return 1.0 / jnp.minimum(t + 1, window).astype(F32)


POOL_BLOCK = 128


def _band(window, forward):
    r = lax.broadcasted_iota(jnp.int32, (POOL_BLOCK, 2 * POOL_BLOCK), 0)
    j = lax.broadcasted_iota(jnp.int32, (POOL_BLOCK, 2 * POOL_BLOCK), 1)
    if forward:
        return jnp.where((j >= r) & (j < r + window), 1.0, 0.0).astype(CDT)
    return jnp.where((j <= r + POOL_BLOCK) & (j > r + POOL_BLOCK - window), 1.0, 0.0).astype(CDT)


def _split_dot(band, v):
    hi = v.astype(jnp.bfloat16)
    lo = (v - hi.astype(F32)).astype(jnp.bfloat16)
    band = band.astype(jnp.bfloat16)
    return _dot(band, hi) + _dot(band, lo)


def _pool_diff(x_ref, halo_ref, diff_ref, i, tq):
    t = i * tq + lax.broadcasted_iota(jnp.int32, (tq, 1), 0)
    for g, window in enumerate(POOL_SIZES):
        cols = slice(g * C_GROUP, (g + 1) * C_GROUP)
        band = _band(window, forward=False)
        inv = _inv_count(t, window)
        for b in range(tq // POOL_BLOCK):
            rows = slice(b * POOL_BLOCK, (b + 1) * POOL_BLOCK)
            cur = x_ref[rows, cols]
            if b == 0:
                above = jnp.where(i > 0, halo_ref[:, cols], jnp.zeros_like(cur))
            else:
                above = x_ref[(b - 1) * POOL_BLOCK:b * POOL_BLOCK, cols]
            pooled = _dot(band, jnp.concatenate([above, cur], axis=0)) * inv[rows]
            diff_ref[rows, cols] = (pooled - cur.astype(F32)).astype(diff_ref.dtype)


GRP_SHARD = (N_CHIPS, 2, 256, C_GROUP)
GRP_ROWS = C_GROUP // N_CHIPS


def _grp_rows(g):
    return g // 2, slice((g % 2) * GRP_ROWS, (g % 2 + 1) * GRP_ROWS)


def _grp_weight(w_ref, g):
    half, rows = _grp_rows(g)
    return jnp.concatenate([w_ref[k, half, rows, :] for k in range(N_CHIPS)], axis=0)


def _c_mid_fwd(proj, w_grp, scale, name, tq=512):
    s = proj.shape[0]
    per = tq // POOL_BLOCK

    def body(x_ref, halo_ref, z_ref, w_ref, sc_ref, y_ref, diff_ref):
        _pool_diff(x_ref, halo_ref, diff_ref, pl.program_id(0), tq)
        for g in range(len(POOL_SIZES)):
            cols = slice(g * C_GROUP, (g + 1) * C_GROUP)
            z = z_ref[:, cols].astype(F32)
            y_ref[:, cols] = (_dot(diff_ref[:, cols], _grp_weight(w_ref, g)) * sc_ref[:, cols] * (z * _sigmoid(z))).astype(y_ref.dtype)

    return pl.pallas_call(
        body, name=name, grid=(s // tq,),
        in_specs=[pl.BlockSpec((tq, C_WIDTH), lambda i: (i, 0)),
                  pl.BlockSpec((POOL_BLOCK, C_WIDTH), lambda i: (jnp.maximum(i * per - 1, 0), 0)),
                  pl.BlockSpec((tq, C_WIDTH), lambda i: (i, 1)),
                  pl.BlockSpec(GRP_SHARD, lambda i: (0, 0, 0, 0)), pl.BlockSpec((1, C_WIDTH), lambda i: (0, 0))],
        out_specs=pl.BlockSpec((tq, C_WIDTH), lambda i: (i, 0)),
        out_shape=jax.ShapeDtypeStruct((s, C_WIDTH), CDT),
        scratch_shapes=[pltpu.VMEM((tq, C_WIDTH), CDT)], compiler_params=_cparams("parallel"),
    )(proj, proj, proj, w_grp, scale)


def _c_mid_bwd(proj, w_grp, scale, dy, name, tq=512):
    s = proj.shape[0]
    per = tq // POOL_BLOCK

    def body(x_ref, halo_ref, z_ref, w_ref, sc_ref, dy_ref, dd_ref, dz_ref, dw_ref, dsc_ref, diff_ref):
        @pl.when(pl.program_id(0) == 0)
        def _():
            dw_ref[...] = jnp.zeros_like(dw_ref)
            dsc_ref[...] = jnp.zeros_like(dsc_ref)

        _pool_diff(x_ref, halo_ref, diff_ref, pl.program_id(0), tq)
        for g in range(len(POOL_SIZES)):
            cols = slice(g * C_GROUP, (g + 1) * C_GROUP)
            d = diff_ref[:, cols]
            wg = _grp_weight(w_ref, g)
            half, rows = _grp_rows(g)
            m0 = _dot(d, wg)
            z = z_ref[:, cols].astype(F32)
            dyv = dy_ref[:, cols].astype(F32)
            sc = sc_ref[:, cols]
            sg = _sigmoid(z)
            dmixed = dyv * (z * sg)
            dz_ref[:, cols] = (dyv * m0 * sc * (sg * (1.0 + z * (1.0 - sg)))).astype(dz_ref.dtype)
            dsc_ref[:, cols] += jnp.sum(dmixed * m0, axis=0, keepdims=True)
            dm0 = (dmixed * sc).astype(CDT)
            dwg = _dot(d, dm0, TN)
            for k in range(N_CHIPS):
                dw_ref[k, half, rows, :] += dwg[k * GRP_ROWS:(k + 1) * GRP_ROWS]
            dd_ref[:, cols] = _dot(dm0, wg, NT)

    blk = pl.BlockSpec((tq, C_WIDTH), lambda i: (i, 0))
    wsp = pl.BlockSpec(GRP_SHARD, lambda i: (0, 0, 0, 0))
    vec = pl.BlockSpec((1, C_WIDTH), lambda i: (0, 0))
    return pl.pallas_call(
        body, name=name, grid=(s // tq,),
        in_specs=[blk, pl.BlockSpec((POOL_BLOCK, C_WIDTH), lambda i: (jnp.maximum(i * per - 1, 0), 0)),
                  pl.BlockSpec((tq, C_WIDTH), lambda i: (i, 1)), wsp, vec, blk],
        out_specs=[blk, blk, wsp, vec],
        out_shape=[jax.ShapeDtypeStruct((s, C_WIDTH), F32), jax.ShapeDtypeStruct((s, C_WIDTH), CDT),
                   jax.ShapeDtypeStruct(GRP_SHARD, F32), jax.ShapeDtypeStruct((1, C_WIDTH), F32)],
        scratch_shapes=[pltpu.VMEM((tq, C_WIDTH), CDT)], compiler_params=_cparams("arbitrary"),
    )(proj, proj, proj, w_grp, scale, dy)


def _c_pool_bwd(ddiff, dz, name, tq=512):
    s = ddiff.shape[0]
    per = tq // POOL_BLOCK
    last = s // tq - 1

    def body(d_ref, halo_ref, dz_ref, o_ref):
        i = pl.program_id(0)
        t = i * tq + lax.broadcasted_iota(jnp.int32, (tq, 1), 0)
        for g, window in enumerate(POOL_SIZES):
            cols = slice(g * C_GROUP, (g + 1) * C_GROUP)
            band = _band(window, forward=True)
            inv = _inv_count(t, window)
            for b in range(tq // POOL_BLOCK):
                rows = slice(b * POOL_BLOCK, (b + 1) * POOL_BLOCK)
                cur = d_ref[rows, cols]
                if b == tq // POOL_BLOCK - 1:
                    below = jnp.where(i < last, halo_ref[:, cols] * (1.0 / window), 0.0)
                else:
                    nxt = slice((b + 1) * POOL_BLOCK, (b + 2) * POOL_BLOCK)
                    below = d_ref[nxt, cols] * inv[nxt]
                summed = _split_dot(band, jnp.concatenate([cur * inv[rows], below], axis=0))
                o_ref[rows, cols] = (summed - cur).astype(o_ref.dtype)
        o_ref[:, C_WIDTH:] = dz_ref[...]

    return pl.pallas_call(
        body, name=name, grid=(s // tq,),
        in_specs=[pl.BlockSpec((tq, C_WIDTH), lambda i: (i, 0)),
                  pl.BlockSpec((POOL_BLOCK, C_WIDTH), lambda i: (jnp.minimum((i + 1) * per, s // POOL_BLOCK - 1), 0)),
                  pl.BlockSpec((tq, C_WIDTH), lambda i: (i, 0))],
        out_specs=pl.BlockSpec((tq, 2 * C_WIDTH), lambda i: (i, 0)),
        out_shape=jax.ShapeDtypeStruct((s, 2 * C_WIDTH), CDT),
        compiler_params=_cparams("parallel"),
    )(ddiff, ddiff, dz)


LAYERS = (("a", 0), ("b", 0), ("c", 0), ("a", 1))


def _shard_spec(block, layer, where):
    if layer is None:
        return pl.BlockSpec((None,) + block, where)
    return pl.BlockSpec((None, None) + block, lambda i, j, q: (where(i, j, q)[0], layer) + where(i, j, q)[1:])


def _w_in_fwd(h, gw, layer, name):
    s, width = h.shape[0], gw.shape[-1]
    return _mm(h, gw, "nn", CDT, name, (s, N_CHIPS * width, D_MODEL), (1024, width, D_MODEL),
               b_spec=_shard_spec((D_MODEL, width), layer, lambda i, j, q: (j, 0, 0)))


def _w_in_dh(dproj, gw, x, dxo, gain, name, tm=512):
    s, width = dproj.shape[0], gw.shape[-1]

    def body(a_ref, b_ref, x_ref, dxo_ref, g_ref, dx_ref, dxc_ref, dg_ref, acc_ref):
        i, q = pl.program_id(0), pl.program_id(1)
        p = _dot(a_ref[...], b_ref[...], NT)

        @pl.when(q == 0)
        def _():
            acc_ref[...] = p

        @pl.when(q > 0)
        def _():
            acc_ref[...] += p

        @pl.when((i == 0) & (q == 0))
        def _():
            dg_ref[...] = jnp.zeros_like(dg_ref)

        @pl.when(q == N_CHIPS - 1)
        def _():
            dh = acc_ref[...]
            xv = x_ref[...]
            r = lax.rsqrt(jnp.mean(xv * xv, axis=-1, keepdims=True) + EPS)
            gd = dh * g_ref[...]
            dx = dxo_ref[...] + r * gd - xv * (r * r * r) * jnp.mean(gd * xv, axis=-1, keepdims=True)
            dx_ref[...] = dx
            dxc_ref[...] = dx.astype(dxc_ref.dtype)
            dg_ref[...] += jnp.sum(dh * xv * r, axis=0, keepdims=True)

    blk = pl.BlockSpec((tm, D_MODEL), lambda i, q: (i, 0))
    vec = pl.BlockSpec((1, D_MODEL), lambda i, q: (0, 0))
    return pl.pallas_call(
        body, name=name, grid=(s // tm, N_CHIPS),
        in_specs=[pl.BlockSpec((tm, width), lambda i, q: (i, q)), pl.BlockSpec((None, D_MODEL, width), lambda i, q: (q, 0, 0)),
                  blk, blk, vec],
        out_specs=[blk, blk, vec],
        out_shape=[jax.ShapeDtypeStruct((s, D_MODEL), F32), jax.ShapeDtypeStruct((s, D_MODEL), CDT), jax.ShapeDtypeStruct((1, D_MODEL), F32)],
        scratch_shapes=[pltpu.VMEM((tm, D_MODEL), F32)], compiler_params=_cparams("arbitrary", "arbitrary"),
    )(dproj, gw, x, dxo, gain)


def _w_in_grad(h, dproj, width, name):
    s = h.shape[0]
    tokens = 2048 if width <= 1536 else 1024
    return _mm(h, dproj, "tn", CDT, name, (D_MODEL, N_CHIPS * width, s), (D_MODEL, width, tokens),
               o_spec=pl.BlockSpec((None, D_MODEL, width), lambda i, j, q: (j, 0, 0)), o_shape=(N_CHIPS, D_MODEL, width))


def _w_out_spec(gw, layer):
    rows = gw.shape[-2]
    if layer is None:
        return pl.BlockSpec((N_CHIPS, rows, D_MODEL), lambda i: (0, 0, 0))
    return pl.BlockSpec((N_CHIPS, None, rows, D_MODEL), lambda i: (0, layer, 0, 0))


def _w_out_fwd(y, gw, layer, x, name, tm=512):
    s, k = y.shape

    def body(y_ref, w_ref, x_ref, o_ref):
        o_ref[...] = x_ref[...] + _dot(y_ref[...], w_ref[...].reshape(k, D_MODEL))

    blk = pl.BlockSpec((tm, D_MODEL), lambda i: (i, 0))
    return pl.pallas_call(
        body, name=name, grid=(s // tm,), in_specs=[pl.BlockSpec((tm, k), lambda i: (i, 0)), _w_out_spec(gw, layer), blk],
        out_specs=blk, out_shape=jax.ShapeDtypeStruct((s, D_MODEL), F32), compiler_params=_cparams("parallel"),
    )(y, gw, x)


def _w_out_dy(dxc, gw, layer, name, tm=1024):
    s, k = dxc.shape[0], N_CHIPS * gw.shape[-2]
    tm = min(tm, s)

    def body(dx_ref, w_ref, o_ref):
        o_ref[...] = _dot(dx_ref[...], w_ref[...].reshape(k, D_MODEL), NT).astype(o_ref.dtype)

    return pl.pallas_call(
        body, name=name, grid=(s // tm,), in_specs=[pl.BlockSpec((tm, D_MODEL), lambda i: (i, 0)), _w_out_spec(gw, layer)],
        out_specs=pl.BlockSpec((tm, k), lambda i: (i, 0)), out_shape=jax.ShapeDtypeStruct((s, k), CDT),
        compiler_params=_cparams("parallel"),
    )(dxc, gw)


def _w_out_grad(y, dxc, rows, name, tokens=1024):
    s, k = y.shape
    tokens = min(tokens, s)
    steps = s // tokens

    def body(y_ref, dx_ref, o_ref, acc_ref):
        i = pl.program_id(0)
        p = _dot(y_ref[...], dx_ref[...], TN)

        @pl.when(i == 0)
        def _():
            acc_ref[...] = p

        @pl.when(i > 0)
        def _():
            acc_ref[...] += p

        @pl.when(i == steps - 1)
        def _():
            o_ref[...] = acc_ref[...].reshape(N_CHIPS, rows, D_MODEL).astype(o_ref.dtype)

    return pl.pallas_call(
        body, name=name, grid=(steps,),
        in_specs=[pl.BlockSpec((tokens, k), lambda i: (i, 0)), pl.BlockSpec((tokens, D_MODEL), lambda i: (i, 0))],
        out_specs=pl.BlockSpec((N_CHIPS, rows, D_MODEL), lambda i: (0, 0, 0)),
        out_shape=jax.ShapeDtypeStruct((N_CHIPS, rows, D_MODEL), CDT),
        scratch_shapes=[pltpu.VMEM((k, D_MODEL), F32)], compiler_params=_cparams("arbitrary"),
    )(y, dxc)


def _local_step(x, target, w, on_grads=None, layer_weights=None):
    s = x.shape[0]
    tabs = _rope_tables(s)
    qk_gains = jnp.concatenate([w["b_q_gain"][0], w["b_k_gain"][0]], axis=0)
    saved = []
    for li, (kind, j) in enumerate(LAYERS):
        if layer_weights is not None:
            w = {**w, **layer_weights(li, x)}
        h = _rms_fwd(x, w["norm_gain"][li:li + 1], f"rms_fwd{li}")
        if kind == "a":
            proj = _w_in_fwd(h, w[f"a_w_in{j}"], None, f"a_in{li}")
            bs_t = w["a_b_s"][j].T
            y = _a_mid_fwd(proj, w["a_v_gain"][j:j + 1], w["a_w_s"][j], bs_t, f"a_mid_fwd{li}")
            x_next = _w_out_fwd(y, w[f"a_w_out{j}"], None, x, f"a_out{li}")
            saved.append((x, h, proj, y))
        elif kind == "b":
            proj = _w_in_fwd(h, w["b_w_in"], None, f"b_in{li}")
            qk = _b_qk_fwd(proj, qk_gains, tabs, f"b_qk_fwd{li}")
            qkv, os_, lses = [], [], []
            for g, dil in enumerate(B_DILATIONS):
                if dil == 1:
                    ops = (qk, qk, proj, (g, 3 + g, 6 + g))
                else:
                    ops = (_to_view(qk, g, 1024, dil, f"b_q_view{li}_{g}"), _to_view(qk, 3 + g, 1024, dil, f"b_k_view{li}_{g}"),
                           _to_view(proj, 6 + g, 1024, dil, f"b_v_view{li}_{g}"), (0, 0, 0))
                o, lse = _b_attn_fwd(*ops, dil, f"b_attn_fwd{li}_{g}")
                if dil > 1:
                    o, lse = _from_view(o, dil, f"b_o_nat{li}_{g}"), _from_view(lse, dil, f"b_lse_nat{li}_{g}")
                qkv.append(ops)
                os_.append(o)
                lses.append(lse)
            y, oj, lj = _b_combine(os_, lses, proj, f"b_combine{li}")
            x_next = _w_out_fwd(y, w["b_w_out"], None, x, f"b_out{li}")
            saved.append((x, h, proj, y, qkv, oj, lj))
        else:
            proj = _w_in_fwd(h, w["c_w_in"], None, f"c_in{li}")
            y = _c_mid_fwd(proj, w["c_w_grp"], w["c_scale"][j:j + 1], f"c_mid_fwd{li}")
            x_next = _w_out_fwd(y, w["c_w_out"], None, x, f"c_out{li}")
            saved.append((x, h, proj, y))
        x = x_next

    dx, dxc, sq = _loss_bwd(x, target, "loss_bwd")
    grads = {"norm_gain": [None] * len(LAYERS), "a_w_in": [None, None], "a_v_gain": [None, None], "a_w_s": [None, None],
             "a_b_s": [None, None], "a_w_out": [None, None]}
    for li in reversed(range(len(LAYERS))):
        kind, j = LAYERS[li]
        sv = saved[li]
        xin, h, proj, y = sv[:4]
        if kind == "a":
            grads["a_w_out"][j] = _w_out_grad(y, dxc, 512, f"a_dwout{li}")
            dy = _w_out_dy(dxc, w[f"a_w_out{j}"], None, f"a_dy{li}")
            dproj, dws, dbs_t, dvg = _a_mid_bwd(proj, dy, w["a_v_gain"][j:j + 1], w["a_w_s"][j], w["a_b_s"][j].T, f"a_mid_bwd{li}")
            grads["a_w_s"][j], grads["a_b_s"][j], grads["a_v_gain"][j] = dws, dbs_t.T, dvg[0]
            grads["a_w_in"][j] = _w_in_grad(h, dproj, 1536, f"a_dwin{li}")
            w_in = w[f"a_w_in{j}"]
        elif kind == "b":
            qkv, oj, lj = sv[4:]
            grads["b_w_out"] = _w_out_grad(y, dxc, 256, f"b_dwout{li}")
            dy = _w_out_dy(dxc, w["b_w_out"], None, f"b_dy{li}")
            do, dz, dd = _b_bwd_prep(dy, oj, proj, f"b_bwd_prep{li}")
            dqkv = []
            for g, dil in enumerate(B_DILATIONS):
                stats = (do, lj, dd)
                if dil > 1:
                    stats = (_to_view(do, 0, 1024, dil, f"b_do_view{li}_{g}"), _to_view(lj, 0, HEAD_DIM, dil, f"b_lj_view{li}_{g}"),
                             _to_view(dd, 0, HEAD_DIM, dil, f"b_dd_view{li}_{g}"))
                d = _b_attn_bwd(*qkv[g], *stats, dil, f"b_attn_bwd{li}_{g}")
                dqkv.append(_from_view(d, dil, f"b_dqkv_nat{li}_{g}") if dil > 1 else d)
            dproj, dgains = _b_qk_bwd(proj, dqkv, dz, qk_gains, tabs, f"b_qk_bwd{li}")
            grads["b_q_gain"], grads["b_k_gain"] = dgains[None, :3], dgains[None, 3:]
            grads["b_w_in"] = _w_in_grad(h, dproj, 2560, f"b_dwin{li}")
            w_in = w["b_w_in"]
        else:
            grads["c_w_out"] = _w_out_grad(y, dxc, 512, f"c_dwout{li}")
            dy = _w_out_dy(dxc, w["c_w_out"], None, f"c_dy{li}")
            ddiff, dz, dwg, dsc = _c_mid_bwd(proj, w["c_w_grp"], w["c_scale"][j:j + 1], dy, f"c_mid_bwd{li}")
            grads["c_w_grp"], grads["c_scale"] = dwg, dsc
            dproj = _c_pool_bwd(ddiff, dz, f"c_pool_bwd{li}")
            grads["c_w_in"] = _w_in_grad(h, dproj, 1024, f"c_dwin{li}")
            w_in = w["c_w_in"]
        gain = w["norm_gain"][li:li + 1]
        if on_grads is not None:
            gain = gain + on_grads(li, grads)
        dx, dxc, dng = _w_in_dh(dproj, w_in, xin, dx, gain, f"{kind}_dh{li}")
        grads["norm_gain"][li] = dng[0]
    for name in ("norm_gain", "a_v_gain", "a_w_s", "a_b_s"):
        grads[name] = jnp.stack(grads[name])
    return sq, dx, grads


ROW = 1024
REDUCE_UNITS = (("a_w_in", 0, 512, 1536), ("a_w_in", 1, 512, 1536), ("a_w_out", 0, 256, 1024), ("a_w_out", 1, 256, 1024),
                ("b_w_in", 0, 512, 2560), ("b_w_out", 0, 128, 1024), ("c_w_in", 0, 512, 1024), ("c_w_grp", 0, 256, 512),
                ("c_w_out", 0, 256, 1024))
GAIN_ROWS = 16
GAINS = len(REDUCE_UNITS)
LAYER_UNITS = {0: (0, 2), 1: (4, 5), 2: (6, 7, 8), 3: (1, 3)}
EARLY_UNITS = LAYER_UNITS
FIRST_UNITS = LAYER_UNITS[0] + (GAINS,)
LATER_UNITS = {li: LAYER_UNITS[li] for li in (1, 2, 3)}
SMALL_OF_HALF = ("a_v_gain", "c_scale")
SMALL_PAD = 16
REP = (("norm_gain", (4, 1024)), ("a_w_s", (2, 8, 128, 128)), ("a_b_s", (2, 8, 128)), ("b_q_gain", (1, 3, 128)), ("b_k_gain", (1, 3, 128)))
REP_CORE = 48
REP_PART = 2 * REP_CORE
SMALL_ROWS = SMALL_PAD + REP_CORE
BLOCK_SHAPES = {"a_w_in": (2, 1024, 1536), "a_v_gain": (2, 512), "a_w_out": (2, 512, 1024), "b_w_in": (1, 1024, 2560),
                "b_w_out": (1, 256, 1024), "c_w_in": (1, 1024, 1024), "c_w_grp": (1, 4, 128, 512), "c_scale": (1, 512),
                "c_w_out": (1, 512, 1024)}
WEIGHTS = ("norm_gain", "a_w_in", "a_v_gain", "a_w_s", "a_b_s", "a_w_out", "b_w_in", "b_q_gain", "b_k_gain", "b_w_out",
           "c_w_in", "c_w_grp", "c_scale", "c_w_out")


def _rows(a, rows):
    a = a.reshape(-1)
    return jnp.pad(a, (0, rows * ROW - a.shape[0])).reshape(rows, ROW)


def _small_unit(vecs, rep_part, dtype):
    halves = [jnp.concatenate([_rows(vecs[h].astype(dtype), SMALL_PAD), rep_part[h * REP_CORE:(h + 1) * REP_CORE].astype(dtype)])
              for h in range(2)]
    return jnp.stack(halves)


def _pack_rep(tree):
    return _rows(jnp.concatenate([tree[n].astype(F32).reshape(-1) for n, _ in REP]), N_CHIPS * REP_PART)


def _unpack_rep(slab):
    flat, out, off = slab.reshape(-1), {}, 0
    for n, shape in REP:
        size = math.prod(shape)
        out[n] = flat[off:off + size].reshape(shape)
        off += size
    return out


def _into_slot(x, first, rows, me, name):
    width = x.shape[1]
    tr = min(rows, 256)

    def body(me_ref, x_ref, o_ref):
        o_ref[...] = x_ref[...].astype(o_ref.dtype)

    return pl.pallas_call(
        body, name=name,
        grid_spec=pltpu.PrefetchScalarGridSpec(
            num_scalar_prefetch=1, grid=(rows // tr,), in_specs=[pl.BlockSpec((tr, width), lambda i, me_ref: (first // tr + i, 0))],
            out_specs=pl.BlockSpec((None, tr, width), lambda i, me_ref: (me_ref[0], i, 0))),
        out_shape=jax.ShapeDtypeStruct((N_CHIPS, rows, width), CDT), compiler_params=_cparams("parallel"),
    )(me.reshape(1), x)


def _gather_inputs(wts, me):
    units = []
    for u, (n, layer, r, w) in enumerate(REDUCE_UNITS):
        slot = _into_slot(wts[n].reshape(-1, w), layer * 2 * r, 2 * r, me, f"slot{u}")
        units.append(slot.reshape(N_CHIPS, 2, r, w))
    gains = jnp.concatenate([wts["a_v_gain"].reshape(-1), wts["c_scale"].reshape(-1)])
    gains = _rows(lax.bitcast_convert_type(gains, CDT), 2 * GAIN_ROWS)
    return units + [_into_slot(gains, 0, 2 * GAIN_ROWS, me, "slot_gains").reshape(N_CHIPS, 2, GAIN_ROWS, ROW)]


def _gathered_weights(indices, units):
    out = {}
    for u, arr in zip(indices, units):
        if u == GAINS:
            gains = lax.bitcast_convert_type(arr[:, 0, :3].reshape(N_CHIPS, 1536, 2), F32)
            out["a_v_gain"] = jnp.concatenate([gains[k, :1024].reshape(2, 512) for k in range(N_CHIPS)], axis=1)
            out["c_scale"] = jnp.concatenate([gains[k, 1024:].reshape(1, 512) for k in range(N_CHIPS)], axis=1)
            continue
        n, layer, r, w = REDUCE_UNITS[u]
        if n == "c_w_grp":
            out[n] = arr
        else:
            out[n + str(layer) if n in ("a_w_in", "a_w_out") else n] = arr.reshape(N_CHIPS, 2 * r, w)
    return out


def _place():
    x, y, c = lax.axis_index("x"), lax.axis_index("y"), lax.axis_index("c")
    chips = [(1 - x, y), (x, 1 - y), (1 - x, 1 - y)]
    return x, y, c, 2 * x + y, (x, y, 1 - c), chips


def _remote(src, dst, sems, j, to):
    send_sems, recv_sems = sems
    return pltpu.make_async_remote_copy(src_ref=src, dst_ref=dst, send_sem=send_sems.at[j], recv_sem=recv_sems.at[j],
                                        device_id=to, device_id_type=MESH)


def _comm_call(body, name, out_shape, n_sems, *args):
    return pl.pallas_call(
        body, name=name, in_specs=[ANY] * len(args), out_specs=ANY, out_shape=out_shape,
        scratch_shapes=[pltpu.SemaphoreType.DMA((n_sems,)), pltpu.SemaphoreType.DMA((n_sems,))],
    )(*args)


def _comm_call_multi(body, name, out_shapes, n_sems, args, aliases=None):
    return pl.pallas_call(
        body, name=name, in_specs=[ANY] * len(args), out_specs=[ANY] * len(out_shapes), out_shape=out_shapes,
        scratch_shapes=[pltpu.SemaphoreType.DMA((n_sems,)), pltpu.SemaphoreType.DMA((n_sems,))],
        input_output_aliases=aliases or {},
    )(*args)


def _allgather_units(units):
    n = len(units)

    def body(*refs):
        outs, sems = refs[n:2 * n], (refs[2 * n], refs[2 * n + 1])
        x, y, c, me, sibling, chips = _place()
        first, passed = [], []
        for u, o_ref in enumerate(outs):
            for j, chip in enumerate(chips):
                first.append(_remote(o_ref.at[me, c], o_ref.at[me, c], sems, 6 * u + j, (*chip, c)))
                first[-1].start()
        for u, o_ref in enumerate(outs):
            for j, (cx, cy) in enumerate(chips):
                landed = o_ref.at[2 * cx + cy, c]
                _remote(landed, landed, sems, 6 * u + j, sibling).wait_recv()
                passed.append(_remote(landed, landed, sems, 6 * u + 3 + j, sibling))
                passed[-1].start()
        for u, o_ref in enumerate(outs):
            for j, (cx, cy) in enumerate(chips):
                landed = o_ref.at[2 * cx + cy, 1 - c]
                _remote(landed, landed, sems, 6 * u + 3 + j, sibling).wait_recv()
        for cp in first + passed:
            cp.wait_send()

    return _comm_call_multi(body, "allgather_first", [jax.ShapeDtypeStruct(u.shape, u.dtype) for u in units], 6 * n, units,
                            aliases={u: u for u in range(n)})


HBM = pl.BlockSpec(memory_space=pltpu.HBM)
SEM = pl.BlockSpec(memory_space=pltpu.SEMAPHORE)
SIDE_EFFECT = pltpu.SideEffectType.DATAFLOW_SIDE_EFFECTING


def _gather_start(groups, after):
    sizes = [len(g) for g in groups]
    units = [u for g in groups for u in g]
    n = len(units)

    def body(*refs):
        arrs, sems, token = refs[:n], refs[n + 1:n + 1 + 2 * len(groups)], refs[-1]
        x, y, c, me, sibling, chips = _place()
        at = 0
        for gi, size in enumerate(sizes):
            for u in range(size):
                mine = arrs[at + u].at[me, c]
                for j, chip in enumerate(chips):
                    _remote(mine, mine, (sems[2 * gi], sems[2 * gi + 1]), 3 * u + j, (*chip, c)).start()
            at += size
        token[...] = jnp.zeros_like(token)

    sem_shapes = [pltpu.SemaphoreType.DMA((3 * size,)) for size in sizes for _ in range(2)]
    outs = pl.pallas_call(
        body, name="gather_start",
        out_shape=(*sem_shapes, *[pltpu.HBM(u.shape, u.dtype) for u in units], jax.ShapeDtypeStruct((8, 128), F32)),
        in_specs=[HBM] * n + [ANY], out_specs=(*[SEM] * len(sem_shapes), *[HBM] * n, pl.BlockSpec(memory_space=pltpu.VMEM)),
        input_output_aliases={i: len(sem_shapes) + i for i in range(n)},
        compiler_params=pltpu.CompilerParams(has_side_effects=SIDE_EFFECT),
    )(*[pltpu.with_memory_space_constraint(u, pltpu.HBM) for u in units], after)
    sems, arrs = outs[:len(sem_shapes)], outs[len(sem_shapes):-1]
    bounds = [sum(sizes[:gi]) for gi in range(len(sizes) + 1)]
    return ([(sems[2 * gi], sems[2 * gi + 1]) for gi in range(len(sizes))],
            [list(arrs[bounds[gi]:bounds[gi + 1]]) for gi in range(len(sizes))], outs[-1])


def _gather_wait(units, sems, after, name):
    n = len(units)

    def body(*refs):
        arrs, send_sems, recv_sems = refs[:n], refs[n], refs[n + 1]
        x, y, c, me, sibling, chips = _place()
        for u in range(n):
            for j, (cx, cy) in enumerate(chips):
                cp = _remote(arrs[u].at[me, c], arrs[u].at[2 * cx + cy, c], (send_sems, recv_sems), 3 * u + j, (cx, cy, c))
                cp.wait_send()
                cp.wait_recv()

    outs = pl.pallas_call(
        body, name=name, out_shape=tuple(pltpu.HBM(u.shape, u.dtype) for u in units),
        in_specs=[HBM] * n + [SEM, SEM, ANY], out_specs=[HBM] * n, input_output_aliases={i: i for i in range(n)},
        compiler_params=pltpu.CompilerParams(has_side_effects=SIDE_EFFECT),
    )(*units, *sems, after)
    return list(outs)


def _forward_units(units, name):
    n = len(units)

    def body(*refs):
        outs, sems = refs[n:2 * n], (refs[2 * n], refs[2 * n + 1])
        x, y, c, me, sibling, chips = _place()
        passed = []
        for u, o_ref in enumerate(outs):
            for j, (cx, cy) in enumerate(chips):
                landed = o_ref.at[2 * cx + cy, c]
                passed.append(_remote(landed, landed, sems, 3 * u + j, sibling))
                passed[-1].start()
        for u, o_ref in enumerate(outs):
            for j, (cx, cy) in enumerate(chips):
                landed = o_ref.at[2 * cx + cy, 1 - c]
                _remote(landed, landed, sems, 3 * u + j, sibling).wait_recv()
        for cp in passed:
            cp.wait_send()

    return _comm_call_multi(body, name, [jax.ShapeDtypeStruct(u.shape, u.dtype) for u in units], 3 * n, units,
                            aliases={u: u for u in range(n)})


def _swap_halves(units, name):
    n = len(units)

    def body(*refs):
        ins, outs, sems = refs[:n], refs[n:2 * n], (refs[2 * n], refs[2 * n + 1])
        x, y, c, me, sibling, chips = _place()
        sent = [_remote(g_ref.at[:, 1 - c], o_ref, sems, u, sibling) for u, (g_ref, o_ref) in enumerate(zip(ins, outs))]
        for cp in sent:
            cp.start()
        for cp in sent:
            cp.wait()

    shapes = [jax.ShapeDtypeStruct((N_CHIPS,) + u.shape[2:], u.dtype) for u in units]
    return _comm_call_multi(body, name, shapes, n, units)


def _scatter_shards(units):
    n = len(units)

    def body(*refs):
        ins, outs, sems = refs[:n], refs[n:2 * n], (refs[2 * n], refs[2 * n + 1])
        x, y, c, me, sibling, chips = _place()
        sent = []
        for u, (s_ref, o_ref) in enumerate(zip(ins, outs)):
            for j, (cx, cy) in enumerate(chips):
                sent.append(_remote(s_ref.at[2 * cx + cy], o_ref.at[me], sems, 3 * u + j, (cx, cy, c)))
                sent[-1].start()
        for u, o_ref in enumerate(outs):
            for j, (cx, cy) in enumerate(chips):
                slot = o_ref.at[2 * cx + cy]
                _remote(slot, slot, sems, 3 * u + j, sibling).wait_recv()
        for cp in sent:
            cp.wait_send()

    return _comm_call_multi(body, "grad_scatter_shards", [jax.ShapeDtypeStruct(u.shape, u.dtype) for u in units], 3 * n, units)


def _scatter_start(units, name):
    n = len(units)

    def body(*refs):
        srcs, lands = refs[:n], refs[n:2 * n]
        send_sems, recv_sems, token = refs[2 * n], refs[2 * n + 1], refs[-1]
        x, y, c, me, sibling, chips = _place()
        for u in range(n):
            for j, (cx, cy) in enumerate(chips):
                _remote(srcs[u].at[2 * cx + cy], lands[u].at[me], (send_sems, recv_sems), 3 * u + j, (cx, cy, c)).start()
        token[...] = jnp.zeros_like(token)

    hbm = [pltpu.HBM(u.shape, u.dtype) for u in units]
    outs = pl.pallas_call(
        body, name=name,
        out_shape=(pltpu.SemaphoreType.DMA((3 * n,)), pltpu.SemaphoreType.DMA((3 * n,)), *hbm, *hbm, jax.ShapeDtypeStruct((8, 128), F32)),
        in_specs=[HBM] * (2 * n), out_specs=(SEM, SEM, *[HBM] * (2 * n), pl.BlockSpec(memory_space=pltpu.VMEM)),
        input_output_aliases={i: 2 + i for i in range(2 * n)},
        compiler_params=pltpu.CompilerParams(has_side_effects=SIDE_EFFECT),
    )(*[pltpu.with_memory_space_constraint(u, pltpu.HBM) for u in units],
      *[pltpu.with_memory_space_constraint(lax.empty(u.shape, u.dtype), pltpu.HBM) for u in units])
    return outs[0], outs[1], outs[2:2 + n], outs[2 + n:2 + 2 * n], outs[-1]


def _scatter_wait(send_sems, recv_sems, srcs, lands, after, name):
    n = len(srcs)

    def body(*refs):
        srcs_, lands_, send_sems_, recv_sems_ = refs[:n], refs[n:2 * n], refs[2 * n], refs[2 * n + 1]
        x, y, c, me, sibling, chips = _place()
        for u in range(n):
            for j, (cx, cy) in enumerate(chips):
                slot = lands_[u].at[2 * cx + cy]
                cp = _remote(srcs_[u].at[2 * cx + cy], slot, (send_sems_, recv_sems_), 3 * u + j, (cx, cy, c))
                cp.wait_send()
                cp.wait_recv()

    hbm = [pltpu.HBM(u.shape, u.dtype) for u in srcs]
    outs = pl.pallas_call(
        body, name=name, out_shape=(*hbm, *hbm),
        in_specs=[HBM] * (2 * n) + [SEM, SEM, ANY], out_specs=[HBM] * (2 * n),
        input_output_aliases={i: i for i in range(2 * n)},
        compiler_params=pltpu.CompilerParams(has_side_effects=SIDE_EFFECT),
    )(*srcs, *lands, send_sems, recv_sems, after)
    return list(outs[n:])


def _share_halves(units):
    n = len(units)

    def body(*refs):
        outs, sems = refs[n:2 * n], (refs[2 * n], refs[2 * n + 1])
        x, y, c, me, sibling, chips = _place()
        sent = [_remote(o_ref.at[:, :, c], o_ref.at[:, :, c], sems, u, sibling) for u, o_ref in enumerate(outs)]
        for cp in sent:
            cp.start()
        for u, o_ref in enumerate(outs):
            theirs = o_ref.at[:, :, 1 - c]
            _remote(theirs, theirs, sems, u, sibling).wait_recv()
        for cp in sent:
            cp.wait_send()

    return _comm_call_multi(body, "share_halves", [jax.ShapeDtypeStruct(u.shape, u.dtype) for u in units], n, units,
                            aliases={u: u for u in range(n)})


def _allgather_rep(rep4):
    def body(r_ref, o_ref, send_sems, recv_sems):
        x, y, c, me, sibling, chips = _place()
        sems = (send_sems, recv_sems)
        sent = [_remote(r_ref, o_ref.at[me], sems, j, (*chip, c)) for j, chip in enumerate(chips)]
        for cp in sent:
            cp.start()
        for j, (cx, cy) in enumerate(chips):
            slot = o_ref.at[2 * cx + cy]
            _remote(slot, slot, sems, j, sibling).wait_recv()
        for cp in sent:
            cp.wait_send()

    return _comm_call(body, "allgather_rep", jax.ShapeDtypeStruct((N_CHIPS,) + rep4.shape, rep4.dtype), 3, rep4)


ADAM_TILE = 64


def _add_halves(mine, theirs, place, name):
    rows, width = theirs.shape[1:]
    tr = min(rows, 256)

    def body(place_ref, a_ref, b_ref, o_ref):
        o_ref[...] = (a_ref[...].astype(F32) + b_ref[...].astype(F32)).astype(o_ref.dtype)

    blk = pl.BlockSpec((None, tr, width), lambda k, i, place_ref: (k, i, 0))
    return pl.pallas_call(
        body, name=name,
        grid_spec=pltpu.PrefetchScalarGridSpec(
            num_scalar_prefetch=1, grid=(N_CHIPS, rows // tr),
            in_specs=[pl.BlockSpec((None, None, tr, width), lambda k, i, place_ref: (k, place_ref[1], i, 0)), blk], out_specs=blk),
        out_shape=jax.ShapeDtypeStruct(theirs.shape, theirs.dtype), compiler_params=_cparams("parallel", "parallel"),
    )(place, mine, theirs)


def _adamw(place, parts, own, w, m, v, layer, name, into=None):
    rows, width = parts.shape[1:]
    layers = w.shape[0]

    def body(place_ref, p_ref, own_ref, w_ref, m_ref, v_ref, *rest):
        o_ref = rest[-1]
        g = jnp.zeros((ADAM_TILE, width), F32)
        for k in range(N_CHIPS):
            g = g + jnp.where(place_ref[0] == k, own_ref[...], p_ref[k]).astype(F32)
        m2 = ADAM_B1 * m_ref[...] + (1.0 - ADAM_B1) * g
        v2 = ADAM_B2 * v_ref[...] + (1.0 - ADAM_B2) * jnp.square(g)
        m_hat = m2 / (1.0 - ADAM_B1 ** ADAM_STEP)
        v_hat = v2 / (1.0 - ADAM_B2 ** ADAM_STEP)
        o_ref[0] = g
        o_ref[1] = -ADAM_LR * (m_hat / (jnp.sqrt(v_hat) + ADAM_EPS) + ADAM_WD * w_ref[...])
        o_ref[2] = m2
        o_ref[3] = v2

    blk = pl.BlockSpec((None, None, ADAM_TILE, width), lambda i, place_ref: (layer, place_ref[1], i, 0))
    blk4 = pl.BlockSpec((4, ADAM_TILE, width), lambda i, place_ref: (0, i, 0))
    mine = pl.BlockSpec((None, ADAM_TILE, width), lambda i, place_ref: (place_ref[0], i, 0))
    out = pl.BlockSpec((4, None, None, ADAM_TILE, width), lambda i, place_ref: (0, layer, place_ref[1], i, 0))
    extra = [] if into is None else [into]
    return pl.pallas_call(
        body, name=name,
        grid_spec=pltpu.PrefetchScalarGridSpec(num_scalar_prefetch=1, grid=(rows // ADAM_TILE,),
                                               in_specs=[blk4, mine, blk, blk, blk] + [ANY] * len(extra), out_specs=out),
        out_shape=jax.ShapeDtypeStruct((4, layers, 2, rows, width), F32),
        input_output_aliases={} if into is None else {6: 0}, compiler_params=_cparams("parallel"),
    )(place, parts, own, w, m, v, *extra)


def kernel(x, norm_gain, a_w_in, a_v_gain, a_w_s, a_b_s, a_w_out, b_w_in, b_q_gain, b_k_gain, b_w_out, c_w_in, c_w_grp, c_scale, c_w_out, loss_target, m_norm_gain, m_a_w_in, m_a_v_gain, m_a_w_s, m_a_b_s, m_a_w_out, m_b_w_in, m_b_q_gain, m_b_k_gain, m_b_w_out, m_c_w_in, m_c_w_grp, m_c_scale, m_c_w_out, v_norm_gain, v_a_w_in, v_a_v_gain, v_a_w_s, v_a_b_s, v_a_w_out, v_b_w_in, v_b_q_gain, v_b_k_gain, v_b_w_out, v_c_w_in, v_c_w_grp, v_c_scale, v_c_w_out):
    wts = dict(norm_gain=norm_gain, a_w_in=a_w_in, a_v_gain=a_v_gain, a_w_s=a_w_s, a_b_s=a_b_s, a_w_out=a_w_out, b_w_in=b_w_in,
               b_q_gain=b_q_gain, b_k_gain=b_k_gain, b_w_out=b_w_out, c_w_in=c_w_in, c_w_grp=c_w_grp, c_scale=c_scale, c_w_out=c_w_out)
    mom1 = dict(norm_gain=m_norm_gain, a_w_in=m_a_w_in, a_v_gain=m_a_v_gain, a_w_s=m_a_w_s, a_b_s=m_a_b_s, a_w_out=m_a_w_out,
                b_w_in=m_b_w_in, b_q_gain=m_b_q_gain, b_k_gain=m_b_k_gain, b_w_out=m_b_w_out, c_w_in=m_c_w_in, c_w_grp=m_c_w_grp,
                c_scale=m_c_scale, c_w_out=m_c_w_out)
    mom2 = dict(norm_gain=v_norm_gain, a_w_in=v_a_w_in, a_v_gain=v_a_v_gain, a_w_s=v_a_w_s, a_b_s=v_a_b_s, a_w_out=v_a_w_out,
                b_w_in=v_b_w_in, b_q_gain=v_b_q_gain, b_k_gain=v_b_k_gain, b_w_out=v_b_w_out, c_w_in=v_c_w_in, c_w_grp=v_c_w_grp,
                c_scale=v_c_scale, c_w_out=v_c_w_out)
    axes = ("x", "y", "c")
    me = 2 * lax.axis_index("x") + lax.axis_index("y")
    core = lax.axis_index("c")

    place = jnp.stack([me, core]).astype(jnp.int32)

    slots = _gather_inputs(wts, me)
    first = _allgather_units([slots[u] for u in FIRST_UNITS])
    full = _gathered_weights(FIRST_UNITS, first)
    later = sorted(LATER_UNITS)
    gather_sems, gather_units, token = _gather_start([[slots[u] for u in LATER_UNITS[li]] for li in later], first[-1])
    for n, _ in REP:
        full[n] = wts[n]
    full["norm_gain"] = wts["norm_gain"] + token[0, 0]

    def layer_weights(li, x_in):
        if li not in LATER_UNITS:
            return {}
        g = later.index(li)
        landed = _gather_wait(gather_units[g], gather_sems[g], x_in, f"gather_wait_l{li}")
        return _gathered_weights(LATER_UNITS[li], _forward_units(landed, f"gather_forward_l{li}"))

    def unit_grad(u, grads):
        n, layer, r, w = REDUCE_UNITS[u]
        g = grads[n][layer] if n in ("a_w_in", "a_w_out") else grads[n]
        return g.astype(CDT).reshape(N_CHIPS, 2, r, w)

    chip_sums, in_flight = {}, []

    def start_exchange(li, grads):
        if li not in EARLY_UNITS:
            return jnp.zeros((), F32)
        units = EARLY_UNITS[li]
        mine = [unit_grad(u, grads) for u in units]
        theirs = _swap_halves(mine, f"grad_swap_halves_l{li}")
        sums = [_add_halves(g, t, place, f"grad_add_halves{u}") for u, g, t in zip(units, mine, theirs)]
        chip_sums.update(zip(units, sums))
        send_sems, recv_sems, srcs, lands, token = _scatter_start(sums, f"grad_scatter_start_l{li}")
        in_flight.append((li, units, send_sems, recv_sems, srcs, lands))
        return token[0, 0]

    sq, grad_x, grads = _local_step(x[0], loss_target[0], full, start_exchange, layer_weights)
    loss = lax.psum(0.5 * jnp.sum(sq) / D_MODEL, axes)

    rep_g = _pack_rep(grads)
    late = [u for u in range(len(REDUCE_UNITS)) if not any(u in us for us in EARLY_UNITS.values())]
    mine = [unit_grad(u, grads) for u in late]
    mine.append(jnp.stack([
        _small_unit([lax.slice_in_dim(grads[v], 512 * k, 512 * (k + 1), axis=1) for v in SMALL_OF_HALF],
                    rep_g[k * REP_PART:(k + 1) * REP_PART], CDT) for k in range(N_CHIPS)]))
    late.append(len(REDUCE_UNITS))
    theirs = _swap_halves(mine, "grad_swap_halves_late")
    sums = [_add_halves(g, t, place, f"grad_add_halves{u}") for u, g, t in zip(late, mine, theirs)]
    chip_sums.update(zip(late, sums))
    parts = dict(zip(late, _scatter_shards(sums)))
    for li, units, send_sems, recv_sems, srcs, lands in in_flight:
        parts.update(zip(units, _scatter_wait(send_sems, recv_sems, srcs, lands, grad_x, f"grad_scatter_wait_l{li}")))
    order = range(len(REDUCE_UNITS) + 1)
    parts, chip_sums = [parts[u] for u in order], [chip_sums[u] for u in order]

    def state_units(tree):
        units = [tree[n].reshape(BLOCK_SHAPES[n][0], 2, r, w) for n, _, r, w in REDUCE_UNITS]
        rep_part = lax.dynamic_slice_in_dim(_pack_rep(tree), me * REP_PART, REP_PART, axis=0)
        return units + [_small_unit([tree[v] for v in SMALL_OF_HALF], rep_part, F32)[None]]

    names = [n for n, _, _, _ in REDUCE_UNITS] + ["small"]
    layers = [layer for _, layer, _, _ in REDUCE_UNITS] + [0]
    res = {}
    for u, (p, s, w_, m_, v_) in enumerate(zip(parts, chip_sums, state_units(wts), state_units(mom1), state_units(mom2))):
        res[names[u]] = _adamw(place, p, s, w_, m_, v_, layers[u], f"adamw{u}", into=res.get(names[u]))
    res = dict(zip(res, _share_halves(list(res.values()))))
    small = res.pop("small")[:, 0]
    rep_mine = jnp.concatenate([small[:, 0, SMALL_PAD:], small[:, 1, SMALL_PAD:]], axis=1)
    rep_all = lax.dynamic_update_slice_in_dim(_allgather_rep(rep_mine), rep_mine[None], me, axis=0)

    outs = []
    for q in range(4):
        tree = {n: arr[q].reshape(BLOCK_SHAPES[n]) for n, arr in res.items()}
        tree["a_v_gain"] = small[q, 0, 0].reshape(2, 512)
        tree["c_scale"] = small[q, 1, 0, :512].reshape(1, 512)
        tree.update(_unpack_rep(rep_all[:, q].reshape(N_CHIPS * REP_PART, ROW)))
        outs.append(tree)
    return (loss, grad_x[None], *[t[n] for t in outs for n in WEIGHTS])
```

```python
import functools
import math

import jax
import jax.numpy as jnp
from jax import lax
from jax.experimental import pallas as pl
from jax.experimental.pallas import tpu as pltpu

F32 = jnp.float32
CDT = jnp.bfloat16

D_MODEL = 1024
EPS = 1e-6
CHUNK = 128
A_WIDTH = 2048
A_GROUPS = 8
A_GROUP_DIM = 256
HEAD_DIM = 128
B_HEADS = 8
B_DILATIONS = (1, 4, 16)
B_QK = 6144
B_IN = 10240
ROPE_HALF = 16
ROPE_THETA = 500000.0
POOL_SIZES = (2, 4, 8, 16)
POOL_HALO = 16
C_WIDTH = 2048
C_GROUP = 512
N_CHIPS = 4

ADAM_LR = 0.001
ADAM_B1 = 0.9
ADAM_B2 = 0.999
ADAM_EPS = 1e-08
ADAM_WD = 0.01
ADAM_STEP = 10

VMEM_LIMIT = 48 * 1024 * 1024
ANY = pl.BlockSpec(memory_space=pl.ANY)
MESH = pl.DeviceIdType.MESH

NN = (((1,), (0,)), ((), ()))
NT = (((1,), (1,)), ((), ()))
TN = (((0,), (0,)), ((), ()))


def _cparams(*sem):
    return pltpu.CompilerParams(dimension_semantics=sem, vmem_limit_bytes=VMEM_LIMIT)


def _dot(a, b, dims=NN):
    return lax.dot_general(a, b, dims, preferred_element_type=F32)


def _sigmoid(z):
    return 1.0 / (1.0 + jnp.exp(-z))


def _lane_sums(v):
    ones = jnp.ones((HEAD_DIM, HEAD_DIM), jnp.bfloat16)
    hi = v.astype(jnp.bfloat16)
    lo = (v - hi.astype(F32)).astype(jnp.bfloat16)
    return _dot(hi, ones) + _dot(lo, ones)


def _mm(a, b, mode, out_dtype, name, mnk, tiles, b_spec=None, o_spec=None, o_shape=None, add=None):
    m, n, k = mnk
    tm, tn, tk = min(tiles[0], m), min(tiles[1], n), min(tiles[2], k)
    nk = k // tk
    a_spec = {"nn": pl.BlockSpec((tm, tk), lambda i, j, q: (i, q)),
              "nt": pl.BlockSpec((tm, tk), lambda i, j, q: (i, q)),
              "tn": pl.BlockSpec((tk, tm), lambda i, j, q: (q, i))}[mode]
    if b_spec is None:
        b_spec = {"nn": pl.BlockSpec((tk, tn), lambda i, j, q: (q, j)),
                  "nt": pl.BlockSpec((tn, tk), lambda i, j, q: (j, q)),
                  "tn": pl.BlockSpec((tk, tn), lambda i, j, q: (q, j))}[mode]
    if o_spec is None:
        o_spec, o_shape = pl.BlockSpec((tm, tn), lambda i, j, q: (i, j)), (m, n)
    dims = {"nn": NN, "nt": NT, "tn": TN}[mode]
    has_add = add is not None

    def body(*refs):
        a_ref, b_ref = refs[0], refs[1]
        o_ref = refs[3] if has_add else refs[2]
        p = _dot(a_ref[...], b_ref[...], dims)

        def finish(v):
            if has_add:
                v = v + refs[2][...]
            o_ref[...] = v.astype(o_ref.dtype)

        if nk == 1:
            finish(p)
        else:
            acc_ref = refs[-1]
            q = pl.program_id(2)

            @pl.when(q == 0)
            def _():
                acc_ref[...] = p

            @pl.when(q > 0)
            def _():
                acc_ref[...] += p

            @pl.when(q == nk - 1)
            def _():
                finish(acc_ref[...])

    in_specs = [a_spec, b_spec]
    args = [a, b]
    if has_add:
        in_specs.append(pl.BlockSpec((tm, tn), lambda i, j, q: (i, j)))
        args.append(add)
    return pl.pallas_call(
        body, name=name, grid=(m // tm, n // tn, nk), in_specs=in_specs, out_specs=o_spec,
        out_shape=jax.ShapeDtypeStruct(o_shape, out_dtype),
        scratch_shapes=[pltpu.VMEM((tm, tn), F32)] if nk > 1 else [],
        compiler_params=_cparams("parallel", "parallel", "arbitrary"),
    )(*args)


def _rms_fwd(x, g, name, tq=512):
    s, d = x.shape

    def body(x_ref, g_ref, h_ref):
        xv = x_ref[...]
        r = lax.rsqrt(jnp.mean(xv * xv, axis=-1, keepdims=True) + EPS)
        h_ref[...] = (xv * r * g_ref[...]).astype(h_ref.dtype)

    return pl.pallas_call(
        body, name=name, grid=(s // tq,),
        in_specs=[pl.BlockSpec((tq, d), lambda i: (i, 0)), pl.BlockSpec((1, d), lambda i: (0, 0))],
        out_specs=pl.BlockSpec((tq, d), lambda i: (i, 0)),
        out_shape=jax.ShapeDtypeStruct((s, d), CDT), compiler_params=_cparams("parallel"),
    )(x, g)


def _loss_bwd(y, target, name, tq=512):
    s, d = y.shape

    def body(y_ref, t_ref, dx_ref, dxc_ref, sq_ref):
        err = y_ref[...] - t_ref[...]
        dx = err * (1.0 / d)
        dx_ref[...] = dx
        dxc_ref[...] = dx.astype(dxc_ref.dtype)

        @pl.when(pl.program_id(0) == 0)
        def _():
            sq_ref[...] = jnp.zeros_like(sq_ref)

        sq_ref[...] += jnp.sum(err * err, axis=0, keepdims=True)

    blk = pl.BlockSpec((tq, d), lambda i: (i, 0))
    vec = pl.BlockSpec((1, d), lambda i: (0, 0))
    return pl.pallas_call(
        body, name=name, grid=(s // tq,), in_specs=[blk, blk], out_specs=[blk, blk, vec],
        out_shape=[jax.ShapeDtypeStruct((s, d), F32), jax.ShapeDtypeStruct((s, d), CDT), jax.ShapeDtypeStruct((1, d), F32)],
        compiler_params=_cparams("arbitrary"),
    )(y, target)


def _tril_mask():
    row = lax.broadcasted_iota(jnp.int32, (CHUNK, CHUNK), 0)
    col = lax.broadcasted_iota(jnp.int32, (CHUNK, CHUNK), 1)
    return row >= col


def _a_mid_fwd(proj, v_gain, w_s, b_s_t, name, tq=256):
    s = proj.shape[0]

    def body(p_ref, vg_ref, ws_ref, bs_ref, y_ref):
        vraw = p_ref[:, A_WIDTH:2 * A_WIDTH].astype(F32)
        r = lax.rsqrt(jnp.mean(vraw * vraw, axis=-1, keepdims=True) + EPS)
        vn = (vraw * r * vg_ref[...]).astype(CDT)
        tri = _tril_mask()
        for g in range(A_GROUPS):
            w = jnp.where(tri, ws_ref[g], 0.0).astype(CDT)
            bias = bs_ref[:, g:g + 1]
            cols = slice(g * A_GROUP_DIM, (g + 1) * A_GROUP_DIM)
            zcols = slice(2 * A_WIDTH + g * A_GROUP_DIM, 2 * A_WIDTH + (g + 1) * A_GROUP_DIM)
            for c in range(tq // CHUNK):
                rows = slice(c * CHUNK, (c + 1) * CHUNK)
                mixed = _dot(w, vn[rows, cols]) + bias
                u = p_ref[rows, cols].astype(F32)
                z = p_ref[rows, zcols].astype(F32)
                y_ref[rows, cols] = (u * mixed * (z * _sigmoid(z))).astype(y_ref.dtype)

    return pl.pallas_call(
        body, name=name, grid=(s // tq,),
        in_specs=[pl.BlockSpec((tq, 3 * A_WIDTH), lambda i: (i, 0)), pl.BlockSpec((1, A_WIDTH), lambda i: (0, 0)),
                  pl.BlockSpec((A_GROUPS, CHUNK, CHUNK), lambda i: (0, 0, 0)), pl.BlockSpec((CHUNK, A_GROUPS), lambda i: (0, 0))],
        out_specs=pl.BlockSpec((tq, A_WIDTH), lambda i: (i, 0)),
        out_shape=jax.ShapeDtypeStruct((s, A_WIDTH), CDT), compiler_params=_cparams("parallel"),
    )(proj, v_gain, w_s, b_s_t)


def _a_mid_bwd(proj, dy, v_gain, w_s, b_s_t, name, tq=256):
    s = proj.shape[0]

    def body(p_ref, dy_ref, vg_ref, ws_ref, bs_ref, dp_ref, dws_ref, dbs_ref, dvg_ref, dvn_ref):
        @pl.when(pl.program_id(0) == 0)
        def _():
            dws_ref[...] = jnp.zeros_like(dws_ref)
            dbs_ref[...] = jnp.zeros_like(dbs_ref)
            dvg_ref[...] = jnp.zeros_like(dvg_ref)

        vraw = p_ref[:, A_WIDTH:2 * A_WIDTH].astype(F32)
        r = lax.rsqrt(jnp.mean(vraw * vraw, axis=-1, keepdims=True) + EPS)
        vhat = vraw * r
        vg = vg_ref[...]
        vn = (vhat * vg).astype(CDT)
        tri = _tril_mask()
        lane = lax.broadcasted_iota(jnp.int32, (CHUNK, A_GROUPS), 1)
        dbs = jnp.zeros((CHUNK, A_GROUPS), F32)
        for g in range(A_GROUPS):
            w = jnp.where(tri, ws_ref[g], 0.0).astype(CDT)
            bias = bs_ref[:, g:g + 1]
            cols = slice(g * A_GROUP_DIM, (g + 1) * A_GROUP_DIM)
            zcols = slice(2 * A_WIDTH + g * A_GROUP_DIM, 2 * A_WIDTH + (g + 1) * A_GROUP_DIM)
            dws = jnp.zeros((CHUNK, CHUNK), F32)
            for c in range(tq // CHUNK):
                rows = slice(c * CHUNK, (c + 1) * CHUNK)
                vn_g = vn[rows, cols]
                mixed = _dot(w, vn_g) + bias
                u = p_ref[rows, cols].astype(F32)
                z = p_ref[rows, zcols].astype(F32)
                dyv = dy_ref[rows, cols].astype(F32)
                sg = _sigmoid(z)
                sz = z * sg
                dyu = dyv * u
                dmixed = dyu * sz
                dp_ref[rows, cols] = (dyv * mixed * sz).astype(dp_ref.dtype)
                dp_ref[rows, zcols] = (dyu * mixed * (sg * (1.0 + z * (1.0 - sg)))).astype(dp_ref.dtype)
                dmc = dmixed.astype(CDT)
                dws = dws + _dot(dmc, vn_g, NT)
                dbs = dbs + jnp.where(lane == g, jnp.sum(dmixed, axis=-1, keepdims=True), 0.0)
                dvn_ref[rows, cols] = _dot(w, dmc, TN)
            dws_ref[g] += jnp.where(tri, dws, 0.0)
        dbs_ref[...] += dbs
        dvn = dvn_ref[...]
        gd = dvn * vg
        dvraw = r * gd - vraw * (r * r * r) * jnp.mean(gd * vraw, axis=-1, keepdims=True)
        dp_ref[:, A_WIDTH:2 * A_WIDTH] = dvraw.astype(dp_ref.dtype)
        dvg_ref[...] += jnp.sum(dvn * vhat, axis=0, keepdims=True)

    return pl.pallas_call(
        body, name=name, grid=(s // tq,),
        in_specs=[pl.BlockSpec((tq, 3 * A_WIDTH), lambda i: (i, 0)), pl.BlockSpec((tq, A_WIDTH), lambda i: (i, 0)),
                  pl.BlockSpec((1, A_WIDTH), lambda i: (0, 0)), pl.BlockSpec((A_GROUPS, CHUNK, CHUNK), lambda i: (0, 0, 0)),
                  pl.BlockSpec((CHUNK, A_GROUPS), lambda i: (0, 0))],
        out_specs=[pl.BlockSpec((tq, 3 * A_WIDTH), lambda i: (i, 0)), pl.BlockSpec((A_GROUPS, CHUNK, CHUNK), lambda i: (0, 0, 0)),
                   pl.BlockSpec((CHUNK, A_GROUPS), lambda i: (0, 0)), pl.BlockSpec((1, A_WIDTH), lambda i: (0, 0))],
        out_shape=[jax.ShapeDtypeStruct((s, 3 * A_WIDTH), CDT), jax.ShapeDtypeStruct((A_GROUPS, CHUNK, CHUNK), F32),
                   jax.ShapeDtypeStruct((CHUNK, A_GROUPS), F32), jax.ShapeDtypeStruct((1, A_WIDTH), F32)],
        scratch_shapes=[pltpu.VMEM((tq, A_WIDTH), F32)],
        compiler_params=_cparams("arbitrary"),
    )(proj, dy, v_gain, w_s, b_s_t)


def _rope_tables(s):
    inv_freq = jnp.power(jnp.float32(ROPE_THETA), -jnp.arange(ROPE_HALF, dtype=F32) / ROPE_HALF)
    ang = jnp.arange(s, dtype=F32)[:, None] * inv_freq[None, :]
    cos, sin = jnp.cos(ang), jnp.sin(ang)
    rest = HEAD_DIM - 2 * ROPE_HALF
    t_c = jnp.concatenate([cos, cos, jnp.ones((s, rest), F32)], axis=1)
    t_a = jnp.concatenate([-sin, jnp.zeros((s, HEAD_DIM - ROPE_HALF), F32)], axis=1)
    t_b = jnp.concatenate([jnp.zeros((s, ROPE_HALF), F32), sin, jnp.zeros((s, rest), F32)], axis=1)
    return t_c, t_a, t_b


def _b_qk_fwd(proj, gains, tabs, name, tq=256):
    s = proj.shape[0]

    def body(p_ref, g_ref, tc_ref, ta_ref, tb_ref, o_ref):
        tc, ta, tb = tc_ref[...], ta_ref[...], tb_ref[...]
        for tg in range(6):
            gain = g_ref[tg:tg + 1, :]
            for h in range(B_HEADS):
                cols = slice(tg * 1024 + h * HEAD_DIM, tg * 1024 + (h + 1) * HEAD_DIM)
                xv = p_ref[:, cols].astype(F32)
                r = lax.rsqrt(_lane_sums(xv * xv) * (1.0 / HEAD_DIM) + EPS)
                xn = xv * r * gain
                y = xn * tc + pltpu.roll(xn, HEAD_DIM - ROPE_HALF, 1) * ta + pltpu.roll(xn, ROPE_HALF, 1) * tb
                o_ref[:, cols] = y.astype(o_ref.dtype)

    tab = pl.BlockSpec((tq, HEAD_DIM), lambda i: (i, 0))
    return pl.pallas_call(
        body, name=name, grid=(s // tq,),
        in_specs=[pl.BlockSpec((tq, B_QK), lambda i: (i, 0)), pl.BlockSpec((6, HEAD_DIM), lambda i: (0, 0)), tab, tab, tab],
        out_specs=pl.BlockSpec((tq, B_QK), lambda i: (i, 0)),
        out_shape=jax.ShapeDtypeStruct((s, B_QK), CDT), compiler_params=_cparams("parallel"),
    )(proj, gains, *tabs)


PERMUTE_BLOCK_BYTES = 4 * 1024 * 1024


def _view_rows(length, dil, width, itemsize):
    rows = 16
    while 2 * rows * dil * width * itemsize <= PERMUTE_BLOCK_BYTES and 2 * rows <= length:
        rows *= 2
    return rows


def _to_view(x, col, width, dil, name):
    s = x.shape[0]
    length = s // dil
    tl = _view_rows(length, dil, width, 4)
    lanes = HEAD_DIM
    nblk = width // lanes

    def body(x_ref, o_ref, slab_ref):
        for b in range(nblk):
            slab_ref[b] = x_ref[:, b * lanes:(b + 1) * lanes].astype(F32)
        for r in range(dil):
            for b in range(nblk):
                o_ref[:, r * width + b * lanes:r * width + (b + 1) * lanes] = (
                    slab_ref.at[b][pl.ds(r, tl, stride=dil), :].astype(o_ref.dtype))

    return pl.pallas_call(
        body, name=name, grid=(length // tl,),
        in_specs=[pl.BlockSpec((tl * dil, width), lambda i: (i, col))],
        out_specs=pl.BlockSpec((tl, dil * width), lambda i: (i, 0)),
        out_shape=jax.ShapeDtypeStruct((length, dil * width), x.dtype),
        scratch_shapes=[pltpu.VMEM((nblk, tl * dil, lanes), F32)],
        compiler_params=_cparams("parallel"),
    )(x)


def _from_view(v, dil, name):
    length, width = v.shape[0], v.shape[1] // dil
    tl = _view_rows(length, dil, width, 4)
    lanes = HEAD_DIM
    nblk = width // lanes

    def body(v_ref, o_ref, slab_ref):
        for r in range(dil):
            for b in range(nblk):
                slab_ref.at[b][pl.ds(r, tl, stride=dil), :] = v_ref[:, r * width + b * lanes:r * width + (b + 1) * lanes].astype(F32)
        for b in range(nblk):
            o_ref[:, b * lanes:(b + 1) * lanes] = slab_ref[b].astype(o_ref.dtype)

    return pl.pallas_call(
        body, name=name, grid=(length // tl,),
        in_specs=[pl.BlockSpec((tl, dil * width), lambda i: (i, 0))],
        out_specs=pl.BlockSpec((tl * dil, width), lambda i: (i, 0)),
        out_shape=jax.ShapeDtypeStruct((length * dil, width), v.dtype),
        scratch_shapes=[pltpu.VMEM((nblk, tl * dil, lanes), F32)],
        compiler_params=_cparams("parallel"),
    )(v)


def _b_attn_fwd(q, k, v, bases, dil, name):
    length = q.shape[0]
    nb = length // CHUNK
    scale = 1.0 / math.sqrt(HEAD_DIM)
    w = B_HEADS * HEAD_DIM
    qb, kb, vb = bases

    def body(q_ref, kc_ref, kp_ref, vc_ref, vp_ref, o_ref, lse_ref):
        n = pl.program_id(1)
        qi = lax.broadcasted_iota(jnp.int32, (CHUNK, 2 * CHUNK), 0)
        ki = lax.broadcasted_iota(jnp.int32, (CHUNK, 2 * CHUNK), 1)
        first_key = jnp.where(n > 0, 0, CHUNK)
        mask = (ki >= qi) & (ki <= qi + CHUNK) & (ki >= first_key)
        lane = lax.broadcasted_iota(jnp.int32, (CHUNK, HEAD_DIM), 1)
        lse_all = jnp.zeros((CHUNK, HEAD_DIM), F32)
        for h in range(B_HEADS):
            sl = slice(h * HEAD_DIM, (h + 1) * HEAD_DIM)
            k2 = jnp.concatenate([kp_ref[:, sl], kc_ref[:, sl]], axis=0)
            v2 = jnp.concatenate([vp_ref[:, sl], vc_ref[:, sl]], axis=0)
            sc = jnp.where(mask, _dot(q_ref[:, sl], k2, NT) * scale, -1e30)
            m = jnp.max(sc, axis=-1, keepdims=True)
            p = jnp.exp(sc - m)
            l = jnp.sum(p, axis=-1, keepdims=True)
            o_ref[:, sl] = _dot(p.astype(CDT), v2) / l
            lse_all = jnp.where(lane == h, m + jnp.log(l), lse_all)
        lse_ref[...] = lse_all

    prev = lambda n: jnp.maximum(n - 1, 0)
    blk = lambda f: pl.BlockSpec((CHUNK, w), f)
    return pl.pallas_call(
        body, name=name, grid=(dil, nb),
        in_specs=[blk(lambda r, n: (n, qb + r)), blk(lambda r, n: (n, kb + r)), blk(lambda r, n: (prev(n), kb + r)),
                  blk(lambda r, n: (n, vb + r)), blk(lambda r, n: (prev(n), vb + r))],
        out_specs=[blk(lambda r, n: (n, r)), pl.BlockSpec((CHUNK, HEAD_DIM), lambda r, n: (n, r))],
        out_shape=[jax.ShapeDtypeStruct((length, dil * w), F32), jax.ShapeDtypeStruct((length, dil * HEAD_DIM), F32)],
        compiler_params=_cparams("parallel", "parallel"),
    )(q, k, k, v, v)


def _b_combine(os_, lses, proj, name, tq=512):
    s = proj.shape[0]
    w = B_HEADS * HEAD_DIM

    def body(o0_ref, o1_ref, o2_ref, l0_ref, l1_ref, l2_ref, z_ref, y_ref, oj_ref, lj_ref):
        l0, l1, l2 = l0_ref[...], l1_ref[...], l2_ref[...]
        m = jnp.maximum(jnp.maximum(l0, l1), l2)
        lj = m + jnp.log(jnp.exp(l0 - m) + jnp.exp(l1 - m) + jnp.exp(l2 - m))
        lj_ref[...] = lj
        w0, w1, w2 = jnp.exp(l0 - lj), jnp.exp(l1 - lj), jnp.exp(l2 - lj)
        for h in range(B_HEADS):
            sl = slice(h * HEAD_DIM, (h + 1) * HEAD_DIM)
            o = w0[:, h:h + 1] * o0_ref[:, sl] + w1[:, h:h + 1] * o1_ref[:, sl] + w2[:, h:h + 1] * o2_ref[:, sl]
            z = z_ref[:, sl].astype(F32)
            oj_ref[:, sl] = o.astype(oj_ref.dtype)
            y_ref[:, sl] = (o * (z * _sigmoid(z))).astype(y_ref.dtype)

    blk = pl.BlockSpec((tq, w), lambda i: (i, 0))
    st = pl.BlockSpec((tq, HEAD_DIM), lambda i: (i, 0))
    return pl.pallas_call(
        body, name=name, grid=(s // tq,),
        in_specs=[blk, blk, blk, st, st, st, pl.BlockSpec((tq, w), lambda i: (i, 9))],
        out_specs=[blk, blk, st],
        out_shape=[jax.ShapeDtypeStruct((s, w), CDT), jax.ShapeDtypeStruct((s, w), CDT), jax.ShapeDtypeStruct((s, HEAD_DIM), F32)],
        compiler_params=_cparams("parallel"),
    )(*os_, *lses, proj)


def _b_bwd_prep(dy, oj, proj, name, tq=512):
    s = proj.shape[0]
    w = B_HEADS * HEAD_DIM

    def body(dy_ref, oj_ref, z_ref, do_ref, dz_ref, dd_ref):
        lane = lax.broadcasted_iota(jnp.int32, (tq, HEAD_DIM), 1)
        dd = jnp.zeros((tq, HEAD_DIM), F32)
        for h in range(B_HEADS):
            sl = slice(h * HEAD_DIM, (h + 1) * HEAD_DIM)
            z = z_ref[:, sl].astype(F32)
            dyv = dy_ref[:, sl].astype(F32)
            o = oj_ref[:, sl].astype(F32)
            sg = _sigmoid(z)
            do = dyv * (z * sg)
            do_ref[:, sl] = do.astype(do_ref.dtype)
            dz_ref[:, sl] = (dyv * o * (sg * (1.0 + z * (1.0 - sg)))).astype(dz_ref.dtype)
            dd = jnp.where(lane == h, jnp.sum(do * o, axis=-1, keepdims=True), dd)
        dd_ref[...] = dd

    blk = pl.BlockSpec((tq, w), lambda i: (i, 0))
    st = pl.BlockSpec((tq, HEAD_DIM), lambda i: (i, 0))
    return pl.pallas_call(
        body, name=name, grid=(s // tq,),
        in_specs=[blk, blk, pl.BlockSpec((tq, w), lambda i: (i, 9))], out_specs=[blk, blk, st],
        out_shape=[jax.ShapeDtypeStruct((s, w), CDT), jax.ShapeDtypeStruct((s, w), CDT), jax.ShapeDtypeStruct((s, HEAD_DIM), F32)],
        compiler_params=_cparams("parallel"),
    )(dy, oj, proj)


def _b_attn_bwd(q, k, v, bases, do, lj, dd, dil, name):
    length = q.shape[0]
    nb = length // CHUNK
    scale = 1.0 / math.sqrt(HEAD_DIM)
    w = B_HEADS * HEAD_DIM
    qb, kb, vb = bases

    def body(qj_ref, qn_ref, k_ref, v_ref, doj_ref, don_ref, lj_ref, ln_ref, dj_ref, dn_ref, out_ref, carry_ref):
        j = pl.program_id(1)

        @pl.when(j == 0)
        def _():
            carry_ref[...] = jnp.zeros_like(carry_ref)

        qi = lax.broadcasted_iota(jnp.int32, (2 * CHUNK, CHUNK), 0)
        ki = lax.broadcasted_iota(jnp.int32, (2 * CHUNK, CHUNK), 1)
        no_next = jnp.where(j + 1 < nb, 0, 2 * CHUNK)
        mask = ((qi < CHUNK) & (ki <= qi)) | ((qi >= CHUNK) & (ki >= qi - CHUNK + no_next))
        for h in range(B_HEADS):
            sl = slice(h * HEAD_DIM, (h + 1) * HEAD_DIM)
            q2 = jnp.concatenate([qj_ref[:, sl], qn_ref[:, sl]], axis=0)
            do2 = jnp.concatenate([doj_ref[:, sl], don_ref[:, sl]], axis=0)
            lse2 = jnp.concatenate([lj_ref[:, h:h + 1], ln_ref[:, h:h + 1]], axis=0)
            d2 = jnp.concatenate([dj_ref[:, h:h + 1], dn_ref[:, h:h + 1]], axis=0)
            k = k_ref[:, sl]
            v = v_ref[:, sl]
            sc = _dot(q2, k, NT) * scale
            p = jnp.where(mask, jnp.exp(sc - lse2), 0.0)
            dp = _dot(do2, v, NT)
            ds = (p * (dp - d2) * scale).astype(CDT)
            dq2 = _dot(ds, k)
            out_ref[:, sl] = (carry_ref[:, sl] + dq2[:CHUNK]).astype(out_ref.dtype)
            carry_ref[:, sl] = dq2[CHUNK:]
            out_ref[:, w + h * HEAD_DIM:w + (h + 1) * HEAD_DIM] = _dot(ds, q2, TN).astype(out_ref.dtype)
            out_ref[:, 2 * w + h * HEAD_DIM:2 * w + (h + 1) * HEAD_DIM] = _dot(p.astype(CDT), do2, TN).astype(out_ref.dtype)

    nxt = lambda j: jnp.minimum(j + 1, nb - 1)
    blk = lambda f: pl.BlockSpec((CHUNK, w), f)
    st = lambda f: pl.BlockSpec((CHUNK, HEAD_DIM), f)
    return pl.pallas_call(
        body, name=name, grid=(dil, nb),
        in_specs=[blk(lambda r, j: (j, qb + r)), blk(lambda r, j: (nxt(j), qb + r)),
                  blk(lambda r, j: (j, kb + r)), blk(lambda r, j: (j, vb + r)),
                  blk(lambda r, j: (j, r)), blk(lambda r, j: (nxt(j), r)),
                  st(lambda r, j: (j, r)), st(lambda r, j: (nxt(j), r)), st(lambda r, j: (j, r)), st(lambda r, j: (nxt(j), r))],
        out_specs=pl.BlockSpec((CHUNK, 3 * w), lambda r, j: (j, r)),
        out_shape=jax.ShapeDtypeStruct((length, dil * 3 * w), CDT),
        scratch_shapes=[pltpu.VMEM((CHUNK, w), F32)],
        compiler_params=_cparams("parallel", "arbitrary"),
    )(q, q, k, v, do, do, lj, lj, dd, dd)


def _b_qk_bwd(proj, dqkv, dz, gains, tabs, name, tq=256):
    s = proj.shape[0]
    w = B_HEADS * HEAD_DIM

    def body(p_ref, d0_ref, d1_ref, d2_ref, dz_ref, g_ref, tc_ref, ta_ref, tb_ref, dp_ref, dg_ref):
        @pl.when(pl.program_id(0) == 0)
        def _():
            dg_ref[...] = jnp.zeros_like(dg_ref)

        tc, ta, tb = tc_ref[...], ta_ref[...], tb_ref[...]
        d_refs = (d0_ref, d1_ref, d2_ref)
        for g in range(3):
            for t in range(2):
                tg = t * 3 + g
                gain = g_ref[tg:tg + 1, :]
                dgain = jnp.zeros((1, HEAD_DIM), F32)
                for h in range(B_HEADS):
                    cols = slice(tg * w + h * HEAD_DIM, tg * w + (h + 1) * HEAD_DIM)
                    xv = p_ref[:, cols].astype(F32)
                    r = lax.rsqrt(_lane_sums(xv * xv) * (1.0 / HEAD_DIM) + EPS)
                    dyv = d_refs[g][:, t * w + h * HEAD_DIM:t * w + (h + 1) * HEAD_DIM].astype(F32)
                    dxn = dyv * tc + pltpu.roll(dyv * ta, ROPE_HALF, 1) + pltpu.roll(dyv * tb, HEAD_DIM - ROPE_HALF, 1)
                    gd = dxn * gain
                    dx = r * gd - xv * (r * r * r) * (_lane_sums(gd * xv) * (1.0 / HEAD_DIM))
                    dp_ref[:, cols] = dx.astype(dp_ref.dtype)
                    dgain = dgain + jnp.sum(dxn * xv * r, axis=0, keepdims=True)
                dg_ref[tg:tg + 1, :] += dgain
            dp_ref[:, (6 + g) * w:(7 + g) * w] = d_refs[g][:, 2 * w:3 * w]
        dp_ref[:, 9 * w:10 * w] = dz_ref[...]

    tab = pl.BlockSpec((tq, HEAD_DIM), lambda i: (i, 0))
    dblk = pl.BlockSpec((tq, 3 * w), lambda i: (i, 0))
    return pl.pallas_call(
        body, name=name, grid=(s // tq,),
        in_specs=[pl.BlockSpec((tq, B_QK), lambda i: (i, 0)), dblk, dblk, dblk, pl.BlockSpec((tq, w), lambda i: (i, 0)),
                  pl.BlockSpec((6, HEAD_DIM), lambda i: (0, 0)), tab, tab, tab],
        out_specs=[pl.BlockSpec((tq, B_IN), lambda i: (i, 0)), pl.BlockSpec((6, HEAD_DIM), lambda i: (0, 0))],
        out_shape=[jax.ShapeDtypeStruct((s, B_IN), CDT), jax.ShapeDtypeStruct((6, HEAD_DIM), F32)],
        compiler_params=_cparams("arbitrary"),
    )(proj, *dqkv, dz, gains, *tabs)


def _inv_count(t, window):
    return 1.0 / jnp.minimum(t + 1, window).astype(F32)


POOL_BLOCK = 128


def _band(window, forward):
    r = lax.broadcasted_iota(jnp.int32, (POOL_BLOCK, 2 * POOL_BLOCK), 0)
    j = lax.broadcasted_iota(jnp.int32, (POOL_BLOCK, 2 * POOL_BLOCK), 1)
    if forward:
        return jnp.where((j >= r) & (j < r + window), 1.0, 0.0).astype(CDT)
    return jnp.where((j <= r + POOL_BLOCK) & (j > r + POOL_BLOCK - window), 1.0, 0.0).astype(CDT)


def _split_dot(band, v):
    hi = v.astype(jnp.bfloat16)
    lo = (v - hi.astype(F32)).astype(jnp.bfloat16)
    band = band.astype(jnp.bfloat16)
    return _dot(band, hi) + _dot(band, lo)


def _pool_diff(x_ref, halo_ref, diff_ref, i, tq):
    t = i * tq + lax.broadcasted_iota(jnp.int32, (tq, 1), 0)
    for g, window in enumerate(POOL_SIZES):
        cols = slice(g * C_GROUP, (g + 1) * C_GROUP)
        band = _band(window, forward=False)
        inv = _inv_count(t, window)
        for b in range(tq // POOL_BLOCK):
            rows = slice(b * POOL_BLOCK, (b + 1) * POOL_BLOCK)
            cur = x_ref[rows, cols]
            if b == 0:
                above = jnp.where(i > 0, halo_ref[:, cols], jnp.zeros_like(cur))
            else:
                above = x_ref[(b - 1) * POOL_BLOCK:b * POOL_BLOCK, cols]
            pooled = _dot(band, jnp.concatenate([above, cur], axis=0)) * inv[rows]
            diff_ref[rows, cols] = (pooled - cur.astype(F32)).astype(diff_ref.dtype)


GRP_SHARD = (N_CHIPS, 2, 256, C_GROUP)
GRP_ROWS = C_GROUP // N_CHIPS


def _grp_rows(g):
    return g // 2, slice((g % 2) * GRP_ROWS, (g % 2 + 1) * GRP_ROWS)


def _grp_weight(w_ref, g):
    half, rows = _grp_rows(g)
    return jnp.concatenate([w_ref[k, half, rows, :] for k in range(N_CHIPS)], axis=0)


def _c_mid_fwd(proj, w_grp, scale, name, tq=512):
    s = proj.shape[0]
    per = tq // POOL_BLOCK

    def body(x_ref, halo_ref, z_ref, w_ref, sc_ref, y_ref, diff_ref):
        _pool_diff(x_ref, halo_ref, diff_ref, pl.program_id(0), tq)
        for g in range(len(POOL_SIZES)):
            cols = slice(g * C_GROUP, (g + 1) * C_GROUP)
            z = z_ref[:, cols].astype(F32)
            y_ref[:, cols] = (_dot(diff_ref[:, cols], _grp_weight(w_ref, g)) * sc_ref[:, cols] * (z * _sigmoid(z))).astype(y_ref.dtype)

    return pl.pallas_call(
        body, name=name, grid=(s // tq,),
        in_specs=[pl.BlockSpec((tq, C_WIDTH), lambda i: (i, 0)),
                  pl.BlockSpec((POOL_BLOCK, C_WIDTH), lambda i: (jnp.maximum(i * per - 1, 0), 0)),
                  pl.BlockSpec((tq, C_WIDTH), lambda i: (i, 1)),
                  pl.BlockSpec(GRP_SHARD, lambda i: (0, 0, 0, 0)), pl.BlockSpec((1, C_WIDTH), lambda i: (0, 0))],
        out_specs=pl.BlockSpec((tq, C_WIDTH), lambda i: (i, 0)),
        out_shape=jax.ShapeDtypeStruct((s, C_WIDTH), CDT),
        scratch_shapes=[pltpu.VMEM((tq, C_WIDTH), CDT)], compiler_params=_cparams("parallel"),
    )(proj, proj, proj, w_grp, scale)


def _c_mid_bwd(proj, w_grp, scale, dy, name, tq=512):
    s = proj.shape[0]
    per = tq // POOL_BLOCK

    def body(x_ref, halo_ref, z_ref, w_ref, sc_ref, dy_ref, dd_ref, dz_ref, dw_ref, dsc_ref, diff_ref):
        @pl.when(pl.program_id(0) == 0)
        def _():
            dw_ref[...] = jnp.zeros_like(dw_ref)
            dsc_ref[...] = jnp.zeros_like(dsc_ref)

        _pool_diff(x_ref, halo_ref, diff_ref, pl.program_id(0), tq)
        for g in range(len(POOL_SIZES)):
            cols = slice(g * C_GROUP, (g + 1) * C_GROUP)
            d = diff_ref[:, cols]
            wg = _grp_weight(w_ref, g)
            half, rows = _grp_rows(g)
            m0 = _dot(d, wg)
            z = z_ref[:, cols].astype(F32)
            dyv = dy_ref[:, cols].astype(F32)
            sc = sc_ref[:, cols]
            sg = _sigmoid(z)
            dmixed = dyv * (z * sg)
            dz_ref[:, cols] = (dyv * m0 * sc * (sg * (1.0 + z * (1.0 - sg)))).astype(dz_ref.dtype)
            dsc_ref[:, cols] += jnp.sum(dmixed * m0, axis=0, keepdims=True)
            dm0 = (dmixed * sc).astype(CDT)
            dwg = _dot(d, dm0, TN)
            for k in range(N_CHIPS):
                dw_ref[k, half, rows, :] += dwg[k * GRP_ROWS:(k + 1) * GRP_ROWS]
            dd_ref[:, cols] = _dot(dm0, wg, NT)

    blk = pl.BlockSpec((tq, C_WIDTH), lambda i: (i, 0))
    wsp = pl.BlockSpec(GRP_SHARD, lambda i: (0, 0, 0, 0))
    vec = pl.BlockSpec((1, C_WIDTH), lambda i: (0, 0))
    return pl.pallas_call(
        body, name=name, grid=(s // tq,),
        in_specs=[blk, pl.BlockSpec((POOL_BLOCK, C_WIDTH), lambda i: (jnp.maximum(i * per - 1, 0), 0)),
                  pl.BlockSpec((tq, C_WIDTH), lambda i: (i, 1)), wsp, vec, blk],
        out_specs=[blk, blk, wsp, vec],
        out_shape=[jax.ShapeDtypeStruct((s, C_WIDTH), F32), jax.ShapeDtypeStruct((s, C_WIDTH), CDT),
                   jax.ShapeDtypeStruct(GRP_SHARD, F32), jax.ShapeDtypeStruct((1, C_WIDTH), F32)],
        scratch_shapes=[pltpu.VMEM((tq, C_WIDTH), CDT)], compiler_params=_cparams("arbitrary"),
    )(proj, proj, proj, w_grp, scale, dy)


def _c_pool_bwd(ddiff, dz, name, tq=512):
    s = ddiff.shape[0]
    per = tq // POOL_BLOCK
    last = s // tq - 1

    def body(d_ref, halo_ref, dz_ref, o_ref):
        i = pl.program_id(0)
        t = i * tq + lax.broadcasted_iota(jnp.int32, (tq, 1), 0)
        for g, window in enumerate(POOL_SIZES):
            cols = slice(g * C_GROUP, (g + 1) * C_GROUP)
            band = _band(window, forward=True)
            inv = _inv_count(t, window)
            for b in range(tq // POOL_BLOCK):
                rows = slice(b * POOL_BLOCK, (b + 1) * POOL_BLOCK)
                cur = d_ref[rows, cols]
                if b == tq // POOL_BLOCK - 1:
                    below = jnp.where(i < last, halo_ref[:, cols] * (1.0 / window), 0.0)
                else:
                    nxt = slice((b + 1) * POOL_BLOCK, (b + 2) * POOL_BLOCK)
                    below = d_ref[nxt, cols] * inv[nxt]
                summed = _split_dot(band, jnp.concatenate([cur * inv[rows], below], axis=0))
                o_ref[rows, cols] = (summed - cur).astype(o_ref.dtype)
        o_ref[:, C_WIDTH:] = dz_ref[...]

    return pl.pallas_call(
        body, name=name, grid=(s // tq,),
        in_specs=[pl.BlockSpec((tq, C_WIDTH), lambda i: (i, 0)),
                  pl.BlockSpec((POOL_BLOCK, C_WIDTH), lambda i: (jnp.minimum((i + 1) * per, s // POOL_BLOCK - 1), 0)),
                  pl.BlockSpec((tq, C_WIDTH), lambda i: (i, 0))],
        out_specs=pl.BlockSpec((tq, 2 * C_WIDTH), lambda i: (i, 0)),
        out_shape=jax.ShapeDtypeStruct((s, 2 * C_WIDTH), CDT),
        compiler_params=_cparams("parallel"),
    )(ddiff, ddiff, dz)


LAYERS = (("a", 0), ("b", 0), ("c", 0), ("a", 1))


def _shard_spec(block, layer, where):
    if layer is None:
        return pl.BlockSpec((None,) + block, where)
    return pl.BlockSpec((None, None) + block, lambda i, j, q: (where(i, j, q)[0], layer) + where(i, j, q)[1:])


def _w_in_fwd(h, gw, layer, name):
    s, width = h.shape[0], gw.shape[-1]
    rows = 2048 if width <= 1024 else 1024
    return _mm(h, gw, "nn", CDT, name, (s, N_CHIPS * width, D_MODEL), (rows, width, D_MODEL),
               b_spec=_shard_spec((D_MODEL, width), layer, lambda i, j, q: (j, 0, 0)))


def _w_in_dh(dproj, gw, x, dxo, gain, name):
    s, width = dproj.shape[0], gw.shape[-1]
    tm = 1024 if width <= 1024 else 512

    def body(a_ref, b_ref, x_ref, dxo_ref, g_ref, dx_ref, dxc_ref, dg_ref, acc_ref):
        i, q = pl.program_id(0), pl.program_id(1)
        p = _dot(a_ref[...], b_ref[...], NT)

        @pl.when(q == 0)
        def _():
            acc_ref[...] = p

        @pl.when(q > 0)
        def _():
            acc_ref[...] += p

        @pl.when((i == 0) & (q == 0))
        def _():
            dg_ref[...] = jnp.zeros_like(dg_ref)

        @pl.when(q == N_CHIPS - 1)
        def _():
            dh = acc_ref[...]
            xv = x_ref[...]
            r = lax.rsqrt(jnp.mean(xv * xv, axis=-1, keepdims=True) + EPS)
            gd = dh * g_ref[...]
            dx = dxo_ref[...] + r * gd - xv * (r * r * r) * jnp.mean(gd * xv, axis=-1, keepdims=True)
            dx_ref[...] = dx
            dxc_ref[...] = dx.astype(dxc_ref.dtype)
            dg_ref[...] += jnp.sum(dh * xv * r, axis=0, keepdims=True)

    blk = pl.BlockSpec((tm, D_MODEL), lambda i, q: (i, 0))
    vec = pl.BlockSpec((1, D_MODEL), lambda i, q: (0, 0))
    return pl.pallas_call(
        body, name=name, grid=(s // tm, N_CHIPS),
        in_specs=[pl.BlockSpec((tm, width), lambda i, q: (i, q)), pl.BlockSpec((None, D_MODEL, width), lambda i, q: (q, 0, 0)),
                  blk, blk, vec],
        out_specs=[blk, blk, vec],
        out_shape=[jax.ShapeDtypeStruct((s, D_MODEL), F32), jax.ShapeDtypeStruct((s, D_MODEL), CDT), jax.ShapeDtypeStruct((1, D_MODEL), F32)],
        scratch_shapes=[pltpu.VMEM((tm, D_MODEL), F32)], compiler_params=_cparams("arbitrary", "arbitrary"),
    )(dproj, gw, x, dxo, gain)


def _w_in_grad(h, dproj, width, name):
    s = h.shape[0]
    tokens = 2048 if width <= 1536 else 1024
    return _mm(h, dproj, "tn", CDT, name, (D_MODEL, N_CHIPS * width, s), (D_MODEL, width, tokens),
               o_spec=pl.BlockSpec((None, D_MODEL, width), lambda i, j, q: (j, 0, 0)), o_shape=(N_CHIPS, D_MODEL, width))


def _w_out_spec(gw, layer):
    rows = gw.shape[-2]
    if layer is None:
        return pl.BlockSpec((N_CHIPS, rows, D_MODEL), lambda i: (0, 0, 0))
    return pl.BlockSpec((N_CHIPS, None, rows, D_MODEL), lambda i: (0, layer, 0, 0))


def _w_out_fwd(y, gw, layer, x, name, tm=1024):
    s, k = y.shape

    def body(y_ref, w_ref, x_ref, o_ref):
        o_ref[...] = x_ref[...] + _dot(y_ref[...], w_ref[...].reshape(k, D_MODEL))

    blk = pl.BlockSpec((tm, D_MODEL), lambda i: (i, 0))
    return pl.pallas_call(
        body, name=name, grid=(s // tm,), in_specs=[pl.BlockSpec((tm, k), lambda i: (i, 0)), _w_out_spec(gw, layer), blk],
        out_specs=blk, out_shape=jax.ShapeDtypeStruct((s, D_MODEL), F32), compiler_params=_cparams("parallel"),
    )(y, gw, x)


def _w_out_dy(dxc, gw, layer, name, tm=1024):
    s, k = dxc.shape[0], N_CHIPS * gw.shape[-2]
    tm = min(tm, s)

    def body(dx_ref, w_ref, o_ref):
        o_ref[...] = _dot(dx_ref[...], w_ref[...].reshape(k, D_MODEL), NT).astype(o_ref.dtype)

    return pl.pallas_call(
        body, name=name, grid=(s // tm,), in_specs=[pl.BlockSpec((tm, D_MODEL), lambda i: (i, 0)), _w_out_spec(gw, layer)],
        out_specs=pl.BlockSpec((tm, k), lambda i: (i, 0)), out_shape=jax.ShapeDtypeStruct((s, k), CDT),
        compiler_params=_cparams("parallel"),
    )(dxc, gw)


def _w_out_grad(y, dxc, rows, name):
    s, k = y.shape
    tokens = min(2048 if k <= 1024 else 1024, s)
    steps = s // tokens

    def body(y_ref, dx_ref, o_ref, acc_ref):
        i = pl.program_id(0)
        p = _dot(y_ref[...], dx_ref[...], TN)

        @pl.when(i == 0)
        def _():
            acc_ref[...] = p

        @pl.when(i > 0)
        def _():
            acc_ref[...] += p

        @pl.when(i == steps - 1)
        def _():
            o_ref[...] = acc_ref[...].reshape(N_CHIPS, rows, D_MODEL).astype(o_ref.dtype)

    return pl.pallas_call(
        body, name=name, grid=(steps,),
        in_specs=[pl.BlockSpec((tokens, k), lambda i: (i, 0)), pl.BlockSpec((tokens, D_MODEL), lambda i: (i, 0))],
        out_specs=pl.BlockSpec((N_CHIPS, rows, D_MODEL), lambda i: (0, 0, 0)),
        out_shape=jax.ShapeDtypeStruct((N_CHIPS, rows, D_MODEL), CDT),
        scratch_shapes=[pltpu.VMEM((k, D_MODEL), F32)], compiler_params=_cparams("arbitrary"),
    )(y, dxc)


def _local_step(x, target, w, on_grads=None, layer_weights=None):
    s = x.shape[0]
    tabs = _rope_tables(s)
    qk_gains = jnp.concatenate([w["b_q_gain"][0], w["b_k_gain"][0]], axis=0)
    saved = []
    for li, (kind, j) in enumerate(LAYERS):
        if layer_weights is not None:
            w = {**w, **layer_weights(li, x)}
        h = _rms_fwd(x, w["norm_gain"][li:li + 1], f"rms_fwd{li}")
        if kind == "a":
            proj = _w_in_fwd(h, w[f"a_w_in{j}"], None, f"a_in{li}")
            bs_t = w["a_b_s"][j].T
            y = _a_mid_fwd(proj, w["a_v_gain"][j:j + 1], w["a_w_s"][j], bs_t, f"a_mid_fwd{li}")
            x_next = _w_out_fwd(y, w[f"a_w_out{j}"], None, x, f"a_out{li}")
            saved.append((x, h, proj, y))
        elif kind == "b":
            proj = _w_in_fwd(h, w["b_w_in"], None, f"b_in{li}")
            qk = _b_qk_fwd(proj, qk_gains, tabs, f"b_qk_fwd{li}")
            qkv, os_, lses = [], [], []
            for g, dil in enumerate(B_DILATIONS):
                if dil == 1:
                    ops = (qk, qk, proj, (g, 3 + g, 6 + g))
                else:
                    ops = (_to_view(qk, g, 1024, dil, f"b_q_view{li}_{g}"), _to_view(qk, 3 + g, 1024, dil, f"b_k_view{li}_{g}"),
                           _to_view(proj, 6 + g, 1024, dil, f"b_v_view{li}_{g}"), (0, 0, 0))
                o, lse = _b_attn_fwd(*ops, dil, f"b_attn_fwd{li}_{g}")
                if dil > 1:
                    o, lse = _from_view(o, dil, f"b_o_nat{li}_{g}"), _from_view(lse, dil, f"b_lse_nat{li}_{g}")
                qkv.append(ops)
                os_.append(o)
                lses.append(lse)
            y, oj, lj = _b_combine(os_, lses, proj, f"b_combine{li}")
            x_next = _w_out_fwd(y, w["b_w_out"], None, x, f"b_out{li}")
            saved.append((x, h, proj, y, qkv, oj, lj))
        else:
            proj = _w_in_fwd(h, w["c_w_in"], None, f"c_in{li}")
            y = _c_mid_fwd(proj, w["c_w_grp"], w["c_scale"][j:j + 1], f"c_mid_fwd{li}")
            x_next = _w_out_fwd(y, w["c_w_out"], None, x, f"c_out{li}")
            saved.append((x, h, proj, y))
        x = x_next

    dx, dxc, sq = _loss_bwd(x, target, "loss_bwd")
    grads = {"norm_gain": [None] * len(LAYERS), "a_w_in": [None, None], "a_v_gain": [None, None], "a_w_s": [None, None],
             "a_b_s": [None, None], "a_w_out": [None, None]}
    for li in reversed(range(len(LAYERS))):
        kind, j = LAYERS[li]
        sv = saved[li]
        xin, h, proj, y = sv[:4]
        if kind == "a":
            grads["a_w_out"][j] = _w_out_grad(y, dxc, 512, f"a_dwout{li}")
            dy = _w_out_dy(dxc, w[f"a_w_out{j}"], None, f"a_dy{li}")
            dproj, dws, dbs_t, dvg = _a_mid_bwd(proj, dy, w["a_v_gain"][j:j + 1], w["a_w_s"][j], w["a_b_s"][j].T, f"a_mid_bwd{li}")
            grads["a_w_s"][j], grads["a_b_s"][j], grads["a_v_gain"][j] = dws, dbs_t.T, dvg[0]
            grads["a_w_in"][j] = _w_in_grad(h, dproj, 1536, f"a_dwin{li}")
            w_in = w[f"a_w_in{j}"]
        elif kind == "b":
            qkv, oj, lj = sv[4:]
            grads["b_w_out"] = _w_out_grad(y, dxc, 256, f"b_dwout{li}")
            dy = _w_out_dy(dxc, w["b_w_out"], None, f"b_dy{li}")
            do, dz, dd = _b_bwd_prep(dy, oj, proj, f"b_bwd_prep{li}")
            dqkv = []
            for g, dil in enumerate(B_DILATIONS):
                stats = (do, lj, dd)
                if dil > 1:
                    stats = (_to_view(do, 0, 1024, dil, f"b_do_view{li}_{g}"), _to_view(lj, 0, HEAD_DIM, dil, f"b_lj_view{li}_{g}"),
                             _to_view(dd, 0, HEAD_DIM, dil, f"b_dd_view{li}_{g}"))
                d = _b_attn_bwd(*qkv[g], *stats, dil, f"b_attn_bwd{li}_{g}")
                dqkv.append(_from_view(d, dil, f"b_dqkv_nat{li}_{g}") if dil > 1 else d)
            dproj, dgains = _b_qk_bwd(proj, dqkv, dz, qk_gains, tabs, f"b_qk_bwd{li}")
            grads["b_q_gain"], grads["b_k_gain"] = dgains[None, :3], dgains[None, 3:]
            grads["b_w_in"] = _w_in_grad(h, dproj, 2560, f"b_dwin{li}")
            w_in = w["b_w_in"]
        else:
            grads["c_w_out"] = _w_out_grad(y, dxc, 512, f"c_dwout{li}")
            dy = _w_out_dy(dxc, w["c_w_out"], None, f"c_dy{li}")
            ddiff, dz, dwg, dsc = _c_mid_bwd(proj, w["c_w_grp"], w["c_scale"][j:j + 1], dy, f"c_mid_bwd{li}")
            grads["c_w_grp"], grads["c_scale"] = dwg, dsc
            dproj = _c_pool_bwd(ddiff, dz, f"c_pool_bwd{li}")
            grads["c_w_in"] = _w_in_grad(h, dproj, 1024, f"c_dwin{li}")
            w_in = w["c_w_in"]
        gain = w["norm_gain"][li:li + 1]
        if on_grads is not None:
            gain = gain + on_grads(li, grads)
        dx, dxc, dng = _w_in_dh(dproj, w_in, xin, dx, gain, f"{kind}_dh{li}")
        grads["norm_gain"][li] = dng[0]
    for name in ("norm_gain", "a_v_gain", "a_w_s", "a_b_s"):
        grads[name] = jnp.stack(grads[name])
    return sq, dx, grads


ROW = 1024
REDUCE_UNITS = (("a_w_in", 0, 512, 1536), ("a_w_in", 1, 512, 1536), ("a_w_out", 0, 256, 1024), ("a_w_out", 1, 256, 1024),
                ("b_w_in", 0, 512, 2560), ("b_w_out", 0, 128, 1024), ("c_w_in", 0, 512, 1024), ("c_w_grp", 0, 256, 512),
                ("c_w_out", 0, 256, 1024))
GAIN_ROWS = 16
GAINS = len(REDUCE_UNITS)
LAYER_UNITS = {0: (0, 2), 1: (4, 5), 2: (6, 7, 8), 3: (1, 3)}
EARLY_UNITS = LAYER_UNITS
FIRST_UNITS = LAYER_UNITS[0] + (GAINS,)
LATER_UNITS = {li: LAYER_UNITS[li] for li in (1, 2, 3)}
SMALL_OF_HALF = ("a_v_gain", "c_scale")
SMALL_PAD = 16
REP = (("norm_gain", (4, 1024)), ("a_w_s", (2, 8, 128, 128)), ("a_b_s", (2, 8, 128)), ("b_q_gain", (1, 3, 128)), ("b_k_gain", (1, 3, 128)))
REP_CORE = 48
REP_PART = 2 * REP_CORE
SMALL_ROWS = SMALL_PAD + REP_CORE
BLOCK_SHAPES = {"a_w_in": (2, 1024, 1536), "a_v_gain": (2, 512), "a_w_out": (2, 512, 1024), "b_w_in": (1, 1024, 2560),
                "b_w_out": (1, 256, 1024), "c_w_in": (1, 1024, 1024), "c_w_grp": (1, 4, 128, 512), "c_scale": (1, 512),
                "c_w_out": (1, 512, 1024)}
WEIGHTS = ("norm_gain", "a_w_in", "a_v_gain", "a_w_s", "a_b_s", "a_w_out", "b_w_in", "b_q_gain", "b_k_gain", "b_w_out",
           "c_w_in", "c_w_grp", "c_scale", "c_w_out")


def _rows(a, rows):
    a = a.reshape(-1)
    return jnp.pad(a, (0, rows * ROW - a.shape[0])).reshape(rows, ROW)


def _small_unit(vecs, rep_part, dtype):
    halves = [jnp.concatenate([_rows(vecs[h].astype(dtype), SMALL_PAD), rep_part[h * REP_CORE:(h + 1) * REP_CORE].astype(dtype)])
              for h in range(2)]
    return jnp.stack(halves)


def _pack_rep(tree):
    return _rows(jnp.concatenate([tree[n].astype(F32).reshape(-1) for n, _ in REP]), N_CHIPS * REP_PART)


def _unpack_rep(slab):
    flat, out, off = slab.reshape(-1), {}, 0
    for n, shape in REP:
        size = math.prod(shape)
        out[n] = flat[off:off + size].reshape(shape)
        off += size
    return out


def _into_slot(x, first, rows, me, name):
    width = x.shape[1]
    tr = min(rows, 256)

    def body(me_ref, x_ref, o_ref):
        o_ref[...] = x_ref[...].astype(o_ref.dtype)

    return pl.pallas_call(
        body, name=name,
        grid_spec=pltpu.PrefetchScalarGridSpec(
            num_scalar_prefetch=1, grid=(rows // tr,), in_specs=[pl.BlockSpec((tr, width), lambda i, me_ref: (first // tr + i, 0))],
            out_specs=pl.BlockSpec((None, tr, width), lambda i, me_ref: (me_ref[0], i, 0))),
        out_shape=jax.ShapeDtypeStruct((N_CHIPS, rows, width), CDT), compiler_params=_cparams("parallel"),
    )(me.reshape(1), x)


def _gather_inputs(wts, me):
    units = []
    for u, (n, layer, r, w) in enumerate(REDUCE_UNITS):
        slot = _into_slot(wts[n].reshape(-1, w), layer * 2 * r, 2 * r, me, f"slot{u}")
        units.append(slot.reshape(N_CHIPS, 2, r, w))
    gains = jnp.concatenate([wts["a_v_gain"].reshape(-1), wts["c_scale"].reshape(-1)])
    gains = _rows(lax.bitcast_convert_type(gains, CDT), 2 * GAIN_ROWS)
    return units + [_into_slot(gains, 0, 2 * GAIN_ROWS, me, "slot_gains").reshape(N_CHIPS, 2, GAIN_ROWS, ROW)]


def _gathered_weights(indices, units):
    out = {}
    for u, arr in zip(indices, units):
        if u == GAINS:
            gains = lax.bitcast_convert_type(arr[:, 0, :3].reshape(N_CHIPS, 1536, 2), F32)
            out["a_v_gain"] = jnp.concatenate([gains[k, :1024].reshape(2, 512) for k in range(N_CHIPS)], axis=1)
            out["c_scale"] = jnp.concatenate([gains[k, 1024:].reshape(1, 512) for k in range(N_CHIPS)], axis=1)
            continue
        n, layer, r, w = REDUCE_UNITS[u]
        if n == "c_w_grp":
            out[n] = arr
        else:
            out[n + str(layer) if n in ("a_w_in", "a_w_out") else n] = arr.reshape(N_CHIPS, 2 * r, w)
    return out


def _place():
    x, y, c = lax.axis_index("x"), lax.axis_index("y"), lax.axis_index("c")
    chips = [(1 - x, y), (x, 1 - y), (1 - x, 1 - y)]
    return x, y, c, 2 * x + y, (x, y, 1 - c), chips


def _remote(src, dst, sems, j, to):
    send_sems, recv_sems = sems
    return pltpu.make_async_remote_copy(src_ref=src, dst_ref=dst, send_sem=send_sems.at[j], recv_sem=recv_sems.at[j],
                                        device_id=to, device_id_type=MESH)


def _comm_call(body, name, out_shape, n_sems, *args):
    return pl.pallas_call(
        body, name=name, in_specs=[ANY] * len(args), out_specs=ANY, out_shape=out_shape,
        scratch_shapes=[pltpu.SemaphoreType.DMA((n_sems,)), pltpu.SemaphoreType.DMA((n_sems,))],
    )(*args)


def _comm_call_multi(body, name, out_shapes, n_sems, args, aliases=None):
    return pl.pallas_call(
        body, name=name, in_specs=[ANY] * len(args), out_specs=[ANY] * len(out_shapes), out_shape=out_shapes,
        scratch_shapes=[pltpu.SemaphoreType.DMA((n_sems,)), pltpu.SemaphoreType.DMA((n_sems,))],
        input_output_aliases=aliases or {},
    )(*args)


def _allgather_units(units):
    n = len(units)

    def body(*refs):
        outs, sems = refs[n:2 * n], (refs[2 * n], refs[2 * n + 1])
        x, y, c, me, sibling, chips = _place()
        first, passed = [], []
        for u, o_ref in enumerate(outs):
            for j, chip in enumerate(chips):
                first.append(_remote(o_ref.at[me, c], o_ref.at[me, c], sems, 6 * u + j, (*chip, c)))
                first[-1].start()
        for u, o_ref in enumerate(outs):
            for j, (cx, cy) in enumerate(chips):
                landed = o_ref.at[2 * cx + cy, c]
                _remote(landed, landed, sems, 6 * u + j, sibling).wait_recv()
                passed.append(_remote(landed, landed, sems, 6 * u + 3 + j, sibling))
                passed[-1].start()
        for u, o_ref in enumerate(outs):
            for j, (cx, cy) in enumerate(chips):
                landed = o_ref.at[2 * cx + cy, 1 - c]
                _remote(landed, landed, sems, 6 * u + 3 + j, sibling).wait_recv()
        for cp in first + passed:
            cp.wait_send()

    return _comm_call_multi(body, "allgather_first", [jax.ShapeDtypeStruct(u.shape, u.dtype) for u in units], 6 * n, units,
                            aliases={u: u for u in range(n)})


HBM = pl.BlockSpec(memory_space=pltpu.HBM)
SEM = pl.BlockSpec(memory_space=pltpu.SEMAPHORE)
SIDE_EFFECT = pltpu.SideEffectType.DATAFLOW_SIDE_EFFECTING


def _gather_start(groups, after):
    sizes = [len(g) for g in groups]
    units = [u for g in groups for u in g]
    n = len(units)

    def body(*refs):
        arrs, sems, token = refs[:n], refs[n + 1:n + 1 + 2 * len(groups)], refs[-1]
        x, y, c, me, sibling, chips = _place()
        at = 0
        for gi, size in enumerate(sizes):
            for u in range(size):
                mine = arrs[at + u].at[me, c]
                for j, chip in enumerate(chips):
                    _remote(mine, mine, (sems[2 * gi], sems[2 * gi + 1]), 3 * u + j, (*chip, c)).start()
            at += size
        token[...] = jnp.zeros_like(token)

    sem_shapes = [pltpu.SemaphoreType.DMA((3 * size,)) for size in sizes for _ in range(2)]
    outs = pl.pallas_call(
        body, name="gather_start",
        out_shape=(*sem_shapes, *[pltpu.HBM(u.shape, u.dtype) for u in units], jax.ShapeDtypeStruct((8, 128), F32)),
        in_specs=[HBM] * n + [ANY], out_specs=(*[SEM] * len(sem_shapes), *[HBM] * n, pl.BlockSpec(memory_space=pltpu.VMEM)),
        input_output_aliases={i: len(sem_shapes) + i for i in range(n)},
        compiler_params=pltpu.CompilerParams(has_side_effects=SIDE_EFFECT),
    )(*[pltpu.with_memory_space_constraint(u, pltpu.HBM) for u in units], after)
    sems, arrs = outs[:len(sem_shapes)], outs[len(sem_shapes):-1]
    bounds = [sum(sizes[:gi]) for gi in range(len(sizes) + 1)]
    return ([(sems[2 * gi], sems[2 * gi + 1]) for gi in range(len(sizes))],
            [list(arrs[bounds[gi]:bounds[gi + 1]]) for gi in range(len(sizes))], outs[-1])


def _gather_wait(units, sems, after, name):
    n = len(units)

    def body(*refs):
        arrs, send_sems, recv_sems = refs[:n], refs[n], refs[n + 1]
        x, y, c, me, sibling, chips = _place()
        for u in range(n):
            for j, (cx, cy) in enumerate(chips):
                cp = _remote(arrs[u].at[me, c], arrs[u].at[2 * cx + cy, c], (send_sems, recv_sems), 3 * u + j, (cx, cy, c))
                cp.wait_send()
                cp.wait_recv()

    outs = pl.pallas_call(
        body, name=name, out_shape=tuple(pltpu.HBM(u.shape, u.dtype) for u in units),
        in_specs=[HBM] * n + [SEM, SEM, ANY], out_specs=[HBM] * n, input_output_aliases={i: i for i in range(n)},
        compiler_params=pltpu.CompilerParams(has_side_effects=SIDE_EFFECT),
    )(*units, *sems, after)
    return list(outs)


def _forward_units(units, name):
    n = len(units)

    def body(*refs):
        outs, sems = refs[n:2 * n], (refs[2 * n], refs[2 * n + 1])
        x, y, c, me, sibling, chips = _place()
        passed = []
        for u, o_ref in enumerate(outs):
            for j, (cx, cy) in enumerate(chips):
                landed = o_ref.at[2 * cx + cy, c]
                passed.append(_remote(landed, landed, sems, 3 * u + j, sibling))
                passed[-1].start()
        for u, o_ref in enumerate(outs):
            for j, (cx, cy) in enumerate(chips):
                landed = o_ref.at[2 * cx + cy, 1 - c]
                _remote(landed, landed, sems, 3 * u + j, sibling).wait_recv()
        for cp in passed:
            cp.wait_send()

    return _comm_call_multi(body, name, [jax.ShapeDtypeStruct(u.shape, u.dtype) for u in units], 3 * n, units,
                            aliases={u: u for u in range(n)})


def _swap_halves(units, name):
    n = len(units)

    def body(*refs):
        ins, outs, sems = refs[:n], refs[n:2 * n], (refs[2 * n], refs[2 * n + 1])
        x, y, c, me, sibling, chips = _place()
        sent = [_remote(g_ref.at[:, 1 - c], o_ref, sems, u, sibling) for u, (g_ref, o_ref) in enumerate(zip(ins, outs))]
        for cp in sent:
            cp.start()
        for cp in sent:
            cp.wait()

    shapes = [jax.ShapeDtypeStruct((N_CHIPS,) + u.shape[2:], u.dtype) for u in units]
    return _comm_call_multi(body, name, shapes, n, units)


def _scatter_shards(units):
    n = len(units)

    def body(*refs):
        ins, outs, sems = refs[:n], refs[n:2 * n], (refs[2 * n], refs[2 * n + 1])
        x, y, c, me, sibling, chips = _place()
        sent = []
        for u, (s_ref, o_ref) in enumerate(zip(ins, outs)):
            for j, (cx, cy) in enumerate(chips):
                sent.append(_remote(s_ref.at[2 * cx + cy], o_ref.at[me], sems, 3 * u + j, (cx, cy, c)))
                sent[-1].start()
        for u, o_ref in enumerate(outs):
            for j, (cx, cy) in enumerate(chips):
                slot = o_ref.at[2 * cx + cy]
                _remote(slot, slot, sems, 3 * u + j, sibling).wait_recv()
        for cp in sent:
            cp.wait_send()

    return _comm_call_multi(body, "grad_scatter_shards", [jax.ShapeDtypeStruct(u.shape, u.dtype) for u in units], 3 * n, units)


def _scatter_start(units, name):
    n = len(units)

    def body(*refs):
        srcs, lands = refs[:n], refs[n:2 * n]
        send_sems, recv_sems, token = refs[2 * n], refs[2 * n + 1], refs[-1]
        x, y, c, me, sibling, chips = _place()
        for u in range(n):
            for j, (cx, cy) in enumerate(chips):
                _remote(srcs[u].at[2 * cx + cy], lands[u].at[me], (send_sems, recv_sems), 3 * u + j, (cx, cy, c)).start()
        token[...] = jnp.zeros_like(token)

    hbm = [pltpu.HBM(u.shape, u.dtype) for u in units]
    outs = pl.pallas_call(
        body, name=name,
        out_shape=(pltpu.SemaphoreType.DMA((3 * n,)), pltpu.SemaphoreType.DMA((3 * n,)), *hbm, *hbm, jax.ShapeDtypeStruct((8, 128), F32)),
        in_specs=[HBM] * (2 * n), out_specs=(SEM, SEM, *[HBM] * (2 * n), pl.BlockSpec(memory_space=pltpu.VMEM)),
        input_output_aliases={i: 2 + i for i in range(2 * n)},
        compiler_params=pltpu.CompilerParams(has_side_effects=SIDE_EFFECT),
    )(*[pltpu.with_memory_space_constraint(u, pltpu.HBM) for u in units],
      *[pltpu.with_memory_space_constraint(lax.empty(u.shape, u.dtype), pltpu.HBM) for u in units])
    return outs[0], outs[1], outs[2:2 + n], outs[2 + n:2 + 2 * n], outs[-1]


def _scatter_wait(send_sems, recv_sems, srcs, lands, after, name):
    n = len(srcs)

    def body(*refs):
        srcs_, lands_, send_sems_, recv_sems_ = refs[:n], refs[n:2 * n], refs[2 * n], refs[2 * n + 1]
        x, y, c, me, sibling, chips = _place()
        for u in range(n):
            for j, (cx, cy) in enumerate(chips):
                slot = lands_[u].at[2 * cx + cy]
                cp = _remote(srcs_[u].at[2 * cx + cy], slot, (send_sems_, recv_sems_), 3 * u + j, (cx, cy, c))
                cp.wait_send()
                cp.wait_recv()

    hbm = [pltpu.HBM(u.shape, u.dtype) for u in srcs]
    outs = pl.pallas_call(
        body, name=name, out_shape=(*hbm, *hbm),
        in_specs=[HBM] * (2 * n) + [SEM, SEM, ANY], out_specs=[HBM] * (2 * n),
        input_output_aliases={i: i for i in range(2 * n)},
        compiler_params=pltpu.CompilerParams(has_side_effects=SIDE_EFFECT),
    )(*srcs, *lands, send_sems, recv_sems, after)
    return list(outs[n:])


def _share_halves(units):
    n = len(units)

    def body(*refs):
        outs, sems = refs[n:2 * n], (refs[2 * n], refs[2 * n + 1])
        x, y, c, me, sibling, chips = _place()
        sent = [_remote(o_ref.at[:, :, c], o_ref.at[:, :, c], sems, u, sibling) for u, o_ref in enumerate(outs)]
        for cp in sent:
            cp.start()
        for u, o_ref in enumerate(outs):
            theirs = o_ref.at[:, :, 1 - c]
            _remote(theirs, theirs, sems, u, sibling).wait_recv()
        for cp in sent:
            cp.wait_send()

    return _comm_call_multi(body, "share_halves", [jax.ShapeDtypeStruct(u.shape, u.dtype) for u in units], n, units,
                            aliases={u: u for u in range(n)})


def _share_start(units, name):
    n = len(units)

    def body(*refs):
        arrs, send_sems, recv_sems, token = refs[:n], refs[n], refs[n + 1], refs[-1]
        x, y, c, me, sibling, chips = _place()
        for u in range(n):
            _remote(arrs[u].at[:, :, c], arrs[u].at[:, :, c], (send_sems, recv_sems), u, sibling).start()
        token[...] = jnp.zeros_like(token)

    hbm = [pltpu.HBM(u.shape, u.dtype) for u in units]
    outs = pl.pallas_call(
        body, name=name,
        out_shape=(pltpu.SemaphoreType.DMA((n,)), pltpu.SemaphoreType.DMA((n,)), *hbm, jax.ShapeDtypeStruct((8, 128), F32)),
        in_specs=[HBM] * n, out_specs=(SEM, SEM, *[HBM] * n, pl.BlockSpec(memory_space=pltpu.VMEM)),
        input_output_aliases={i: 2 + i for i in range(n)},
        compiler_params=pltpu.CompilerParams(has_side_effects=SIDE_EFFECT),
    )(*[pltpu.with_memory_space_constraint(u, pltpu.HBM) for u in units])
    return outs[0], outs[1], list(outs[2:2 + n]), outs[-1]


def _share_wait(send_sems, recv_sems, units, after, name):
    n = len(units)

    def body(*refs):
        arrs, send_sems_, recv_sems_ = refs[:n], refs[n], refs[n + 1]
        x, y, c, me, sibling, chips = _place()
        for u in range(n):
            cp = _remote(arrs[u].at[:, :, c], arrs[u].at[:, :, 1 - c], (send_sems_, recv_sems_), u, sibling)
            cp.wait_send()
            cp.wait_recv()

    outs = pl.pallas_call(
        body, name=name, out_shape=tuple(pltpu.HBM(u.shape, u.dtype) for u in units),
        in_specs=[HBM] * n + [SEM, SEM, ANY], out_specs=[HBM] * n, input_output_aliases={i: i for i in range(n)},
        compiler_params=pltpu.CompilerParams(has_side_effects=SIDE_EFFECT),
    )(*units, send_sems, recv_sems, after)
    return list(outs)


def _allgather_rep(rep4):
    def body(r_ref, o_ref, send_sems, recv_sems):
        x, y, c, me, sibling, chips = _place()
        sems = (send_sems, recv_sems)
        sent = [_remote(r_ref, o_ref.at[me], sems, j, (*chip, c)) for j, chip in enumerate(chips)]
        for cp in sent:
            cp.start()
        for j, (cx, cy) in enumerate(chips):
            slot = o_ref.at[2 * cx + cy]
            _remote(slot, slot, sems, j, sibling).wait_recv()
        for cp in sent:
            cp.wait_send()

    return _comm_call(body, "allgather_rep", jax.ShapeDtypeStruct((N_CHIPS,) + rep4.shape, rep4.dtype), 3, rep4)


ADAM_TILE = 64


def _add_halves(mine, theirs, place, name):
    rows, width = theirs.shape[1:]
    tr = min(rows, 256)

    def body(place_ref, a_ref, b_ref, o_ref):
        o_ref[...] = (a_ref[...].astype(F32) + b_ref[...].astype(F32)).astype(o_ref.dtype)

    blk = pl.BlockSpec((None, tr, width), lambda k, i, place_ref: (k, i, 0))
    return pl.pallas_call(
        body, name=name,
        grid_spec=pltpu.PrefetchScalarGridSpec(
            num_scalar_prefetch=1, grid=(N_CHIPS, rows // tr),
            in_specs=[pl.BlockSpec((None, None, tr, width), lambda k, i, place_ref: (k, place_ref[1], i, 0)), blk], out_specs=blk),
        out_shape=jax.ShapeDtypeStruct(theirs.shape, theirs.dtype), compiler_params=_cparams("parallel", "parallel"),
    )(place, mine, theirs)


def _adamw(place, parts, own, w, m, v, layer, name, into=None):
    rows, width = parts.shape[1:]
    layers = w.shape[0]

    def body(place_ref, p_ref, own_ref, w_ref, m_ref, v_ref, *rest):
        o_ref = rest[-1]
        g = jnp.zeros((ADAM_TILE, width), F32)
        for k in range(N_CHIPS):
            g = g + jnp.where(place_ref[0] == k, own_ref[...], p_ref[k]).astype(F32)
        m2 = ADAM_B1 * m_ref[...] + (1.0 - ADAM_B1) * g
        v2 = ADAM_B2 * v_ref[...] + (1.0 - ADAM_B2) * jnp.square(g)
        m_hat = m2 / (1.0 - ADAM_B1 ** ADAM_STEP)
        v_hat = v2 / (1.0 - ADAM_B2 ** ADAM_STEP)
        o_ref[0] = g
        o_ref[1] = -ADAM_LR * (m_hat / (jnp.sqrt(v_hat) + ADAM_EPS) + ADAM_WD * w_ref[...])
        o_ref[2] = m2
        o_ref[3] = v2

    blk = pl.BlockSpec((None, None, ADAM_TILE, width), lambda i, place_ref: (layer, place_ref[1], i, 0))
    blk4 = pl.BlockSpec((4, ADAM_TILE, width), lambda i, place_ref: (0, i, 0))
    mine = pl.BlockSpec((None, ADAM_TILE, width), lambda i, place_ref: (place_ref[0], i, 0))
    out = pl.BlockSpec((4, None, None, ADAM_TILE, width), lambda i, place_ref: (0, layer, place_ref[1], i, 0))
    extra = [] if into is None else [into]
    return pl.pallas_call(
        body, name=name,
        grid_spec=pltpu.PrefetchScalarGridSpec(num_scalar_prefetch=1, grid=(rows // ADAM_TILE,),
                                               in_specs=[blk4, mine, blk, blk, blk] + [ANY] * len(extra), out_specs=out),
        out_shape=jax.ShapeDtypeStruct((4, layers, 2, rows, width), F32),
        input_output_aliases={} if into is None else {6: 0}, compiler_params=_cparams("parallel"),
    )(place, parts, own, w, m, v, *extra)


def kernel(x, norm_gain, a_w_in, a_v_gain, a_w_s, a_b_s, a_w_out, b_w_in, b_q_gain, b_k_gain, b_w_out, c_w_in, c_w_grp, c_scale, c_w_out, loss_target, m_norm_gain, m_a_w_in, m_a_v_gain, m_a_w_s, m_a_b_s, m_a_w_out, m_b_w_in, m_b_q_gain, m_b_k_gain, m_b_w_out, m_c_w_in, m_c_w_grp, m_c_scale, m_c_w_out, v_norm_gain, v_a_w_in, v_a_v_gain, v_a_w_s, v_a_b_s, v_a_w_out, v_b_w_in, v_b_q_gain, v_b_k_gain, v_b_w_out, v_c_w_in, v_c_w_grp, v_c_scale, v_c_w_out):
    wts = dict(norm_gain=norm_gain, a_w_in=a_w_in, a_v_gain=a_v_gain, a_w_s=a_w_s, a_b_s=a_b_s, a_w_out=a_w_out, b_w_in=b_w_in,
               b_q_gain=b_q_gain, b_k_gain=b_k_gain, b_w_out=b_w_out, c_w_in=c_w_in, c_w_grp=c_w_grp, c_scale=c_scale, c_w_out=c_w_out)
    mom1 = dict(norm_gain=m_norm_gain, a_w_in=m_a_w_in, a_v_gain=m_a_v_gain, a_w_s=m_a_w_s, a_b_s=m_a_b_s, a_w_out=m_a_w_out,
                b_w_in=m_b_w_in, b_q_gain=m_b_q_gain, b_k_gain=m_b_k_gain, b_w_out=m_b_w_out, c_w_in=m_c_w_in, c_w_grp=m_c_w_grp,
                c_scale=m_c_scale, c_w_out=m_c_w_out)
    mom2 = dict(norm_gain=v_norm_gain, a_w_in=v_a_w_in, a_v_gain=v_a_v_gain, a_w_s=v_a_w_s, a_b_s=v_a_b_s, a_w_out=v_a_w_out,
                b_w_in=v_b_w_in, b_q_gain=v_b_q_gain, b_k_gain=v_b_k_gain, b_w_out=v_b_w_out, c_w_in=v_c_w_in, c_w_grp=v_c_w_grp,
                c_scale=v_c_scale, c_w_out=v_c_w_out)
    axes = ("x", "y", "c")
    me = 2 * lax.axis_index("x") + lax.axis_index("y")
    core = lax.axis_index("c")

    place = jnp.stack([me, core]).astype(jnp.int32)

    slots = _gather_inputs(wts, me)
    first = _allgather_units([slots[u] for u in FIRST_UNITS])
    full = _gathered_weights(FIRST_UNITS, first)
    later = sorted(LATER_UNITS)
    gather_sems, gather_units, token = _gather_start([[slots[u] for u in LATER_UNITS[li]] for li in later], first[-1])
    for n, _ in REP:
        full[n] = wts[n]
    full["norm_gain"] = wts["norm_gain"] + token[0, 0]

    def layer_weights(li, x_in):
        if li not in LATER_UNITS:
            return {}
        g = later.index(li)
        landed = _gather_wait(gather_units[g], gather_sems[g], x_in, f"gather_wait_l{li}")
        return _gathered_weights(LATER_UNITS[li], _forward_units(landed, f"gather_forward_l{li}"))

    def unit_grad(u, grads):
        n, layer, r, w = REDUCE_UNITS[u]
        g = grads[n][layer] if n in ("a_w_in", "a_w_out") else grads[n]
        return g.astype(CDT).reshape(N_CHIPS, 2, r, w)

    chip_sums, in_flight = {}, []

    def start_exchange(li, grads):
        if li not in EARLY_UNITS:
            return jnp.zeros((), F32)
        units = EARLY_UNITS[li]
        mine = [unit_grad(u, grads) for u in units]
        theirs = _swap_halves(mine, f"grad_swap_halves_l{li}")
        sums = [_add_halves(g, t, place, f"grad_add_halves{u}") for u, g, t in zip(units, mine, theirs)]
        chip_sums.update(zip(units, sums))
        send_sems, recv_sems, srcs, lands, token = _scatter_start(sums, f"grad_scatter_start_l{li}")
        in_flight.append((li, units, send_sems, recv_sems, srcs, lands))
        return token[0, 0]

    sq, grad_x, grads = _local_step(x[0], loss_target[0], full, start_exchange, layer_weights)
    loss = lax.psum(0.5 * jnp.sum(sq) / D_MODEL, axes)

    rep_g = _pack_rep(grads)
    late = [u for u in range(len(REDUCE_UNITS)) if not any(u in us for us in EARLY_UNITS.values())]
    mine = [unit_grad(u, grads) for u in late]
    mine.append(jnp.stack([
        _small_unit([lax.slice_in_dim(grads[v], 512 * k, 512 * (k + 1), axis=1) for v in SMALL_OF_HALF],
                    rep_g[k * REP_PART:(k + 1) * REP_PART], CDT) for k in range(N_CHIPS)]))
    late.append(len(REDUCE_UNITS))
    theirs = _swap_halves(mine, "grad_swap_halves_late")
    sums = [_add_halves(g, t, place, f"grad_add_halves{u}") for u, g, t in zip(late, mine, theirs)]
    chip_sums.update(zip(late, sums))
    parts = dict(zip(late, _scatter_shards(sums)))
    for li, units, send_sems, recv_sems, srcs, lands in in_flight:
        parts.update(zip(units, _scatter_wait(send_sems, recv_sems, srcs, lands, grad_x, f"grad_scatter_wait_l{li}")))
    order = range(len(REDUCE_UNITS) + 1)
    parts, chip_sums = [parts[u] for u in order], [chip_sums[u] for u in order]

    def state_units(tree):
        units = [tree[n].reshape(BLOCK_SHAPES[n][0], 2, r, w) for n, _, r, w in REDUCE_UNITS]
        rep_part = lax.dynamic_slice_in_dim(_pack_rep(tree), me * REP_PART, REP_PART, axis=0)
        return units + [_small_unit([tree[v] for v in SMALL_OF_HALF], rep_part, F32)[None]]

    names = [n for n, _, _, _ in REDUCE_UNITS] + ["small"]
    layers = [layer for _, layer, _, _ in REDUCE_UNITS] + [0]
    res, state = {}, list(zip(parts, chip_sums, state_units(wts), state_units(mom1), state_units(mom2)))

    def update(units):
        for u in units:
            res[names[u]] = _adamw(place, *state[u], layers[u], f"adamw{u}", into=res.get(names[u]))

    update(LAYER_UNITS[1] + LAYER_UNITS[2])
    early = list(res)
    send_sems, recv_sems, early_arrs, _ = _share_start([res[n] for n in early], "share_start")
    update(LAYER_UNITS[0] + LAYER_UNITS[3] + (len(REDUCE_UNITS),))
    rest = [n for n in res if n not in early]
    res.update(zip(rest, _share_halves([res[n] for n in rest])))
    res.update(zip(early, _share_wait(send_sems, recv_sems, early_arrs, res[rest[0]], "share_wait")))
    small = res.pop("small")[:, 0]
    rep_mine = jnp.concatenate([small[:, 0, SMALL_PAD:], small[:, 1, SMALL_PAD:]], axis=1)
    rep_all = lax.dynamic_update_slice_in_dim(_allgather_rep(rep_mine), rep_mine[None], me, axis=0)

    outs = []
    for q in range(4):
        tree = {n: arr[q].reshape(BLOCK_SHAPES[n]) for n, arr in res.items()}
        tree["a_v_gain"] = small[q, 0, 0].reshape(2, 512)
        tree["c_scale"] = small[q, 1, 0, :512].reshape(1, 512)
        tree.update(_unpack_rep(rep_all[:, q].reshape(N_CHIPS * REP_PART, ROW)))
        outs.append(tree)
    return (loss, grad_x[None], *[t[n] for t in outs for n in WEIGHTS])
```

```python
import functools
import math

import jax
import jax.numpy as jnp
from jax import lax
from jax.experimental import pallas as pl
from jax.experimental.pallas import tpu as pltpu

F32 = jnp.float32
CDT = jnp.bfloat16

D_MODEL = 1024
EPS = 1e-6
CHUNK = 128
A_WIDTH = 2048
A_GROUPS = 8
A_GROUP_DIM = 256
HEAD_DIM = 128
B_HEADS = 8
B_DILATIONS = (1, 4, 16)
B_QK = 6144
B_IN = 10240
ROPE_HALF = 16
ROPE_THETA = 500000.0
POOL_SIZES = (2, 4, 8, 16)
POOL_HALO = 16
C_WIDTH = 2048
C_GROUP = 512
N_CHIPS = 4

ADAM_LR = 0.001
ADAM_B1 = 0.9
ADAM_B2 = 0.999
ADAM_EPS = 1e-08
ADAM_WD = 0.01
ADAM_STEP = 10

VMEM_LIMIT = 48 * 1024 * 1024
ANY = pl.BlockSpec(memory_space=pl.ANY)
MESH = pl.DeviceIdType.MESH

NN = (((1,), (0,)), ((), ()))
NT = (((1,), (1,)), ((), ()))
TN = (((0,), (0,)), ((), ()))


def _cparams(*sem):
    return pltpu.CompilerParams(dimension_semantics=sem, vmem_limit_bytes=VMEM_LIMIT)


def _dot(a, b, dims=NN):
    return lax.dot_general(a, b, dims, preferred_element_type=F32)


def _sigmoid(z):
    return 1.0 / (1.0 + jnp.exp(-z))


def _lane_sums(v):
    ones = jnp.ones((HEAD_DIM, HEAD_DIM), jnp.bfloat16)
    hi = v.astype(jnp.bfloat16)
    lo = (v - hi.astype(F32)).astype(jnp.bfloat16)
    return _dot(hi, ones) + _dot(lo, ones)


def _mm(a, b, mode, out_dtype, name, mnk, tiles, b_spec=None, o_spec=None, o_shape=None, add=None):
    m, n, k = mnk
    tm, tn, tk = min(tiles[0], m), min(tiles[1], n), min(tiles[2], k)
    nk = k // tk
    a_spec = {"nn": pl.BlockSpec((tm, tk), lambda i, j, q: (i, q)),
              "nt": pl.BlockSpec((tm, tk), lambda i, j, q: (i, q)),
              "tn": pl.BlockSpec((tk, tm), lambda i, j, q: (q, i))}[mode]
    if b_spec is None:
        b_spec = {"nn": pl.BlockSpec((tk, tn), lambda i, j, q: (q, j)),
                  "nt": pl.BlockSpec((tn, tk), lambda i, j, q: (j, q)),
                  "tn": pl.BlockSpec((tk, tn), lambda i, j, q: (q, j))}[mode]
    if o_spec is None:
        o_spec, o_shape = pl.BlockSpec((tm, tn), lambda i, j, q: (i, j)), (m, n)
    dims = {"nn": NN, "nt": NT, "tn": TN}[mode]
    has_add = add is not None

    def body(*refs):
        a_ref, b_ref = refs[0], refs[1]
        o_ref = refs[3] if has_add else refs[2]
        p = _dot(a_ref[...], b_ref[...], dims)

        def finish(v):
            if has_add:
                v = v + refs[2][...]
            o_ref[...] = v.astype(o_ref.dtype)

        if nk == 1:
            finish(p)
        else:
            acc_ref = refs[-1]
            q = pl.program_id(2)

            @pl.when(q == 0)
            def _():
                acc_ref[...] = p

            @pl.when(q > 0)
            def _():
                acc_ref[...] += p

            @pl.when(q == nk - 1)
            def _():
                finish(acc_ref[...])

    in_specs = [a_spec, b_spec]
    args = [a, b]
    if has_add:
        in_specs.append(pl.BlockSpec((tm, tn), lambda i, j, q: (i, j)))
        args.append(add)
    return pl.pallas_call(
        body, name=name, grid=(m // tm, n // tn, nk), in_specs=in_specs, out_specs=o_spec,
        out_shape=jax.ShapeDtypeStruct(o_shape, out_dtype),
        scratch_shapes=[pltpu.VMEM((tm, tn), F32)] if nk > 1 else [],
        compiler_params=_cparams("parallel", "parallel", "arbitrary"),
    )(*args)


def _rms_fwd(x, g, name, tq=512):
    s, d = x.shape

    def body(x_ref, g_ref, h_ref):
        xv = x_ref[...]
        r = lax.rsqrt(jnp.mean(xv * xv, axis=-1, keepdims=True) + EPS)
        h_ref[...] = (xv * r * g_ref[...]).astype(h_ref.dtype)

    return pl.pallas_call(
        body, name=name, grid=(s // tq,),
        in_specs=[pl.BlockSpec((tq, d), lambda i: (i, 0)), pl.BlockSpec((1, d), lambda i: (0, 0))],
        out_specs=pl.BlockSpec((tq, d), lambda i: (i, 0)),
        out_shape=jax.ShapeDtypeStruct((s, d), CDT), compiler_params=_cparams("parallel"),
    )(x, g)


def _loss_bwd(y, target, name, tq=512):
    s, d = y.shape

    def body(y_ref, t_ref, dx_ref, dxc_ref, sq_ref):
        err = y_ref[...] - t_ref[...]
        dx = err * (1.0 / d)
        dx_ref[...] = dx
        dxc_ref[...] = dx.astype(dxc_ref.dtype)

        @pl.when(pl.program_id(0) == 0)
        def _():
            sq_ref[...] = jnp.zeros_like(sq_ref)

        sq_ref[...] += jnp.sum(err * err, axis=0, keepdims=True)

    blk = pl.BlockSpec((tq, d), lambda i: (i, 0))
    vec = pl.BlockSpec((1, d), lambda i: (0, 0))
    return pl.pallas_call(
        body, name=name, grid=(s // tq,), in_specs=[blk, blk], out_specs=[blk, blk, vec],
        out_shape=[jax.ShapeDtypeStruct((s, d), F32), jax.ShapeDtypeStruct((s, d), CDT), jax.ShapeDtypeStruct((1, d), F32)],
        compiler_params=_cparams("arbitrary"),
    )(y, target)


def _tril_mask():
    row = lax.broadcasted_iota(jnp.int32, (CHUNK, CHUNK), 0)
    col = lax.broadcasted_iota(jnp.int32, (CHUNK, CHUNK), 1)
    return row >= col


def _a_mid_fwd(proj, v_gain, w_s, b_s_t, name, tq=256):
    s = proj.shape[0]

    def body(p_ref, vg_ref, ws_ref, bs_ref, y_ref):
        vraw = p_ref[:, A_WIDTH:2 * A_WIDTH].astype(F32)
        r = lax.rsqrt(jnp.mean(vraw * vraw, axis=-1, keepdims=True) + EPS)
        vn = (vraw * r * vg_ref[...]).astype(CDT)
        tri = _tril_mask()
        for g in range(A_GROUPS):
            w = jnp.where(tri, ws_ref[g], 0.0).astype(CDT)
            bias = bs_ref[:, g:g + 1]
            cols = slice(g * A_GROUP_DIM, (g + 1) * A_GROUP_DIM)
            zcols = slice(2 * A_WIDTH + g * A_GROUP_DIM, 2 * A_WIDTH + (g + 1) * A_GROUP_DIM)
            for c in range(tq // CHUNK):
                rows = slice(c * CHUNK, (c + 1) * CHUNK)
                mixed = _dot(w, vn[rows, cols]) + bias
                u = p_ref[rows, cols].astype(F32)
                z = p_ref[rows, zcols].astype(F32)
                y_ref[rows, cols] = (u * mixed * (z * _sigmoid(z))).astype(y_ref.dtype)

    return pl.pallas_call(
        body, name=name, grid=(s // tq,),
        in_specs=[pl.BlockSpec((tq, 3 * A_WIDTH), lambda i: (i, 0)), pl.BlockSpec((1, A_WIDTH), lambda i: (0, 0)),
                  pl.BlockSpec((A_GROUPS, CHUNK, CHUNK), lambda i: (0, 0, 0)), pl.BlockSpec((CHUNK, A_GROUPS), lambda i: (0, 0))],
        out_specs=pl.BlockSpec((tq, A_WIDTH), lambda i: (i, 0)),
        out_shape=jax.ShapeDtypeStruct((s, A_WIDTH), CDT), compiler_params=_cparams("parallel"),
    )(proj, v_gain, w_s, b_s_t)


def _a_mid_bwd(proj, dy, v_gain, w_s, b_s_t, name, tq=256):
    s = proj.shape[0]

    def body(p_ref, dy_ref, vg_ref, ws_ref, bs_ref, dp_ref, dws_ref, dbs_ref, dvg_ref, dvn_ref):
        @pl.when(pl.program_id(0) == 0)
        def _():
            dws_ref[...] = jnp.zeros_like(dws_ref)
            dbs_ref[...] = jnp.zeros_like(dbs_ref)
            dvg_ref[...] = jnp.zeros_like(dvg_ref)

        vraw = p_ref[:, A_WIDTH:2 * A_WIDTH].astype(F32)
        r = lax.rsqrt(jnp.mean(vraw * vraw, axis=-1, keepdims=True) + EPS)
        vhat = vraw * r
        vg = vg_ref[...]
        vn = (vhat * vg).astype(CDT)
        tri = _tril_mask()
        lane = lax.broadcasted_iota(jnp.int32, (CHUNK, A_GROUPS), 1)
        dbs = jnp.zeros((CHUNK, A_GROUPS), F32)
        for g in range(A_GROUPS):
            w = jnp.where(tri, ws_ref[g], 0.0).astype(CDT)
            bias = bs_ref[:, g:g + 1]
            cols = slice(g * A_GROUP_DIM, (g + 1) * A_GROUP_DIM)
            zcols = slice(2 * A_WIDTH + g * A_GROUP_DIM, 2 * A_WIDTH + (g + 1) * A_GROUP_DIM)
            dws = jnp.zeros((CHUNK, CHUNK), F32)
            for c in range(tq // CHUNK):
                rows = slice(c * CHUNK, (c + 1) * CHUNK)
                vn_g = vn[rows, cols]
                mixed = _dot(w, vn_g) + bias
                u = p_ref[rows, cols].astype(F32)
                z = p_ref[rows, zcols].astype(F32)
                dyv = dy_ref[rows, cols].astype(F32)
                sg = _sigmoid(z)
                sz = z * sg
                dyu = dyv * u
                dmixed = dyu * sz
                dp_ref[rows, cols] = (dyv * mixed * sz).astype(dp_ref.dtype)
                dp_ref[rows, zcols] = (dyu * mixed * (sg * (1.0 + z * (1.0 - sg)))).astype(dp_ref.dtype)
                dmc = dmixed.astype(CDT)
                dws = dws + _dot(dmc, vn_g, NT)
                dbs = dbs + jnp.where(lane == g, jnp.sum(dmixed, axis=-1, keepdims=True), 0.0)
                dvn_ref[rows, cols] = _dot(w, dmc, TN)
            dws_ref[g] += jnp.where(tri, dws, 0.0)
        dbs_ref[...] += dbs
        dvn = dvn_ref[...]
        gd = dvn * vg
        dvraw = r * gd - vraw * (r * r * r) * jnp.mean(gd * vraw, axis=-1, keepdims=True)
        dp_ref[:, A_WIDTH:2 * A_WIDTH] = dvraw.astype(dp_ref.dtype)
        dvg_ref[...] += jnp.sum(dvn * vhat, axis=0, keepdims=True)

    return pl.pallas_call(
        body, name=name, grid=(s // tq,),
        in_specs=[pl.BlockSpec((tq, 3 * A_WIDTH), lambda i: (i, 0)), pl.BlockSpec((tq, A_WIDTH), lambda i: (i, 0)),
                  pl.BlockSpec((1, A_WIDTH), lambda i: (0, 0)), pl.BlockSpec((A_GROUPS, CHUNK, CHUNK), lambda i: (0, 0, 0)),
                  pl.BlockSpec((CHUNK, A_GROUPS), lambda i: (0, 0))],
        out_specs=[pl.BlockSpec((tq, 3 * A_WIDTH), lambda i: (i, 0)), pl.BlockSpec((A_GROUPS, CHUNK, CHUNK), lambda i: (0, 0, 0)),
                   pl.BlockSpec((CHUNK, A_GROUPS), lambda i: (0, 0)), pl.BlockSpec((1, A_WIDTH), lambda i: (0, 0))],
        out_shape=[jax.ShapeDtypeStruct((s, 3 * A_WIDTH), CDT), jax.ShapeDtypeStruct((A_GROUPS, CHUNK, CHUNK), F32),
                   jax.ShapeDtypeStruct((CHUNK, A_GROUPS), F32), jax.ShapeDtypeStruct((1, A_WIDTH), F32)],
        scratch_shapes=[pltpu.VMEM((tq, A_WIDTH), F32)],
        compiler_params=_cparams("arbitrary"),
    )(proj, dy, v_gain, w_s, b_s_t)


def _rope_tables(s):
    inv_freq = jnp.power(jnp.float32(ROPE_THETA), -jnp.arange(ROPE_HALF, dtype=F32) / ROPE_HALF)
    ang = jnp.arange(s, dtype=F32)[:, None] * inv_freq[None, :]
    cos, sin = jnp.cos(ang), jnp.sin(ang)
    rest = HEAD_DIM - 2 * ROPE_HALF
    t_c = jnp.concatenate([cos, cos, jnp.ones((s, rest), F32)], axis=1)
    t_a = jnp.concatenate([-sin, jnp.zeros((s, HEAD_DIM - ROPE_HALF), F32)], axis=1)
    t_b = jnp.concatenate([jnp.zeros((s, ROPE_HALF), F32), sin, jnp.zeros((s, rest), F32)], axis=1)
    return t_c, t_a, t_b


def _b_qk_fwd(proj, gains, tabs, name, tq=256):
    s = proj.shape[0]

    def body(p_ref, g_ref, tc_ref, ta_ref, tb_ref, o_ref):
        tc, ta, tb = tc_ref[...], ta_ref[...], tb_ref[...]
        for tg in range(6):
            gain = g_ref[tg:tg + 1, :]
            for h in range(B_HEADS):
                cols = slice(tg * 1024 + h * HEAD_DIM, tg * 1024 + (h + 1) * HEAD_DIM)
                xv = p_ref[:, cols].astype(F32)
                r = lax.rsqrt(_lane_sums(xv * xv) * (1.0 / HEAD_DIM) + EPS)
                xn = xv * r * gain
                y = xn * tc + pltpu.roll(xn, HEAD_DIM - ROPE_HALF, 1) * ta + pltpu.roll(xn, ROPE_HALF, 1) * tb
                o_ref[:, cols] = y.astype(o_ref.dtype)

    tab = pl.BlockSpec((tq, HEAD_DIM), lambda i: (i, 0))
    return pl.pallas_call(
        body, name=name, grid=(s // tq,),
        in_specs=[pl.BlockSpec((tq, B_QK), lambda i: (i, 0)), pl.BlockSpec((6, HEAD_DIM), lambda i: (0, 0)), tab, tab, tab],
        out_specs=pl.BlockSpec((tq, B_QK), lambda i: (i, 0)),
        out_shape=jax.ShapeDtypeStruct((s, B_QK), CDT), compiler_params=_cparams("parallel"),
    )(proj, gains, *tabs)


PERMUTE_BLOCK_BYTES = 4 * 1024 * 1024


def _view_rows(length, dil, width, itemsize):
    rows = 16
    while 2 * rows * dil * width * itemsize <= PERMUTE_BLOCK_BYTES and 2 * rows <= length:
        rows *= 2
    return rows


def _to_view(x, col, width, dil, name):
    s = x.shape[0]
    length = s // dil
    tl = _view_rows(length, dil, width, 4)
    lanes = HEAD_DIM
    nblk = width // lanes

    def body(x_ref, o_ref, slab_ref):
        for b in range(nblk):
            slab_ref[b] = x_ref[:, b * lanes:(b + 1) * lanes].astype(F32)
        for r in range(dil):
            for b in range(nblk):
                o_ref[:, r * width + b * lanes:r * width + (b + 1) * lanes] = (
                    slab_ref.at[b][pl.ds(r, tl, stride=dil), :].astype(o_ref.dtype))

    return pl.pallas_call(
        body, name=name, grid=(length // tl,),
        in_specs=[pl.BlockSpec((tl * dil, width), lambda i: (i, col))],
        out_specs=pl.BlockSpec((tl, dil * width), lambda i: (i, 0)),
        out_shape=jax.ShapeDtypeStruct((length, dil * width), x.dtype),
        scratch_shapes=[pltpu.VMEM((nblk, tl * dil, lanes), F32)],
        compiler_params=_cparams("parallel"),
    )(x)


def _from_view(v, dil, name):
    length, width = v.shape[0], v.shape[1] // dil
    tl = _view_rows(length, dil, width, 4)
    lanes = HEAD_DIM
    nblk = width // lanes

    def body(v_ref, o_ref, slab_ref):
        for r in range(dil):
            for b in range(nblk):
                slab_ref.at[b][pl.ds(r, tl, stride=dil), :] = v_ref[:, r * width + b * lanes:r * width + (b + 1) * lanes].astype(F32)
        for b in range(nblk):
            o_ref[:, b * lanes:(b + 1) * lanes] = slab_ref[b].astype(o_ref.dtype)

    return pl.pallas_call(
        body, name=name, grid=(length // tl,),
        in_specs=[pl.BlockSpec((tl, dil * width), lambda i: (i, 0))],
        out_specs=pl.BlockSpec((tl * dil, width), lambda i: (i, 0)),
        out_shape=jax.ShapeDtypeStruct((length * dil, width), v.dtype),
        scratch_shapes=[pltpu.VMEM((nblk, tl * dil, lanes), F32)],
        compiler_params=_cparams("parallel"),
    )(v)


def _b_attn_fwd(q, k, v, bases, dil, name):
    length = q.shape[0]
    nb = length // CHUNK
    scale = 1.0 / math.sqrt(HEAD_DIM)
    w = B_HEADS * HEAD_DIM
    qb, kb, vb = bases

    def body(q_ref, kc_ref, kp_ref, vc_ref, vp_ref, o_ref, lse_ref):
        n = pl.program_id(1)
        qi = lax.broadcasted_iota(jnp.int32, (CHUNK, 2 * CHUNK), 0)
        ki = lax.broadcasted_iota(jnp.int32, (CHUNK, 2 * CHUNK), 1)
        first_key = jnp.where(n > 0, 0, CHUNK)
        mask = (ki >= qi) & (ki <= qi + CHUNK) & (ki >= first_key)
        lane = lax.broadcasted_iota(jnp.int32, (CHUNK, HEAD_DIM), 1)
        lse_all = jnp.zeros((CHUNK, HEAD_DIM), F32)
        for h in range(B_HEADS):
            sl = slice(h * HEAD_DIM, (h + 1) * HEAD_DIM)
            k2 = jnp.concatenate([kp_ref[:, sl], kc_ref[:, sl]], axis=0)
            v2 = jnp.concatenate([vp_ref[:, sl], vc_ref[:, sl]], axis=0)
            sc = jnp.where(mask, _dot(q_ref[:, sl], k2, NT) * scale, -1e30)
            m = jnp.max(sc, axis=-1, keepdims=True)
            p = jnp.exp(sc - m)
            l = jnp.sum(p, axis=-1, keepdims=True)
            o_ref[:, sl] = _dot(p.astype(CDT), v2) / l
            lse_all = jnp.where(lane == h, m + jnp.log(l), lse_all)
        lse_ref[...] = lse_all

    prev = lambda n: jnp.maximum(n - 1, 0)
    blk = lambda f: pl.BlockSpec((CHUNK, w), f)
    return pl.pallas_call(
        body, name=name, grid=(dil, nb),
        in_specs=[blk(lambda r, n: (n, qb + r)), blk(lambda r, n: (n, kb + r)), blk(lambda r, n: (prev(n), kb + r)),
                  blk(lambda r, n: (n, vb + r)), blk(lambda r, n: (prev(n), vb + r))],
        out_specs=[blk(lambda r, n: (n, r)), pl.BlockSpec((CHUNK, HEAD_DIM), lambda r, n: (n, r))],
        out_shape=[jax.ShapeDtypeStruct((length, dil * w), F32), jax.ShapeDtypeStruct((length, dil * HEAD_DIM), F32)],
        compiler_params=_cparams("parallel", "parallel"),
    )(q, k, k, v, v)


def _b_combine(os_, lses, proj, name, tq=512):
    s = proj.shape[0]
    w = B_HEADS * HEAD_DIM

    def body(o0_ref, o1_ref, o2_ref, l0_ref, l1_ref, l2_ref, z_ref, y_ref, oj_ref, lj_ref):
        l0, l1, l2 = l0_ref[...], l1_ref[...], l2_ref[...]
        m = jnp.maximum(jnp.maximum(l0, l1), l2)
        lj = m + jnp.log(jnp.exp(l0 - m) + jnp.exp(l1 - m) + jnp.exp(l2 - m))
        lj_ref[...] = lj
        w0, w1, w2 = jnp.exp(l0 - lj), jnp.exp(l1 - lj), jnp.exp(l2 - lj)
        for h in range(B_HEADS):
            sl = slice(h * HEAD_DIM, (h + 1) * HEAD_DIM)
            o = w0[:, h:h + 1] * o0_ref[:, sl] + w1[:, h:h + 1] * o1_ref[:, sl] + w2[:, h:h + 1] * o2_ref[:, sl]
            z = z_ref[:, sl].astype(F32)
            oj_ref[:, sl] = o.astype(oj_ref.dtype)
            y_ref[:, sl] = (o * (z * _sigmoid(z))).astype(y_ref.dtype)

    blk = pl.BlockSpec((tq, w), lambda i: (i, 0))
    st = pl.BlockSpec((tq, HEAD_DIM), lambda i: (i, 0))
    return pl.pallas_call(
        body, name=name, grid=(s // tq,),
        in_specs=[blk, blk, blk, st, st, st, pl.BlockSpec((tq, w), lambda i: (i, 9))],
        out_specs=[blk, blk, st],
        out_shape=[jax.ShapeDtypeStruct((s, w), CDT), jax.ShapeDtypeStruct((s, w), CDT), jax.ShapeDtypeStruct((s, HEAD_DIM), F32)],
        compiler_params=_cparams("parallel"),
    )(*os_, *lses, proj)


def _b_bwd_prep(dy, oj, proj, name, tq=512):
    s = proj.shape[0]
    w = B_HEADS * HEAD_DIM

    def body(dy_ref, oj_ref, z_ref, do_ref, dz_ref, dd_ref):
        lane = lax.broadcasted_iota(jnp.int32, (tq, HEAD_DIM), 1)
        dd = jnp.zeros((tq, HEAD_DIM), F32)
        for h in range(B_HEADS):
            sl = slice(h * HEAD_DIM, (h + 1) * HEAD_DIM)
            z = z_ref[:, sl].astype(F32)
            dyv = dy_ref[:, sl].astype(F32)
            o = oj_ref[:, sl].astype(F32)
            sg = _sigmoid(z)
            do = dyv * (z * sg)
            do_ref[:, sl] = do.astype(do_ref.dtype)
            dz_ref[:, sl] = (dyv * o * (sg * (1.0 + z * (1.0 - sg)))).astype(dz_ref.dtype)
            dd = jnp.where(lane == h, jnp.sum(do * o, axis=-1, keepdims=True), dd)
        dd_ref[...] = dd

    blk = pl.BlockSpec((tq, w), lambda i: (i, 0))
    st = pl.BlockSpec((tq, HEAD_DIM), lambda i: (i, 0))
    return pl.pallas_call(
        body, name=name, grid=(s // tq,),
        in_specs=[blk, blk, pl.BlockSpec((tq, w), lambda i: (i, 9))], out_specs=[blk, blk, st],
        out_shape=[jax.ShapeDtypeStruct((s, w), CDT), jax.ShapeDtypeStruct((s, w), CDT), jax.ShapeDtypeStruct((s, HEAD_DIM), F32)],
        compiler_params=_cparams("parallel"),
    )(dy, oj, proj)


def _b_attn_bwd(q, k, v, bases, do, lj, dd, dil, name):
    length = q.shape[0]
    nb = length // CHUNK
    scale = 1.0 / math.sqrt(HEAD_DIM)
    w = B_HEADS * HEAD_DIM
    qb, kb, vb = bases

    def body(qj_ref, qn_ref, k_ref, v_ref, doj_ref, don_ref, lj_ref, ln_ref, dj_ref, dn_ref, out_ref, carry_ref):
        j = pl.program_id(1)

        @pl.when(j == 0)
        def _():
            carry_ref[...] = jnp.zeros_like(carry_ref)

        qi = lax.broadcasted_iota(jnp.int32, (2 * CHUNK, CHUNK), 0)
        ki = lax.broadcasted_iota(jnp.int32, (2 * CHUNK, CHUNK), 1)
        no_next = jnp.where(j + 1 < nb, 0, 2 * CHUNK)
        mask = ((qi < CHUNK) & (ki <= qi)) | ((qi >= CHUNK) & (ki >= qi - CHUNK + no_next))
        for h in range(B_HEADS):
            sl = slice(h * HEAD_DIM, (h + 1) * HEAD_DIM)
            q2 = jnp.concatenate([qj_ref[:, sl], qn_ref[:, sl]], axis=0)
            do2 = jnp.concatenate([doj_ref[:, sl], don_ref[:, sl]], axis=0)
            lse2 = jnp.concatenate([lj_ref[:, h:h + 1], ln_ref[:, h:h + 1]], axis=0)
            d2 = jnp.concatenate([dj_ref[:, h:h + 1], dn_ref[:, h:h + 1]], axis=0)
            k = k_ref[:, sl]
            v = v_ref[:, sl]
            sc = _dot(q2, k, NT) * scale
            p = jnp.where(mask, jnp.exp(sc - lse2), 0.0)
            dp = _dot(do2, v, NT)
            ds = (p * (dp - d2) * scale).astype(CDT)
            dq2 = _dot(ds, k)
            out_ref[:, sl] = (carry_ref[:, sl] + dq2[:CHUNK]).astype(out_ref.dtype)
            carry_ref[:, sl] = dq2[CHUNK:]
            out_ref[:, w + h * HEAD_DIM:w + (h + 1) * HEAD_DIM] = _dot(ds, q2, TN).astype(out_ref.dtype)
            out_ref[:, 2 * w + h * HEAD_DIM:2 * w + (h + 1) * HEAD_DIM] = _dot(p.astype(CDT), do2, TN).astype(out_ref.dtype)

    nxt = lambda j: jnp.minimum(j + 1, nb - 1)
    blk = lambda f: pl.BlockSpec((CHUNK, w), f)
    st = lambda f: pl.BlockSpec((CHUNK, HEAD_DIM), f)
    return pl.pallas_call(
        body, name=name, grid=(dil, nb),
        in_specs=[blk(lambda r, j: (j, qb + r)), blk(lambda r, j: (nxt(j), qb + r)),
                  blk(lambda r, j: (j, kb + r)), blk(lambda r, j: (j, vb + r)),
                  blk(lambda r, j: (j, r)), blk(lambda r, j: (nxt(j), r)),
                  st(lambda r, j: (j, r)), st(lambda r, j: (nxt(j), r)), st(lambda r, j: (j, r)), st(lambda r, j: (nxt(j), r))],
        out_specs=pl.BlockSpec((CHUNK, 3 * w), lambda r, j: (j, r)),
        out_shape=jax.ShapeDtypeStruct((length, dil * 3 * w), CDT),
        scratch_shapes=[pltpu.VMEM((CHUNK, w), F32)],
        compiler_params=_cparams("parallel", "arbitrary"),
    )(q, q, k, v, do, do, lj, lj, dd, dd)


def _b_qk_bwd(proj, dqkv, dz, gains, tabs, name, tq=256):
    s = proj.shape[0]
    w = B_HEADS * HEAD_DIM

    def body(p_ref, d0_ref, d1_ref, d2_ref, dz_ref, g_ref, tc_ref, ta_ref, tb_ref, dp_ref, dg_ref):
        @pl.when(pl.program_id(0) == 0)
        def _():
            dg_ref[...] = jnp.zeros_like(dg_ref)

        tc, ta, tb = tc_ref[...], ta_ref[...], tb_ref[...]
        d_refs = (d0_ref, d1_ref, d2_ref)
        for g in range(3):
            for t in range(2):
                tg = t * 3 + g
                gain = g_ref[tg:tg + 1, :]
                dgain = jnp.zeros((1, HEAD_DIM), F32)
                for h in range(B_HEADS):
                    cols = slice(tg * w + h * HEAD_DIM, tg * w + (h + 1) * HEAD_DIM)
                    xv = p_ref[:, cols].astype(F32)
                    r = lax.rsqrt(_lane_sums(xv * xv) * (1.0 / HEAD_DIM) + EPS)
                    dyv = d_refs[g][:, t * w + h * HEAD_DIM:t * w + (h + 1) * HEAD_DIM].astype(F32)
                    dxn = dyv * tc + pltpu.roll(dyv * ta, ROPE_HALF, 1) + pltpu.roll(dyv * tb, HEAD_DIM - ROPE_HALF, 1)
                    gd = dxn * gain
                    dx = r * gd - xv * (r * r * r) * (_lane_sums(gd * xv) * (1.0 / HEAD_DIM))
                    dp_ref[:, cols] = dx.astype(dp_ref.dtype)
                    dgain = dgain + jnp.sum(dxn * xv * r, axis=0, keepdims=True)
                dg_ref[tg:tg + 1, :] += dgain
            dp_ref[:, (6 + g) * w:(7 + g) * w] = d_refs[g][:, 2 * w:3 * w]
        dp_ref[:, 9 * w:10 * w] = dz_ref[...]

    tab = pl.BlockSpec((tq, HEAD_DIM), lambda i: (i, 0))
    dblk = pl.BlockSpec((tq, 3 * w), lambda i: (i, 0))
    return pl.pallas_call(
        body, name=name, grid=(s // tq,),
        in_specs=[pl.BlockSpec((tq, B_QK), lambda i: (i, 0)), dblk, dblk, dblk, pl.BlockSpec((tq, w), lambda i: (i, 0)),
                  pl.BlockSpec((6, HEAD_DIM), lambda i: (0, 0)), tab, tab, tab],
        out_specs=[pl.BlockSpec((tq, B_IN), lambda i: (i, 0)), pl.BlockSpec((6, HEAD_DIM), lambda i: (0, 0))],
        out_shape=[jax.ShapeDtypeStruct((s, B_IN), CDT), jax.ShapeDtypeStruct((6, HEAD_DIM), F32)],
        compiler_params=_cparams("arbitrary"),
    )(proj, *dqkv, dz, gains, *tabs)


def _inv_count(t, window):
    return 1.0 / jnp.minimum(t + 1, window).astype(F32)


POOL_BLOCK = 128


def _band(window, forward):
    r = lax.broadcasted_iota(jnp.int32, (POOL_BLOCK, 2 * POOL_BLOCK), 0)
    j = lax.broadcasted_iota(jnp.int32, (POOL_BLOCK, 2 * POOL_BLOCK), 1)
    if forward:
        return jnp.where((j >= r) & (j < r + window), 1.0, 0.0).astype(CDT)
    return jnp.where((j <= r + POOL_BLOCK) & (j > r + POOL_BLOCK - window), 1.0, 0.0).astype(CDT)


def _split_dot(band, v):
    hi = v.astype(jnp.bfloat16)
    lo = (v - hi.astype(F32)).astype(jnp.bfloat16)
    band = band.astype(jnp.bfloat16)
    return _dot(band, hi) + _dot(band, lo)


def _pool_diff(x_ref, halo_ref, diff_ref, i, tq):
    t = i * tq + lax.broadcasted_iota(jnp.int32, (tq, 1), 0)
    for g, window in enumerate(POOL_SIZES):
        cols = slice(g * C_GROUP, (g + 1) * C_GROUP)
        band = _band(window, forward=False)
        inv = _inv_count(t, window)
        for b in range(tq // POOL_BLOCK):
            rows = slice(b * POOL_BLOCK, (b + 1) * POOL_BLOCK)
            cur = x_ref[rows, cols]
            if b == 0:
                above = jnp.where(i > 0, halo_ref[:, cols], jnp.zeros_like(cur))
            else:
                above = x_ref[(b - 1) * POOL_BLOCK:b * POOL_BLOCK, cols]
            pooled = _dot(band, jnp.concatenate([above, cur], axis=0)) * inv[rows]
            diff_ref[rows, cols] = (pooled - cur.astype(F32)).astype(diff_ref.dtype)


GRP_SHARD = (N_CHIPS, 2, 256, C_GROUP)
GRP_ROWS = C_GROUP // N_CHIPS


def _grp_rows(g):
    return g // 2, slice((g % 2) * GRP_ROWS, (g % 2 + 1) * GRP_ROWS)


def _grp_weight(w_ref, g):
    half, rows = _grp_rows(g)
    return jnp.concatenate([w_ref[k, half, rows, :] for k in range(N_CHIPS)], axis=0)


def _c_mid_fwd(proj, w_grp, scale, name, tq=512):
    s = proj.shape[0]
    per = tq // POOL_BLOCK

    def body(x_ref, halo_ref, z_ref, w_ref, sc_ref, y_ref, diff_ref):
        _pool_diff(x_ref, halo_ref, diff_ref, pl.program_id(0), tq)
        for g in range(len(POOL_SIZES)):
            cols = slice(g * C_GROUP, (g + 1) * C_GROUP)
            z = z_ref[:, cols].astype(F32)
            y_ref[:, cols] = (_dot(diff_ref[:, cols], _grp_weight(w_ref, g)) * sc_ref[:, cols] * (z * _sigmoid(z))).astype(y_ref.dtype)

    return pl.pallas_call(
        body, name=name, grid=(s // tq,),
        in_specs=[pl.BlockSpec((tq, C_WIDTH), lambda i: (i, 0)),
                  pl.BlockSpec((POOL_BLOCK, C_WIDTH), lambda i: (jnp.maximum(i * per - 1, 0), 0)),
                  pl.BlockSpec((tq, C_WIDTH), lambda i: (i, 1)),
                  pl.BlockSpec(GRP_SHARD, lambda i: (0, 0, 0, 0)), pl.BlockSpec((1, C_WIDTH), lambda i: (0, 0))],
        out_specs=pl.BlockSpec((tq, C_WIDTH), lambda i: (i, 0)),
        out_shape=jax.ShapeDtypeStruct((s, C_WIDTH), CDT),
        scratch_shapes=[pltpu.VMEM((tq, C_WIDTH), CDT)], compiler_params=_cparams("parallel"),
    )(proj, proj, proj, w_grp, scale)


def _c_mid_bwd(proj, w_grp, scale, dy, name, tq=512):
    s = proj.shape[0]
    per = tq // POOL_BLOCK

    def body(x_ref, halo_ref, z_ref, w_ref, sc_ref, dy_ref, dd_ref, dz_ref, dw_ref, dsc_ref, diff_ref):
        @pl.when(pl.program_id(0) == 0)
        def _():
            dw_ref[...] = jnp.zeros_like(dw_ref)
            dsc_ref[...] = jnp.zeros_like(dsc_ref)

        _pool_diff(x_ref, halo_ref, diff_ref, pl.program_id(0), tq)
        for g in range(len(POOL_SIZES)):
            cols = slice(g * C_GROUP, (g + 1) * C_GROUP)
            d = diff_ref[:, cols]
            wg = _grp_weight(w_ref, g)
            half, rows = _grp_rows(g)
            m0 = _dot(d, wg)
            z = z_ref[:, cols].astype(F32)
            dyv = dy_ref[:, cols].astype(F32)
            sc = sc_ref[:, cols]
            sg = _sigmoid(z)
            dmixed = dyv * (z * sg)
            dz_ref[:, cols] = (dyv * m0 * sc * (sg * (1.0 + z * (1.0 - sg)))).astype(dz_ref.dtype)
            dsc_ref[:, cols] += jnp.sum(dmixed * m0, axis=0, keepdims=True)
            dm0 = (dmixed * sc).astype(CDT)
            dwg = _dot(d, dm0, TN)
            for k in range(N_CHIPS):
                dw_ref[k, half, rows, :] += dwg[k * GRP_ROWS:(k + 1) * GRP_ROWS]
            dd_ref[:, cols] = _dot(dm0, wg, NT)

    blk = pl.BlockSpec((tq, C_WIDTH), lambda i: (i, 0))
    wsp = pl.BlockSpec(GRP_SHARD, lambda i: (0, 0, 0, 0))
    vec = pl.BlockSpec((1, C_WIDTH), lambda i: (0, 0))
    return pl.pallas_call(
        body, name=name, grid=(s // tq,),
        in_specs=[blk, pl.BlockSpec((POOL_BLOCK, C_WIDTH), lambda i: (jnp.maximum(i * per - 1, 0), 0)),
                  pl.BlockSpec((tq, C_WIDTH), lambda i: (i, 1)), wsp, vec, blk],
        out_specs=[blk, blk, wsp, vec],
        out_shape=[jax.ShapeDtypeStruct((s, C_WIDTH), F32), jax.ShapeDtypeStruct((s, C_WIDTH), CDT),
                   jax.ShapeDtypeStruct(GRP_SHARD, F32), jax.ShapeDtypeStruct((1, C_WIDTH), F32)],
        scratch_shapes=[pltpu.VMEM((tq, C_WIDTH), CDT)], compiler_params=_cparams("arbitrary"),
    )(proj, proj, proj, w_grp, scale, dy)


def _c_pool_bwd(ddiff, dz, name, tq=512):
    s = ddiff.shape[0]
    per = tq // POOL_BLOCK
    last = s // tq - 1

    def body(d_ref, halo_ref, dz_ref, o_ref):
        i = pl.program_id(0)
        t = i * tq + lax.broadcasted_iota(jnp.int32, (tq, 1), 0)
        for g, window in enumerate(POOL_SIZES):
            cols = slice(g * C_GROUP, (g + 1) * C_GROUP)
            band = _band(window, forward=True)
            inv = _inv_count(t, window)
            for b in range(tq // POOL_BLOCK):
                rows = slice(b * POOL_BLOCK, (b + 1) * POOL_BLOCK)
                cur = d_ref[rows, cols]
                if b == tq // POOL_BLOCK - 1:
                    below = jnp.where(i < last, halo_ref[:, cols] * (1.0 / window), 0.0)
                else:
                    nxt = slice((b + 1) * POOL_BLOCK, (b + 2) * POOL_BLOCK)
                    below = d_ref[nxt, cols] * inv[nxt]
                summed = _split_dot(band, jnp.concatenate([cur * inv[rows], below], axis=0))
                o_ref[rows, cols] = (summed - cur).astype(o_ref.dtype)
        o_ref[:, C_WIDTH:] = dz_ref[...]

    return pl.pallas_call(
        body, name=name, grid=(s // tq,),
        in_specs=[pl.BlockSpec((tq, C_WIDTH), lambda i: (i, 0)),
                  pl.BlockSpec((POOL_BLOCK, C_WIDTH), lambda i: (jnp.minimum((i + 1) * per, s // POOL_BLOCK - 1), 0)),
                  pl.BlockSpec((tq, C_WIDTH), lambda i: (i, 0))],
        out_specs=pl.BlockSpec((tq, 2 * C_WIDTH), lambda i: (i, 0)),
        out_shape=jax.ShapeDtypeStruct((s, 2 * C_WIDTH), CDT),
        compiler_params=_cparams("parallel"),
    )(ddiff, ddiff, dz)


LAYERS = (("a", 0), ("b", 0), ("c", 0), ("a", 1))


def _shard_spec(block, layer, where):
    if layer is None:
        return pl.BlockSpec((None,) + block, where)
    return pl.BlockSpec((None, None) + block, lambda i, j, q: (where(i, j, q)[0], layer) + where(i, j, q)[1:])


def _w_in_fwd(h, gw, layer, name):
    s, width = h.shape[0], gw.shape[-1]
    rows = 2048 if width <= 1024 else 1024
    return _mm(h, gw, "nn", CDT, name, (s, N_CHIPS * width, D_MODEL), (rows, width, D_MODEL),
               b_spec=_shard_spec((D_MODEL, width), layer, lambda i, j, q: (j, 0, 0)))


def _w_in_dh(dproj, gw, x, dxo, gain, name):
    s, width = dproj.shape[0], gw.shape[-1]
    tm = 1024 if width <= 1024 else 512

    def body(a_ref, b_ref, x_ref, dxo_ref, g_ref, dx_ref, dxc_ref, dg_ref, acc_ref):
        i, q = pl.program_id(0), pl.program_id(1)
        p = _dot(a_ref[...], b_ref[...], NT)

        @pl.when(q == 0)
        def _():
            acc_ref[...] = p

        @pl.when(q > 0)
        def _():
            acc_ref[...] += p

        @pl.when((i == 0) & (q == 0))
        def _():
            dg_ref[...] = jnp.zeros_like(dg_ref)

        @pl.when(q == N_CHIPS - 1)
        def _():
            dh = acc_ref[...]
            xv = x_ref[...]
            r = lax.rsqrt(jnp.mean(xv * xv, axis=-1, keepdims=True) + EPS)
            gd = dh * g_ref[...]
            dx = dxo_ref[...] + r * gd - xv * (r * r * r) * jnp.mean(gd * xv, axis=-1, keepdims=True)
            dx_ref[...] = dx
            dxc_ref[...] = dx.astype(dxc_ref.dtype)
            dg_ref[...] += jnp.sum(dh * xv * r, axis=0, keepdims=True)

    blk = pl.BlockSpec((tm, D_MODEL), lambda i, q: (i, 0))
    vec = pl.BlockSpec((1, D_MODEL), lambda i, q: (0, 0))
    return pl.pallas_call(
        body, name=name, grid=(s // tm, N_CHIPS),
        in_specs=[pl.BlockSpec((tm, width), lambda i, q: (i, q)), pl.BlockSpec((None, D_MODEL, width), lambda i, q: (q, 0, 0)),
                  blk, blk, vec],
        out_specs=[blk, blk, vec],
        out_shape=[jax.ShapeDtypeStruct((s, D_MODEL), F32), jax.ShapeDtypeStruct((s, D_MODEL), CDT), jax.ShapeDtypeStruct((1, D_MODEL), F32)],
        scratch_shapes=[pltpu.VMEM((tm, D_MODEL), F32)], compiler_params=_cparams("arbitrary", "arbitrary"),
    )(dproj, gw, x, dxo, gain)


def _w_in_grad(h, dproj, width, name):
    s = h.shape[0]
    tokens = 2048 if width <= 1536 else 1024
    return _mm(h, dproj, "tn", CDT, name, (D_MODEL, N_CHIPS * width, s), (D_MODEL, width, tokens),
               o_spec=pl.BlockSpec((None, D_MODEL, width), lambda i, j, q: (j, 0, 0)), o_shape=(N_CHIPS, D_MODEL, width))


def _w_out_spec(gw, layer):
    rows = gw.shape[-2]
    if layer is None:
        return pl.BlockSpec((N_CHIPS, rows, D_MODEL), lambda i: (0, 0, 0))
    return pl.BlockSpec((N_CHIPS, None, rows, D_MODEL), lambda i: (0, layer, 0, 0))


def _w_out_fwd(y, gw, layer, x, name, tm=1024):
    s, k = y.shape

    def body(y_ref, w_ref, x_ref, o_ref):
        o_ref[...] = x_ref[...] + _dot(y_ref[...], w_ref[...].reshape(k, D_MODEL))

    blk = pl.BlockSpec((tm, D_MODEL), lambda i: (i, 0))
    return pl.pallas_call(
        body, name=name, grid=(s // tm,), in_specs=[pl.BlockSpec((tm, k), lambda i: (i, 0)), _w_out_spec(gw, layer), blk],
        out_specs=blk, out_shape=jax.ShapeDtypeStruct((s, D_MODEL), F32), compiler_params=_cparams("parallel"),
    )(y, gw, x)


def _w_out_dy(dxc, gw, layer, name, tm=1024):
    s, k = dxc.shape[0], N_CHIPS * gw.shape[-2]
    tm = min(tm, s)

    def body(dx_ref, w_ref, o_ref):
        o_ref[...] = _dot(dx_ref[...], w_ref[...].reshape(k, D_MODEL), NT).astype(o_ref.dtype)

    return pl.pallas_call(
        body, name=name, grid=(s // tm,), in_specs=[pl.BlockSpec((tm, D_MODEL), lambda i: (i, 0)), _w_out_spec(gw, layer)],
        out_specs=pl.BlockSpec((tm, k), lambda i: (i, 0)), out_shape=jax.ShapeDtypeStruct((s, k), CDT),
        compiler_params=_cparams("parallel"),
    )(dxc, gw)


def _w_out_grad(y, dxc, rows, name):
    s, k = y.shape
    tokens = min(2048 if k <= 1024 else 1024, s)
    steps = s // tokens

    def body(y_ref, dx_ref, o_ref, acc_ref):
        i = pl.program_id(0)
        p = _dot(y_ref[...], dx_ref[...], TN)

        @pl.when(i == 0)
        def _():
            acc_ref[...] = p

        @pl.when(i > 0)
        def _():
            acc_ref[...] += p

        @pl.when(i == steps - 1)
        def _():
            o_ref[...] = acc_ref[...].reshape(N_CHIPS, rows, D_MODEL).astype(o_ref.dtype)

    return pl.pallas_call(
        body, name=name, grid=(steps,),
        in_specs=[pl.BlockSpec((tokens, k), lambda i: (i, 0)), pl.BlockSpec((tokens, D_MODEL), lambda i: (i, 0))],
        out_specs=pl.BlockSpec((N_CHIPS, rows, D_MODEL), lambda i: (0, 0, 0)),
        out_shape=jax.ShapeDtypeStruct((N_CHIPS, rows, D_MODEL), CDT),
        scratch_shapes=[pltpu.VMEM((k, D_MODEL), F32)], compiler_params=_cparams("arbitrary"),
    )(y, dxc)


def _local_step(x, target, w, on_grads=None, layer_weights=None):
    s = x.shape[0]
    tabs = _rope_tables(s)
    qk_gains = jnp.concatenate([w["b_q_gain"][0], w["b_k_gain"][0]], axis=0)
    saved = []
    for li, (kind, j) in enumerate(LAYERS):
        if layer_weights is not None:
            w = {**w, **layer_weights(li, x)}
        h = _rms_fwd(x, w["norm_gain"][li:li + 1], f"rms_fwd{li}")
        if kind == "a":
            proj = _w_in_fwd(h, w[f"a_w_in{j}"], None, f"a_in{li}")
            bs_t = w["a_b_s"][j].T
            y = _a_mid_fwd(proj, w["a_v_gain"][j:j + 1], w["a_w_s"][j], bs_t, f"a_mid_fwd{li}")
            x_next = _w_out_fwd(y, w[f"a_w_out{j}"], None, x, f"a_out{li}")
            saved.append((x, h, proj, y))
        elif kind == "b":
            proj = _w_in_fwd(h, w["b_w_in"], None, f"b_in{li}")
            qk = _b_qk_fwd(proj, qk_gains, tabs, f"b_qk_fwd{li}")
            qkv, os_, lses = [], [], []
            for g, dil in enumerate(B_DILATIONS):
                if dil == 1:
                    ops = (qk, qk, proj, (g, 3 + g, 6 + g))
                else:
                    ops = (_to_view(qk, g, 1024, dil, f"b_q_view{li}_{g}"), _to_view(qk, 3 + g, 1024, dil, f"b_k_view{li}_{g}"),
                           _to_view(proj, 6 + g, 1024, dil, f"b_v_view{li}_{g}"), (0, 0, 0))
                o, lse = _b_attn_fwd(*ops, dil, f"b_attn_fwd{li}_{g}")
                if dil > 1:
                    o, lse = _from_view(o, dil, f"b_o_nat{li}_{g}"), _from_view(lse, dil, f"b_lse_nat{li}_{g}")
                qkv.append(ops)
                os_.append(o)
                lses.append(lse)
            y, oj, lj = _b_combine(os_, lses, proj, f"b_combine{li}")
            x_next = _w_out_fwd(y, w["b_w_out"], None, x, f"b_out{li}")
            saved.append((x, h, proj, y, qkv, oj, lj))
        else:
            proj = _w_in_fwd(h, w["c_w_in"], None, f"c_in{li}")
            y = _c_mid_fwd(proj, w["c_w_grp"], w["c_scale"][j:j + 1], f"c_mid_fwd{li}")
            x_next = _w_out_fwd(y, w["c_w_out"], None, x, f"c_out{li}")
            saved.append((x, h, proj, y))
        x = x_next

    dx, dxc, sq = _loss_bwd(x, target, "loss_bwd")
    grads = {"norm_gain": [None] * len(LAYERS), "a_w_in": [None, None], "a_v_gain": [None, None], "a_w_s": [None, None],
             "a_b_s": [None, None], "a_w_out": [None, None]}
    for li in reversed(range(len(LAYERS))):
        kind, j = LAYERS[li]
        sv = saved[li]
        xin, h, proj, y = sv[:4]
        if kind == "a":
            grads["a_w_out"][j] = _w_out_grad(y, dxc, 512, f"a_dwout{li}")
            dy = _w_out_dy(dxc, w[f"a_w_out{j}"], None, f"a_dy{li}")
            dproj, dws, dbs_t, dvg = _a_mid_bwd(proj, dy, w["a_v_gain"][j:j + 1], w["a_w_s"][j], w["a_b_s"][j].T, f"a_mid_bwd{li}")
            grads["a_w_s"][j], grads["a_b_s"][j], grads["a_v_gain"][j] = dws, dbs_t.T, dvg[0]
            grads["a_w_in"][j] = _w_in_grad(h, dproj, 1536, f"a_dwin{li}")
            w_in = w[f"a_w_in{j}"]
        elif kind == "b":
            qkv, oj, lj = sv[4:]
            grads["b_w_out"] = _w_out_grad(y, dxc, 256, f"b_dwout{li}")
            dy = _w_out_dy(dxc, w["b_w_out"], None, f"b_dy{li}")
            do, dz, dd = _b_bwd_prep(dy, oj, proj, f"b_bwd_prep{li}")
            dqkv = []
            for g, dil in enumerate(B_DILATIONS):
                stats = (do, lj, dd)
                if dil > 1:
                    stats = (_to_view(do, 0, 1024, dil, f"b_do_view{li}_{g}"), _to_view(lj, 0, HEAD_DIM, dil, f"b_lj_view{li}_{g}"),
                             _to_view(dd, 0, HEAD_DIM, dil, f"b_dd_view{li}_{g}"))
                d = _b_attn_bwd(*qkv[g], *stats, dil, f"b_attn_bwd{li}_{g}")
                dqkv.append(_from_view(d, dil, f"b_dqkv_nat{li}_{g}") if dil > 1 else d)
            dproj, dgains = _b_qk_bwd(proj, dqkv, dz, qk_gains, tabs, f"b_qk_bwd{li}")
            grads["b_q_gain"], grads["b_k_gain"] = dgains[None, :3], dgains[None, 3:]
            grads["b_w_in"] = _w_in_grad(h, dproj, 2560, f"b_dwin{li}")
            w_in = w["b_w_in"]
        else:
            grads["c_w_out"] = _w_out_grad(y, dxc, 512, f"c_dwout{li}")
            dy = _w_out_dy(dxc, w["c_w_out"], None, f"c_dy{li}")
            ddiff, dz, dwg, dsc = _c_mid_bwd(proj, w["c_w_grp"], w["c_scale"][j:j + 1], dy, f"c_mid_bwd{li}")
            grads["c_w_grp"], grads["c_scale"] = dwg, dsc
            dproj = _c_pool_bwd(ddiff, dz, f"c_pool_bwd{li}")
            grads["c_w_in"] = _w_in_grad(h, dproj, 1024, f"c_dwin{li}")
            w_in = w["c_w_in"]
        gain = w["norm_gain"][li:li + 1]
        if on_grads is not None:
            gain = gain + on_grads(li, grads)
        dx, dxc, dng = _w_in_dh(dproj, w_in, xin, dx, gain, f"{kind}_dh{li}")
        grads["norm_gain"][li] = dng[0]
    for name in ("norm_gain", "a_v_gain", "a_w_s", "a_b_s"):
        grads[name] = jnp.stack(grads[name])
    return sq, dx, grads


ROW = 1024
REDUCE_UNITS = (("a_w_in", 0, 512, 1536), ("a_w_in", 1, 512, 1536), ("a_w_out", 0, 256, 1024), ("a_w_out", 1, 256, 1024),
                ("b_w_in", 0, 512, 2560), ("b_w_out", 0, 128, 1024), ("c_w_in", 0, 512, 1024), ("c_w_grp", 0, 256, 512),
                ("c_w_out", 0, 256, 1024))
GAIN_ROWS = 16
GAINS = len(REDUCE_UNITS)
LAYER_UNITS = {0: (0, 2), 1: (4, 5), 2: (6, 7, 8), 3: (1, 3)}
EARLY_UNITS = LAYER_UNITS
FIRST_UNITS = LAYER_UNITS[0] + (GAINS,)
LATER_UNITS = {li: LAYER_UNITS[li] for li in (1, 2, 3)}
SMALL_OF_HALF = ("a_v_gain", "c_scale")
SMALL_PAD = 16
REP = (("norm_gain", (4, 1024)), ("a_w_s", (2, 8, 128, 128)), ("a_b_s", (2, 8, 128)), ("b_q_gain", (1, 3, 128)), ("b_k_gain", (1, 3, 128)))
REP_CORE = 48
REP_PART = 2 * REP_CORE
SMALL_ROWS = SMALL_PAD + REP_CORE
BLOCK_SHAPES = {"a_w_in": (2, 1024, 1536), "a_v_gain": (2, 512), "a_w_out": (2, 512, 1024), "b_w_in": (1, 1024, 2560),
                "b_w_out": (1, 256, 1024), "c_w_in": (1, 1024, 1024), "c_w_grp": (1, 4, 128, 512), "c_scale": (1, 512),
                "c_w_out": (1, 512, 1024)}
WEIGHTS = ("norm_gain", "a_w_in", "a_v_gain", "a_w_s", "a_b_s", "a_w_out", "b_w_in", "b_q_gain", "b_k_gain", "b_w_out",
           "c_w_in", "c_w_grp", "c_scale", "c_w_out")


def _rows(a, rows):
    a = a.reshape(-1)
    return jnp.pad(a, (0, rows * ROW - a.shape[0])).reshape(rows, ROW)


def _small_unit(vecs, rep_part, dtype):
    halves = [jnp.concatenate([_rows(vecs[h].astype(dtype), SMALL_PAD), rep_part[h * REP_CORE:(h + 1) * REP_CORE].astype(dtype)])
              for h in range(2)]
    return jnp.stack(halves)


def _pack_rep(tree):
    return _rows(jnp.concatenate([tree[n].astype(F32).reshape(-1) for n, _ in REP]), N_CHIPS * REP_PART)


def _unpack_rep(slab):
    flat, out, off = slab.reshape(-1), {}, 0
    for n, shape in REP:
        size = math.prod(shape)
        out[n] = flat[off:off + size].reshape(shape)
        off += size
    return out


def _into_slot(x, first, rows, me, name):
    width = x.shape[1]
    tr = min(rows, 256)

    def body(me_ref, x_ref, o_ref):
        o_ref[...] = x_ref[...].astype(o_ref.dtype)

    return pl.pallas_call(
        body, name=name,
        grid_spec=pltpu.PrefetchScalarGridSpec(
            num_scalar_prefetch=1, grid=(rows // tr,), in_specs=[pl.BlockSpec((tr, width), lambda i, me_ref: (first // tr + i, 0))],
            out_specs=pl.BlockSpec((None, tr, width), lambda i, me_ref: (me_ref[0], i, 0))),
        out_shape=jax.ShapeDtypeStruct((N_CHIPS, rows, width), CDT), compiler_params=_cparams("parallel"),
    )(me.reshape(1), x)


def _gather_inputs(wts, me):
    units = []
    for u, (n, layer, r, w) in enumerate(REDUCE_UNITS):
        slot = _into_slot(wts[n].reshape(-1, w), layer * 2 * r, 2 * r, me, f"slot{u}")
        units.append(slot.reshape(N_CHIPS, 2, r, w))
    gains = jnp.concatenate([wts["a_v_gain"].reshape(-1), wts["c_scale"].reshape(-1)])
    gains = _rows(lax.bitcast_convert_type(gains, CDT), 2 * GAIN_ROWS)
    return units + [_into_slot(gains, 0, 2 * GAIN_ROWS, me, "slot_gains").reshape(N_CHIPS, 2, GAIN_ROWS, ROW)]


def _gathered_weights(indices, units):
    out = {}
    for u, arr in zip(indices, units):
        if u == GAINS:
            gains = lax.bitcast_convert_type(arr[:, 0, :3].reshape(N_CHIPS, 1536, 2), F32)
            out["a_v_gain"] = jnp.concatenate([gains[k, :1024].reshape(2, 512) for k in range(N_CHIPS)], axis=1)
            out["c_scale"] = jnp.concatenate([gains[k, 1024:].reshape(1, 512) for k in range(N_CHIPS)], axis=1)
            continue
        n, layer, r, w = REDUCE_UNITS[u]
        if n == "c_w_grp":
            out[n] = arr
        else:
            out[n + str(layer) if n in ("a_w_in", "a_w_out") else n] = arr.reshape(N_CHIPS, 2 * r, w)
    return out


def _place():
    x, y, c = lax.axis_index("x"), lax.axis_index("y"), lax.axis_index("c")
    chips = [(1 - x, y), (x, 1 - y), (1 - x, 1 - y)]
    return x, y, c, 2 * x + y, (x, y, 1 - c), chips


def _remote(src, dst, sems, j, to):
    send_sems, recv_sems = sems
    return pltpu.make_async_remote_copy(src_ref=src, dst_ref=dst, send_sem=send_sems.at[j], recv_sem=recv_sems.at[j],
                                        device_id=to, device_id_type=MESH)


def _comm_call_multi(body, name, out_shapes, n_sems, args, aliases=None):
    return pl.pallas_call(
        body, name=name, in_specs=[ANY] * len(args), out_specs=[ANY] * len(out_shapes), out_shape=out_shapes,
        scratch_shapes=[pltpu.SemaphoreType.DMA((n_sems,)), pltpu.SemaphoreType.DMA((n_sems,))],
        input_output_aliases=aliases or {},
    )(*args)


def _allgather_units(units):
    n = len(units)

    def body(*refs):
        outs, sems = refs[n:2 * n], (refs[2 * n], refs[2 * n + 1])
        x, y, c, me, sibling, chips = _place()
        first, passed = [], []
        for u, o_ref in enumerate(outs):
            for j, chip in enumerate(chips):
                first.append(_remote(o_ref.at[me, c], o_ref.at[me, c], sems, 6 * u + j, (*chip, c)))
                first[-1].start()
        for u, o_ref in enumerate(outs):
            for j, (cx, cy) in enumerate(chips):
                landed = o_ref.at[2 * cx + cy, c]
                _remote(landed, landed, sems, 6 * u + j, sibling).wait_recv()
                passed.append(_remote(landed, landed, sems, 6 * u + 3 + j, sibling))
                passed[-1].start()
        for u, o_ref in enumerate(outs):
            for j, (cx, cy) in enumerate(chips):
                landed = o_ref.at[2 * cx + cy, 1 - c]
                _remote(landed, landed, sems, 6 * u + 3 + j, sibling).wait_recv()
        for cp in first + passed:
            cp.wait_send()

    return _comm_call_multi(body, "allgather_first", [jax.ShapeDtypeStruct(u.shape, u.dtype) for u in units], 6 * n, units,
                            aliases={u: u for u in range(n)})


HBM = pl.BlockSpec(memory_space=pltpu.HBM)
SEM = pl.BlockSpec(memory_space=pltpu.SEMAPHORE)
SIDE_EFFECT = pltpu.SideEffectType.DATAFLOW_SIDE_EFFECTING


def _gather_start(groups, after):
    sizes = [len(g) for g in groups]
    units = [u for g in groups for u in g]
    n = len(units)

    def body(*refs):
        arrs, sems, token = refs[:n], refs[n + 1:n + 1 + 2 * len(groups)], refs[-1]
        x, y, c, me, sibling, chips = _place()
        at = 0
        for gi, size in enumerate(sizes):
            for u in range(size):
                mine = arrs[at + u].at[me, c]
                for j, chip in enumerate(chips):
                    _remote(mine, mine, (sems[2 * gi], sems[2 * gi + 1]), 3 * u + j, (*chip, c)).start()
            at += size
        token[...] = jnp.zeros_like(token)

    sem_shapes = [pltpu.SemaphoreType.DMA((3 * size,)) for size in sizes for _ in range(2)]
    outs = pl.pallas_call(
        body, name="gather_start",
        out_shape=(*sem_shapes, *[pltpu.HBM(u.shape, u.dtype) for u in units], jax.ShapeDtypeStruct((8, 128), F32)),
        in_specs=[HBM] * n + [ANY], out_specs=(*[SEM] * len(sem_shapes), *[HBM] * n, pl.BlockSpec(memory_space=pltpu.VMEM)),
        input_output_aliases={i: len(sem_shapes) + i for i in range(n)},
        compiler_params=pltpu.CompilerParams(has_side_effects=SIDE_EFFECT),
    )(*[pltpu.with_memory_space_constraint(u, pltpu.HBM) for u in units], after)
    sems, arrs = outs[:len(sem_shapes)], outs[len(sem_shapes):-1]
    bounds = [sum(sizes[:gi]) for gi in range(len(sizes) + 1)]
    return ([(sems[2 * gi], sems[2 * gi + 1]) for gi in range(len(sizes))],
            [list(arrs[bounds[gi]:bounds[gi + 1]]) for gi in range(len(sizes))], outs[-1])


def _gather_wait(units, sems, after, name):
    n = len(units)

    def body(*refs):
        arrs, send_sems, recv_sems = refs[:n], refs[n], refs[n + 1]
        x, y, c, me, sibling, chips = _place()
        for u in range(n):
            for j, (cx, cy) in enumerate(chips):
                cp = _remote(arrs[u].at[me, c], arrs[u].at[2 * cx + cy, c], (send_sems, recv_sems), 3 * u + j, (cx, cy, c))
                cp.wait_send()
                cp.wait_recv()

    outs = pl.pallas_call(
        body, name=name, out_shape=tuple(pltpu.HBM(u.shape, u.dtype) for u in units),
        in_specs=[HBM] * n + [SEM, SEM, ANY], out_specs=[HBM] * n, input_output_aliases={i: i for i in range(n)},
        compiler_params=pltpu.CompilerParams(has_side_effects=SIDE_EFFECT),
    )(*units, *sems, after)
    return list(outs)


def _forward_units(units, name):
    n = len(units)

    def body(*refs):
        outs, sems = refs[n:2 * n], (refs[2 * n], refs[2 * n + 1])
        x, y, c, me, sibling, chips = _place()
        passed = []
        for u, o_ref in enumerate(outs):
            for j, (cx, cy) in enumerate(chips):
                landed = o_ref.at[2 * cx + cy, c]
                passed.append(_remote(landed, landed, sems, 3 * u + j, sibling))
                passed[-1].start()
        for u, o_ref in enumerate(outs):
            for j, (cx, cy) in enumerate(chips):
                landed = o_ref.at[2 * cx + cy, 1 - c]
                _remote(landed, landed, sems, 3 * u + j, sibling).wait_recv()
        for cp in passed:
            cp.wait_send()

    return _comm_call_multi(body, name, [jax.ShapeDtypeStruct(u.shape, u.dtype) for u in units], 3 * n, units,
                            aliases={u: u for u in range(n)})


def _swap_halves(units, name):
    n = len(units)

    def body(*refs):
        ins, outs, sems = refs[:n], refs[n:2 * n], (refs[2 * n], refs[2 * n + 1])
        x, y, c, me, sibling, chips = _place()
        sent = [_remote(g_ref.at[:, 1 - c], o_ref, sems, u, sibling) for u, (g_ref, o_ref) in enumerate(zip(ins, outs))]
        for cp in sent:
            cp.start()
        for cp in sent:
            cp.wait()

    shapes = [jax.ShapeDtypeStruct((N_CHIPS,) + u.shape[2:], u.dtype) for u in units]
    return _comm_call_multi(body, name, shapes, n, units)


def _scatter_shards(units):
    n = len(units)

    def body(*refs):
        ins, outs, sems = refs[:n], refs[n:2 * n], (refs[2 * n], refs[2 * n + 1])
        x, y, c, me, sibling, chips = _place()
        sent = []
        for u, (s_ref, o_ref) in enumerate(zip(ins, outs)):
            for j, (cx, cy) in enumerate(chips):
                sent.append(_remote(s_ref.at[2 * cx + cy], o_ref.at[me], sems, 3 * u + j, (cx, cy, c)))
                sent[-1].start()
        for u, o_ref in enumerate(outs):
            for j, (cx, cy) in enumerate(chips):
                slot = o_ref.at[2 * cx + cy]
                _remote(slot, slot, sems, 3 * u + j, sibling).wait_recv()
        for cp in sent:
            cp.wait_send()

    return _comm_call_multi(body, "grad_scatter_shards", [jax.ShapeDtypeStruct(u.shape, u.dtype) for u in units], 3 * n, units)


def _scatter_start(units, name):
    n = len(units)

    def body(*refs):
        srcs, lands = refs[:n], refs[n:2 * n]
        send_sems, recv_sems, token = refs[2 * n], refs[2 * n + 1], refs[-1]
        x, y, c, me, sibling, chips = _place()
        for u in range(n):
            for j, (cx, cy) in enumerate(chips):
                _remote(srcs[u].at[2 * cx + cy], lands[u].at[me], (send_sems, recv_sems), 3 * u + j, (cx, cy, c)).start()
        token[...] = jnp.zeros_like(token)

    hbm = [pltpu.HBM(u.shape, u.dtype) for u in units]
    outs = pl.pallas_call(
        body, name=name,
        out_shape=(pltpu.SemaphoreType.DMA((3 * n,)), pltpu.SemaphoreType.DMA((3 * n,)), *hbm, *hbm, jax.ShapeDtypeStruct((8, 128), F32)),
        in_specs=[HBM] * (2 * n), out_specs=(SEM, SEM, *[HBM] * (2 * n), pl.BlockSpec(memory_space=pltpu.VMEM)),
        input_output_aliases={i: 2 + i for i in range(2 * n)},
        compiler_params=pltpu.CompilerParams(has_side_effects=SIDE_EFFECT),
    )(*[pltpu.with_memory_space_constraint(u, pltpu.HBM) for u in units],
      *[pltpu.with_memory_space_constraint(lax.empty(u.shape, u.dtype), pltpu.HBM) for u in units])
    return outs[0], outs[1], outs[2:2 + n], outs[2 + n:2 + 2 * n], outs[-1]


def _scatter_wait(send_sems, recv_sems, srcs, lands, after, name):
    n = len(srcs)

    def body(*refs):
        srcs_, lands_, send_sems_, recv_sems_ = refs[:n], refs[n:2 * n], refs[2 * n], refs[2 * n + 1]
        x, y, c, me, sibling, chips = _place()
        for u in range(n):
            for j, (cx, cy) in enumerate(chips):
                slot = lands_[u].at[2 * cx + cy]
                cp = _remote(srcs_[u].at[2 * cx + cy], slot, (send_sems_, recv_sems_), 3 * u + j, (cx, cy, c))
                cp.wait_send()
                cp.wait_recv()

    hbm = [pltpu.HBM(u.shape, u.dtype) for u in srcs]
    outs = pl.pallas_call(
        body, name=name, out_shape=(*hbm, *hbm),
        in_specs=[HBM] * (2 * n) + [SEM, SEM, ANY], out_specs=[HBM] * (2 * n),
        input_output_aliases={i: i for i in range(2 * n)},
        compiler_params=pltpu.CompilerParams(has_side_effects=SIDE_EFFECT),
    )(*srcs, *lands, send_sems, recv_sems, after)
    return list(outs[n:])


def _share_halves(units, name):
    n = len(units)

    def body(*refs):
        outs, sems = refs[n:2 * n], (refs[2 * n], refs[2 * n + 1])
        x, y, c, me, sibling, chips = _place()
        sent = [_remote(o_ref.at[:, :, c], o_ref.at[:, :, c], sems, u, sibling) for u, o_ref in enumerate(outs)]
        for cp in sent:
            cp.start()
        for u, o_ref in enumerate(outs):
            theirs = o_ref.at[:, :, 1 - c]
            _remote(theirs, theirs, sems, u, sibling).wait_recv()
        for cp in sent:
            cp.wait_send()

    return _comm_call_multi(body, name, [jax.ShapeDtypeStruct(u.shape, u.dtype) for u in units], n, units,
                            aliases={u: u for u in range(n)})


def _share_start(units, name):
    n = len(units)

    def body(*refs):
        arrs, send_sems, recv_sems, token = refs[:n], refs[n], refs[n + 1], refs[-1]
        x, y, c, me, sibling, chips = _place()
        for u in range(n):
            _remote(arrs[u].at[:, :, c], arrs[u].at[:, :, c], (send_sems, recv_sems), u, sibling).start()
        token[...] = jnp.zeros_like(token)

    hbm = [pltpu.HBM(u.shape, u.dtype) for u in units]
    outs = pl.pallas_call(
        body, name=name,
        out_shape=(pltpu.SemaphoreType.DMA((n,)), pltpu.SemaphoreType.DMA((n,)), *hbm, jax.ShapeDtypeStruct((8, 128), F32)),
        in_specs=[HBM] * n, out_specs=(SEM, SEM, *[HBM] * n, pl.BlockSpec(memory_space=pltpu.VMEM)),
        input_output_aliases={i: 2 + i for i in range(n)},
        compiler_params=pltpu.CompilerParams(has_side_effects=SIDE_EFFECT),
    )(*[pltpu.with_memory_space_constraint(u, pltpu.HBM) for u in units])
    return outs[0], outs[1], list(outs[2:2 + n]), outs[-1]


def _share_wait(send_sems, recv_sems, units, after, name):
    n = len(units)

    def body(*refs):
        arrs, send_sems_, recv_sems_ = refs[:n], refs[n], refs[n + 1]
        x, y, c, me, sibling, chips = _place()
        for u in range(n):
            cp = _remote(arrs[u].at[:, :, c], arrs[u].at[:, :, 1 - c], (send_sems_, recv_sems_), u, sibling)
            cp.wait_send()
            cp.wait_recv()

    outs = pl.pallas_call(
        body, name=name, out_shape=tuple(pltpu.HBM(u.shape, u.dtype) for u in units),
        in_specs=[HBM] * n + [SEM, SEM, ANY], out_specs=[HBM] * n, input_output_aliases={i: i for i in range(n)},
        compiler_params=pltpu.CompilerParams(has_side_effects=SIDE_EFFECT),
    )(*units, send_sems, recv_sems, after)
    return list(outs)


def _rep_start(rep4):
    def body(r_ref, land_ref, send_sems, recv_sems, r_thru, land_thru):
        x, y, c, me, sibling, chips = _place()
        for j, chip in enumerate(chips):
            _remote(r_ref, land_ref.at[me], (send_sems, recv_sems), j, (*chip, c)).start()

    land = jax.ShapeDtypeStruct((N_CHIPS,) + rep4.shape, rep4.dtype)
    return pl.pallas_call(
        body, name="rep_start",
        out_shape=(pltpu.SemaphoreType.DMA((3,)), pltpu.SemaphoreType.DMA((3,)), pltpu.HBM(rep4.shape, rep4.dtype),
                   pltpu.HBM(land.shape, land.dtype)),
        in_specs=[HBM, HBM], out_specs=(SEM, SEM, HBM, HBM), input_output_aliases={0: 2, 1: 3},
        compiler_params=pltpu.CompilerParams(has_side_effects=SIDE_EFFECT),
    )(pltpu.with_memory_space_constraint(rep4, pltpu.HBM),
      pltpu.with_memory_space_constraint(lax.empty(land.shape, land.dtype), pltpu.HBM))


def _rep_wait(send_sems, recv_sems, rep4, land, after):
    def body(r_ref, land_ref, send_sems_, recv_sems_, after_ref, r_dead, land_out):
        x, y, c, me, sibling, chips = _place()
        for j, (cx, cy) in enumerate(chips):
            cp = _remote(r_ref, land_ref.at[2 * cx + cy], (send_sems_, recv_sems_), j, (cx, cy, c))
            cp.wait_send()
            cp.wait_recv()

    return pl.pallas_call(
        body, name="rep_wait", out_shape=(pltpu.HBM(rep4.shape, rep4.dtype), pltpu.HBM(land.shape, land.dtype)),
        in_specs=[HBM, HBM, SEM, SEM, ANY], out_specs=(HBM, HBM), input_output_aliases={0: 0, 1: 1},
        compiler_params=pltpu.CompilerParams(has_side_effects=SIDE_EFFECT),
    )(rep4, land, send_sems, recv_sems, after)[1]


ADAM_TILE = 64


def _add_halves(mine, theirs, place, name):
    rows, width = theirs.shape[1:]
    tr = min(rows, 256)

    def body(place_ref, a_ref, b_ref, o_ref):
        o_ref[...] = (a_ref[...].astype(F32) + b_ref[...].astype(F32)).astype(o_ref.dtype)

    blk = pl.BlockSpec((None, tr, width), lambda k, i, place_ref: (k, i, 0))
    return pl.pallas_call(
        body, name=name,
        grid_spec=pltpu.PrefetchScalarGridSpec(
            num_scalar_prefetch=1, grid=(N_CHIPS, rows // tr),
            in_specs=[pl.BlockSpec((None, None, tr, width), lambda k, i, place_ref: (k, place_ref[1], i, 0)), blk], out_specs=blk),
        out_shape=jax.ShapeDtypeStruct(theirs.shape, theirs.dtype), compiler_params=_cparams("parallel", "parallel"),
    )(place, mine, theirs)


def _adamw(place, parts, own, w, m, v, layer, name, into=None):
    rows, width = parts.shape[1:]
    layers = w.shape[0]

    def body(place_ref, p_ref, own_ref, w_ref, m_ref, v_ref, *rest):
        o_ref = rest[-1]
        g = jnp.zeros((ADAM_TILE, width), F32)
        for k in range(N_CHIPS):
            g = g + jnp.where(place_ref[0] == k, own_ref[...], p_ref[k]).astype(F32)
        m2 = ADAM_B1 * m_ref[...] + (1.0 - ADAM_B1) * g
        v2 = ADAM_B2 * v_ref[...] + (1.0 - ADAM_B2) * jnp.square(g)
        m_hat = m2 / (1.0 - ADAM_B1 ** ADAM_STEP)
        v_hat = v2 / (1.0 - ADAM_B2 ** ADAM_STEP)
        o_ref[0] = g
        o_ref[1] = -ADAM_LR * (m_hat / (jnp.sqrt(v_hat) + ADAM_EPS) + ADAM_WD * w_ref[...])
        o_ref[2] = m2
        o_ref[3] = v2

    blk = pl.BlockSpec((None, None, ADAM_TILE, width), lambda i, place_ref: (layer, place_ref[1], i, 0))
    blk4 = pl.BlockSpec((4, ADAM_TILE, width), lambda i, place_ref: (0, i, 0))
    mine = pl.BlockSpec((None, ADAM_TILE, width), lambda i, place_ref: (place_ref[0], i, 0))
    out = pl.BlockSpec((4, None, None, ADAM_TILE, width), lambda i, place_ref: (0, layer, place_ref[1], i, 0))
    extra = [] if into is None else [into]
    return pl.pallas_call(
        body, name=name,
        grid_spec=pltpu.PrefetchScalarGridSpec(num_scalar_prefetch=1, grid=(rows // ADAM_TILE,),
                                               in_specs=[blk4, mine, blk, blk, blk] + [ANY] * len(extra), out_specs=out),
        out_shape=jax.ShapeDtypeStruct((4, layers, 2, rows, width), F32),
        input_output_aliases={} if into is None else {6: 0}, compiler_params=_cparams("parallel"),
    )(place, parts, own, w, m, v, *extra)


def kernel(x, norm_gain, a_w_in, a_v_gain, a_w_s, a_b_s, a_w_out, b_w_in, b_q_gain, b_k_gain, b_w_out, c_w_in, c_w_grp, c_scale, c_w_out, loss_target, m_norm_gain, m_a_w_in, m_a_v_gain, m_a_w_s, m_a_b_s, m_a_w_out, m_b_w_in, m_b_q_gain, m_b_k_gain, m_b_w_out, m_c_w_in, m_c_w_grp, m_c_scale, m_c_w_out, v_norm_gain, v_a_w_in, v_a_v_gain, v_a_w_s, v_a_b_s, v_a_w_out, v_b_w_in, v_b_q_gain, v_b_k_gain, v_b_w_out, v_c_w_in, v_c_w_grp, v_c_scale, v_c_w_out):
    wts = dict(norm_gain=norm_gain, a_w_in=a_w_in, a_v_gain=a_v_gain, a_w_s=a_w_s, a_b_s=a_b_s, a_w_out=a_w_out, b_w_in=b_w_in,
               b_q_gain=b_q_gain, b_k_gain=b_k_gain, b_w_out=b_w_out, c_w_in=c_w_in, c_w_grp=c_w_grp, c_scale=c_scale, c_w_out=c_w_out)
    mom1 = dict(norm_gain=m_norm_gain, a_w_in=m_a_w_in, a_v_gain=m_a_v_gain, a_w_s=m_a_w_s, a_b_s=m_a_b_s, a_w_out=m_a_w_out,
                b_w_in=m_b_w_in, b_q_gain=m_b_q_gain, b_k_gain=m_b_k_gain, b_w_out=m_b_w_out, c_w_in=m_c_w_in, c_w_grp=m_c_w_grp,
                c_scale=m_c_scale, c_w_out=m_c_w_out)
    mom2 = dict(norm_gain=v_norm_gain, a_w_in=v_a_w_in, a_v_gain=v_a_v_gain, a_w_s=v_a_w_s, a_b_s=v_a_b_s, a_w_out=v_a_w_out,
                b_w_in=v_b_w_in, b_q_gain=v_b_q_gain, b_k_gain=v_b_k_gain, b_w_out=v_b_w_out, c_w_in=v_c_w_in, c_w_grp=v_c_w_grp,
                c_scale=v_c_scale, c_w_out=v_c_w_out)
    axes = ("x", "y", "c")
    me = 2 * lax.axis_index("x") + lax.axis_index("y")
    core = lax.axis_index("c")

    place = jnp.stack([me, core]).astype(jnp.int32)

    slots = _gather_inputs(wts, me)
    first = _allgather_units([slots[u] for u in FIRST_UNITS])
    full = _gathered_weights(FIRST_UNITS, first)
    later = sorted(LATER_UNITS)
    gather_sems, gather_units, token = _gather_start([[slots[u] for u in LATER_UNITS[li]] for li in later], first[-1])
    for n, _ in REP:
        full[n] = wts[n]
    full["norm_gain"] = wts["norm_gain"] + token[0, 0]

    def layer_weights(li, x_in):
        if li not in LATER_UNITS:
            return {}
        g = later.index(li)
        landed = _gather_wait(gather_units[g], gather_sems[g], x_in, f"gather_wait_l{li}")
        return _gathered_weights(LATER_UNITS[li], _forward_units(landed, f"gather_forward_l{li}"))

    def unit_grad(u, grads):
        n, layer, r, w = REDUCE_UNITS[u]
        g = grads[n][layer] if n in ("a_w_in", "a_w_out") else grads[n]
        return g.astype(CDT).reshape(N_CHIPS, 2, r, w)

    chip_sums, in_flight = {}, []

    def start_exchange(li, grads):
        if li not in EARLY_UNITS:
            return jnp.zeros((), F32)
        units = EARLY_UNITS[li]
        mine = [unit_grad(u, grads) for u in units]
        theirs = _swap_halves(mine, f"grad_swap_halves_l{li}")
        sums = [_add_halves(g, t, place, f"grad_add_halves{u}") for u, g, t in zip(units, mine, theirs)]
        chip_sums.update(zip(units, sums))
        send_sems, recv_sems, srcs, lands, token = _scatter_start(sums, f"grad_scatter_start_l{li}")
        in_flight.append((li, units, send_sems, recv_sems, srcs, lands))
        return token[0, 0]

    sq, grad_x, grads = _local_step(x[0], loss_target[0], full, start_exchange, layer_weights)
    loss = lax.psum(0.5 * jnp.sum(sq) / D_MODEL, axes)

    rep_g = _pack_rep(grads)
    late = [u for u in range(len(REDUCE_UNITS)) if not any(u in us for us in EARLY_UNITS.values())]
    mine = [unit_grad(u, grads) for u in late]
    mine.append(jnp.stack([
        _small_unit([lax.slice_in_dim(grads[v], 512 * k, 512 * (k + 1), axis=1) for v in SMALL_OF_HALF],
                    rep_g[k * REP_PART:(k + 1) * REP_PART], CDT) for k in range(N_CHIPS)]))
    late.append(len(REDUCE_UNITS))
    theirs = _swap_halves(mine, "grad_swap_halves_late")
    sums = [_add_halves(g, t, place, f"grad_add_halves{u}") for u, g, t in zip(late, mine, theirs)]
    chip_sums.update(zip(late, sums))
    parts = dict(zip(late, _scatter_shards(sums)))
    for li, units, send_sems, recv_sems, srcs, lands in in_flight:
        parts.update(zip(units, _scatter_wait(send_sems, recv_sems, srcs, lands, grad_x, f"grad_scatter_wait_l{li}")))
    order = range(len(REDUCE_UNITS) + 1)
    parts, chip_sums = [parts[u] for u in order], [chip_sums[u] for u in order]

    def state_units(tree):
        units = [tree[n].reshape(BLOCK_SHAPES[n][0], 2, r, w) for n, _, r, w in REDUCE_UNITS]
        rep_part = lax.dynamic_slice_in_dim(_pack_rep(tree), me * REP_PART, REP_PART, axis=0)
        return units + [_small_unit([tree[v] for v in SMALL_OF_HALF], rep_part, F32)[None]]

    names = [n for n, _, _, _ in REDUCE_UNITS] + ["small"]
    layers = [layer for _, layer, _, _ in REDUCE_UNITS] + [0]
    res, state = {}, list(zip(parts, chip_sums, state_units(wts), state_units(mom1), state_units(mom2)))

    def update(units):
        for u in units:
            res[names[u]] = _adamw(place, *state[u], layers[u], f"adamw{u}", into=res.get(names[u]))

    update((len(REDUCE_UNITS),))
    small = _share_halves([res.pop("small")], "share_small")[0][:, 0]
    rep_mine = jnp.concatenate([small[:, 0, SMALL_PAD:], small[:, 1, SMALL_PAD:]], axis=1)
    rep_sems = _rep_start(rep_mine)
    update(LAYER_UNITS[1] + LAYER_UNITS[2])
    early = list(res)
    send_sems, recv_sems, early_arrs, _ = _share_start([res[n] for n in early], "share_start")
    update(LAYER_UNITS[0] + LAYER_UNITS[3])
    rest = [n for n in res if n not in early]
    res.update(zip(rest, _share_halves([res[n] for n in rest], "share_halves")))
    res.update(zip(early, _share_wait(send_sems, recv_sems, early_arrs, res[rest[0]], "share_wait")))
    rep_all = _rep_wait(*rep_sems, res[rest[0]])
    rep_all = lax.dynamic_update_slice_in_dim(rep_all, rep_mine[None], me, axis=0)

    outs = []
    for q in range(4):
        tree = {n: arr[q].reshape(BLOCK_SHAPES[n]) for n, arr in res.items()}
        tree["a_v_gain"] = small[q, 0, 0].reshape(2, 512)
        tree["c_scale"] = small[q, 1, 0, :512].reshape(1, 512)
        tree.update(_unpack_rep(rep_all[:, q].reshape(N_CHIPS * REP_PART, ROW)))
        outs.append(tree)
    return (loss, grad_x[None], *[t[n] for t in outs for n in WEIGHTS])
```

```python
import functools
import math

import jax
import jax.numpy as jnp
from jax import lax
from jax.experimental import pallas as pl
from jax.experimental.pallas import tpu as pltpu

F32 = jnp.float32
CDT = jnp.bfloat16

D_MODEL = 1024
EPS = 1e-6
CHUNK = 128
A_WIDTH = 2048
A_GROUPS = 8
A_GROUP_DIM = 256
HEAD_DIM = 128
B_HEADS = 8
B_DILATIONS = (1, 4, 16)
B_QK = 6144
B_IN = 10240
ROPE_HALF = 16
ROPE_THETA = 500000.0
POOL_SIZES = (2, 4, 8, 16)
POOL_HALO = 16
C_WIDTH = 2048
C_GROUP = 512
N_CHIPS = 4

ADAM_LR = 0.001
ADAM_B1 = 0.9
ADAM_B2 = 0.999
ADAM_EPS = 1e-08
ADAM_WD = 0.01
ADAM_STEP = 10

VMEM_LIMIT = 48 * 1024 * 1024
ANY = pl.BlockSpec(memory_space=pl.ANY)
MESH = pl.DeviceIdType.MESH

NN = (((1,), (0,)), ((), ()))
NT = (((1,), (1,)), ((), ()))
TN = (((0,), (0,)), ((), ()))


def _cparams(*sem):
    return pltpu.CompilerParams(dimension_semantics=sem, vmem_limit_bytes=VMEM_LIMIT)


def _dot(a, b, dims=NN):
    return lax.dot_general(a, b, dims, preferred_element_type=F32)


def _sigmoid(z):
    return 1.0 / (1.0 + jnp.exp(-z))


def _lane_sums(v):
    ones = jnp.ones((HEAD_DIM, HEAD_DIM), jnp.bfloat16)
    hi = v.astype(jnp.bfloat16)
    lo = (v - hi.astype(F32)).astype(jnp.bfloat16)
    return _dot(hi, ones) + _dot(lo, ones)


def _mm(a, b, mode, out_dtype, name, mnk, tiles, b_spec=None, o_spec=None, o_shape=None, add=None):
    m, n, k = mnk
    tm, tn, tk = min(tiles[0], m), min(tiles[1], n), min(tiles[2], k)
    nk = k // tk
    a_spec = {"nn": pl.BlockSpec((tm, tk), lambda i, j, q: (i, q)),
              "nt": pl.BlockSpec((tm, tk), lambda i, j, q: (i, q)),
              "tn": pl.BlockSpec((tk, tm), lambda i, j, q: (q, i))}[mode]
    if b_spec is None:
        b_spec = {"nn": pl.BlockSpec((tk, tn), lambda i, j, q: (q, j)),
                  "nt": pl.BlockSpec((tn, tk), lambda i, j, q: (j, q)),
                  "tn": pl.BlockSpec((tk, tn), lambda i, j, q: (q, j))}[mode]
    if o_spec is None:
        o_spec, o_shape = pl.BlockSpec((tm, tn), lambda i, j, q: (i, j)), (m, n)
    dims = {"nn": NN, "nt": NT, "tn": TN}[mode]
    has_add = add is not None

    def body(*refs):
        a_ref, b_ref = refs[0], refs[1]
        o_ref = refs[3] if has_add else refs[2]
        p = _dot(a_ref[...], b_ref[...], dims)

        def finish(v):
            if has_add:
                v = v + refs[2][...]
            o_ref[...] = v.astype(o_ref.dtype)

        if nk == 1:
            finish(p)
        else:
            acc_ref = refs[-1]
            q = pl.program_id(2)

            @pl.when(q == 0)
            def _():
                acc_ref[...] = p

            @pl.when(q > 0)
            def _():
                acc_ref[...] += p

            @pl.when(q == nk - 1)
            def _():
                finish(acc_ref[...])

    in_specs = [a_spec, b_spec]
    args = [a, b]
    if has_add:
        in_specs.append(pl.BlockSpec((tm, tn), lambda i, j, q: (i, j)))
        args.append(add)
    return pl.pallas_call(
        body, name=name, grid=(m // tm, n // tn, nk), in_specs=in_specs, out_specs=o_spec,
        out_shape=jax.ShapeDtypeStruct(o_shape, out_dtype),
        scratch_shapes=[pltpu.VMEM((tm, tn), F32)] if nk > 1 else [],
        compiler_params=_cparams("parallel", "parallel", "arbitrary"),
    )(*args)


def _rms_fwd(x, g, name, tq=512):
    s, d = x.shape

    def body(x_ref, g_ref, h_ref):
        xv = x_ref[...]
        r = lax.rsqrt(jnp.mean(xv * xv, axis=-1, keepdims=True) + EPS)
        h_ref[...] = (xv * r * g_ref[...]).astype(h_ref.dtype)

    return pl.pallas_call(
        body, name=name, grid=(s // tq,),
        in_specs=[pl.BlockSpec((tq, d), lambda i: (i, 0)), pl.BlockSpec((1, d), lambda i: (0, 0))],
        out_specs=pl.BlockSpec((tq, d), lambda i: (i, 0)),
        out_shape=jax.ShapeDtypeStruct((s, d), CDT), compiler_params=_cparams("parallel"),
    )(x, g)


def _loss_bwd(y, target, name, tq=512):
    s, d = y.shape

    def body(y_ref, t_ref, dx_ref, dxc_ref, sq_ref):
        err = y_ref[...] - t_ref[...]
        dx = err * (1.0 / d)
        dx_ref[...] = dx
        dxc_ref[...] = dx.astype(dxc_ref.dtype)

        @pl.when(pl.program_id(0) == 0)
        def _():
            sq_ref[...] = jnp.zeros_like(sq_ref)

        sq_ref[...] += jnp.sum(err * err, axis=0, keepdims=True)

    blk = pl.BlockSpec((tq, d), lambda i: (i, 0))
    vec = pl.BlockSpec((1, d), lambda i: (0, 0))
    return pl.pallas_call(
        body, name=name, grid=(s // tq,), in_specs=[blk, blk], out_specs=[blk, blk, vec],
        out_shape=[jax.ShapeDtypeStruct((s, d), F32), jax.ShapeDtypeStruct((s, d), CDT), jax.ShapeDtypeStruct((1, d), F32)],
        compiler_params=_cparams("arbitrary"),
    )(y, target)


def _tril_mask():
    row = lax.broadcasted_iota(jnp.int32, (CHUNK, CHUNK), 0)
    col = lax.broadcasted_iota(jnp.int32, (CHUNK, CHUNK), 1)
    return row >= col


def _a_mid_fwd(proj, v_gain, w_s, b_s_t, name, tq=256):
    s = proj.shape[0]

    def body(p_ref, vg_ref, ws_ref, bs_ref, y_ref):
        vraw = p_ref[:, A_WIDTH:2 * A_WIDTH].astype(F32)
        r = lax.rsqrt(jnp.mean(vraw * vraw, axis=-1, keepdims=True) + EPS)
        vn = (vraw * r * vg_ref[...]).astype(CDT)
        tri = _tril_mask()
        for g in range(A_GROUPS):
            w = jnp.where(tri, ws_ref[g], 0.0).astype(CDT)
            bias = bs_ref[:, g:g + 1]
            cols = slice(g * A_GROUP_DIM, (g + 1) * A_GROUP_DIM)
            zcols = slice(2 * A_WIDTH + g * A_GROUP_DIM, 2 * A_WIDTH + (g + 1) * A_GROUP_DIM)
            for c in range(tq // CHUNK):
                rows = slice(c * CHUNK, (c + 1) * CHUNK)
                mixed = _dot(w, vn[rows, cols]) + bias
                u = p_ref[rows, cols].astype(F32)
                z = p_ref[rows, zcols].astype(F32)
                y_ref[rows, cols] = (u * mixed * (z * _sigmoid(z))).astype(y_ref.dtype)

    return pl.pallas_call(
        body, name=name, grid=(s // tq,),
        in_specs=[pl.BlockSpec((tq, 3 * A_WIDTH), lambda i: (i, 0)), pl.BlockSpec((1, A_WIDTH), lambda i: (0, 0)),
                  pl.BlockSpec((A_GROUPS, CHUNK, CHUNK), lambda i: (0, 0, 0)), pl.BlockSpec((CHUNK, A_GROUPS), lambda i: (0, 0))],
        out_specs=pl.BlockSpec((tq, A_WIDTH), lambda i: (i, 0)),
        out_shape=jax.ShapeDtypeStruct((s, A_WIDTH), CDT), compiler_params=_cparams("parallel"),
    )(proj, v_gain, w_s, b_s_t)


def _a_mid_bwd(proj, dy, v_gain, w_s, b_s_t, name, tq=256):
    s = proj.shape[0]

    def body(p_ref, dy_ref, vg_ref, ws_ref, bs_ref, dp_ref, dws_ref, dbs_ref, dvg_ref, dvn_ref):
        @pl.when(pl.program_id(0) == 0)
        def _():
            dws_ref[...] = jnp.zeros_like(dws_ref)
            dbs_ref[...] = jnp.zeros_like(dbs_ref)
            dvg_ref[...] = jnp.zeros_like(dvg_ref)

        vraw = p_ref[:, A_WIDTH:2 * A_WIDTH].astype(F32)
        r = lax.rsqrt(jnp.mean(vraw * vraw, axis=-1, keepdims=True) + EPS)
        vhat = vraw * r
        vg = vg_ref[...]
        vn = (vhat * vg).astype(CDT)
        tri = _tril_mask()
        lane = lax.broadcasted_iota(jnp.int32, (CHUNK, A_GROUPS), 1)
        dbs = jnp.zeros((CHUNK, A_GROUPS), F32)
        for g in range(A_GROUPS):
            w = jnp.where(tri, ws_ref[g], 0.0).astype(CDT)
            bias = bs_ref[:, g:g + 1]
            cols = slice(g * A_GROUP_DIM, (g + 1) * A_GROUP_DIM)
            zcols = slice(2 * A_WIDTH + g * A_GROUP_DIM, 2 * A_WIDTH + (g + 1) * A_GROUP_DIM)
            dws = jnp.zeros((CHUNK, CHUNK), F32)
            for c in range(tq // CHUNK):
                rows = slice(c * CHUNK, (c + 1) * CHUNK)
                vn_g = vn[rows, cols]
                mixed = _dot(w, vn_g) + bias
                u = p_ref[rows, cols].astype(F32)
                z = p_ref[rows, zcols].astype(F32)
                dyv = dy_ref[rows, cols].astype(F32)
                sg = _sigmoid(z)
                sz = z * sg
                dyu = dyv * u
                dmixed = dyu * sz
                dp_ref[rows, cols] = (dyv * mixed * sz).astype(dp_ref.dtype)
                dp_ref[rows, zcols] = (dyu * mixed * (sg * (1.0 + z * (1.0 - sg)))).astype(dp_ref.dtype)
                dmc = dmixed.astype(CDT)
                dws = dws + _dot(dmc, vn_g, NT)
                dbs = dbs + jnp.where(lane == g, jnp.sum(dmixed, axis=-1, keepdims=True), 0.0)
                dvn_ref[rows, cols] = _dot(w, dmc, TN)
            dws_ref[g] += jnp.where(tri, dws, 0.0)
        dbs_ref[...] += dbs
        dvn = dvn_ref[...]
        gd = dvn * vg
        dvraw = r * gd - vraw * (r * r * r) * jnp.mean(gd * vraw, axis=-1, keepdims=True)
        dp_ref[:, A_WIDTH:2 * A_WIDTH] = dvraw.astype(dp_ref.dtype)
        dvg_ref[...] += jnp.sum(dvn * vhat, axis=0, keepdims=True)

    return pl.pallas_call(
        body, name=name, grid=(s // tq,),
        in_specs=[pl.BlockSpec((tq, 3 * A_WIDTH), lambda i: (i, 0)), pl.BlockSpec((tq, A_WIDTH), lambda i: (i, 0)),
                  pl.BlockSpec((1, A_WIDTH), lambda i: (0, 0)), pl.BlockSpec((A_GROUPS, CHUNK, CHUNK), lambda i: (0, 0, 0)),
                  pl.BlockSpec((CHUNK, A_GROUPS), lambda i: (0, 0))],
        out_specs=[pl.BlockSpec((tq, 3 * A_WIDTH), lambda i: (i, 0)), pl.BlockSpec((A_GROUPS, CHUNK, CHUNK), lambda i: (0, 0, 0)),
                   pl.BlockSpec((CHUNK, A_GROUPS), lambda i: (0, 0)), pl.BlockSpec((1, A_WIDTH), lambda i: (0, 0))],
        out_shape=[jax.ShapeDtypeStruct((s, 3 * A_WIDTH), CDT), jax.ShapeDtypeStruct((A_GROUPS, CHUNK, CHUNK), F32),
                   jax.ShapeDtypeStruct((CHUNK, A_GROUPS), F32), jax.ShapeDtypeStruct((1, A_WIDTH), F32)],
        scratch_shapes=[pltpu.VMEM((tq, A_WIDTH), F32)],
        compiler_params=_cparams("arbitrary"),
    )(proj, dy, v_gain, w_s, b_s_t)


def _rope_tables(s):
    inv_freq = jnp.power(jnp.float32(ROPE_THETA), -jnp.arange(ROPE_HALF, dtype=F32) / ROPE_HALF)
    ang = jnp.arange(s, dtype=F32)[:, None] * inv_freq[None, :]
    cos, sin = jnp.cos(ang), jnp.sin(ang)
    rest = HEAD_DIM - 2 * ROPE_HALF
    t_c = jnp.concatenate([cos, cos, jnp.ones((s, rest), F32)], axis=1)
    t_a = jnp.concatenate([-sin, jnp.zeros((s, HEAD_DIM - ROPE_HALF), F32)], axis=1)
    t_b = jnp.concatenate([jnp.zeros((s, ROPE_HALF), F32), sin, jnp.zeros((s, rest), F32)], axis=1)
    return t_c, t_a, t_b


def _b_qk_fwd(proj, gains, tabs, name, tq=256):
    s = proj.shape[0]

    def body(p_ref, g_ref, tc_ref, ta_ref, tb_ref, o_ref):
        tc, ta, tb = tc_ref[...], ta_ref[...], tb_ref[...]
        for tg in range(6):
            gain = g_ref[tg:tg + 1, :]
            for h in range(B_HEADS):
                cols = slice(tg * 1024 + h * HEAD_DIM, tg * 1024 + (h + 1) * HEAD_DIM)
                xv = p_ref[:, cols].astype(F32)
                r = lax.rsqrt(_lane_sums(xv * xv) * (1.0 / HEAD_DIM) + EPS)
                xn = xv * r * gain
                y = xn * tc + pltpu.roll(xn, HEAD_DIM - ROPE_HALF, 1) * ta + pltpu.roll(xn, ROPE_HALF, 1) * tb
                o_ref[:, cols] = y.astype(o_ref.dtype)

    tab = pl.BlockSpec((tq, HEAD_DIM), lambda i: (i, 0))
    return pl.pallas_call(
        body, name=name, grid=(s // tq,),
        in_specs=[pl.BlockSpec((tq, B_QK), lambda i: (i, 0)), pl.BlockSpec((6, HEAD_DIM), lambda i: (0, 0)), tab, tab, tab],
        out_specs=pl.BlockSpec((tq, B_QK), lambda i: (i, 0)),
        out_shape=jax.ShapeDtypeStruct((s, B_QK), CDT), compiler_params=_cparams("parallel"),
    )(proj, gains, *tabs)


PERMUTE_BLOCK_BYTES = 4 * 1024 * 1024


def _view_rows(length, dil, width, itemsize):
    rows = 16
    while 2 * rows * dil * width * itemsize <= PERMUTE_BLOCK_BYTES and 2 * rows <= length:
        rows *= 2
    return rows


def _to_view(x, col, width, dil, name):
    s = x.shape[0]
    length = s // dil
    tl = _view_rows(length, dil, width, 4)
    lanes = HEAD_DIM
    nblk = width // lanes

    def body(x_ref, o_ref, slab_ref):
        for b in range(nblk):
            slab_ref[b] = x_ref[:, b * lanes:(b + 1) * lanes].astype(F32)
        for r in range(dil):
            for b in range(nblk):
                o_ref[:, r * width + b * lanes:r * width + (b + 1) * lanes] = (
                    slab_ref.at[b][pl.ds(r, tl, stride=dil), :].astype(o_ref.dtype))

    return pl.pallas_call(
        body, name=name, grid=(length // tl,),
        in_specs=[pl.BlockSpec((tl * dil, width), lambda i: (i, col))],
        out_specs=pl.BlockSpec((tl, dil * width), lambda i: (i, 0)),
        out_shape=jax.ShapeDtypeStruct((length, dil * width), x.dtype),
        scratch_shapes=[pltpu.VMEM((nblk, tl * dil, lanes), F32)],
        compiler_params=_cparams("parallel"),
    )(x)


def _from_view(v, dil, name):
    length, width = v.shape[0], v.shape[1] // dil
    tl = _view_rows(length, dil, width, 4)
    lanes = HEAD_DIM
    nblk = width // lanes

    def body(v_ref, o_ref, slab_ref):
        for r in range(dil):
            for b in range(nblk):
                slab_ref.at[b][pl.ds(r, tl, stride=dil), :] = v_ref[:, r * width + b * lanes:r * width + (b + 1) * lanes].astype(F32)
        for b in range(nblk):
            o_ref[:, b * lanes:(b + 1) * lanes] = slab_ref[b].astype(o_ref.dtype)

    return pl.pallas_call(
        body, name=name, grid=(length // tl,),
        in_specs=[pl.BlockSpec((tl, dil * width), lambda i: (i, 0))],
        out_specs=pl.BlockSpec((tl * dil, width), lambda i: (i, 0)),
        out_shape=jax.ShapeDtypeStruct((length * dil, width), v.dtype),
        scratch_shapes=[pltpu.VMEM((nblk, tl * dil, lanes), F32)],
        compiler_params=_cparams("parallel"),
    )(v)


def _b_attn_fwd(q, k, v, bases, dil, name):
    length = q.shape[0]
    nb = length // CHUNK
    per = 2 if nb % 2 == 0 else 1
    scale = 1.0 / math.sqrt(HEAD_DIM)
    w = B_HEADS * HEAD_DIM
    qb, kb, vb = bases

    def body(q_ref, kc_ref, kp_ref, vc_ref, vp_ref, o_ref, lse_ref):
        n = pl.program_id(1)
        qi = lax.broadcasted_iota(jnp.int32, (CHUNK, 2 * CHUNK), 0)
        ki = lax.broadcasted_iota(jnp.int32, (CHUNK, 2 * CHUNK), 1)
        first_key = jnp.where(n > 0, 0, CHUNK)
        band = (ki >= qi) & (ki <= qi + CHUNK)
        lane = lax.broadcasted_iota(jnp.int32, (CHUNK, HEAD_DIM), 1)
        for b in range(per):
            rows = slice(b * CHUNK, (b + 1) * CHUNK)
            above = slice((b - 1) * CHUNK, b * CHUNK)
            mask = band & (ki >= first_key) if b == 0 else band
            lse_all = jnp.zeros((CHUNK, HEAD_DIM), F32)
            for h in range(B_HEADS):
                sl = slice(h * HEAD_DIM, (h + 1) * HEAD_DIM)
                k_prev = kp_ref[:, sl] if b == 0 else kc_ref[above, sl]
                v_prev = vp_ref[:, sl] if b == 0 else vc_ref[above, sl]
                k2 = jnp.concatenate([k_prev, kc_ref[rows, sl]], axis=0)
                v2 = jnp.concatenate([v_prev, vc_ref[rows, sl]], axis=0)
                sc = jnp.where(mask, _dot(q_ref[rows, sl], k2, NT) * scale, -1e30)
                m = jnp.max(sc, axis=-1, keepdims=True)
                p = jnp.exp(sc - m)
                l = jnp.sum(p, axis=-1, keepdims=True)
                o_ref[rows, sl] = (_dot(p.astype(CDT), v2) / l).astype(o_ref.dtype)
                lse_all = jnp.where(lane == h, m + jnp.log(l), lse_all)
            lse_ref[rows, :] = lse_all

    prev = lambda n: jnp.maximum(per * n - 1, 0)
    blk = lambda f: pl.BlockSpec((per * CHUNK, w), f)
    halo = lambda f: pl.BlockSpec((CHUNK, w), f)
    return pl.pallas_call(
        body, name=name, grid=(dil, nb // per),
        in_specs=[blk(lambda r, n: (n, qb + r)), blk(lambda r, n: (n, kb + r)), halo(lambda r, n: (prev(n), kb + r)),
                  blk(lambda r, n: (n, vb + r)), halo(lambda r, n: (prev(n), vb + r))],
        out_specs=[blk(lambda r, n: (n, r)), pl.BlockSpec((per * CHUNK, HEAD_DIM), lambda r, n: (n, r))],
        out_shape=[jax.ShapeDtypeStruct((length, dil * w), CDT), jax.ShapeDtypeStruct((length, dil * HEAD_DIM), F32)],
        compiler_params=_cparams("parallel", "parallel"),
    )(q, k, k, v, v)


def _b_combine(os_, lses, proj, name, tq=512):
    s = proj.shape[0]
    w = B_HEADS * HEAD_DIM

    def body(o0_ref, o1_ref, o2_ref, l0_ref, l1_ref, l2_ref, z_ref, y_ref, oj_ref, lj_ref):
        l0, l1, l2 = l0_ref[...], l1_ref[...], l2_ref[...]
        m = jnp.maximum(jnp.maximum(l0, l1), l2)
        lj = m + jnp.log(jnp.exp(l0 - m) + jnp.exp(l1 - m) + jnp.exp(l2 - m))
        lj_ref[...] = lj
        w0, w1, w2 = jnp.exp(l0 - lj), jnp.exp(l1 - lj), jnp.exp(l2 - lj)
        for h in range(B_HEADS):
            sl = slice(h * HEAD_DIM, (h + 1) * HEAD_DIM)
            o = (w0[:, h:h + 1] * o0_ref[:, sl].astype(F32) + w1[:, h:h + 1] * o1_ref[:, sl].astype(F32)
                 + w2[:, h:h + 1] * o2_ref[:, sl].astype(F32))
            z = z_ref[:, sl].astype(F32)
            oj_ref[:, sl] = o.astype(oj_ref.dtype)
            y_ref[:, sl] = (o * (z * _sigmoid(z))).astype(y_ref.dtype)

    blk = pl.BlockSpec((tq, w), lambda i: (i, 0))
    st = pl.BlockSpec((tq, HEAD_DIM), lambda i: (i, 0))
    return pl.pallas_call(
        body, name=name, grid=(s // tq,),
        in_specs=[blk, blk, blk, st, st, st, pl.BlockSpec((tq, w), lambda i: (i, 9))],
        out_specs=[blk, blk, st],
        out_shape=[jax.ShapeDtypeStruct((s, w), CDT), jax.ShapeDtypeStruct((s, w), CDT), jax.ShapeDtypeStruct((s, HEAD_DIM), F32)],
        compiler_params=_cparams("parallel"),
    )(*os_, *lses, proj)


def _b_bwd_prep(dy, oj, proj, name, tq=512):
    s = proj.shape[0]
    w = B_HEADS * HEAD_DIM

    def body(dy_ref, oj_ref, z_ref, do_ref, dz_ref, dd_ref):
        lane = lax.broadcasted_iota(jnp.int32, (tq, HEAD_DIM), 1)
        dd = jnp.zeros((tq, HEAD_DIM), F32)
        for h in range(B_HEADS):
            sl = slice(h * HEAD_DIM, (h + 1) * HEAD_DIM)
            z = z_ref[:, sl].astype(F32)
            dyv = dy_ref[:, sl].astype(F32)
            o = oj_ref[:, sl].astype(F32)
            sg = _sigmoid(z)
            do = dyv * (z * sg)
            do_ref[:, sl] = do.astype(do_ref.dtype)
            dz_ref[:, sl] = (dyv * o * (sg * (1.0 + z * (1.0 - sg)))).astype(dz_ref.dtype)
            dd = jnp.where(lane == h, jnp.sum(do * o, axis=-1, keepdims=True), dd)
        dd_ref[...] = dd

    blk = pl.BlockSpec((tq, w), lambda i: (i, 0))
    st = pl.BlockSpec((tq, HEAD_DIM), lambda i: (i, 0))
    return pl.pallas_call(
        body, name=name, grid=(s // tq,),
        in_specs=[blk, blk, pl.BlockSpec((tq, w), lambda i: (i, 9))], out_specs=[blk, blk, st],
        out_shape=[jax.ShapeDtypeStruct((s, w), CDT), jax.ShapeDtypeStruct((s, w), CDT), jax.ShapeDtypeStruct((s, HEAD_DIM), F32)],
        compiler_params=_cparams("parallel"),
    )(dy, oj, proj)


def _b_attn_bwd(q, k, v, bases, do, lj, dd, dil, name):
    length = q.shape[0]
    nb = length // CHUNK
    scale = 1.0 / math.sqrt(HEAD_DIM)
    w = B_HEADS * HEAD_DIM
    qb, kb, vb = bases

    def body(qj_ref, qn_ref, k_ref, v_ref, doj_ref, don_ref, lj_ref, ln_ref, dj_ref, dn_ref, out_ref, carry_ref):
        j = pl.program_id(1)

        @pl.when(j == 0)
        def _():
            carry_ref[...] = jnp.zeros_like(carry_ref)

        qi = lax.broadcasted_iota(jnp.int32, (2 * CHUNK, CHUNK), 0)
        ki = lax.broadcasted_iota(jnp.int32, (2 * CHUNK, CHUNK), 1)
        no_next = jnp.where(j + 1 < nb, 0, 2 * CHUNK)
        mask = ((qi < CHUNK) & (ki <= qi)) | ((qi >= CHUNK) & (ki >= qi - CHUNK + no_next))
        for h in range(B_HEADS):
            sl = slice(h * HEAD_DIM, (h + 1) * HEAD_DIM)
            q2 = jnp.concatenate([qj_ref[:, sl], qn_ref[:, sl]], axis=0)
            do2 = jnp.concatenate([doj_ref[:, sl], don_ref[:, sl]], axis=0)
            lse2 = jnp.concatenate([lj_ref[:, h:h + 1], ln_ref[:, h:h + 1]], axis=0)
            d2 = jnp.concatenate([dj_ref[:, h:h + 1], dn_ref[:, h:h + 1]], axis=0)
            k = k_ref[:, sl]
            v = v_ref[:, sl]
            sc = _dot(q2, k, NT) * scale
            p = jnp.where(mask, jnp.exp(sc - lse2), 0.0)
            dp = _dot(do2, v, NT)
            ds = (p * (dp - d2) * scale).astype(CDT)
            dq2 = _dot(ds, k)
            out_ref[:, sl] = (carry_ref[:, sl] + dq2[:CHUNK]).astype(out_ref.dtype)
            carry_ref[:, sl] = dq2[CHUNK:]
            out_ref[:, w + h * HEAD_DIM:w + (h + 1) * HEAD_DIM] = _dot(ds, q2, TN).astype(out_ref.dtype)
            out_ref[:, 2 * w + h * HEAD_DIM:2 * w + (h + 1) * HEAD_DIM] = _dot(p.astype(CDT), do2, TN).astype(out_ref.dtype)

    nxt = lambda j: jnp.minimum(j + 1, nb - 1)
    blk = lambda f: pl.BlockSpec((CHUNK, w), f)
    st = lambda f: pl.BlockSpec((CHUNK, HEAD_DIM), f)
    return pl.pallas_call(
        body, name=name, grid=(dil, nb),
        in_specs=[blk(lambda r, j: (j, qb + r)), blk(lambda r, j: (nxt(j), qb + r)),
                  blk(lambda r, j: (j, kb + r)), blk(lambda r, j: (j, vb + r)),
                  blk(lambda r, j: (j, r)), blk(lambda r, j: (nxt(j), r)),
                  st(lambda r, j: (j, r)), st(lambda r, j: (nxt(j), r)), st(lambda r, j: (j, r)), st(lambda r, j: (nxt(j), r))],
        out_specs=pl.BlockSpec((CHUNK, 3 * w), lambda r, j: (j, r)),
        out_shape=jax.ShapeDtypeStruct((length, dil * 3 * w), CDT),
        scratch_shapes=[pltpu.VMEM((CHUNK, w), F32)],
        compiler_params=_cparams("parallel", "arbitrary"),
    )(q, q, k, v, do, do, lj, lj, dd, dd)


def _b_qk_bwd(proj, dqkv, dz, gains, tabs, name, tq=256):
    s = proj.shape[0]
    w = B_HEADS * HEAD_DIM

    def body(p_ref, d0_ref, d1_ref, d2_ref, dz_ref, g_ref, tc_ref, ta_ref, tb_ref, dp_ref, dg_ref):
        @pl.when(pl.program_id(0) == 0)
        def _():
            dg_ref[...] = jnp.zeros_like(dg_ref)

        tc, ta, tb = tc_ref[...], ta_ref[...], tb_ref[...]
        d_refs = (d0_ref, d1_ref, d2_ref)
        for g in range(3):
            for t in range(2):
                tg = t * 3 + g
                gain = g_ref[tg:tg + 1, :]
                dgain = jnp.zeros((1, HEAD_DIM), F32)
                for h in range(B_HEADS):
                    cols = slice(tg * w + h * HEAD_DIM, tg * w + (h + 1) * HEAD_DIM)
                    xv = p_ref[:, cols].astype(F32)
                    r = lax.rsqrt(_lane_sums(xv * xv) * (1.0 / HEAD_DIM) + EPS)
                    dyv = d_refs[g][:, t * w + h * HEAD_DIM:t * w + (h + 1) * HEAD_DIM].astype(F32)
                    dxn = dyv * tc + pltpu.roll(dyv * ta, ROPE_HALF, 1) + pltpu.roll(dyv * tb, HEAD_DIM - ROPE_HALF, 1)
                    gd = dxn * gain
                    dx = r * gd - xv * (r * r * r) * (_lane_sums(gd * xv) * (1.0 / HEAD_DIM))
                    dp_ref[:, cols] = dx.astype(dp_ref.dtype)
                    dgain = dgain + jnp.sum(dxn * xv * r, axis=0, keepdims=True)
                dg_ref[tg:tg + 1, :] += dgain
            dp_ref[:, (6 + g) * w:(7 + g) * w] = d_refs[g][:, 2 * w:3 * w]
        dp_ref[:, 9 * w:10 * w] = dz_ref[...]

    tab = pl.BlockSpec((tq, HEAD_DIM), lambda i: (i, 0))
    dblk = pl.BlockSpec((tq, 3 * w), lambda i: (i, 0))
    return pl.pallas_call(
        body, name=name, grid=(s // tq,),
        in_specs=[pl.BlockSpec((tq, B_QK), lambda i: (i, 0)), dblk, dblk, dblk, pl.BlockSpec((tq, w), lambda i: (i, 0)),
                  pl.BlockSpec((6, HEAD_DIM), lambda i: (0, 0)), tab, tab, tab],
        out_specs=[pl.BlockSpec((tq, B_IN), lambda i: (i, 0)), pl.BlockSpec((6, HEAD_DIM), lambda i: (0, 0))],
        out_shape=[jax.ShapeDtypeStruct((s, B_IN), CDT), jax.ShapeDtypeStruct((6, HEAD_DIM), F32)],
        compiler_params=_cparams("arbitrary"),
    )(proj, *dqkv, dz, gains, *tabs)


def _inv_count(t, window):
    return 1.0 / jnp.minimum(t + 1, window).astype(F32)


POOL_BLOCK = 128


def _band(window, forward):
    r = lax.broadcasted_iota(jnp.int32, (POOL_BLOCK, 2 * POOL_BLOCK), 0)
    j = lax.broadcasted_iota(jnp.int32, (POOL_BLOCK, 2 * POOL_BLOCK), 1)
    if forward:
        return jnp.where((j >= r) & (j < r + window), 1.0, 0.0).astype(CDT)
    return jnp.where((j <= r + POOL_BLOCK) & (j > r + POOL_BLOCK - window), 1.0, 0.0).astype(CDT)


def _split_dot(band, v):
    hi = v.astype(jnp.bfloat16)
    lo = (v - hi.astype(F32)).astype(jnp.bfloat16)
    band = band.astype(jnp.bfloat16)
    return _dot(band, hi) + _dot(band, lo)


def _pool_diff(x_ref, halo_ref, diff_ref, i, tq):
    t = i * tq + lax.broadcasted_iota(jnp.int32, (tq, 1), 0)
    for g, window in enumerate(POOL_SIZES):
        cols = slice(g * C_GROUP, (g + 1) * C_GROUP)
        band = _band(window, forward=False)
        inv = _inv_count(t, window)
        for b in range(tq // POOL_BLOCK):
            rows = slice(b * POOL_BLOCK, (b + 1) * POOL_BLOCK)
            cur = x_ref[rows, cols]
            if b == 0:
                above = jnp.where(i > 0, halo_ref[:, cols], jnp.zeros_like(cur))
            else:
                above = x_ref[(b - 1) * POOL_BLOCK:b * POOL_BLOCK, cols]
            pooled = _dot(band, jnp.concatenate([above, cur], axis=0)) * inv[rows]
            diff_ref[rows, cols] = (pooled - cur.astype(F32)).astype(diff_ref.dtype)


GRP_SHARD = (N_CHIPS, 2, 256, C_GROUP)
GRP_ROWS = C_GROUP // N_CHIPS


def _grp_rows(g):
    return g // 2, slice((g % 2) * GRP_ROWS, (g % 2 + 1) * GRP_ROWS)


def _grp_weight(w_ref, g):
    half, rows = _grp_rows(g)
    return jnp.concatenate([w_ref[k, half, rows, :] for k in range(N_CHIPS)], axis=0)


def _c_mid_fwd(proj, w_grp, scale, name, tq=512):
    s = proj.shape[0]
    per = tq // POOL_BLOCK

    def body(x_ref, halo_ref, z_ref, w_ref, sc_ref, y_ref, diff_ref):
        _pool_diff(x_ref, halo_ref, diff_ref, pl.program_id(0), tq)
        for g in range(len(POOL_SIZES)):
            cols = slice(g * C_GROUP, (g + 1) * C_GROUP)
            z = z_ref[:, cols].astype(F32)
            y_ref[:, cols] = (_dot(diff_ref[:, cols], _grp_weight(w_ref, g)) * sc_ref[:, cols] * (z * _sigmoid(z))).astype(y_ref.dtype)

    return pl.pallas_call(
        body, name=name, grid=(s // tq,),
        in_specs=[pl.BlockSpec((tq, C_WIDTH), lambda i: (i, 0)),
                  pl.BlockSpec((POOL_BLOCK, C_WIDTH), lambda i: (jnp.maximum(i * per - 1, 0), 0)),
                  pl.BlockSpec((tq, C_WIDTH), lambda i: (i, 1)),
                  pl.BlockSpec(GRP_SHARD, lambda i: (0, 0, 0, 0)), pl.BlockSpec((1, C_WIDTH), lambda i: (0, 0))],
        out_specs=pl.BlockSpec((tq, C_WIDTH), lambda i: (i, 0)),
        out_shape=jax.ShapeDtypeStruct((s, C_WIDTH), CDT),
        scratch_shapes=[pltpu.VMEM((tq, C_WIDTH), CDT)], compiler_params=_cparams("parallel"),
    )(proj, proj, proj, w_grp, scale)


def _c_mid_bwd(proj, w_grp, scale, dy, name, tq=512):
    s = proj.shape[0]
    per = tq // POOL_BLOCK

    def body(x_ref, halo_ref, z_ref, w_ref, sc_ref, dy_ref, dd_ref, dz_ref, dw_ref, dsc_ref, diff_ref):
        @pl.when(pl.program_id(0) == 0)
        def _():
            dw_ref[...] = jnp.zeros_like(dw_ref)
            dsc_ref[...] = jnp.zeros_like(dsc_ref)

        _pool_diff(x_ref, halo_ref, diff_ref, pl.program_id(0), tq)
        for g in range(len(POOL_SIZES)):
            cols = slice(g * C_GROUP, (g + 1) * C_GROUP)
            d = diff_ref[:, cols]
            wg = _grp_weight(w_ref, g)
            half, rows = _grp_rows(g)
            m0 = _dot(d, wg)
            z = z_ref[:, cols].astype(F32)
            dyv = dy_ref[:, cols].astype(F32)
            sc = sc_ref[:, cols]
            sg = _sigmoid(z)
            dmixed = dyv * (z * sg)
            dz_ref[:, cols] = (dyv * m0 * sc * (sg * (1.0 + z * (1.0 - sg)))).astype(dz_ref.dtype)
            dsc_ref[:, cols] += jnp.sum(dmixed * m0, axis=0, keepdims=True)
            dm0 = (dmixed * sc).astype(CDT)
            dwg = _dot(d, dm0, TN)
            for k in range(N_CHIPS):
                dw_ref[k, half, rows, :] += dwg[k * GRP_ROWS:(k + 1) * GRP_ROWS]
            dd_ref[:, cols] = _dot(dm0, wg, NT)

    blk = pl.BlockSpec((tq, C_WIDTH), lambda i: (i, 0))
    wsp = pl.BlockSpec(GRP_SHARD, lambda i: (0, 0, 0, 0))
    vec = pl.BlockSpec((1, C_WIDTH), lambda i: (0, 0))
    return pl.pallas_call(
        body, name=name, grid=(s // tq,),
        in_specs=[blk, pl.BlockSpec((POOL_BLOCK, C_WIDTH), lambda i: (jnp.maximum(i * per - 1, 0), 0)),
                  pl.BlockSpec((tq, C_WIDTH), lambda i: (i, 1)), wsp, vec, blk],
        out_specs=[blk, blk, wsp, vec],
        out_shape=[jax.ShapeDtypeStruct((s, C_WIDTH), F32), jax.ShapeDtypeStruct((s, C_WIDTH), CDT),
                   jax.ShapeDtypeStruct(GRP_SHARD, F32), jax.ShapeDtypeStruct((1, C_WIDTH), F32)],
        scratch_shapes=[pltpu.VMEM((tq, C_WIDTH), CDT)], compiler_params=_cparams("arbitrary"),
    )(proj, proj, proj, w_grp, scale, dy)


def _c_pool_bwd(ddiff, dz, name, tq=512):
    s = ddiff.shape[0]
    per = tq // POOL_BLOCK
    last = s // tq - 1

    def body(d_ref, halo_ref, dz_ref, o_ref):
        i = pl.program_id(0)
        t = i * tq + lax.broadcasted_iota(jnp.int32, (tq, 1), 0)
        for g, window in enumerate(POOL_SIZES):
            cols = slice(g * C_GROUP, (g + 1) * C_GROUP)
            band = _band(window, forward=True)
            inv = _inv_count(t, window)
            for b in range(tq // POOL_BLOCK):
                rows = slice(b * POOL_BLOCK, (b + 1) * POOL_BLOCK)
                cur = d_ref[rows, cols]
                if b == tq // POOL_BLOCK - 1:
                    below = jnp.where(i < last, halo_ref[:, cols] * (1.0 / window), 0.0)
                else:
                    nxt = slice((b + 1) * POOL_BLOCK, (b + 2) * POOL_BLOCK)
                    below = d_ref[nxt, cols] * inv[nxt]
                summed = _split_dot(band, jnp.concatenate([cur * inv[rows], below], axis=0))
                o_ref[rows, cols] = (summed - cur).astype(o_ref.dtype)
        o_ref[:, C_WIDTH:] = dz_ref[...]

    return pl.pallas_call(
        body, name=name, grid=(s // tq,),
        in_specs=[pl.BlockSpec((tq, C_WIDTH), lambda i: (i, 0)),
                  pl.BlockSpec((POOL_BLOCK, C_WIDTH), lambda i: (jnp.minimum((i + 1) * per, s // POOL_BLOCK - 1), 0)),
                  pl.BlockSpec((tq, C_WIDTH), lambda i: (i, 0))],
        out_specs=pl.BlockSpec((tq, 2 * C_WIDTH), lambda i: (i, 0)),
        out_shape=jax.ShapeDtypeStruct((s, 2 * C_WIDTH), CDT),
        compiler_params=_cparams("parallel"),
    )(ddiff, ddiff, dz)


LAYERS = (("a", 0), ("b", 0), ("c", 0), ("a", 1))


def _shard_spec(block, layer, where):
    if layer is None:
        return pl.BlockSpec((None,) + block, where)
    return pl.BlockSpec((None, None) + block, lambda i, j, q: (where(i, j, q)[0], layer) + where(i, j, q)[1:])


def _w_in_fwd(h, gw, layer, name):
    s, width = h.shape[0], gw.shape[-1]
    rows = 2048 if width <= 1024 else 1024
    return _mm(h, gw, "nn", CDT, name, (s, N_CHIPS * width, D_MODEL), (rows, width, D_MODEL),
               b_spec=_shard_spec((D_MODEL, width), layer, lambda i, j, q: (j, 0, 0)))


def _w_in_dh(dproj, gw, x, dxo, gain, name):
    s, width = dproj.shape[0], gw.shape[-1]
    tm = 1024 if width <= 1024 else 512

    def body(a_ref, b_ref, x_ref, dxo_ref, g_ref, dx_ref, dxc_ref, dg_ref, acc_ref):
        i, q = pl.program_id(0), pl.program_id(1)
        p = _dot(a_ref[...], b_ref[...], NT)

        @pl.when(q == 0)
        def _():
            acc_ref[...] = p

        @pl.when(q > 0)
        def _():
            acc_ref[...] += p

        @pl.when((i == 0) & (q == 0))
        def _():
            dg_ref[...] = jnp.zeros_like(dg_ref)

        @pl.when(q == N_CHIPS - 1)
        def _():
            dh = acc_ref[...]
            xv = x_ref[...]
            r = lax.rsqrt(jnp.mean(xv * xv, axis=-1, keepdims=True) + EPS)
            gd = dh * g_ref[...]
            dx = dxo_ref[...] + r * gd - xv * (r * r * r) * jnp.mean(gd * xv, axis=-1, keepdims=True)
            dx_ref[...] = dx
            dxc_ref[...] = dx.astype(dxc_ref.dtype)
            dg_ref[...] += jnp.sum(dh * xv * r, axis=0, keepdims=True)

    blk = pl.BlockSpec((tm, D_MODEL), lambda i, q: (i, 0))
    vec = pl.BlockSpec((1, D_MODEL), lambda i, q: (0, 0))
    return pl.pallas_call(
        body, name=name, grid=(s // tm, N_CHIPS),
        in_specs=[pl.BlockSpec((tm, width), lambda i, q: (i, q)), pl.BlockSpec((None, D_MODEL, width), lambda i, q: (q, 0, 0)),
                  blk, blk, vec],
        out_specs=[blk, blk, vec],
        out_shape=[jax.ShapeDtypeStruct((s, D_MODEL), F32), jax.ShapeDtypeStruct((s, D_MODEL), CDT), jax.ShapeDtypeStruct((1, D_MODEL), F32)],
        scratch_shapes=[pltpu.VMEM((tm, D_MODEL), F32)], compiler_params=_cparams("arbitrary", "arbitrary"),
    )(dproj, gw, x, dxo, gain)


def _w_in_grad(h, dproj, width, name):
    s = h.shape[0]
    tokens = 2048 if width <= 1536 else 1024
    return _mm(h, dproj, "tn", CDT, name, (D_MODEL, N_CHIPS * width, s), (D_MODEL, width, tokens),
               o_spec=pl.BlockSpec((None, D_MODEL, width), lambda i, j, q: (j, 0, 0)), o_shape=(N_CHIPS, D_MODEL, width))


def _w_out_spec(gw, layer):
    rows = gw.shape[-2]
    if layer is None:
        return pl.BlockSpec((N_CHIPS, rows, D_MODEL), lambda i: (0, 0, 0))
    return pl.BlockSpec((N_CHIPS, None, rows, D_MODEL), lambda i: (0, layer, 0, 0))


def _w_out_fwd(y, gw, layer, x, name, tm=1024):
    s, k = y.shape

    def body(y_ref, w_ref, x_ref, o_ref):
        o_ref[...] = x_ref[...] + _dot(y_ref[...], w_ref[...].reshape(k, D_MODEL))

    blk = pl.BlockSpec((tm, D_MODEL), lambda i: (i, 0))
    return pl.pallas_call(
        body, name=name, grid=(s // tm,), in_specs=[pl.BlockSpec((tm, k), lambda i: (i, 0)), _w_out_spec(gw, layer), blk],
        out_specs=blk, out_shape=jax.ShapeDtypeStruct((s, D_MODEL), F32), compiler_params=_cparams("parallel"),
    )(y, gw, x)


def _w_out_dy(dxc, gw, layer, name, tm=1024):
    s, k = dxc.shape[0], N_CHIPS * gw.shape[-2]
    tm = min(tm, s)

    def body(dx_ref, w_ref, o_ref):
        o_ref[...] = _dot(dx_ref[...], w_ref[...].reshape(k, D_MODEL), NT).astype(o_ref.dtype)

    return pl.pallas_call(
        body, name=name, grid=(s // tm,), in_specs=[pl.BlockSpec((tm, D_MODEL), lambda i: (i, 0)), _w_out_spec(gw, layer)],
        out_specs=pl.BlockSpec((tm, k), lambda i: (i, 0)), out_shape=jax.ShapeDtypeStruct((s, k), CDT),
        compiler_params=_cparams("parallel"),
    )(dxc, gw)


def _w_out_grad(y, dxc, rows, name):
    s, k = y.shape
    tokens = min(2048 if k <= 1024 else 1024, s)
    steps = s // tokens

    def body(y_ref, dx_ref, o_ref, acc_ref):
        i = pl.program_id(0)
        p = _dot(y_ref[...], dx_ref[...], TN)

        @pl.when(i == 0)
        def _():
            acc_ref[...] = p

        @pl.when(i > 0)
        def _():
            acc_ref[...] += p

        @pl.when(i == steps - 1)
        def _():
            o_ref[...] = acc_ref[...].reshape(N_CHIPS, rows, D_MODEL).astype(o_ref.dtype)

    return pl.pallas_call(
        body, name=name, grid=(steps,),
        in_specs=[pl.BlockSpec((tokens, k), lambda i: (i, 0)), pl.BlockSpec((tokens, D_MODEL), lambda i: (i, 0))],
        out_specs=pl.BlockSpec((N_CHIPS, rows, D_MODEL), lambda i: (0, 0, 0)),
        out_shape=jax.ShapeDtypeStruct((N_CHIPS, rows, D_MODEL), CDT),
        scratch_shapes=[pltpu.VMEM((k, D_MODEL), F32)], compiler_params=_cparams("arbitrary"),
    )(y, dxc)


def _local_step(x, target, w, on_grads=None, layer_weights=None):
    s = x.shape[0]
    tabs = _rope_tables(s)
    qk_gains = jnp.concatenate([w["b_q_gain"][0], w["b_k_gain"][0]], axis=0)
    saved = []
    for li, (kind, j) in enumerate(LAYERS):
        if layer_weights is not None:
            w = {**w, **layer_weights(li, x)}
        h = _rms_fwd(x, w["norm_gain"][li:li + 1], f"rms_fwd{li}")
        if kind == "a":
            proj = _w_in_fwd(h, w[f"a_w_in{j}"], None, f"a_in{li}")
            bs_t = w["a_b_s"][j].T
            y = _a_mid_fwd(proj, w["a_v_gain"][j:j + 1], w["a_w_s"][j], bs_t, f"a_mid_fwd{li}")
            x_next = _w_out_fwd(y, w[f"a_w_out{j}"], None, x, f"a_out{li}")
            saved.append((x, h, proj, y))
        elif kind == "b":
            proj = _w_in_fwd(h, w["b_w_in"], None, f"b_in{li}")
            qk = _b_qk_fwd(proj, qk_gains, tabs, f"b_qk_fwd{li}")
            qkv, os_, lses = [], [], []
            for g, dil in enumerate(B_DILATIONS):
                if dil == 1:
                    ops = (qk, qk, proj, (g, 3 + g, 6 + g))
                else:
                    ops = (_to_view(qk, g, 1024, dil, f"b_q_view{li}_{g}"), _to_view(qk, 3 + g, 1024, dil, f"b_k_view{li}_{g}"),
                           _to_view(proj, 6 + g, 1024, dil, f"b_v_view{li}_{g}"), (0, 0, 0))
                o, lse = _b_attn_fwd(*ops, dil, f"b_attn_fwd{li}_{g}")
                if dil > 1:
                    o, lse = _from_view(o, dil, f"b_o_nat{li}_{g}"), _from_view(lse, dil, f"b_lse_nat{li}_{g}")
                qkv.append(ops)
                os_.append(o)
                lses.append(lse)
            y, oj, lj = _b_combine(os_, lses, proj, f"b_combine{li}")
            x_next = _w_out_fwd(y, w["b_w_out"], None, x, f"b_out{li}")
            saved.append((x, h, proj, y, qkv, oj, lj))
        else:
            proj = _w_in_fwd(h, w["c_w_in"], None, f"c_in{li}")
            y = _c_mid_fwd(proj, w["c_w_grp"], w["c_scale"][j:j + 1], f"c_mid_fwd{li}")
            x_next = _w_out_fwd(y, w["c_w_out"], None, x, f"c_out{li}")
            saved.append((x, h, proj, y))
        x = x_next

    dx, dxc, sq = _loss_bwd(x, target, "loss_bwd")
    grads = {"norm_gain": [None] * len(LAYERS), "a_w_in": [None, None], "a_v_gain": [None, None], "a_w_s": [None, None],
             "a_b_s": [None, None], "a_w_out": [None, None]}
    for li in reversed(range(len(LAYERS))):
        kind, j = LAYERS[li]
        sv = saved[li]
        xin, h, proj, y = sv[:4]
        if kind == "a":
            grads["a_w_out"][j] = _w_out_grad(y, dxc, 512, f"a_dwout{li}")
            dy = _w_out_dy(dxc, w[f"a_w_out{j}"], None, f"a_dy{li}")
            dproj, dws, dbs_t, dvg = _a_mid_bwd(proj, dy, w["a_v_gain"][j:j + 1], w["a_w_s"][j], w["a_b_s"][j].T, f"a_mid_bwd{li}")
            grads["a_w_s"][j], grads["a_b_s"][j], grads["a_v_gain"][j] = dws, dbs_t.T, dvg[0]
            grads["a_w_in"][j] = _w_in_grad(h, dproj, 1536, f"a_dwin{li}")
            w_in = w[f"a_w_in{j}"]
        elif kind == "b":
            qkv, oj, lj = sv[4:]
            grads["b_w_out"] = _w_out_grad(y, dxc, 256, f"b_dwout{li}")
            dy = _w_out_dy(dxc, w["b_w_out"], None, f"b_dy{li}")
            do, dz, dd = _b_bwd_prep(dy, oj, proj, f"b_bwd_prep{li}")
            dqkv = []
            for g, dil in enumerate(B_DILATIONS):
                stats = (do, lj, dd)
                if dil > 1:
                    stats = (_to_view(do, 0, 1024, dil, f"b_do_view{li}_{g}"), _to_view(lj, 0, HEAD_DIM, dil, f"b_lj_view{li}_{g}"),
                             _to_view(dd, 0, HEAD_DIM, dil, f"b_dd_view{li}_{g}"))
                d = _b_attn_bwd(*qkv[g], *stats, dil, f"b_attn_bwd{li}_{g}")
                dqkv.append(_from_view(d, dil, f"b_dqkv_nat{li}_{g}") if dil > 1 else d)
            dproj, dgains = _b_qk_bwd(proj, dqkv, dz, qk_gains, tabs, f"b_qk_bwd{li}")
            grads["b_q_gain"], grads["b_k_gain"] = dgains[None, :3], dgains[None, 3:]
            grads["b_w_in"] = _w_in_grad(h, dproj, 2560, f"b_dwin{li}")
            w_in = w["b_w_in"]
        else:
            grads["c_w_out"] = _w_out_grad(y, dxc, 512, f"c_dwout{li}")
            dy = _w_out_dy(dxc, w["c_w_out"], None, f"c_dy{li}")
            ddiff, dz, dwg, dsc = _c_mid_bwd(proj, w["c_w_grp"], w["c_scale"][j:j + 1], dy, f"c_mid_bwd{li}")
            grads["c_w_grp"], grads["c_scale"] = dwg, dsc
            dproj = _c_pool_bwd(ddiff, dz, f"c_pool_bwd{li}")
            grads["c_w_in"] = _w_in_grad(h, dproj, 1024, f"c_dwin{li}")
            w_in = w["c_w_in"]
        gain = w["norm_gain"][li:li + 1]
        if on_grads is not None:
            gain = gain + on_grads(li, grads)
        dx, dxc, dng = _w_in_dh(dproj, w_in, xin, dx, gain, f"{kind}_dh{li}")
        grads["norm_gain"][li] = dng[0]
    for name in ("norm_gain", "a_v_gain", "a_w_s", "a_b_s"):
        grads[name] = jnp.stack(grads[name])
    return sq, dx, grads


ROW = 1024
REDUCE_UNITS = (("a_w_in", 0, 512, 1536), ("a_w_in", 1, 512, 1536), ("a_w_out", 0, 256, 1024), ("a_w_out", 1, 256, 1024),
                ("b_w_in", 0, 512, 2560), ("b_w_out", 0, 128, 1024), ("c_w_in", 0, 512, 1024), ("c_w_grp", 0, 256, 512),
                ("c_w_out", 0, 256, 1024))
GAIN_ROWS = 16
GAINS = len(REDUCE_UNITS)
LAYER_UNITS = {0: (0, 2), 1: (4, 5), 2: (6, 7, 8), 3: (1, 3)}
EARLY_UNITS = LAYER_UNITS
FIRST_UNITS = LAYER_UNITS[0] + (GAINS,)
LATER_UNITS = {li: LAYER_UNITS[li] for li in (1, 2, 3)}
SMALL_OF_HALF = ("a_v_gain", "c_scale")
SMALL_PAD = 16
REP = (("norm_gain", (4, 1024)), ("a_w_s", (2, 8, 128, 128)), ("a_b_s", (2, 8, 128)), ("b_q_gain", (1, 3, 128)), ("b_k_gain", (1, 3, 128)))
REP_CORE = 48
REP_PART = 2 * REP_CORE
SMALL_ROWS = SMALL_PAD + REP_CORE
BLOCK_SHAPES = {"a_w_in": (2, 1024, 1536), "a_v_gain": (2, 512), "a_w_out": (2, 512, 1024), "b_w_in": (1, 1024, 2560),
                "b_w_out": (1, 256, 1024), "c_w_in": (1, 1024, 1024), "c_w_grp": (1, 4, 128, 512), "c_scale": (1, 512),
                "c_w_out": (1, 512, 1024)}
WEIGHTS = ("norm_gain", "a_w_in", "a_v_gain", "a_w_s", "a_b_s", "a_w_out", "b_w_in", "b_q_gain", "b_k_gain", "b_w_out",
           "c_w_in", "c_w_grp", "c_scale", "c_w_out")


def _rows(a, rows):
    a = a.reshape(-1)
    return jnp.pad(a, (0, rows * ROW - a.shape[0])).reshape(rows, ROW)


def _small_unit(vecs, rep_part, dtype):
    halves = [jnp.concatenate([_rows(vecs[h].astype(dtype), SMALL_PAD), rep_part[h * REP_CORE:(h + 1) * REP_CORE].astype(dtype)])
              for h in range(2)]
    return jnp.stack(halves)


def _pack_rep(tree):
    return _rows(jnp.concatenate([tree[n].astype(F32).reshape(-1) for n, _ in REP]), N_CHIPS * REP_PART)


def _unpack_rep(slab):
    flat, out, off = slab.reshape(-1), {}, 0
    for n, shape in REP:
        size = math.prod(shape)
        out[n] = flat[off:off + size].reshape(shape)
        off += size
    return out


def _into_slot(x, first, rows, me, name):
    width = x.shape[1]
    tr = min(rows, 256)

    def body(me_ref, x_ref, o_ref):
        o_ref[...] = x_ref[...].astype(o_ref.dtype)

    return pl.pallas_call(
        body, name=name,
        grid_spec=pltpu.PrefetchScalarGridSpec(
            num_scalar_prefetch=1, grid=(rows // tr,), in_specs=[pl.BlockSpec((tr, width), lambda i, me_ref: (first // tr + i, 0))],
            out_specs=pl.BlockSpec((None, tr, width), lambda i, me_ref: (me_ref[0], i, 0))),
        out_shape=jax.ShapeDtypeStruct((N_CHIPS, rows, width), CDT), compiler_params=_cparams("parallel"),
    )(me.reshape(1), x)


def _gather_inputs(wts, me):
    units = []
    for u, (n, layer, r, w) in enumerate(REDUCE_UNITS):
        slot = _into_slot(wts[n].reshape(-1, w), layer * 2 * r, 2 * r, me, f"slot{u}")
        units.append(slot.reshape(N_CHIPS, 2, r, w))
    gains = jnp.concatenate([wts["a_v_gain"].reshape(-1), wts["c_scale"].reshape(-1)])
    gains = _rows(lax.bitcast_convert_type(gains, CDT), 2 * GAIN_ROWS)
    return units + [_into_slot(gains, 0, 2 * GAIN_ROWS, me, "slot_gains").reshape(N_CHIPS, 2, GAIN_ROWS, ROW)]


def _gathered_weights(indices, units):
    out = {}
    for u, arr in zip(indices, units):
        if u == GAINS:
            gains = lax.bitcast_convert_type(arr[:, 0, :3].reshape(N_CHIPS, 1536, 2), F32)
            out["a_v_gain"] = jnp.concatenate([gains[k, :1024].reshape(2, 512) for k in range(N_CHIPS)], axis=1)
            out["c_scale"] = jnp.concatenate([gains[k, 1024:].reshape(1, 512) for k in range(N_CHIPS)], axis=1)
            continue
        n, layer, r, w = REDUCE_UNITS[u]
        if n == "c_w_grp":
            out[n] = arr
        else:
            out[n + str(layer) if n in ("a_w_in", "a_w_out") else n] = arr.reshape(N_CHIPS, 2 * r, w)
    return out


def _place():
    x, y, c = lax.axis_index("x"), lax.axis_index("y"), lax.axis_index("c")
    chips = [(1 - x, y), (x, 1 - y), (1 - x, 1 - y)]
    return x, y, c, 2 * x + y, (x, y, 1 - c), chips


def _remote(src, dst, sems, j, to):
    send_sems, recv_sems = sems
    return pltpu.make_async_remote_copy(src_ref=src, dst_ref=dst, send_sem=send_sems.at[j], recv_sem=recv_sems.at[j],
                                        device_id=to, device_id_type=MESH)


def _comm_call_multi(body, name, out_shapes, n_sems, args, aliases=None):
    return pl.pallas_call(
        body, name=name, in_specs=[ANY] * len(args), out_specs=[ANY] * len(out_shapes), out_shape=out_shapes,
        scratch_shapes=[pltpu.SemaphoreType.DMA((n_sems,)), pltpu.SemaphoreType.DMA((n_sems,))],
        input_output_aliases=aliases or {},
    )(*args)


def _allgather_units(units):
    n = len(units)

    def body(*refs):
        outs, sems = refs[n:2 * n], (refs[2 * n], refs[2 * n + 1])
        x, y, c, me, sibling, chips = _place()
        first, passed = [], []
        for u, o_ref in enumerate(outs):
            for j, chip in enumerate(chips):
                first.append(_remote(o_ref.at[me, c], o_ref.at[me, c], sems, 6 * u + j, (*chip, c)))
                first[-1].start()
        for u, o_ref in enumerate(outs):
            for j, (cx, cy) in enumerate(chips):
                landed = o_ref.at[2 * cx + cy, c]
                _remote(landed, landed, sems, 6 * u + j, sibling).wait_recv()
                passed.append(_remote(landed, landed, sems, 6 * u + 3 + j, sibling))
                passed[-1].start()
        for u, o_ref in enumerate(outs):
            for j, (cx, cy) in enumerate(chips):
                landed = o_ref.at[2 * cx + cy, 1 - c]
                _remote(landed, landed, sems, 6 * u + 3 + j, sibling).wait_recv()
        for cp in first + passed:
            cp.wait_send()

    return _comm_call_multi(body, "allgather_first", [jax.ShapeDtypeStruct(u.shape, u.dtype) for u in units], 6 * n, units,
                            aliases={u: u for u in range(n)})


HBM = pl.BlockSpec(memory_space=pltpu.HBM)
SEM = pl.BlockSpec(memory_space=pltpu.SEMAPHORE)
SIDE_EFFECT = pltpu.SideEffectType.DATAFLOW_SIDE_EFFECTING


def _gather_start(groups, after):
    sizes = [len(g) for g in groups]
    units = [u for g in groups for u in g]
    n = len(units)

    def body(*refs):
        arrs, sems, token = refs[:n], refs[n + 1:n + 1 + 2 * len(groups)], refs[-1]
        x, y, c, me, sibling, chips = _place()
        at = 0
        for gi, size in enumerate(sizes):
            for u in range(size):
                mine = arrs[at + u].at[me, c]
                for j, chip in enumerate(chips):
                    _remote(mine, mine, (sems[2 * gi], sems[2 * gi + 1]), 3 * u + j, (*chip, c)).start()
            at += size
        token[...] = jnp.zeros_like(token)

    sem_shapes = [pltpu.SemaphoreType.DMA((3 * size,)) for size in sizes for _ in range(2)]
    outs = pl.pallas_call(
        body, name="gather_start",
        out_shape=(*sem_shapes, *[pltpu.HBM(u.shape, u.dtype) for u in units], jax.ShapeDtypeStruct((8, 128), F32)),
        in_specs=[HBM] * n + [ANY], out_specs=(*[SEM] * len(sem_shapes), *[HBM] * n, pl.BlockSpec(memory_space=pltpu.VMEM)),
        input_output_aliases={i: len(sem_shapes) + i for i in range(n)},
        compiler_params=pltpu.CompilerParams(has_side_effects=SIDE_EFFECT),
    )(*[pltpu.with_memory_space_constraint(u, pltpu.HBM) for u in units], after)
    sems, arrs = outs[:len(sem_shapes)], outs[len(sem_shapes):-1]
    bounds = [sum(sizes[:gi]) for gi in range(len(sizes) + 1)]
    return ([(sems[2 * gi], sems[2 * gi + 1]) for gi in range(len(sizes))],
            [list(arrs[bounds[gi]:bounds[gi + 1]]) for gi in range(len(sizes))], outs[-1])


def _gather_wait(units, sems, after, name):
    n = len(units)

    def body(*refs):
        arrs, send_sems, recv_sems = refs[:n], refs[n], refs[n + 1]
        x, y, c, me, sibling, chips = _place()
        for u in range(n):
            for j, (cx, cy) in enumerate(chips):
                cp = _remote(arrs[u].at[me, c], arrs[u].at[2 * cx + cy, c], (send_sems, recv_sems), 3 * u + j, (cx, cy, c))
                cp.wait_send()
                cp.wait_recv()

    outs = pl.pallas_call(
        body, name=name, out_shape=tuple(pltpu.HBM(u.shape, u.dtype) for u in units),
        in_specs=[HBM] * n + [SEM, SEM, ANY], out_specs=[HBM] * n, input_output_aliases={i: i for i in range(n)},
        compiler_params=pltpu.CompilerParams(has_side_effects=SIDE_EFFECT),
    )(*units, *sems, after)
    return list(outs)


def _forward_units(units, name):
    n = len(units)

    def body(*refs):
        outs, sems = refs[n:2 * n], (refs[2 * n], refs[2 * n + 1])
        x, y, c, me, sibling, chips = _place()
        passed = []
        for u, o_ref in enumerate(outs):
            for j, (cx, cy) in enumerate(chips):
                landed = o_ref.at[2 * cx + cy, c]
                passed.append(_remote(landed, landed, sems, 3 * u + j, sibling))
                passed[-1].start()
        for u, o_ref in enumerate(outs):
            for j, (cx, cy) in enumerate(chips):
                landed = o_ref.at[2 * cx + cy, 1 - c]
                _remote(landed, landed, sems, 3 * u + j, sibling).wait_recv()
        for cp in passed:
            cp.wait_send()

    return _comm_call_multi(body, name, [jax.ShapeDtypeStruct(u.shape, u.dtype) for u in units], 3 * n, units,
                            aliases={u: u for u in range(n)})


def _swap_halves(units, name):
    n = len(units)

    def body(*refs):
        ins, outs, sems = refs[:n], refs[n:2 * n], (refs[2 * n], refs[2 * n + 1])
        x, y, c, me, sibling, chips = _place()
        sent = [_remote(g_ref.at[:, 1 - c], o_ref, sems, u, sibling) for u, (g_ref, o_ref) in enumerate(zip(ins, outs))]
        for cp in sent:
            cp.start()
        for cp in sent:
            cp.wait()

    shapes = [jax.ShapeDtypeStruct((N_CHIPS,) + u.shape[2:], u.dtype) for u in units]
    return _comm_call_multi(body, name, shapes, n, units)


def _scatter_shards(units):
    n = len(units)

    def body(*refs):
        ins, outs, sems = refs[:n], refs[n:2 * n], (refs[2 * n], refs[2 * n + 1])
        x, y, c, me, sibling, chips = _place()
        sent = []
        for u, (s_ref, o_ref) in enumerate(zip(ins, outs)):
            for j, (cx, cy) in enumerate(chips):
                sent.append(_remote(s_ref.at[2 * cx + cy], o_ref.at[me], sems, 3 * u + j, (cx, cy, c)))
                sent[-1].start()
        for u, o_ref in enumerate(outs):
            for j, (cx, cy) in enumerate(chips):
                slot = o_ref.at[2 * cx + cy]
                _remote(slot, slot, sems, 3 * u + j, sibling).wait_recv()
        for cp in sent:
            cp.wait_send()

    return _comm_call_multi(body, "grad_scatter_shards", [jax.ShapeDtypeStruct(u.shape, u.dtype) for u in units], 3 * n, units)


def _scatter_start(units, name):
    n = len(units)

    def body(*refs):
        srcs, lands = refs[:n], refs[n:2 * n]
        send_sems, recv_sems, token = refs[2 * n], refs[2 * n + 1], refs[-1]
        x, y, c, me, sibling, chips = _place()
        for u in range(n):
            for j, (cx, cy) in enumerate(chips):
                _remote(srcs[u].at[2 * cx + cy], lands[u].at[me], (send_sems, recv_sems), 3 * u + j, (cx, cy, c)).start()
        token[...] = jnp.zeros_like(token)

    hbm = [pltpu.HBM(u.shape, u.dtype) for u in units]
    outs = pl.pallas_call(
        body, name=name,
        out_shape=(pltpu.SemaphoreType.DMA((3 * n,)), pltpu.SemaphoreType.DMA((3 * n,)), *hbm, *hbm, jax.ShapeDtypeStruct((8, 128), F32)),
        in_specs=[HBM] * (2 * n), out_specs=(SEM, SEM, *[HBM] * (2 * n), pl.BlockSpec(memory_space=pltpu.VMEM)),
        input_output_aliases={i: 2 + i for i in range(2 * n)},
        compiler_params=pltpu.CompilerParams(has_side_effects=SIDE_EFFECT),
    )(*[pltpu.with_memory_space_constraint(u, pltpu.HBM) for u in units],
      *[pltpu.with_memory_space_constraint(lax.empty(u.shape, u.dtype), pltpu.HBM) for u in units])
    return outs[0], outs[1], outs[2:2 + n], outs[2 + n:2 + 2 * n], outs[-1]


def _scatter_wait(send_sems, recv_sems, srcs, lands, after, name):
    n = len(srcs)

    def body(*refs):
        srcs_, lands_, send_sems_, recv_sems_ = refs[:n], refs[n:2 * n], refs[2 * n], refs[2 * n + 1]
        x, y, c, me, sibling, chips = _place()
        for u in range(n):
            for j, (cx, cy) in enumerate(chips):
                slot = lands_[u].at[2 * cx + cy]
                cp = _remote(srcs_[u].at[2 * cx + cy], slot, (send_sems_, recv_sems_), 3 * u + j, (cx, cy, c))
                cp.wait_send()
                cp.wait_recv()

    hbm = [pltpu.HBM(u.shape, u.dtype) for u in srcs]
    outs = pl.pallas_call(
        body, name=name, out_shape=(*hbm, *hbm),
        in_specs=[HBM] * (2 * n) + [SEM, SEM, ANY], out_specs=[HBM] * (2 * n),
        input_output_aliases={i: i for i in range(2 * n)},
        compiler_params=pltpu.CompilerParams(has_side_effects=SIDE_EFFECT),
    )(*srcs, *lands, send_sems, recv_sems, after)
    return list(outs[n:])


def _share_halves(units, name):
    n = len(units)

    def body(*refs):
        outs, sems = refs[n:2 * n], (refs[2 * n], refs[2 * n + 1])
        x, y, c, me, sibling, chips = _place()
        sent = [_remote(o_ref.at[:, :, c], o_ref.at[:, :, c], sems, u, sibling) for u, o_ref in enumerate(outs)]
        for cp in sent:
            cp.start()
        for u, o_ref in enumerate(outs):
            theirs = o_ref.at[:, :, 1 - c]
            _remote(theirs, theirs, sems, u, sibling).wait_recv()
        for cp in sent:
            cp.wait_send()

    return _comm_call_multi(body, name, [jax.ShapeDtypeStruct(u.shape, u.dtype) for u in units], n, units,
                            aliases={u: u for u in range(n)})


def _share_start(units, name):
    n = len(units)

    def body(*refs):
        arrs, send_sems, recv_sems, token = refs[:n], refs[n], refs[n + 1], refs[-1]
        x, y, c, me, sibling, chips = _place()
        for u in range(n):
            _remote(arrs[u].at[:, :, c], arrs[u].at[:, :, c], (send_sems, recv_sems), u, sibling).start()
        token[...] = jnp.zeros_like(token)

    hbm = [pltpu.HBM(u.shape, u.dtype) for u in units]
    outs = pl.pallas_call(
        body, name=name,
        out_shape=(pltpu.SemaphoreType.DMA((n,)), pltpu.SemaphoreType.DMA((n,)), *hbm, jax.ShapeDtypeStruct((8, 128), F32)),
        in_specs=[HBM] * n, out_specs=(SEM, SEM, *[HBM] * n, pl.BlockSpec(memory_space=pltpu.VMEM)),
        input_output_aliases={i: 2 + i for i in range(n)},
        compiler_params=pltpu.CompilerParams(has_side_effects=SIDE_EFFECT),
    )(*[pltpu.with_memory_space_constraint(u, pltpu.HBM) for u in units])
    return outs[0], outs[1], list(outs[2:2 + n]), outs[-1]


def _share_wait(send_sems, recv_sems, units, after, name):
    n = len(units)

    def body(*refs):
        arrs, send_sems_, recv_sems_ = refs[:n], refs[n], refs[n + 1]
        x, y, c, me, sibling, chips = _place()
        for u in range(n):
            cp = _remote(arrs[u].at[:, :, c], arrs[u].at[:, :, 1 - c], (send_sems_, recv_sems_), u, sibling)
            cp.wait_send()
            cp.wait_recv()

    outs = pl.pallas_call(
        body, name=name, out_shape=tuple(pltpu.HBM(u.shape, u.dtype) for u in units),
        in_specs=[HBM] * n + [SEM, SEM, ANY], out_specs=[HBM] * n, input_output_aliases={i: i for i in range(n)},
        compiler_params=pltpu.CompilerParams(has_side_effects=SIDE_EFFECT),
    )(*units, send_sems, recv_sems, after)
    return list(outs)


def _rep_start(rep4):
    def body(r_ref, land_ref, send_sems, recv_sems, r_thru, land_thru):
        x, y, c, me, sibling, chips = _place()
        for j, chip in enumerate(chips):
            _remote(r_ref, land_ref.at[me], (send_sems, recv_sems), j, (*chip, c)).start()

    land = jax.ShapeDtypeStruct((N_CHIPS,) + rep4.shape, rep4.dtype)
    return pl.pallas_call(
        body, name="rep_start",
        out_shape=(pltpu.SemaphoreType.DMA((3,)), pltpu.SemaphoreType.DMA((3,)), pltpu.HBM(rep4.shape, rep4.dtype),
                   pltpu.HBM(land.shape, land.dtype)),
        in_specs=[HBM, HBM], out_specs=(SEM, SEM, HBM, HBM), input_output_aliases={0: 2, 1: 3},
        compiler_params=pltpu.CompilerParams(has_side_effects=SIDE_EFFECT),
    )(pltpu.with_memory_space_constraint(rep4, pltpu.HBM),
      pltpu.with_memory_space_constraint(lax.empty(land.shape, land.dtype), pltpu.HBM))


def _rep_wait(send_sems, recv_sems, rep4, land, after):
    def body(r_ref, land_ref, send_sems_, recv_sems_, after_ref, r_dead, land_out):
        x, y, c, me, sibling, chips = _place()
        for j, (cx, cy) in enumerate(chips):
            cp = _remote(r_ref, land_ref.at[2 * cx + cy], (send_sems_, recv_sems_), j, (cx, cy, c))
            cp.wait_send()
            cp.wait_recv()

    return pl.pallas_call(
        body, name="rep_wait", out_shape=(pltpu.HBM(rep4.shape, rep4.dtype), pltpu.HBM(land.shape, land.dtype)),
        in_specs=[HBM, HBM, SEM, SEM, ANY], out_specs=(HBM, HBM), input_output_aliases={0: 0, 1: 1},
        compiler_params=pltpu.CompilerParams(has_side_effects=SIDE_EFFECT),
    )(rep4, land, send_sems, recv_sems, after)[1]


ADAM_TILE = 64


def _add_halves(mine, theirs, place, name):
    rows, width = theirs.shape[1:]
    tr = min(rows, 256)

    def body(place_ref, a_ref, b_ref, o_ref):
        o_ref[...] = (a_ref[...].astype(F32) + b_ref[...].astype(F32)).astype(o_ref.dtype)

    blk = pl.BlockSpec((None, tr, width), lambda k, i, place_ref: (k, i, 0))
    return pl.pallas_call(
        body, name=name,
        grid_spec=pltpu.PrefetchScalarGridSpec(
            num_scalar_prefetch=1, grid=(N_CHIPS, rows // tr),
            in_specs=[pl.BlockSpec((None, None, tr, width), lambda k, i, place_ref: (k, place_ref[1], i, 0)), blk], out_specs=blk),
        out_shape=jax.ShapeDtypeStruct(theirs.shape, theirs.dtype), compiler_params=_cparams("parallel", "parallel"),
    )(place, mine, theirs)


def _adam_update(g, w_ref, m_ref, v_ref, o_ref):
    m2 = ADAM_B1 * m_ref[...] + (1.0 - ADAM_B1) * g
    v2 = ADAM_B2 * v_ref[...] + (1.0 - ADAM_B2) * jnp.square(g)
    m_hat = m2 / (1.0 - ADAM_B1 ** ADAM_STEP)
    v_hat = v2 / (1.0 - ADAM_B2 ** ADAM_STEP)
    o_ref[0] = g
    o_ref[1] = -ADAM_LR * (m_hat / (jnp.sqrt(v_hat) + ADAM_EPS) + ADAM_WD * w_ref[...])
    o_ref[2] = m2
    o_ref[3] = v2


def _adamw_rep(g, w, m, v):
    rows = g.shape[0]

    def body(g_ref, w_ref, m_ref, v_ref, o_ref):
        _adam_update(g_ref[...], w_ref, m_ref, v_ref, o_ref)

    blk = pl.BlockSpec((ADAM_TILE, ROW), lambda i: (i, 0))
    return pl.pallas_call(
        body, name="adamw_rep", grid=(rows // ADAM_TILE,), in_specs=[blk, blk, blk, blk],
        out_specs=pl.BlockSpec((4, ADAM_TILE, ROW), lambda i: (0, i, 0)),
        out_shape=jax.ShapeDtypeStruct((4, rows, ROW), F32), compiler_params=_cparams("parallel"),
    )(g, w, m, v)


def _adamw(place, parts, own, w, m, v, layer, name, into=None):
    rows, width = parts.shape[1:]
    layers = w.shape[0]

    def body(place_ref, p_ref, own_ref, w_ref, m_ref, v_ref, *rest):
        o_ref = rest[-1]
        g = jnp.zeros((ADAM_TILE, width), F32)
        for k in range(N_CHIPS):
            g = g + jnp.where(place_ref[0] == k, own_ref[...], p_ref[k]).astype(F32)
        _adam_update(g, w_ref, m_ref, v_ref, o_ref)

    blk = pl.BlockSpec((None, None, ADAM_TILE, width), lambda i, place_ref: (layer, place_ref[1], i, 0))
    blk4 = pl.BlockSpec((4, ADAM_TILE, width), lambda i, place_ref: (0, i, 0))
    mine = pl.BlockSpec((None, ADAM_TILE, width), lambda i, place_ref: (place_ref[0], i, 0))
    out = pl.BlockSpec((4, None, None, ADAM_TILE, width), lambda i, place_ref: (0, layer, place_ref[1], i, 0))
    extra = [] if into is None else [into]
    return pl.pallas_call(
        body, name=name,
        grid_spec=pltpu.PrefetchScalarGridSpec(num_scalar_prefetch=1, grid=(rows // ADAM_TILE,),
                                               in_specs=[blk4, mine, blk, blk, blk] + [ANY] * len(extra), out_specs=out),
        out_shape=jax.ShapeDtypeStruct((4, layers, 2, rows, width), F32),
        input_output_aliases={} if into is None else {6: 0}, compiler_params=_cparams("parallel"),
    )(place, parts, own, w, m, v, *extra)


def kernel(x, norm_gain, a_w_in, a_v_gain, a_w_s, a_b_s, a_w_out, b_w_in, b_q_gain, b_k_gain, b_w_out, c_w_in, c_w_grp, c_scale, c_w_out, loss_target, m_norm_gain, m_a_w_in, m_a_v_gain, m_a_w_s, m_a_b_s, m_a_w_out, m_b_w_in, m_b_q_gain, m_b_k_gain, m_b_w_out, m_c_w_in, m_c_w_grp, m_c_scale, m_c_w_out, v_norm_gain, v_a_w_in, v_a_v_gain, v_a_w_s, v_a_b_s, v_a_w_out, v_b_w_in, v_b_q_gain, v_b_k_gain, v_b_w_out, v_c_w_in, v_c_w_grp, v_c_scale, v_c_w_out):
    wts = dict(norm_gain=norm_gain, a_w_in=a_w_in, a_v_gain=a_v_gain, a_w_s=a_w_s, a_b_s=a_b_s, a_w_out=a_w_out, b_w_in=b_w_in,
               b_q_gain=b_q_gain, b_k_gain=b_k_gain, b_w_out=b_w_out, c_w_in=c_w_in, c_w_grp=c_w_grp, c_scale=c_scale, c_w_out=c_w_out)
    mom1 = dict(norm_gain=m_norm_gain, a_w_in=m_a_w_in, a_v_gain=m_a_v_gain, a_w_s=m_a_w_s, a_b_s=m_a_b_s, a_w_out=m_a_w_out,
                b_w_in=m_b_w_in, b_q_gain=m_b_q_gain, b_k_gain=m_b_k_gain, b_w_out=m_b_w_out, c_w_in=m_c_w_in, c_w_grp=m_c_w_grp,
                c_scale=m_c_scale, c_w_out=m_c_w_out)
    mom2 = dict(norm_gain=v_norm_gain, a_w_in=v_a_w_in, a_v_gain=v_a_v_gain, a_w_s=v_a_w_s, a_b_s=v_a_b_s, a_w_out=v_a_w_out,
                b_w_in=v_b_w_in, b_q_gain=v_b_q_gain, b_k_gain=v_b_k_gain, b_w_out=v_b_w_out, c_w_in=v_c_w_in, c_w_grp=v_c_w_grp,
                c_scale=v_c_scale, c_w_out=v_c_w_out)
    axes = ("x", "y", "c")
    me = 2 * lax.axis_index("x") + lax.axis_index("y")
    core = lax.axis_index("c")

    place = jnp.stack([me, core]).astype(jnp.int32)

    slots = _gather_inputs(wts, me)
    first = _allgather_units([slots[u] for u in FIRST_UNITS])
    full = _gathered_weights(FIRST_UNITS, first)
    later = sorted(LATER_UNITS)
    gather_sems, gather_units, token = _gather_start([[slots[u] for u in LATER_UNITS[li]] for li in later], first[-1])
    for n, _ in REP:
        full[n] = wts[n]
    full["norm_gain"] = wts["norm_gain"] + token[0, 0]

    def layer_weights(li, x_in):
        if li not in LATER_UNITS:
            return {}
        g = later.index(li)
        landed = _gather_wait(gather_units[g], gather_sems[g], x_in, f"gather_wait_l{li}")
        return _gathered_weights(LATER_UNITS[li], _forward_units(landed, f"gather_forward_l{li}"))

    def unit_grad(u, grads):
        n, layer, r, w = REDUCE_UNITS[u]
        g = grads[n][layer] if n in ("a_w_in", "a_w_out") else grads[n]
        return g.astype(CDT).reshape(N_CHIPS, 2, r, w)

    chip_sums, in_flight = {}, []

    def start_exchange(li, grads):
        if li not in EARLY_UNITS:
            return jnp.zeros((), F32)
        units = EARLY_UNITS[li]
        mine = [unit_grad(u, grads) for u in units]
        theirs = _swap_halves(mine, f"grad_swap_halves_l{li}")
        sums = [_add_halves(g, t, place, f"grad_add_halves{u}") for u, g, t in zip(units, mine, theirs)]
        chip_sums.update(zip(units, sums))
        send_sems, recv_sems, srcs, lands, token = _scatter_start(sums, f"grad_scatter_start_l{li}")
        in_flight.append((li, units, send_sems, recv_sems, srcs, lands))
        return token[0, 0]

    sq, grad_x, grads = _local_step(x[0], loss_target[0], full, start_exchange, layer_weights)
    loss = lax.psum(0.5 * jnp.sum(sq) / D_MODEL, axes)

    rep_g = _pack_rep(grads)
    late = [u for u in range(len(REDUCE_UNITS)) if not any(u in us for us in EARLY_UNITS.values())]
    mine = [unit_grad(u, grads) for u in late]
    mine.append(jnp.stack([
        _small_unit([lax.slice_in_dim(grads[v], 512 * k, 512 * (k + 1), axis=1) for v in SMALL_OF_HALF],
                    rep_g[k * REP_PART:(k + 1) * REP_PART], CDT) for k in range(N_CHIPS)]))
    late.append(len(REDUCE_UNITS))
    theirs = _swap_halves(mine, "grad_swap_halves_late")
    sums = [_add_halves(g, t, place, f"grad_add_halves{u}") for u, g, t in zip(late, mine, theirs)]
    chip_sums.update(zip(late, sums))
    parts = dict(zip(late, _scatter_shards(sums)))
    for li, units, send_sems, recv_sems, srcs, lands in in_flight:
        parts.update(zip(units, _scatter_wait(send_sems, recv_sems, srcs, lands, grad_x, f"grad_scatter_wait_l{li}")))
    order = range(len(REDUCE_UNITS) + 1)
    parts, chip_sums = [parts[u] for u in order], [chip_sums[u] for u in order]

    def state_units(tree):
        units = [tree[n].reshape(BLOCK_SHAPES[n][0], 2, r, w) for n, _, r, w in REDUCE_UNITS]
        rep_part = lax.dynamic_slice_in_dim(_pack_rep(tree), me * REP_PART, REP_PART, axis=0)
        return units + [_small_unit([tree[v] for v in SMALL_OF_HALF], rep_part, F32)[None]]

    names = [n for n, _, _, _ in REDUCE_UNITS] + ["small"]
    layers = [layer for _, layer, _, _ in REDUCE_UNITS] + [0]
    res, state = {}, list(zip(parts, chip_sums, state_units(wts), state_units(mom1), state_units(mom2)))

    def update(units):
        for u in units:
            res[names[u]] = _adamw(place, *state[u], layers[u], f"adamw{u}", into=res.get(names[u]))

    update((len(REDUCE_UNITS),))
    small = _share_halves([res.pop("small")], "share_small")[0][:, 0]
    rep_mine = jnp.concatenate([small[0, 0, SMALL_PAD:], small[0, 1, SMALL_PAD:]], axis=0)
    rep_sems = _rep_start(rep_mine)
    update(LAYER_UNITS[1] + LAYER_UNITS[2])
    early = list(res)
    send_sems, recv_sems, early_arrs, _ = _share_start([res[n] for n in early], "share_start")
    update(LAYER_UNITS[0] + LAYER_UNITS[3])
    rest = [n for n in res if n not in early]
    res.update(zip(rest, _share_halves([res[n] for n in rest], "share_halves")))
    res.update(zip(early, _share_wait(send_sems, recv_sems, early_arrs, res[rest[0]], "share_wait")))
    rep_g = lax.dynamic_update_slice_in_dim(_rep_wait(*rep_sems, res[rest[0]]), rep_mine[None], me, axis=0)
    rep_all = _adamw_rep(rep_g.reshape(N_CHIPS * REP_PART, ROW), _pack_rep(wts), _pack_rep(mom1), _pack_rep(mom2))

    outs = []
    for q in range(4):
        tree = {n: arr[q].reshape(BLOCK_SHAPES[n]) for n, arr in res.items()}
        tree["a_v_gain"] = small[q, 0, 0].reshape(2, 512)
        tree["c_scale"] = small[q, 1, 0, :512].reshape(1, 512)
        tree.update(_unpack_rep(rep_all[q]))
        outs.append(tree)
    return (loss, grad_x[None], *[t[n] for t in outs for n in WEIGHTS])
```

```python
import functools
import math

import jax
import jax.numpy as jnp
from jax import lax
from jax.experimental import pallas as pl
from jax.experimental.pallas import tpu as pltpu

F32 = jnp.float32
CDT = jnp.bfloat16

D_MODEL = 1024
EPS = 1e-6
CHUNK = 128
A_WIDTH = 2048
A_GROUPS = 8
A_GROUP_DIM = 256
HEAD_DIM = 128
B_HEADS = 8
B_DILATIONS = (1, 4, 16)
B_QK = 6144
B_IN = 10240
ROPE_HALF = 16
ROPE_THETA = 500000.0
POOL_SIZES = (2, 4, 8, 16)
POOL_HALO = 16
C_WIDTH = 2048
C_GROUP = 512
N_CHIPS = 4

ADAM_LR = 0.001
ADAM_B1 = 0.9
ADAM_B2 = 0.999
ADAM_EPS = 1e-08
ADAM_WD = 0.01
ADAM_STEP = 10

VMEM_LIMIT = 48 * 1024 * 1024
ANY = pl.BlockSpec(memory_space=pl.ANY)
MESH = pl.DeviceIdType.MESH

NN = (((1,), (0,)), ((), ()))
NT = (((1,), (1,)), ((), ()))
TN = (((0,), (0,)), ((), ()))


def _cparams(*sem):
    return pltpu.CompilerParams(dimension_semantics=sem, vmem_limit_bytes=VMEM_LIMIT)


def _dot(a, b, dims=NN):
    return lax.dot_general(a, b, dims, preferred_element_type=F32)


def _sigmoid(z):
    return 1.0 / (1.0 + jnp.exp(-z))


def _lane_sums(v):
    ones = jnp.ones((HEAD_DIM, HEAD_DIM), jnp.bfloat16)
    hi = v.astype(jnp.bfloat16)
    lo = (v - hi.astype(F32)).astype(jnp.bfloat16)
    return _dot(hi, ones) + _dot(lo, ones)


def _mm(a, b, mode, out_dtype, name, mnk, tiles, b_spec=None, o_spec=None, o_shape=None, add=None):
    m, n, k = mnk
    tm, tn, tk = min(tiles[0], m), min(tiles[1], n), min(tiles[2], k)
    nk = k // tk
    a_spec = {"nn": pl.BlockSpec((tm, tk), lambda i, j, q: (i, q)),
              "nt": pl.BlockSpec((tm, tk), lambda i, j, q: (i, q)),
              "tn": pl.BlockSpec((tk, tm), lambda i, j, q: (q, i))}[mode]
    if b_spec is None:
        b_spec = {"nn": pl.BlockSpec((tk, tn), lambda i, j, q: (q, j)),
                  "nt": pl.BlockSpec((tn, tk), lambda i, j, q: (j, q)),
                  "tn": pl.BlockSpec((tk, tn), lambda i, j, q: (q, j))}[mode]
    if o_spec is None:
        o_spec, o_shape = pl.BlockSpec((tm, tn), lambda i, j, q: (i, j)), (m, n)
    dims = {"nn": NN, "nt": NT, "tn": TN}[mode]
    has_add = add is not None

    def body(*refs):
        a_ref, b_ref = refs[0], refs[1]
        o_ref = refs[3] if has_add else refs[2]
        p = _dot(a_ref[...], b_ref[...], dims)

        def finish(v):
            if has_add:
                v = v + refs[2][...]
            o_ref[...] = v.astype(o_ref.dtype)

        if nk == 1:
            finish(p)
        else:
            acc_ref = refs[-1]
            q = pl.program_id(2)

            @pl.when(q == 0)
            def _():
                acc_ref[...] = p

            @pl.when(q > 0)
            def _():
                acc_ref[...] += p

            @pl.when(q == nk - 1)
            def _():
                finish(acc_ref[...])

    in_specs = [a_spec, b_spec]
    args = [a, b]
    if has_add:
        in_specs.append(pl.BlockSpec((tm, tn), lambda i, j, q: (i, j)))
        args.append(add)
    return pl.pallas_call(
        body, name=name, grid=(m // tm, n // tn, nk), in_specs=in_specs, out_specs=o_spec,
        out_shape=jax.ShapeDtypeStruct(o_shape, out_dtype),
        scratch_shapes=[pltpu.VMEM((tm, tn), F32)] if nk > 1 else [],
        compiler_params=_cparams("parallel", "parallel", "arbitrary"),
    )(*args)


def _rms_fwd(x, g, name, tq=512):
    s, d = x.shape

    def body(x_ref, g_ref, h_ref):
        xv = x_ref[...]
        r = lax.rsqrt(jnp.mean(xv * xv, axis=-1, keepdims=True) + EPS)
        h_ref[...] = (xv * r * g_ref[...]).astype(h_ref.dtype)

    return pl.pallas_call(
        body, name=name, grid=(s // tq,),
        in_specs=[pl.BlockSpec((tq, d), lambda i: (i, 0)), pl.BlockSpec((1, d), lambda i: (0, 0))],
        out_specs=pl.BlockSpec((tq, d), lambda i: (i, 0)),
        out_shape=jax.ShapeDtypeStruct((s, d), CDT), compiler_params=_cparams("parallel"),
    )(x, g)


def _loss_bwd(y, target, name, tq=512):
    s, d = y.shape

    def body(y_ref, t_ref, dx_ref, dxc_ref, sq_ref):
        err = y_ref[...] - t_ref[...]
        dx = err * (1.0 / d)
        dx_ref[...] = dx
        dxc_ref[...] = dx.astype(dxc_ref.dtype)

        @pl.when(pl.program_id(0) == 0)
        def _():
            sq_ref[...] = jnp.zeros_like(sq_ref)

        sq_ref[...] += jnp.sum(err * err, axis=0, keepdims=True)

    blk = pl.BlockSpec((tq, d), lambda i: (i, 0))
    vec = pl.BlockSpec((1, d), lambda i: (0, 0))
    return pl.pallas_call(
        body, name=name, grid=(s // tq,), in_specs=[blk, blk], out_specs=[blk, blk, vec],
        out_shape=[jax.ShapeDtypeStruct((s, d), F32), jax.ShapeDtypeStruct((s, d), CDT), jax.ShapeDtypeStruct((1, d), F32)],
        compiler_params=_cparams("arbitrary"),
    )(y, target)


def _tril_mask():
    row = lax.broadcasted_iota(jnp.int32, (CHUNK, CHUNK), 0)
    col = lax.broadcasted_iota(jnp.int32, (CHUNK, CHUNK), 1)
    return row >= col


def _a_mid_fwd(proj, v_gain, w_s, b_s_t, name, tq=256):
    s = proj.shape[0]

    def body(p_ref, vg_ref, ws_ref, bs_ref, y_ref):
        vraw = p_ref[:, A_WIDTH:2 * A_WIDTH].astype(F32)
        r = lax.rsqrt(jnp.mean(vraw * vraw, axis=-1, keepdims=True) + EPS)
        vn = (vraw * r * vg_ref[...]).astype(CDT)
        tri = _tril_mask()
        for g in range(A_GROUPS):
            w = jnp.where(tri, ws_ref[g], 0.0).astype(CDT)
            bias = bs_ref[:, g:g + 1]
            cols = slice(g * A_GROUP_DIM, (g + 1) * A_GROUP_DIM)
            zcols = slice(2 * A_WIDTH + g * A_GROUP_DIM, 2 * A_WIDTH + (g + 1) * A_GROUP_DIM)
            for c in range(tq // CHUNK):
                rows = slice(c * CHUNK, (c + 1) * CHUNK)
                mixed = _dot(w, vn[rows, cols]) + bias
                u = p_ref[rows, cols].astype(F32)
                z = p_ref[rows, zcols].astype(F32)
                y_ref[rows, cols] = (u * mixed * (z * _sigmoid(z))).astype(y_ref.dtype)

    return pl.pallas_call(
        body, name=name, grid=(s // tq,),
        in_specs=[pl.BlockSpec((tq, 3 * A_WIDTH), lambda i: (i, 0)), pl.BlockSpec((1, A_WIDTH), lambda i: (0, 0)),
                  pl.BlockSpec((A_GROUPS, CHUNK, CHUNK), lambda i: (0, 0, 0)), pl.BlockSpec((CHUNK, A_GROUPS), lambda i: (0, 0))],
        out_specs=pl.BlockSpec((tq, A_WIDTH), lambda i: (i, 0)),
        out_shape=jax.ShapeDtypeStruct((s, A_WIDTH), CDT), compiler_params=_cparams("parallel"),
    )(proj, v_gain, w_s, b_s_t)


def _a_mid_bwd(proj, dy, v_gain, w_s, b_s_t, name, tq=256):
    s = proj.shape[0]

    def body(p_ref, dy_ref, vg_ref, ws_ref, bs_ref, dp_ref, dws_ref, dbs_ref, dvg_ref, dvn_ref):
        @pl.when(pl.program_id(0) == 0)
        def _():
            dws_ref[...] = jnp.zeros_like(dws_ref)
            dbs_ref[...] = jnp.zeros_like(dbs_ref)
            dvg_ref[...] = jnp.zeros_like(dvg_ref)

        vraw = p_ref[:, A_WIDTH:2 * A_WIDTH].astype(F32)
        r = lax.rsqrt(jnp.mean(vraw * vraw, axis=-1, keepdims=True) + EPS)
        vhat = vraw * r
        vg = vg_ref[...]
        vn = (vhat * vg).astype(CDT)
        tri = _tril_mask()
        lane = lax.broadcasted_iota(jnp.int32, (CHUNK, A_GROUPS), 1)
        dbs = jnp.zeros((CHUNK, A_GROUPS), F32)
        for g in range(A_GROUPS):
            w = jnp.where(tri, ws_ref[g], 0.0).astype(CDT)
            bias = bs_ref[:, g:g + 1]
            cols = slice(g * A_GROUP_DIM, (g + 1) * A_GROUP_DIM)
            zcols = slice(2 * A_WIDTH + g * A_GROUP_DIM, 2 * A_WIDTH + (g + 1) * A_GROUP_DIM)
            dws = jnp.zeros((CHUNK, CHUNK), F32)
            for c in range(tq // CHUNK):
                rows = slice(c * CHUNK, (c + 1) * CHUNK)
                vn_g = vn[rows, cols]
                mixed = _dot(w, vn_g) + bias
                u = p_ref[rows, cols].astype(F32)
                z = p_ref[rows, zcols].astype(F32)
                dyv = dy_ref[rows, cols].astype(F32)
                sg = _sigmoid(z)
                sz = z * sg
                dyu = dyv * u
                dmixed = dyu * sz
                dp_ref[rows, cols] = (dyv * mixed * sz).astype(dp_ref.dtype)
                dp_ref[rows, zcols] = (dyu * mixed * (sg * (1.0 + z * (1.0 - sg)))).astype(dp_ref.dtype)
                dmc = dmixed.astype(CDT)
                dws = dws + _dot(dmc, vn_g, NT)
                dbs = dbs + jnp.where(lane == g, jnp.sum(dmixed, axis=-1, keepdims=True), 0.0)
                dvn_ref[rows, cols] = _dot(w, dmc, TN)
            dws_ref[g] += jnp.where(tri, dws, 0.0)
        dbs_ref[...] += dbs
        dvn = dvn_ref[...]
        gd = dvn * vg
        dvraw = r * gd - vraw * (r * r * r) * jnp.mean(gd * vraw, axis=-1, keepdims=True)
        dp_ref[:, A_WIDTH:2 * A_WIDTH] = dvraw.astype(dp_ref.dtype)
        dvg_ref[...] += jnp.sum(dvn * vhat, axis=0, keepdims=True)

    return pl.pallas_call(
        body, name=name, grid=(s // tq,),
        in_specs=[pl.BlockSpec((tq, 3 * A_WIDTH), lambda i: (i, 0)), pl.BlockSpec((tq, A_WIDTH), lambda i: (i, 0)),
                  pl.BlockSpec((1, A_WIDTH), lambda i: (0, 0)), pl.BlockSpec((A_GROUPS, CHUNK, CHUNK), lambda i: (0, 0, 0)),
                  pl.BlockSpec((CHUNK, A_GROUPS), lambda i: (0, 0))],
        out_specs=[pl.BlockSpec((tq, 3 * A_WIDTH), lambda i: (i, 0)), pl.BlockSpec((A_GROUPS, CHUNK, CHUNK), lambda i: (0, 0, 0)),
                   pl.BlockSpec((CHUNK, A_GROUPS), lambda i: (0, 0)), pl.BlockSpec((1, A_WIDTH), lambda i: (0, 0))],
        out_shape=[jax.ShapeDtypeStruct((s, 3 * A_WIDTH), CDT), jax.ShapeDtypeStruct((A_GROUPS, CHUNK, CHUNK), F32),
                   jax.ShapeDtypeStruct((CHUNK, A_GROUPS), F32), jax.ShapeDtypeStruct((1, A_WIDTH), F32)],
        scratch_shapes=[pltpu.VMEM((tq, A_WIDTH), F32)],
        compiler_params=_cparams("arbitrary"),
    )(proj, dy, v_gain, w_s, b_s_t)


def _rope_tables(s):
    inv_freq = jnp.power(jnp.float32(ROPE_THETA), -jnp.arange(ROPE_HALF, dtype=F32) / ROPE_HALF)
    ang = jnp.arange(s, dtype=F32)[:, None] * inv_freq[None, :]
    cos, sin = jnp.cos(ang), jnp.sin(ang)
    rest = HEAD_DIM - 2 * ROPE_HALF
    t_c = jnp.concatenate([cos, cos, jnp.ones((s, rest), F32)], axis=1)
    t_a = jnp.concatenate([-sin, jnp.zeros((s, HEAD_DIM - ROPE_HALF), F32)], axis=1)
    t_b = jnp.concatenate([jnp.zeros((s, ROPE_HALF), F32), sin, jnp.zeros((s, rest), F32)], axis=1)
    return t_c, t_a, t_b


def _b_qk_fwd(proj, gains, tabs, name, tq=256):
    s = proj.shape[0]

    def body(p_ref, g_ref, tc_ref, ta_ref, tb_ref, o_ref):
        tc, ta, tb = tc_ref[...], ta_ref[...], tb_ref[...]
        for tg in range(6):
            gain = g_ref[tg:tg + 1, :]
            for h in range(B_HEADS):
                cols = slice(tg * 1024 + h * HEAD_DIM, tg * 1024 + (h + 1) * HEAD_DIM)
                xv = p_ref[:, cols].astype(F32)
                r = lax.rsqrt(_lane_sums(xv * xv) * (1.0 / HEAD_DIM) + EPS)
                xn = xv * r * gain
                y = xn * tc + pltpu.roll(xn, HEAD_DIM - ROPE_HALF, 1) * ta + pltpu.roll(xn, ROPE_HALF, 1) * tb
                o_ref[:, cols] = y.astype(o_ref.dtype)

    tab = pl.BlockSpec((tq, HEAD_DIM), lambda i: (i, 0))
    return pl.pallas_call(
        body, name=name, grid=(s // tq,),
        in_specs=[pl.BlockSpec((tq, B_QK), lambda i: (i, 0)), pl.BlockSpec((6, HEAD_DIM), lambda i: (0, 0)), tab, tab, tab],
        out_specs=pl.BlockSpec((tq, B_QK), lambda i: (i, 0)),
        out_shape=jax.ShapeDtypeStruct((s, B_QK), CDT), compiler_params=_cparams("parallel"),
    )(proj, gains, *tabs)


PERMUTE_BLOCK_BYTES = 4 * 1024 * 1024


def _view_rows(length, dil, width, itemsize):
    rows = 16
    while 2 * rows * dil * width * itemsize <= PERMUTE_BLOCK_BYTES and 2 * rows <= length:
        rows *= 2
    return rows


def _to_view(x, col, width, dil, name):
    s = x.shape[0]
    length = s // dil
    tl = _view_rows(length, dil, width, 4)
    lanes = HEAD_DIM
    nblk = width // lanes

    def body(x_ref, o_ref, slab_ref):
        for b in range(nblk):
            slab_ref[b] = x_ref[:, b * lanes:(b + 1) * lanes].astype(F32)
        for r in range(dil):
            for b in range(nblk):
                o_ref[:, r * width + b * lanes:r * width + (b + 1) * lanes] = (
                    slab_ref.at[b][pl.ds(r, tl, stride=dil), :].astype(o_ref.dtype))

    return pl.pallas_call(
        body, name=name, grid=(length // tl,),
        in_specs=[pl.BlockSpec((tl * dil, width), lambda i: (i, col))],
        out_specs=pl.BlockSpec((tl, dil * width), lambda i: (i, 0)),
        out_shape=jax.ShapeDtypeStruct((length, dil * width), x.dtype),
        scratch_shapes=[pltpu.VMEM((nblk, tl * dil, lanes), F32)],
        compiler_params=_cparams("parallel"),
    )(x)


def _from_view(v, dil, name):
    length, width = v.shape[0], v.shape[1] // dil
    tl = _view_rows(length, dil, width, 4)
    lanes = HEAD_DIM
    nblk = width // lanes

    def body(v_ref, o_ref, slab_ref):
        for r in range(dil):
            for b in range(nblk):
                slab_ref.at[b][pl.ds(r, tl, stride=dil), :] = v_ref[:, r * width + b * lanes:r * width + (b + 1) * lanes].astype(F32)
        for b in range(nblk):
            o_ref[:, b * lanes:(b + 1) * lanes] = slab_ref[b].astype(o_ref.dtype)

    return pl.pallas_call(
        body, name=name, grid=(length // tl,),
        in_specs=[pl.BlockSpec((tl, dil * width), lambda i: (i, 0))],
        out_specs=pl.BlockSpec((tl * dil, width), lambda i: (i, 0)),
        out_shape=jax.ShapeDtypeStruct((length * dil, width), v.dtype),
        scratch_shapes=[pltpu.VMEM((nblk, tl * dil, lanes), F32)],
        compiler_params=_cparams("parallel"),
    )(v)


def _b_attn_fwd(q, k, v, bases, dil, name):
    length = q.shape[0]
    nb = length // CHUNK
    per = 2 if nb % 2 == 0 else 1
    scale = 1.0 / math.sqrt(HEAD_DIM)
    w = B_HEADS * HEAD_DIM
    qb, kb, vb = bases

    def body(q_ref, kc_ref, kp_ref, vc_ref, vp_ref, o_ref, lse_ref):
        n = pl.program_id(1)
        qi = lax.broadcasted_iota(jnp.int32, (CHUNK, 2 * CHUNK), 0)
        ki = lax.broadcasted_iota(jnp.int32, (CHUNK, 2 * CHUNK), 1)
        first_key = jnp.where(n > 0, 0, CHUNK)
        band = (ki >= qi) & (ki <= qi + CHUNK)
        lane = lax.broadcasted_iota(jnp.int32, (CHUNK, HEAD_DIM), 1)
        for b in range(per):
            rows = slice(b * CHUNK, (b + 1) * CHUNK)
            above = slice((b - 1) * CHUNK, b * CHUNK)
            mask = band & (ki >= first_key) if b == 0 else band
            lse_all = jnp.zeros((CHUNK, HEAD_DIM), F32)
            for h in range(B_HEADS):
                sl = slice(h * HEAD_DIM, (h + 1) * HEAD_DIM)
                k_prev = kp_ref[:, sl] if b == 0 else kc_ref[above, sl]
                v_prev = vp_ref[:, sl] if b == 0 else vc_ref[above, sl]
                k2 = jnp.concatenate([k_prev, kc_ref[rows, sl]], axis=0)
                v2 = jnp.concatenate([v_prev, vc_ref[rows, sl]], axis=0)
                sc = jnp.where(mask, _dot(q_ref[rows, sl], k2, NT) * scale, -1e30)
                m = jnp.max(sc, axis=-1, keepdims=True)
                p = jnp.exp(sc - m)
                l = jnp.sum(p, axis=-1, keepdims=True)
                o_ref[rows, sl] = (_dot(p.astype(CDT), v2) / l).astype(o_ref.dtype)
                lse_all = jnp.where(lane == h, m + jnp.log(l), lse_all)
            lse_ref[rows, :] = lse_all

    prev = lambda n: jnp.maximum(per * n - 1, 0)
    blk = lambda f: pl.BlockSpec((per * CHUNK, w), f)
    halo = lambda f: pl.BlockSpec((CHUNK, w), f)
    return pl.pallas_call(
        body, name=name, grid=(dil, nb // per),
        in_specs=[blk(lambda r, n: (n, qb + r)), blk(lambda r, n: (n, kb + r)), halo(lambda r, n: (prev(n), kb + r)),
                  blk(lambda r, n: (n, vb + r)), halo(lambda r, n: (prev(n), vb + r))],
        out_specs=[blk(lambda r, n: (n, r)), pl.BlockSpec((per * CHUNK, HEAD_DIM), lambda r, n: (n, r))],
        out_shape=[jax.ShapeDtypeStruct((length, dil * w), CDT), jax.ShapeDtypeStruct((length, dil * HEAD_DIM), F32)],
        compiler_params=_cparams("parallel", "parallel"),
    )(q, k, k, v, v)


def _b_combine(os_, lses, proj, name, tq=512):
    s = proj.shape[0]
    w = B_HEADS * HEAD_DIM

    def body(o0_ref, o1_ref, o2_ref, l0_ref, l1_ref, l2_ref, z_ref, y_ref, oj_ref, lj_ref):
        l0, l1, l2 = l0_ref[...], l1_ref[...], l2_ref[...]
        m = jnp.maximum(jnp.maximum(l0, l1), l2)
        lj = m + jnp.log(jnp.exp(l0 - m) + jnp.exp(l1 - m) + jnp.exp(l2 - m))
        lj_ref[...] = lj
        w0, w1, w2 = jnp.exp(l0 - lj), jnp.exp(l1 - lj), jnp.exp(l2 - lj)
        for h in range(B_HEADS):
            sl = slice(h * HEAD_DIM, (h + 1) * HEAD_DIM)
            o = (w0[:, h:h + 1] * o0_ref[:, sl].astype(F32) + w1[:, h:h + 1] * o1_ref[:, sl].astype(F32)
                 + w2[:, h:h + 1] * o2_ref[:, sl].astype(F32))
            z = z_ref[:, sl].astype(F32)
            oj_ref[:, sl] = o.astype(oj_ref.dtype)
            y_ref[:, sl] = (o * (z * _sigmoid(z))).astype(y_ref.dtype)

    blk = pl.BlockSpec((tq, w), lambda i: (i, 0))
    st = pl.BlockSpec((tq, HEAD_DIM), lambda i: (i, 0))
    return pl.pallas_call(
        body, name=name, grid=(s // tq,),
        in_specs=[blk, blk, blk, st, st, st, pl.BlockSpec((tq, w), lambda i: (i, 9))],
        out_specs=[blk, blk, st],
        out_shape=[jax.ShapeDtypeStruct((s, w), CDT), jax.ShapeDtypeStruct((s, w), CDT), jax.ShapeDtypeStruct((s, HEAD_DIM), F32)],
        compiler_params=_cparams("parallel"),
    )(*os_, *lses, proj)


def _b_bwd_prep(dy, oj, proj, name, tq=512):
    s = proj.shape[0]
    w = B_HEADS * HEAD_DIM

    def body(dy_ref, oj_ref, z_ref, do_ref, dz_ref, dd_ref):
        lane = lax.broadcasted_iota(jnp.int32, (tq, HEAD_DIM), 1)
        dd = jnp.zeros((tq, HEAD_DIM), F32)
        for h in range(B_HEADS):
            sl = slice(h * HEAD_DIM, (h + 1) * HEAD_DIM)
            z = z_ref[:, sl].astype(F32)
            dyv = dy_ref[:, sl].astype(F32)
            o = oj_ref[:, sl].astype(F32)
            sg = _sigmoid(z)
            do = dyv * (z * sg)
            do_ref[:, sl] = do.astype(do_ref.dtype)
            dz_ref[:, sl] = (dyv * o * (sg * (1.0 + z * (1.0 - sg)))).astype(dz_ref.dtype)
            dd = jnp.where(lane == h, jnp.sum(do * o, axis=-1, keepdims=True), dd)
        dd_ref[...] = dd

    blk = pl.BlockSpec((tq, w), lambda i: (i, 0))
    st = pl.BlockSpec((tq, HEAD_DIM), lambda i: (i, 0))
    return pl.pallas_call(
        body, name=name, grid=(s // tq,),
        in_specs=[blk, blk, pl.BlockSpec((tq, w), lambda i: (i, 9))], out_specs=[blk, blk, st],
        out_shape=[jax.ShapeDtypeStruct((s, w), CDT), jax.ShapeDtypeStruct((s, w), CDT), jax.ShapeDtypeStruct((s, HEAD_DIM), F32)],
        compiler_params=_cparams("parallel"),
    )(dy, oj, proj)


def _b_attn_bwd(q, k, v, bases, do, lj, dd, dil, name):
    length = q.shape[0]
    nb = length // CHUNK
    per = 2 if nb % 2 == 0 else 1
    steps = nb // per
    scale = 1.0 / math.sqrt(HEAD_DIM)
    w = B_HEADS * HEAD_DIM
    qb, kb, vb = bases

    def body(qc_ref, qn_ref, k_ref, v_ref, doc_ref, don_ref, lc_ref, ln_ref, dc_ref, dn_ref, out_ref, carry_ref):
        j = pl.program_id(1)

        @pl.when(j == 0)
        def _():
            carry_ref[...] = jnp.zeros_like(carry_ref)

        qi = lax.broadcasted_iota(jnp.int32, (2 * CHUNK, CHUNK), 0)
        ki = lax.broadcasted_iota(jnp.int32, (2 * CHUNK, CHUNK), 1)
        no_next = jnp.where(j + 1 < steps, 0, 2 * CHUNK)
        for h in range(B_HEADS):
            sl = slice(h * HEAD_DIM, (h + 1) * HEAD_DIM)
            st_ = slice(h, h + 1)
            carried = carry_ref[:, sl]
            for b in range(per):
                rows = slice(b * CHUNK, (b + 1) * CHUNK)
                if b + 1 < per:
                    after = slice((b + 1) * CHUNK, (b + 2) * CHUNK)
                    q_n, do_n, l_n, d_n, gone = qc_ref[after, sl], doc_ref[after, sl], lc_ref[after, st_], dc_ref[after, st_], 0
                else:
                    q_n, do_n, l_n, d_n, gone = qn_ref[:, sl], don_ref[:, sl], ln_ref[:, st_], dn_ref[:, st_], no_next
                mask = ((qi < CHUNK) & (ki <= qi)) | ((qi >= CHUNK) & (ki >= qi - CHUNK + gone))
                q2 = jnp.concatenate([qc_ref[rows, sl], q_n], axis=0)
                do2 = jnp.concatenate([doc_ref[rows, sl], do_n], axis=0)
                lse2 = jnp.concatenate([lc_ref[rows, st_], l_n], axis=0)
                d2 = jnp.concatenate([dc_ref[rows, st_], d_n], axis=0)
                k = k_ref[rows, sl]
                v = v_ref[rows, sl]
                sc = _dot(q2, k, NT) * scale
                p = jnp.where(mask, jnp.exp(sc - lse2), 0.0)
                dp = _dot(do2, v, NT)
                ds = (p * (dp - d2) * scale).astype(CDT)
                dq2 = _dot(ds, k)
                out_ref[rows, sl] = (carried + dq2[:CHUNK]).astype(out_ref.dtype)
                carried = dq2[CHUNK:]
                out_ref[rows, w + h * HEAD_DIM:w + (h + 1) * HEAD_DIM] = _dot(ds, q2, TN).astype(out_ref.dtype)
                out_ref[rows, 2 * w + h * HEAD_DIM:2 * w + (h + 1) * HEAD_DIM] = _dot(p.astype(CDT), do2, TN).astype(out_ref.dtype)
            carry_ref[:, sl] = carried

    nxt = lambda j: jnp.minimum(per * (j + 1), nb - 1)
    blk = lambda f: pl.BlockSpec((per * CHUNK, w), f)
    one = lambda f: pl.BlockSpec((CHUNK, w), f)
    st = lambda f: pl.BlockSpec((per * CHUNK, HEAD_DIM), f)
    st1 = lambda f: pl.BlockSpec((CHUNK, HEAD_DIM), f)
    return pl.pallas_call(
        body, name=name, grid=(dil, steps),
        in_specs=[blk(lambda r, j: (j, qb + r)), one(lambda r, j: (nxt(j), qb + r)),
                  blk(lambda r, j: (j, kb + r)), blk(lambda r, j: (j, vb + r)),
                  blk(lambda r, j: (j, r)), one(lambda r, j: (nxt(j), r)),
                  st(lambda r, j: (j, r)), st1(lambda r, j: (nxt(j), r)), st(lambda r, j: (j, r)), st1(lambda r, j: (nxt(j), r))],
        out_specs=pl.BlockSpec((per * CHUNK, 3 * w), lambda r, j: (j, r)),
        out_shape=jax.ShapeDtypeStruct((length, dil * 3 * w), CDT),
        scratch_shapes=[pltpu.VMEM((CHUNK, w), F32)],
        compiler_params=_cparams("parallel", "arbitrary"),
    )(q, q, k, v, do, do, lj, lj, dd, dd)


def _b_qk_bwd(proj, dqkv, dz, gains, tabs, name, tq=256):
    s = proj.shape[0]
    w = B_HEADS * HEAD_DIM

    def body(p_ref, d0_ref, d1_ref, d2_ref, dz_ref, g_ref, tc_ref, ta_ref, tb_ref, dp_ref, dg_ref):
        @pl.when(pl.program_id(0) == 0)
        def _():
            dg_ref[...] = jnp.zeros_like(dg_ref)

        tc, ta, tb = tc_ref[...], ta_ref[...], tb_ref[...]
        d_refs = (d0_ref, d1_ref, d2_ref)
        for g in range(3):
            for t in range(2):
                tg = t * 3 + g
                gain = g_ref[tg:tg + 1, :]
                dgain = jnp.zeros((1, HEAD_DIM), F32)
                for h in range(B_HEADS):
                    cols = slice(tg * w + h * HEAD_DIM, tg * w + (h + 1) * HEAD_DIM)
                    xv = p_ref[:, cols].astype(F32)
                    r = lax.rsqrt(_lane_sums(xv * xv) * (1.0 / HEAD_DIM) + EPS)
                    dyv = d_refs[g][:, t * w + h * HEAD_DIM:t * w + (h + 1) * HEAD_DIM].astype(F32)
                    dxn = dyv * tc + pltpu.roll(dyv * ta, ROPE_HALF, 1) + pltpu.roll(dyv * tb, HEAD_DIM - ROPE_HALF, 1)
                    gd = dxn * gain
                    dx = r * gd - xv * (r * r * r) * (_lane_sums(gd * xv) * (1.0 / HEAD_DIM))
                    dp_ref[:, cols] = dx.astype(dp_ref.dtype)
                    dgain = dgain + jnp.sum(dxn * xv * r, axis=0, keepdims=True)
                dg_ref[tg:tg + 1, :] += dgain
            dp_ref[:, (6 + g) * w:(7 + g) * w] = d_refs[g][:, 2 * w:3 * w]
        dp_ref[:, 9 * w:10 * w] = dz_ref[...]

    tab = pl.BlockSpec((tq, HEAD_DIM), lambda i: (i, 0))
    dblk = pl.BlockSpec((tq, 3 * w), lambda i: (i, 0))
    return pl.pallas_call(
        body, name=name, grid=(s // tq,),
        in_specs=[pl.BlockSpec((tq, B_QK), lambda i: (i, 0)), dblk, dblk, dblk, pl.BlockSpec((tq, w), lambda i: (i, 0)),
                  pl.BlockSpec((6, HEAD_DIM), lambda i: (0, 0)), tab, tab, tab],
        out_specs=[pl.BlockSpec((tq, B_IN), lambda i: (i, 0)), pl.BlockSpec((6, HEAD_DIM), lambda i: (0, 0))],
        out_shape=[jax.ShapeDtypeStruct((s, B_IN), CDT), jax.ShapeDtypeStruct((6, HEAD_DIM), F32)],
        compiler_params=_cparams("arbitrary"),
    )(proj, *dqkv, dz, gains, *tabs)


def _inv_count(t, window):
    return 1.0 / jnp.minimum(t + 1, window).astype(F32)


POOL_BLOCK = 128


def _band(window, forward):
    r = lax.broadcasted_iota(jnp.int32, (POOL_BLOCK, 2 * POOL_BLOCK), 0)
    j = lax.broadcasted_iota(jnp.int32, (POOL_BLOCK, 2 * POOL_BLOCK), 1)
    if forward:
        return jnp.where((j >= r) & (j < r + window), 1.0, 0.0).astype(CDT)
    return jnp.where((j <= r + POOL_BLOCK) & (j > r + POOL_BLOCK - window), 1.0, 0.0).astype(CDT)


def _split_dot(band, v):
    hi = v.astype(jnp.bfloat16)
    lo = (v - hi.astype(F32)).astype(jnp.bfloat16)
    band = band.astype(jnp.bfloat16)
    return _dot(band, hi) + _dot(band, lo)


def _pool_diff(x_ref, halo_ref, diff_ref, i, tq):
    t = i * tq + lax.broadcasted_iota(jnp.int32, (tq, 1), 0)
    for g, window in enumerate(POOL_SIZES):
        cols = slice(g * C_GROUP, (g + 1) * C_GROUP)
        band = _band(window, forward=False)
        inv = _inv_count(t, window)
        for b in range(tq // POOL_BLOCK):
            rows = slice(b * POOL_BLOCK, (b + 1) * POOL_BLOCK)
            cur = x_ref[rows, cols]
            if b == 0:
                above = jnp.where(i > 0, halo_ref[:, cols], jnp.zeros_like(cur))
            else:
                above = x_ref[(b - 1) * POOL_BLOCK:b * POOL_BLOCK, cols]
            pooled = _dot(band, jnp.concatenate([above, cur], axis=0)) * inv[rows]
            diff_ref[rows, cols] = (pooled - cur.astype(F32)).astype(diff_ref.dtype)


GRP_SHARD = (N_CHIPS, 2, 256, C_GROUP)
GRP_ROWS = C_GROUP // N_CHIPS


def _grp_rows(g):
    return g // 2, slice((g % 2) * GRP_ROWS, (g % 2 + 1) * GRP_ROWS)


def _grp_weight(w_ref, g):
    half, rows = _grp_rows(g)
    return jnp.concatenate([w_ref[k, half, rows, :] for k in range(N_CHIPS)], axis=0)


def _c_mid_fwd(proj, w_grp, scale, name, tq=512):
    s = proj.shape[0]
    per = tq // POOL_BLOCK

    def body(x_ref, halo_ref, z_ref, w_ref, sc_ref, y_ref, diff_ref):
        _pool_diff(x_ref, halo_ref, diff_ref, pl.program_id(0), tq)
        for g in range(len(POOL_SIZES)):
            cols = slice(g * C_GROUP, (g + 1) * C_GROUP)
            z = z_ref[:, cols].astype(F32)
            y_ref[:, cols] = (_dot(diff_ref[:, cols], _grp_weight(w_ref, g)) * sc_ref[:, cols] * (z * _sigmoid(z))).astype(y_ref.dtype)

    return pl.pallas_call(
        body, name=name, grid=(s // tq,),
        in_specs=[pl.BlockSpec((tq, C_WIDTH), lambda i: (i, 0)),
                  pl.BlockSpec((POOL_BLOCK, C_WIDTH), lambda i: (jnp.maximum(i * per - 1, 0), 0)),
                  pl.BlockSpec((tq, C_WIDTH), lambda i: (i, 1)),
                  pl.BlockSpec(GRP_SHARD, lambda i: (0, 0, 0, 0)), pl.BlockSpec((1, C_WIDTH), lambda i: (0, 0))],
        out_specs=pl.BlockSpec((tq, C_WIDTH), lambda i: (i, 0)),
        out_shape=jax.ShapeDtypeStruct((s, C_WIDTH), CDT),
        scratch_shapes=[pltpu.VMEM((tq, C_WIDTH), CDT)], compiler_params=_cparams("parallel"),
    )(proj, proj, proj, w_grp, scale)


def _c_mid_bwd(proj, w_grp, scale, dy, name, tq=512):
    s = proj.shape[0]
    per = tq // POOL_BLOCK

    def body(x_ref, halo_ref, z_ref, w_ref, sc_ref, dy_ref, dd_ref, dz_ref, dw_ref, dsc_ref, diff_ref):
        @pl.when(pl.program_id(0) == 0)
        def _():
            dw_ref[...] = jnp.zeros_like(dw_ref)
            dsc_ref[...] = jnp.zeros_like(dsc_ref)

        _pool_diff(x_ref, halo_ref, diff_ref, pl.program_id(0), tq)
        for g in range(len(POOL_SIZES)):
            cols = slice(g * C_GROUP, (g + 1) * C_GROUP)
            d = diff_ref[:, cols]
            wg = _grp_weight(w_ref, g)
            half, rows = _grp_rows(g)
            m0 = _dot(d, wg)
            z = z_ref[:, cols].astype(F32)
            dyv = dy_ref[:, cols].astype(F32)
            sc = sc_ref[:, cols]
            sg = _sigmoid(z)
            dmixed = dyv * (z * sg)
            dz_ref[:, cols] = (dyv * m0 * sc * (sg * (1.0 + z * (1.0 - sg)))).astype(dz_ref.dtype)
            dsc_ref[:, cols] += jnp.sum(dmixed * m0, axis=0, keepdims=True)
            dm0 = (dmixed * sc).astype(CDT)
            dwg = _dot(d, dm0, TN)
            for k in range(N_CHIPS):
                dw_ref[k, half, rows, :] += dwg[k * GRP_ROWS:(k + 1) * GRP_ROWS]
            dd_ref[:, cols] = _dot(dm0, wg, NT)

    blk = pl.BlockSpec((tq, C_WIDTH), lambda i: (i, 0))
    wsp = pl.BlockSpec(GRP_SHARD, lambda i: (0, 0, 0, 0))
    vec = pl.BlockSpec((1, C_WIDTH), lambda i: (0, 0))
    return pl.pallas_call(
        body, name=name, grid=(s // tq,),
        in_specs=[blk, pl.BlockSpec((POOL_BLOCK, C_WIDTH), lambda i: (jnp.maximum(i * per - 1, 0), 0)),
                  pl.BlockSpec((tq, C_WIDTH), lambda i: (i, 1)), wsp, vec, blk],
        out_specs=[blk, blk, wsp, vec],
        out_shape=[jax.ShapeDtypeStruct((s, C_WIDTH), F32), jax.ShapeDtypeStruct((s, C_WIDTH), CDT),
                   jax.ShapeDtypeStruct(GRP_SHARD, F32), jax.ShapeDtypeStruct((1, C_WIDTH), F32)],
        scratch_shapes=[pltpu.VMEM((tq, C_WIDTH), CDT)], compiler_params=_cparams("arbitrary"),
    )(proj, proj, proj, w_grp, scale, dy)


def _c_pool_bwd(ddiff, dz, name, tq=512):
    s = ddiff.shape[0]
    per = tq // POOL_BLOCK
    last = s // tq - 1

    def body(d_ref, halo_ref, dz_ref, o_ref):
        i = pl.program_id(0)
        t = i * tq + lax.broadcasted_iota(jnp.int32, (tq, 1), 0)
        for g, window in enumerate(POOL_SIZES):
            cols = slice(g * C_GROUP, (g + 1) * C_GROUP)
            band = _band(window, forward=True)
            inv = _inv_count(t, window)
            for b in range(tq // POOL_BLOCK):
                rows = slice(b * POOL_BLOCK, (b + 1) * POOL_BLOCK)
                cur = d_ref[rows, cols]
                if b == tq // POOL_BLOCK - 1:
                    below = jnp.where(i < last, halo_ref[:, cols] * (1.0 / window), 0.0)
                else:
                    nxt = slice((b + 1) * POOL_BLOCK, (b + 2) * POOL_BLOCK)
                    below = d_ref[nxt, cols] * inv[nxt]
                summed = _split_dot(band, jnp.concatenate([cur * inv[rows], below], axis=0))
                o_ref[rows, cols] = (summed - cur).astype(o_ref.dtype)
        o_ref[:, C_WIDTH:] = dz_ref[...]

    return pl.pallas_call(
        body, name=name, grid=(s // tq,),
        in_specs=[pl.BlockSpec((tq, C_WIDTH), lambda i: (i, 0)),
                  pl.BlockSpec((POOL_BLOCK, C_WIDTH), lambda i: (jnp.minimum((i + 1) * per, s // POOL_BLOCK - 1), 0)),
                  pl.BlockSpec((tq, C_WIDTH), lambda i: (i, 0))],
        out_specs=pl.BlockSpec((tq, 2 * C_WIDTH), lambda i: (i, 0)),
        out_shape=jax.ShapeDtypeStruct((s, 2 * C_WIDTH), CDT),
        compiler_params=_cparams("parallel"),
    )(ddiff, ddiff, dz)


LAYERS = (("a", 0), ("b", 0), ("c", 0), ("a", 1))


def _shard_spec(block, layer, where):
    if layer is None:
        return pl.BlockSpec((None,) + block, where)
    return pl.BlockSpec((None, None) + block, lambda i, j, q: (where(i, j, q)[0], layer) + where(i, j, q)[1:])


def _w_in_fwd(h, gw, layer, name):
    s, width = h.shape[0], gw.shape[-1]
    rows = 2048 if width <= 1024 else 1024
    return _mm(h, gw, "nn", CDT, name, (s, N_CHIPS * width, D_MODEL), (rows, width, D_MODEL),
               b_spec=_shard_spec((D_MODEL, width), layer, lambda i, j, q: (j, 0, 0)))


def _w_in_dh(dproj, gw, x, dxo, gain, name):
    s, width = dproj.shape[0], gw.shape[-1]
    tm = 1024 if width <= 1024 else 512

    def body(a_ref, b_ref, x_ref, dxo_ref, g_ref, dx_ref, dxc_ref, dg_ref, acc_ref):
        i, q = pl.program_id(0), pl.program_id(1)
        p = _dot(a_ref[...], b_ref[...], NT)

        @pl.when(q == 0)
        def _():
            acc_ref[...] = p

        @pl.when(q > 0)
        def _():
            acc_ref[...] += p

        @pl.when((i == 0) & (q == 0))
        def _():
            dg_ref[...] = jnp.zeros_like(dg_ref)

        @pl.when(q == N_CHIPS - 1)
        def _():
            dh = acc_ref[...]
            xv = x_ref[...]
            r = lax.rsqrt(jnp.mean(xv * xv, axis=-1, keepdims=True) + EPS)
            gd = dh * g_ref[...]
            dx = dxo_ref[...] + r * gd - xv * (r * r * r) * jnp.mean(gd * xv, axis=-1, keepdims=True)
            dx_ref[...] = dx
            dxc_ref[...] = dx.astype(dxc_ref.dtype)
            dg_ref[...] += jnp.sum(dh * xv * r, axis=0, keepdims=True)

    blk = pl.BlockSpec((tm, D_MODEL), lambda i, q: (i, 0))
    vec = pl.BlockSpec((1, D_MODEL), lambda i, q: (0, 0))
    return pl.pallas_call(
        body, name=name, grid=(s // tm, N_CHIPS),
        in_specs=[pl.BlockSpec((tm, width), lambda i, q: (i, q)), pl.BlockSpec((None, D_MODEL, width), lambda i, q: (q, 0, 0)),
                  blk, blk, vec],
        out_specs=[blk, blk, vec],
        out_shape=[jax.ShapeDtypeStruct((s, D_MODEL), F32), jax.ShapeDtypeStruct((s, D_MODEL), CDT), jax.ShapeDtypeStruct((1, D_MODEL), F32)],
        scratch_shapes=[pltpu.VMEM((tm, D_MODEL), F32)], compiler_params=_cparams("arbitrary", "arbitrary"),
    )(dproj, gw, x, dxo, gain)


def _w_in_grad(h, dproj, width, name):
    s = h.shape[0]
    tokens = 2048 if width <= 1536 else 1024
    return _mm(h, dproj, "tn", CDT, name, (D_MODEL, N_CHIPS * width, s), (D_MODEL, width, tokens),
               o_spec=pl.BlockSpec((None, D_MODEL, width), lambda i, j, q: (j, 0, 0)), o_shape=(N_CHIPS, D_MODEL, width))


def _w_out_spec(gw, layer):
    rows = gw.shape[-2]
    if layer is None:
        return pl.BlockSpec((N_CHIPS, rows, D_MODEL), lambda i: (0, 0, 0))
    return pl.BlockSpec((N_CHIPS, None, rows, D_MODEL), lambda i: (0, layer, 0, 0))


def _w_out_fwd(y, gw, layer, x, name, tm=1024):
    s, k = y.shape

    def body(y_ref, w_ref, x_ref, o_ref):
        o_ref[...] = x_ref[...] + _dot(y_ref[...], w_ref[...].reshape(k, D_MODEL))

    blk = pl.BlockSpec((tm, D_MODEL), lambda i: (i, 0))
    return pl.pallas_call(
        body, name=name, grid=(s // tm,), in_specs=[pl.BlockSpec((tm, k), lambda i: (i, 0)), _w_out_spec(gw, layer), blk],
        out_specs=blk, out_shape=jax.ShapeDtypeStruct((s, D_MODEL), F32), compiler_params=_cparams("parallel"),
    )(y, gw, x)


def _w_out_dy(dxc, gw, layer, name, tm=1024):
    s, k = dxc.shape[0], N_CHIPS * gw.shape[-2]
    tm = min(tm, s)

    def body(dx_ref, w_ref, o_ref):
        o_ref[...] = _dot(dx_ref[...], w_ref[...].reshape(k, D_MODEL), NT).astype(o_ref.dtype)

    return pl.pallas_call(
        body, name=name, grid=(s // tm,), in_specs=[pl.BlockSpec((tm, D_MODEL), lambda i: (i, 0)), _w_out_spec(gw, layer)],
        out_specs=pl.BlockSpec((tm, k), lambda i: (i, 0)), out_shape=jax.ShapeDtypeStruct((s, k), CDT),
        compiler_params=_cparams("parallel"),
    )(dxc, gw)


def _w_out_grad(y, dxc, rows, name):
    s, k = y.shape
    tokens = min(2048 if k <= 1024 else 1024, s)
    steps = s // tokens

    def body(y_ref, dx_ref, o_ref, acc_ref):
        i = pl.program_id(0)
        p = _dot(y_ref[...], dx_ref[...], TN)

        @pl.when(i == 0)
        def _():
            acc_ref[...] = p

        @pl.when(i > 0)
        def _():
            acc_ref[...] += p

        @pl.when(i == steps - 1)
        def _():
            o_ref[...] = acc_ref[...].reshape(N_CHIPS, rows, D_MODEL).astype(o_ref.dtype)

    return pl.pallas_call(
        body, name=name, grid=(steps,),
        in_specs=[pl.BlockSpec((tokens, k), lambda i: (i, 0)), pl.BlockSpec((tokens, D_MODEL), lambda i: (i, 0))],
        out_specs=pl.BlockSpec((N_CHIPS, rows, D_MODEL), lambda i: (0, 0, 0)),
        out_shape=jax.ShapeDtypeStruct((N_CHIPS, rows, D_MODEL), CDT),
        scratch_shapes=[pltpu.VMEM((k, D_MODEL), F32)], compiler_params=_cparams("arbitrary"),
    )(y, dxc)


def _local_step(x, target, w, on_grads=None, layer_weights=None):
    s = x.shape[0]
    tabs = _rope_tables(s)
    qk_gains = jnp.concatenate([w["b_q_gain"][0], w["b_k_gain"][0]], axis=0)
    saved = []
    for li, (kind, j) in enumerate(LAYERS):
        if layer_weights is not None:
            w = {**w, **layer_weights(li, x)}
        h = _rms_fwd(x, w["norm_gain"][li:li + 1], f"rms_fwd{li}")
        if kind == "a":
            proj = _w_in_fwd(h, w[f"a_w_in{j}"], None, f"a_in{li}")
            bs_t = w["a_b_s"][j].T
            y = _a_mid_fwd(proj, w["a_v_gain"][j:j + 1], w["a_w_s"][j], bs_t, f"a_mid_fwd{li}")
            x_next = _w_out_fwd(y, w[f"a_w_out{j}"], None, x, f"a_out{li}")
            saved.append((x, h, proj, y))
        elif kind == "b":
            proj = _w_in_fwd(h, w["b_w_in"], None, f"b_in{li}")
            qk = _b_qk_fwd(proj, qk_gains, tabs, f"b_qk_fwd{li}")
            qkv, os_, lses = [], [], []
            for g, dil in enumerate(B_DILATIONS):
                if dil == 1:
                    ops = (qk, qk, proj, (g, 3 + g, 6 + g))
                else:
                    ops = (_to_view(qk, g, 1024, dil, f"b_q_view{li}_{g}"), _to_view(qk, 3 + g, 1024, dil, f"b_k_view{li}_{g}"),
                           _to_view(proj, 6 + g, 1024, dil, f"b_v_view{li}_{g}"), (0, 0, 0))
                o, lse = _b_attn_fwd(*ops, dil, f"b_attn_fwd{li}_{g}")
                if dil > 1:
                    o, lse = _from_view(o, dil, f"b_o_nat{li}_{g}"), _from_view(lse, dil, f"b_lse_nat{li}_{g}")
                qkv.append(ops)
                os_.append(o)
                lses.append(lse)
            y, oj, lj = _b_combine(os_, lses, proj, f"b_combine{li}")
            x_next = _w_out_fwd(y, w["b_w_out"], None, x, f"b_out{li}")
            saved.append((x, h, proj, y, qkv, oj, lj))
        else:
            proj = _w_in_fwd(h, w["c_w_in"], None, f"c_in{li}")
            y = _c_mid_fwd(proj, w["c_w_grp"], w["c_scale"][j:j + 1], f"c_mid_fwd{li}")
            x_next = _w_out_fwd(y, w["c_w_out"], None, x, f"c_out{li}")
            saved.append((x, h, proj, y))
        x = x_next

    dx, dxc, sq = _loss_bwd(x, target, "loss_bwd")
    grads = {"norm_gain": [None] * len(LAYERS), "a_w_in": [None, None], "a_v_gain": [None, None], "a_w_s": [None, None],
             "a_b_s": [None, None], "a_w_out": [None, None]}
    for li in reversed(range(len(LAYERS))):
        kind, j = LAYERS[li]
        sv = saved[li]
        xin, h, proj, y = sv[:4]
        if kind == "a":
            grads["a_w_out"][j] = _w_out_grad(y, dxc, 512, f"a_dwout{li}")
            dy = _w_out_dy(dxc, w[f"a_w_out{j}"], None, f"a_dy{li}")
            dproj, dws, dbs_t, dvg = _a_mid_bwd(proj, dy, w["a_v_gain"][j:j + 1], w["a_w_s"][j], w["a_b_s"][j].T, f"a_mid_bwd{li}")
            grads["a_w_s"][j], grads["a_b_s"][j], grads["a_v_gain"][j] = dws, dbs_t.T, dvg[0]
            grads["a_w_in"][j] = _w_in_grad(h, dproj, 1536, f"a_dwin{li}")
            w_in = w[f"a_w_in{j}"]
        elif kind == "b":
            qkv, oj, lj = sv[4:]
            grads["b_w_out"] = _w_out_grad(y, dxc, 256, f"b_dwout{li}")
            dy = _w_out_dy(dxc, w["b_w_out"], None, f"b_dy{li}")
            do, dz, dd = _b_bwd_prep(dy, oj, proj, f"b_bwd_prep{li}")
            dqkv = []
            for g, dil in enumerate(B_DILATIONS):
                stats = (do, lj, dd)
                if dil > 1:
                    stats = (_to_view(do, 0, 1024, dil, f"b_do_view{li}_{g}"), _to_view(lj, 0, HEAD_DIM, dil, f"b_lj_view{li}_{g}"),
                             _to_view(dd, 0, HEAD_DIM, dil, f"b_dd_view{li}_{g}"))
                d = _b_attn_bwd(*qkv[g], *stats, dil, f"b_attn_bwd{li}_{g}")
                dqkv.append(_from_view(d, dil, f"b_dqkv_nat{li}_{g}") if dil > 1 else d)
            dproj, dgains = _b_qk_bwd(proj, dqkv, dz, qk_gains, tabs, f"b_qk_bwd{li}")
            grads["b_q_gain"], grads["b_k_gain"] = dgains[None, :3], dgains[None, 3:]
            grads["b_w_in"] = _w_in_grad(h, dproj, 2560, f"b_dwin{li}")
            w_in = w["b_w_in"]
        else:
            grads["c_w_out"] = _w_out_grad(y, dxc, 512, f"c_dwout{li}")
            dy = _w_out_dy(dxc, w["c_w_out"], None, f"c_dy{li}")
            ddiff, dz, dwg, dsc = _c_mid_bwd(proj, w["c_w_grp"], w["c_scale"][j:j + 1], dy, f"c_mid_bwd{li}")
            grads["c_w_grp"], grads["c_scale"] = dwg, dsc
            dproj = _c_pool_bwd(ddiff, dz, f"c_pool_bwd{li}")
            grads["c_w_in"] = _w_in_grad(h, dproj, 1024, f"c_dwin{li}")
            w_in = w["c_w_in"]
        gain = w["norm_gain"][li:li + 1]
        if on_grads is not None:
            gain = gain + on_grads(li, grads)
        dx, dxc, dng = _w_in_dh(dproj, w_in, xin, dx, gain, f"{kind}_dh{li}")
        grads["norm_gain"][li] = dng[0]
    for name in ("norm_gain", "a_v_gain", "a_w_s", "a_b_s"):
        grads[name] = jnp.stack(grads[name])
    return sq, dx, grads


ROW = 1024
REDUCE_UNITS = (("a_w_in", 0, 512, 1536), ("a_w_in", 1, 512, 1536), ("a_w_out", 0, 256, 1024), ("a_w_out", 1, 256, 1024),
                ("b_w_in", 0, 512, 2560), ("b_w_out", 0, 128, 1024), ("c_w_in", 0, 512, 1024), ("c_w_grp", 0, 256, 512),
                ("c_w_out", 0, 256, 1024))
GAIN_ROWS = 16
GAINS = len(REDUCE_UNITS)
LAYER_UNITS = {0: (0, 2), 1: (4, 5), 2: (6, 7, 8), 3: (1, 3)}
EARLY_UNITS = LAYER_UNITS
FIRST_UNITS = LAYER_UNITS[0] + (GAINS,)
LATER_UNITS = {li: LAYER_UNITS[li] for li in (1, 2, 3)}
SMALL_OF_HALF = ("a_v_gain", "c_scale")
SMALL_PAD = 16
REP = (("norm_gain", (4, 1024)), ("a_w_s", (2, 8, 128, 128)), ("a_b_s", (2, 8, 128)), ("b_q_gain", (1, 3, 128)), ("b_k_gain", (1, 3, 128)))
REP_CORE = 48
REP_PART = 2 * REP_CORE
SMALL_ROWS = SMALL_PAD + REP_CORE
BLOCK_SHAPES = {"a_w_in": (2, 1024, 1536), "a_v_gain": (2, 512), "a_w_out": (2, 512, 1024), "b_w_in": (1, 1024, 2560),
                "b_w_out": (1, 256, 1024), "c_w_in": (1, 1024, 1024), "c_w_grp": (1, 4, 128, 512), "c_scale": (1, 512),
                "c_w_out": (1, 512, 1024)}
WEIGHTS = ("norm_gain", "a_w_in", "a_v_gain", "a_w_s", "a_b_s", "a_w_out", "b_w_in", "b_q_gain", "b_k_gain", "b_w_out",
           "c_w_in", "c_w_grp", "c_scale", "c_w_out")


def _rows(a, rows):
    a = a.reshape(-1)
    return jnp.pad(a, (0, rows * ROW - a.shape[0])).reshape(rows, ROW)


def _small_unit(vecs, rep_part, dtype):
    halves = [jnp.concatenate([_rows(vecs[h].astype(dtype), SMALL_PAD), rep_part[h * REP_CORE:(h + 1) * REP_CORE].astype(dtype)])
              for h in range(2)]
    return jnp.stack(halves)


def _pack_rep(tree):
    return _rows(jnp.concatenate([tree[n].astype(F32).reshape(-1) for n, _ in REP]), N_CHIPS * REP_PART)


def _unpack_rep(slab):
    flat, out, off = slab.reshape(-1), {}, 0
    for n, shape in REP:
        size = math.prod(shape)
        out[n] = flat[off:off + size].reshape(shape)
        off += size
    return out


def _into_slot(x, first, rows, me, name):
    width = x.shape[1]
    tr = min(rows, 256)

    def body(me_ref, x_ref, o_ref):
        o_ref[...] = x_ref[...].astype(o_ref.dtype)

    return pl.pallas_call(
        body, name=name,
        grid_spec=pltpu.PrefetchScalarGridSpec(
            num_scalar_prefetch=1, grid=(rows // tr,), in_specs=[pl.BlockSpec((tr, width), lambda i, me_ref: (first // tr + i, 0))],
            out_specs=pl.BlockSpec((None, tr, width), lambda i, me_ref: (me_ref[0], i, 0))),
        out_shape=jax.ShapeDtypeStruct((N_CHIPS, rows, width), CDT), compiler_params=_cparams("parallel"),
    )(me.reshape(1), x)


def _gather_inputs(wts, me):
    units = []
    for u, (n, layer, r, w) in enumerate(REDUCE_UNITS):
        slot = _into_slot(wts[n].reshape(-1, w), layer * 2 * r, 2 * r, me, f"slot{u}")
        units.append(slot.reshape(N_CHIPS, 2, r, w))
    gains = jnp.concatenate([wts["a_v_gain"].reshape(-1), wts["c_scale"].reshape(-1)])
    gains = _rows(lax.bitcast_convert_type(gains, CDT), 2 * GAIN_ROWS)
    return units + [_into_slot(gains, 0, 2 * GAIN_ROWS, me, "slot_gains").reshape(N_CHIPS, 2, GAIN_ROWS, ROW)]


def _gathered_weights(indices, units):
    out = {}
    for u, arr in zip(indices, units):
        if u == GAINS:
            gains = lax.bitcast_convert_type(arr[:, 0, :3].reshape(N_CHIPS, 1536, 2), F32)
            out["a_v_gain"] = jnp.concatenate([gains[k, :1024].reshape(2, 512) for k in range(N_CHIPS)], axis=1)
            out["c_scale"] = jnp.concatenate([gains[k, 1024:].reshape(1, 512) for k in range(N_CHIPS)], axis=1)
            continue
        n, layer, r, w = REDUCE_UNITS[u]
        if n == "c_w_grp":
            out[n] = arr
        else:
            out[n + str(layer) if n in ("a_w_in", "a_w_out") else n] = arr.reshape(N_CHIPS, 2 * r, w)
    return out


def _place():
    x, y, c = lax.axis_index("x"), lax.axis_index("y"), lax.axis_index("c")
    chips = [(1 - x, y), (x, 1 - y), (1 - x, 1 - y)]
    return x, y, c, 2 * x + y, (x, y, 1 - c), chips


def _remote(src, dst, sems, j, to):
    send_sems, recv_sems = sems
    return pltpu.make_async_remote_copy(src_ref=src, dst_ref=dst, send_sem=send_sems.at[j], recv_sem=recv_sems.at[j],
                                        device_id=to, device_id_type=MESH)


def _comm_call_multi(body, name, out_shapes, n_sems, args, aliases=None):
    return pl.pallas_call(
        body, name=name, in_specs=[ANY] * len(args), out_specs=[ANY] * len(out_shapes), out_shape=out_shapes,
        scratch_shapes=[pltpu.SemaphoreType.DMA((n_sems,)), pltpu.SemaphoreType.DMA((n_sems,))],
        input_output_aliases=aliases or {},
    )(*args)


def _allgather_units(units):
    n = len(units)

    def body(*refs):
        outs, sems = refs[n:2 * n], (refs[2 * n], refs[2 * n + 1])
        x, y, c, me, sibling, chips = _place()
        first, passed = [], []
        for u, o_ref in enumerate(outs):
            for j, chip in enumerate(chips):
                first.append(_remote(o_ref.at[me, c], o_ref.at[me, c], sems, 6 * u + j, (*chip, c)))
                first[-1].start()
        for u, o_ref in enumerate(outs):
            for j, (cx, cy) in enumerate(chips):
                landed = o_ref.at[2 * cx + cy, c]
                _remote(landed, landed, sems, 6 * u + j, sibling).wait_recv()
                passed.append(_remote(landed, landed, sems, 6 * u + 3 + j, sibling))
                passed[-1].start()
        for u, o_ref in enumerate(outs):
            for j, (cx, cy) in enumerate(chips):
                landed = o_ref.at[2 * cx + cy, 1 - c]
                _remote(landed, landed, sems, 6 * u + 3 + j, sibling).wait_recv()
        for cp in first + passed:
            cp.wait_send()

    return _comm_call_multi(body, "allgather_first", [jax.ShapeDtypeStruct(u.shape, u.dtype) for u in units], 6 * n, units,
                            aliases={u: u for u in range(n)})


HBM = pl.BlockSpec(memory_space=pltpu.HBM)
SEM = pl.BlockSpec(memory_space=pltpu.SEMAPHORE)
SIDE_EFFECT = pltpu.SideEffectType.DATAFLOW_SIDE_EFFECTING


def _gather_start(groups, after):
    sizes = [len(g) for g in groups]
    units = [u for g in groups for u in g]
    n = len(units)

    def body(*refs):
        arrs, sems, token = refs[:n], refs[n + 1:n + 1 + 2 * len(groups)], refs[-1]
        x, y, c, me, sibling, chips = _place()
        at = 0
        for gi, size in enumerate(sizes):
            for u in range(size):
                mine = arrs[at + u].at[me, c]
                for j, chip in enumerate(chips):
                    _remote(mine, mine, (sems[2 * gi], sems[2 * gi + 1]), 3 * u + j, (*chip, c)).start()
            at += size
        token[...] = jnp.zeros_like(token)

    sem_shapes = [pltpu.SemaphoreType.DMA((3 * size,)) for size in sizes for _ in range(2)]
    outs = pl.pallas_call(
        body, name="gather_start",
        out_shape=(*sem_shapes, *[pltpu.HBM(u.shape, u.dtype) for u in units], jax.ShapeDtypeStruct((8, 128), F32)),
        in_specs=[HBM] * n + [ANY], out_specs=(*[SEM] * len(sem_shapes), *[HBM] * n, pl.BlockSpec(memory_space=pltpu.VMEM)),
        input_output_aliases={i: len(sem_shapes) + i for i in range(n)},
        compiler_params=pltpu.CompilerParams(has_side_effects=SIDE_EFFECT),
    )(*[pltpu.with_memory_space_constraint(u, pltpu.HBM) for u in units], after)
    sems, arrs = outs[:len(sem_shapes)], outs[len(sem_shapes):-1]
    bounds = [sum(sizes[:gi]) for gi in range(len(sizes) + 1)]
    return ([(sems[2 * gi], sems[2 * gi + 1]) for gi in range(len(sizes))],
            [list(arrs[bounds[gi]:bounds[gi + 1]]) for gi in range(len(sizes))], outs[-1])


def _gather_wait(units, sems, after, name):
    n = len(units)

    def body(*refs):
        arrs, send_sems, recv_sems = refs[:n], refs[n], refs[n + 1]
        x, y, c, me, sibling, chips = _place()
        for u in range(n):
            for j, (cx, cy) in enumerate(chips):
                cp = _remote(arrs[u].at[me, c], arrs[u].at[2 * cx + cy, c], (send_sems, recv_sems), 3 * u + j, (cx, cy, c))
                cp.wait_send()
                cp.wait_recv()

    outs = pl.pallas_call(
        body, name=name, out_shape=tuple(pltpu.HBM(u.shape, u.dtype) for u in units),
        in_specs=[HBM] * n + [SEM, SEM, ANY], out_specs=[HBM] * n, input_output_aliases={i: i for i in range(n)},
        compiler_params=pltpu.CompilerParams(has_side_effects=SIDE_EFFECT),
    )(*units, *sems, after)
    return list(outs)


def _forward_units(units, name):
    n = len(units)

    def body(*refs):
        outs, sems = refs[n:2 * n], (refs[2 * n], refs[2 * n + 1])
        x, y, c, me, sibling, chips = _place()
        passed = []
        for u, o_ref in enumerate(outs):
            for j, (cx, cy) in enumerate(chips):
                landed = o_ref.at[2 * cx + cy, c]
                passed.append(_remote(landed, landed, sems, 3 * u + j, sibling))
                passed[-1].start()
        for u, o_ref in enumerate(outs):
            for j, (cx, cy) in enumerate(chips):
                landed = o_ref.at[2 * cx + cy, 1 - c]
                _remote(landed, landed, sems, 3 * u + j, sibling).wait_recv()
        for cp in passed:
            cp.wait_send()

    return _comm_call_multi(body, name, [jax.ShapeDtypeStruct(u.shape, u.dtype) for u in units], 3 * n, units,
                            aliases={u: u for u in range(n)})


def _swap_halves(units, name):
    n = len(units)

    def body(*refs):
        ins, outs, sems = refs[:n], refs[n:2 * n], (refs[2 * n], refs[2 * n + 1])
        x, y, c, me, sibling, chips = _place()
        sent = [_remote(g_ref.at[:, 1 - c], o_ref, sems, u, sibling) for u, (g_ref, o_ref) in enumerate(zip(ins, outs))]
        for cp in sent:
            cp.start()
        for cp in sent:
            cp.wait()

    shapes = [jax.ShapeDtypeStruct((N_CHIPS,) + u.shape[2:], u.dtype) for u in units]
    return _comm_call_multi(body, name, shapes, n, units)


def _scatter_shards(units):
    n = len(units)

    def body(*refs):
        ins, outs, sems = refs[:n], refs[n:2 * n], (refs[2 * n], refs[2 * n + 1])
        x, y, c, me, sibling, chips = _place()
        sent = []
        for u, (s_ref, o_ref) in enumerate(zip(ins, outs)):
            for j, (cx, cy) in enumerate(chips):
                sent.append(_remote(s_ref.at[2 * cx + cy], o_ref.at[me], sems, 3 * u + j, (cx, cy, c)))
                sent[-1].start()
        for u, o_ref in enumerate(outs):
            for j, (cx, cy) in enumerate(chips):
                slot = o_ref.at[2 * cx + cy]
                _remote(slot, slot, sems, 3 * u + j, sibling).wait_recv()
        for cp in sent:
            cp.wait_send()

    return _comm_call_multi(body, "grad_scatter_shards", [jax.ShapeDtypeStruct(u.shape, u.dtype) for u in units], 3 * n, units)


def _scatter_start(units, name):
    n = len(units)

    def body(*refs):
        srcs, lands = refs[:n], refs[n:2 * n]
        send_sems, recv_sems, token = refs[2 * n], refs[2 * n + 1], refs[-1]
        x, y, c, me, sibling, chips = _place()
        for u in range(n):
            for j, (cx, cy) in enumerate(chips):
                _remote(srcs[u].at[2 * cx + cy], lands[u].at[me], (send_sems, recv_sems), 3 * u + j, (cx, cy, c)).start()
        token[...] = jnp.zeros_like(token)

    hbm = [pltpu.HBM(u.shape, u.dtype) for u in units]
    outs = pl.pallas_call(
        body, name=name,
        out_shape=(pltpu.SemaphoreType.DMA((3 * n,)), pltpu.SemaphoreType.DMA((3 * n,)), *hbm, *hbm, jax.ShapeDtypeStruct((8, 128), F32)),
        in_specs=[HBM] * (2 * n), out_specs=(SEM, SEM, *[HBM] * (2 * n), pl.BlockSpec(memory_space=pltpu.VMEM)),
        input_output_aliases={i: 2 + i for i in range(2 * n)},
        compiler_params=pltpu.CompilerParams(has_side_effects=SIDE_EFFECT),
    )(*[pltpu.with_memory_space_constraint(u, pltpu.HBM) for u in units],
      *[pltpu.with_memory_space_constraint(lax.empty(u.shape, u.dtype), pltpu.HBM) for u in units])
    return outs[0], outs[1], outs[2:2 + n], outs[2 + n:2 + 2 * n], outs[-1]


def _scatter_wait(send_sems, recv_sems, srcs, lands, after, name):
    n = len(srcs)

    def body(*refs):
        srcs_, lands_, send_sems_, recv_sems_ = refs[:n], refs[n:2 * n], refs[2 * n], refs[2 * n + 1]
        x, y, c, me, sibling, chips = _place()
        for u in range(n):
            for j, (cx, cy) in enumerate(chips):
                slot = lands_[u].at[2 * cx + cy]
                cp = _remote(srcs_[u].at[2 * cx + cy], slot, (send_sems_, recv_sems_), 3 * u + j, (cx, cy, c))
                cp.wait_send()
                cp.wait_recv()

    hbm = [pltpu.HBM(u.shape, u.dtype) for u in srcs]
    outs = pl.pallas_call(
        body, name=name, out_shape=(*hbm, *hbm),
        in_specs=[HBM] * (2 * n) + [SEM, SEM, ANY], out_specs=[HBM] * (2 * n),
        input_output_aliases={i: i for i in range(2 * n)},
        compiler_params=pltpu.CompilerParams(has_side_effects=SIDE_EFFECT),
    )(*srcs, *lands, send_sems, recv_sems, after)
    return list(outs[n:])


def _share_halves(units, name):
    n = len(units)

    def body(*refs):
        outs, sems = refs[n:2 * n], (refs[2 * n], refs[2 * n + 1])
        x, y, c, me, sibling, chips = _place()
        sent = [_remote(o_ref.at[:, :, c], o_ref.at[:, :, c], sems, u, sibling) for u, o_ref in enumerate(outs)]
        for cp in sent:
            cp.start()
        for u, o_ref in enumerate(outs):
            theirs = o_ref.at[:, :, 1 - c]
            _remote(theirs, theirs, sems, u, sibling).wait_recv()
        for cp in sent:
            cp.wait_send()

    return _comm_call_multi(body, name, [jax.ShapeDtypeStruct(u.shape, u.dtype) for u in units], n, units,
                            aliases={u: u for u in range(n)})


def _share_start(units, name):
    n = len(units)

    def body(*refs):
        arrs, send_sems, recv_sems, token = refs[:n], refs[n], refs[n + 1], refs[-1]
        x, y, c, me, sibling, chips = _place()
        for u in range(n):
            _remote(arrs[u].at[:, :, c], arrs[u].at[:, :, c], (send_sems, recv_sems), u, sibling).start()
        token[...] = jnp.zeros_like(token)

    hbm = [pltpu.HBM(u.shape, u.dtype) for u in units]
    outs = pl.pallas_call(
        body, name=name,
        out_shape=(pltpu.SemaphoreType.DMA((n,)), pltpu.SemaphoreType.DMA((n,)), *hbm, jax.ShapeDtypeStruct((8, 128), F32)),
        in_specs=[HBM] * n, out_specs=(SEM, SEM, *[HBM] * n, pl.BlockSpec(memory_space=pltpu.VMEM)),
        input_output_aliases={i: 2 + i for i in range(n)},
        compiler_params=pltpu.CompilerParams(has_side_effects=SIDE_EFFECT),
    )(*[pltpu.with_memory_space_constraint(u, pltpu.HBM) for u in units])
    return outs[0], outs[1], list(outs[2:2 + n]), outs[-1]


def _share_wait(send_sems, recv_sems, units, after, name):
    n = len(units)

    def body(*refs):
        arrs, send_sems_, recv_sems_ = refs[:n], refs[n], refs[n + 1]
        x, y, c, me, sibling, chips = _place()
        for u in range(n):
            cp = _remote(arrs[u].at[:, :, c], arrs[u].at[:, :, 1 - c], (send_sems_, recv_sems_), u, sibling)
            cp.wait_send()
            cp.wait_recv()

    outs = pl.pallas_call(
        body, name=name, out_shape=tuple(pltpu.HBM(u.shape, u.dtype) for u in units),
        in_specs=[HBM] * n + [SEM, SEM, ANY], out_specs=[HBM] * n, input_output_aliases={i: i for i in range(n)},
        compiler_params=pltpu.CompilerParams(has_side_effects=SIDE_EFFECT),
    )(*units, send_sems, recv_sems, after)
    return list(outs)


def _rep_start(rep4):
    def body(r_ref, land_ref, send_sems, recv_sems, r_thru, land_thru):
        x, y, c, me, sibling, chips = _place()
        for j, chip in enumerate(chips):
            _remote(r_ref, land_ref.at[me], (send_sems, recv_sems), j, (*chip, c)).start()

    land = jax.ShapeDtypeStruct((N_CHIPS,) + rep4.shape, rep4.dtype)
    return pl.pallas_call(
        body, name="rep_start",
        out_shape=(pltpu.SemaphoreType.DMA((3,)), pltpu.SemaphoreType.DMA((3,)), pltpu.HBM(rep4.shape, rep4.dtype),
                   pltpu.HBM(land.shape, land.dtype)),
        in_specs=[HBM, HBM], out_specs=(SEM, SEM, HBM, HBM), input_output_aliases={0: 2, 1: 3},
        compiler_params=pltpu.CompilerParams(has_side_effects=SIDE_EFFECT),
    )(pltpu.with_memory_space_constraint(rep4, pltpu.HBM),
      pltpu.with_memory_space_constraint(lax.empty(land.shape, land.dtype), pltpu.HBM))


def _rep_wait(send_sems, recv_sems, rep4, land, after):
    def body(r_ref, land_ref, send_sems_, recv_sems_, after_ref, r_dead, land_out):
        x, y, c, me, sibling, chips = _place()
        for j, (cx, cy) in enumerate(chips):
            cp = _remote(r_ref, land_ref.at[2 * cx + cy], (send_sems_, recv_sems_), j, (cx, cy, c))
            cp.wait_send()
            cp.wait_recv()

    return pl.pallas_call(
        body, name="rep_wait", out_shape=(pltpu.HBM(rep4.shape, rep4.dtype), pltpu.HBM(land.shape, land.dtype)),
        in_specs=[HBM, HBM, SEM, SEM, ANY], out_specs=(HBM, HBM), input_output_aliases={0: 0, 1: 1},
        compiler_params=pltpu.CompilerParams(has_side_effects=SIDE_EFFECT),
    )(rep4, land, send_sems, recv_sems, after)[1]


ADAM_TILE = 64


def _add_halves(mine, theirs, place, name):
    rows, width = theirs.shape[1:]
    tr = min(rows, 256)

    def body(place_ref, a_ref, b_ref, o_ref):
        o_ref[...] = (a_ref[...].astype(F32) + b_ref[...].astype(F32)).astype(o_ref.dtype)

    blk = pl.BlockSpec((None, tr, width), lambda k, i, place_ref: (k, i, 0))
    return pl.pallas_call(
        body, name=name,
        grid_spec=pltpu.PrefetchScalarGridSpec(
            num_scalar_prefetch=1, grid=(N_CHIPS, rows // tr),
            in_specs=[pl.BlockSpec((None, None, tr, width), lambda k, i, place_ref: (k, place_ref[1], i, 0)), blk], out_specs=blk),
        out_shape=jax.ShapeDtypeStruct(theirs.shape, theirs.dtype), compiler_params=_cparams("parallel", "parallel"),
    )(place, mine, theirs)


def _adam_update(g, w_ref, m_ref, v_ref, o_ref):
    m2 = ADAM_B1 * m_ref[...] + (1.0 - ADAM_B1) * g
    v2 = ADAM_B2 * v_ref[...] + (1.0 - ADAM_B2) * jnp.square(g)
    m_hat = m2 / (1.0 - ADAM_B1 ** ADAM_STEP)
    v_hat = v2 / (1.0 - ADAM_B2 ** ADAM_STEP)
    o_ref[0] = g
    o_ref[1] = -ADAM_LR * (m_hat / (jnp.sqrt(v_hat) + ADAM_EPS) + ADAM_WD * w_ref[...])
    o_ref[2] = m2
    o_ref[3] = v2


def _adamw_rep(g, w, m, v):
    rows = g.shape[0]

    def body(g_ref, w_ref, m_ref, v_ref, o_ref):
        _adam_update(g_ref[...], w_ref, m_ref, v_ref, o_ref)

    blk = pl.BlockSpec((ADAM_TILE, ROW), lambda i: (i, 0))
    return pl.pallas_call(
        body, name="adamw_rep", grid=(rows // ADAM_TILE,), in_specs=[blk, blk, blk, blk],
        out_specs=pl.BlockSpec((4, ADAM_TILE, ROW), lambda i: (0, i, 0)),
        out_shape=jax.ShapeDtypeStruct((4, rows, ROW), F32), compiler_params=_cparams("parallel"),
    )(g, w, m, v)


def _adamw(place, parts, own, w, m, v, layer, name, into=None):
    rows, width = parts.shape[1:]
    layers = w.shape[0]

    def body(place_ref, p_ref, own_ref, w_ref, m_ref, v_ref, *rest):
        o_ref = rest[-1]
        g = jnp.zeros((ADAM_TILE, width), F32)
        for k in range(N_CHIPS):
            g = g + jnp.where(place_ref[0] == k, own_ref[...], p_ref[k]).astype(F32)
        _adam_update(g, w_ref, m_ref, v_ref, o_ref)

    blk = pl.BlockSpec((None, None, ADAM_TILE, width), lambda i, place_ref: (layer, place_ref[1], i, 0))
    blk4 = pl.BlockSpec((4, ADAM_TILE, width), lambda i, place_ref: (0, i, 0))
    mine = pl.BlockSpec((None, ADAM_TILE, width), lambda i, place_ref: (place_ref[0], i, 0))
    out = pl.BlockSpec((4, None, None, ADAM_TILE, width), lambda i, place_ref: (0, layer, place_ref[1], i, 0))
    extra = [] if into is None else [into]
    return pl.pallas_call(
        body, name=name,
        grid_spec=pltpu.PrefetchScalarGridSpec(num_scalar_prefetch=1, grid=(rows // ADAM_TILE,),
                                               in_specs=[blk4, mine, blk, blk, blk] + [ANY] * len(extra), out_specs=out),
        out_shape=jax.ShapeDtypeStruct((4, layers, 2, rows, width), F32),
        input_output_aliases={} if into is None else {6: 0}, compiler_params=_cparams("parallel"),
    )(place, parts, own, w, m, v, *extra)


def kernel(x, norm_gain, a_w_in, a_v_gain, a_w_s, a_b_s, a_w_out, b_w_in, b_q_gain, b_k_gain, b_w_out, c_w_in, c_w_grp, c_scale, c_w_out, loss_target, m_norm_gain, m_a_w_in, m_a_v_gain, m_a_w_s, m_a_b_s, m_a_w_out, m_b_w_in, m_b_q_gain, m_b_k_gain, m_b_w_out, m_c_w_in, m_c_w_grp, m_c_scale, m_c_w_out, v_norm_gain, v_a_w_in, v_a_v_gain, v_a_w_s, v_a_b_s, v_a_w_out, v_b_w_in, v_b_q_gain, v_b_k_gain, v_b_w_out, v_c_w_in, v_c_w_grp, v_c_scale, v_c_w_out):
    wts = dict(norm_gain=norm_gain, a_w_in=a_w_in, a_v_gain=a_v_gain, a_w_s=a_w_s, a_b_s=a_b_s, a_w_out=a_w_out, b_w_in=b_w_in,
               b_q_gain=b_q_gain, b_k_gain=b_k_gain, b_w_out=b_w_out, c_w_in=c_w_in, c_w_grp=c_w_grp, c_scale=c_scale, c_w_out=c_w_out)
    mom1 = dict(norm_gain=m_norm_gain, a_w_in=m_a_w_in, a_v_gain=m_a_v_gain, a_w_s=m_a_w_s, a_b_s=m_a_b_s, a_w_out=m_a_w_out,
                b_w_in=m_b_w_in, b_q_gain=m_b_q_gain, b_k_gain=m_b_k_gain, b_w_out=m_b_w_out, c_w_in=m_c_w_in, c_w_grp=m_c_w_grp,
                c_scale=m_c_scale, c_w_out=m_c_w_out)
    mom2 = dict(norm_gain=v_norm_gain, a_w_in=v_a_w_in, a_v_gain=v_a_v_gain, a_w_s=v_a_w_s, a_b_s=v_a_b_s, a_w_out=v_a_w_out,
                b_w_in=v_b_w_in, b_q_gain=v_b_q_gain, b_k_gain=v_b_k_gain, b_w_out=v_b_w_out, c_w_in=v_c_w_in, c_w_grp=v_c_w_grp,
                c_scale=v_c_scale, c_w_out=v_c_w_out)
    axes = ("x", "y", "c")
    me = 2 * lax.axis_index("x") + lax.axis_index("y")
    core = lax.axis_index("c")

    place = jnp.stack([me, core]).astype(jnp.int32)

    slots = _gather_inputs(wts, me)
    first = _allgather_units([slots[u] for u in FIRST_UNITS])
    full = _gathered_weights(FIRST_UNITS, first)
    later = sorted(LATER_UNITS)
    gather_sems, gather_units, token = _gather_start([[slots[u] for u in LATER_UNITS[li]] for li in later], first[-1])
    for n, _ in REP:
        full[n] = wts[n]
    full["norm_gain"] = wts["norm_gain"] + token[0, 0]

    def layer_weights(li, x_in):
        if li not in LATER_UNITS:
            return {}
        g = later.index(li)
        landed = _gather_wait(gather_units[g], gather_sems[g], x_in, f"gather_wait_l{li}")
        return _gathered_weights(LATER_UNITS[li], _forward_units(landed, f"gather_forward_l{li}"))

    def unit_grad(u, grads):
        n, layer, r, w = REDUCE_UNITS[u]
        g = grads[n][layer] if n in ("a_w_in", "a_w_out") else grads[n]
        return g.astype(CDT).reshape(N_CHIPS, 2, r, w)

    chip_sums, in_flight = {}, []

    def start_exchange(li, grads):
        if li not in EARLY_UNITS:
            return jnp.zeros((), F32)
        units = EARLY_UNITS[li]
        mine = [unit_grad(u, grads) for u in units]
        theirs = _swap_halves(mine, f"grad_swap_halves_l{li}")
        sums = [_add_halves(g, t, place, f"grad_add_halves{u}") for u, g, t in zip(units, mine, theirs)]
        chip_sums.update(zip(units, sums))
        send_sems, recv_sems, srcs, lands, token = _scatter_start(sums, f"grad_scatter_start_l{li}")
        in_flight.append((li, units, send_sems, recv_sems, srcs, lands))
        return token[0, 0]

    sq, grad_x, grads = _local_step(x[0], loss_target[0], full, start_exchange, layer_weights)
    loss = lax.psum(0.5 * jnp.sum(sq) / D_MODEL, axes)

    rep_g = _pack_rep(grads)
    late = [u for u in range(len(REDUCE_UNITS)) if not any(u in us for us in EARLY_UNITS.values())]
    mine = [unit_grad(u, grads) for u in late]
    mine.append(jnp.stack([
        _small_unit([lax.slice_in_dim(grads[v], 512 * k, 512 * (k + 1), axis=1) for v in SMALL_OF_HALF],
                    rep_g[k * REP_PART:(k + 1) * REP_PART], CDT) for k in range(N_CHIPS)]))
    late.append(len(REDUCE_UNITS))
    theirs = _swap_halves(mine, "grad_swap_halves_late")
    sums = [_add_halves(g, t, place, f"grad_add_halves{u}") for u, g, t in zip(late, mine, theirs)]
    chip_sums.update(zip(late, sums))
    parts = dict(zip(late, _scatter_shards(sums)))
    for li, units, send_sems, recv_sems, srcs, lands in in_flight:
        parts.update(zip(units, _scatter_wait(send_sems, recv_sems, srcs, lands, grad_x, f"grad_scatter_wait_l{li}")))
    order = range(len(REDUCE_UNITS) + 1)
    parts, chip_sums = [parts[u] for u in order], [chip_sums[u] for u in order]

    def state_units(tree):
        units = [tree[n].reshape(BLOCK_SHAPES[n][0], 2, r, w) for n, _, r, w in REDUCE_UNITS]
        rep_part = lax.dynamic_slice_in_dim(_pack_rep(tree), me * REP_PART, REP_PART, axis=0)
        return units + [_small_unit([tree[v] for v in SMALL_OF_HALF], rep_part, F32)[None]]

    names = [n for n, _, _, _ in REDUCE_UNITS] + ["small"]
    layers = [layer for _, layer, _, _ in REDUCE_UNITS] + [0]
    res, state = {}, list(zip(parts, chip_sums, state_units(wts), state_units(mom1), state_units(mom2)))

    def update(units):
        for u in units:
            res[names[u]] = _adamw(place, *state[u], layers[u], f"adamw{u}", into=res.get(names[u]))

    update((len(REDUCE_UNITS),))
    small = _share_halves([res.pop("small")], "share_small")[0][:, 0]
    rep_mine = jnp.concatenate([small[0, 0, SMALL_PAD:], small[0, 1, SMALL_PAD:]], axis=0)
    rep_sems = _rep_start(rep_mine)
    update(LAYER_UNITS[1] + LAYER_UNITS[2])
    early = list(res)
    send_sems, recv_sems, early_arrs, _ = _share_start([res[n] for n in early], "share_start")
    update(LAYER_UNITS[0] + LAYER_UNITS[3])
    rest = [n for n in res if n not in early]
    res.update(zip(rest, _share_halves([res[n] for n in rest], "share_halves")))
    res.update(zip(early, _share_wait(send_sems, recv_sems, early_arrs, res[rest[0]], "share_wait")))
    rep_g = lax.dynamic_update_slice_in_dim(_rep_wait(*rep_sems, res[rest[0]]), rep_mine[None], me, axis=0)
    rep_all = _adamw_rep(rep_g.reshape(N_CHIPS * REP_PART, ROW), _pack_rep(wts), _pack_rep(mom1), _pack_rep(mom2))

    outs = []
    for q in range(4):
        tree = {n: arr[q].reshape(BLOCK_SHAPES[n]) for n, arr in res.items()}
        tree["a_v_gain"] = small[q, 0, 0].reshape(2, 512)
        tree["c_scale"] = small[q, 1, 0, :512].reshape(1, 512)
        tree.update(_unpack_rep(rep_all[q]))
        outs.append(tree)
    return (loss, grad_x[None], *[t[n] for t in outs for n in WEIGHTS])
```

```python
import math

import jax
import jax.numpy as jnp
from jax import lax
from jax.experimental import pallas as pl
from jax.experimental.pallas import tpu as pltpu

F32 = jnp.float32
CDT = jnp.bfloat16

D_MODEL = 1024
EPS = 1e-6
CHUNK = 128
A_WIDTH = 2048
A_GROUPS = 8
A_GROUP_DIM = 256
HEAD_DIM = 128
B_HEADS = 8
B_DILATIONS = (1, 4, 16)
B_QK = 6144
B_IN = 10240
ROPE_HALF = 16
ROPE_THETA = 500000.0
POOL_SIZES = (2, 4, 8, 16)
POOL_HALO = 16
C_WIDTH = 2048
C_GROUP = 512
N_CHIPS = 4

ADAM_LR = 0.001
ADAM_B1 = 0.9
ADAM_B2 = 0.999
ADAM_EPS = 1e-08
ADAM_WD = 0.01
ADAM_STEP = 10

VMEM_LIMIT = 48 * 1024 * 1024
ANY = pl.BlockSpec(memory_space=pl.ANY)
MESH = pl.DeviceIdType.MESH

NN = (((1,), (0,)), ((), ()))
NT = (((1,), (1,)), ((), ()))
TN = (((0,), (0,)), ((), ()))


def _cparams(*sem):
    return pltpu.CompilerParams(dimension_semantics=sem, vmem_limit_bytes=VMEM_LIMIT)


def _dot(a, b, dims=NN):
    return lax.dot_general(a, b, dims, preferred_element_type=F32)


def _sigmoid(z):
    return 1.0 / (1.0 + jnp.exp(-z))


def _lane_sums(v):
    ones = jnp.ones((HEAD_DIM, HEAD_DIM), jnp.bfloat16)
    hi = v.astype(jnp.bfloat16)
    lo = (v - hi.astype(F32)).astype(jnp.bfloat16)
    return _dot(hi, ones) + _dot(lo, ones)


def _mm(a, b, mode, out_dtype, name, mnk, tiles, b_spec=None, o_spec=None, o_shape=None, add=None):
    m, n, k = mnk
    tm, tn, tk = min(tiles[0], m), min(tiles[1], n), min(tiles[2], k)
    nk = k // tk
    a_spec = {"nn": pl.BlockSpec((tm, tk), lambda i, j, q: (i, q)),
              "nt": pl.BlockSpec((tm, tk), lambda i, j, q: (i, q)),
              "tn": pl.BlockSpec((tk, tm), lambda i, j, q: (q, i))}[mode]
    if b_spec is None:
        b_spec = {"nn": pl.BlockSpec((tk, tn), lambda i, j, q: (q, j)),
                  "nt": pl.BlockSpec((tn, tk), lambda i, j, q: (j, q)),
                  "tn": pl.BlockSpec((tk, tn), lambda i, j, q: (q, j))}[mode]
    if o_spec is None:
        o_spec, o_shape = pl.BlockSpec((tm, tn), lambda i, j, q: (i, j)), (m, n)
    dims = {"nn": NN, "nt": NT, "tn": TN}[mode]
    has_add = add is not None

    def body(*refs):
        a_ref, b_ref = refs[0], refs[1]
        o_ref = refs[3] if has_add else refs[2]
        p = _dot(a_ref[...], b_ref[...], dims)

        def finish(v):
            if has_add:
                v = v + refs[2][...]
            o_ref[...] = v.astype(o_ref.dtype)

        if nk == 1:
            finish(p)
        else:
            acc_ref = refs[-1]
            q = pl.program_id(2)

            @pl.when(q == 0)
            def _():
                acc_ref[...] = p

            @pl.when(q > 0)
            def _():
                acc_ref[...] += p

            @pl.when(q == nk - 1)
            def _():
                finish(acc_ref[...])

    in_specs = [a_spec, b_spec]
    args = [a, b]
    if has_add:
        in_specs.append(pl.BlockSpec((tm, tn), lambda i, j, q: (i, j)))
        args.append(add)
    return pl.pallas_call(
        body, name=name, grid=(m // tm, n // tn, nk), in_specs=in_specs, out_specs=o_spec,
        out_shape=jax.ShapeDtypeStruct(o_shape, out_dtype),
        scratch_shapes=[pltpu.VMEM((tm, tn), F32)] if nk > 1 else [],
        compiler_params=_cparams("parallel", "parallel", "arbitrary"),
    )(*args)


def _rms_fwd(x, g, name, tq=512):
    s, d = x.shape

    def body(x_ref, g_ref, h_ref):
        xv = x_ref[...]
        r = lax.rsqrt(jnp.mean(xv * xv, axis=-1, keepdims=True) + EPS)
        h_ref[...] = (xv * r * g_ref[...]).astype(h_ref.dtype)

    return pl.pallas_call(
        body, name=name, grid=(s // tq,),
        in_specs=[pl.BlockSpec((tq, d), lambda i: (i, 0)), pl.BlockSpec((1, d), lambda i: (0, 0))],
        out_specs=pl.BlockSpec((tq, d), lambda i: (i, 0)),
        out_shape=jax.ShapeDtypeStruct((s, d), CDT), compiler_params=_cparams("parallel"),
    )(x, g)


def _loss_bwd(y, target, name, tq=512):
    s, d = y.shape

    def body(y_ref, t_ref, dx_ref, dxc_ref, sq_ref):
        err = y_ref[...] - t_ref[...]
        dx = err * (1.0 / d)
        dx_ref[...] = dx
        dxc_ref[...] = dx.astype(dxc_ref.dtype)

        @pl.when(pl.program_id(0) == 0)
        def _():
            sq_ref[...] = jnp.zeros_like(sq_ref)

        sq_ref[...] += jnp.sum(err * err, axis=0, keepdims=True)

    blk = pl.BlockSpec((tq, d), lambda i: (i, 0))
    vec = pl.BlockSpec((1, d), lambda i: (0, 0))
    return pl.pallas_call(
        body, name=name, grid=(s // tq,), in_specs=[blk, blk], out_specs=[blk, blk, vec],
        out_shape=[jax.ShapeDtypeStruct((s, d), F32), jax.ShapeDtypeStruct((s, d), CDT), jax.ShapeDtypeStruct((1, d), F32)],
        compiler_params=_cparams("arbitrary"),
    )(y, target)


def _tril_mask():
    row = lax.broadcasted_iota(jnp.int32, (CHUNK, CHUNK), 0)
    col = lax.broadcasted_iota(jnp.int32, (CHUNK, CHUNK), 1)
    return row >= col


def _a_mid_fwd(proj, v_gain, w_s, b_s_t, name, tq=256):
    s = proj.shape[0]

    def body(p_ref, vg_ref, ws_ref, bs_ref, y_ref):
        vraw = p_ref[:, A_WIDTH:2 * A_WIDTH].astype(F32)
        r = lax.rsqrt(jnp.mean(vraw * vraw, axis=-1, keepdims=True) + EPS)
        vn = (vraw * r * vg_ref[...]).astype(CDT)
        tri = _tril_mask()
        for g in range(A_GROUPS):
            w = jnp.where(tri, ws_ref[g], 0.0).astype(CDT)
            bias = bs_ref[:, g:g + 1]
            cols = slice(g * A_GROUP_DIM, (g + 1) * A_GROUP_DIM)
            zcols = slice(2 * A_WIDTH + g * A_GROUP_DIM, 2 * A_WIDTH + (g + 1) * A_GROUP_DIM)
            for c in range(tq // CHUNK):
                rows = slice(c * CHUNK, (c + 1) * CHUNK)
                mixed = _dot(w, vn[rows, cols]) + bias
                u = p_ref[rows, cols].astype(F32)
                z = p_ref[rows, zcols].astype(F32)
                y_ref[rows, cols] = (u * mixed * (z * _sigmoid(z))).astype(y_ref.dtype)

    return pl.pallas_call(
        body, name=name, grid=(s // tq,),
        in_specs=[pl.BlockSpec((tq, 3 * A_WIDTH), lambda i: (i, 0)), pl.BlockSpec((1, A_WIDTH), lambda i: (0, 0)),
                  pl.BlockSpec((A_GROUPS, CHUNK, CHUNK), lambda i: (0, 0, 0)), pl.BlockSpec((CHUNK, A_GROUPS), lambda i: (0, 0))],
        out_specs=pl.BlockSpec((tq, A_WIDTH), lambda i: (i, 0)),
        out_shape=jax.ShapeDtypeStruct((s, A_WIDTH), CDT), compiler_params=_cparams("parallel"),
    )(proj, v_gain, w_s, b_s_t)


def _a_mid_bwd(proj, dy, v_gain, w_s, b_s_t, name, tq=256):
    s = proj.shape[0]

    def body(p_ref, dy_ref, vg_ref, ws_ref, bs_ref, dp_ref, dws_ref, dbs_ref, dvg_ref, dvn_ref):
        @pl.when(pl.program_id(0) == 0)
        def _():
            dws_ref[...] = jnp.zeros_like(dws_ref)
            dbs_ref[...] = jnp.zeros_like(dbs_ref)
            dvg_ref[...] = jnp.zeros_like(dvg_ref)

        vraw = p_ref[:, A_WIDTH:2 * A_WIDTH].astype(F32)
        r = lax.rsqrt(jnp.mean(vraw * vraw, axis=-1, keepdims=True) + EPS)
        vhat = vraw * r
        vg = vg_ref[...]
        vn = (vhat * vg).astype(CDT)
        tri = _tril_mask()
        lane = lax.broadcasted_iota(jnp.int32, (CHUNK, A_GROUPS), 1)
        dbs = jnp.zeros((CHUNK, A_GROUPS), F32)
        for g in range(A_GROUPS):
            w = jnp.where(tri, ws_ref[g], 0.0).astype(CDT)
            bias = bs_ref[:, g:g + 1]
            cols = slice(g * A_GROUP_DIM, (g + 1) * A_GROUP_DIM)
            zcols = slice(2 * A_WIDTH + g * A_GROUP_DIM, 2 * A_WIDTH + (g + 1) * A_GROUP_DIM)
            dws = jnp.zeros((CHUNK, CHUNK), F32)
            for c in range(tq // CHUNK):
                rows = slice(c * CHUNK, (c + 1) * CHUNK)
                vn_g = vn[rows, cols]
                mixed = _dot(w, vn_g) + bias
                u = p_ref[rows, cols].astype(F32)
                z = p_ref[rows, zcols].astype(F32)
                dyv = dy_ref[rows, cols].astype(F32)
                sg = _sigmoid(z)
                sz = z * sg
                dyu = dyv * u
                dmixed = dyu * sz
                dp_ref[rows, cols] = (dyv * mixed * sz).astype(dp_ref.dtype)
                dp_ref[rows, zcols] = (dyu * mixed * (sg * (1.0 + z * (1.0 - sg)))).astype(dp_ref.dtype)
                dmc = dmixed.astype(CDT)
                dws = dws + _dot(dmc, vn_g, NT)
                dbs = dbs + jnp.where(lane == g, jnp.sum(dmixed, axis=-1, keepdims=True), 0.0)
                dvn_ref[rows, cols] = _dot(w, dmc, TN)
            dws_ref[g] += jnp.where(tri, dws, 0.0)
        dbs_ref[...] += dbs
        dvn = dvn_ref[...]
        gd = dvn * vg
        dvraw = r * gd - vraw * (r * r * r) * jnp.mean(gd * vraw, axis=-1, keepdims=True)
        dp_ref[:, A_WIDTH:2 * A_WIDTH] = dvraw.astype(dp_ref.dtype)
        dvg_ref[...] += jnp.sum(dvn * vhat, axis=0, keepdims=True)

    return pl.pallas_call(
        body, name=name, grid=(s // tq,),
        in_specs=[pl.BlockSpec((tq, 3 * A_WIDTH), lambda i: (i, 0)), pl.BlockSpec((tq, A_WIDTH), lambda i: (i, 0)),
                  pl.BlockSpec((1, A_WIDTH), lambda i: (0, 0)), pl.BlockSpec((A_GROUPS, CHUNK, CHUNK), lambda i: (0, 0, 0)),
                  pl.BlockSpec((CHUNK, A_GROUPS), lambda i: (0, 0))],
        out_specs=[pl.BlockSpec((tq, 3 * A_WIDTH), lambda i: (i, 0)), pl.BlockSpec((A_GROUPS, CHUNK, CHUNK), lambda i: (0, 0, 0)),
                   pl.BlockSpec((CHUNK, A_GROUPS), lambda i: (0, 0)), pl.BlockSpec((1, A_WIDTH), lambda i: (0, 0))],
        out_shape=[jax.ShapeDtypeStruct((s, 3 * A_WIDTH), CDT), jax.ShapeDtypeStruct((A_GROUPS, CHUNK, CHUNK), F32),
                   jax.ShapeDtypeStruct((CHUNK, A_GROUPS), F32), jax.ShapeDtypeStruct((1, A_WIDTH), F32)],
        scratch_shapes=[pltpu.VMEM((tq, A_WIDTH), F32)],
        compiler_params=_cparams("arbitrary"),
    )(proj, dy, v_gain, w_s, b_s_t)


def _rope_tables(s):
    inv_freq = jnp.power(jnp.float32(ROPE_THETA), -jnp.arange(ROPE_HALF, dtype=F32) / ROPE_HALF)
    ang = jnp.arange(s, dtype=F32)[:, None] * inv_freq[None, :]
    cos, sin = jnp.cos(ang), jnp.sin(ang)
    rest = HEAD_DIM - 2 * ROPE_HALF
    t_c = jnp.concatenate([cos, cos, jnp.ones((s, rest), F32)], axis=1)
    t_a = jnp.concatenate([-sin, jnp.zeros((s, HEAD_DIM - ROPE_HALF), F32)], axis=1)
    t_b = jnp.concatenate([jnp.zeros((s, ROPE_HALF), F32), sin, jnp.zeros((s, rest), F32)], axis=1)
    return t_c, t_a, t_b


def _b_qk_fwd(proj, gains, tabs, name, tq=256):
    s = proj.shape[0]

    def body(p_ref, g_ref, tc_ref, ta_ref, tb_ref, o_ref):
        tc, ta, tb = tc_ref[...], ta_ref[...], tb_ref[...]
        for tg in range(6):
            gain = g_ref[tg:tg + 1, :]
            for h in range(B_HEADS):
                cols = slice(tg * 1024 + h * HEAD_DIM, tg * 1024 + (h + 1) * HEAD_DIM)
                xv = p_ref[:, cols].astype(F32)
                r = lax.rsqrt(_lane_sums(xv * xv) * (1.0 / HEAD_DIM) + EPS)
                xn = xv * r * gain
                y = xn * tc + pltpu.roll(xn, HEAD_DIM - ROPE_HALF, 1) * ta + pltpu.roll(xn, ROPE_HALF, 1) * tb
                o_ref[:, cols] = y.astype(o_ref.dtype)

    tab = pl.BlockSpec((tq, HEAD_DIM), lambda i: (i, 0))
    return pl.pallas_call(
        body, name=name, grid=(s // tq,),
        in_specs=[pl.BlockSpec((tq, B_QK), lambda i: (i, 0)), pl.BlockSpec((6, HEAD_DIM), lambda i: (0, 0)), tab, tab, tab],
        out_specs=pl.BlockSpec((tq, B_QK), lambda i: (i, 0)),
        out_shape=jax.ShapeDtypeStruct((s, B_QK), CDT), compiler_params=_cparams("parallel"),
    )(proj, gains, *tabs)


PERMUTE_BLOCK_BYTES = 4 * 1024 * 1024


def _view_rows(length, dil, width, itemsize):
    rows = 16
    while 2 * rows * dil * width * itemsize <= PERMUTE_BLOCK_BYTES and 2 * rows <= length:
        rows *= 2
    return rows


def _to_view(x, col, width, dil, name):
    s = x.shape[0]
    length = s // dil
    tl = _view_rows(length, dil, width, 4)
    lanes = HEAD_DIM
    nblk = width // lanes

    def body(x_ref, o_ref, slab_ref):
        for b in range(nblk):
            slab_ref[b] = x_ref[:, b * lanes:(b + 1) * lanes].astype(F32)
        for r in range(dil):
            for b in range(nblk):
                o_ref[:, r * width + b * lanes:r * width + (b + 1) * lanes] = (
                    slab_ref.at[b][pl.ds(r, tl, stride=dil), :].astype(o_ref.dtype))

    return pl.pallas_call(
        body, name=name, grid=(length // tl,),
        in_specs=[pl.BlockSpec((tl * dil, width), lambda i: (i, col))],
        out_specs=pl.BlockSpec((tl, dil * width), lambda i: (i, 0)),
        out_shape=jax.ShapeDtypeStruct((length, dil * width), x.dtype),
        scratch_shapes=[pltpu.VMEM((nblk, tl * dil, lanes), F32)],
        compiler_params=_cparams("parallel"),
    )(x)


def _from_view(v, dil, name):
    length, width = v.shape[0], v.shape[1] // dil
    tl = _view_rows(length, dil, width, 4)
    lanes = HEAD_DIM
    nblk = width // lanes

    def body(v_ref, o_ref, slab_ref):
        for r in range(dil):
            for b in range(nblk):
                slab_ref.at[b][pl.ds(r, tl, stride=dil), :] = v_ref[:, r * width + b * lanes:r * width + (b + 1) * lanes].astype(F32)
        for b in range(nblk):
            o_ref[:, b * lanes:(b + 1) * lanes] = slab_ref[b].astype(o_ref.dtype)

    return pl.pallas_call(
        body, name=name, grid=(length // tl,),
        in_specs=[pl.BlockSpec((tl, dil * width), lambda i: (i, 0))],
        out_specs=pl.BlockSpec((tl * dil, width), lambda i: (i, 0)),
        out_shape=jax.ShapeDtypeStruct((length * dil, width), v.dtype),
        scratch_shapes=[pltpu.VMEM((nblk, tl * dil, lanes), F32)],
        compiler_params=_cparams("parallel"),
    )(v)


def _b_attn_fwd(q, k, v, bases, dil, name):
    length = q.shape[0]
    nb = length // CHUNK
    per = 4 if nb % 4 == 0 else 2 if nb % 2 == 0 else 1
    scale = 1.0 / math.sqrt(HEAD_DIM)
    w = B_HEADS * HEAD_DIM
    qb, kb, vb = bases

    def body(q_ref, kc_ref, kp_ref, vc_ref, vp_ref, o_ref, lse_ref):
        n = pl.program_id(1)
        qi = lax.broadcasted_iota(jnp.int32, (CHUNK, 2 * CHUNK), 0)
        ki = lax.broadcasted_iota(jnp.int32, (CHUNK, 2 * CHUNK), 1)
        first_key = jnp.where(n > 0, 0, CHUNK)
        band = (ki >= qi) & (ki <= qi + CHUNK)
        lane = lax.broadcasted_iota(jnp.int32, (CHUNK, HEAD_DIM), 1)
        for b in range(per):
            rows = slice(b * CHUNK, (b + 1) * CHUNK)
            above = slice((b - 1) * CHUNK, b * CHUNK)
            mask = band & (ki >= first_key) if b == 0 else band
            lse_all = jnp.zeros((CHUNK, HEAD_DIM), F32)
            for h in range(B_HEADS):
                sl = slice(h * HEAD_DIM, (h + 1) * HEAD_DIM)
                k_prev = kp_ref[:, sl] if b == 0 else kc_ref[above, sl]
                v_prev = vp_ref[:, sl] if b == 0 else vc_ref[above, sl]
                k2 = jnp.concatenate([k_prev, kc_ref[rows, sl]], axis=0)
                v2 = jnp.concatenate([v_prev, vc_ref[rows, sl]], axis=0)
                sc = jnp.where(mask, _dot(q_ref[rows, sl], k2, NT) * scale, -1e30)
                m = jnp.max(sc, axis=-1, keepdims=True)
                p = jnp.exp(sc - m)
                l = jnp.sum(p, axis=-1, keepdims=True)
                o_ref[rows, sl] = (_dot(p.astype(CDT), v2) / l).astype(o_ref.dtype)
                lse_all = jnp.where(lane == h, m + jnp.log(l), lse_all)
            lse_ref[rows, :] = lse_all

    prev = lambda n: jnp.maximum(per * n - 1, 0)
    blk = lambda f: pl.BlockSpec((per * CHUNK, w), f)
    halo = lambda f: pl.BlockSpec((CHUNK, w), f)
    return pl.pallas_call(
        body, name=name, grid=(dil, nb // per),
        in_specs=[blk(lambda r, n: (n, qb + r)), blk(lambda r, n: (n, kb + r)), halo(lambda r, n: (prev(n), kb + r)),
                  blk(lambda r, n: (n, vb + r)), halo(lambda r, n: (prev(n), vb + r))],
        out_specs=[blk(lambda r, n: (n, r)), pl.BlockSpec((per * CHUNK, HEAD_DIM), lambda r, n: (n, r))],
        out_shape=[jax.ShapeDtypeStruct((length, dil * w), CDT), jax.ShapeDtypeStruct((length, dil * HEAD_DIM), F32)],
        compiler_params=_cparams("parallel", "parallel"),
    )(q, k, k, v, v)


def _b_combine(os_, lses, proj, name, tq=512):
    s = proj.shape[0]
    w = B_HEADS * HEAD_DIM

    def body(o0_ref, o1_ref, o2_ref, l0_ref, l1_ref, l2_ref, z_ref, y_ref, oj_ref, lj_ref):
        l0, l1, l2 = l0_ref[...], l1_ref[...], l2_ref[...]
        m = jnp.maximum(jnp.maximum(l0, l1), l2)
        lj = m + jnp.log(jnp.exp(l0 - m) + jnp.exp(l1 - m) + jnp.exp(l2 - m))
        lj_ref[...] = lj
        w0, w1, w2 = jnp.exp(l0 - lj), jnp.exp(l1 - lj), jnp.exp(l2 - lj)
        for h in range(B_HEADS):
            sl = slice(h * HEAD_DIM, (h + 1) * HEAD_DIM)
            o = (w0[:, h:h + 1] * o0_ref[:, sl].astype(F32) + w1[:, h:h + 1] * o1_ref[:, sl].astype(F32)
                 + w2[:, h:h + 1] * o2_ref[:, sl].astype(F32))
            z = z_ref[:, sl].astype(F32)
            oj_ref[:, sl] = o.astype(oj_ref.dtype)
            y_ref[:, sl] = (o * (z * _sigmoid(z))).astype(y_ref.dtype)

    blk = pl.BlockSpec((tq, w), lambda i: (i, 0))
    st = pl.BlockSpec((tq, HEAD_DIM), lambda i: (i, 0))
    return pl.pallas_call(
        body, name=name, grid=(s // tq,),
        in_specs=[blk, blk, blk, st, st, st, pl.BlockSpec((tq, w), lambda i: (i, 9))],
        out_specs=[blk, blk, st],
        out_shape=[jax.ShapeDtypeStruct((s, w), CDT), jax.ShapeDtypeStruct((s, w), CDT), jax.ShapeDtypeStruct((s, HEAD_DIM), F32)],
        compiler_params=_cparams("parallel"),
    )(*os_, *lses, proj)


def _b_bwd_prep(dy, oj, proj, name, tq=512):
    s = proj.shape[0]
    w = B_HEADS * HEAD_DIM

    def body(dy_ref, oj_ref, z_ref, do_ref, dz_ref, dd_ref):
        lane = lax.broadcasted_iota(jnp.int32, (tq, HEAD_DIM), 1)
        dd = jnp.zeros((tq, HEAD_DIM), F32)
        for h in range(B_HEADS):
            sl = slice(h * HEAD_DIM, (h + 1) * HEAD_DIM)
            z = z_ref[:, sl].astype(F32)
            dyv = dy_ref[:, sl].astype(F32)
            o = oj_ref[:, sl].astype(F32)
            sg = _sigmoid(z)
            do = dyv * (z * sg)
            do_ref[:, sl] = do.astype(do_ref.dtype)
            dz_ref[:, sl] = (dyv * o * (sg * (1.0 + z * (1.0 - sg)))).astype(dz_ref.dtype)
            dd = jnp.where(lane == h, jnp.sum(do * o, axis=-1, keepdims=True), dd)
        dd_ref[...] = dd

    blk = pl.BlockSpec((tq, w), lambda i: (i, 0))
    st = pl.BlockSpec((tq, HEAD_DIM), lambda i: (i, 0))
    return pl.pallas_call(
        body, name=name, grid=(s // tq,),
        in_specs=[blk, blk, pl.BlockSpec((tq, w), lambda i: (i, 9))], out_specs=[blk, blk, st],
        out_shape=[jax.ShapeDtypeStruct((s, w), CDT), jax.ShapeDtypeStruct((s, w), CDT), jax.ShapeDtypeStruct((s, HEAD_DIM), F32)],
        compiler_params=_cparams("parallel"),
    )(dy, oj, proj)


def _b_attn_bwd(q, k, v, bases, do, lj, dd, dil, name):
    length = q.shape[0]
    nb = length // CHUNK
    per = 2 if nb % 2 == 0 else 1
    steps = nb // per
    scale = 1.0 / math.sqrt(HEAD_DIM)
    w = B_HEADS * HEAD_DIM
    qb, kb, vb = bases

    def body(qc_ref, qn_ref, k_ref, v_ref, doc_ref, don_ref, lc_ref, ln_ref, dc_ref, dn_ref, out_ref, carry_ref):
        j = pl.program_id(1)

        @pl.when(j == 0)
        def _():
            carry_ref[...] = jnp.zeros_like(carry_ref)

        qi = lax.broadcasted_iota(jnp.int32, (2 * CHUNK, CHUNK), 0)
        ki = lax.broadcasted_iota(jnp.int32, (2 * CHUNK, CHUNK), 1)
        no_next = jnp.where(j + 1 < steps, 0, 2 * CHUNK)
        for h in range(B_HEADS):
            sl = slice(h * HEAD_DIM, (h + 1) * HEAD_DIM)
            st_ = slice(h, h + 1)
            carried = carry_ref[:, sl]
            for b in range(per):
                rows = slice(b * CHUNK, (b + 1) * CHUNK)
                if b + 1 < per:
                    after = slice((b + 1) * CHUNK, (b + 2) * CHUNK)
                    q_n, do_n, l_n, d_n, gone = qc_ref[after, sl], doc_ref[after, sl], lc_ref[after, st_], dc_ref[after, st_], 0
                else:
                    q_n, do_n, l_n, d_n, gone = qn_ref[:, sl], don_ref[:, sl], ln_ref[:, st_], dn_ref[:, st_], no_next
                mask = ((qi < CHUNK) & (ki <= qi)) | ((qi >= CHUNK) & (ki >= qi - CHUNK + gone))
                q2 = jnp.concatenate([qc_ref[rows, sl], q_n], axis=0)
                do2 = jnp.concatenate([doc_ref[rows, sl], do_n], axis=0)
                lse2 = jnp.concatenate([lc_ref[rows, st_], l_n], axis=0)
                d2 = jnp.concatenate([dc_ref[rows, st_], d_n], axis=0)
                k = k_ref[rows, sl]
                v = v_ref[rows, sl]
                sc = _dot(q2, k, NT) * scale
                p = jnp.where(mask, jnp.exp(sc - lse2), 0.0)
                dp = _dot(do2, v, NT)
                ds = (p * (dp - d2) * scale).astype(CDT)
                dq2 = _dot(ds, k)
                out_ref[rows, sl] = (carried + dq2[:CHUNK]).astype(out_ref.dtype)
                carried = dq2[CHUNK:]
                out_ref[rows, w + h * HEAD_DIM:w + (h + 1) * HEAD_DIM] = _dot(ds, q2, TN).astype(out_ref.dtype)
                out_ref[rows, 2 * w + h * HEAD_DIM:2 * w + (h + 1) * HEAD_DIM] = _dot(p.astype(CDT), do2, TN).astype(out_ref.dtype)
            carry_ref[:, sl] = carried

    nxt = lambda j: jnp.minimum(per * (j + 1), nb - 1)
    blk = lambda f: pl.BlockSpec((per * CHUNK, w), f)
    one = lambda f: pl.BlockSpec((CHUNK, w), f)
    st = lambda f: pl.BlockSpec((per * CHUNK, HEAD_DIM), f)
    st1 = lambda f: pl.BlockSpec((CHUNK, HEAD_DIM), f)
    return pl.pallas_call(
        body, name=name, grid=(dil, steps),
        in_specs=[blk(lambda r, j: (j, qb + r)), one(lambda r, j: (nxt(j), qb + r)),
                  blk(lambda r, j: (j, kb + r)), blk(lambda r, j: (j, vb + r)),
                  blk(lambda r, j: (j, r)), one(lambda r, j: (nxt(j), r)),
                  st(lambda r, j: (j, r)), st1(lambda r, j: (nxt(j), r)), st(lambda r, j: (j, r)), st1(lambda r, j: (nxt(j), r))],
        out_specs=pl.BlockSpec((per * CHUNK, 3 * w), lambda r, j: (j, r)),
        out_shape=jax.ShapeDtypeStruct((length, dil * 3 * w), CDT),
        scratch_shapes=[pltpu.VMEM((CHUNK, w), F32)],
        compiler_params=_cparams("parallel", "arbitrary"),
    )(q, q, k, v, do, do, lj, lj, dd, dd)


def _b_qk_bwd(proj, dqkv, dz, gains, tabs, name, tq=256):
    s = proj.shape[0]
    w = B_HEADS * HEAD_DIM

    def body(p_ref, d0_ref, d1_ref, d2_ref, dz_ref, g_ref, tc_ref, ta_ref, tb_ref, dp_ref, dg_ref):
        @pl.when(pl.program_id(0) == 0)
        def _():
            dg_ref[...] = jnp.zeros_like(dg_ref)

        tc, ta, tb = tc_ref[...], ta_ref[...], tb_ref[...]
        d_refs = (d0_ref, d1_ref, d2_ref)
        for g in range(3):
            for t in range(2):
                tg = t * 3 + g
                gain = g_ref[tg:tg + 1, :]
                dgain = jnp.zeros((1, HEAD_DIM), F32)
                for h in range(B_HEADS):
                    cols = slice(tg * w + h * HEAD_DIM, tg * w + (h + 1) * HEAD_DIM)
                    xv = p_ref[:, cols].astype(F32)
                    r = lax.rsqrt(_lane_sums(xv * xv) * (1.0 / HEAD_DIM) + EPS)
                    dyv = d_refs[g][:, t * w + h * HEAD_DIM:t * w + (h + 1) * HEAD_DIM].astype(F32)
                    dxn = dyv * tc + pltpu.roll(dyv * ta, ROPE_HALF, 1) + pltpu.roll(dyv * tb, HEAD_DIM - ROPE_HALF, 1)
                    gd = dxn * gain
                    dx = r * gd - xv * (r * r * r) * (_lane_sums(gd * xv) * (1.0 / HEAD_DIM))
                    dp_ref[:, cols] = dx.astype(dp_ref.dtype)
                    dgain = dgain + jnp.sum(dxn * xv * r, axis=0, keepdims=True)
                dg_ref[tg:tg + 1, :] += dgain
            dp_ref[:, (6 + g) * w:(7 + g) * w] = d_refs[g][:, 2 * w:3 * w]
        dp_ref[:, 9 * w:10 * w] = dz_ref[...]

    tab = pl.BlockSpec((tq, HEAD_DIM), lambda i: (i, 0))
    dblk = pl.BlockSpec((tq, 3 * w), lambda i: (i, 0))
    return pl.pallas_call(
        body, name=name, grid=(s // tq,),
        in_specs=[pl.BlockSpec((tq, B_QK), lambda i: (i, 0)), dblk, dblk, dblk, pl.BlockSpec((tq, w), lambda i: (i, 0)),
                  pl.BlockSpec((6, HEAD_DIM), lambda i: (0, 0)), tab, tab, tab],
        out_specs=[pl.BlockSpec((tq, B_IN), lambda i: (i, 0)), pl.BlockSpec((6, HEAD_DIM), lambda i: (0, 0))],
        out_shape=[jax.ShapeDtypeStruct((s, B_IN), CDT), jax.ShapeDtypeStruct((6, HEAD_DIM), F32)],
        compiler_params=_cparams("arbitrary"),
    )(proj, *dqkv, dz, gains, *tabs)


def _inv_count(t, window):
    return 1.0 / jnp.minimum(t + 1, window).astype(F32)


POOL_BLOCK = 128


def _band(window, forward):
    r = lax.broadcasted_iota(jnp.int32, (POOL_BLOCK, 2 * POOL_BLOCK), 0)
    j = lax.broadcasted_iota(jnp.int32, (POOL_BLOCK, 2 * POOL_BLOCK), 1)
    if forward:
        return jnp.where((j >= r) & (j < r + window), 1.0, 0.0).astype(CDT)
    return jnp.where((j <= r + POOL_BLOCK) & (j > r + POOL_BLOCK - window), 1.0, 0.0).astype(CDT)


def _split_dot(band, v):
    hi = v.astype(jnp.bfloat16)
    lo = (v - hi.astype(F32)).astype(jnp.bfloat16)
    band = band.astype(jnp.bfloat16)
    return _dot(band, hi) + _dot(band, lo)


def _pool_diff(x_ref, halo_ref, diff_ref, i, tq):
    t = i * tq + lax.broadcasted_iota(jnp.int32, (tq, 1), 0)
    for g, window in enumerate(POOL_SIZES):
        cols = slice(g * C_GROUP, (g + 1) * C_GROUP)
        band = _band(window, forward=False)
        inv = _inv_count(t, window)
        for b in range(tq // POOL_BLOCK):
            rows = slice(b * POOL_BLOCK, (b + 1) * POOL_BLOCK)
            cur = x_ref[rows, cols]
            if b == 0:
                above = jnp.where(i > 0, halo_ref[:, cols], jnp.zeros_like(cur))
            else:
                above = x_ref[(b - 1) * POOL_BLOCK:b * POOL_BLOCK, cols]
            pooled = _dot(band, jnp.concatenate([above, cur], axis=0)) * inv[rows]
            diff_ref[rows, cols] = (pooled - cur.astype(F32)).astype(diff_ref.dtype)


GRP_SHARD = (N_CHIPS, 2, 256, C_GROUP)
GRP_ROWS = C_GROUP // N_CHIPS


def _grp_rows(g):
    return g // 2, slice((g % 2) * GRP_ROWS, (g % 2 + 1) * GRP_ROWS)


def _grp_weight(w_ref, g):
    half, rows = _grp_rows(g)
    return jnp.concatenate([w_ref[k, half, rows, :] for k in range(N_CHIPS)], axis=0)


def _c_mid_fwd(proj, w_grp, scale, name, tq=512):
    s = proj.shape[0]
    per = tq // POOL_BLOCK

    def body(x_ref, halo_ref, z_ref, w_ref, sc_ref, y_ref, diff_ref):
        _pool_diff(x_ref, halo_ref, diff_ref, pl.program_id(0), tq)
        for g in range(len(POOL_SIZES)):
            cols = slice(g * C_GROUP, (g + 1) * C_GROUP)
            z = z_ref[:, cols].astype(F32)
            y_ref[:, cols] = (_dot(diff_ref[:, cols], _grp_weight(w_ref, g)) * sc_ref[:, cols] * (z * _sigmoid(z))).astype(y_ref.dtype)

    return pl.pallas_call(
        body, name=name, grid=(s // tq,),
        in_specs=[pl.BlockSpec((tq, C_WIDTH), lambda i: (i, 0)),
                  pl.BlockSpec((POOL_BLOCK, C_WIDTH), lambda i: (jnp.maximum(i * per - 1, 0), 0)),
                  pl.BlockSpec((tq, C_WIDTH), lambda i: (i, 1)),
                  pl.BlockSpec(GRP_SHARD, lambda i: (0, 0, 0, 0)), pl.BlockSpec((1, C_WIDTH), lambda i: (0, 0))],
        out_specs=pl.BlockSpec((tq, C_WIDTH), lambda i: (i, 0)),
        out_shape=jax.ShapeDtypeStruct((s, C_WIDTH), CDT),
        scratch_shapes=[pltpu.VMEM((tq, C_WIDTH), CDT)], compiler_params=_cparams("parallel"),
    )(proj, proj, proj, w_grp, scale)


def _c_mid_bwd(proj, w_grp, scale, dy, name, tq=512):
    s = proj.shape[0]
    per = tq // POOL_BLOCK

    def body(x_ref, halo_ref, z_ref, w_ref, sc_ref, dy_ref, dd_ref, dz_ref, dw_ref, dsc_ref, diff_ref):
        @pl.when(pl.program_id(0) == 0)
        def _():
            dw_ref[...] = jnp.zeros_like(dw_ref)
            dsc_ref[...] = jnp.zeros_like(dsc_ref)

        _pool_diff(x_ref, halo_ref, diff_ref, pl.program_id(0), tq)
        for g in range(len(POOL_SIZES)):
            cols = slice(g * C_GROUP, (g + 1) * C_GROUP)
            d = diff_ref[:, cols]
            wg = _grp_weight(w_ref, g)
            half, rows = _grp_rows(g)
            m0 = _dot(d, wg)
            z = z_ref[:, cols].astype(F32)
            dyv = dy_ref[:, cols].astype(F32)
            sc = sc_ref[:, cols]
            sg = _sigmoid(z)
            dmixed = dyv * (z * sg)
            dz_ref[:, cols] = (dyv * m0 * sc * (sg * (1.0 + z * (1.0 - sg)))).astype(dz_ref.dtype)
            dsc_ref[:, cols] += jnp.sum(dmixed * m0, axis=0, keepdims=True)
            dm0 = (dmixed * sc).astype(CDT)
            dwg = _dot(d, dm0, TN)
            for k in range(N_CHIPS):
                dw_ref[k, half, rows, :] += dwg[k * GRP_ROWS:(k + 1) * GRP_ROWS]
            dd_ref[:, cols] = _dot(dm0, wg, NT)

    blk = pl.BlockSpec((tq, C_WIDTH), lambda i: (i, 0))
    wsp = pl.BlockSpec(GRP_SHARD, lambda i: (0, 0, 0, 0))
    vec = pl.BlockSpec((1, C_WIDTH), lambda i: (0, 0))
    return pl.pallas_call(
        body, name=name, grid=(s // tq,),
        in_specs=[blk, pl.BlockSpec((POOL_BLOCK, C_WIDTH), lambda i: (jnp.maximum(i * per - 1, 0), 0)),
                  pl.BlockSpec((tq, C_WIDTH), lambda i: (i, 1)), wsp, vec, blk],
        out_specs=[blk, blk, wsp, vec],
        out_shape=[jax.ShapeDtypeStruct((s, C_WIDTH), F32), jax.ShapeDtypeStruct((s, C_WIDTH), CDT),
                   jax.ShapeDtypeStruct(GRP_SHARD, F32), jax.ShapeDtypeStruct((1, C_WIDTH), F32)],
        scratch_shapes=[pltpu.VMEM((tq, C_WIDTH), CDT)], compiler_params=_cparams("arbitrary"),
    )(proj, proj, proj, w_grp, scale, dy)


def _c_pool_bwd(ddiff, dz, name, tq=512):
    s = ddiff.shape[0]
    per = tq // POOL_BLOCK
    last = s // tq - 1

    def body(d_ref, halo_ref, dz_ref, o_ref):
        i = pl.program_id(0)
        t = i * tq + lax.broadcasted_iota(jnp.int32, (tq, 1), 0)
        for g, window in enumerate(POOL_SIZES):
            cols = slice(g * C_GROUP, (g + 1) * C_GROUP)
            band = _band(window, forward=True)
            inv = _inv_count(t, window)
            for b in range(tq // POOL_BLOCK):
                rows = slice(b * POOL_BLOCK, (b + 1) * POOL_BLOCK)
                cur = d_ref[rows, cols]
                if b == tq // POOL_BLOCK - 1:
                    below = jnp.where(i < last, halo_ref[:, cols] * (1.0 / window), 0.0)
                else:
                    nxt = slice((b + 1) * POOL_BLOCK, (b + 2) * POOL_BLOCK)
                    below = d_ref[nxt, cols] * inv[nxt]
                summed = _split_dot(band, jnp.concatenate([cur * inv[rows], below], axis=0))
                o_ref[rows, cols] = (summed - cur).astype(o_ref.dtype)
        o_ref[:, C_WIDTH:] = dz_ref[...]

    return pl.pallas_call(
        body, name=name, grid=(s // tq,),
        in_specs=[pl.BlockSpec((tq, C_WIDTH), lambda i: (i, 0)),
                  pl.BlockSpec((POOL_BLOCK, C_WIDTH), lambda i: (jnp.minimum((i + 1) * per, s // POOL_BLOCK - 1), 0)),
                  pl.BlockSpec((tq, C_WIDTH), lambda i: (i, 0))],
        out_specs=pl.BlockSpec((tq, 2 * C_WIDTH), lambda i: (i, 0)),
        out_shape=jax.ShapeDtypeStruct((s, 2 * C_WIDTH), CDT),
        compiler_params=_cparams("parallel"),
    )(ddiff, ddiff, dz)


LAYERS = (("a", 0), ("b", 0), ("c", 0), ("a", 1))


def _shard_spec(block, layer, where):
    if layer is None:
        return pl.BlockSpec((None,) + block, where)
    return pl.BlockSpec((None, None) + block, lambda i, j, q: (where(i, j, q)[0], layer) + where(i, j, q)[1:])


def _w_in_fwd(h, gw, layer, name):
    s, width = h.shape[0], gw.shape[-1]
    rows = 2048 if width <= 1024 else 1024
    return _mm(h, gw, "nn", CDT, name, (s, N_CHIPS * width, D_MODEL), (rows, width, D_MODEL),
               b_spec=_shard_spec((D_MODEL, width), layer, lambda i, j, q: (j, 0, 0)))


def _w_in_dh(dproj, gw, x, dxo, gain, name):
    s, width = dproj.shape[0], gw.shape[-1]
    tm = 1024 if width <= 1024 else 512

    def body(a_ref, b_ref, x_ref, dxo_ref, g_ref, dx_ref, dxc_ref, dg_ref, acc_ref):
        i, q = pl.program_id(0), pl.program_id(1)
        p = _dot(a_ref[...], b_ref[...], NT)

        @pl.when(q == 0)
        def _():
            acc_ref[...] = p

        @pl.when(q > 0)
        def _():
            acc_ref[...] += p

        @pl.when((i == 0) & (q == 0))
        def _():
            dg_ref[...] = jnp.zeros_like(dg_ref)

        @pl.when(q == N_CHIPS - 1)
        def _():
            dh = acc_ref[...]
            xv = x_ref[...]
            r = lax.rsqrt(jnp.mean(xv * xv, axis=-1, keepdims=True) + EPS)
            gd = dh * g_ref[...]
            dx = dxo_ref[...] + r * gd - xv * (r * r * r) * jnp.mean(gd * xv, axis=-1, keepdims=True)
            dx_ref[...] = dx
            dxc_ref[...] = dx.astype(dxc_ref.dtype)
            dg_ref[...] += jnp.sum(dh * xv * r, axis=0, keepdims=True)

    blk = pl.BlockSpec((tm, D_MODEL), lambda i, q: (i, 0))
    vec = pl.BlockSpec((1, D_MODEL), lambda i, q: (0, 0))
    return pl.pallas_call(
        body, name=name, grid=(s // tm, N_CHIPS),
        in_specs=[pl.BlockSpec((tm, width), lambda i, q: (i, q)), pl.BlockSpec((None, D_MODEL, width), lambda i, q: (q, 0, 0)),
                  blk, blk, vec],
        out_specs=[blk, blk, vec],
        out_shape=[jax.ShapeDtypeStruct((s, D_MODEL), F32), jax.ShapeDtypeStruct((s, D_MODEL), CDT), jax.ShapeDtypeStruct((1, D_MODEL), F32)],
        scratch_shapes=[pltpu.VMEM((tm, D_MODEL), F32)], compiler_params=_cparams("arbitrary", "arbitrary"),
    )(dproj, gw, x, dxo, gain)


def _w_in_grad(h, dproj, width, name):
    s = h.shape[0]
    tokens = 2048 if width <= 1536 else 1024
    return _mm(h, dproj, "tn", CDT, name, (D_MODEL, N_CHIPS * width, s), (D_MODEL, width, tokens),
               o_spec=pl.BlockSpec((None, D_MODEL, width), lambda i, j, q: (j, 0, 0)), o_shape=(N_CHIPS, D_MODEL, width))


def _w_out_spec(gw, layer):
    rows = gw.shape[-2]
    if layer is None:
        return pl.BlockSpec((N_CHIPS, rows, D_MODEL), lambda i: (0, 0, 0))
    return pl.BlockSpec((N_CHIPS, None, rows, D_MODEL), lambda i: (0, layer, 0, 0))


def _w_out_fwd(y, gw, layer, x, name, tm=1024):
    s, k = y.shape

    def body(y_ref, w_ref, x_ref, o_ref):
        o_ref[...] = x_ref[...] + _dot(y_ref[...], w_ref[...].reshape(k, D_MODEL))

    blk = pl.BlockSpec((tm, D_MODEL), lambda i: (i, 0))
    return pl.pallas_call(
        body, name=name, grid=(s // tm,), in_specs=[pl.BlockSpec((tm, k), lambda i: (i, 0)), _w_out_spec(gw, layer), blk],
        out_specs=blk, out_shape=jax.ShapeDtypeStruct((s, D_MODEL), F32), compiler_params=_cparams("parallel"),
    )(y, gw, x)


def _w_out_dy(dxc, gw, layer, name, tm=1024):
    s, k = dxc.shape[0], N_CHIPS * gw.shape[-2]
    tm = min(tm, s)

    def body(dx_ref, w_ref, o_ref):
        o_ref[...] = _dot(dx_ref[...], w_ref[...].reshape(k, D_MODEL), NT).astype(o_ref.dtype)

    return pl.pallas_call(
        body, name=name, grid=(s // tm,), in_specs=[pl.BlockSpec((tm, D_MODEL), lambda i: (i, 0)), _w_out_spec(gw, layer)],
        out_specs=pl.BlockSpec((tm, k), lambda i: (i, 0)), out_shape=jax.ShapeDtypeStruct((s, k), CDT),
        compiler_params=_cparams("parallel"),
    )(dxc, gw)


def _w_out_grad(y, dxc, rows, name):
    s, k = y.shape
    tokens = min(2048 if k <= 1024 else 1024, s)
    steps = s // tokens

    def body(y_ref, dx_ref, o_ref, acc_ref):
        i = pl.program_id(0)
        p = _dot(y_ref[...], dx_ref[...], TN)

        @pl.when(i == 0)
        def _():
            acc_ref[...] = p

        @pl.when(i > 0)
        def _():
            acc_ref[...] += p

        @pl.when(i == steps - 1)
        def _():
            o_ref[...] = acc_ref[...].reshape(N_CHIPS, rows, D_MODEL).astype(o_ref.dtype)

    return pl.pallas_call(
        body, name=name, grid=(steps,),
        in_specs=[pl.BlockSpec((tokens, k), lambda i: (i, 0)), pl.BlockSpec((tokens, D_MODEL), lambda i: (i, 0))],
        out_specs=pl.BlockSpec((N_CHIPS, rows, D_MODEL), lambda i: (0, 0, 0)),
        out_shape=jax.ShapeDtypeStruct((N_CHIPS, rows, D_MODEL), CDT),
        scratch_shapes=[pltpu.VMEM((k, D_MODEL), F32)], compiler_params=_cparams("arbitrary"),
    )(y, dxc)


def _local_step(x, target, w, on_grads=None, layer_weights=None):
    s = x.shape[0]
    tabs = _rope_tables(s)
    qk_gains = jnp.concatenate([w["b_q_gain"][0], w["b_k_gain"][0]], axis=0)
    saved = []
    for li, (kind, j) in enumerate(LAYERS):
        if layer_weights is not None:
            w = {**w, **layer_weights(li, x)}
        h = _rms_fwd(x, w["norm_gain"][li:li + 1], f"rms_fwd{li}")
        if kind == "a":
            proj = _w_in_fwd(h, w[f"a_w_in{j}"], None, f"a_in{li}")
            bs_t = w["a_b_s"][j].T
            y = _a_mid_fwd(proj, w["a_v_gain"][j:j + 1], w["a_w_s"][j], bs_t, f"a_mid_fwd{li}")
            x_next = _w_out_fwd(y, w[f"a_w_out{j}"], None, x, f"a_out{li}")
            saved.append((x, h, proj, y))
        elif kind == "b":
            proj = _w_in_fwd(h, w["b_w_in"], None, f"b_in{li}")
            qk = _b_qk_fwd(proj, qk_gains, tabs, f"b_qk_fwd{li}")
            qkv, os_, lses = [], [], []
            for g, dil in enumerate(B_DILATIONS):
                if dil == 1:
                    ops = (qk, qk, proj, (g, 3 + g, 6 + g))
                else:
                    ops = (_to_view(qk, g, 1024, dil, f"b_q_view{li}_{g}"), _to_view(qk, 3 + g, 1024, dil, f"b_k_view{li}_{g}"),
                           _to_view(proj, 6 + g, 1024, dil, f"b_v_view{li}_{g}"), (0, 0, 0))
                o, lse = _b_attn_fwd(*ops, dil, f"b_attn_fwd{li}_{g}")
                if dil > 1:
                    o, lse = _from_view(o, dil, f"b_o_nat{li}_{g}"), _from_view(lse, dil, f"b_lse_nat{li}_{g}")
                qkv.append(ops)
                os_.append(o)
                lses.append(lse)
            y, oj, lj = _b_combine(os_, lses, proj, f"b_combine{li}")
            x_next = _w_out_fwd(y, w["b_w_out"], None, x, f"b_out{li}")
            saved.append((x, h, proj, y, qkv, oj, lj))
        else:
            proj = _w_in_fwd(h, w["c_w_in"], None, f"c_in{li}")
            y = _c_mid_fwd(proj, w["c_w_grp"], w["c_scale"][j:j + 1], f"c_mid_fwd{li}")
            x_next = _w_out_fwd(y, w["c_w_out"], None, x, f"c_out{li}")
            saved.append((x, h, proj, y))
        x = x_next

    dx, dxc, sq = _loss_bwd(x, target, "loss_bwd")
    grads = {"norm_gain": [None] * len(LAYERS), "a_w_in": [None, None], "a_v_gain": [None, None], "a_w_s": [None, None],
             "a_b_s": [None, None], "a_w_out": [None, None]}
    for li in reversed(range(len(LAYERS))):
        kind, j = LAYERS[li]
        sv = saved[li]
        xin, h, proj, y = sv[:4]
        if kind == "a":
            grads["a_w_out"][j] = _w_out_grad(y, dxc, 512, f"a_dwout{li}")
            dy = _w_out_dy(dxc, w[f"a_w_out{j}"], None, f"a_dy{li}")
            dproj, dws, dbs_t, dvg = _a_mid_bwd(proj, dy, w["a_v_gain"][j:j + 1], w["a_w_s"][j], w["a_b_s"][j].T, f"a_mid_bwd{li}")
            grads["a_w_s"][j], grads["a_b_s"][j], grads["a_v_gain"][j] = dws, dbs_t.T, dvg[0]
            grads["a_w_in"][j] = _w_in_grad(h, dproj, 1536, f"a_dwin{li}")
            w_in = w[f"a_w_in{j}"]
        elif kind == "b":
            qkv, oj, lj = sv[4:]
            grads["b_w_out"] = _w_out_grad(y, dxc, 256, f"b_dwout{li}")
            dy = _w_out_dy(dxc, w["b_w_out"], None, f"b_dy{li}")
            do, dz, dd = _b_bwd_prep(dy, oj, proj, f"b_bwd_prep{li}")
            dqkv = []
            for g, dil in enumerate(B_DILATIONS):
                stats = (do, lj, dd)
                if dil > 1:
                    stats = (_to_view(do, 0, 1024, dil, f"b_do_view{li}_{g}"), _to_view(lj, 0, HEAD_DIM, dil, f"b_lj_view{li}_{g}"),
                             _to_view(dd, 0, HEAD_DIM, dil, f"b_dd_view{li}_{g}"))
                d = _b_attn_bwd(*qkv[g], *stats, dil, f"b_attn_bwd{li}_{g}")
                dqkv.append(_from_view(d, dil, f"b_dqkv_nat{li}_{g}") if dil > 1 else d)
            dproj, dgains = _b_qk_bwd(proj, dqkv, dz, qk_gains, tabs, f"b_qk_bwd{li}")
            grads["b_q_gain"], grads["b_k_gain"] = dgains[None, :3], dgains[None, 3:]
            grads["b_w_in"] = _w_in_grad(h, dproj, 2560, f"b_dwin{li}")
            w_in = w["b_w_in"]
        else:
            grads["c_w_out"] = _w_out_grad(y, dxc, 512, f"c_dwout{li}")
            dy = _w_out_dy(dxc, w["c_w_out"], None, f"c_dy{li}")
            ddiff, dz, dwg, dsc = _c_mid_bwd(proj, w["c_w_grp"], w["c_scale"][j:j + 1], dy, f"c_mid_bwd{li}")
            grads["c_w_grp"], grads["c_scale"] = dwg, dsc
            dproj = _c_pool_bwd(ddiff, dz, f"c_pool_bwd{li}")
            grads["c_w_in"] = _w_in_grad(h, dproj, 1024, f"c_dwin{li}")
            w_in = w["c_w_in"]
        gain = w["norm_gain"][li:li + 1]
        if on_grads is not None:
            gain = gain + on_grads(li, grads)
        dx, dxc, dng = _w_in_dh(dproj, w_in, xin, dx, gain, f"{kind}_dh{li}")
        grads["norm_gain"][li] = dng[0]
    for name in ("norm_gain", "a_v_gain", "a_w_s", "a_b_s"):
        grads[name] = jnp.stack(grads[name])
    return sq, dx, grads


ROW = 1024
REDUCE_UNITS = (("a_w_in", 0, 512, 1536), ("a_w_in", 1, 512, 1536), ("a_w_out", 0, 256, 1024), ("a_w_out", 1, 256, 1024),
                ("b_w_in", 0, 512, 2560), ("b_w_out", 0, 128, 1024), ("c_w_in", 0, 512, 1024), ("c_w_grp", 0, 256, 512),
                ("c_w_out", 0, 256, 1024))
GAIN_ROWS = 16
GAINS = len(REDUCE_UNITS)
LAYER_UNITS = {0: (0, 2), 1: (4, 5), 2: (6, 7, 8), 3: (1, 3)}
EARLY_UNITS = LAYER_UNITS
FIRST_UNITS = LAYER_UNITS[0] + (GAINS,)
LATER_UNITS = {li: LAYER_UNITS[li] for li in (1, 2, 3)}
SMALL_OF_HALF = ("a_v_gain", "c_scale")
SMALL_PAD = 16
REP = (("norm_gain", (4, 1024)), ("a_w_s", (2, 8, 128, 128)), ("a_b_s", (2, 8, 128)), ("b_q_gain", (1, 3, 128)), ("b_k_gain", (1, 3, 128)))
REP_CORE = 48
REP_PART = 2 * REP_CORE
SMALL_ROWS = SMALL_PAD + REP_CORE
BLOCK_SHAPES = {"a_w_in": (2, 1024, 1536), "a_v_gain": (2, 512), "a_w_out": (2, 512, 1024), "b_w_in": (1, 1024, 2560),
                "b_w_out": (1, 256, 1024), "c_w_in": (1, 1024, 1024), "c_w_grp": (1, 4, 128, 512), "c_scale": (1, 512),
                "c_w_out": (1, 512, 1024)}
WEIGHTS = ("norm_gain", "a_w_in", "a_v_gain", "a_w_s", "a_b_s", "a_w_out", "b_w_in", "b_q_gain", "b_k_gain", "b_w_out",
           "c_w_in", "c_w_grp", "c_scale", "c_w_out")


def _rows(a, rows):
    a = a.reshape(-1)
    return jnp.pad(a, (0, rows * ROW - a.shape[0])).reshape(rows, ROW)


def _small_unit(vecs, rep_part, dtype):
    halves = [jnp.concatenate([_rows(vecs[h].astype(dtype), SMALL_PAD), rep_part[h * REP_CORE:(h + 1) * REP_CORE].astype(dtype)])
              for h in range(2)]
    return jnp.stack(halves)


def _pack_rep(tree):
    return _rows(jnp.concatenate([tree[n].astype(F32).reshape(-1) for n, _ in REP]), N_CHIPS * REP_PART)


def _unpack_rep(slab):
    flat, out, off = slab.reshape(-1), {}, 0
    for n, shape in REP:
        size = math.prod(shape)
        out[n] = flat[off:off + size].reshape(shape)
        off += size
    return out


def _into_slot(x, first, rows, me, name):
    width = x.shape[1]
    tr = min(rows, 256)

    def body(me_ref, x_ref, o_ref):
        o_ref[...] = x_ref[...].astype(o_ref.dtype)

    return pl.pallas_call(
        body, name=name,
        grid_spec=pltpu.PrefetchScalarGridSpec(
            num_scalar_prefetch=1, grid=(rows // tr,), in_specs=[pl.BlockSpec((tr, width), lambda i, me_ref: (first // tr + i, 0))],
            out_specs=pl.BlockSpec((None, tr, width), lambda i, me_ref: (me_ref[0], i, 0))),
        out_shape=jax.ShapeDtypeStruct((N_CHIPS, rows, width), CDT), compiler_params=_cparams("parallel"),
    )(me.reshape(1), x)


def _gather_inputs(wts, me):
    units = []
    for u, (n, layer, r, w) in enumerate(REDUCE_UNITS):
        slot = _into_slot(wts[n].reshape(-1, w), layer * 2 * r, 2 * r, me, f"slot{u}")
        units.append(slot.reshape(N_CHIPS, 2, r, w))
    gains = jnp.concatenate([wts["a_v_gain"].reshape(-1), wts["c_scale"].reshape(-1)])
    gains = _rows(lax.bitcast_convert_type(gains, CDT), 2 * GAIN_ROWS)
    return units + [_into_slot(gains, 0, 2 * GAIN_ROWS, me, "slot_gains").reshape(N_CHIPS, 2, GAIN_ROWS, ROW)]


def _gathered_weights(indices, units):
    out = {}
    for u, arr in zip(indices, units):
        if u == GAINS:
            gains = lax.bitcast_convert_type(arr[:, 0, :3].reshape(N_CHIPS, 1536, 2), F32)
            out["a_v_gain"] = jnp.concatenate([gains[k, :1024].reshape(2, 512) for k in range(N_CHIPS)], axis=1)
            out["c_scale"] = jnp.concatenate([gains[k, 1024:].reshape(1, 512) for k in range(N_CHIPS)], axis=1)
            continue
        n, layer, r, w = REDUCE_UNITS[u]
        if n == "c_w_grp":
            out[n] = arr
        else:
            out[n + str(layer) if n in ("a_w_in", "a_w_out") else n] = arr.reshape(N_CHIPS, 2 * r, w)
    return out


def _place():
    x, y, c = lax.axis_index("x"), lax.axis_index("y"), lax.axis_index("c")
    chips = [(1 - x, y), (x, 1 - y), (1 - x, 1 - y)]
    return x, y, c, 2 * x + y, (x, y, 1 - c), chips


def _remote(src, dst, sems, j, to):
    send_sems, recv_sems = sems
    return pltpu.make_async_remote_copy(src_ref=src, dst_ref=dst, send_sem=send_sems.at[j], recv_sem=recv_sems.at[j],
                                        device_id=to, device_id_type=MESH)


def _comm_call_multi(body, name, out_shapes, n_sems, args, aliases=None):
    return pl.pallas_call(
        body, name=name, in_specs=[ANY] * len(args), out_specs=[ANY] * len(out_shapes), out_shape=out_shapes,
        scratch_shapes=[pltpu.SemaphoreType.DMA((n_sems,)), pltpu.SemaphoreType.DMA((n_sems,))],
        input_output_aliases=aliases or {},
    )(*args)


def _allgather_units(units):
    n = len(units)

    def body(*refs):
        outs, sems = refs[n:2 * n], (refs[2 * n], refs[2 * n + 1])
        x, y, c, me, sibling, chips = _place()
        first, passed = [], []
        for u, o_ref in enumerate(outs):
            for j, chip in enumerate(chips):
                first.append(_remote(o_ref.at[me, c], o_ref.at[me, c], sems, 6 * u + j, (*chip, c)))
                first[-1].start()
        for u, o_ref in enumerate(outs):
            for j, (cx, cy) in enumerate(chips):
                landed = o_ref.at[2 * cx + cy, c]
                _remote(landed, landed, sems, 6 * u + j, sibling).wait_recv()
                passed.append(_remote(landed, landed, sems, 6 * u + 3 + j, sibling))
                passed[-1].start()
        for u, o_ref in enumerate(outs):
            for j, (cx, cy) in enumerate(chips):
                landed = o_ref.at[2 * cx + cy, 1 - c]
                _remote(landed, landed, sems, 6 * u + 3 + j, sibling).wait_recv()
        for cp in first + passed:
            cp.wait_send()

    return _comm_call_multi(body, "allgather_first", [jax.ShapeDtypeStruct(u.shape, u.dtype) for u in units], 6 * n, units,
                            aliases={u: u for u in range(n)})


HBM = pl.BlockSpec(memory_space=pltpu.HBM)
SEM = pl.BlockSpec(memory_space=pltpu.SEMAPHORE)
SIDE_EFFECT = pltpu.SideEffectType.DATAFLOW_SIDE_EFFECTING


def _gather_start(groups, after):
    sizes = [len(g) for g in groups]
    units = [u for g in groups for u in g]
    n = len(units)

    def body(*refs):
        arrs, sems, token = refs[:n], refs[n + 1:n + 1 + 2 * len(groups)], refs[-1]
        x, y, c, me, sibling, chips = _place()
        at = 0
        for gi, size in enumerate(sizes):
            for u in range(size):
                mine = arrs[at + u].at[me, c]
                for j, chip in enumerate(chips):
                    _remote(mine, mine, (sems[2 * gi], sems[2 * gi + 1]), 3 * u + j, (*chip, c)).start()
            at += size
        token[...] = jnp.zeros_like(token)

    sem_shapes = [pltpu.SemaphoreType.DMA((3 * size,)) for size in sizes for _ in range(2)]
    outs = pl.pallas_call(
        body, name="gather_start",
        out_shape=(*sem_shapes, *[pltpu.HBM(u.shape, u.dtype) for u in units], jax.ShapeDtypeStruct((8, 128), F32)),
        in_specs=[HBM] * n + [ANY], out_specs=(*[SEM] * len(sem_shapes), *[HBM] * n, pl.BlockSpec(memory_space=pltpu.VMEM)),
        input_output_aliases={i: len(sem_shapes) + i for i in range(n)},
        compiler_params=pltpu.CompilerParams(has_side_effects=SIDE_EFFECT),
    )(*[pltpu.with_memory_space_constraint(u, pltpu.HBM) for u in units], after)
    sems, arrs = outs[:len(sem_shapes)], outs[len(sem_shapes):-1]
    bounds = [sum(sizes[:gi]) for gi in range(len(sizes) + 1)]
    return ([(sems[2 * gi], sems[2 * gi + 1]) for gi in range(len(sizes))],
            [list(arrs[bounds[gi]:bounds[gi + 1]]) for gi in range(len(sizes))], outs[-1])


def _gather_wait(units, sems, after, name):
    n = len(units)

    def body(*refs):
        arrs, send_sems, recv_sems = refs[:n], refs[n], refs[n + 1]
        x, y, c, me, sibling, chips = _place()
        for u in range(n):
            for j, (cx, cy) in enumerate(chips):
                cp = _remote(arrs[u].at[me, c], arrs[u].at[2 * cx + cy, c], (send_sems, recv_sems), 3 * u + j, (cx, cy, c))
                cp.wait_send()
                cp.wait_recv()

    outs = pl.pallas_call(
        body, name=name, out_shape=tuple(pltpu.HBM(u.shape, u.dtype) for u in units),
        in_specs=[HBM] * n + [SEM, SEM, ANY], out_specs=[HBM] * n, input_output_aliases={i: i for i in range(n)},
        compiler_params=pltpu.CompilerParams(has_side_effects=SIDE_EFFECT),
    )(*units, *sems, after)
    return list(outs)


def _forward_units(units, name):
    n = len(units)

    def body(*refs):
        outs, sems = refs[n:2 * n], (refs[2 * n], refs[2 * n + 1])
        x, y, c, me, sibling, chips = _place()
        passed = []
        for u, o_ref in enumerate(outs):
            for j, (cx, cy) in enumerate(chips):
                landed = o_ref.at[2 * cx + cy, c]
                passed.append(_remote(landed, landed, sems, 3 * u + j, sibling))
                passed[-1].start()
        for u, o_ref in enumerate(outs):
            for j, (cx, cy) in enumerate(chips):
                landed = o_ref.at[2 * cx + cy, 1 - c]
                _remote(landed, landed, sems, 3 * u + j, sibling).wait_recv()
        for cp in passed:
            cp.wait_send()

    return _comm_call_multi(body, name, [jax.ShapeDtypeStruct(u.shape, u.dtype) for u in units], 3 * n, units,
                            aliases={u: u for u in range(n)})


def _swap_halves(units, name):
    n = len(units)

    def body(*refs):
        ins, outs, sems = refs[:n], refs[n:2 * n], (refs[2 * n], refs[2 * n + 1])
        x, y, c, me, sibling, chips = _place()
        sent = [_remote(g_ref.at[:, 1 - c], o_ref, sems, u, sibling) for u, (g_ref, o_ref) in enumerate(zip(ins, outs))]
        for cp in sent:
            cp.start()
        for cp in sent:
            cp.wait()

    shapes = [jax.ShapeDtypeStruct((N_CHIPS,) + u.shape[2:], u.dtype) for u in units]
    return _comm_call_multi(body, name, shapes, n, units)


def _scatter_shards(units):
    n = len(units)

    def body(*refs):
        ins, outs, sems = refs[:n], refs[n:2 * n], (refs[2 * n], refs[2 * n + 1])
        x, y, c, me, sibling, chips = _place()
        sent = []
        for u, (s_ref, o_ref) in enumerate(zip(ins, outs)):
            for j, (cx, cy) in enumerate(chips):
                sent.append(_remote(s_ref.at[2 * cx + cy], o_ref.at[me], sems, 3 * u + j, (cx, cy, c)))
                sent[-1].start()
        for u, o_ref in enumerate(outs):
            for j, (cx, cy) in enumerate(chips):
                slot = o_ref.at[2 * cx + cy]
                _remote(slot, slot, sems, 3 * u + j, sibling).wait_recv()
        for cp in sent:
            cp.wait_send()

    return _comm_call_multi(body, "grad_scatter_shards", [jax.ShapeDtypeStruct(u.shape, u.dtype) for u in units], 3 * n, units)


def _scatter_start(units, name):
    n = len(units)

    def body(*refs):
        srcs, lands = refs[:n], refs[n:2 * n]
        send_sems, recv_sems, token = refs[2 * n], refs[2 * n + 1], refs[-1]
        x, y, c, me, sibling, chips = _place()
        for u in range(n):
            for j, (cx, cy) in enumerate(chips):
                _remote(srcs[u].at[2 * cx + cy], lands[u].at[me], (send_sems, recv_sems), 3 * u + j, (cx, cy, c)).start()
        token[...] = jnp.zeros_like(token)

    hbm = [pltpu.HBM(u.shape, u.dtype) for u in units]
    outs = pl.pallas_call(
        body, name=name,
        out_shape=(pltpu.SemaphoreType.DMA((3 * n,)), pltpu.SemaphoreType.DMA((3 * n,)), *hbm, *hbm, jax.ShapeDtypeStruct((8, 128), F32)),
        in_specs=[HBM] * (2 * n), out_specs=(SEM, SEM, *[HBM] * (2 * n), pl.BlockSpec(memory_space=pltpu.VMEM)),
        input_output_aliases={i: 2 + i for i in range(2 * n)},
        compiler_params=pltpu.CompilerParams(has_side_effects=SIDE_EFFECT),
    )(*[pltpu.with_memory_space_constraint(u, pltpu.HBM) for u in units],
      *[pltpu.with_memory_space_constraint(lax.empty(u.shape, u.dtype), pltpu.HBM) for u in units])
    return outs[0], outs[1], outs[2:2 + n], outs[2 + n:2 + 2 * n], outs[-1]


def _scatter_wait(send_sems, recv_sems, srcs, lands, after, name):
    n = len(srcs)

    def body(*refs):
        srcs_, lands_, send_sems_, recv_sems_ = refs[:n], refs[n:2 * n], refs[2 * n], refs[2 * n + 1]
        x, y, c, me, sibling, chips = _place()
        for u in range(n):
            for j, (cx, cy) in enumerate(chips):
                slot = lands_[u].at[2 * cx + cy]
                cp = _remote(srcs_[u].at[2 * cx + cy], slot, (send_sems_, recv_sems_), 3 * u + j, (cx, cy, c))
                cp.wait_send()
                cp.wait_recv()

    hbm = [pltpu.HBM(u.shape, u.dtype) for u in srcs]
    outs = pl.pallas_call(
        body, name=name, out_shape=(*hbm, *hbm),
        in_specs=[HBM] * (2 * n) + [SEM, SEM, ANY], out_specs=[HBM] * (2 * n),
        input_output_aliases={i: i for i in range(2 * n)},
        compiler_params=pltpu.CompilerParams(has_side_effects=SIDE_EFFECT),
    )(*srcs, *lands, send_sems, recv_sems, after)
    return list(outs[n:])


def _share_halves(units, name):
    n = len(units)

    def body(*refs):
        outs, sems = refs[n:2 * n], (refs[2 * n], refs[2 * n + 1])
        x, y, c, me, sibling, chips = _place()
        sent = [_remote(o_ref.at[:, :, c], o_ref.at[:, :, c], sems, u, sibling) for u, o_ref in enumerate(outs)]
        for cp in sent:
            cp.start()
        for u, o_ref in enumerate(outs):
            theirs = o_ref.at[:, :, 1 - c]
            _remote(theirs, theirs, sems, u, sibling).wait_recv()
        for cp in sent:
            cp.wait_send()

    return _comm_call_multi(body, name, [jax.ShapeDtypeStruct(u.shape, u.dtype) for u in units], n, units,
                            aliases={u: u for u in range(n)})


def _share_start(units, name):
    n = len(units)

    def body(*refs):
        arrs, send_sems, recv_sems, token = refs[:n], refs[n], refs[n + 1], refs[-1]
        x, y, c, me, sibling, chips = _place()
        for u in range(n):
            _remote(arrs[u].at[:, :, c], arrs[u].at[:, :, c], (send_sems, recv_sems), u, sibling).start()
        token[...] = jnp.zeros_like(token)

    hbm = [pltpu.HBM(u.shape, u.dtype) for u in units]
    outs = pl.pallas_call(
        body, name=name,
        out_shape=(pltpu.SemaphoreType.DMA((n,)), pltpu.SemaphoreType.DMA((n,)), *hbm, jax.ShapeDtypeStruct((8, 128), F32)),
        in_specs=[HBM] * n, out_specs=(SEM, SEM, *[HBM] * n, pl.BlockSpec(memory_space=pltpu.VMEM)),
        input_output_aliases={i: 2 + i for i in range(n)},
        compiler_params=pltpu.CompilerParams(has_side_effects=SIDE_EFFECT),
    )(*[pltpu.with_memory_space_constraint(u, pltpu.HBM) for u in units])
    return outs[0], outs[1], list(outs[2:2 + n]), outs[-1]


def _share_wait(send_sems, recv_sems, units, after, name):
    n = len(units)

    def body(*refs):
        arrs, send_sems_, recv_sems_ = refs[:n], refs[n], refs[n + 1]
        x, y, c, me, sibling, chips = _place()
        for u in range(n):
            cp = _remote(arrs[u].at[:, :, c], arrs[u].at[:, :, 1 - c], (send_sems_, recv_sems_), u, sibling)
            cp.wait_send()
            cp.wait_recv()

    outs = pl.pallas_call(
        body, name=name, out_shape=tuple(pltpu.HBM(u.shape, u.dtype) for u in units),
        in_specs=[HBM] * n + [SEM, SEM, ANY], out_specs=[HBM] * n, input_output_aliases={i: i for i in range(n)},
        compiler_params=pltpu.CompilerParams(has_side_effects=SIDE_EFFECT),
    )(*units, send_sems, recv_sems, after)
    return list(outs)


def _rep_start(rep4):
    def body(r_ref, land_ref, send_sems, recv_sems, r_thru, land_thru):
        x, y, c, me, sibling, chips = _place()
        for j, chip in enumerate(chips):
            _remote(r_ref, land_ref.at[me], (send_sems, recv_sems), j, (*chip, c)).start()

    land = jax.ShapeDtypeStruct((N_CHIPS,) + rep4.shape, rep4.dtype)
    return pl.pallas_call(
        body, name="rep_start",
        out_shape=(pltpu.SemaphoreType.DMA((3,)), pltpu.SemaphoreType.DMA((3,)), pltpu.HBM(rep4.shape, rep4.dtype),
                   pltpu.HBM(land.shape, land.dtype)),
        in_specs=[HBM, HBM], out_specs=(SEM, SEM, HBM, HBM), input_output_aliases={0: 2, 1: 3},
        compiler_params=pltpu.CompilerParams(has_side_effects=SIDE_EFFECT),
    )(pltpu.with_memory_space_constraint(rep4, pltpu.HBM),
      pltpu.with_memory_space_constraint(lax.empty(land.shape, land.dtype), pltpu.HBM))


def _rep_wait(send_sems, recv_sems, rep4, land, after):
    def body(r_ref, land_ref, send_sems_, recv_sems_, after_ref, r_dead, land_out):
        x, y, c, me, sibling, chips = _place()
        for j, (cx, cy) in enumerate(chips):
            cp = _remote(r_ref, land_ref.at[2 * cx + cy], (send_sems_, recv_sems_), j, (cx, cy, c))
            cp.wait_send()
            cp.wait_recv()

    return pl.pallas_call(
        body, name="rep_wait", out_shape=(pltpu.HBM(rep4.shape, rep4.dtype), pltpu.HBM(land.shape, land.dtype)),
        in_specs=[HBM, HBM, SEM, SEM, ANY], out_specs=(HBM, HBM), input_output_aliases={0: 0, 1: 1},
        compiler_params=pltpu.CompilerParams(has_side_effects=SIDE_EFFECT),
    )(rep4, land, send_sems, recv_sems, after)[1]


ADAM_TILE = 64


def _add_halves(mine, theirs, place, name):
    rows, width = theirs.shape[1:]
    tr = min(rows, 256)

    def body(place_ref, a_ref, b_ref, o_ref):
        o_ref[...] = (a_ref[...].astype(F32) + b_ref[...].astype(F32)).astype(o_ref.dtype)

    blk = pl.BlockSpec((None, tr, width), lambda k, i, place_ref: (k, i, 0))
    return pl.pallas_call(
        body, name=name,
        grid_spec=pltpu.PrefetchScalarGridSpec(
            num_scalar_prefetch=1, grid=(N_CHIPS, rows // tr),
            in_specs=[pl.BlockSpec((None, None, tr, width), lambda k, i, place_ref: (k, place_ref[1], i, 0)), blk], out_specs=blk),
        out_shape=jax.ShapeDtypeStruct(theirs.shape, theirs.dtype), compiler_params=_cparams("parallel", "parallel"),
    )(place, mine, theirs)


def _adam_update(g, w_ref, m_ref, v_ref, o_ref):
    m2 = ADAM_B1 * m_ref[...] + (1.0 - ADAM_B1) * g
    v2 = ADAM_B2 * v_ref[...] + (1.0 - ADAM_B2) * jnp.square(g)
    m_hat = m2 / (1.0 - ADAM_B1 ** ADAM_STEP)
    v_hat = v2 / (1.0 - ADAM_B2 ** ADAM_STEP)
    o_ref[0] = g
    o_ref[1] = -ADAM_LR * (m_hat / (jnp.sqrt(v_hat) + ADAM_EPS) + ADAM_WD * w_ref[...])
    o_ref[2] = m2
    o_ref[3] = v2


def _adamw_rep(g, w, m, v):
    rows = g.shape[0]

    def body(g_ref, w_ref, m_ref, v_ref, o_ref):
        _adam_update(g_ref[...], w_ref, m_ref, v_ref, o_ref)

    blk = pl.BlockSpec((ADAM_TILE, ROW), lambda i: (i, 0))
    return pl.pallas_call(
        body, name="adamw_rep", grid=(rows // ADAM_TILE,), in_specs=[blk, blk, blk, blk],
        out_specs=pl.BlockSpec((4, ADAM_TILE, ROW), lambda i: (0, i, 0)),
        out_shape=jax.ShapeDtypeStruct((4, rows, ROW), F32), compiler_params=_cparams("parallel"),
    )(g, w, m, v)


def _adamw(place, parts, own, w, m, v, layer, name, into=None):
    rows, width = parts.shape[1:]
    layers = w.shape[0]

    def body(place_ref, p_ref, own_ref, w_ref, m_ref, v_ref, *rest):
        o_ref = rest[-1]
        g = jnp.zeros((ADAM_TILE, width), F32)
        for k in range(N_CHIPS):
            g = g + jnp.where(place_ref[0] == k, own_ref[...], p_ref[k]).astype(F32)
        _adam_update(g, w_ref, m_ref, v_ref, o_ref)

    blk = pl.BlockSpec((None, None, ADAM_TILE, width), lambda i, place_ref: (layer, place_ref[1], i, 0))
    blk4 = pl.BlockSpec((4, ADAM_TILE, width), lambda i, place_ref: (0, i, 0))
    mine = pl.BlockSpec((None, ADAM_TILE, width), lambda i, place_ref: (place_ref[0], i, 0))
    out = pl.BlockSpec((4, None, None, ADAM_TILE, width), lambda i, place_ref: (0, layer, place_ref[1], i, 0))
    extra = [] if into is None else [into]
    return pl.pallas_call(
        body, name=name,
        grid_spec=pltpu.PrefetchScalarGridSpec(num_scalar_prefetch=1, grid=(rows // ADAM_TILE,),
                                               in_specs=[blk4, mine, blk, blk, blk] + [ANY] * len(extra), out_specs=out),
        out_shape=jax.ShapeDtypeStruct((4, layers, 2, rows, width), F32),
        input_output_aliases={} if into is None else {6: 0}, compiler_params=_cparams("parallel"),
    )(place, parts, own, w, m, v, *extra)


def kernel(x, norm_gain, a_w_in, a_v_gain, a_w_s, a_b_s, a_w_out, b_w_in, b_q_gain, b_k_gain, b_w_out, c_w_in, c_w_grp, c_scale, c_w_out, loss_target, m_norm_gain, m_a_w_in, m_a_v_gain, m_a_w_s, m_a_b_s, m_a_w_out, m_b_w_in, m_b_q_gain, m_b_k_gain, m_b_w_out, m_c_w_in, m_c_w_grp, m_c_scale, m_c_w_out, v_norm_gain, v_a_w_in, v_a_v_gain, v_a_w_s, v_a_b_s, v_a_w_out, v_b_w_in, v_b_q_gain, v_b_k_gain, v_b_w_out, v_c_w_in, v_c_w_grp, v_c_scale, v_c_w_out):
    wts = dict(norm_gain=norm_gain, a_w_in=a_w_in, a_v_gain=a_v_gain, a_w_s=a_w_s, a_b_s=a_b_s, a_w_out=a_w_out, b_w_in=b_w_in,
               b_q_gain=b_q_gain, b_k_gain=b_k_gain, b_w_out=b_w_out, c_w_in=c_w_in, c_w_grp=c_w_grp, c_scale=c_scale, c_w_out=c_w_out)
    mom1 = dict(norm_gain=m_norm_gain, a_w_in=m_a_w_in, a_v_gain=m_a_v_gain, a_w_s=m_a_w_s, a_b_s=m_a_b_s, a_w_out=m_a_w_out,
                b_w_in=m_b_w_in, b_q_gain=m_b_q_gain, b_k_gain=m_b_k_gain, b_w_out=m_b_w_out, c_w_in=m_c_w_in, c_w_grp=m_c_w_grp,
                c_scale=m_c_scale, c_w_out=m_c_w_out)
    mom2 = dict(norm_gain=v_norm_gain, a_w_in=v_a_w_in, a_v_gain=v_a_v_gain, a_w_s=v_a_w_s, a_b_s=v_a_b_s, a_w_out=v_a_w_out,
                b_w_in=v_b_w_in, b_q_gain=v_b_q_gain, b_k_gain=v_b_k_gain, b_w_out=v_b_w_out, c_w_in=v_c_w_in, c_w_grp=v_c_w_grp,
                c_scale=v_c_scale, c_w_out=v_c_w_out)
    axes = ("x", "y", "c")
    me = 2 * lax.axis_index("x") + lax.axis_index("y")
    core = lax.axis_index("c")

    place = jnp.stack([me, core]).astype(jnp.int32)

    slots = _gather_inputs(wts, me)
    first = _allgather_units([slots[u] for u in FIRST_UNITS])
    full = _gathered_weights(FIRST_UNITS, first)
    later = sorted(LATER_UNITS)
    gather_sems, gather_units, token = _gather_start([[slots[u] for u in LATER_UNITS[li]] for li in later], first[-1])
    for n, _ in REP:
        full[n] = wts[n]
    full["norm_gain"] = wts["norm_gain"] + token[0, 0]

    def layer_weights(li, x_in):
        if li not in LATER_UNITS:
            return {}
        g = later.index(li)
        landed = _gather_wait(gather_units[g], gather_sems[g], x_in, f"gather_wait_l{li}")
        return _gathered_weights(LATER_UNITS[li], _forward_units(landed, f"gather_forward_l{li}"))

    def unit_grad(u, grads):
        n, layer, r, w = REDUCE_UNITS[u]
        g = grads[n][layer] if n in ("a_w_in", "a_w_out") else grads[n]
        return g.astype(CDT).reshape(N_CHIPS, 2, r, w)

    chip_sums, in_flight = {}, []

    def start_exchange(li, grads):
        if li not in EARLY_UNITS:
            return jnp.zeros((), F32)
        units = EARLY_UNITS[li]
        mine = [unit_grad(u, grads) for u in units]
        theirs = _swap_halves(mine, f"grad_swap_halves_l{li}")
        sums = [_add_halves(g, t, place, f"grad_add_halves{u}") for u, g, t in zip(units, mine, theirs)]
        chip_sums.update(zip(units, sums))
        send_sems, recv_sems, srcs, lands, token = _scatter_start(sums, f"grad_scatter_start_l{li}")
        in_flight.append((li, units, send_sems, recv_sems, srcs, lands))
        return token[0, 0]

    sq, grad_x, grads = _local_step(x[0], loss_target[0], full, start_exchange, layer_weights)
    loss = lax.psum(0.5 * jnp.sum(sq) / D_MODEL, axes)

    rep_g = _pack_rep(grads)
    late = [u for u in range(len(REDUCE_UNITS)) if not any(u in us for us in EARLY_UNITS.values())]
    mine = [unit_grad(u, grads) for u in late]
    mine.append(jnp.stack([
        _small_unit([lax.slice_in_dim(grads[v], 512 * k, 512 * (k + 1), axis=1) for v in SMALL_OF_HALF],
                    rep_g[k * REP_PART:(k + 1) * REP_PART], CDT) for k in range(N_CHIPS)]))
    late.append(len(REDUCE_UNITS))
    theirs = _swap_halves(mine, "grad_swap_halves_late")
    sums = [_add_halves(g, t, place, f"grad_add_halves{u}") for u, g, t in zip(late, mine, theirs)]
    chip_sums.update(zip(late, sums))
    parts = dict(zip(late, _scatter_shards(sums)))
    for li, units, send_sems, recv_sems, srcs, lands in in_flight:
        parts.update(zip(units, _scatter_wait(send_sems, recv_sems, srcs, lands, grad_x, f"grad_scatter_wait_l{li}")))
    order = range(len(REDUCE_UNITS) + 1)
    parts, chip_sums = [parts[u] for u in order], [chip_sums[u] for u in order]

    def state_units(tree):
        units = [tree[n].reshape(BLOCK_SHAPES[n][0], 2, r, w) for n, _, r, w in REDUCE_UNITS]
        rep_part = lax.dynamic_slice_in_dim(_pack_rep(tree), me * REP_PART, REP_PART, axis=0)
        return units + [_small_unit([tree[v] for v in SMALL_OF_HALF], rep_part, F32)[None]]

    names = [n for n, _, _, _ in REDUCE_UNITS] + ["small"]
    layers = [layer for _, layer, _, _ in REDUCE_UNITS] + [0]
    res, state = {}, list(zip(parts, chip_sums, state_units(wts), state_units(mom1), state_units(mom2)))

    def update(units):
        for u in units:
            res[names[u]] = _adamw(place, *state[u], layers[u], f"adamw{u}", into=res.get(names[u]))

    update((len(REDUCE_UNITS),))
    small = _share_halves([res.pop("small")], "share_small")[0][:, 0]
    rep_mine = jnp.concatenate([small[0, 0, SMALL_PAD:], small[0, 1, SMALL_PAD:]], axis=0)
    rep_sems = _rep_start(rep_mine)
    update(LAYER_UNITS[1] + LAYER_UNITS[2])
    early = list(res)
    send_sems, recv_sems, early_arrs, _ = _share_start([res[n] for n in early], "share_start")
    update(LAYER_UNITS[0] + LAYER_UNITS[3])
    rest = [n for n in res if n not in early]
    res.update(zip(rest, _share_halves([res[n] for n in rest], "share_halves")))
    res.update(zip(early, _share_wait(send_sems, recv_sems, early_arrs, res[rest[0]], "share_wait")))
    rep_g = lax.dynamic_update_slice_in_dim(_rep_wait(*rep_sems, res[rest[0]]), rep_mine[None], me, axis=0)
    rep_all = _adamw_rep(rep_g.reshape(N_CHIPS * REP_PART, ROW), _pack_rep(wts), _pack_rep(mom1), _pack_rep(mom2))

    outs = []
    for q in range(4):
        tree = {n: arr[q].reshape(BLOCK_SHAPES[n]) for n, arr in res.items()}
        tree["a_v_gain"] = small[q, 0, 0].reshape(2, 512)
        tree["c_scale"] = small[q, 1, 0, :512].reshape(1, 512)
        tree.update(_unpack_rep(rep_all[q]))
        outs.append(tree)
    return (loss, grad_x[None], *[t[n] for t in outs for n in WEIGHTS])
```

```python
import math

import jax
import jax.numpy as jnp
from jax import lax
from jax.experimental import pallas as pl
from jax.experimental.pallas import tpu as pltpu

F32 = jnp.float32
CDT = jnp.bfloat16

D_MODEL = 1024
EPS = 1e-6
CHUNK = 128
A_WIDTH = 2048
A_GROUPS = 8
A_GROUP_DIM = 256
HEAD_DIM = 128
B_HEADS = 8
B_DILATIONS = (1, 4, 16)
B_QK = 6144
B_IN = 10240
ROPE_HALF = 16
ROPE_THETA = 500000.0
POOL_SIZES = (2, 4, 8, 16)
POOL_HALO = 16
C_WIDTH = 2048
C_GROUP = 512
N_CHIPS = 4

ADAM_LR = 0.001
ADAM_B1 = 0.9
ADAM_B2 = 0.999
ADAM_EPS = 1e-08
ADAM_WD = 0.01
ADAM_STEP = 10

VMEM_LIMIT = 48 * 1024 * 1024
ANY = pl.BlockSpec(memory_space=pl.ANY)
MESH = pl.DeviceIdType.MESH

NN = (((1,), (0,)), ((), ()))
NT = (((1,), (1,)), ((), ()))
TN = (((0,), (0,)), ((), ()))


def _cparams(*sem):
    return pltpu.CompilerParams(dimension_semantics=sem, vmem_limit_bytes=VMEM_LIMIT)


def _dot(a, b, dims=NN):
    return lax.dot_general(a, b, dims, preferred_element_type=F32)


def _sigmoid(z):
    return 1.0 / (1.0 + jnp.exp(-z))


def _lane_sums(v):
    ones = jnp.ones((HEAD_DIM, HEAD_DIM), jnp.bfloat16)
    hi = v.astype(jnp.bfloat16)
    lo = (v - hi.astype(F32)).astype(jnp.bfloat16)
    return _dot(hi, ones) + _dot(lo, ones)


def _mm(a, b, mode, out_dtype, name, mnk, tiles, b_spec=None, o_spec=None, o_shape=None, add=None):
    m, n, k = mnk
    tm, tn, tk = min(tiles[0], m), min(tiles[1], n), min(tiles[2], k)
    nk = k // tk
    a_spec = {"nn": pl.BlockSpec((tm, tk), lambda i, j, q: (i, q)),
              "nt": pl.BlockSpec((tm, tk), lambda i, j, q: (i, q)),
              "tn": pl.BlockSpec((tk, tm), lambda i, j, q: (q, i))}[mode]
    if b_spec is None:
        b_spec = {"nn": pl.BlockSpec((tk, tn), lambda i, j, q: (q, j)),
                  "nt": pl.BlockSpec((tn, tk), lambda i, j, q: (j, q)),
                  "tn": pl.BlockSpec((tk, tn), lambda i, j, q: (q, j))}[mode]
    if o_spec is None:
        o_spec, o_shape = pl.BlockSpec((tm, tn), lambda i, j, q: (i, j)), (m, n)
    dims = {"nn": NN, "nt": NT, "tn": TN}[mode]
    has_add = add is not None

    def body(*refs):
        a_ref, b_ref = refs[0], refs[1]
        o_ref = refs[3] if has_add else refs[2]
        p = _dot(a_ref[...], b_ref[...], dims)

        def finish(v):
            if has_add:
                v = v + refs[2][...]
            o_ref[...] = v.astype(o_ref.dtype)

        if nk == 1:
            finish(p)
        else:
            acc_ref = refs[-1]
            q = pl.program_id(2)

            @pl.when(q == 0)
            def _():
                acc_ref[...] = p

            @pl.when(q > 0)
            def _():
                acc_ref[...] += p

            @pl.when(q == nk - 1)
            def _():
                finish(acc_ref[...])

    in_specs = [a_spec, b_spec]
    args = [a, b]
    if has_add:
        in_specs.append(pl.BlockSpec((tm, tn), lambda i, j, q: (i, j)))
        args.append(add)
    return pl.pallas_call(
        body, name=name, grid=(m // tm, n // tn, nk), in_specs=in_specs, out_specs=o_spec,
        out_shape=jax.ShapeDtypeStruct(o_shape, out_dtype),
        scratch_shapes=[pltpu.VMEM((tm, tn), F32)] if nk > 1 else [],
        compiler_params=_cparams("parallel", "parallel", "arbitrary"),
    )(*args)


def _rms_fwd(x, g, name, tq=512):
    s, d = x.shape

    def body(x_ref, g_ref, h_ref):
        xv = x_ref[...]
        r = lax.rsqrt(jnp.mean(xv * xv, axis=-1, keepdims=True) + EPS)
        h_ref[...] = (xv * r * g_ref[...]).astype(h_ref.dtype)

    return pl.pallas_call(
        body, name=name, grid=(s // tq,),
        in_specs=[pl.BlockSpec((tq, d), lambda i: (i, 0)), pl.BlockSpec((1, d), lambda i: (0, 0))],
        out_specs=pl.BlockSpec((tq, d), lambda i: (i, 0)),
        out_shape=jax.ShapeDtypeStruct((s, d), CDT), compiler_params=_cparams("parallel"),
    )(x, g)


def _loss_bwd(y, target, name, tq=512):
    s, d = y.shape

    def body(y_ref, t_ref, dx_ref, dxc_ref, sq_ref):
        err = y_ref[...] - t_ref[...]
        dx = err * (1.0 / d)
        dx_ref[...] = dx
        dxc_ref[...] = dx.astype(dxc_ref.dtype)

        @pl.when(pl.program_id(0) == 0)
        def _():
            sq_ref[...] = jnp.zeros_like(sq_ref)

        sq_ref[...] += jnp.sum(err * err, axis=0, keepdims=True)

    blk = pl.BlockSpec((tq, d), lambda i: (i, 0))
    vec = pl.BlockSpec((1, d), lambda i: (0, 0))
    return pl.pallas_call(
        body, name=name, grid=(s // tq,), in_specs=[blk, blk], out_specs=[blk, blk, vec],
        out_shape=[jax.ShapeDtypeStruct((s, d), F32), jax.ShapeDtypeStruct((s, d), CDT), jax.ShapeDtypeStruct((1, d), F32)],
        compiler_params=_cparams("arbitrary"),
    )(y, target)


def _tril_mask():
    row = lax.broadcasted_iota(jnp.int32, (CHUNK, CHUNK), 0)
    col = lax.broadcasted_iota(jnp.int32, (CHUNK, CHUNK), 1)
    return row >= col


def _a_mid_fwd(proj, v_gain, w_s, b_s_t, name, tq=512):
    s = proj.shape[0]

    def body(p_ref, vg_ref, ws_ref, bs_ref, y_ref):
        vraw = p_ref[:, A_WIDTH:2 * A_WIDTH].astype(F32)
        r = lax.rsqrt(jnp.mean(vraw * vraw, axis=-1, keepdims=True) + EPS)
        vn = (vraw * r * vg_ref[...]).astype(CDT)
        tri = _tril_mask()
        for g in range(A_GROUPS):
            w = jnp.where(tri, ws_ref[g], 0.0).astype(CDT)
            bias = bs_ref[:, g:g + 1]
            cols = slice(g * A_GROUP_DIM, (g + 1) * A_GROUP_DIM)
            zcols = slice(2 * A_WIDTH + g * A_GROUP_DIM, 2 * A_WIDTH + (g + 1) * A_GROUP_DIM)
            for c in range(tq // CHUNK):
                rows = slice(c * CHUNK, (c + 1) * CHUNK)
                mixed = _dot(w, vn[rows, cols]) + bias
                u = p_ref[rows, cols].astype(F32)
                z = p_ref[rows, zcols].astype(F32)
                y_ref[rows, cols] = (u * mixed * (z * _sigmoid(z))).astype(y_ref.dtype)

    return pl.pallas_call(
        body, name=name, grid=(s // tq,),
        in_specs=[pl.BlockSpec((tq, 3 * A_WIDTH), lambda i: (i, 0)), pl.BlockSpec((1, A_WIDTH), lambda i: (0, 0)),
                  pl.BlockSpec((A_GROUPS, CHUNK, CHUNK), lambda i: (0, 0, 0)), pl.BlockSpec((CHUNK, A_GROUPS), lambda i: (0, 0))],
        out_specs=pl.BlockSpec((tq, A_WIDTH), lambda i: (i, 0)),
        out_shape=jax.ShapeDtypeStruct((s, A_WIDTH), CDT), compiler_params=_cparams("parallel"),
    )(proj, v_gain, w_s, b_s_t)


def _a_mid_bwd(proj, dy, v_gain, w_s, b_s_t, name, tq=512):
    s = proj.shape[0]

    def body(p_ref, dy_ref, vg_ref, ws_ref, bs_ref, dp_ref, dws_ref, dbs_ref, dvg_ref, dvn_ref):
        @pl.when(pl.program_id(0) == 0)
        def _():
            dws_ref[...] = jnp.zeros_like(dws_ref)
            dbs_ref[...] = jnp.zeros_like(dbs_ref)
            dvg_ref[...] = jnp.zeros_like(dvg_ref)

        vraw = p_ref[:, A_WIDTH:2 * A_WIDTH].astype(F32)
        r = lax.rsqrt(jnp.mean(vraw * vraw, axis=-1, keepdims=True) + EPS)
        vhat = vraw * r
        vg = vg_ref[...]
        vn = (vhat * vg).astype(CDT)
        tri = _tril_mask()
        lane = lax.broadcasted_iota(jnp.int32, (CHUNK, A_GROUPS), 1)
        dbs = jnp.zeros((CHUNK, A_GROUPS), F32)
        for g in range(A_GROUPS):
            w = jnp.where(tri, ws_ref[g], 0.0).astype(CDT)
            bias = bs_ref[:, g:g + 1]
            cols = slice(g * A_GROUP_DIM, (g + 1) * A_GROUP_DIM)
            zcols = slice(2 * A_WIDTH + g * A_GROUP_DIM, 2 * A_WIDTH + (g + 1) * A_GROUP_DIM)
            dws = jnp.zeros((CHUNK, CHUNK), F32)
            for c in range(tq // CHUNK):
                rows = slice(c * CHUNK, (c + 1) * CHUNK)
                vn_g = vn[rows, cols]
                mixed = _dot(w, vn_g) + bias
                u = p_ref[rows, cols].astype(F32)
                z = p_ref[rows, zcols].astype(F32)
                dyv = dy_ref[rows, cols].astype(F32)
                sg = _sigmoid(z)
                sz = z * sg
                dyu = dyv * u
                dmixed = dyu * sz
                dp_ref[rows, cols] = (dyv * mixed * sz).astype(dp_ref.dtype)
                dp_ref[rows, zcols] = (dyu * mixed * (sg * (1.0 + z * (1.0 - sg)))).astype(dp_ref.dtype)
                dmc = dmixed.astype(CDT)
                dws = dws + _dot(dmc, vn_g, NT)
                dbs = dbs + jnp.where(lane == g, jnp.sum(dmixed, axis=-1, keepdims=True), 0.0)
                dvn_ref[rows, cols] = _dot(w, dmc, TN)
            dws_ref[g] += jnp.where(tri, dws, 0.0)
        dbs_ref[...] += dbs
        dvn = dvn_ref[...]
        gd = dvn * vg
        dvraw = r * gd - vraw * (r * r * r) * jnp.mean(gd * vraw, axis=-1, keepdims=True)
        dp_ref[:, A_WIDTH:2 * A_WIDTH] = dvraw.astype(dp_ref.dtype)
        dvg_ref[...] += jnp.sum(dvn * vhat, axis=0, keepdims=True)

    return pl.pallas_call(
        body, name=name, grid=(s // tq,),
        in_specs=[pl.BlockSpec((tq, 3 * A_WIDTH), lambda i: (i, 0)), pl.BlockSpec((tq, A_WIDTH), lambda i: (i, 0)),
                  pl.BlockSpec((1, A_WIDTH), lambda i: (0, 0)), pl.BlockSpec((A_GROUPS, CHUNK, CHUNK), lambda i: (0, 0, 0)),
                  pl.BlockSpec((CHUNK, A_GROUPS), lambda i: (0, 0))],
        out_specs=[pl.BlockSpec((tq, 3 * A_WIDTH), lambda i: (i, 0)), pl.BlockSpec((A_GROUPS, CHUNK, CHUNK), lambda i: (0, 0, 0)),
                   pl.BlockSpec((CHUNK, A_GROUPS), lambda i: (0, 0)), pl.BlockSpec((1, A_WIDTH), lambda i: (0, 0))],
        out_shape=[jax.ShapeDtypeStruct((s, 3 * A_WIDTH), CDT), jax.ShapeDtypeStruct((A_GROUPS, CHUNK, CHUNK), F32),
                   jax.ShapeDtypeStruct((CHUNK, A_GROUPS), F32), jax.ShapeDtypeStruct((1, A_WIDTH), F32)],
        scratch_shapes=[pltpu.VMEM((tq, A_WIDTH), F32)],
        compiler_params=_cparams("arbitrary"),
    )(proj, dy, v_gain, w_s, b_s_t)


def _rope_tables(s):
    inv_freq = jnp.power(jnp.float32(ROPE_THETA), -jnp.arange(ROPE_HALF, dtype=F32) / ROPE_HALF)
    ang = jnp.arange(s, dtype=F32)[:, None] * inv_freq[None, :]
    cos, sin = jnp.cos(ang), jnp.sin(ang)
    rest = HEAD_DIM - 2 * ROPE_HALF
    t_c = jnp.concatenate([cos, cos, jnp.ones((s, rest), F32)], axis=1)
    t_a = jnp.concatenate([-sin, jnp.zeros((s, HEAD_DIM - ROPE_HALF), F32)], axis=1)
    t_b = jnp.concatenate([jnp.zeros((s, ROPE_HALF), F32), sin, jnp.zeros((s, rest), F32)], axis=1)
    return t_c, t_a, t_b


def _b_qk_fwd(proj, gains, tabs, name, tq=256):
    s = proj.shape[0]

    def body(p_ref, g_ref, tc_ref, ta_ref, tb_ref, o_ref):
        tc, ta, tb = tc_ref[...], ta_ref[...], tb_ref[...]
        for tg in range(6):
            gain = g_ref[tg:tg + 1, :]
            for h in range(B_HEADS):
                cols = slice(tg * 1024 + h * HEAD_DIM, tg * 1024 + (h + 1) * HEAD_DIM)
                xv = p_ref[:, cols].astype(F32)
                r = lax.rsqrt(_lane_sums(xv * xv) * (1.0 / HEAD_DIM) + EPS)
                xn = xv * r * gain
                y = xn * tc + pltpu.roll(xn, HEAD_DIM - ROPE_HALF, 1) * ta + pltpu.roll(xn, ROPE_HALF, 1) * tb
                o_ref[:, cols] = y.astype(o_ref.dtype)

    tab = pl.BlockSpec((tq, HEAD_DIM), lambda i: (i, 0))
    return pl.pallas_call(
        body, name=name, grid=(s // tq,),
        in_specs=[pl.BlockSpec((tq, B_QK), lambda i: (i, 0)), pl.BlockSpec((6, HEAD_DIM), lambda i: (0, 0)), tab, tab, tab],
        out_specs=pl.BlockSpec((tq, B_QK), lambda i: (i, 0)),
        out_shape=jax.ShapeDtypeStruct((s, B_QK), CDT), compiler_params=_cparams("parallel"),
    )(proj, gains, *tabs)


PERMUTE_BLOCK_BYTES = 4 * 1024 * 1024


def _view_rows(length, dil, width, itemsize):
    rows = 16
    while 2 * rows * dil * width * itemsize <= PERMUTE_BLOCK_BYTES and 2 * rows <= length:
        rows *= 2
    return rows


def _to_view(x, col, width, dil, name):
    s = x.shape[0]
    length = s // dil
    tl = _view_rows(length, dil, width, 4)
    lanes = HEAD_DIM
    nblk = width // lanes

    def body(x_ref, o_ref, slab_ref):
        for b in range(nblk):
            slab_ref[b] = x_ref[:, b * lanes:(b + 1) * lanes].astype(F32)
        for r in range(dil):
            for b in range(nblk):
                o_ref[:, r * width + b * lanes:r * width + (b + 1) * lanes] = (
                    slab_ref.at[b][pl.ds(r, tl, stride=dil), :].astype(o_ref.dtype))

    return pl.pallas_call(
        body, name=name, grid=(length // tl,),
        in_specs=[pl.BlockSpec((tl * dil, width), lambda i: (i, col))],
        out_specs=pl.BlockSpec((tl, dil * width), lambda i: (i, 0)),
        out_shape=jax.ShapeDtypeStruct((length, dil * width), x.dtype),
        scratch_shapes=[pltpu.VMEM((nblk, tl * dil, lanes), F32)],
        compiler_params=_cparams("parallel"),
    )(x)


def _from_view(v, dil, name):
    length, width = v.shape[0], v.shape[1] // dil
    tl = _view_rows(length, dil, width, 4)
    lanes = HEAD_DIM
    nblk = width // lanes

    def body(v_ref, o_ref, slab_ref):
        for r in range(dil):
            for b in range(nblk):
                slab_ref.at[b][pl.ds(r, tl, stride=dil), :] = v_ref[:, r * width + b * lanes:r * width + (b + 1) * lanes].astype(F32)
        for b in range(nblk):
            o_ref[:, b * lanes:(b + 1) * lanes] = slab_ref[b].astype(o_ref.dtype)

    return pl.pallas_call(
        body, name=name, grid=(length // tl,),
        in_specs=[pl.BlockSpec((tl, dil * width), lambda i: (i, 0))],
        out_specs=pl.BlockSpec((tl * dil, width), lambda i: (i, 0)),
        out_shape=jax.ShapeDtypeStruct((length * dil, width), v.dtype),
        scratch_shapes=[pltpu.VMEM((nblk, tl * dil, lanes), F32)],
        compiler_params=_cparams("parallel"),
    )(v)


def _b_attn_fwd(q, k, v, bases, dil, name):
    length = q.shape[0]
    nb = length // CHUNK
    per = 4 if nb % 4 == 0 else 2 if nb % 2 == 0 else 1
    scale = 1.0 / math.sqrt(HEAD_DIM)
    w = B_HEADS * HEAD_DIM
    qb, kb, vb = bases

    def body(q_ref, kc_ref, kp_ref, vc_ref, vp_ref, o_ref, lse_ref):
        n = pl.program_id(1)
        qi = lax.broadcasted_iota(jnp.int32, (CHUNK, 2 * CHUNK), 0)
        ki = lax.broadcasted_iota(jnp.int32, (CHUNK, 2 * CHUNK), 1)
        first_key = jnp.where(n > 0, 0, CHUNK)
        band = (ki >= qi) & (ki <= qi + CHUNK)
        lane = lax.broadcasted_iota(jnp.int32, (CHUNK, HEAD_DIM), 1)
        for b in range(per):
            rows = slice(b * CHUNK, (b + 1) * CHUNK)
            above = slice((b - 1) * CHUNK, b * CHUNK)
            mask = band & (ki >= first_key) if b == 0 else band
            lse_all = jnp.zeros((CHUNK, HEAD_DIM), F32)
            for h in range(B_HEADS):
                sl = slice(h * HEAD_DIM, (h + 1) * HEAD_DIM)
                k_prev = kp_ref[:, sl] if b == 0 else kc_ref[above, sl]
                v_prev = vp_ref[:, sl] if b == 0 else vc_ref[above, sl]
                k2 = jnp.concatenate([k_prev, kc_ref[rows, sl]], axis=0)
                v2 = jnp.concatenate([v_prev, vc_ref[rows, sl]], axis=0)
                sc = jnp.where(mask, _dot(q_ref[rows, sl], k2, NT) * scale, -1e30)
                m = jnp.max(sc, axis=-1, keepdims=True)
                p = jnp.exp(sc - m)
                l = jnp.sum(p, axis=-1, keepdims=True)
                o_ref[rows, sl] = (_dot(p.astype(CDT), v2) / l).astype(o_ref.dtype)
                lse_all = jnp.where(lane == h, m + jnp.log(l), lse_all)
            lse_ref[rows, :] = lse_all

    prev = lambda n: jnp.maximum(per * n - 1, 0)
    blk = lambda f: pl.BlockSpec((per * CHUNK, w), f)
    halo = lambda f: pl.BlockSpec((CHUNK, w), f)
    return pl.pallas_call(
        body, name=name, grid=(dil, nb // per),
        in_specs=[blk(lambda r, n: (n, qb + r)), blk(lambda r, n: (n, kb + r)), halo(lambda r, n: (prev(n), kb + r)),
                  blk(lambda r, n: (n, vb + r)), halo(lambda r, n: (prev(n), vb + r))],
        out_specs=[blk(lambda r, n: (n, r)), pl.BlockSpec((per * CHUNK, HEAD_DIM), lambda r, n: (n, r))],
        out_shape=[jax.ShapeDtypeStruct((length, dil * w), CDT), jax.ShapeDtypeStruct((length, dil * HEAD_DIM), F32)],
        compiler_params=_cparams("parallel", "parallel"),
    )(q, k, k, v, v)


def _b_combine(os_, lses, proj, name, tq=512):
    s = proj.shape[0]
    w = B_HEADS * HEAD_DIM

    def body(o0_ref, o1_ref, o2_ref, l0_ref, l1_ref, l2_ref, z_ref, y_ref, oj_ref, lj_ref):
        l0, l1, l2 = l0_ref[...], l1_ref[...], l2_ref[...]
        m = jnp.maximum(jnp.maximum(l0, l1), l2)
        lj = m + jnp.log(jnp.exp(l0 - m) + jnp.exp(l1 - m) + jnp.exp(l2 - m))
        lj_ref[...] = lj
        w0, w1, w2 = jnp.exp(l0 - lj), jnp.exp(l1 - lj), jnp.exp(l2 - lj)
        for h in range(B_HEADS):
            sl = slice(h * HEAD_DIM, (h + 1) * HEAD_DIM)
            o = (w0[:, h:h + 1] * o0_ref[:, sl].astype(F32) + w1[:, h:h + 1] * o1_ref[:, sl].astype(F32)
                 + w2[:, h:h + 1] * o2_ref[:, sl].astype(F32))
            z = z_ref[:, sl].astype(F32)
            oj_ref[:, sl] = o.astype(oj_ref.dtype)
            y_ref[:, sl] = (o * (z * _sigmoid(z))).astype(y_ref.dtype)

    blk = pl.BlockSpec((tq, w), lambda i: (i, 0))
    st = pl.BlockSpec((tq, HEAD_DIM), lambda i: (i, 0))
    return pl.pallas_call(
        body, name=name, grid=(s // tq,),
        in_specs=[blk, blk, blk, st, st, st, pl.BlockSpec((tq, w), lambda i: (i, 9))],
        out_specs=[blk, blk, st],
        out_shape=[jax.ShapeDtypeStruct((s, w), CDT), jax.ShapeDtypeStruct((s, w), CDT), jax.ShapeDtypeStruct((s, HEAD_DIM), F32)],
        compiler_params=_cparams("parallel"),
    )(*os_, *lses, proj)


def _b_bwd_prep(dy, oj, proj, name, tq=512):
    s = proj.shape[0]
    w = B_HEADS * HEAD_DIM

    def body(dy_ref, oj_ref, z_ref, do_ref, dz_ref, dd_ref):
        lane = lax.broadcasted_iota(jnp.int32, (tq, HEAD_DIM), 1)
        dd = jnp.zeros((tq, HEAD_DIM), F32)
        for h in range(B_HEADS):
            sl = slice(h * HEAD_DIM, (h + 1) * HEAD_DIM)
            z = z_ref[:, sl].astype(F32)
            dyv = dy_ref[:, sl].astype(F32)
            o = oj_ref[:, sl].astype(F32)
            sg = _sigmoid(z)
            do = dyv * (z * sg)
            do_ref[:, sl] = do.astype(do_ref.dtype)
            dz_ref[:, sl] = (dyv * o * (sg * (1.0 + z * (1.0 - sg)))).astype(dz_ref.dtype)
            dd = jnp.where(lane == h, jnp.sum(do * o, axis=-1, keepdims=True), dd)
        dd_ref[...] = dd

    blk = pl.BlockSpec((tq, w), lambda i: (i, 0))
    st = pl.BlockSpec((tq, HEAD_DIM), lambda i: (i, 0))
    return pl.pallas_call(
        body, name=name, grid=(s // tq,),
        in_specs=[blk, blk, pl.BlockSpec((tq, w), lambda i: (i, 9))], out_specs=[blk, blk, st],
        out_shape=[jax.ShapeDtypeStruct((s, w), CDT), jax.ShapeDtypeStruct((s, w), CDT), jax.ShapeDtypeStruct((s, HEAD_DIM), F32)],
        compiler_params=_cparams("parallel"),
    )(dy, oj, proj)


def _b_attn_bwd(q, k, v, bases, do, lj, dd, dil, name):
    length = q.shape[0]
    nb = length // CHUNK
    per = 2 if nb % 2 == 0 else 1
    steps = nb // per
    scale = 1.0 / math.sqrt(HEAD_DIM)
    w = B_HEADS * HEAD_DIM
    qb, kb, vb = bases

    def body(qc_ref, qn_ref, k_ref, v_ref, doc_ref, don_ref, lc_ref, ln_ref, dc_ref, dn_ref, out_ref, carry_ref):
        j = pl.program_id(1)

        @pl.when(j == 0)
        def _():
            carry_ref[...] = jnp.zeros_like(carry_ref)

        qi = lax.broadcasted_iota(jnp.int32, (2 * CHUNK, CHUNK), 0)
        ki = lax.broadcasted_iota(jnp.int32, (2 * CHUNK, CHUNK), 1)
        no_next = jnp.where(j + 1 < steps, 0, 2 * CHUNK)
        for h in range(B_HEADS):
            sl = slice(h * HEAD_DIM, (h + 1) * HEAD_DIM)
            st_ = slice(h, h + 1)
            carried = carry_ref[:, sl]
            for b in range(per):
                rows = slice(b * CHUNK, (b + 1) * CHUNK)
                if b + 1 < per:
                    after = slice((b + 1) * CHUNK, (b + 2) * CHUNK)
                    q_n, do_n, l_n, d_n, gone = qc_ref[after, sl], doc_ref[after, sl], lc_ref[after, st_], dc_ref[after, st_], 0
                else:
                    q_n, do_n, l_n, d_n, gone = qn_ref[:, sl], don_ref[:, sl], ln_ref[:, st_], dn_ref[:, st_], no_next
                mask = ((qi < CHUNK) & (ki <= qi)) | ((qi >= CHUNK) & (ki >= qi - CHUNK + gone))
                q2 = jnp.concatenate([qc_ref[rows, sl], q_n], axis=0)
                do2 = jnp.concatenate([doc_ref[rows, sl], do_n], axis=0)
                lse2 = jnp.concatenate([lc_ref[rows, st_], l_n], axis=0)
                d2 = jnp.concatenate([dc_ref[rows, st_], d_n], axis=0)
                k = k_ref[rows, sl]
                v = v_ref[rows, sl]
                sc = _dot(q2, k, NT) * scale
                p = jnp.where(mask, jnp.exp(sc - lse2), 0.0)
                dp = _dot(do2, v, NT)
                ds = (p * (dp - d2) * scale).astype(CDT)
                dq2 = _dot(ds, k)
                out_ref[rows, sl] = (carried + dq2[:CHUNK]).astype(out_ref.dtype)
                carried = dq2[CHUNK:]
                out_ref[rows, w + h * HEAD_DIM:w + (h + 1) * HEAD_DIM] = _dot(ds, q2, TN).astype(out_ref.dtype)
                out_ref[rows, 2 * w + h * HEAD_DIM:2 * w + (h + 1) * HEAD_DIM] = _dot(p.astype(CDT), do2, TN).astype(out_ref.dtype)
            carry_ref[:, sl] = carried

    nxt = lambda j: jnp.minimum(per * (j + 1), nb - 1)
    blk = lambda f: pl.BlockSpec((per * CHUNK, w), f)
    one = lambda f: pl.BlockSpec((CHUNK, w), f)
    st = lambda f: pl.BlockSpec((per * CHUNK, HEAD_DIM), f)
    st1 = lambda f: pl.BlockSpec((CHUNK, HEAD_DIM), f)
    return pl.pallas_call(
        body, name=name, grid=(dil, steps),
        in_specs=[blk(lambda r, j: (j, qb + r)), one(lambda r, j: (nxt(j), qb + r)),
                  blk(lambda r, j: (j, kb + r)), blk(lambda r, j: (j, vb + r)),
                  blk(lambda r, j: (j, r)), one(lambda r, j: (nxt(j), r)),
                  st(lambda r, j: (j, r)), st1(lambda r, j: (nxt(j), r)), st(lambda r, j: (j, r)), st1(lambda r, j: (nxt(j), r))],
        out_specs=pl.BlockSpec((per * CHUNK, 3 * w), lambda r, j: (j, r)),
        out_shape=jax.ShapeDtypeStruct((length, dil * 3 * w), CDT),
        scratch_shapes=[pltpu.VMEM((CHUNK, w), F32)],
        compiler_params=_cparams("parallel", "arbitrary"),
    )(q, q, k, v, do, do, lj, lj, dd, dd)


def _b_qk_bwd(proj, dqkv, dz, gains, tabs, name, tq=256):
    s = proj.shape[0]
    w = B_HEADS * HEAD_DIM

    def body(p_ref, d0_ref, d1_ref, d2_ref, dz_ref, g_ref, tc_ref, ta_ref, tb_ref, dp_ref, dg_ref):
        @pl.when(pl.program_id(0) == 0)
        def _():
            dg_ref[...] = jnp.zeros_like(dg_ref)

        tc, ta, tb = tc_ref[...], ta_ref[...], tb_ref[...]
        d_refs = (d0_ref, d1_ref, d2_ref)
        for g in range(3):
            for t in range(2):
                tg = t * 3 + g
                gain = g_ref[tg:tg + 1, :]
                dgain = jnp.zeros((1, HEAD_DIM), F32)
                for h in range(B_HEADS):
                    cols = slice(tg * w + h * HEAD_DIM, tg * w + (h + 1) * HEAD_DIM)
                    xv = p_ref[:, cols].astype(F32)
                    r = lax.rsqrt(_lane_sums(xv * xv) * (1.0 / HEAD_DIM) + EPS)
                    dyv = d_refs[g][:, t * w + h * HEAD_DIM:t * w + (h + 1) * HEAD_DIM].astype(F32)
                    dxn = dyv * tc + pltpu.roll(dyv * ta, ROPE_HALF, 1) + pltpu.roll(dyv * tb, HEAD_DIM - ROPE_HALF, 1)
                    gd = dxn * gain
                    dx = r * gd - xv * (r * r * r) * (_lane_sums(gd * xv) * (1.0 / HEAD_DIM))
                    dp_ref[:, cols] = dx.astype(dp_ref.dtype)
                    dgain = dgain + jnp.sum(dxn * xv * r, axis=0, keepdims=True)
                dg_ref[tg:tg + 1, :] += dgain
            dp_ref[:, (6 + g) * w:(7 + g) * w] = d_refs[g][:, 2 * w:3 * w]
        dp_ref[:, 9 * w:10 * w] = dz_ref[...]

    tab = pl.BlockSpec((tq, HEAD_DIM), lambda i: (i, 0))
    dblk = pl.BlockSpec((tq, 3 * w), lambda i: (i, 0))
    return pl.pallas_call(
        body, name=name, grid=(s // tq,),
        in_specs=[pl.BlockSpec((tq, B_QK), lambda i: (i, 0)), dblk, dblk, dblk, pl.BlockSpec((tq, w), lambda i: (i, 0)),
                  pl.BlockSpec((6, HEAD_DIM), lambda i: (0, 0)), tab, tab, tab],
        out_specs=[pl.BlockSpec((tq, B_IN), lambda i: (i, 0)), pl.BlockSpec((6, HEAD_DIM), lambda i: (0, 0))],
        out_shape=[jax.ShapeDtypeStruct((s, B_IN), CDT), jax.ShapeDtypeStruct((6, HEAD_DIM), F32)],
        compiler_params=_cparams("arbitrary"),
    )(proj, *dqkv, dz, gains, *tabs)


def _inv_count(t, window):
    return 1.0 / jnp.minimum(t + 1, window).astype(F32)


POOL_BLOCK = 128


def _band(window, forward):
    r = lax.broadcasted_iota(jnp.int32, (POOL_BLOCK, 2 * POOL_BLOCK), 0)
    j = lax.broadcasted_iota(jnp.int32, (POOL_BLOCK, 2 * POOL_BLOCK), 1)
    if forward:
        return jnp.where((j >= r) & (j < r + window), 1.0, 0.0).astype(CDT)
    return jnp.where((j <= r + POOL_BLOCK) & (j > r + POOL_BLOCK - window), 1.0, 0.0).astype(CDT)


def _split_dot(band, v):
    hi = v.astype(jnp.bfloat16)
    lo = (v - hi.astype(F32)).astype(jnp.bfloat16)
    band = band.astype(jnp.bfloat16)
    return _dot(band, hi) + _dot(band, lo)


def _pool_diff(x_ref, halo_ref, diff_ref, i, tq):
    t = i * tq + lax.broadcasted_iota(jnp.int32, (tq, 1), 0)
    for g, window in enumerate(POOL_SIZES):
        cols = slice(g * C_GROUP, (g + 1) * C_GROUP)
        band = _band(window, forward=False)
        inv = _inv_count(t, window)
        for b in range(tq // POOL_BLOCK):
            rows = slice(b * POOL_BLOCK, (b + 1) * POOL_BLOCK)
            cur = x_ref[rows, cols]
            if b == 0:
                above = jnp.where(i > 0, halo_ref[:, cols], jnp.zeros_like(cur))
            else:
                above = x_ref[(b - 1) * POOL_BLOCK:b * POOL_BLOCK, cols]
            pooled = _dot(band, jnp.concatenate([above, cur], axis=0)) * inv[rows]
            diff_ref[rows, cols] = (pooled - cur.astype(F32)).astype(diff_ref.dtype)


GRP_SHARD = (N_CHIPS, 2, 256, C_GROUP)
GRP_ROWS = C_GROUP // N_CHIPS


def _grp_rows(g):
    return g // 2, slice((g % 2) * GRP_ROWS, (g % 2 + 1) * GRP_ROWS)


def _grp_weight(w_ref, g):
    half, rows = _grp_rows(g)
    return jnp.concatenate([w_ref[k, half, rows, :] for k in range(N_CHIPS)], axis=0)


def _c_mid_fwd(proj, w_grp, scale, name, tq=512):
    s = proj.shape[0]
    per = tq // POOL_BLOCK

    def body(x_ref, halo_ref, z_ref, w_ref, sc_ref, y_ref, diff_ref):
        _pool_diff(x_ref, halo_ref, diff_ref, pl.program_id(0), tq)
        for g in range(len(POOL_SIZES)):
            cols = slice(g * C_GROUP, (g + 1) * C_GROUP)
            z = z_ref[:, cols].astype(F32)
            y_ref[:, cols] = (_dot(diff_ref[:, cols], _grp_weight(w_ref, g)) * sc_ref[:, cols] * (z * _sigmoid(z))).astype(y_ref.dtype)

    return pl.pallas_call(
        body, name=name, grid=(s // tq,),
        in_specs=[pl.BlockSpec((tq, C_WIDTH), lambda i: (i, 0)),
                  pl.BlockSpec((POOL_BLOCK, C_WIDTH), lambda i: (jnp.maximum(i * per - 1, 0), 0)),
                  pl.BlockSpec((tq, C_WIDTH), lambda i: (i, 1)),
                  pl.BlockSpec(GRP_SHARD, lambda i: (0, 0, 0, 0)), pl.BlockSpec((1, C_WIDTH), lambda i: (0, 0))],
        out_specs=pl.BlockSpec((tq, C_WIDTH), lambda i: (i, 0)),
        out_shape=jax.ShapeDtypeStruct((s, C_WIDTH), CDT),
        scratch_shapes=[pltpu.VMEM((tq, C_WIDTH), CDT)], compiler_params=_cparams("parallel"),
    )(proj, proj, proj, w_grp, scale)


def _c_mid_bwd(proj, w_grp, scale, dy, name, tq=512):
    s = proj.shape[0]
    per = tq // POOL_BLOCK

    def body(x_ref, halo_ref, z_ref, w_ref, sc_ref, dy_ref, dd_ref, dz_ref, dw_ref, dsc_ref, diff_ref):
        @pl.when(pl.program_id(0) == 0)
        def _():
            dw_ref[...] = jnp.zeros_like(dw_ref)
            dsc_ref[...] = jnp.zeros_like(dsc_ref)

        _pool_diff(x_ref, halo_ref, diff_ref, pl.program_id(0), tq)
        for g in range(len(POOL_SIZES)):
            cols = slice(g * C_GROUP, (g + 1) * C_GROUP)
            d = diff_ref[:, cols]
            wg = _grp_weight(w_ref, g)
            half, rows = _grp_rows(g)
            m0 = _dot(d, wg)
            z = z_ref[:, cols].astype(F32)
            dyv = dy_ref[:, cols].astype(F32)
            sc = sc_ref[:, cols]
            sg = _sigmoid(z)
            dmixed = dyv * (z * sg)
            dz_ref[:, cols] = (dyv * m0 * sc * (sg * (1.0 + z * (1.0 - sg)))).astype(dz_ref.dtype)
            dsc_ref[:, cols] += jnp.sum(dmixed * m0, axis=0, keepdims=True)
            dm0 = (dmixed * sc).astype(CDT)
            dwg = _dot(d, dm0, TN)
            for k in range(N_CHIPS):
                dw_ref[k, half, rows, :] += dwg[k * GRP_ROWS:(k + 1) * GRP_ROWS]
            dd_ref[:, cols] = _dot(dm0, wg, NT)

    blk = pl.BlockSpec((tq, C_WIDTH), lambda i: (i, 0))
    wsp = pl.BlockSpec(GRP_SHARD, lambda i: (0, 0, 0, 0))
    vec = pl.BlockSpec((1, C_WIDTH), lambda i: (0, 0))
    return pl.pallas_call(
        body, name=name, grid=(s // tq,),
        in_specs=[blk, pl.BlockSpec((POOL_BLOCK, C_WIDTH), lambda i: (jnp.maximum(i * per - 1, 0), 0)),
                  pl.BlockSpec((tq, C_WIDTH), lambda i: (i, 1)), wsp, vec, blk],
        out_specs=[blk, blk, wsp, vec],
        out_shape=[jax.ShapeDtypeStruct((s, C_WIDTH), F32), jax.ShapeDtypeStruct((s, C_WIDTH), CDT),
                   jax.ShapeDtypeStruct(GRP_SHARD, F32), jax.ShapeDtypeStruct((1, C_WIDTH), F32)],
        scratch_shapes=[pltpu.VMEM((tq, C_WIDTH), CDT)], compiler_params=_cparams("arbitrary"),
    )(proj, proj, proj, w_grp, scale, dy)


def _c_pool_bwd(ddiff, dz, name, tq=512):
    s = ddiff.shape[0]
    per = tq // POOL_BLOCK
    last = s // tq - 1

    def body(d_ref, halo_ref, dz_ref, o_ref):
        i = pl.program_id(0)
        t = i * tq + lax.broadcasted_iota(jnp.int32, (tq, 1), 0)
        for g, window in enumerate(POOL_SIZES):
            cols = slice(g * C_GROUP, (g + 1) * C_GROUP)
            band = _band(window, forward=True)
            inv = _inv_count(t, window)
            for b in range(tq // POOL_BLOCK):
                rows = slice(b * POOL_BLOCK, (b + 1) * POOL_BLOCK)
                cur = d_ref[rows, cols]
                if b == tq // POOL_BLOCK - 1:
                    below = jnp.where(i < last, halo_ref[:, cols] * (1.0 / window), 0.0)
                else:
                    nxt = slice((b + 1) * POOL_BLOCK, (b + 2) * POOL_BLOCK)
                    below = d_ref[nxt, cols] * inv[nxt]
                summed = _split_dot(band, jnp.concatenate([cur * inv[rows], below], axis=0))
                o_ref[rows, cols] = (summed - cur).astype(o_ref.dtype)
        o_ref[:, C_WIDTH:] = dz_ref[...]

    return pl.pallas_call(
        body, name=name, grid=(s // tq,),
        in_specs=[pl.BlockSpec((tq, C_WIDTH), lambda i: (i, 0)),
                  pl.BlockSpec((POOL_BLOCK, C_WIDTH), lambda i: (jnp.minimum((i + 1) * per, s // POOL_BLOCK - 1), 0)),
                  pl.BlockSpec((tq, C_WIDTH), lambda i: (i, 0))],
        out_specs=pl.BlockSpec((tq, 2 * C_WIDTH), lambda i: (i, 0)),
        out_shape=jax.ShapeDtypeStruct((s, 2 * C_WIDTH), CDT),
        compiler_params=_cparams("parallel"),
    )(ddiff, ddiff, dz)


LAYERS = (("a", 0), ("b", 0), ("c", 0), ("a", 1))


def _shard_spec(block, layer, where):
    if layer is None:
        return pl.BlockSpec((None,) + block, where)
    return pl.BlockSpec((None, None) + block, lambda i, j, q: (where(i, j, q)[0], layer) + where(i, j, q)[1:])


def _w_in_fwd(h, gw, layer, name):
    s, width = h.shape[0], gw.shape[-1]
    rows = 2048 if width <= 1024 else 1024
    return _mm(h, gw, "nn", CDT, name, (s, N_CHIPS * width, D_MODEL), (rows, width, D_MODEL),
               b_spec=_shard_spec((D_MODEL, width), layer, lambda i, j, q: (j, 0, 0)))


def _w_in_dh(dproj, gw, x, dxo, gain, name):
    s, width = dproj.shape[0], gw.shape[-1]
    tm = 1024 if width <= 1024 else 512

    def body(a_ref, b_ref, x_ref, dxo_ref, g_ref, dx_ref, dxc_ref, dg_ref, acc_ref):
        i, q = pl.program_id(0), pl.program_id(1)
        p = _dot(a_ref[...], b_ref[...], NT)

        @pl.when(q == 0)
        def _():
            acc_ref[...] = p

        @pl.when(q > 0)
        def _():
            acc_ref[...] += p

        @pl.when((i == 0) & (q == 0))
        def _():
            dg_ref[...] = jnp.zeros_like(dg_ref)

        @pl.when(q == N_CHIPS - 1)
        def _():
            dh = acc_ref[...]
            xv = x_ref[...]
            r = lax.rsqrt(jnp.mean(xv * xv, axis=-1, keepdims=True) + EPS)
            gd = dh * g_ref[...]
            dx = dxo_ref[...] + r * gd - xv * (r * r * r) * jnp.mean(gd * xv, axis=-1, keepdims=True)
            dx_ref[...] = dx
            dxc_ref[...] = dx.astype(dxc_ref.dtype)
            dg_ref[...] += jnp.sum(dh * xv * r, axis=0, keepdims=True)

    blk = pl.BlockSpec((tm, D_MODEL), lambda i, q: (i, 0))
    vec = pl.BlockSpec((1, D_MODEL), lambda i, q: (0, 0))
    return pl.pallas_call(
        body, name=name, grid=(s // tm, N_CHIPS),
        in_specs=[pl.BlockSpec((tm, width), lambda i, q: (i, q)), pl.BlockSpec((None, D_MODEL, width), lambda i, q: (q, 0, 0)),
                  blk, blk, vec],
        out_specs=[blk, blk, vec],
        out_shape=[jax.ShapeDtypeStruct((s, D_MODEL), F32), jax.ShapeDtypeStruct((s, D_MODEL), CDT), jax.ShapeDtypeStruct((1, D_MODEL), F32)],
        scratch_shapes=[pltpu.VMEM((tm, D_MODEL), F32)], compiler_params=_cparams("arbitrary", "arbitrary"),
    )(dproj, gw, x, dxo, gain)


def _w_in_grad(h, dproj, width, name):
    s = h.shape[0]
    tokens = 2048 if width <= 1536 else 1024
    return _mm(h, dproj, "tn", CDT, name, (D_MODEL, N_CHIPS * width, s), (D_MODEL, width, tokens),
               o_spec=pl.BlockSpec((None, D_MODEL, width), lambda i, j, q: (j, 0, 0)), o_shape=(N_CHIPS, D_MODEL, width))


def _w_out_spec(gw, layer):
    rows = gw.shape[-2]
    if layer is None:
        return pl.BlockSpec((N_CHIPS, rows, D_MODEL), lambda i: (0, 0, 0))
    return pl.BlockSpec((N_CHIPS, None, rows, D_MODEL), lambda i: (0, layer, 0, 0))


def _w_out_fwd(y, gw, layer, x, name, tm=1024):
    s, k = y.shape

    def body(y_ref, w_ref, x_ref, o_ref):
        o_ref[...] = x_ref[...] + _dot(y_ref[...], w_ref[...].reshape(k, D_MODEL))

    blk = pl.BlockSpec((tm, D_MODEL), lambda i: (i, 0))
    return pl.pallas_call(
        body, name=name, grid=(s // tm,), in_specs=[pl.BlockSpec((tm, k), lambda i: (i, 0)), _w_out_spec(gw, layer), blk],
        out_specs=blk, out_shape=jax.ShapeDtypeStruct((s, D_MODEL), F32), compiler_params=_cparams("parallel"),
    )(y, gw, x)


def _w_out_dy(dxc, gw, layer, name, tm=1024):
    s, k = dxc.shape[0], N_CHIPS * gw.shape[-2]
    tm = min(tm, s)

    def body(dx_ref, w_ref, o_ref):
        o_ref[...] = _dot(dx_ref[...], w_ref[...].reshape(k, D_MODEL), NT).astype(o_ref.dtype)

    return pl.pallas_call(
        body, name=name, grid=(s // tm,), in_specs=[pl.BlockSpec((tm, D_MODEL), lambda i: (i, 0)), _w_out_spec(gw, layer)],
        out_specs=pl.BlockSpec((tm, k), lambda i: (i, 0)), out_shape=jax.ShapeDtypeStruct((s, k), CDT),
        compiler_params=_cparams("parallel"),
    )(dxc, gw)


def _w_out_grad(y, dxc, rows, name):
    s, k = y.shape
    tokens = min(2048 if k <= 1024 else 1024, s)
    steps = s // tokens

    def body(y_ref, dx_ref, o_ref, acc_ref):
        i = pl.program_id(0)
        p = _dot(y_ref[...], dx_ref[...], TN)

        @pl.when(i == 0)
        def _():
            acc_ref[...] = p

        @pl.when(i > 0)
        def _():
            acc_ref[...] += p

        @pl.when(i == steps - 1)
        def _():
            o_ref[...] = acc_ref[...].reshape(N_CHIPS, rows, D_MODEL).astype(o_ref.dtype)

    return pl.pallas_call(
        body, name=name, grid=(steps,),
        in_specs=[pl.BlockSpec((tokens, k), lambda i: (i, 0)), pl.BlockSpec((tokens, D_MODEL), lambda i: (i, 0))],
        out_specs=pl.BlockSpec((N_CHIPS, rows, D_MODEL), lambda i: (0, 0, 0)),
        out_shape=jax.ShapeDtypeStruct((N_CHIPS, rows, D_MODEL), CDT),
        scratch_shapes=[pltpu.VMEM((k, D_MODEL), F32)], compiler_params=_cparams("arbitrary"),
    )(y, dxc)


def _local_step(x, target, w, on_grads=None, layer_weights=None):
    s = x.shape[0]
    tabs = _rope_tables(s)
    qk_gains = jnp.concatenate([w["b_q_gain"][0], w["b_k_gain"][0]], axis=0)
    saved = []
    for li, (kind, j) in enumerate(LAYERS):
        if layer_weights is not None:
            w = {**w, **layer_weights(li, x)}
        h = _rms_fwd(x, w["norm_gain"][li:li + 1], f"rms_fwd{li}")
        if kind == "a":
            proj = _w_in_fwd(h, w[f"a_w_in{j}"], None, f"a_in{li}")
            bs_t = w["a_b_s"][j].T
            y = _a_mid_fwd(proj, w["a_v_gain"][j:j + 1], w["a_w_s"][j], bs_t, f"a_mid_fwd{li}")
            x_next = _w_out_fwd(y, w[f"a_w_out{j}"], None, x, f"a_out{li}")
            saved.append((x, h, proj, y))
        elif kind == "b":
            proj = _w_in_fwd(h, w["b_w_in"], None, f"b_in{li}")
            qk = _b_qk_fwd(proj, qk_gains, tabs, f"b_qk_fwd{li}")
            qkv, os_, lses = [], [], []
            for g, dil in enumerate(B_DILATIONS):
                if dil == 1:
                    ops = (qk, qk, proj, (g, 3 + g, 6 + g))
                else:
                    ops = (_to_view(qk, g, 1024, dil, f"b_q_view{li}_{g}"), _to_view(qk, 3 + g, 1024, dil, f"b_k_view{li}_{g}"),
                           _to_view(proj, 6 + g, 1024, dil, f"b_v_view{li}_{g}"), (0, 0, 0))
                o, lse = _b_attn_fwd(*ops, dil, f"b_attn_fwd{li}_{g}")
                if dil > 1:
                    o, lse = _from_view(o, dil, f"b_o_nat{li}_{g}"), _from_view(lse, dil, f"b_lse_nat{li}_{g}")
                qkv.append(ops)
                os_.append(o)
                lses.append(lse)
            y, oj, lj = _b_combine(os_, lses, proj, f"b_combine{li}")
            x_next = _w_out_fwd(y, w["b_w_out"], None, x, f"b_out{li}")
            saved.append((x, h, proj, y, qkv, oj, lj))
        else:
            proj = _w_in_fwd(h, w["c_w_in"], None, f"c_in{li}")
            y = _c_mid_fwd(proj, w["c_w_grp"], w["c_scale"][j:j + 1], f"c_mid_fwd{li}")
            x_next = _w_out_fwd(y, w["c_w_out"], None, x, f"c_out{li}")
            saved.append((x, h, proj, y))
        x = x_next

    dx, dxc, sq = _loss_bwd(x, target, "loss_bwd")
    grads = {"norm_gain": [None] * len(LAYERS), "a_w_in": [None, None], "a_v_gain": [None, None], "a_w_s": [None, None],
             "a_b_s": [None, None], "a_w_out": [None, None]}
    for li in reversed(range(len(LAYERS))):
        kind, j = LAYERS[li]
        sv = saved[li]
        xin, h, proj, y = sv[:4]
        if kind == "a":
            grads["a_w_out"][j] = _w_out_grad(y, dxc, 512, f"a_dwout{li}")
            dy = _w_out_dy(dxc, w[f"a_w_out{j}"], None, f"a_dy{li}")
            dproj, dws, dbs_t, dvg = _a_mid_bwd(proj, dy, w["a_v_gain"][j:j + 1], w["a_w_s"][j], w["a_b_s"][j].T, f"a_mid_bwd{li}")
            grads["a_w_s"][j], grads["a_b_s"][j], grads["a_v_gain"][j] = dws, dbs_t.T, dvg[0]
            grads["a_w_in"][j] = _w_in_grad(h, dproj, 1536, f"a_dwin{li}")
            w_in = w[f"a_w_in{j}"]
        elif kind == "b":
            qkv, oj, lj = sv[4:]
            grads["b_w_out"] = _w_out_grad(y, dxc, 256, f"b_dwout{li}")
            dy = _w_out_dy(dxc, w["b_w_out"], None, f"b_dy{li}")
            do, dz, dd = _b_bwd_prep(dy, oj, proj, f"b_bwd_prep{li}")
            dqkv = []
            for g, dil in enumerate(B_DILATIONS):
                stats = (do, lj, dd)
                if dil > 1:
                    stats = (_to_view(do, 0, 1024, dil, f"b_do_view{li}_{g}"), _to_view(lj, 0, HEAD_DIM, dil, f"b_lj_view{li}_{g}"),
                             _to_view(dd, 0, HEAD_DIM, dil, f"b_dd_view{li}_{g}"))
                d = _b_attn_bwd(*qkv[g], *stats, dil, f"b_attn_bwd{li}_{g}")
                dqkv.append(_from_view(d, dil, f"b_dqkv_nat{li}_{g}") if dil > 1 else d)
            dproj, dgains = _b_qk_bwd(proj, dqkv, dz, qk_gains, tabs, f"b_qk_bwd{li}")
            grads["b_q_gain"], grads["b_k_gain"] = dgains[None, :3], dgains[None, 3:]
            grads["b_w_in"] = _w_in_grad(h, dproj, 2560, f"b_dwin{li}")
            w_in = w["b_w_in"]
        else:
            grads["c_w_out"] = _w_out_grad(y, dxc, 512, f"c_dwout{li}")
            dy = _w_out_dy(dxc, w["c_w_out"], None, f"c_dy{li}")
            ddiff, dz, dwg, dsc = _c_mid_bwd(proj, w["c_w_grp"], w["c_scale"][j:j + 1], dy, f"c_mid_bwd{li}")
            grads["c_w_grp"], grads["c_scale"] = dwg, dsc
            dproj = _c_pool_bwd(ddiff, dz, f"c_pool_bwd{li}")
            grads["c_w_in"] = _w_in_grad(h, dproj, 1024, f"c_dwin{li}")
            w_in = w["c_w_in"]
        gain = w["norm_gain"][li:li + 1]
        if on_grads is not None:
            gain = gain + on_grads(li, grads)
        dx, dxc, dng = _w_in_dh(dproj, w_in, xin, dx, gain, f"{kind}_dh{li}")
        grads["norm_gain"][li] = dng[0]
    for name in ("norm_gain", "a_v_gain", "a_w_s", "a_b_s"):
        grads[name] = jnp.stack(grads[name])
    return sq, dx, grads


ROW = 1024
REDUCE_UNITS = (("a_w_in", 0, 512, 1536), ("a_w_in", 1, 512, 1536), ("a_w_out", 0, 256, 1024), ("a_w_out", 1, 256, 1024),
                ("b_w_in", 0, 512, 2560), ("b_w_out", 0, 128, 1024), ("c_w_in", 0, 512, 1024), ("c_w_grp", 0, 256, 512),
                ("c_w_out", 0, 256, 1024))
GAIN_ROWS = 16
GAINS = len(REDUCE_UNITS)
LAYER_UNITS = {0: (0, 2), 1: (4, 5), 2: (6, 7, 8), 3: (1, 3)}
EARLY_UNITS = LAYER_UNITS
FIRST_UNITS = LAYER_UNITS[0] + (GAINS,)
LATER_UNITS = {li: LAYER_UNITS[li] for li in (1, 2, 3)}
SMALL_OF_HALF = ("a_v_gain", "c_scale")
SMALL_PAD = 16
REP = (("norm_gain", (4, 1024)), ("a_w_s", (2, 8, 128, 128)), ("a_b_s", (2, 8, 128)), ("b_q_gain", (1, 3, 128)), ("b_k_gain", (1, 3, 128)))
REP_CORE = 48
REP_PART = 2 * REP_CORE
SMALL_ROWS = SMALL_PAD + REP_CORE
BLOCK_SHAPES = {"a_w_in": (2, 1024, 1536), "a_v_gain": (2, 512), "a_w_out": (2, 512, 1024), "b_w_in": (1, 1024, 2560),
                "b_w_out": (1, 256, 1024), "c_w_in": (1, 1024, 1024), "c_w_grp": (1, 4, 128, 512), "c_scale": (1, 512),
                "c_w_out": (1, 512, 1024)}
WEIGHTS = ("norm_gain", "a_w_in", "a_v_gain", "a_w_s", "a_b_s", "a_w_out", "b_w_in", "b_q_gain", "b_k_gain", "b_w_out",
           "c_w_in", "c_w_grp", "c_scale", "c_w_out")


def _rows(a, rows):
    a = a.reshape(-1)
    return jnp.pad(a, (0, rows * ROW - a.shape[0])).reshape(rows, ROW)


def _small_unit(vecs, rep_part, dtype):
    halves = [jnp.concatenate([_rows(vecs[h].astype(dtype), SMALL_PAD), rep_part[h * REP_CORE:(h + 1) * REP_CORE].astype(dtype)])
              for h in range(2)]
    return jnp.stack(halves)


def _pack_rep(tree):
    return _rows(jnp.concatenate([tree[n].astype(F32).reshape(-1) for n, _ in REP]), N_CHIPS * REP_PART)


def _unpack_rep(slab):
    flat, out, off = slab.reshape(-1), {}, 0
    for n, shape in REP:
        size = math.prod(shape)
        out[n] = flat[off:off + size].reshape(shape)
        off += size
    return out


def _into_slot(x, first, rows, me, name):
    width = x.shape[1]
    tr = min(rows, 256)

    def body(me_ref, x_ref, o_ref):
        o_ref[...] = x_ref[...].astype(o_ref.dtype)

    return pl.pallas_call(
        body, name=name,
        grid_spec=pltpu.PrefetchScalarGridSpec(
            num_scalar_prefetch=1, grid=(rows // tr,), in_specs=[pl.BlockSpec((tr, width), lambda i, me_ref: (first // tr + i, 0))],
            out_specs=pl.BlockSpec((None, tr, width), lambda i, me_ref: (me_ref[0], i, 0))),
        out_shape=jax.ShapeDtypeStruct((N_CHIPS, rows, width), CDT), compiler_params=_cparams("parallel"),
    )(me.reshape(1), x)


def _gather_inputs(wts, me):
    units = []
    for u, (n, layer, r, w) in enumerate(REDUCE_UNITS):
        slot = _into_slot(wts[n].reshape(-1, w), layer * 2 * r, 2 * r, me, f"slot{u}")
        units.append(slot.reshape(N_CHIPS, 2, r, w))
    gains = jnp.concatenate([wts["a_v_gain"].reshape(-1), wts["c_scale"].reshape(-1)])
    gains = _rows(lax.bitcast_convert_type(gains, CDT), 2 * GAIN_ROWS)
    return units + [_into_slot(gains, 0, 2 * GAIN_ROWS, me, "slot_gains").reshape(N_CHIPS, 2, GAIN_ROWS, ROW)]


def _gathered_weights(indices, units):
    out = {}
    for u, arr in zip(indices, units):
        if u == GAINS:
            gains = lax.bitcast_convert_type(arr[:, 0, :3].reshape(N_CHIPS, 1536, 2), F32)
            out["a_v_gain"] = jnp.concatenate([gains[k, :1024].reshape(2, 512) for k in range(N_CHIPS)], axis=1)
            out["c_scale"] = jnp.concatenate([gains[k, 1024:].reshape(1, 512) for k in range(N_CHIPS)], axis=1)
            continue
        n, layer, r, w = REDUCE_UNITS[u]
        if n == "c_w_grp":
            out[n] = arr
        else:
            out[n + str(layer) if n in ("a_w_in", "a_w_out") else n] = arr.reshape(N_CHIPS, 2 * r, w)
    return out


def _place():
    x, y, c = lax.axis_index("x"), lax.axis_index("y"), lax.axis_index("c")
    chips = [(1 - x, y), (x, 1 - y), (1 - x, 1 - y)]
    return x, y, c, 2 * x + y, (x, y, 1 - c), chips


def _remote(src, dst, sems, j, to):
    send_sems, recv_sems = sems
    return pltpu.make_async_remote_copy(src_ref=src, dst_ref=dst, send_sem=send_sems.at[j], recv_sem=recv_sems.at[j],
                                        device_id=to, device_id_type=MESH)


def _comm_call_multi(body, name, out_shapes, n_sems, args, aliases=None):
    return pl.pallas_call(
        body, name=name, in_specs=[ANY] * len(args), out_specs=[ANY] * len(out_shapes), out_shape=out_shapes,
        scratch_shapes=[pltpu.SemaphoreType.DMA((n_sems,)), pltpu.SemaphoreType.DMA((n_sems,))],
        input_output_aliases=aliases or {},
    )(*args)


def _allgather_units(units):
    n = len(units)

    def body(*refs):
        outs, sems = refs[n:2 * n], (refs[2 * n], refs[2 * n + 1])
        x, y, c, me, sibling, chips = _place()
        first, passed = [], []
        for u, o_ref in enumerate(outs):
            for j, chip in enumerate(chips):
                first.append(_remote(o_ref.at[me, c], o_ref.at[me, c], sems, 6 * u + j, (*chip, c)))
                first[-1].start()
        for u, o_ref in enumerate(outs):
            for j, (cx, cy) in enumerate(chips):
                landed = o_ref.at[2 * cx + cy, c]
                _remote(landed, landed, sems, 6 * u + j, sibling).wait_recv()
                passed.append(_remote(landed, landed, sems, 6 * u + 3 + j, sibling))
                passed[-1].start()
        for u, o_ref in enumerate(outs):
            for j, (cx, cy) in enumerate(chips):
                landed = o_ref.at[2 * cx + cy, 1 - c]
                _remote(landed, landed, sems, 6 * u + 3 + j, sibling).wait_recv()
        for cp in first + passed:
            cp.wait_send()

    return _comm_call_multi(body, "allgather_first", [jax.ShapeDtypeStruct(u.shape, u.dtype) for u in units], 6 * n, units,
                            aliases={u: u for u in range(n)})


HBM = pl.BlockSpec(memory_space=pltpu.HBM)
SEM = pl.BlockSpec(memory_space=pltpu.SEMAPHORE)
SIDE_EFFECT = pltpu.SideEffectType.DATAFLOW_SIDE_EFFECTING


def _gather_start(groups, after):
    sizes = [len(g) for g in groups]
    units = [u for g in groups for u in g]
    n = len(units)

    def body(*refs):
        arrs, sems, token = refs[:n], refs[n + 1:n + 1 + 2 * len(groups)], refs[-1]
        x, y, c, me, sibling, chips = _place()
        at = 0
        for gi, size in enumerate(sizes):
            for u in range(size):
                mine = arrs[at + u].at[me, c]
                for j, chip in enumerate(chips):
                    _remote(mine, mine, (sems[2 * gi], sems[2 * gi + 1]), 3 * u + j, (*chip, c)).start()
            at += size
        token[...] = jnp.zeros_like(token)

    sem_shapes = [pltpu.SemaphoreType.DMA((3 * size,)) for size in sizes for _ in range(2)]
    outs = pl.pallas_call(
        body, name="gather_start",
        out_shape=(*sem_shapes, *[pltpu.HBM(u.shape, u.dtype) for u in units], jax.ShapeDtypeStruct((8, 128), F32)),
        in_specs=[HBM] * n + [ANY], out_specs=(*[SEM] * len(sem_shapes), *[HBM] * n, pl.BlockSpec(memory_space=pltpu.VMEM)),
        input_output_aliases={i: len(sem_shapes) + i for i in range(n)},
        compiler_params=pltpu.CompilerParams(has_side_effects=SIDE_EFFECT),
    )(*[pltpu.with_memory_space_constraint(u, pltpu.HBM) for u in units], after)
    sems, arrs = outs[:len(sem_shapes)], outs[len(sem_shapes):-1]
    bounds = [sum(sizes[:gi]) for gi in range(len(sizes) + 1)]
    return ([(sems[2 * gi], sems[2 * gi + 1]) for gi in range(len(sizes))],
            [list(arrs[bounds[gi]:bounds[gi + 1]]) for gi in range(len(sizes))], outs[-1])


def _gather_wait(units, sems, after, name):
    n = len(units)

    def body(*refs):
        arrs, send_sems, recv_sems = refs[:n], refs[n], refs[n + 1]
        x, y, c, me, sibling, chips = _place()
        for u in range(n):
            for j, (cx, cy) in enumerate(chips):
                cp = _remote(arrs[u].at[me, c], arrs[u].at[2 * cx + cy, c], (send_sems, recv_sems), 3 * u + j, (cx, cy, c))
                cp.wait_send()
                cp.wait_recv()

    outs = pl.pallas_call(
        body, name=name, out_shape=tuple(pltpu.HBM(u.shape, u.dtype) for u in units),
        in_specs=[HBM] * n + [SEM, SEM, ANY], out_specs=[HBM] * n, input_output_aliases={i: i for i in range(n)},
        compiler_params=pltpu.CompilerParams(has_side_effects=SIDE_EFFECT),
    )(*units, *sems, after)
    return list(outs)


def _forward_units(units, name):
    n = len(units)

    def body(*refs):
        outs, sems = refs[n:2 * n], (refs[2 * n], refs[2 * n + 1])
        x, y, c, me, sibling, chips = _place()
        passed = []
        for u, o_ref in enumerate(outs):
            for j, (cx, cy) in enumerate(chips):
                landed = o_ref.at[2 * cx + cy, c]
                passed.append(_remote(landed, landed, sems, 3 * u + j, sibling))
                passed[-1].start()
        for u, o_ref in enumerate(outs):
            for j, (cx, cy) in enumerate(chips):
                landed = o_ref.at[2 * cx + cy, 1 - c]
                _remote(landed, landed, sems, 3 * u + j, sibling).wait_recv()
        for cp in passed:
            cp.wait_send()

    return _comm_call_multi(body, name, [jax.ShapeDtypeStruct(u.shape, u.dtype) for u in units], 3 * n, units,
                            aliases={u: u for u in range(n)})


def _swap_halves(units, name):
    n = len(units)

    def body(*refs):
        ins, outs, sems = refs[:n], refs[n:2 * n], (refs[2 * n], refs[2 * n + 1])
        x, y, c, me, sibling, chips = _place()
        sent = [_remote(g_ref.at[:, 1 - c], o_ref, sems, u, sibling) for u, (g_ref, o_ref) in enumerate(zip(ins, outs))]
        for cp in sent:
            cp.start()
        for cp in sent:
            cp.wait()

    shapes = [jax.ShapeDtypeStruct((N_CHIPS,) + u.shape[2:], u.dtype) for u in units]
    return _comm_call_multi(body, name, shapes, n, units)


def _scatter_shards(units):
    n = len(units)

    def body(*refs):
        ins, outs, sems = refs[:n], refs[n:2 * n], (refs[2 * n], refs[2 * n + 1])
        x, y, c, me, sibling, chips = _place()
        sent = []
        for u, (s_ref, o_ref) in enumerate(zip(ins, outs)):
            for j, (cx, cy) in enumerate(chips):
                sent.append(_remote(s_ref.at[2 * cx + cy], o_ref.at[me], sems, 3 * u + j, (cx, cy, c)))
                sent[-1].start()
        for u, o_ref in enumerate(outs):
            for j, (cx, cy) in enumerate(chips):
                slot = o_ref.at[2 * cx + cy]
                _remote(slot, slot, sems, 3 * u + j, sibling).wait_recv()
        for cp in sent:
            cp.wait_send()

    return _comm_call_multi(body, "grad_scatter_shards", [jax.ShapeDtypeStruct(u.shape, u.dtype) for u in units], 3 * n, units)


def _scatter_start(units, name):
    n = len(units)

    def body(*refs):
        srcs, lands = refs[:n], refs[n:2 * n]
        send_sems, recv_sems, token = refs[2 * n], refs[2 * n + 1], refs[-1]
        x, y, c, me, sibling, chips = _place()
        for u in range(n):
            for j, (cx, cy) in enumerate(chips):
                _remote(srcs[u].at[2 * cx + cy], lands[u].at[me], (send_sems, recv_sems), 3 * u + j, (cx, cy, c)).start()
        token[...] = jnp.zeros_like(token)

    hbm = [pltpu.HBM(u.shape, u.dtype) for u in units]
    outs = pl.pallas_call(
        body, name=name,
        out_shape=(pltpu.SemaphoreType.DMA((3 * n,)), pltpu.SemaphoreType.DMA((3 * n,)), *hbm, *hbm, jax.ShapeDtypeStruct((8, 128), F32)),
        in_specs=[HBM] * (2 * n), out_specs=(SEM, SEM, *[HBM] * (2 * n), pl.BlockSpec(memory_space=pltpu.VMEM)),
        input_output_aliases={i: 2 + i for i in range(2 * n)},
        compiler_params=pltpu.CompilerParams(has_side_effects=SIDE_EFFECT),
    )(*[pltpu.with_memory_space_constraint(u, pltpu.HBM) for u in units],
      *[pltpu.with_memory_space_constraint(lax.empty(u.shape, u.dtype), pltpu.HBM) for u in units])
    return outs[0], outs[1], outs[2:2 + n], outs[2 + n:2 + 2 * n], outs[-1]


def _scatter_wait(send_sems, recv_sems, srcs, lands, after, name):
    n = len(srcs)

    def body(*refs):
        srcs_, lands_, send_sems_, recv_sems_ = refs[:n], refs[n:2 * n], refs[2 * n], refs[2 * n + 1]
        x, y, c, me, sibling, chips = _place()
        for u in range(n):
            for j, (cx, cy) in enumerate(chips):
                slot = lands_[u].at[2 * cx + cy]
                cp = _remote(srcs_[u].at[2 * cx + cy], slot, (send_sems_, recv_sems_), 3 * u + j, (cx, cy, c))
                cp.wait_send()
                cp.wait_recv()

    hbm = [pltpu.HBM(u.shape, u.dtype) for u in srcs]
    outs = pl.pallas_call(
        body, name=name, out_shape=(*hbm, *hbm),
        in_specs=[HBM] * (2 * n) + [SEM, SEM, ANY], out_specs=[HBM] * (2 * n),
        input_output_aliases={i: i for i in range(2 * n)},
        compiler_params=pltpu.CompilerParams(has_side_effects=SIDE_EFFECT),
    )(*srcs, *lands, send_sems, recv_sems, after)
    return list(outs[n:])


def _share_halves(units, name):
    n = len(units)

    def body(*refs):
        outs, sems = refs[n:2 * n], (refs[2 * n], refs[2 * n + 1])
        x, y, c, me, sibling, chips = _place()
        sent = [_remote(o_ref.at[:, :, c], o_ref.at[:, :, c], sems, u, sibling) for u, o_ref in enumerate(outs)]
        for cp in sent:
            cp.start()
        for u, o_ref in enumerate(outs):
            theirs = o_ref.at[:, :, 1 - c]
            _remote(theirs, theirs, sems, u, sibling).wait_recv()
        for cp in sent:
            cp.wait_send()

    return _comm_call_multi(body, name, [jax.ShapeDtypeStruct(u.shape, u.dtype) for u in units], n, units,
                            aliases={u: u for u in range(n)})


def _share_start(units, name):
    n = len(units)

    def body(*refs):
        arrs, send_sems, recv_sems, token = refs[:n], refs[n], refs[n + 1], refs[-1]
        x, y, c, me, sibling, chips = _place()
        for u in range(n):
            _remote(arrs[u].at[:, :, c], arrs[u].at[:, :, c], (send_sems, recv_sems), u, sibling).start()
        token[...] = jnp.zeros_like(token)

    hbm = [pltpu.HBM(u.shape, u.dtype) for u in units]
    outs = pl.pallas_call(
        body, name=name,
        out_shape=(pltpu.SemaphoreType.DMA((n,)), pltpu.SemaphoreType.DMA((n,)), *hbm, jax.ShapeDtypeStruct((8, 128), F32)),
        in_specs=[HBM] * n, out_specs=(SEM, SEM, *[HBM] * n, pl.BlockSpec(memory_space=pltpu.VMEM)),
        input_output_aliases={i: 2 + i for i in range(n)},
        compiler_params=pltpu.CompilerParams(has_side_effects=SIDE_EFFECT),
    )(*[pltpu.with_memory_space_constraint(u, pltpu.HBM) for u in units])
    return outs[0], outs[1], list(outs[2:2 + n]), outs[-1]


def _share_wait(send_sems, recv_sems, units, after, name):
    n = len(units)

    def body(*refs):
        arrs, send_sems_, recv_sems_ = refs[:n], refs[n], refs[n + 1]
        x, y, c, me, sibling, chips = _place()
        for u in range(n):
            cp = _remote(arrs[u].at[:, :, c], arrs[u].at[:, :, 1 - c], (send_sems_, recv_sems_), u, sibling)
            cp.wait_send()
            cp.wait_recv()

    outs = pl.pallas_call(
        body, name=name, out_shape=tuple(pltpu.HBM(u.shape, u.dtype) for u in units),
        in_specs=[HBM] * n + [SEM, SEM, ANY], out_specs=[HBM] * n, input_output_aliases={i: i for i in range(n)},
        compiler_params=pltpu.CompilerParams(has_side_effects=SIDE_EFFECT),
    )(*units, send_sems, recv_sems, after)
    return list(outs)


def _rep_start(rep4):
    def body(r_ref, land_ref, send_sems, recv_sems, r_thru, land_thru):
        x, y, c, me, sibling, chips = _place()
        for j, chip in enumerate(chips):
            _remote(r_ref, land_ref.at[me], (send_sems, recv_sems), j, (*chip, c)).start()

    land = jax.ShapeDtypeStruct((N_CHIPS,) + rep4.shape, rep4.dtype)
    return pl.pallas_call(
        body, name="rep_start",
        out_shape=(pltpu.SemaphoreType.DMA((3,)), pltpu.SemaphoreType.DMA((3,)), pltpu.HBM(rep4.shape, rep4.dtype),
                   pltpu.HBM(land.shape, land.dtype)),
        in_specs=[HBM, HBM], out_specs=(SEM, SEM, HBM, HBM), input_output_aliases={0: 2, 1: 3},
        compiler_params=pltpu.CompilerParams(has_side_effects=SIDE_EFFECT),
    )(pltpu.with_memory_space_constraint(rep4, pltpu.HBM),
      pltpu.with_memory_space_constraint(lax.empty(land.shape, land.dtype), pltpu.HBM))


def _rep_wait(send_sems, recv_sems, rep4, land, after):
    def body(r_ref, land_ref, send_sems_, recv_sems_, after_ref, r_dead, land_out):
        x, y, c, me, sibling, chips = _place()
        for j, (cx, cy) in enumerate(chips):
            cp = _remote(r_ref, land_ref.at[2 * cx + cy], (send_sems_, recv_sems_), j, (cx, cy, c))
            cp.wait_send()
            cp.wait_recv()

    return pl.pallas_call(
        body, name="rep_wait", out_shape=(pltpu.HBM(rep4.shape, rep4.dtype), pltpu.HBM(land.shape, land.dtype)),
        in_specs=[HBM, HBM, SEM, SEM, ANY], out_specs=(HBM, HBM), input_output_aliases={0: 0, 1: 1},
        compiler_params=pltpu.CompilerParams(has_side_effects=SIDE_EFFECT),
    )(rep4, land, send_sems, recv_sems, after)[1]


ADAM_TILE = 64


def _add_halves(mine, theirs, place, name):
    rows, width = theirs.shape[1:]
    tr = min(rows, 256)

    def body(place_ref, a_ref, b_ref, o_ref):
        o_ref[...] = (a_ref[...].astype(F32) + b_ref[...].astype(F32)).astype(o_ref.dtype)

    blk = pl.BlockSpec((None, tr, width), lambda k, i, place_ref: (k, i, 0))
    return pl.pallas_call(
        body, name=name,
        grid_spec=pltpu.PrefetchScalarGridSpec(
            num_scalar_prefetch=1, grid=(N_CHIPS, rows // tr),
            in_specs=[pl.BlockSpec((None, None, tr, width), lambda k, i, place_ref: (k, place_ref[1], i, 0)), blk], out_specs=blk),
        out_shape=jax.ShapeDtypeStruct(theirs.shape, theirs.dtype), compiler_params=_cparams("parallel", "parallel"),
    )(place, mine, theirs)


def _adam_update(g, w_ref, m_ref, v_ref, o_ref):
    m2 = ADAM_B1 * m_ref[...] + (1.0 - ADAM_B1) * g
    v2 = ADAM_B2 * v_ref[...] + (1.0 - ADAM_B2) * jnp.square(g)
    m_hat = m2 / (1.0 - ADAM_B1 ** ADAM_STEP)
    v_hat = v2 / (1.0 - ADAM_B2 ** ADAM_STEP)
    o_ref[0] = g
    o_ref[1] = -ADAM_LR * (m_hat / (jnp.sqrt(v_hat) + ADAM_EPS) + ADAM_WD * w_ref[...])
    o_ref[2] = m2
    o_ref[3] = v2


def _adamw_rep(g, w, m, v):
    rows = g.shape[0]

    def body(g_ref, w_ref, m_ref, v_ref, o_ref):
        _adam_update(g_ref[...], w_ref, m_ref, v_ref, o_ref)

    blk = pl.BlockSpec((ADAM_TILE, ROW), lambda i: (i, 0))
    return pl.pallas_call(
        body, name="adamw_rep", grid=(rows // ADAM_TILE,), in_specs=[blk, blk, blk, blk],
        out_specs=pl.BlockSpec((4, ADAM_TILE, ROW), lambda i: (0, i, 0)),
        out_shape=jax.ShapeDtypeStruct((4, rows, ROW), F32), compiler_params=_cparams("parallel"),
    )(g, w, m, v)


def _adamw(place, parts, own, w, m, v, layer, name, into=None):
    rows, width = parts.shape[1:]
    layers = w.shape[0]

    def body(place_ref, p_ref, own_ref, w_ref, m_ref, v_ref, *rest):
        o_ref = rest[-1]
        g = jnp.zeros((ADAM_TILE, width), F32)
        for k in range(N_CHIPS):
            g = g + jnp.where(place_ref[0] == k, own_ref[...], p_ref[k]).astype(F32)
        _adam_update(g, w_ref, m_ref, v_ref, o_ref)

    blk = pl.BlockSpec((None, None, ADAM_TILE, width), lambda i, place_ref: (layer, place_ref[1], i, 0))
    blk4 = pl.BlockSpec((4, ADAM_TILE, width), lambda i, place_ref: (0, i, 0))
    mine = pl.BlockSpec((None, ADAM_TILE, width), lambda i, place_ref: (place_ref[0], i, 0))
    out = pl.BlockSpec((4, None, None, ADAM_TILE, width), lambda i, place_ref: (0, layer, place_ref[1], i, 0))
    extra = [] if into is None else [into]
    return pl.pallas_call(
        body, name=name,
        grid_spec=pltpu.PrefetchScalarGridSpec(num_scalar_prefetch=1, grid=(rows // ADAM_TILE,),
                                               in_specs=[blk4, mine, blk, blk, blk] + [ANY] * len(extra), out_specs=out),
        out_shape=jax.ShapeDtypeStruct((4, layers, 2, rows, width), F32),
        input_output_aliases={} if into is None else {6: 0}, compiler_params=_cparams("parallel"),
    )(place, parts, own, w, m, v, *extra)


def kernel(x, norm_gain, a_w_in, a_v_gain, a_w_s, a_b_s, a_w_out, b_w_in, b_q_gain, b_k_gain, b_w_out, c_w_in, c_w_grp, c_scale, c_w_out, loss_target, m_norm_gain, m_a_w_in, m_a_v_gain, m_a_w_s, m_a_b_s, m_a_w_out, m_b_w_in, m_b_q_gain, m_b_k_gain, m_b_w_out, m_c_w_in, m_c_w_grp, m_c_scale, m_c_w_out, v_norm_gain, v_a_w_in, v_a_v_gain, v_a_w_s, v_a_b_s, v_a_w_out, v_b_w_in, v_b_q_gain, v_b_k_gain, v_b_w_out, v_c_w_in, v_c_w_grp, v_c_scale, v_c_w_out):
    wts = dict(norm_gain=norm_gain, a_w_in=a_w_in, a_v_gain=a_v_gain, a_w_s=a_w_s, a_b_s=a_b_s, a_w_out=a_w_out, b_w_in=b_w_in,
               b_q_gain=b_q_gain, b_k_gain=b_k_gain, b_w_out=b_w_out, c_w_in=c_w_in, c_w_grp=c_w_grp, c_scale=c_scale, c_w_out=c_w_out)
    mom1 = dict(norm_gain=m_norm_gain, a_w_in=m_a_w_in, a_v_gain=m_a_v_gain, a_w_s=m_a_w_s, a_b_s=m_a_b_s, a_w_out=m_a_w_out,
                b_w_in=m_b_w_in, b_q_gain=m_b_q_gain, b_k_gain=m_b_k_gain, b_w_out=m_b_w_out, c_w_in=m_c_w_in, c_w_grp=m_c_w_grp,
                c_scale=m_c_scale, c_w_out=m_c_w_out)
    mom2 = dict(norm_gain=v_norm_gain, a_w_in=v_a_w_in, a_v_gain=v_a_v_gain, a_w_s=v_a_w_s, a_b_s=v_a_b_s, a_w_out=v_a_w_out,
                b_w_in=v_b_w_in, b_q_gain=v_b_q_gain, b_k_gain=v_b_k_gain, b_w_out=v_b_w_out, c_w_in=v_c_w_in, c_w_grp=v_c_w_grp,
                c_scale=v_c_scale, c_w_out=v_c_w_out)
    axes = ("x", "y", "c")
    me = 2 * lax.axis_index("x") + lax.axis_index("y")
    core = lax.axis_index("c")

    place = jnp.stack([me, core]).astype(jnp.int32)

    slots = _gather_inputs(wts, me)
    first = _allgather_units([slots[u] for u in FIRST_UNITS])
    full = _gathered_weights(FIRST_UNITS, first)
    later = sorted(LATER_UNITS)
    gather_sems, gather_units, token = _gather_start([[slots[u] for u in LATER_UNITS[li]] for li in later], first[-1])
    for n, _ in REP:
        full[n] = wts[n]
    full["norm_gain"] = wts["norm_gain"] + token[0, 0]

    def layer_weights(li, x_in):
        if li not in LATER_UNITS:
            return {}
        g = later.index(li)
        landed = _gather_wait(gather_units[g], gather_sems[g], x_in, f"gather_wait_l{li}")
        return _gathered_weights(LATER_UNITS[li], _forward_units(landed, f"gather_forward_l{li}"))

    def unit_grad(u, grads):
        n, layer, r, w = REDUCE_UNITS[u]
        g = grads[n][layer] if n in ("a_w_in", "a_w_out") else grads[n]
        return g.astype(CDT).reshape(N_CHIPS, 2, r, w)

    chip_sums, in_flight = {}, []

    def start_exchange(li, grads):
        if li not in EARLY_UNITS:
            return jnp.zeros((), F32)
        units = EARLY_UNITS[li]
        mine = [unit_grad(u, grads) for u in units]
        theirs = _swap_halves(mine, f"grad_swap_halves_l{li}")
        sums = [_add_halves(g, t, place, f"grad_add_halves{u}") for u, g, t in zip(units, mine, theirs)]
        chip_sums.update(zip(units, sums))
        send_sems, recv_sems, srcs, lands, token = _scatter_start(sums, f"grad_scatter_start_l{li}")
        in_flight.append((li, units, send_sems, recv_sems, srcs, lands))
        return token[0, 0]

    sq, grad_x, grads = _local_step(x[0], loss_target[0], full, start_exchange, layer_weights)
    loss = lax.psum(0.5 * jnp.sum(sq) / D_MODEL, axes)

    rep_g = _pack_rep(grads)
    late = [u for u in range(len(REDUCE_UNITS)) if not any(u in us for us in EARLY_UNITS.values())]
    mine = [unit_grad(u, grads) for u in late]
    mine.append(jnp.stack([
        _small_unit([lax.slice_in_dim(grads[v], 512 * k, 512 * (k + 1), axis=1) for v in SMALL_OF_HALF],
                    rep_g[k * REP_PART:(k + 1) * REP_PART], CDT) for k in range(N_CHIPS)]))
    late.append(len(REDUCE_UNITS))
    theirs = _swap_halves(mine, "grad_swap_halves_late")
    sums = [_add_halves(g, t, place, f"grad_add_halves{u}") for u, g, t in zip(late, mine, theirs)]
    chip_sums.update(zip(late, sums))
    parts = dict(zip(late, _scatter_shards(sums)))
    for li, units, send_sems, recv_sems, srcs, lands in in_flight:
        parts.update(zip(units, _scatter_wait(send_sems, recv_sems, srcs, lands, grad_x, f"grad_scatter_wait_l{li}")))
    order = range(len(REDUCE_UNITS) + 1)
    parts, chip_sums = [parts[u] for u in order], [chip_sums[u] for u in order]

    def state_units(tree):
        units = [tree[n].reshape(BLOCK_SHAPES[n][0], 2, r, w) for n, _, r, w in REDUCE_UNITS]
        rep_part = lax.dynamic_slice_in_dim(_pack_rep(tree), me * REP_PART, REP_PART, axis=0)
        return units + [_small_unit([tree[v] for v in SMALL_OF_HALF], rep_part, F32)[None]]

    names = [n for n, _, _, _ in REDUCE_UNITS] + ["small"]
    layers = [layer for _, layer, _, _ in REDUCE_UNITS] + [0]
    res, state = {}, list(zip(parts, chip_sums, state_units(wts), state_units(mom1), state_units(mom2)))

    def update(units):
        for u in units:
            res[names[u]] = _adamw(place, *state[u], layers[u], f"adamw{u}", into=res.get(names[u]))

    update((len(REDUCE_UNITS),))
    small = _share_halves([res.pop("small")], "share_small")[0][:, 0]
    rep_mine = jnp.concatenate([small[0, 0, SMALL_PAD:], small[0, 1, SMALL_PAD:]], axis=0)
    rep_sems = _rep_start(rep_mine)
    update(LAYER_UNITS[1] + LAYER_UNITS[2])
    early = list(res)
    send_sems, recv_sems, early_arrs, _ = _share_start([res[n] for n in early], "share_start")
    update(LAYER_UNITS[0] + LAYER_UNITS[3])
    rest = [n for n in res if n not in early]
    res.update(zip(rest, _share_halves([res[n] for n in rest], "share_halves")))
    res.update(zip(early, _share_wait(send_sems, recv_sems, early_arrs, res[rest[0]], "share_wait")))
    rep_g = lax.dynamic_update_slice_in_dim(_rep_wait(*rep_sems, res[rest[0]]), rep_mine[None], me, axis=0)
    rep_all = _adamw_rep(rep_g.reshape(N_CHIPS * REP_PART, ROW), _pack_rep(wts), _pack_rep(mom1), _pack_rep(mom2))

    outs = []
    for q in range(4):
        tree = {n: arr[q].reshape(BLOCK_SHAPES[n]) for n, arr in res.items()}
        tree["a_v_gain"] = small[q, 0, 0].reshape(2, 512)
        tree["c_scale"] = small[q, 1, 0, :512].reshape(1, 512)
        tree.update(_unpack_rep(rep_all[q]))
        outs.append(tree)
    return (loss, grad_x[None], *[t[n] for t in outs for n in WEIGHTS])
```
